```python
import math
import jax, jax.numpy as jnp
from jax import lax
import numpy as np

D_MODEL = 1024
BATCH = 8
SEQ = 8192
DEPTH = 2

N_MIXERS = 2
N_Q_HEADS = 8
N_KV_HEADS = 2
HEAD_DIM = D_MODEL // N_Q_HEADS
Q_PER_KV = N_Q_HEADS // N_KV_HEADS
ROT_DIM = HEAD_DIM // 4
ROPE_THETA = 500000.0
WINDOW = 128
BLOCK = 128
D_RNN = 3 * D_MODEL // 2
N_RNN_BLOCKS = 16
RNN_BLOCK_W = D_RNN // N_RNN_BLOCKS
CONV_W = 4
CONV_LEFT = 2
LRU_C = 8.0
D_FF = 4 * D_MODEL
DN_ALPHA = (2.0 * DEPTH) ** 0.25
DN_BETA = (8.0 * DEPTH) ** -0.25
LN_EPS = 1e-5
ADA_INIT = 0.5

N_ATTN_LAYERS = (DEPTH + N_MIXERS - 1) // N_MIXERS
N_RNN_LAYERS = DEPTH // N_MIXERS

kernel_name = "hybrid_swa_rglru_adaln_deepnorm_encoder"


def layer_norm(x, g, b):
    xf = x.astype(jnp.float32)
    mu = jnp.mean(xf, axis=-1, keepdims=True)
    var = jnp.mean(jnp.square(xf - mu), axis=-1, keepdims=True)
    y = (xf - mu) * lax.rsqrt(var + LN_EPS)
    return (y * g.astype(jnp.float32) + b.astype(jnp.float32)).astype(x.dtype)


def ada_modulation(c, w, b):
    mod = jax.nn.silu(c) @ w + b
    shift, scale, gate = jnp.split(mod, 3, axis=-1)
    return shift[:, None, :], scale[:, None, :], gate[:, None, :]


def partial_rope(t, cos, sin):
    half = ROT_DIM // 2
    tr = t[..., :ROT_DIM].astype(jnp.float32)
    t1, t2 = tr[..., :half], tr[..., half:]
    cs, sn = cos[None, :, None, :], sin[None, :, None, :]
    rot = jnp.concatenate([t1 * cs - t2 * sn, t2 * cs + t1 * sn], axis=-1).astype(t.dtype)
    return jnp.concatenate([rot, t[..., ROT_DIM:]], axis=-1)


def windowed_gqa(h, w_in, w_out, sinks):
    B, S, _ = h.shape
    nblk = S // BLOCK
    qkv = h @ w_in
    q, k, v = jnp.split(qkv, [N_Q_HEADS * HEAD_DIM, (N_Q_HEADS + N_KV_HEADS) * HEAD_DIM], axis=-1)
    q = q.reshape(B, S, N_Q_HEADS, HEAD_DIM)
    k = k.reshape(B, S, N_KV_HEADS, HEAD_DIM)
    v = v.reshape(B, S, N_KV_HEADS, HEAD_DIM)
    pos = jnp.arange(S, dtype=jnp.float32)
    inv_freq = ROPE_THETA ** (-jnp.arange(0, ROT_DIM, 2, dtype=jnp.float32) / ROT_DIM)
    ang = pos[:, None] * inv_freq[None, :]
    cos, sin = jnp.cos(ang), jnp.sin(ang)
    q = partial_rope(q, cos, sin)
    k = partial_rope(k, cos, sin)
    qb = q.reshape(B, nblk, BLOCK, N_KV_HEADS, Q_PER_KV, HEAD_DIM)

    def band(t):
        tp = jnp.pad(t, ((0, 0), (BLOCK, BLOCK), (0, 0), (0, 0)))
        tb = tp.reshape(B, nblk + 2, BLOCK, N_KV_HEADS, HEAD_DIM)
        return jnp.concatenate([tb[:, :-2], tb[:, 1:-1], tb[:, 2:]], axis=2)

    kb, vb = band(k), band(v)
    scores = jnp.einsum("bnqhgd,bnjhd->bnhgqj", qb, kb,
                        preferred_element_type=jnp.float32) * (HEAD_DIM ** -0.5)
    blk = jnp.arange(nblk)[:, None, None] * BLOCK
    qpos = blk + jnp.arange(BLOCK)[None, :, None]
    kpos = blk - BLOCK + jnp.arange(3 * BLOCK)[None, None, :]
    valid = (jnp.abs(qpos - kpos) <= WINDOW) & (kpos >= 0) & (kpos < S)
    scores = jnp.where(valid[None, :, None, None], scores, -jnp.inf)
    sink = sinks.astype(jnp.float32).reshape(N_KV_HEADS, Q_PER_KV)[None, None, :, :, None, None]
    m = jnp.maximum(jnp.max(scores, axis=-1, keepdims=True), sink)
    p = jnp.exp(scores - m)
    denom = jnp.sum(p, axis=-1, keepdims=True) + jnp.exp(sink - m)
    probs = (p / denom).astype(v.dtype)
    o = jnp.einsum("bnhgqj,bnjhd->bnqhgd", probs, vb)
    o = o.reshape(B, S, N_Q_HEADS * HEAD_DIM)
    return o @ w_out


def centred_depthwise_conv(x, w, b):
    S = x.shape[1]
    xp = jnp.pad(x, ((0, 0), (CONV_LEFT, CONV_W - 1 - CONV_LEFT), (0, 0)))
    out = xp[:, 0:S] * w[0] + b
    for j in range(1, CONV_W):
        out = out + xp[:, j:j + S] * w[j]
    return out


def rg_lru(x, w_a, b_a, w_x, b_x, lam, reverse):
    B, S, _ = x.shape
    xb = x.reshape(B, S, N_RNN_BLOCKS, RNN_BLOCK_W)
    r = jax.nn.sigmoid((jnp.einsum("bsni,nij->bsnj", xb, w_a).reshape(B, S, D_RNN) + b_a).astype(jnp.float32))
    i = jax.nn.sigmoid((jnp.einsum("bsni,nij->bsnj", xb, w_x).reshape(B, S, D_RNN) + b_x).astype(jnp.float32))
    log_a = -LRU_C * r * jax.nn.softplus(-lam.astype(jnp.float32))
    a = jnp.exp(log_a)
    mult = jnp.sqrt(-jnp.expm1(2.0 * log_a))
    u = mult * (i * x.astype(jnp.float32))

    def combine(left, right):
        a_l, b_l = left
        a_r, b_r = right
        return a_l * a_r, a_r * b_l + b_r

    _, hs = lax.associative_scan(combine, (a, u), axis=1, reverse=reverse)
    return hs


def recurrent_block(h, w_in, conv_w, conv_b, w_a, b_a, w_x, b_x, lam, w_out):
    z = h @ w_in
    xr, gate = jnp.split(z, 2, axis=-1)
    xr = centred_depthwise_conv(xr, conv_w, conv_b)
    y = (rg_lru(xr, w_a[0], b_a[0], w_x[0], b_x[0], lam[0], reverse=False)
         + rg_lru(xr, w_a[1], b_a[1], w_x[1], b_x[1], lam[1], reverse=True))
    y = y.astype(h.dtype) * jax.nn.gelu(gate)
    return y @ w_out


def sq_relu_mlp(h, w1, w2):
    return jnp.square(jax.nn.relu(h @ w1)) @ w2


def _fwd_setup_inputs(seed: int = 0) -> dict:
    key = jax.random.key(seed)
    ks = jax.random.split(key, 24)
    nrm = lambda k, shape, s: jax.random.normal(k, shape, jnp.float32) * s
    u = jax.random.uniform(ks[17], (N_RNN_LAYERS, 2, D_RNN), jnp.float32, 0.9, 0.999)
    s_lam = u ** (1.0 / LRU_C)
    lam = jnp.log(s_lam) - jnp.log1p(-s_lam)
    return {
        "x": nrm(ks[0], (BATCH, SEQ, D_MODEL), 1.0),
        "c": nrm(ks[1], (BATCH, D_MODEL), 1.0),
        "ada_w": nrm(ks[2], (DEPTH, 2, D_MODEL, 3 * D_MODEL), ADA_INIT * D_MODEL ** -0.5),
        "ada_b": nrm(ks[3], (DEPTH, 2, 3 * D_MODEL), 0.01),
        "ln_g": 1.0 + nrm(ks[4], (DEPTH, 2, D_MODEL), 0.02),
        "ln_b": nrm(ks[5], (DEPTH, 2, D_MODEL), 0.02),
        "attn_w_in": nrm(ks[6], (N_ATTN_LAYERS, D_MODEL, (N_Q_HEADS + 2 * N_KV_HEADS) * HEAD_DIM), D_MODEL ** -0.5),
        "attn_w_out": nrm(ks[7], (N_ATTN_LAYERS, N_Q_HEADS * HEAD_DIM, D_MODEL), DN_BETA * (N_Q_HEADS * HEAD_DIM) ** -0.5),
        "attn_sinks": nrm(ks[8], (N_ATTN_LAYERS, N_Q_HEADS), 1.0),
        "rnn_w_in": nrm(ks[9], (N_RNN_LAYERS, D_MODEL, 2 * D_RNN), D_MODEL ** -0.5),
        "rnn_conv_w": nrm(ks[10], (N_RNN_LAYERS, CONV_W, D_RNN), CONV_W ** -0.5),
        "rnn_conv_b": nrm(ks[11], (N_RNN_LAYERS, D_RNN), 0.01),
        "rnn_w_a": nrm(ks[12], (N_RNN_LAYERS, 2, N_RNN_BLOCKS, RNN_BLOCK_W, RNN_BLOCK_W), RNN_BLOCK_W ** -0.5),
        "rnn_b_a": nrm(ks[13], (N_RNN_LAYERS, 2, D_RNN), 0.01),
        "rnn_w_x": nrm(ks[14], (N_RNN_LAYERS, 2, N_RNN_BLOCKS, RNN_BLOCK_W, RNN_BLOCK_W), RNN_BLOCK_W ** -0.5),
        "rnn_b_x": nrm(ks[15], (N_RNN_LAYERS, 2, D_RNN), 0.01),
        "rnn_lam": lam,
        "rnn_w_out": nrm(ks[16], (N_RNN_LAYERS, D_RNN, D_MODEL), DN_BETA * D_RNN ** -0.5),
        "mlp_w1": nrm(ks[18], (DEPTH, D_MODEL, D_FF), D_MODEL ** -0.5),
        "mlp_w2": nrm(ks[19], (DEPTH, D_FF, D_MODEL), DN_BETA * D_FF ** -0.5),
    }


def _fwd_reference(x, c, ada_w, ada_b, ln_g, ln_b, attn_w_in, attn_w_out, attn_sinks,
              rnn_w_in, rnn_conv_w, rnn_conv_b, rnn_w_a, rnn_b_a, rnn_w_x, rnn_b_x, rnn_lam,
              rnn_w_out, mlp_w1, mlp_w2):
    for i in range(DEPTH):
        j = i // N_MIXERS
        shift, scale, gate = ada_modulation(c, ada_w[i, 0], ada_b[i, 0])
        h = x * (1.0 + scale) + shift
        if i % N_MIXERS == 0:
            y = windowed_gqa(h, attn_w_in[j], attn_w_out[j], attn_sinks[j])
        else:
            y = recurrent_block(h, rnn_w_in[j], rnn_conv_w[j], rnn_conv_b[j], rnn_w_a[j], rnn_b_a[j],
                                rnn_w_x[j], rnn_b_x[j], rnn_lam[j], rnn_w_out[j])
        x = layer_norm(DN_ALPHA * x + (1.0 + gate) * y, ln_g[i, 0], ln_b[i, 0])
        shift, scale, gate = ada_modulation(c, ada_w[i, 1], ada_b[i, 1])
        y = sq_relu_mlp(x * (1.0 + scale) + shift, mlp_w1[i], mlp_w2[i])
        x = layer_norm(DN_ALPHA * x + (1.0 + gate) * y, ln_g[i, 1], ln_b[i, 1])
    return x


import jax as _jax
import jax.numpy as _jnp

TWIN_FORMAT = 'train_step'
FWD_PARAMS = ['x', 'c', 'ada_w', 'ada_b', 'ln_g', 'ln_b', 'attn_w_in', 'attn_w_out', 'attn_sinks', 'rnn_w_in', 'rnn_conv_w', 'rnn_conv_b', 'rnn_w_a', 'rnn_b_a', 'rnn_w_x', 'rnn_b_x', 'rnn_lam', 'rnn_w_out', 'mlp_w1', 'mlp_w2']
TWIN_WEIGHTS = ['ada_w', 'ada_b', 'ln_g', 'ln_b', 'attn_w_in', 'attn_w_out', 'attn_sinks', 'rnn_w_in', 'rnn_conv_w', 'rnn_conv_b', 'rnn_w_a', 'rnn_b_a', 'rnn_w_x', 'rnn_b_x', 'rnn_lam', 'rnn_w_out', 'mlp_w1', 'mlp_w2']
TWIN_DIFF_INPUT = 'x'
TWIN_INPUTS = ['x', 'c', 'ada_w', 'ada_b', 'ln_g', 'ln_b', 'attn_w_in', 'attn_w_out', 'attn_sinks', 'rnn_w_in', 'rnn_conv_w', 'rnn_conv_b', 'rnn_w_a', 'rnn_b_a', 'rnn_w_x', 'rnn_b_x', 'rnn_lam', 'rnn_w_out', 'mlp_w1', 'mlp_w2', 'loss_target', 'm_ada_w', 'm_ada_b', 'm_ln_g', 'm_ln_b', 'm_attn_w_in', 'm_attn_w_out', 'm_attn_sinks', 'm_rnn_w_in', 'm_rnn_conv_w', 'm_rnn_conv_b', 'm_rnn_w_a', 'm_rnn_b_a', 'm_rnn_w_x', 'm_rnn_b_x', 'm_rnn_lam', 'm_rnn_w_out', 'm_mlp_w1', 'm_mlp_w2', 'v_ada_w', 'v_ada_b', 'v_ln_g', 'v_ln_b', 'v_attn_w_in', 'v_attn_w_out', 'v_attn_sinks', 'v_rnn_w_in', 'v_rnn_conv_w', 'v_rnn_conv_b', 'v_rnn_w_a', 'v_rnn_b_a', 'v_rnn_w_x', 'v_rnn_b_x', 'v_rnn_lam', 'v_rnn_w_out', 'v_mlp_w1', 'v_mlp_w2']
TWIN_OUTPUTS = ['loss', 'grad_x', 'grad_ada_w', 'grad_ada_b', 'grad_ln_g', 'grad_ln_b', 'grad_attn_w_in', 'grad_attn_w_out', 'grad_attn_sinks', 'grad_rnn_w_in', 'grad_rnn_conv_w', 'grad_rnn_conv_b', 'grad_rnn_w_a', 'grad_rnn_b_a', 'grad_rnn_w_x', 'grad_rnn_b_x', 'grad_rnn_lam', 'grad_rnn_w_out', 'grad_mlp_w1', 'grad_mlp_w2', 'delta_ada_w', 'delta_ada_b', 'delta_ln_g', 'delta_ln_b', 'delta_attn_w_in', 'delta_attn_w_out', 'delta_attn_sinks', 'delta_rnn_w_in', 'delta_rnn_conv_w', 'delta_rnn_conv_b', 'delta_rnn_w_a', 'delta_rnn_b_a', 'delta_rnn_w_x', 'delta_rnn_b_x', 'delta_rnn_lam', 'delta_rnn_w_out', 'delta_mlp_w1', 'delta_mlp_w2', 'new_m_ada_w', 'new_m_ada_b', 'new_m_ln_g', 'new_m_ln_b', 'new_m_attn_w_in', 'new_m_attn_w_out', 'new_m_attn_sinks', 'new_m_rnn_w_in', 'new_m_rnn_conv_w', 'new_m_rnn_conv_b', 'new_m_rnn_w_a', 'new_m_rnn_b_a', 'new_m_rnn_w_x', 'new_m_rnn_b_x', 'new_m_rnn_lam', 'new_m_rnn_w_out', 'new_m_mlp_w1', 'new_m_mlp_w2', 'new_v_ada_w', 'new_v_ada_b', 'new_v_ln_g', 'new_v_ln_b', 'new_v_attn_w_in', 'new_v_attn_w_out', 'new_v_attn_sinks', 'new_v_rnn_w_in', 'new_v_rnn_conv_w', 'new_v_rnn_conv_b', 'new_v_rnn_w_a', 'new_v_rnn_b_a', 'new_v_rnn_w_x', 'new_v_rnn_b_x', 'new_v_rnn_lam', 'new_v_rnn_w_out', 'new_v_mlp_w1', 'new_v_mlp_w2']
TWIN_LEAF_KINDS = {'loss': 'loss', 'grad_x': 'grad_x', 'grad_ada_w': 'grad_w', 'grad_ada_b': 'grad_w', 'grad_ln_g': 'grad_w', 'grad_ln_b': 'grad_w', 'grad_attn_w_in': 'grad_w', 'grad_attn_w_out': 'grad_w', 'grad_attn_sinks': 'grad_w', 'grad_rnn_w_in': 'grad_w', 'grad_rnn_conv_w': 'grad_w', 'grad_rnn_conv_b': 'grad_w', 'grad_rnn_w_a': 'grad_w', 'grad_rnn_b_a': 'grad_w', 'grad_rnn_w_x': 'grad_w', 'grad_rnn_b_x': 'grad_w', 'grad_rnn_lam': 'grad_w', 'grad_rnn_w_out': 'grad_w', 'grad_mlp_w1': 'grad_w', 'grad_mlp_w2': 'grad_w', 'delta_ada_w': 'delta_w', 'delta_ada_b': 'delta_w', 'delta_ln_g': 'delta_w', 'delta_ln_b': 'delta_w', 'delta_attn_w_in': 'delta_w', 'delta_attn_w_out': 'delta_w', 'delta_attn_sinks': 'delta_w', 'delta_rnn_w_in': 'delta_w', 'delta_rnn_conv_w': 'delta_w', 'delta_rnn_conv_b': 'delta_w', 'delta_rnn_w_a': 'delta_w', 'delta_rnn_b_a': 'delta_w', 'delta_rnn_w_x': 'delta_w', 'delta_rnn_b_x': 'delta_w', 'delta_rnn_lam': 'delta_w', 'delta_rnn_w_out': 'delta_w', 'delta_mlp_w1': 'delta_w', 'delta_mlp_w2': 'delta_w', 'new_m_ada_w': 'new_m', 'new_m_ada_b': 'new_m', 'new_m_ln_g': 'new_m', 'new_m_ln_b': 'new_m', 'new_m_attn_w_in': 'new_m', 'new_m_attn_w_out': 'new_m', 'new_m_attn_sinks': 'new_m', 'new_m_rnn_w_in': 'new_m', 'new_m_rnn_conv_w': 'new_m', 'new_m_rnn_conv_b': 'new_m', 'new_m_rnn_w_a': 'new_m', 'new_m_rnn_b_a': 'new_m', 'new_m_rnn_w_x': 'new_m', 'new_m_rnn_b_x': 'new_m', 'new_m_rnn_lam': 'new_m', 'new_m_rnn_w_out': 'new_m', 'new_m_mlp_w1': 'new_m', 'new_m_mlp_w2': 'new_m', 'new_v_ada_w': 'new_v', 'new_v_ada_b': 'new_v', 'new_v_ln_g': 'new_v', 'new_v_ln_b': 'new_v', 'new_v_attn_w_in': 'new_v', 'new_v_attn_w_out': 'new_v', 'new_v_attn_sinks': 'new_v', 'new_v_rnn_w_in': 'new_v', 'new_v_rnn_conv_w': 'new_v', 'new_v_rnn_conv_b': 'new_v', 'new_v_rnn_w_a': 'new_v', 'new_v_rnn_b_a': 'new_v', 'new_v_rnn_w_x': 'new_v', 'new_v_rnn_b_x': 'new_v', 'new_v_rnn_lam': 'new_v', 'new_v_rnn_w_out': 'new_v', 'new_v_mlp_w1': 'new_v', 'new_v_mlp_w2': 'new_v'}


def _forward(args):
    return _fwd_reference(*[args[k] for k in FWD_PARAMS])


def _output_shape():
    def fwd():
        inp = _fwd_setup_inputs(0)
        return _fwd_reference(*[inp[k] for k in FWD_PARAMS])
    out = _jax.eval_shape(fwd)
    return out.shape, out.dtype

N_MICROBATCH = 1
ADAM_LR = 0.001
ADAM_B1 = 0.9
ADAM_B2 = 0.999
ADAM_EPS = 1e-08
ADAM_WD = 0.01
ADAM_STEP = 10
PER_EXAMPLE_BATCH_AXIS = {'x': 0, 'c': 0, 'loss_target': 0}
SHARED_INPUTS = []
_WEIGHT_DTYPES = {'ada_w': _jnp.float32, 'ada_b': _jnp.float32, 'ln_g': _jnp.float32, 'ln_b': _jnp.float32, 'attn_w_in': _jnp.float32, 'attn_w_out': _jnp.float32, 'attn_sinks': _jnp.float32, 'rnn_w_in': _jnp.float32, 'rnn_conv_w': _jnp.float32, 'rnn_conv_b': _jnp.float32, 'rnn_w_a': _jnp.float32, 'rnn_b_a': _jnp.float32, 'rnn_w_x': _jnp.float32, 'rnn_b_x': _jnp.float32, 'rnn_lam': _jnp.float32, 'rnn_w_out': _jnp.float32, 'mlp_w1': _jnp.float32, 'mlp_w2': _jnp.float32}
MOMENT_SCALE = {'ada_w': 2.033272e-01, 'ada_b': 5.428965e-01, 'ln_g': 3.249015e+01, 'ln_b': 9.608056e+00, 'attn_w_in': 7.573992e-02, 'attn_w_out': 2.063531e-01, 'attn_sinks': 3.077283e-03, 'rnn_w_in': 2.502909e-01, 'rnn_conv_w': 3.009815e-01, 'rnn_conv_b': 9.975723e-01, 'rnn_w_a': 1.581297e-02, 'rnn_b_a': 3.097980e-02, 'rnn_w_x': 3.546516e-02, 'rnn_b_x': 7.203627e-02, 'rnn_lam': 8.297749e-02, 'rnn_w_out': 6.728574e-01, 'mlp_w1': 8.010320e-02, 'mlp_w2': 5.251032e-01}


def _to_microbatches(a, axis):
    t = _jnp.moveaxis(a, axis, 0)
    t = t.reshape((N_MICROBATCH, t.shape[0] // N_MICROBATCH) + t.shape[1:])
    return _jnp.moveaxis(t, 1, axis + 1)


def setup_inputs(seed: int = 0) -> dict:
    inp = _fwd_setup_inputs(seed)
    key = _jax.random.fold_in(_jax.random.key(seed), 7919)
    shape, _ = _output_shape()
    out = dict(inp)
    out["loss_target"] = _jax.random.normal(_jax.random.fold_in(key, 0), shape, _jnp.float32)
    for i, name in enumerate(TWIN_WEIGHTS):
        w = inp[name].astype(_jnp.float32)
        if MOMENT_SCALE is None:
            s = _jnp.sqrt(_jnp.mean(_jnp.square(w)) + 1e-30)
        else:
            s = MOMENT_SCALE[name]
        km, kv = _jax.random.split(_jax.random.fold_in(key, i + 1))
        out[name] = w
        out["m_" + name] = s * _jax.random.normal(km, w.shape, _jnp.float32)
        out["v_" + name] = (s * s) * _jax.random.uniform(kv, w.shape, _jnp.float32, 0.5, 1.5)
    if N_MICROBATCH > 1:
        for name, axis in PER_EXAMPLE_BATCH_AXIS.items():
            out[name] = _to_microbatches(out[name], axis)
    return {'x': out['x'], 'c': out['c'], 'ada_w': out['ada_w'], 'ada_b': out['ada_b'], 'ln_g': out['ln_g'], 'ln_b': out['ln_b'], 'attn_w_in': out['attn_w_in'], 'attn_w_out': out['attn_w_out'], 'attn_sinks': out['attn_sinks'], 'rnn_w_in': out['rnn_w_in'], 'rnn_conv_w': out['rnn_conv_w'], 'rnn_conv_b': out['rnn_conv_b'], 'rnn_w_a': out['rnn_w_a'], 'rnn_b_a': out['rnn_b_a'], 'rnn_w_x': out['rnn_w_x'], 'rnn_b_x': out['rnn_b_x'], 'rnn_lam': out['rnn_lam'], 'rnn_w_out': out['rnn_w_out'], 'mlp_w1': out['mlp_w1'], 'mlp_w2': out['mlp_w2'], 'loss_target': out['loss_target'], 'm_ada_w': out['m_ada_w'], 'm_ada_b': out['m_ada_b'], 'm_ln_g': out['m_ln_g'], 'm_ln_b': out['m_ln_b'], 'm_attn_w_in': out['m_attn_w_in'], 'm_attn_w_out': out['m_attn_w_out'], 'm_attn_sinks': out['m_attn_sinks'], 'm_rnn_w_in': out['m_rnn_w_in'], 'm_rnn_conv_w': out['m_rnn_conv_w'], 'm_rnn_conv_b': out['m_rnn_conv_b'], 'm_rnn_w_a': out['m_rnn_w_a'], 'm_rnn_b_a': out['m_rnn_b_a'], 'm_rnn_w_x': out['m_rnn_w_x'], 'm_rnn_b_x': out['m_rnn_b_x'], 'm_rnn_lam': out['m_rnn_lam'], 'm_rnn_w_out': out['m_rnn_w_out'], 'm_mlp_w1': out['m_mlp_w1'], 'm_mlp_w2': out['m_mlp_w2'], 'v_ada_w': out['v_ada_w'], 'v_ada_b': out['v_ada_b'], 'v_ln_g': out['v_ln_g'], 'v_ln_b': out['v_ln_b'], 'v_attn_w_in': out['v_attn_w_in'], 'v_attn_w_out': out['v_attn_w_out'], 'v_attn_sinks': out['v_attn_sinks'], 'v_rnn_w_in': out['v_rnn_w_in'], 'v_rnn_conv_w': out['v_rnn_conv_w'], 'v_rnn_conv_b': out['v_rnn_conv_b'], 'v_rnn_w_a': out['v_rnn_w_a'], 'v_rnn_b_a': out['v_rnn_b_a'], 'v_rnn_w_x': out['v_rnn_w_x'], 'v_rnn_b_x': out['v_rnn_b_x'], 'v_rnn_lam': out['v_rnn_lam'], 'v_rnn_w_out': out['v_rnn_w_out'], 'v_mlp_w1': out['v_mlp_w1'], 'v_mlp_w2': out['v_mlp_w2']}


def _loss(weights, diff, rest, loss_target):
    with _jax.named_scope("forward"):
        args = {**rest, TWIN_DIFF_INPUT: diff, **{k: w.astype(_WEIGHT_DTYPES[k]) for k, w in weights.items()}}
        y = _forward(args)
    with _jax.named_scope("loss_head"):
        err = _jnp.square(y.astype(_jnp.float32) - loss_target)
        return 0.5 * _jnp.sum(_jnp.mean(err, axis=-1)) if err.ndim else 0.5 * err


def _adamw(w, g, m, v):
    m = ADAM_B1 * m + (1.0 - ADAM_B1) * g
    v = ADAM_B2 * v + (1.0 - ADAM_B2) * _jnp.square(g)
    m_hat = m / (1.0 - ADAM_B1 ** ADAM_STEP)
    v_hat = v / (1.0 - ADAM_B2 ** ADAM_STEP)
    delta = -ADAM_LR * (m_hat / (_jnp.sqrt(v_hat) + ADAM_EPS) + ADAM_WD * w)
    return delta, m, v


def reference(x, c, ada_w, ada_b, ln_g, ln_b, attn_w_in, attn_w_out, attn_sinks, rnn_w_in, rnn_conv_w, rnn_conv_b, rnn_w_a, rnn_b_a, rnn_w_x, rnn_b_x, rnn_lam, rnn_w_out, mlp_w1, mlp_w2, loss_target, m_ada_w, m_ada_b, m_ln_g, m_ln_b, m_attn_w_in, m_attn_w_out, m_attn_sinks, m_rnn_w_in, m_rnn_conv_w, m_rnn_conv_b, m_rnn_w_a, m_rnn_b_a, m_rnn_w_x, m_rnn_b_x, m_rnn_lam, m_rnn_w_out, m_mlp_w1, m_mlp_w2, v_ada_w, v_ada_b, v_ln_g, v_ln_b, v_attn_w_in, v_attn_w_out, v_attn_sinks, v_rnn_w_in, v_rnn_conv_w, v_rnn_conv_b, v_rnn_w_a, v_rnn_b_a, v_rnn_w_x, v_rnn_b_x, v_rnn_lam, v_rnn_w_out, v_mlp_w1, v_mlp_w2):
    given = dict(x=x, c=c, ada_w=ada_w, ada_b=ada_b, ln_g=ln_g, ln_b=ln_b, attn_w_in=attn_w_in, attn_w_out=attn_w_out, attn_sinks=attn_sinks, rnn_w_in=rnn_w_in, rnn_conv_w=rnn_conv_w, rnn_conv_b=rnn_conv_b, rnn_w_a=rnn_w_a, rnn_b_a=rnn_b_a, rnn_w_x=rnn_w_x, rnn_b_x=rnn_b_x, rnn_lam=rnn_lam, rnn_w_out=rnn_w_out, mlp_w1=mlp_w1, mlp_w2=mlp_w2, loss_target=loss_target, m_ada_w=m_ada_w, m_ada_b=m_ada_b, m_ln_g=m_ln_g, m_ln_b=m_ln_b, m_attn_w_in=m_attn_w_in, m_attn_w_out=m_attn_w_out, m_attn_sinks=m_attn_sinks, m_rnn_w_in=m_rnn_w_in, m_rnn_conv_w=m_rnn_conv_w, m_rnn_conv_b=m_rnn_conv_b, m_rnn_w_a=m_rnn_w_a, m_rnn_b_a=m_rnn_b_a, m_rnn_w_x=m_rnn_w_x, m_rnn_b_x=m_rnn_b_x, m_rnn_lam=m_rnn_lam, m_rnn_w_out=m_rnn_w_out, m_mlp_w1=m_mlp_w1, m_mlp_w2=m_mlp_w2, v_ada_w=v_ada_w, v_ada_b=v_ada_b, v_ln_g=v_ln_g, v_ln_b=v_ln_b, v_attn_w_in=v_attn_w_in, v_attn_w_out=v_attn_w_out, v_attn_sinks=v_attn_sinks, v_rnn_w_in=v_rnn_w_in, v_rnn_conv_w=v_rnn_conv_w, v_rnn_conv_b=v_rnn_conv_b, v_rnn_w_a=v_rnn_w_a, v_rnn_b_a=v_rnn_b_a, v_rnn_w_x=v_rnn_w_x, v_rnn_b_x=v_rnn_b_x, v_rnn_lam=v_rnn_lam, v_rnn_w_out=v_rnn_w_out, v_mlp_w1=v_mlp_w1, v_mlp_w2=v_mlp_w2)
    weights = {n: given[n] for n in TWIN_WEIGHTS}
    shared = {n: given[n] for n in SHARED_INPUTS}
    per_example = {n: given[n] for n in ['x', 'c']}
    grad_fn = _jax.value_and_grad(_loss, argnums=(0, 1))

    def one_microbatch(ex, loss_target):
        ex = dict(ex)
        diff = ex.pop(TWIN_DIFF_INPUT)
        return grad_fn(weights, diff, {**shared, **ex}, loss_target)

    if N_MICROBATCH == 1:
        loss, (grad_w, grad_x) = one_microbatch(per_example, given["loss_target"])
    else:
        def body(carry, xs):
            loss_sum, grad_sum = carry
            l_k, (gw_k, gx_k) = one_microbatch(xs[0], xs[1])
            with _jax.named_scope("update"):
                return (loss_sum + l_k, _jax.tree.map(_jnp.add, grad_sum, gw_k)), gx_k

        init = (_jnp.zeros((), _jnp.float32), _jax.tree.map(_jnp.zeros_like, weights))
        (loss, grad_w), grad_x = _jax.lax.scan(body, init, (per_example, given["loss_target"]))
    with _jax.named_scope("update"):
        delta_w, new_m, new_v = {}, {}, {}
        for n in TWIN_WEIGHTS:
            delta_w[n], new_m[n], new_v[n] = _adamw(weights[n], grad_w[n], given["m_" + n], given["v_" + n])
    return (loss, grad_x, *[grad_w[n] for n in TWIN_WEIGHTS], *[delta_w[n] for n in TWIN_WEIGHTS],
            *[new_m[n] for n in TWIN_WEIGHTS], *[new_v[n] for n in TWIN_WEIGHTS])
```

```python
import functools
import math

import jax
import jax.numpy as jnp
from jax import lax
from jax.experimental import pallas as pl
from jax.experimental.pallas import tpu as pltpu

F32, BF16 = jnp.float32, jnp.bfloat16
MESH = pl.DeviceIdType.MESH

D_MODEL = 1024
N_Q, N_KV, HEAD = 8, 2, 128
ROT, THETA = 32, 500000.0
QBLK = 128
D_QKV = (N_Q + 2 * N_KV) * HEAD
D_RNN, N_RB, RB_W = 1536, 16, 96
CG = 384
N_CG = D_RNN // CG
D_FF = 4096
FF_CHUNK = 1024
DEPTH = 2
ALPHA = (2.0 * DEPTH) ** 0.25
LN_EPS = 1e-5
LRU_C = 8.0
N_DEV = 8
LR, B1, B2, ADAM_EPS, WD, STEP = 0.001, 0.9, 0.999, 1e-8, 0.01, 10

VMEM_LIMIT = 56 * 1024 * 1024
TM_MM = 512
TM_MLP = 256
TT_RNN = 512
SB_RNN = 128
TK_WG = 1024


def _nn(a, b):
    return jnp.dot(a, b, preferred_element_type=F32)


def _nt(a, b):
    return lax.dot_general(a, b, (((1,), (1,)), ((), ())), preferred_element_type=F32)


def _tn(a, b):
    return lax.dot_general(a, b, (((0,), (0,)), ((), ())), preferred_element_type=F32)


def _blk(n, pref):
    t = min(n, pref)
    assert n % t == 0, (n, pref)
    return t


def _params(**kw):
    return pltpu.CompilerParams(vmem_limit_bytes=VMEM_LIMIT, **kw)


def _row(tm, w):
    return pl.BlockSpec((tm, w), lambda i: (i, 0))


def _res(shape):
    return pl.BlockSpec(shape, lambda i: (0,) * len(shape), pipeline_mode=pl.Buffered(1))


def _mod(mod_ref, k):
    return mod_ref[3 * k:3 * k + 1, :], mod_ref[3 * k + 1:3 * k + 2, :], mod_ref[3 * k + 2:3 * k + 3, :]


def _ln_stats(z):
    mu = jnp.mean(z, axis=-1, keepdims=True)
    zc = z - mu
    var = jnp.mean(zc * zc, axis=-1, keepdims=True)
    rstd = lax.rsqrt(var + LN_EPS)
    return zc * rstd, rstd


def _ln_bwd(dxo, xhat, rstd, g):
    dxh = dxo * g
    m1 = jnp.mean(dxh, axis=-1, keepdims=True)
    m2 = jnp.mean(dxh * xhat, axis=-1, keepdims=True)
    return rstd * (dxh - m1 - xhat * m2)


def _colsum(v):
    return jnp.sum(v, axis=0, keepdims=True)


def _sigmoid(v):
    return 1.0 / (1.0 + jnp.exp(-v))


def _gelu_parts(v):
    k = math.sqrt(2.0 / math.pi)
    u = k * (v + 0.044715 * v * v * v)
    t = jnp.tanh(u)
    g = 0.5 * v * (1.0 + t)
    dg = 0.5 * (1.0 + t) + 0.5 * v * (1.0 - t * t) * k * (1.0 + 3.0 * 0.044715 * v * v)
    return g, dg


def _me():
    return lax.axis_index("x"), lax.axis_index("y"), lax.axis_index("c")


def _idx(p):
    return 4 * p[0] + 2 * p[1] + p[2]


def _peers(me):
    x, y, c = me
    out = []
    for k in range(1, N_DEV):
        out.append((1 - x if k & 4 else x, 1 - y if k & 2 else y, 1 - c if k & 1 else c))
    return out


def all_gather(srcs, name):
    n = len(srcs)

    def body(*refs):
        ins, outs = refs[:n], refs[n:2 * n]
        send_sems, recv_sems, local_sems = refs[2 * n:]
        x, y, c = me = _me()
        sibling = (x, y, 1 - c)
        chips = [(1 - x, y), (x, 1 - y), (1 - x, 1 - y)]

        def copy(t, k, block, to, src=None):
            slot = outs[t].at[_idx(block)]
            return pltpu.make_async_remote_copy(
                src_ref=slot if src is None else src, dst_ref=slot,
                send_sem=send_sems.at[t, k], recv_sem=recv_sems.at[t, k],
                device_id=to, device_id_type=MESH)

        mine = [pltpu.make_async_copy(ins[t], outs[t].at[_idx(me)], local_sems.at[t]) for t in range(n)]
        for cp in mine:
            cp.start()
        first = []
        for t in range(n):
            first.append(copy(t, 0, me, sibling, src=ins[t]))
            first += [copy(t, 1 + j, me, (*chip, c), src=ins[t]) for j, chip in enumerate(chips)]
        for cp in first:
            cp.start()
        passed = []
        for j, chip in enumerate(chips):
            for t in range(n):
                copy(t, 1 + j, (*chip, c), me).wait_recv()
                fwd = copy(t, 4 + j, (*chip, c), sibling)
                fwd.start()
                passed.append(fwd)
        for t in range(n):
            copy(t, 0, sibling, me).wait_recv()
            for j, chip in enumerate(chips):
                copy(t, 4 + j, (*chip, 1 - c), me).wait_recv()
        for cp in first + passed:
            cp.wait_send()
        for cp in mine:
            cp.wait()

    any_spec = pl.BlockSpec(memory_space=pl.ANY)
    return pl.pallas_call(
        body, name=name,
        out_shape=[jax.ShapeDtypeStruct((N_DEV,) + s.shape, s.dtype) for s in srcs],
        in_specs=[any_spec] * n, out_specs=[any_spec] * n,
        scratch_shapes=[pltpu.SemaphoreType.DMA((n, 7)), pltpu.SemaphoreType.DMA((n, 7)),
                        pltpu.SemaphoreType.DMA((n,))],
    )(*srcs)


def _a2a_start(srcs, dsts, send_sems, recv_sems, local_sems, me, sem_base=0):
    peers = _peers(me)
    started = []
    for t in range(len(srcs)):
        loc = pltpu.make_async_copy(srcs[t].at[_idx(me)], dsts[t].at[_idx(me)], local_sems.at[sem_base + t])
        loc.start()
        started.append(("local", loc))
        for k, p in enumerate(peers):
            cp = pltpu.make_async_remote_copy(
                src_ref=srcs[t].at[_idx(p)], dst_ref=dsts[t].at[_idx(me)],
                send_sem=send_sems.at[sem_base + t, k], recv_sem=recv_sems.at[sem_base + t, k],
                device_id=p, device_id_type=MESH)
            cp.start()
            started.append(("remote", cp))
    return started


def _a2a_finish(started, dsts, send_sems, recv_sems, me, sem_base=0):
    peers = _peers(me)
    for t in range(len(dsts)):
        for k, p in enumerate(peers):
            slot = dsts[t].at[_idx(p)]
            pltpu.make_async_remote_copy(
                src_ref=slot, dst_ref=slot, send_sem=send_sems.at[sem_base + t, k],
                recv_sem=recv_sems.at[sem_base + t, k], device_id=p, device_id_type=MESH).wait_recv()
    for kind, cp in started:
        if kind == "local":
            cp.wait()
        else:
            cp.wait_send()


def all_to_all(srcs, name):
    n = len(srcs)

    def body(*refs):
        ins, outs = refs[:n], refs[n:2 * n]
        send_sems, recv_sems, local_sems = refs[2 * n:]
        me = _me()
        started = _a2a_start(ins, outs, send_sems, recv_sems, local_sems, me)
        _a2a_finish(started, outs, send_sems, recv_sems, me)

    any_spec = pl.BlockSpec(memory_space=pl.ANY)
    return pl.pallas_call(
        body, name=name,
        out_shape=[jax.ShapeDtypeStruct(s.shape, s.dtype) for s in srcs],
        in_specs=[any_spec] * n, out_specs=[any_spec] * n,
        scratch_shapes=[pltpu.SemaphoreType.DMA((n, 7)), pltpu.SemaphoreType.DMA((n, 7)),
                        pltpu.SemaphoreType.DMA((n,))],
    )(*srcs)


def ada_modulation(c8, ada_w, ada_b):
    def body(c_ref, w_ref, b_ref, call_ref, modr_ref, modp, send_sems, recv_sems, local_sems):
        me = _me()
        for j in range(N_DEV):
            modp[j] = jnp.zeros(modp.shape[1:], F32)
        peers = _peers(me)
        sends = []
        for k, p in enumerate(peers):
            cp = pltpu.make_async_remote_copy(
                src_ref=c_ref, dst_ref=call_ref.at[_idx(me)], send_sem=send_sems.at[0, k],
                recv_sem=recv_sems.at[0, k], device_id=p, device_id_type=MESH)
            cp.start()
            sends.append(cp)
        call_ref[_idx(me)] = c_ref[...]
        for k, p in enumerate(peers):
            slot = call_ref.at[_idx(p)]
            pltpu.make_async_remote_copy(
                src_ref=slot, dst_ref=slot, send_sem=send_sems.at[0, k], recv_sem=recv_sems.at[0, k],
                device_id=p, device_id_type=MESH).wait_recv()
        for cp in sends:
            cp.wait_send()
        cv = call_ref[...].reshape(N_DEV * 8, D_MODEL)
        s = (cv * _sigmoid(cv)).astype(BF16)
        for k in range(4):
            res = _nn(s, w_ref[k].astype(BF16)) + b_ref[k]
            for j in range(N_DEV):
                modp[j, 8 * k:8 * k + 8, :] = res[8 * j:8 * j + 8, :]
        started = _a2a_start([modp], [modr_ref], send_sems, recv_sems, local_sems, me, sem_base=1)
        _a2a_finish(started, [modr_ref], send_sems, recv_sems, me, sem_base=1)

    vm = pl.BlockSpec(memory_space=pltpu.VMEM)
    return pl.pallas_call(
        body, name="ada_modulation",
        out_shape=[jax.ShapeDtypeStruct((N_DEV, 8, D_MODEL), F32), jax.ShapeDtypeStruct((N_DEV, 32, CG), F32)],
        in_specs=[vm, vm, vm], out_specs=[vm, vm],
        scratch_shapes=[pltpu.VMEM((N_DEV, 32, CG), F32), pltpu.SemaphoreType.DMA((2, 7)),
                        pltpu.SemaphoreType.DMA((2, 7)), pltpu.SemaphoreType.DMA((2,))],
        compiler_params=_params(),
    )(c8, ada_w, ada_b)


def ada_grads(gsend, c_t):
    def body(g_ref, ct_ref, gw_ref, gb_ref, grecv, send_sems, recv_sems, local_sems):
        me = _me()
        started = _a2a_start([g_ref], [grecv], send_sems, recv_sems, local_sems, me)
        _a2a_finish(started, [grecv], send_sems, recv_sems, me)
        ct = ct_ref[...]
        st = (ct * _sigmoid(ct)).astype(BF16).astype(F32)
        gb = jnp.zeros((8, CG), F32)
        for b in range(N_DEV):
            gb = gb + grecv[b]
        gb_ref[...] = gb
        for k in range(4):
            acc = jnp.zeros((D_MODEL, CG), F32)
            for b in range(N_DEV):
                row = grecv[b, k:k + 1, :].astype(BF16).astype(F32)
                acc = acc + st[:, b:b + 1] * row
            gw_ref[k] = acc

    vm = pl.BlockSpec(memory_space=pltpu.VMEM)
    return pl.pallas_call(
        body, name="ada_grads",
        out_shape=[jax.ShapeDtypeStruct((4, D_MODEL, CG), F32), jax.ShapeDtypeStruct((8, CG), F32)],
        in_specs=[vm, vm], out_specs=[vm, vm],
        scratch_shapes=[pltpu.VMEM((N_DEV, 8, CG), F32), pltpu.SemaphoreType.DMA((1, 7)),
                        pltpu.SemaphoreType.DMA((1, 7)), pltpu.SemaphoreType.DMA((1,))],
        compiler_params=_params(),
    )(gsend, c_t)


def small_reduce(ssend, rw):
    r = ssend.shape[1]

    def body(s_ref, red_ref, wag_ref, recv, send_sems, recv_sems, local_sems):
        me = _me()
        started = _a2a_start([s_ref], [recv], send_sems, recv_sems, local_sems, me)
        _a2a_finish(started, [recv], send_sems, recv_sems, me)
        acc = recv[0]
        for j in range(1, N_DEV):
            acc = acc + recv[j]
        red_ref[...] = acc
        peers = _peers(me)
        top = red_ref.at[pl.ds(0, rw), :]
        sends = []
        for k, p in enumerate(peers):
            cp = pltpu.make_async_remote_copy(
                src_ref=top, dst_ref=wag_ref.at[_idx(me)], send_sem=send_sems.at[1, k],
                recv_sem=recv_sems.at[1, k], device_id=p, device_id_type=MESH)
            cp.start()
            sends.append(cp)
        wag_ref[_idx(me)] = acc[0:rw, :]
        for k, p in enumerate(peers):
            slot = wag_ref.at[_idx(p)]
            pltpu.make_async_remote_copy(
                src_ref=slot, dst_ref=slot, send_sem=send_sems.at[1, k], recv_sem=recv_sems.at[1, k],
                device_id=p, device_id_type=MESH).wait_recv()
        for cp in sends:
            cp.wait_send()

    vm = pl.BlockSpec(memory_space=pltpu.VMEM)
    return pl.pallas_call(
        body, name="small_reduce",
        out_shape=[jax.ShapeDtypeStruct((r, 128), F32), jax.ShapeDtypeStruct((N_DEV, rw, 128), F32)],
        in_specs=[vm], out_specs=[vm, vm],
        scratch_shapes=[pltpu.VMEM((N_DEV, r, 128), F32), pltpu.SemaphoreType.DMA((2, 7)),
                        pltpu.SemaphoreType.DMA((2, 7)), pltpu.SemaphoreType.DMA((1,))],
        compiler_params=_params(),
    )(ssend)


def _rope(t, cos, s1, s2):
    return t * cos + pltpu.roll(t, 16, 1) * s1 + pltpu.roll(t, HEAD - 16, 1) * s2


def _rope_bwd(d, cos, s1, s2):
    return d * cos + pltpu.roll(d * s1, HEAD - 16, 1) + pltpu.roll(d * s2, 16, 1)


def attn_in_fwd(x, mods, k, win_t, rope):
    s = x.shape[0]
    tm = _blk(s, TM_MM)

    def body(x_ref, mod_ref, w_ref, c_ref, s1_ref, s2_ref, q_ref, k_ref, v_ref):
        shift, scale, _ = _mod(mod_ref, k)
        h = (x_ref[...] * (1.0 + scale) + shift).astype(BF16)
        qkv = _nt(h, w_ref[...])
        cos, s1, s2 = c_ref[...], s1_ref[...], s2_ref[...]
        for hh in range(N_Q + N_KV):
            r = _rope(qkv[:, HEAD * hh:HEAD * (hh + 1)], cos, s1, s2).astype(BF16)
            if hh < N_Q:
                q_ref[:, HEAD * hh:HEAD * (hh + 1)] = r
            else:
                k_ref[:, HEAD * (hh - N_Q):HEAD * (hh - N_Q + 1)] = r
        v_ref[...] = qkv[:, HEAD * (N_Q + N_KV):].astype(BF16)

    return pl.pallas_call(
        body, name="attn_in_fwd", grid=(s // tm,),
        out_shape=[jax.ShapeDtypeStruct((s, N_Q * HEAD), BF16), jax.ShapeDtypeStruct((s, N_KV * HEAD), BF16),
                   jax.ShapeDtypeStruct((s, N_KV * HEAD), BF16)],
        in_specs=[_row(tm, D_MODEL), _res(mods.shape), _res(win_t.shape),
                  _row(tm, HEAD), _row(tm, HEAD), _row(tm, HEAD)],
        out_specs=[_row(tm, N_Q * HEAD), _row(tm, N_KV * HEAD), _row(tm, N_KV * HEAD)],
        compiler_params=_params(),
    )(x, mods, win_t, *rope)


def _kv_specs(nblk):
    w = N_KV * HEAD
    return [pl.BlockSpec((QBLK, w), lambda n: (jnp.maximum(n - 1, 0), 0)),
            pl.BlockSpec((QBLK, w), lambda n: (n, 0)),
            pl.BlockSpec((QBLK, w), lambda n: (jnp.minimum(n + 1, nblk - 1), 0))]


def _attn_mask(n, s):
    qi = lax.broadcasted_iota(jnp.int32, (QBLK, 3 * QBLK), 0)
    kj = lax.broadcasted_iota(jnp.int32, (QBLK, 3 * QBLK), 1)
    rel = kj - QBLK - qi
    kpos = kj + (n - 1) * QBLK
    return (jnp.abs(rel) <= QBLK) & (kpos >= 0) & (kpos < s)


def _attn_probs(qh, kh, valid, sink):
    sc = _nt(qh, kh) * (HEAD ** -0.5)
    sc = jnp.where(valid, sc, -1e30)
    m = jnp.maximum(jnp.max(sc, axis=-1, keepdims=True), sink)
    p = jnp.exp(sc - m)
    es = jnp.exp(sink - m)
    denom = jnp.sum(p, axis=-1, keepdims=True) + es
    return p / denom, es / denom


def attn_fwd(q, kk, v, sinks):
    s = q.shape[0]
    nblk = s // QBLK

    def body(sink_ref, q_ref, kp, ko, kn, vp, vo, vn, o_ref):
        n = pl.program_id(0)
        kcat = jnp.concatenate([kp[...], ko[...], kn[...]], axis=0)
        vcat = jnp.concatenate([vp[...], vo[...], vn[...]], axis=0)
        valid = _attn_mask(n, s)
        for hq in range(N_Q):
            kv = hq // (N_Q // N_KV)
            probs, _ = _attn_probs(q_ref[:, HEAD * hq:HEAD * (hq + 1)], kcat[:, HEAD * kv:HEAD * (kv + 1)],
                                   valid, sink_ref[0, hq])
            o_ref[:, HEAD * hq:HEAD * (hq + 1)] = _nn(probs.astype(BF16), vcat[:, HEAD * kv:HEAD * (kv + 1)]).astype(BF16)

    return pl.pallas_call(
        body, name="attn_fwd", grid=(nblk,),
        out_shape=jax.ShapeDtypeStruct((s, N_Q * HEAD), BF16),
        in_specs=[pl.BlockSpec(memory_space=pltpu.SMEM), pl.BlockSpec((QBLK, N_Q * HEAD), lambda n: (n, 0))]
        + _kv_specs(nblk) + _kv_specs(nblk),
        out_specs=pl.BlockSpec((QBLK, N_Q * HEAD), lambda n: (n, 0)),
        compiler_params=_params(),
    )(sinks, q, kk, kk, kk, v, v, v)


def post_fwd(ypre, w, x, mods, k, lng, lnb, gate_act=None):
    s = x.shape[0]
    tm = _blk(s, TM_MM)
    kdim = w.shape[0]
    rnn = gate_act is not None

    def body(*refs):
        if rnn:
            gt_ref, hf_ref, hb_ref, w_ref, x_ref, mod_ref, g_ref, b_ref, xo_ref, y_ref, yp_ref = refs
            act, _ = _gelu_parts(gt_ref[...])
            yp = ((hf_ref[...] + hb_ref[...]) * act).astype(BF16)
            yp_ref[...] = yp
        else:
            yp_ref, w_ref, x_ref, mod_ref, g_ref, b_ref, xo_ref, y_ref = refs
            yp = yp_ref[...]
        _, _, gate = _mod(mod_ref, k)
        y = _nn(yp, w_ref[...])
        y_ref[...] = y
        xhat, _ = _ln_stats(ALPHA * x_ref[...] + (1.0 + gate) * y)
        xo_ref[...] = xhat * g_ref[...] + b_ref[...]

    act_in = list(gate_act) if rnn else [ypre]
    out_shape = [jax.ShapeDtypeStruct((s, D_MODEL), F32), jax.ShapeDtypeStruct((s, D_MODEL), F32)]
    out_specs = [_row(tm, D_MODEL), _row(tm, D_MODEL)]
    if rnn:
        out_shape.append(jax.ShapeDtypeStruct((s, kdim), BF16))
        out_specs.append(_row(tm, kdim))
    return pl.pallas_call(
        body, name="rnn_post_fwd" if rnn else "attn_post_fwd", grid=(s // tm,),
        out_shape=out_shape,
        in_specs=[_row(tm, kdim)] * len(act_in) + [_res(w.shape), _row(tm, D_MODEL), _res(mods.shape),
                                                    _res(lng.shape), _res(lnb.shape)],
        out_specs=out_specs,
        compiler_params=_params(),
    )(*act_in, w, x, mods, lng, lnb)


def mlp_fwd(x, mods, k, w1_t, w2, lng, lnb):
    s = x.shape[0]
    tm = _blk(s, TM_MLP)

    def body(x_ref, mod_ref, w1_ref, w2_ref, g_ref, b_ref, xo_ref, y_ref):
        xv = x_ref[...]
        shift, scale, gate = _mod(mod_ref, k)
        h = (xv * (1.0 + scale) + shift).astype(BF16)
        y = jnp.zeros((tm, D_MODEL), F32)
        for c in range(D_FF // FF_CHUNK):
            rows = slice(FF_CHUNK * c, FF_CHUNK * (c + 1))
            a = jnp.maximum(_nt(h, w1_ref[rows, :]), 0.0)
            y = y + _nn((a * a).astype(BF16), w2_ref[rows, :])
        y_ref[...] = y
        xhat, _ = _ln_stats(ALPHA * xv + (1.0 + gate) * y)
        xo_ref[...] = xhat * g_ref[...] + b_ref[...]

    return pl.pallas_call(
        body, name="mlp_fwd", grid=(s // tm,),
        out_shape=[jax.ShapeDtypeStruct((s, D_MODEL), F32)] * 2,
        in_specs=[_row(tm, D_MODEL), _res(mods.shape), _res(w1_t.shape), _res(w2.shape),
                  _res(lng.shape), _res(lnb.shape)],
        out_specs=[_row(tm, D_MODEL)] * 2,
        compiler_params=_params(),
    )(x, mods, w1_t, w2, lng, lnb)


def rnn_in_fwd(x, mods, k, win_t):
    s = x.shape[0]
    tm = _blk(s, TM_MM)

    def body(x_ref, mod_ref, w_ref, xr_ref, gt_ref):
        shift, scale, _ = _mod(mod_ref, k)
        h = (x_ref[...] * (1.0 + scale) + shift).astype(BF16)
        xr_ref[...] = _nt(h, w_ref[0:D_RNN, :])
        gt_ref[...] = _nt(h, w_ref[D_RNN:2 * D_RNN, :])

    return pl.pallas_call(
        body, name="rnn_in_fwd", grid=(s // tm,),
        out_shape=[jax.ShapeDtypeStruct((s, D_RNN), F32)] * 2,
        in_specs=[_row(tm, D_MODEL), _res(mods.shape), _res(win_t.shape)],
        out_specs=[_row(tm, D_RNN)] * 2,
        compiler_params=_params(),
    )(x, mods, win_t)


def _shift_rows(v, k, row):
    n = v.shape[0]
    r = pltpu.roll(v, k % n, 0)
    keep = (row >= k) if k > 0 else (row < n + k)
    return jnp.where(keep, r, 0.0)


def conv_fwd(xr, cw, cb):
    s = xr.shape[0]

    def body(x_ref, w_ref, b_ref, o_ref):
        xv = x_ref[...]
        row = lax.broadcasted_iota(jnp.int32, xv.shape, 0)
        o_ref[...] = (b_ref[...] + w_ref[0:1, :] * _shift_rows(xv, 2, row) + w_ref[1:2, :] * _shift_rows(xv, 1, row)
                      + w_ref[2:3, :] * xv + w_ref[3:4, :] * _shift_rows(xv, -1, row))

    slab = pl.BlockSpec((s, 128), lambda j: (0, j))
    return pl.pallas_call(
        body, name="conv_fwd", grid=(D_RNN // 128,),
        out_shape=jax.ShapeDtypeStruct((s, D_RNN), F32),
        in_specs=[slab, pl.BlockSpec((4, 128), lambda j: (0, j)), pl.BlockSpec((1, 128), lambda j: (0, j))],
        out_specs=slab,
        compiler_params=_params(),
    )(xr, cw, cb)


def conv_bwd(da, db, xr, cw):
    s = xr.shape[0]

    def body(da_ref, db_ref, x_ref, w_ref, dx_ref, dw_ref, dbias_ref):
        d = da_ref[...] + db_ref[...]
        xv = x_ref[...]
        row = lax.broadcasted_iota(jnp.int32, xv.shape, 0)
        dx_ref[...] = (w_ref[0:1, :] * _shift_rows(d, -2, row) + w_ref[1:2, :] * _shift_rows(d, -1, row)
                       + w_ref[2:3, :] * d + w_ref[3:4, :] * _shift_rows(d, 1, row))
        dw_ref[0:1, :] = _colsum(d * _shift_rows(xv, 2, row))
        dw_ref[1:2, :] = _colsum(d * _shift_rows(xv, 1, row))
        dw_ref[2:3, :] = _colsum(d * xv)
        dw_ref[3:4, :] = _colsum(d * _shift_rows(xv, -1, row))
        dbias_ref[...] = _colsum(d)

    slab = pl.BlockSpec((s, 128), lambda j: (0, j))
    return pl.pallas_call(
        body, name="conv_bwd", grid=(D_RNN // 128,),
        out_shape=[jax.ShapeDtypeStruct((s, D_RNN), F32), jax.ShapeDtypeStruct((4, D_RNN), F32),
                   jax.ShapeDtypeStruct((1, D_RNN), F32)],
        in_specs=[slab, slab, slab, pl.BlockSpec((4, 128), lambda j: (0, j))],
        out_specs=[slab, pl.BlockSpec((4, 128), lambda j: (0, j)), pl.BlockSpec((1, 128), lambda j: (0, j))],
        compiler_params=_params(),
    )(da, db, xr, cw)


def _softplus_neg(lam):
    z = -lam
    e = jnp.exp(-jnp.abs(z))
    u = 1.0 + e
    log1p = jnp.where(u == 1.0, e, jnp.log(u) * e / jnp.where(u == 1.0, 1.0, u - 1.0))
    return jnp.maximum(z, 0.0) + log1p, 1.0 / (1.0 + jnp.exp(lam))


def _lru_gates(xv, wa_ref, wx_ref, ba_ref, bx_ref, lam_ref):
    xb = xv.astype(BF16)
    r = _sigmoid(_nn(xb, wa_ref[...]) + ba_ref[...])
    i = _sigmoid(_nn(xb, wx_ref[...]) + bx_ref[...])
    sp, sg = _softplus_neg(lam_ref[...])
    la = -LRU_C * r * sp
    a = jnp.exp(la)
    th = jnp.tanh(la)
    mult = jnp.sqrt(-2.0 * th / (1.0 - th))
    return xb, r, i, sp, sg, a, mult


def _scan(a, u, h0, reverse):
    n = a.shape[0]
    row = lax.broadcasted_iota(jnp.int32, a.shape, 0)
    sh = 1
    while sh < n:
        if reverse:
            keep = row < n - sh
            a_s = jnp.where(keep, pltpu.roll(a, n - sh, 0), 1.0)
            u_s = jnp.where(keep, pltpu.roll(u, n - sh, 0), 0.0)
        else:
            keep = row >= sh
            a_s = jnp.where(keep, pltpu.roll(a, sh, 0), 1.0)
            u_s = jnp.where(keep, pltpu.roll(u, sh, 0), 0.0)
        u = a * u_s + u
        a = a * a_s
        sh *= 2
    return u + a * h0


def _lru_specs(nt, tt, reverse):
    tmap = (lambda t: nt - 1 - t) if reverse else (lambda t: t)
    blk = pl.BlockSpec((tt, CG), lambda g, t: (tmap(t), g))
    wsp = pl.BlockSpec((None, CG, CG), lambda g, t: (g, 0, 0))
    vec = pl.BlockSpec((1, CG), lambda g, t: (0, g))
    return tmap, blk, wsp, vec


def lru_fwd(xc, wa, wx, ba, bx, lam, reverse):
    s = xc.shape[0]
    tt = _blk(s, TT_RNN)
    sb = _blk(tt, SB_RNN)
    nt = s // tt

    def body(x_ref, wa_ref, wx_ref, ba_ref, bx_ref, lam_ref, hs_ref, carry):
        @pl.when(pl.program_id(1) == 0)
        def _():
            carry[...] = jnp.zeros(carry.shape, F32)

        xv = x_ref[...]
        _, _, i, _, _, a, mult = _lru_gates(xv, wa_ref, wx_ref, ba_ref, bx_ref, lam_ref)
        u = mult * (i * xv)
        h0 = carry[0:1, :]
        order = range(tt // sb - 1, -1, -1) if reverse else range(tt // sb)
        for j in order:
            rows = slice(sb * j, sb * (j + 1))
            h = _scan(a[rows], u[rows], h0, reverse)
            hs_ref[rows, :] = h
            h0 = h[0:1, :] if reverse else h[sb - 1:sb, :]
        carry[0:1, :] = h0

    _, blk, wsp, vec = _lru_specs(nt, tt, reverse)
    return pl.pallas_call(
        body, name="lru_fwd_rev" if reverse else "lru_fwd", grid=(N_CG, nt),
        out_shape=jax.ShapeDtypeStruct((s, D_RNN), F32),
        in_specs=[blk, wsp, wsp, vec, vec, vec], out_specs=blk,
        scratch_shapes=[pltpu.VMEM((8, CG), F32)],
        compiler_params=_params(),
    )(xc, wa, wx, ba, bx, lam)


def lru_bwd(xc, dhs, hs, wa, wx, ba, bx, lam, reverse):
    s = xc.shape[0]
    tt = _blk(s, TT_RNN)
    sb = _blk(tt, SB_RNN)
    nt = s // tt
    back = not reverse

    def body(x_ref, dh_ref, hs_ref, nb_ref, wa_ref, wx_ref, ba_ref, bx_ref, lam_ref,
             dx_ref, dwa_ref, dwx_ref, dba_ref, dbx_ref, dlam_ref, carry):
        t = pl.program_id(1)

        @pl.when(t == 0)
        def _():
            carry[...] = jnp.zeros(carry.shape, F32)
            dwa_ref[...] = jnp.zeros(dwa_ref.shape, F32)
            dwx_ref[...] = jnp.zeros(dwx_ref.shape, F32)
            dba_ref[...] = jnp.zeros(dba_ref.shape, F32)
            dbx_ref[...] = jnp.zeros(dbx_ref.shape, F32)
            dlam_ref[...] = jnp.zeros(dlam_ref.shape, F32)

        xv = x_ref[...]
        xb, r, i, sp, sg, a, mult = _lru_gates(xv, wa_ref, wx_ref, ba_ref, bx_ref, lam_ref)
        row = lax.broadcasted_iota(jnp.int32, xv.shape, 0)
        hsv = hs_ref[...]
        inner = t < nt - 1
        if reverse:
            edge = jnp.where(inner, nb_ref[0:1, :], 0.0)
            hprev = jnp.where(row == tt - 1, edge, pltpu.roll(hsv, tt - 1, 0))
            a_next = jnp.where(row == 0, carry[1:2, :], pltpu.roll(a, 1, 0))
        else:
            edge = jnp.where(inner, nb_ref[7:8, :], 0.0)
            hprev = jnp.where(row == 0, edge, pltpu.roll(hsv, 1, 0))
            a_next = jnp.where(row == tt - 1, carry[1:2, :], pltpu.roll(a, tt - 1, 0))
        dhv = dh_ref[...]
        g0 = carry[0:1, :]
        parts = [None] * (tt // sb)
        order = range(tt // sb - 1, -1, -1) if back else range(tt // sb)
        for j in order:
            rows = slice(sb * j, sb * (j + 1))
            gj = _scan(a_next[rows], dhv[rows], g0, back)
            parts[j] = gj
            g0 = gj[0:1, :] if back else gj[sb - 1:sb, :]
        g = jnp.concatenate(parts, axis=0) if len(parts) > 1 else parts[0]
        carry[0:1, :] = g0
        carry[1:2, :] = a[0:1, :] if back else a[tt - 1:tt, :]

        da = g * hprev
        dmult = g * (i * xv)
        di = g * mult * xv
        dla = da * a - dmult * (a * a) / mult
        dpa = (dla * (-LRU_C * sp)) * r * (1.0 - r)
        dpx = di * i * (1.0 - i)
        dlam_ref[...] += _colsum(dla * (LRU_C * r * sg))
        dba_ref[...] += _colsum(dpa)
        dbx_ref[...] += _colsum(dpx)
        dpab, dpxb = dpa.astype(BF16), dpx.astype(BF16)
        dx_ref[...] = g * mult * i + _nt(dpab, wa_ref[...]) + _nt(dpxb, wx_ref[...])
        dwa_ref[...] += _tn(xb, dpab)
        dwx_ref[...] += _tn(xb, dpxb)

    tmap, blk, wsp, vec = _lru_specs(nt, tt, back)
    per8 = tt // 8
    if reverse:
        nb = pl.BlockSpec((8, CG), lambda g, t: (jnp.minimum((tmap(t) + 1) * per8, s // 8 - 1), g))
    else:
        nb = pl.BlockSpec((8, CG), lambda g, t: (jnp.maximum(tmap(t) * per8 - 1, 0), g))
    return pl.pallas_call(
        body, name="lru_bwd_rev" if reverse else "lru_bwd", grid=(N_CG, nt),
        out_shape=[jax.ShapeDtypeStruct((s, D_RNN), F32), jax.ShapeDtypeStruct((N_CG, CG, CG), F32),
                   jax.ShapeDtypeStruct((N_CG, CG, CG), F32)] + [jax.ShapeDtypeStruct((1, D_RNN), F32)] * 3,
        in_specs=[blk, blk, blk, nb, wsp, wsp, vec, vec, vec],
        out_specs=[blk, wsp, wsp, vec, vec, vec],
        scratch_shapes=[pltpu.VMEM((8, CG), F32)],
        compiler_params=_params(),
    )(xc, dhs, hs, hs, wa, wx, ba, bx, lam)


def loss_grad(xo, target):
    s = xo.shape[0]
    tm = _blk(s, TM_MM)

    def body(x_ref, t_ref, d_ref, sq_ref):
        @pl.when(pl.program_id(0) == 0)
        def _():
            sq_ref[...] = jnp.zeros(sq_ref.shape, F32)

        e = x_ref[...] - t_ref[...]
        d_ref[...] = e * (1.0 / D_MODEL)
        sq_ref[...] += _colsum(e * e)

    return pl.pallas_call(
        body, name="loss_grad", grid=(s // tm,),
        out_shape=[jax.ShapeDtypeStruct((s, D_MODEL), F32), jax.ShapeDtypeStruct((1, D_MODEL), F32)],
        in_specs=[_row(tm, D_MODEL)] * 2, out_specs=[_row(tm, D_MODEL), _res((1, D_MODEL))],
        compiler_params=_params(),
    )(xo, target)


def _ln_part_bwd(dxo, x, y, gate, g, sums_ref):
    xhat, rstd = _ln_stats(ALPHA * x + (1.0 + gate) * y)
    dz = _ln_bwd(dxo, xhat, rstd, g)
    sums_ref[2:3, :] += _colsum(dz * y)
    sums_ref[3:4, :] += _colsum(dxo * xhat)
    sums_ref[4:5, :] += _colsum(dxo)
    return dz


def mlp_bwd(dxo, x, y, mods, k, w1_t, w2, lng):
    s = x.shape[0]
    tm = _blk(s, TM_MLP)

    def body(d_ref, x_ref, y_ref, mod_ref, w1_ref, w2_ref, g_ref,
             dx_ref, da_ref, r_ref, h_ref, dy_ref, sums_ref):
        @pl.when(pl.program_id(0) == 0)
        def _():
            sums_ref[...] = jnp.zeros(sums_ref.shape, F32)

        xv = x_ref[...]
        shift, scale, gate = _mod(mod_ref, k)
        dz = _ln_part_bwd(d_ref[...], xv, y_ref[...], gate, g_ref[...], sums_ref)
        dyb = (dz * (1.0 + gate)).astype(BF16)
        dy_ref[...] = dyb
        h = (xv * (1.0 + scale) + shift).astype(BF16)
        h_ref[...] = h
        dh = jnp.zeros((tm, D_MODEL), F32)
        for c in range(D_FF // FF_CHUNK):
            rows = slice(FF_CHUNK * c, FF_CHUNK * (c + 1))
            a = jnp.maximum(_nt(h, w1_ref[rows, :]), 0.0)
            r_ref[:, rows] = (a * a).astype(BF16)
            da = (_nt(dyb, w2_ref[rows, :]) * (2.0 * a)).astype(BF16)
            da_ref[:, rows] = da
            dh = dh + _nn(da, w1_ref[rows, :])
        dx_ref[...] = ALPHA * dz + dh * (1.0 + scale)
        sums_ref[0:1, :] += _colsum(dh)
        sums_ref[1:2, :] += _colsum(dh * xv)

    return pl.pallas_call(
        body, name="mlp_bwd", grid=(s // tm,),
        out_shape=[jax.ShapeDtypeStruct((s, D_MODEL), F32), jax.ShapeDtypeStruct((s, D_FF), BF16),
                   jax.ShapeDtypeStruct((s, D_FF), BF16), jax.ShapeDtypeStruct((s, D_MODEL), BF16),
                   jax.ShapeDtypeStruct((s, D_MODEL), BF16), jax.ShapeDtypeStruct((8, D_MODEL), F32)],
        in_specs=[_row(tm, D_MODEL)] * 3 + [_res(mods.shape), _res(w1_t.shape), _res(w2.shape), _res(lng.shape)],
        out_specs=[_row(tm, D_MODEL), _row(tm, D_FF), _row(tm, D_FF), _row(tm, D_MODEL), _row(tm, D_MODEL),
                   _res((8, D_MODEL))],
        compiler_params=_params(),
    )(dxo, x, y, mods, w1_t, w2, lng)


def post_bwd(dxo, x, y, mods, k, w, lng, gate_act=None):
    s = x.shape[0]
    tm = _blk(s, TM_MM)
    kdim = w.shape[0]
    rnn = gate_act is not None

    def body(*refs):
        if rnn:
            (d_ref, x_ref, y_ref, mod_ref, w_ref, g_ref, gt_ref, hf_ref, hb_ref,
             dres_ref, dy_ref, sums_ref, dhs_ref, dgt_ref) = refs
        else:
            d_ref, x_ref, y_ref, mod_ref, w_ref, g_ref, dres_ref, dy_ref, sums_ref, dyp_ref = refs

        @pl.when(pl.program_id(0) == 0)
        def _():
            sums_ref[...] = jnp.zeros(sums_ref.shape, F32)

        _, _, gate = _mod(mod_ref, k)
        dz = _ln_part_bwd(d_ref[...], x_ref[...], y_ref[...], gate, g_ref[...], sums_ref)
        dres_ref[...] = ALPHA * dz
        dyb = (dz * (1.0 + gate)).astype(BF16)
        dy_ref[...] = dyb
        dyp = _nt(dyb, w_ref[...])
        if rnn:
            act, dact = _gelu_parts(gt_ref[...])
            dhs_ref[...] = dyp * act
            dgt_ref[...] = dyp * (hf_ref[...] + hb_ref[...]) * dact
        else:
            dyp_ref[...] = dyp.astype(BF16)

    ins = [dxo, x, y, mods, w, lng] + (list(gate_act) if rnn else [])
    in_specs = [_row(tm, D_MODEL)] * 3 + [_res(mods.shape), _res(w.shape), _res(lng.shape)]
    out_shape = [jax.ShapeDtypeStruct((s, D_MODEL), F32), jax.ShapeDtypeStruct((s, D_MODEL), BF16),
                 jax.ShapeDtypeStruct((8, D_MODEL), F32)]
    out_specs = [_row(tm, D_MODEL), _row(tm, D_MODEL), _res((8, D_MODEL))]
    if rnn:
        in_specs += [_row(tm, kdim)] * 3
        out_shape += [jax.ShapeDtypeStruct((s, kdim), F32)] * 2
        out_specs += [_row(tm, kdim)] * 2
    else:
        out_shape.append(jax.ShapeDtypeStruct((s, kdim), BF16))
        out_specs.append(_row(tm, kdim))
    return pl.pallas_call(
        body, name="rnn_post_bwd" if rnn else "attn_post_bwd", grid=(s // tm,),
        out_shape=out_shape, in_specs=in_specs, out_specs=out_specs,
        compiler_params=_params(),
    )(*ins)


def attn_bwd(q, kk, v, do, sinks):
    s = q.shape[0]
    nblk = s // QBLK
    scale = HEAD ** -0.5

    def body(sink_ref, q_ref, do_ref, kp, ko, kn, vp, vo, vn, dq_ref, dkp_ref, dvp_ref, ds_ref):
        n = pl.program_id(0)

        @pl.when(n == 0)
        def _():
            ds_ref[...] = jnp.zeros(ds_ref.shape, F32)

        kcat = jnp.concatenate([kp[...], ko[...], kn[...]], axis=0)
        vcat = jnp.concatenate([vp[...], vo[...], vn[...]], axis=0)
        valid = _attn_mask(n, s)
        lane = lax.broadcasted_iota(jnp.int32, (1, 128), 1)
        dsink = jnp.zeros((1, 128), F32)
        dk = [jnp.zeros((3 * QBLK, HEAD), F32) for _ in range(N_KV)]
        dv = [jnp.zeros((3 * QBLK, HEAD), F32) for _ in range(N_KV)]
        for hq in range(N_Q):
            kv = hq // (N_Q // N_KV)
            cols = slice(HEAD * hq, HEAD * (hq + 1))
            qh, doh = q_ref[:, cols], do_ref[:, cols]
            kh, vh = kcat[:, HEAD * kv:HEAD * (kv + 1)], vcat[:, HEAD * kv:HEAD * (kv + 1)]
            probs, psink = _attn_probs(qh, kh, valid, sink_ref[0, hq])
            dprobs = _nt(doh, vh)
            dv[kv] = dv[kv] + _tn(probs.astype(BF16), doh)
            rowdot = jnp.sum(probs * dprobs, axis=-1, keepdims=True)
            dsb = (probs * (dprobs - rowdot) * scale).astype(BF16)
            dq_ref[:, cols] = _nn(dsb, kh)
            dk[kv] = dk[kv] + _tn(dsb, qh)
            dsink = dsink + jnp.where(lane == hq, _colsum(-psink * rowdot), 0.0)
        for kv in range(N_KV):
            dkp_ref[:, HEAD * kv:HEAD * (kv + 1)] = dk[kv]
            dvp_ref[:, HEAD * kv:HEAD * (kv + 1)] = dv[kv]
        ds_ref[...] += dsink

    qspec = pl.BlockSpec((QBLK, N_Q * HEAD), lambda n: (n, 0))
    pspec = pl.BlockSpec((None, 3 * QBLK, N_KV * HEAD), lambda n: (n, 0, 0))
    return pl.pallas_call(
        body, name="attn_bwd", grid=(nblk,),
        out_shape=[jax.ShapeDtypeStruct((s, N_Q * HEAD), F32),
                   jax.ShapeDtypeStruct((nblk, 3 * QBLK, N_KV * HEAD), F32),
                   jax.ShapeDtypeStruct((nblk, 3 * QBLK, N_KV * HEAD), F32),
                   jax.ShapeDtypeStruct((1, 128), F32)],
        in_specs=[pl.BlockSpec(memory_space=pltpu.SMEM), qspec, qspec] + _kv_specs(nblk) + _kv_specs(nblk),
        out_specs=[qspec, pspec, pspec, pl.BlockSpec((1, 128), lambda n: (0, 0))],
        compiler_params=_params(),
    )(sinks, q, do, kk, kk, kk, v, v, v)


def kv_combine(dkp, dvp):
    nblk = dkp.shape[0]
    w = N_KV * HEAD

    def body(kp, ko, kn, vp, vo, vn, dk_ref, dv_ref):
        n = pl.program_id(0)
        dk_ref[...] = jnp.where(n > 0, kp[...], 0.0) + ko[...] + jnp.where(n < nblk - 1, kn[...], 0.0)
        dv_ref[...] = jnp.where(n > 0, vp[...], 0.0) + vo[...] + jnp.where(n < nblk - 1, vn[...], 0.0)

    specs = [pl.BlockSpec((None, QBLK, w), lambda n: (jnp.maximum(n - 1, 0), 2, 0)),
             pl.BlockSpec((None, QBLK, w), lambda n: (n, 1, 0)),
             pl.BlockSpec((None, QBLK, w), lambda n: (jnp.minimum(n + 1, nblk - 1), 0, 0))]
    out = pl.BlockSpec((QBLK, w), lambda n: (n, 0))
    return pl.pallas_call(
        body, name="kv_combine", grid=(nblk,),
        out_shape=[jax.ShapeDtypeStruct((nblk * QBLK, w), F32)] * 2,
        in_specs=specs + specs, out_specs=[out, out],
        compiler_params=_params(),
    )(dkp, dkp, dkp, dvp, dvp, dvp)


def _in_bwd_tail(dzb, w_ref, x_ref, mod_ref, k, dres_ref, dx_ref, h_ref, sums_ref):
    xv = x_ref[...]
    shift, scale, _ = _mod(mod_ref, k)
    h_ref[...] = (xv * (1.0 + scale) + shift).astype(BF16)
    dh = _nn(dzb, w_ref[...])
    dx_ref[...] = dres_ref[...] + dh * (1.0 + scale)
    sums_ref[0:1, :] += _colsum(dh)
    sums_ref[1:2, :] += _colsum(dh * xv)


def attn_in_bwd(dq, dk, dv, rope, x, mods, k, win_t, dres):
    s = x.shape[0]
    tm = _blk(s, TM_MM)

    def body(dq_ref, dk_ref, dv_ref, c_ref, s1_ref, s2_ref, x_ref, mod_ref, w_ref, dres_ref,
             dx_ref, dz_ref, h_ref, sums_ref):
        @pl.when(pl.program_id(0) == 0)
        def _():
            sums_ref[...] = jnp.zeros(sums_ref.shape, F32)

        cos, s1, s2 = c_ref[...], s1_ref[...], s2_ref[...]
        for hh in range(N_Q + N_KV):
            src = dq_ref[:, HEAD * hh:HEAD * (hh + 1)] if hh < N_Q else dk_ref[:, HEAD * (hh - N_Q):HEAD * (hh - N_Q + 1)]
            dz_ref[:, HEAD * hh:HEAD * (hh + 1)] = _rope_bwd(src, cos, s1, s2).astype(BF16)
        dz_ref[:, HEAD * (N_Q + N_KV):] = dv_ref[...].astype(BF16)
        _in_bwd_tail(dz_ref[...], w_ref, x_ref, mod_ref, k, dres_ref, dx_ref, h_ref, sums_ref)

    return pl.pallas_call(
        body, name="attn_in_bwd", grid=(s // tm,),
        out_shape=[jax.ShapeDtypeStruct((s, D_MODEL), F32), jax.ShapeDtypeStruct((s, D_QKV), BF16),
                   jax.ShapeDtypeStruct((s, D_MODEL), BF16), jax.ShapeDtypeStruct((8, D_MODEL), F32)],
        in_specs=[_row(tm, N_Q * HEAD), _row(tm, N_KV * HEAD), _row(tm, N_KV * HEAD),
                  _row(tm, HEAD), _row(tm, HEAD), _row(tm, HEAD), _row(tm, D_MODEL),
                  _res(mods.shape), _res(win_t.shape), _row(tm, D_MODEL)],
        out_specs=[_row(tm, D_MODEL), _row(tm, D_QKV), _row(tm, D_MODEL), _res((8, D_MODEL))],
        compiler_params=_params(),
    )(dq, dk, dv, *rope, x, mods, win_t, dres)


def rnn_in_bwd(dxr, dgt, x, mods, k, win_t, dres):
    s = x.shape[0]
    tm = _blk(s, TM_MM)

    def body(dxr_ref, dgt_ref, x_ref, mod_ref, w_ref, dres_ref, dx_ref, dz_ref, h_ref, sums_ref):
        @pl.when(pl.program_id(0) == 0)
        def _():
            sums_ref[...] = jnp.zeros(sums_ref.shape, F32)

        dz_ref[:, 0:D_RNN] = dxr_ref[...].astype(BF16)
        dz_ref[:, D_RNN:2 * D_RNN] = dgt_ref[...].astype(BF16)
        _in_bwd_tail(dz_ref[...], w_ref, x_ref, mod_ref, k, dres_ref, dx_ref, h_ref, sums_ref)

    return pl.pallas_call(
        body, name="rnn_in_bwd", grid=(s // tm,),
        out_shape=[jax.ShapeDtypeStruct((s, D_MODEL), F32), jax.ShapeDtypeStruct((s, 2 * D_RNN), BF16),
                   jax.ShapeDtypeStruct((s, D_MODEL), BF16), jax.ShapeDtypeStruct((8, D_MODEL), F32)],
        in_specs=[_row(tm, D_RNN), _row(tm, D_RNN), _row(tm, D_MODEL), _res(mods.shape), _res(win_t.shape),
                  _row(tm, D_MODEL)],
        out_specs=[_row(tm, D_MODEL), _row(tm, 2 * D_RNN), _row(tm, D_MODEL), _res((8, D_MODEL))],
        compiler_params=_params(),
    )(dxr, dgt, x, mods, win_t, dres)


def wgrad(a, b, name):
    s, m = a.shape
    n = b.shape[1]
    tm = next(t for t in (1024, 768, 512, 384, 256, 128) if m % t == 0)
    tk = _blk(s, TK_WG)
    nk = s // tk

    def body(a_ref, b_ref, o_ref, acc):
        kk = pl.program_id(1)

        @pl.when(kk == 0)
        def _():
            acc[...] = jnp.zeros(acc.shape, F32)

        acc[...] += _tn(a_ref[...], b_ref[...])

        @pl.when(kk == nk - 1)
        def _():
            o_ref[...] = acc[...].astype(BF16)

    out = pl.pallas_call(
        body, name=name, grid=(m // tm, nk),
        out_shape=jax.ShapeDtypeStruct((m, n), BF16),
        in_specs=[pl.BlockSpec((tk, tm), lambda i, kk: (kk, i)), pl.BlockSpec((tk, n), lambda i, kk: (kk, 0))],
        out_specs=pl.BlockSpec((tm, n), lambda i, kk: (i, 0)),
        scratch_shapes=[pltpu.VMEM((tm, n), F32)],
        compiler_params=_params(),
    )(a, b)
    return out.reshape(N_DEV, m // N_DEV, n)


def part_sum(parts, name):
    _, r, c = parts.shape
    tr = next(t for t in (256, 192, 128, 64, 32, 16, 8) if r % t == 0)

    def body(p_ref, o_ref):
        acc = p_ref[0].astype(F32)
        for j in range(1, N_DEV):
            acc = acc + p_ref[j].astype(F32)
        o_ref[...] = acc

    return pl.pallas_call(
        body, name=name, grid=(r // tr,),
        out_shape=jax.ShapeDtypeStruct((r, c), F32),
        in_specs=[pl.BlockSpec((N_DEV, tr, c), lambda i: (0, i, 0))],
        out_specs=pl.BlockSpec((tr, c), lambda i: (i, 0)),
        compiler_params=_params(),
    )(parts)


def adamw(w, g, m, v, name):
    shape = w.shape
    c = shape[-1]
    r = w.size // c
    w2, g2, m2, v2 = (t.reshape(r, c) for t in (w, g, m, v))
    tr = r if r * c <= 512 * 1024 else next(t for t in (512, 256, 128, 64, 32, 16, 8) if r % t == 0)

    def body(w_ref, g_ref, m_ref, v_ref, d_ref, nm_ref, nv_ref):
        gv = g_ref[...]
        nm = B1 * m_ref[...] + (1.0 - B1) * gv
        nv = B2 * v_ref[...] + (1.0 - B2) * (gv * gv)
        nm_ref[...] = nm
        nv_ref[...] = nv
        m_hat = nm / (1.0 - B1 ** STEP)
        v_hat = nv / (1.0 - B2 ** STEP)
        d_ref[...] = -LR * (m_hat / (jnp.sqrt(v_hat) + ADAM_EPS) + WD * w_ref[...])

    spec = pl.BlockSpec((tr, c), lambda i: (i, 0))
    outs = pl.pallas_call(
        body, name=name, grid=(r // tr,),
        out_shape=[jax.ShapeDtypeStruct((r, c), F32)] * 3,
        in_specs=[spec] * 4, out_specs=[spec] * 3,
        compiler_params=_params(),
    )(w2, g2, m2, v2)
    return tuple(o.reshape(shape) for o in outs)


def _rope_tables(s):
    pos = jnp.arange(s, dtype=F32)
    inv_freq = THETA ** (-jnp.arange(0, ROT, 2, dtype=F32) / ROT)
    ang = pos[:, None] * inv_freq[None, :]
    cos, sin = jnp.cos(ang), jnp.sin(ang)
    half = ROT // 2
    zeros = jnp.zeros((s, HEAD - ROT), F32)
    c = jnp.concatenate([cos, cos, jnp.ones((s, HEAD - ROT), F32)], axis=1)
    s1 = jnp.concatenate([jnp.zeros((s, half), F32), sin, zeros], axis=1)
    s2 = jnp.concatenate([-sin, jnp.zeros((s, half), F32), zeros], axis=1)
    return c, s1, s2


def _blockdiag(w):
    w4 = w.reshape(N_CG, 4, RB_W, RB_W)
    eye = jnp.eye(4, dtype=w.dtype)
    return jnp.einsum("gipq,ij->gipjq", w4, eye).reshape(N_CG, CG, CG)


def _diag_blocks(w):
    w5 = w.reshape(N_CG, 4, RB_W, 4, RB_W)
    eye = jnp.eye(4, dtype=w.dtype)
    return jnp.einsum("gipjq,ij->gipq", w5, eye).reshape(N_RB, RB_W, RB_W)


def _cols(full, per):
    lead = full.shape[:-1]
    t = full.reshape(lead + (N_DEV, per))
    return jnp.moveaxis(t, -2, 0).reshape(N_DEV, -1)


def kernel(x, c, ada_w, ada_b, ln_g, ln_b, attn_w_in, attn_w_out, attn_sinks, rnn_w_in, rnn_conv_w, rnn_conv_b, rnn_w_a, rnn_b_a, rnn_w_x, rnn_b_x, rnn_lam, rnn_w_out, mlp_w1, mlp_w2, loss_target, m_ada_w, m_ada_b, m_ln_g, m_ln_b, m_attn_w_in, m_attn_w_out, m_attn_sinks, m_rnn_w_in, m_rnn_conv_w, m_rnn_conv_b, m_rnn_w_a, m_rnn_b_a, m_rnn_w_x, m_rnn_b_x, m_rnn_lam, m_rnn_w_out, m_mlp_w1, m_mlp_w2, v_ada_w, v_ada_b, v_ln_g, v_ln_b, v_attn_w_in, v_attn_w_out, v_attn_sinks, v_rnn_w_in, v_rnn_conv_w, v_rnn_conv_b, v_rnn_w_a, v_rnn_b_a, v_rnn_w_x, v_rnn_b_x, v_rnn_lam, v_rnn_w_out, v_mlp_w1, v_mlp_w2):
    s = x.shape[1]
    x0 = x.reshape(s, D_MODEL)
    target = loss_target.reshape(s, D_MODEL)
    weights = dict(ada_w=ada_w, ada_b=ada_b, ln_g=ln_g, ln_b=ln_b, attn_w_in=attn_w_in, attn_w_out=attn_w_out,
                   attn_sinks=attn_sinks, rnn_w_in=rnn_w_in, rnn_conv_w=rnn_conv_w, rnn_conv_b=rnn_conv_b,
                   rnn_w_a=rnn_w_a, rnn_b_a=rnn_b_a, rnn_w_x=rnn_w_x, rnn_b_x=rnn_b_x, rnn_lam=rnn_lam,
                   rnn_w_out=rnn_w_out, mlp_w1=mlp_w1, mlp_w2=mlp_w2)
    moments_m = dict(ada_w=m_ada_w, ada_b=m_ada_b, ln_g=m_ln_g, ln_b=m_ln_b, attn_w_in=m_attn_w_in,
                     attn_w_out=m_attn_w_out, attn_sinks=m_attn_sinks, rnn_w_in=m_rnn_w_in,
                     rnn_conv_w=m_rnn_conv_w, rnn_conv_b=m_rnn_conv_b, rnn_w_a=m_rnn_w_a, rnn_b_a=m_rnn_b_a,
                     rnn_w_x=m_rnn_w_x, rnn_b_x=m_rnn_b_x, rnn_lam=m_rnn_lam, rnn_w_out=m_rnn_w_out,
                     mlp_w1=m_mlp_w1, mlp_w2=m_mlp_w2)
    moments_v = dict(ada_w=v_ada_w, ada_b=v_ada_b, ln_g=v_ln_g, ln_b=v_ln_b, attn_w_in=v_attn_w_in,
                     attn_w_out=v_attn_w_out, attn_sinks=v_attn_sinks, rnn_w_in=v_rnn_w_in,
                     rnn_conv_w=v_rnn_conv_w, rnn_conv_b=v_rnn_conv_b, rnn_w_a=v_rnn_w_a, rnn_b_a=v_rnn_b_a,
                     rnn_w_x=v_rnn_w_x, rnn_b_x=v_rnn_b_x, rnn_lam=v_rnn_lam, rnn_w_out=v_rnn_w_out,
                     mlp_w1=v_mlp_w1, mlp_w2=v_mlp_w2)
    names = list(weights)

    def t16(w):
        return w.T.astype(BF16)

    big = [t16(attn_w_in[0]), attn_w_out[0].astype(BF16), t16(rnn_w_in[0]), rnn_w_out[0].astype(BF16),
           t16(mlp_w1[0]), mlp_w2[0].astype(BF16), t16(mlp_w1[1]), mlp_w2[1].astype(BF16)]
    small_local = jnp.concatenate([
        ln_g.reshape(-1), ln_b.reshape(-1), rnn_conv_w.reshape(-1), rnn_conv_b.reshape(-1),
        rnn_b_a.reshape(-1), rnn_b_x.reshape(-1), rnn_lam.reshape(-1)])
    small_local = jnp.pad(small_local, (0, 4096 - small_local.shape[0])).reshape(32, 128)
    gathered = all_gather(big + [small_local], "weight_gather")
    win_t, wout, rin_t, rout, w1t_0, w2_0, w1t_1, w2_1 = (
        g.reshape(N_DEV * g.shape[1], D_MODEL) for g in gathered[:8])
    sm = gathered[8].reshape(N_DEV, 4096)

    def full_vec(off, rows, per):
        piece = sm[:, off:off + rows * per].reshape(N_DEV, rows, per)
        return jnp.moveaxis(piece, 0, 1).reshape(rows, N_DEV * per)

    lng_f, lnb_f = full_vec(0, 4, 128), full_vec(512, 4, 128)
    cw_f, cb_f = full_vec(1024, 4, 192), full_vec(1792, 1, 192)
    ba_f, bx_f, lam_f = full_vec(1984, 2, 192), full_vec(2368, 2, 192), full_vec(2752, 2, 192)
    wa_bd = [_blockdiag(rnn_w_a[0, d]).astype(BF16) for d in range(2)]
    wx_bd = [_blockdiag(rnn_w_x[0, d]).astype(BF16) for d in range(2)]

    c_all, modr = ada_modulation(jnp.broadcast_to(c, (8, D_MODEL)), ada_w.reshape(4, D_MODEL, CG),
                                 ada_b.reshape(4, 1, CG))
    mods = modr.reshape(N_DEV, 4, 8, CG)[:, :, 0, :]
    mods = jnp.moveaxis(mods, 0, 1).reshape(4, 3, D_MODEL).reshape(12, D_MODEL)
    rope = _rope_tables(s)
    ln = lambda k: (lng_f[k:k + 1], lnb_f[k:k + 1])

    q, kk, v = attn_in_fwd(x0, mods, 0, win_t, rope)
    o = attn_fwd(q, kk, v, attn_sinks)
    x1, y0 = post_fwd(o, wout, x0, mods, 0, *ln(0))
    x2, y1 = mlp_fwd(x1, mods, 1, w1t_0, w2_0, *ln(1))
    xr, gt = rnn_in_fwd(x2, mods, 2, rin_t)
    xc = conv_fwd(xr, cw_f, cb_f)
    hf = lru_fwd(xc, wa_bd[0], wx_bd[0], ba_f[0:1], bx_f[0:1], lam_f[0:1], False)
    hb = lru_fwd(xc, wa_bd[1], wx_bd[1], ba_f[1:2], bx_f[1:2], lam_f[1:2], True)
    x3, y2, ypre = post_fwd(None, rout, x2, mods, 2, *ln(2), gate_act=(gt, hf, hb))
    x4, y3 = mlp_fwd(x3, mods, 3, w1t_1, w2_1, *ln(3))
    dx4, sq = loss_grad(x4, target)
    loss = lax.psum(0.5 * jnp.sum(sq) / D_MODEL, ("x", "y", "c"))

    dx3, da1, r1, h3, dy3, sums3 = mlp_bwd(dx4, x3, y3, mods, 3, w1t_1, w2_1, lng_f[3:4])
    g_w1t_1 = wgrad(da1, h3, "wgrad_w1_1")
    g_w2_1 = wgrad(r1, dy3, "wgrad_w2_1")
    dres2, dy2, sums2a, dhs, dgt = post_bwd(dx3, x2, y2, mods, 2, rout, lng_f[2:3], gate_act=(gt, hf, hb))
    g_rout = wgrad(ypre, dy2, "wgrad_rnn_out")
    dxc_f, dwa_f, dwx_f, dba_f, dbx_f, dlam_f = lru_bwd(xc, dhs, hf, wa_bd[0], wx_bd[0], ba_f[0:1], bx_f[0:1],
                                                        lam_f[0:1], False)
    dxc_b, dwa_b, dwx_b, dba_b, dbx_b, dlam_b = lru_bwd(xc, dhs, hb, wa_bd[1], wx_bd[1], ba_f[1:2], bx_f[1:2],
                                                        lam_f[1:2], True)
    dxr, dcw, dcb = conv_bwd(dxc_f, dxc_b, xr, cw_f)
    dx2, dzz, h2, sums2b = rnn_in_bwd(dxr, dgt, x2, mods, 2, rin_t, dres2)
    g_rin_t = wgrad(dzz, h2, "wgrad_rnn_in")
    dx1, da0, r0, h1, dy1, sums1 = mlp_bwd(dx2, x1, y1, mods, 1, w1t_0, w2_0, lng_f[1:2])
    g_w1t_0 = wgrad(da0, h1, "wgrad_w1_0")
    g_w2_0 = wgrad(r0, dy1, "wgrad_w2_0")
    dres0, dy0, sums0a, do = post_bwd(dx1, x0, y0, mods, 0, wout, lng_f[0:1])
    g_wout = wgrad(o, dy0, "wgrad_attn_out")
    dq, dkp, dvp, dsink = attn_bwd(q, kk, v, do, attn_sinks)
    dk, dv = kv_combine(dkp, dvp)
    dx0, dqkv, h0, sums0b = attn_in_bwd(dq, dk, dv, rope, x0, mods, 0, win_t, dres0)
    g_win_t = wgrad(dqkv, h0, "wgrad_attn_in")

    big_parts = all_to_all([g_win_t, g_wout, g_rin_t, g_rout, g_w1t_0, g_w2_0, g_w1t_1, g_w2_1], "grad_exchange")
    gsum = [part_sum(p, "part_sum_%d" % i) for i, p in enumerate(big_parts)]
    grads = {
        "attn_w_in": gsum[0].T[None], "attn_w_out": gsum[1][None],
        "rnn_w_in": gsum[2].T[None], "rnn_w_out": gsum[3][None],
        "mlp_w1": jnp.stack([gsum[4].T, gsum[6].T]), "mlp_w2": jnp.stack([gsum[5], gsum[7]]),
    }

    sums = [sums0a + sums0b, sums1, sums2a + sums2b, sums3]
    gmod = jnp.stack([t[0:3] for t in sums])
    gsend = jnp.moveaxis(gmod.reshape(4, N_DEV, CG), 1, 0)
    gsend = jnp.pad(gsend, ((0, 0), (0, 4), (0, 0)))
    c_t = c_all[:, 0, :].T
    g_ada_w, g_ada_b = ada_grads(gsend, c_t)
    grads["ada_w"] = g_ada_w.reshape(ada_w.shape)
    grads["ada_b"] = g_ada_b[0:4].reshape(ada_b.shape)

    d_wa = jnp.stack([_diag_blocks(dwa_f), _diag_blocks(dwa_b)])
    d_wx = jnp.stack([_diag_blocks(dwx_f), _diag_blocks(dwx_b)])
    nflat = d_wa.size // N_DEV
    tail = jnp.concatenate([
        _cols(dcw, 192), _cols(dcb, 192),
        _cols(jnp.concatenate([dba_f, dba_b]), 192), _cols(jnp.concatenate([dbx_f, dbx_b]), 192),
        _cols(jnp.concatenate([dlam_f, dlam_b]), 192),
        _cols(jnp.stack([t[3] for t in sums]), 128), _cols(jnp.stack([t[4] for t in sums]), 128),
        jnp.broadcast_to(dsink[:, 0:8], (N_DEV, 8))], axis=1)
    tail = jnp.pad(tail, ((0, 0), (0, 32 * 128 - tail.shape[1])))
    ssend = jnp.concatenate([d_wa.reshape(N_DEV, nflat), d_wx.reshape(N_DEV, nflat), tail], axis=1)
    rw = 2 * nflat // 128
    red, wag = small_reduce(ssend.reshape(N_DEV, rw + 32, 128), rw)
    wag = wag.reshape(N_DEV, 2 * nflat)
    grads["rnn_w_a"] = wag[:, :nflat].reshape(rnn_w_a.shape)
    grads["rnn_w_x"] = wag[:, nflat:].reshape(rnn_w_x.shape)
    tl = red[rw:].reshape(-1)
    grads["rnn_conv_w"] = tl[0:768].reshape(rnn_conv_w.shape)
    grads["rnn_conv_b"] = tl[768:960].reshape(rnn_conv_b.shape)
    grads["rnn_b_a"] = tl[960:1344].reshape(rnn_b_a.shape)
    grads["rnn_b_x"] = tl[1344:1728].reshape(rnn_b_x.shape)
    grads["rnn_lam"] = tl[1728:2112].reshape(rnn_lam.shape)
    grads["ln_g"] = tl[2112:2624].reshape(ln_g.shape)
    grads["ln_b"] = tl[2624:3136].reshape(ln_b.shape)
    grads["attn_sinks"] = tl[3136:3144].reshape(attn_sinks.shape)

    delta, new_m, new_v = {}, {}, {}
    for n in names:
        delta[n], new_m[n], new_v[n] = adamw(weights[n], grads[n], moments_m[n], moments_v[n], "adamw_" + n)
    return (loss, dx0.reshape(x.shape), *[grads[n] for n in names], *[delta[n] for n in names],
            *[new_m[n] for n in names], *[new_v[n] for n in names])
```

```python
import functools
import math

import jax
import jax.numpy as jnp
from jax import lax
from jax.experimental import pallas as pl
from jax.experimental.pallas import tpu as pltpu

F32, BF16 = jnp.float32, jnp.bfloat16
MESH = pl.DeviceIdType.MESH

D_MODEL = 1024
N_Q, N_KV, HEAD = 8, 2, 128
ROT, THETA = 32, 500000.0
QBLK = 128
D_QKV = (N_Q + 2 * N_KV) * HEAD
D_RNN, N_RB, RB_W = 1536, 16, 96
CG = 384
N_CG = D_RNN // CG
D_FF = 4096
FF_CHUNK = 1024
DEPTH = 2
ALPHA = (2.0 * DEPTH) ** 0.25
LN_EPS = 1e-5
LRU_C = 8.0
N_DEV = 8
LR, B1, B2, ADAM_EPS, WD, STEP = 0.001, 0.9, 0.999, 1e-8, 0.01, 10

VMEM_LIMIT = 56 * 1024 * 1024
TM_MM = 512
TM_MLP = 256
TT_RNN = 512
SB_RNN = 128
TK_WG = 1024


def _nn(a, b):
    return jnp.dot(a, b, preferred_element_type=F32)


def _nt(a, b):
    return lax.dot_general(a, b, (((1,), (1,)), ((), ())), preferred_element_type=F32)


def _tn(a, b):
    return lax.dot_general(a, b, (((0,), (0,)), ((), ())), preferred_element_type=F32)


def _blk(n, pref):
    t = min(n, pref)
    assert n % t == 0, (n, pref)
    return t


def _params(**kw):
    return pltpu.CompilerParams(vmem_limit_bytes=VMEM_LIMIT, **kw)


def _row(tm, w):
    return pl.BlockSpec((tm, w), lambda i: (i, 0))


def _res(shape):
    return pl.BlockSpec(shape, lambda i: (0,) * len(shape), pipeline_mode=pl.Buffered(1))


def _mod(mod_ref, k):
    return mod_ref[3 * k:3 * k + 1, :], mod_ref[3 * k + 1:3 * k + 2, :], mod_ref[3 * k + 2:3 * k + 3, :]


def _ln_stats(z):
    mu = jnp.mean(z, axis=-1, keepdims=True)
    zc = z - mu
    var = jnp.mean(zc * zc, axis=-1, keepdims=True)
    rstd = lax.rsqrt(var + LN_EPS)
    return zc * rstd, rstd


def _ln_bwd(dxo, xhat, rstd, g):
    dxh = dxo * g
    m1 = jnp.mean(dxh, axis=-1, keepdims=True)
    m2 = jnp.mean(dxh * xhat, axis=-1, keepdims=True)
    return rstd * (dxh - m1 - xhat * m2)


def _colsum(v):
    return jnp.sum(v, axis=0, keepdims=True)


def _sigmoid(v):
    return 1.0 / (1.0 + jnp.exp(-v))


def _gelu_parts(v):
    k = math.sqrt(2.0 / math.pi)
    u = k * (v + 0.044715 * v * v * v)
    t = jnp.tanh(u)
    g = 0.5 * v * (1.0 + t)
    dg = 0.5 * (1.0 + t) + 0.5 * v * (1.0 - t * t) * k * (1.0 + 3.0 * 0.044715 * v * v)
    return g, dg


def _me():
    return lax.axis_index("x"), lax.axis_index("y"), lax.axis_index("c")


def _idx(p):
    return 4 * p[0] + 2 * p[1] + p[2]


def _peers(me):
    x, y, c = me
    out = []
    for k in range(1, N_DEV):
        out.append((1 - x if k & 4 else x, 1 - y if k & 2 else y, 1 - c if k & 1 else c))
    return out


class _Gather:
    def __init__(self, srcs):
        self.srcs = list(srcs)
        n = len(self.srcs)
        self.out_shape = [jax.ShapeDtypeStruct((N_DEV,) + s.shape, s.dtype) for s in self.srcs]
        self.scratch = [pltpu.SemaphoreType.DMA((n, 7)), pltpu.SemaphoreType.DMA((n, 7)),
                        pltpu.SemaphoreType.DMA((n,))]

    @staticmethod
    def _places():
        x, y, c = me = _me()
        return me, (x, y, 1 - c), [(1 - x, y), (x, 1 - y), (1 - x, 1 - y)]

    @staticmethod
    def _copy(outs, sems, t, k, block, to, src=None):
        slot = outs[t].at[_idx(block)]
        return pltpu.make_async_remote_copy(
            src_ref=slot if src is None else src, dst_ref=slot, send_sem=sems[0].at[t, k],
            recv_sem=sems[1].at[t, k], device_id=to, device_id_type=MESH)

    def _firsts(self, ins, outs, sems):
        me, sibling, chips = self._places()
        out = []
        for t in range(len(ins)):
            out.append(self._copy(outs, sems, t, 0, me, sibling, src=ins[t]))
            out += [self._copy(outs, sems, t, 1 + j, me, (*chip, me[2]), src=ins[t]) for j, chip in enumerate(chips)]
        return out

    def _locals(self, ins, outs, sems):
        me = _me()
        return [pltpu.make_async_copy(ins[t], outs[t].at[_idx(me)], sems[2].at[t]) for t in range(len(ins))]

    def start(self, ins, outs, sems):
        for cp in self._locals(ins, outs, sems) + self._firsts(ins, outs, sems):
            cp.start()

    def mid(self, ins, outs, sems):
        me, sibling, chips = self._places()
        for j, chip in enumerate(chips):
            for t in range(len(ins)):
                self._copy(outs, sems, t, 1 + j, (*chip, me[2]), me).wait_recv()
                self._copy(outs, sems, t, 4 + j, (*chip, me[2]), sibling).start()

    def finish(self, ins, outs, sems):
        me, sibling, chips = self._places()
        for t in range(len(ins)):
            self._copy(outs, sems, t, 0, sibling, me).wait_recv()
            for j, chip in enumerate(chips):
                self._copy(outs, sems, t, 4 + j, (*chip, 1 - me[2]), me).wait_recv()
        for cp in self._firsts(ins, outs, sems):
            cp.wait_send()
        for j, chip in enumerate(chips):
            for t in range(len(ins)):
                self._copy(outs, sems, t, 4 + j, (*chip, me[2]), sibling).wait_send()
        for cp in self._locals(ins, outs, sems):
            cp.wait()


class _AllToAll:
    def __init__(self, srcs):
        self.srcs = list(srcs)
        n = len(self.srcs)
        self.out_shape = [jax.ShapeDtypeStruct(s.shape, s.dtype) for s in self.srcs]
        self.scratch = [pltpu.SemaphoreType.DMA((n, 7)), pltpu.SemaphoreType.DMA((n, 7)),
                        pltpu.SemaphoreType.DMA((n,))]

    def _copies(self, ins, outs, sems):
        me = _me()
        loc, rem = [], []
        for t in range(len(ins)):
            loc.append(pltpu.make_async_copy(ins[t].at[_idx(me)], outs[t].at[_idx(me)], sems[2].at[t]))
            for k, p in enumerate(_peers(me)):
                rem.append(pltpu.make_async_remote_copy(
                    src_ref=ins[t].at[_idx(p)], dst_ref=outs[t].at[_idx(me)], send_sem=sems[0].at[t, k],
                    recv_sem=sems[1].at[t, k], device_id=p, device_id_type=MESH))
        return loc, rem

    def start(self, ins, outs, sems):
        loc, rem = self._copies(ins, outs, sems)
        for cp in loc + rem:
            cp.start()

    def mid(self, ins, outs, sems):
        pass

    def finish(self, ins, outs, sems):
        me = _me()
        for t in range(len(ins)):
            for k, p in enumerate(_peers(me)):
                slot = outs[t].at[_idx(p)]
                pltpu.make_async_remote_copy(
                    src_ref=slot, dst_ref=slot, send_sem=sems[0].at[t, k], recv_sem=sems[1].at[t, k],
                    device_id=p, device_id_type=MESH).wait_recv()
        loc, rem = self._copies(ins, outs, sems)
        for cp in rem:
            cp.wait_send()
        for cp in loc:
            cp.wait()


def exchange(ex, name):
    n = len(ex.srcs)

    def body(*refs):
        ins, outs, sems = refs[:n], refs[n:2 * n], refs[2 * n:]
        ex.start(ins, outs, sems)
        ex.mid(ins, outs, sems)
        ex.finish(ins, outs, sems)

    any_spec = pl.BlockSpec(memory_space=pl.ANY)
    return pl.pallas_call(
        body, name=name, out_shape=ex.out_shape, in_specs=[any_spec] * n, out_specs=[any_spec] * n,
        scratch_shapes=ex.scratch,
    )(*ex.srcs)


def _call(body, *, name, grid, in_specs, out_specs, out_shape, args, scratch_shapes=(), rider=None):
    in_specs, out_specs, out_shape = list(in_specs), list(out_specs), list(out_shape)
    scratch_shapes = list(scratch_shapes)
    if rider is None:
        return pl.pallas_call(body, name=name, grid=grid, out_shape=out_shape, in_specs=in_specs,
                              out_specs=out_specs, scratch_shapes=scratch_shapes, compiler_params=_params())(*args)
    nci, nco, ncs, nr = len(in_specs), len(out_shape), len(scratch_shapes), len(rider.srcs)
    nsteps = math.prod(grid)
    mid = min((3 * nsteps) // 4, nsteps - 2)
    assert 0 < mid, (name, grid)

    def full(*refs):
        ci, ri = refs[:nci], refs[nci:nci + nr]
        co, ro = refs[nci + nr:nci + nr + nco], refs[nci + nr + nco:nci + 2 * nr + nco]
        cs, rs = refs[nci + 2 * nr + nco:nci + 2 * nr + nco + ncs], refs[nci + 2 * nr + nco + ncs:]
        step = pl.program_id(0)
        for d in range(1, len(grid)):
            step = step * grid[d] + pl.program_id(d)

        @pl.when(step == 0)
        def _():
            rider.start(ri, ro, rs)

        @pl.when(step == mid)
        def _():
            rider.mid(ri, ro, rs)

        body(*ci, *co, *cs)

        @pl.when(step == nsteps - 1)
        def _():
            rider.finish(ri, ro, rs)

    any_spec = pl.BlockSpec(memory_space=pl.ANY)
    return pl.pallas_call(
        full, name=name, grid=grid, out_shape=out_shape + rider.out_shape,
        in_specs=in_specs + [any_spec] * nr, out_specs=out_specs + [any_spec] * nr,
        scratch_shapes=scratch_shapes + rider.scratch, compiler_params=_params(),
    )(*args, *rider.srcs)


def _a2a_start(srcs, dsts, send_sems, recv_sems, local_sems, me, sem_base=0):
    peers = _peers(me)
    started = []
    for t in range(len(srcs)):
        loc = pltpu.make_async_copy(srcs[t].at[_idx(me)], dsts[t].at[_idx(me)], local_sems.at[sem_base + t])
        loc.start()
        started.append(("local", loc))
        for k, p in enumerate(peers):
            cp = pltpu.make_async_remote_copy(
                src_ref=srcs[t].at[_idx(p)], dst_ref=dsts[t].at[_idx(me)],
                send_sem=send_sems.at[sem_base + t, k], recv_sem=recv_sems.at[sem_base + t, k],
                device_id=p, device_id_type=MESH)
            cp.start()
            started.append(("remote", cp))
    return started


def _a2a_finish(started, dsts, send_sems, recv_sems, me, sem_base=0):
    peers = _peers(me)
    for t in range(len(dsts)):
        for k, p in enumerate(peers):
            slot = dsts[t].at[_idx(p)]
            pltpu.make_async_remote_copy(
                src_ref=slot, dst_ref=slot, send_sem=send_sems.at[sem_base + t, k],
                recv_sem=recv_sems.at[sem_base + t, k], device_id=p, device_id_type=MESH).wait_recv()
    for kind, cp in started:
        if kind == "local":
            cp.wait()
        else:
            cp.wait_send()


def ada_modulation(c8, ada_w, ada_b):
    def body(c_ref, w_ref, b_ref, call_ref, modr_ref, modp, send_sems, recv_sems, local_sems):
        me = _me()
        for j in range(N_DEV):
            modp[j] = jnp.zeros(modp.shape[1:], F32)
        peers = _peers(me)
        sends = []
        for k, p in enumerate(peers):
            cp = pltpu.make_async_remote_copy(
                src_ref=c_ref, dst_ref=call_ref.at[_idx(me)], send_sem=send_sems.at[0, k],
                recv_sem=recv_sems.at[0, k], device_id=p, device_id_type=MESH)
            cp.start()
            sends.append(cp)
        call_ref[_idx(me)] = c_ref[...]
        for k, p in enumerate(peers):
            slot = call_ref.at[_idx(p)]
            pltpu.make_async_remote_copy(
                src_ref=slot, dst_ref=slot, send_sem=send_sems.at[0, k], recv_sem=recv_sems.at[0, k],
                device_id=p, device_id_type=MESH).wait_recv()
        for cp in sends:
            cp.wait_send()
        cv = call_ref[...].reshape(N_DEV * 8, D_MODEL)
        s = (cv * _sigmoid(cv)).astype(BF16)
        for k in range(4):
            res = _nn(s, w_ref[k].astype(BF16)) + b_ref[k]
            for j in range(N_DEV):
                modp[j, 8 * k:8 * k + 8, :] = res[8 * j:8 * j + 8, :]
        started = _a2a_start([modp], [modr_ref], send_sems, recv_sems, local_sems, me, sem_base=1)
        _a2a_finish(started, [modr_ref], send_sems, recv_sems, me, sem_base=1)

    vm = pl.BlockSpec(memory_space=pltpu.VMEM)
    return pl.pallas_call(
        body, name="ada_modulation",
        out_shape=[jax.ShapeDtypeStruct((N_DEV, 8, D_MODEL), F32), jax.ShapeDtypeStruct((N_DEV, 32, CG), F32)],
        in_specs=[vm, vm, vm], out_specs=[vm, vm],
        scratch_shapes=[pltpu.VMEM((N_DEV, 32, CG), F32), pltpu.SemaphoreType.DMA((2, 7)),
                        pltpu.SemaphoreType.DMA((2, 7)), pltpu.SemaphoreType.DMA((2,))],
        compiler_params=_params(),
    )(c8, ada_w, ada_b)


def ada_grads(gsend, c_t):
    def body(g_ref, ct_ref, gw_ref, gb_ref, grecv, send_sems, recv_sems, local_sems):
        me = _me()
        started = _a2a_start([g_ref], [grecv], send_sems, recv_sems, local_sems, me)
        _a2a_finish(started, [grecv], send_sems, recv_sems, me)
        ct = ct_ref[...]
        st = (ct * _sigmoid(ct)).astype(BF16).astype(F32)
        gb = jnp.zeros((8, CG), F32)
        for b in range(N_DEV):
            gb = gb + grecv[b]
        gb_ref[...] = gb
        for k in range(4):
            acc = jnp.zeros((D_MODEL, CG), F32)
            for b in range(N_DEV):
                row = grecv[b, k:k + 1, :].astype(BF16).astype(F32)
                acc = acc + st[:, b:b + 1] * row
            gw_ref[k] = acc

    vm = pl.BlockSpec(memory_space=pltpu.VMEM)
    return pl.pallas_call(
        body, name="ada_grads",
        out_shape=[jax.ShapeDtypeStruct((4, D_MODEL, CG), F32), jax.ShapeDtypeStruct((8, CG), F32)],
        in_specs=[vm, vm], out_specs=[vm, vm],
        scratch_shapes=[pltpu.VMEM((N_DEV, 8, CG), F32), pltpu.SemaphoreType.DMA((1, 7)),
                        pltpu.SemaphoreType.DMA((1, 7)), pltpu.SemaphoreType.DMA((1,))],
        compiler_params=_params(),
    )(gsend, c_t)


def small_reduce(ssend, rw):
    r = ssend.shape[1]

    def body(s_ref, red_ref, wag_ref, recv, send_sems, recv_sems, local_sems):
        me = _me()
        started = _a2a_start([s_ref], [recv], send_sems, recv_sems, local_sems, me)
        _a2a_finish(started, [recv], send_sems, recv_sems, me)
        acc = recv[0]
        for j in range(1, N_DEV):
            acc = acc + recv[j]
        red_ref[...] = acc
        peers = _peers(me)
        top = red_ref.at[pl.ds(0, rw), :]
        sends = []
        for k, p in enumerate(peers):
            cp = pltpu.make_async_remote_copy(
                src_ref=top, dst_ref=wag_ref.at[_idx(me)], send_sem=send_sems.at[1, k],
                recv_sem=recv_sems.at[1, k], device_id=p, device_id_type=MESH)
            cp.start()
            sends.append(cp)
        wag_ref[_idx(me)] = acc[0:rw, :]
        for k, p in enumerate(peers):
            slot = wag_ref.at[_idx(p)]
            pltpu.make_async_remote_copy(
                src_ref=slot, dst_ref=slot, send_sem=send_sems.at[1, k], recv_sem=recv_sems.at[1, k],
                device_id=p, device_id_type=MESH).wait_recv()
        for cp in sends:
            cp.wait_send()

    vm = pl.BlockSpec(memory_space=pltpu.VMEM)
    return pl.pallas_call(
        body, name="small_reduce",
        out_shape=[jax.ShapeDtypeStruct((r, 128), F32), jax.ShapeDtypeStruct((N_DEV, rw, 128), F32)],
        in_specs=[vm], out_specs=[vm, vm],
        scratch_shapes=[pltpu.VMEM((N_DEV, r, 128), F32), pltpu.SemaphoreType.DMA((2, 7)),
                        pltpu.SemaphoreType.DMA((2, 7)), pltpu.SemaphoreType.DMA((1,))],
        compiler_params=_params(),
    )(ssend)


def _rope(t, cos, s1, s2):
    return t * cos + pltpu.roll(t, 16, 1) * s1 + pltpu.roll(t, HEAD - 16, 1) * s2


def _rope_bwd(d, cos, s1, s2):
    return d * cos + pltpu.roll(d * s1, HEAD - 16, 1) + pltpu.roll(d * s2, 16, 1)


def attn_in_fwd(x, mods, k, win_t, rope):
    s = x.shape[0]
    tm = _blk(s, TM_MM)

    def body(x_ref, mod_ref, w_ref, c_ref, s1_ref, s2_ref, q_ref, k_ref, v_ref):
        shift, scale, _ = _mod(mod_ref, k)
        h = (x_ref[...] * (1.0 + scale) + shift).astype(BF16)
        qkv = _nt(h, w_ref[...])
        cos, s1, s2 = c_ref[...], s1_ref[...], s2_ref[...]
        for hh in range(N_Q + N_KV):
            r = _rope(qkv[:, HEAD * hh:HEAD * (hh + 1)], cos, s1, s2).astype(BF16)
            if hh < N_Q:
                q_ref[:, HEAD * hh:HEAD * (hh + 1)] = r
            else:
                k_ref[:, HEAD * (hh - N_Q):HEAD * (hh - N_Q + 1)] = r
        v_ref[...] = qkv[:, HEAD * (N_Q + N_KV):].astype(BF16)

    return pl.pallas_call(
        body, name="attn_in_fwd", grid=(s // tm,),
        out_shape=[jax.ShapeDtypeStruct((s, N_Q * HEAD), BF16), jax.ShapeDtypeStruct((s, N_KV * HEAD), BF16),
                   jax.ShapeDtypeStruct((s, N_KV * HEAD), BF16)],
        in_specs=[_row(tm, D_MODEL), _res(mods.shape), _res(win_t.shape),
                  _row(tm, HEAD), _row(tm, HEAD), _row(tm, HEAD)],
        out_specs=[_row(tm, N_Q * HEAD), _row(tm, N_KV * HEAD), _row(tm, N_KV * HEAD)],
        compiler_params=_params(),
    )(x, mods, win_t, *rope)


def _kv_specs(nblk):
    w = N_KV * HEAD
    return [pl.BlockSpec((QBLK, w), lambda n: (jnp.maximum(n - 1, 0), 0)),
            pl.BlockSpec((QBLK, w), lambda n: (n, 0)),
            pl.BlockSpec((QBLK, w), lambda n: (jnp.minimum(n + 1, nblk - 1), 0))]


def _attn_mask(n, s):
    qi = lax.broadcasted_iota(jnp.int32, (QBLK, 3 * QBLK), 0)
    kj = lax.broadcasted_iota(jnp.int32, (QBLK, 3 * QBLK), 1)
    rel = kj - QBLK - qi
    kpos = kj + (n - 1) * QBLK
    return (jnp.abs(rel) <= QBLK) & (kpos >= 0) & (kpos < s)


def _attn_probs(qh, kh, valid, sink):
    sc = _nt(qh, kh) * (HEAD ** -0.5)
    sc = jnp.where(valid, sc, -1e30)
    m = jnp.maximum(jnp.max(sc, axis=-1, keepdims=True), sink)
    p = jnp.exp(sc - m)
    es = jnp.exp(sink - m)
    denom = jnp.sum(p, axis=-1, keepdims=True) + es
    return p / denom, es / denom


def attn_fwd(q, kk, v, sinks, rider=None):
    s = q.shape[0]
    nblk = s // QBLK

    def body(sink_ref, q_ref, kp, ko, kn, vp, vo, vn, o_ref):
        n = pl.program_id(0)
        kcat = jnp.concatenate([kp[...], ko[...], kn[...]], axis=0)
        vcat = jnp.concatenate([vp[...], vo[...], vn[...]], axis=0)
        valid = _attn_mask(n, s)
        for hq in range(N_Q):
            kv = hq // (N_Q // N_KV)
            probs, _ = _attn_probs(q_ref[:, HEAD * hq:HEAD * (hq + 1)], kcat[:, HEAD * kv:HEAD * (kv + 1)],
                                   valid, sink_ref[0, hq])
            o_ref[:, HEAD * hq:HEAD * (hq + 1)] = _nn(probs.astype(BF16), vcat[:, HEAD * kv:HEAD * (kv + 1)]).astype(BF16)

    return _call(
        body, name="attn_fwd", grid=(nblk,),
        out_shape=[jax.ShapeDtypeStruct((s, N_Q * HEAD), BF16)],
        in_specs=[pl.BlockSpec(memory_space=pltpu.SMEM), pl.BlockSpec((QBLK, N_Q * HEAD), lambda n: (n, 0))]
        + _kv_specs(nblk) + _kv_specs(nblk),
        out_specs=[pl.BlockSpec((QBLK, N_Q * HEAD), lambda n: (n, 0))],
        args=(sinks, q, kk, kk, kk, v, v, v), rider=rider)


def post_fwd(ypre, w, x, mods, k, lng, lnb, gate_act=None):
    s = x.shape[0]
    tm = _blk(s, TM_MM)
    kdim = w.shape[0]
    rnn = gate_act is not None

    def body(*refs):
        if rnn:
            gt_ref, hf_ref, hb_ref, w_ref, x_ref, mod_ref, g_ref, b_ref, xo_ref, y_ref, yp_ref = refs
            act, _ = _gelu_parts(gt_ref[...])
            yp = ((hf_ref[...] + hb_ref[...]) * act).astype(BF16)
            yp_ref[...] = yp
        else:
            yp_ref, w_ref, x_ref, mod_ref, g_ref, b_ref, xo_ref, y_ref = refs
            yp = yp_ref[...]
        _, _, gate = _mod(mod_ref, k)
        y = _nn(yp, w_ref[...])
        y_ref[...] = y
        xhat, _ = _ln_stats(ALPHA * x_ref[...] + (1.0 + gate) * y)
        xo_ref[...] = xhat * g_ref[...] + b_ref[...]

    act_in = list(gate_act) if rnn else [ypre]
    out_shape = [jax.ShapeDtypeStruct((s, D_MODEL), F32), jax.ShapeDtypeStruct((s, D_MODEL), F32)]
    out_specs = [_row(tm, D_MODEL), _row(tm, D_MODEL)]
    if rnn:
        out_shape.append(jax.ShapeDtypeStruct((s, kdim), BF16))
        out_specs.append(_row(tm, kdim))
    return pl.pallas_call(
        body, name="rnn_post_fwd" if rnn else "attn_post_fwd", grid=(s // tm,),
        out_shape=out_shape,
        in_specs=[_row(tm, kdim)] * len(act_in) + [_res(w.shape), _row(tm, D_MODEL), _res(mods.shape),
                                                    _res(lng.shape), _res(lnb.shape)],
        out_specs=out_specs,
        compiler_params=_params(),
    )(*act_in, w, x, mods, lng, lnb)


def mlp_fwd(x, mods, k, w1_t, w2, lng, lnb, rider=None):
    s = x.shape[0]
    tm = _blk(s, TM_MLP)

    def body(x_ref, mod_ref, w1_ref, w2_ref, g_ref, b_ref, xo_ref, y_ref):
        xv = x_ref[...]
        shift, scale, gate = _mod(mod_ref, k)
        h = (xv * (1.0 + scale) + shift).astype(BF16)
        y = jnp.zeros((tm, D_MODEL), F32)
        for c in range(D_FF // FF_CHUNK):
            rows = slice(FF_CHUNK * c, FF_CHUNK * (c + 1))
            a = jnp.maximum(_nt(h, w1_ref[rows, :]), 0.0)
            y = y + _nn((a * a).astype(BF16), w2_ref[rows, :])
        y_ref[...] = y
        xhat, _ = _ln_stats(ALPHA * xv + (1.0 + gate) * y)
        xo_ref[...] = xhat * g_ref[...] + b_ref[...]

    return _call(
        body, name="mlp_fwd", grid=(s // tm,),
        out_shape=[jax.ShapeDtypeStruct((s, D_MODEL), F32)] * 2,
        in_specs=[_row(tm, D_MODEL), _res(mods.shape), _res(w1_t.shape), _res(w2.shape),
                  _res(lng.shape), _res(lnb.shape)],
        out_specs=[_row(tm, D_MODEL)] * 2,
        args=(x, mods, w1_t, w2, lng, lnb), rider=rider)


def rnn_in_fwd(x, mods, k, win_t):
    s = x.shape[0]
    tm = _blk(s, TM_MM)

    def body(x_ref, mod_ref, w_ref, xr_ref, gt_ref):
        shift, scale, _ = _mod(mod_ref, k)
        h = (x_ref[...] * (1.0 + scale) + shift).astype(BF16)
        xr_ref[...] = _nt(h, w_ref[0:D_RNN, :])
        gt_ref[...] = _nt(h, w_ref[D_RNN:2 * D_RNN, :])

    return pl.pallas_call(
        body, name="rnn_in_fwd", grid=(s // tm,),
        out_shape=[jax.ShapeDtypeStruct((s, D_RNN), F32)] * 2,
        in_specs=[_row(tm, D_MODEL), _res(mods.shape), _res(win_t.shape)],
        out_specs=[_row(tm, D_RNN)] * 2,
        compiler_params=_params(),
    )(x, mods, win_t)


def _shift_rows(v, k, row):
    n = v.shape[0]
    r = pltpu.roll(v, k % n, 0)
    keep = (row >= k) if k > 0 else (row < n + k)
    return jnp.where(keep, r, 0.0)


def conv_fwd(xr, cw, cb):
    s = xr.shape[0]

    def body(x_ref, w_ref, b_ref, o_ref):
        xv = x_ref[...]
        row = lax.broadcasted_iota(jnp.int32, xv.shape, 0)
        o_ref[...] = (b_ref[...] + w_ref[0:1, :] * _shift_rows(xv, 2, row) + w_ref[1:2, :] * _shift_rows(xv, 1, row)
                      + w_ref[2:3, :] * xv + w_ref[3:4, :] * _shift_rows(xv, -1, row))

    slab = pl.BlockSpec((s, 128), lambda j: (0, j))
    return pl.pallas_call(
        body, name="conv_fwd", grid=(D_RNN // 128,),
        out_shape=jax.ShapeDtypeStruct((s, D_RNN), F32),
        in_specs=[slab, pl.BlockSpec((4, 128), lambda j: (0, j)), pl.BlockSpec((1, 128), lambda j: (0, j))],
        out_specs=slab,
        compiler_params=_params(),
    )(xr, cw, cb)


def conv_bwd(da, db, xr, cw):
    s = xr.shape[0]

    def body(da_ref, db_ref, x_ref, w_ref, dx_ref, dw_ref, dbias_ref):
        d = da_ref[...] + db_ref[...]
        xv = x_ref[...]
        row = lax.broadcasted_iota(jnp.int32, xv.shape, 0)
        dx_ref[...] = (w_ref[0:1, :] * _shift_rows(d, -2, row) + w_ref[1:2, :] * _shift_rows(d, -1, row)
                       + w_ref[2:3, :] * d + w_ref[3:4, :] * _shift_rows(d, 1, row))
        dw_ref[0:1, :] = _colsum(d * _shift_rows(xv, 2, row))
        dw_ref[1:2, :] = _colsum(d * _shift_rows(xv, 1, row))
        dw_ref[2:3, :] = _colsum(d * xv)
        dw_ref[3:4, :] = _colsum(d * _shift_rows(xv, -1, row))
        dbias_ref[...] = _colsum(d)

    slab = pl.BlockSpec((s, 128), lambda j: (0, j))
    return pl.pallas_call(
        body, name="conv_bwd", grid=(D_RNN // 128,),
        out_shape=[jax.ShapeDtypeStruct((s, D_RNN), F32), jax.ShapeDtypeStruct((4, D_RNN), F32),
                   jax.ShapeDtypeStruct((1, D_RNN), F32)],
        in_specs=[slab, slab, slab, pl.BlockSpec((4, 128), lambda j: (0, j))],
        out_specs=[slab, pl.BlockSpec((4, 128), lambda j: (0, j)), pl.BlockSpec((1, 128), lambda j: (0, j))],
        compiler_params=_params(),
    )(da, db, xr, cw)


def _softplus_neg(lam):
    z = -lam
    e = jnp.exp(-jnp.abs(z))
    u = 1.0 + e
    log1p = jnp.where(u == 1.0, e, jnp.log(u) * e / jnp.where(u == 1.0, 1.0, u - 1.0))
    return jnp.maximum(z, 0.0) + log1p, 1.0 / (1.0 + jnp.exp(lam))


def _lru_gates(xv, wa_ref, wx_ref, ba_ref, bx_ref, lam_ref):
    xb = xv.astype(BF16)
    r = _sigmoid(_nn(xb, wa_ref[...]) + ba_ref[...])
    i = _sigmoid(_nn(xb, wx_ref[...]) + bx_ref[...])
    sp, sg = _softplus_neg(lam_ref[...])
    la = -LRU_C * r * sp
    a = jnp.exp(la)
    th = jnp.tanh(la)
    mult = jnp.sqrt(-2.0 * th / (1.0 - th))
    return xb, r, i, sp, sg, a, mult


def _scan(a, u, h0, reverse):
    n = a.shape[0]
    row = lax.broadcasted_iota(jnp.int32, a.shape, 0)
    sh = 1
    while sh < n:
        if reverse:
            keep = row < n - sh
            a_s = jnp.where(keep, pltpu.roll(a, n - sh, 0), 1.0)
            u_s = jnp.where(keep, pltpu.roll(u, n - sh, 0), 0.0)
        else:
            keep = row >= sh
            a_s = jnp.where(keep, pltpu.roll(a, sh, 0), 1.0)
            u_s = jnp.where(keep, pltpu.roll(u, sh, 0), 0.0)
        u = a * u_s + u
        a = a * a_s
        sh *= 2
    return u + a * h0


def _lru_specs(nt, tt, reverse):
    tmap = (lambda t: nt - 1 - t) if reverse else (lambda t: t)
    blk = pl.BlockSpec((tt, CG), lambda g, t: (tmap(t), g))
    wsp = pl.BlockSpec((None, CG, CG), lambda g, t: (g, 0, 0))
    vec = pl.BlockSpec((1, CG), lambda g, t: (0, g))
    return tmap, blk, wsp, vec


def lru_fwd(xc, wa, wx, ba, bx, lam, reverse):
    s = xc.shape[0]
    tt = _blk(s, TT_RNN)
    sb = _blk(tt, SB_RNN)
    nt = s // tt

    def body(x_ref, wa_ref, wx_ref, ba_ref, bx_ref, lam_ref, hs_ref, carry):
        @pl.when(pl.program_id(1) == 0)
        def _():
            carry[...] = jnp.zeros(carry.shape, F32)

        xv = x_ref[...]
        _, _, i, _, _, a, mult = _lru_gates(xv, wa_ref, wx_ref, ba_ref, bx_ref, lam_ref)
        u = mult * (i * xv)
        h0 = carry[0:1, :]
        order = range(tt // sb - 1, -1, -1) if reverse else range(tt // sb)
        for j in order:
            rows = slice(sb * j, sb * (j + 1))
            h = _scan(a[rows], u[rows], h0, reverse)
            hs_ref[rows, :] = h
            h0 = h[0:1, :] if reverse else h[sb - 1:sb, :]
        carry[0:1, :] = h0

    _, blk, wsp, vec = _lru_specs(nt, tt, reverse)
    return pl.pallas_call(
        body, name="lru_fwd_rev" if reverse else "lru_fwd", grid=(N_CG, nt),
        out_shape=jax.ShapeDtypeStruct((s, D_RNN), F32),
        in_specs=[blk, wsp, wsp, vec, vec, vec], out_specs=blk,
        scratch_shapes=[pltpu.VMEM((8, CG), F32)],
        compiler_params=_params(),
    )(xc, wa, wx, ba, bx, lam)


def lru_bwd(xc, dhs, hs, wa, wx, ba, bx, lam, reverse, rider=None):
    s = xc.shape[0]
    tt = _blk(s, TT_RNN)
    sb = _blk(tt, SB_RNN)
    nt = s // tt
    back = not reverse

    def body(x_ref, dh_ref, hs_ref, nb_ref, wa_ref, wx_ref, ba_ref, bx_ref, lam_ref,
             dx_ref, dwa_ref, dwx_ref, dba_ref, dbx_ref, dlam_ref, carry):
        t = pl.program_id(1)

        @pl.when(t == 0)
        def _():
            carry[...] = jnp.zeros(carry.shape, F32)
            dwa_ref[...] = jnp.zeros(dwa_ref.shape, F32)
            dwx_ref[...] = jnp.zeros(dwx_ref.shape, F32)
            dba_ref[...] = jnp.zeros(dba_ref.shape, F32)
            dbx_ref[...] = jnp.zeros(dbx_ref.shape, F32)
            dlam_ref[...] = jnp.zeros(dlam_ref.shape, F32)

        xv = x_ref[...]
        xb, r, i, sp, sg, a, mult = _lru_gates(xv, wa_ref, wx_ref, ba_ref, bx_ref, lam_ref)
        row = lax.broadcasted_iota(jnp.int32, xv.shape, 0)
        hsv = hs_ref[...]
        inner = t < nt - 1
        if reverse:
            edge = jnp.where(inner, nb_ref[0:1, :], 0.0)
            hprev = jnp.where(row == tt - 1, edge, pltpu.roll(hsv, tt - 1, 0))
            a_next = jnp.where(row == 0, carry[1:2, :], pltpu.roll(a, 1, 0))
        else:
            edge = jnp.where(inner, nb_ref[7:8, :], 0.0)
            hprev = jnp.where(row == 0, edge, pltpu.roll(hsv, 1, 0))
            a_next = jnp.where(row == tt - 1, carry[1:2, :], pltpu.roll(a, tt - 1, 0))
        dhv = dh_ref[...]
        g0 = carry[0:1, :]
        parts = [None] * (tt // sb)
        order = range(tt // sb - 1, -1, -1) if back else range(tt // sb)
        for j in order:
            rows = slice(sb * j, sb * (j + 1))
            gj = _scan(a_next[rows], dhv[rows], g0, back)
            parts[j] = gj
            g0 = gj[0:1, :] if back else gj[sb - 1:sb, :]
        g = jnp.concatenate(parts, axis=0) if len(parts) > 1 else parts[0]
        carry[0:1, :] = g0
        carry[1:2, :] = a[0:1, :] if back else a[tt - 1:tt, :]

        da = g * hprev
        dmult = g * (i * xv)
        di = g * mult * xv
        dla = da * a - dmult * (a * a) / mult
        dpa = (dla * (-LRU_C * sp)) * r * (1.0 - r)
        dpx = di * i * (1.0 - i)
        dlam_ref[...] += _colsum(dla * (LRU_C * r * sg))
        dba_ref[...] += _colsum(dpa)
        dbx_ref[...] += _colsum(dpx)
        dpab, dpxb = dpa.astype(BF16), dpx.astype(BF16)
        dx_ref[...] = g * mult * i + _nt(dpab, wa_ref[...]) + _nt(dpxb, wx_ref[...])
        dwa_ref[...] += _tn(xb, dpab)
        dwx_ref[...] += _tn(xb, dpxb)

    tmap, blk, wsp, vec = _lru_specs(nt, tt, back)
    per8 = tt // 8
    if reverse:
        nb = pl.BlockSpec((8, CG), lambda g, t: (jnp.minimum((tmap(t) + 1) * per8, s // 8 - 1), g))
    else:
        nb = pl.BlockSpec((8, CG), lambda g, t: (jnp.maximum(tmap(t) * per8 - 1, 0), g))
    return _call(
        body, name="lru_bwd_rev" if reverse else "lru_bwd", grid=(N_CG, nt),
        out_shape=[jax.ShapeDtypeStruct((s, D_RNN), F32), jax.ShapeDtypeStruct((N_CG, CG, CG), F32),
                   jax.ShapeDtypeStruct((N_CG, CG, CG), F32)] + [jax.ShapeDtypeStruct((1, D_RNN), F32)] * 3,
        in_specs=[blk, blk, blk, nb, wsp, wsp, vec, vec, vec],
        out_specs=[blk, wsp, wsp, vec, vec, vec],
        scratch_shapes=[pltpu.VMEM((8, CG), F32)],
        args=(xc, dhs, hs, hs, wa, wx, ba, bx, lam), rider=rider)


def loss_grad(xo, target):
    s = xo.shape[0]
    tm = _blk(s, TM_MM)

    def body(x_ref, t_ref, d_ref, sq_ref):
        @pl.when(pl.program_id(0) == 0)
        def _():
            sq_ref[...] = jnp.zeros(sq_ref.shape, F32)

        e = x_ref[...] - t_ref[...]
        d_ref[...] = e * (1.0 / D_MODEL)
        sq_ref[...] += _colsum(e * e)

    return pl.pallas_call(
        body, name="loss_grad", grid=(s // tm,),
        out_shape=[jax.ShapeDtypeStruct((s, D_MODEL), F32), jax.ShapeDtypeStruct((1, D_MODEL), F32)],
        in_specs=[_row(tm, D_MODEL)] * 2, out_specs=[_row(tm, D_MODEL), _res((1, D_MODEL))],
        compiler_params=_params(),
    )(xo, target)


def _ln_part_bwd(dxo, x, y, gate, g, sums_ref):
    xhat, rstd = _ln_stats(ALPHA * x + (1.0 + gate) * y)
    dz = _ln_bwd(dxo, xhat, rstd, g)
    sums_ref[2:3, :] += _colsum(dz * y)
    sums_ref[3:4, :] += _colsum(dxo * xhat)
    sums_ref[4:5, :] += _colsum(dxo)
    return dz


def mlp_bwd(dxo, x, y, mods, k, w1_t, w2, lng, rider=None):
    s = x.shape[0]
    tm = _blk(s, TM_MLP)

    def body(d_ref, x_ref, y_ref, mod_ref, w1_ref, w2_ref, g_ref,
             dx_ref, da_ref, r_ref, h_ref, dy_ref, sums_ref):
        @pl.when(pl.program_id(0) == 0)
        def _():
            sums_ref[...] = jnp.zeros(sums_ref.shape, F32)

        xv = x_ref[...]
        shift, scale, gate = _mod(mod_ref, k)
        dz = _ln_part_bwd(d_ref[...], xv, y_ref[...], gate, g_ref[...], sums_ref)
        dyb = (dz * (1.0 + gate)).astype(BF16)
        dy_ref[...] = dyb
        h = (xv * (1.0 + scale) + shift).astype(BF16)
        h_ref[...] = h
        dh = jnp.zeros((tm, D_MODEL), F32)
        for c in range(D_FF // FF_CHUNK):
            rows = slice(FF_CHUNK * c, FF_CHUNK * (c + 1))
            a = jnp.maximum(_nt(h, w1_ref[rows, :]), 0.0)
            r_ref[:, rows] = (a * a).astype(BF16)
            da = (_nt(dyb, w2_ref[rows, :]) * (2.0 * a)).astype(BF16)
            da_ref[:, rows] = da
            dh = dh + _nn(da, w1_ref[rows, :])
        dx_ref[...] = ALPHA * dz + dh * (1.0 + scale)
        sums_ref[0:1, :] += _colsum(dh)
        sums_ref[1:2, :] += _colsum(dh * xv)

    return _call(
        body, name="mlp_bwd", grid=(s // tm,),
        out_shape=[jax.ShapeDtypeStruct((s, D_MODEL), F32), jax.ShapeDtypeStruct((s, D_FF), BF16),
                   jax.ShapeDtypeStruct((s, D_FF), BF16), jax.ShapeDtypeStruct((s, D_MODEL), BF16),
                   jax.ShapeDtypeStruct((s, D_MODEL), BF16), jax.ShapeDtypeStruct((8, D_MODEL), F32)],
        in_specs=[_row(tm, D_MODEL)] * 3 + [_res(mods.shape), _res(w1_t.shape), _res(w2.shape), _res(lng.shape)],
        out_specs=[_row(tm, D_MODEL), _row(tm, D_FF), _row(tm, D_FF), _row(tm, D_MODEL), _row(tm, D_MODEL),
                   _res((8, D_MODEL))],
        args=(dxo, x, y, mods, w1_t, w2, lng), rider=rider)


def post_bwd(dxo, x, y, mods, k, w, lng, gate_act=None, rider=None):
    s = x.shape[0]
    tm = _blk(s, TM_MM)
    kdim = w.shape[0]
    rnn = gate_act is not None

    def body(*refs):
        if rnn:
            (d_ref, x_ref, y_ref, mod_ref, w_ref, g_ref, gt_ref, hf_ref, hb_ref,
             dres_ref, dy_ref, sums_ref, dhs_ref, dgt_ref) = refs
        else:
            d_ref, x_ref, y_ref, mod_ref, w_ref, g_ref, dres_ref, dy_ref, sums_ref, dyp_ref = refs

        @pl.when(pl.program_id(0) == 0)
        def _():
            sums_ref[...] = jnp.zeros(sums_ref.shape, F32)

        _, _, gate = _mod(mod_ref, k)
        dz = _ln_part_bwd(d_ref[...], x_ref[...], y_ref[...], gate, g_ref[...], sums_ref)
        dres_ref[...] = ALPHA * dz
        dyb = (dz * (1.0 + gate)).astype(BF16)
        dy_ref[...] = dyb
        dyp = _nt(dyb, w_ref[...])
        if rnn:
            act, dact = _gelu_parts(gt_ref[...])
            dhs_ref[...] = dyp * act
            dgt_ref[...] = dyp * (hf_ref[...] + hb_ref[...]) * dact
        else:
            dyp_ref[...] = dyp.astype(BF16)

    ins = [dxo, x, y, mods, w, lng] + (list(gate_act) if rnn else [])
    in_specs = [_row(tm, D_MODEL)] * 3 + [_res(mods.shape), _res(w.shape), _res(lng.shape)]
    out_shape = [jax.ShapeDtypeStruct((s, D_MODEL), F32), jax.ShapeDtypeStruct((s, D_MODEL), BF16),
                 jax.ShapeDtypeStruct((8, D_MODEL), F32)]
    out_specs = [_row(tm, D_MODEL), _row(tm, D_MODEL), _res((8, D_MODEL))]
    if rnn:
        in_specs += [_row(tm, kdim)] * 3
        out_shape += [jax.ShapeDtypeStruct((s, kdim), F32)] * 2
        out_specs += [_row(tm, kdim)] * 2
    else:
        out_shape.append(jax.ShapeDtypeStruct((s, kdim), BF16))
        out_specs.append(_row(tm, kdim))
    return _call(
        body, name="rnn_post_bwd" if rnn else "attn_post_bwd", grid=(s // tm,),
        out_shape=out_shape, in_specs=in_specs, out_specs=out_specs, args=ins, rider=rider)


def attn_bwd(q, kk, v, do, sinks, rider=None):
    s = q.shape[0]
    nblk = s // QBLK
    scale = HEAD ** -0.5

    def body(sink_ref, q_ref, do_ref, kp, ko, kn, vp, vo, vn, dq_ref, dkp_ref, dvp_ref, ds_ref):
        n = pl.program_id(0)

        @pl.when(n == 0)
        def _():
            ds_ref[...] = jnp.zeros(ds_ref.shape, F32)

        kcat = jnp.concatenate([kp[...], ko[...], kn[...]], axis=0)
        vcat = jnp.concatenate([vp[...], vo[...], vn[...]], axis=0)
        valid = _attn_mask(n, s)
        lane = lax.broadcasted_iota(jnp.int32, (1, 128), 1)
        dsink = jnp.zeros((1, 128), F32)
        dk = [jnp.zeros((3 * QBLK, HEAD), F32) for _ in range(N_KV)]
        dv = [jnp.zeros((3 * QBLK, HEAD), F32) for _ in range(N_KV)]
        for hq in range(N_Q):
            kv = hq // (N_Q // N_KV)
            cols = slice(HEAD * hq, HEAD * (hq + 1))
            qh, doh = q_ref[:, cols], do_ref[:, cols]
            kh, vh = kcat[:, HEAD * kv:HEAD * (kv + 1)], vcat[:, HEAD * kv:HEAD * (kv + 1)]
            probs, psink = _attn_probs(qh, kh, valid, sink_ref[0, hq])
            dprobs = _nt(doh, vh)
            dv[kv] = dv[kv] + _tn(probs.astype(BF16), doh)
            rowdot = jnp.sum(probs * dprobs, axis=-1, keepdims=True)
            dsb = (probs * (dprobs - rowdot) * scale).astype(BF16)
            dq_ref[:, cols] = _nn(dsb, kh)
            dk[kv] = dk[kv] + _tn(dsb, qh)
            dsink = dsink + jnp.where(lane == hq, _colsum(-psink * rowdot), 0.0)
        for kv in range(N_KV):
            dkp_ref[:, HEAD * kv:HEAD * (kv + 1)] = dk[kv]
            dvp_ref[:, HEAD * kv:HEAD * (kv + 1)] = dv[kv]
        ds_ref[...] += dsink

    qspec = pl.BlockSpec((QBLK, N_Q * HEAD), lambda n: (n, 0))
    pspec = pl.BlockSpec((None, 3 * QBLK, N_KV * HEAD), lambda n: (n, 0, 0))
    return _call(
        body, name="attn_bwd", grid=(nblk,),
        out_shape=[jax.ShapeDtypeStruct((s, N_Q * HEAD), F32),
                   jax.ShapeDtypeStruct((nblk, 3 * QBLK, N_KV * HEAD), F32),
                   jax.ShapeDtypeStruct((nblk, 3 * QBLK, N_KV * HEAD), F32),
                   jax.ShapeDtypeStruct((1, 128), F32)],
        in_specs=[pl.BlockSpec(memory_space=pltpu.SMEM), qspec, qspec] + _kv_specs(nblk) + _kv_specs(nblk),
        out_specs=[qspec, pspec, pspec, pl.BlockSpec((1, 128), lambda n: (0, 0))],
        args=(sinks, q, do, kk, kk, kk, v, v, v), rider=rider)


def kv_combine(dkp, dvp, rider=None):
    nblk = dkp.shape[0]
    w = N_KV * HEAD

    def body(kp, ko, kn, vp, vo, vn, dk_ref, dv_ref):
        n = pl.program_id(0)
        dk_ref[...] = jnp.where(n > 0, kp[...], 0.0) + ko[...] + jnp.where(n < nblk - 1, kn[...], 0.0)
        dv_ref[...] = jnp.where(n > 0, vp[...], 0.0) + vo[...] + jnp.where(n < nblk - 1, vn[...], 0.0)

    specs = [pl.BlockSpec((None, QBLK, w), lambda n: (jnp.maximum(n - 1, 0), 2, 0)),
             pl.BlockSpec((None, QBLK, w), lambda n: (n, 1, 0)),
             pl.BlockSpec((None, QBLK, w), lambda n: (jnp.minimum(n + 1, nblk - 1), 0, 0))]
    out = pl.BlockSpec((QBLK, w), lambda n: (n, 0))
    return _call(
        body, name="kv_combine", grid=(nblk,),
        out_shape=[jax.ShapeDtypeStruct((nblk * QBLK, w), F32)] * 2,
        in_specs=specs + specs, out_specs=[out, out],
        args=(dkp, dkp, dkp, dvp, dvp, dvp), rider=rider)


def _in_bwd_tail(dzb, w_ref, x_ref, mod_ref, k, dres_ref, dx_ref, h_ref, sums_ref):
    xv = x_ref[...]
    shift, scale, _ = _mod(mod_ref, k)
    h_ref[...] = (xv * (1.0 + scale) + shift).astype(BF16)
    dh = _nn(dzb, w_ref[...])
    dx_ref[...] = dres_ref[...] + dh * (1.0 + scale)
    sums_ref[0:1, :] += _colsum(dh)
    sums_ref[1:2, :] += _colsum(dh * xv)


def attn_in_bwd(dq, dk, dv, rope, x, mods, k, win_t, dres):
    s = x.shape[0]
    tm = _blk(s, TM_MM)

    def body(dq_ref, dk_ref, dv_ref, c_ref, s1_ref, s2_ref, x_ref, mod_ref, w_ref, dres_ref,
             dx_ref, dz_ref, h_ref, sums_ref):
        @pl.when(pl.program_id(0) == 0)
        def _():
            sums_ref[...] = jnp.zeros(sums_ref.shape, F32)

        cos, s1, s2 = c_ref[...], s1_ref[...], s2_ref[...]
        for hh in range(N_Q + N_KV):
            src = dq_ref[:, HEAD * hh:HEAD * (hh + 1)] if hh < N_Q else dk_ref[:, HEAD * (hh - N_Q):HEAD * (hh - N_Q + 1)]
            dz_ref[:, HEAD * hh:HEAD * (hh + 1)] = _rope_bwd(src, cos, s1, s2).astype(BF16)
        dz_ref[:, HEAD * (N_Q + N_KV):] = dv_ref[...].astype(BF16)
        _in_bwd_tail(dz_ref[...], w_ref, x_ref, mod_ref, k, dres_ref, dx_ref, h_ref, sums_ref)

    return pl.pallas_call(
        body, name="attn_in_bwd", grid=(s // tm,),
        out_shape=[jax.ShapeDtypeStruct((s, D_MODEL), F32), jax.ShapeDtypeStruct((s, D_QKV), BF16),
                   jax.ShapeDtypeStruct((s, D_MODEL), BF16), jax.ShapeDtypeStruct((8, D_MODEL), F32)],
        in_specs=[_row(tm, N_Q * HEAD), _row(tm, N_KV * HEAD), _row(tm, N_KV * HEAD),
                  _row(tm, HEAD), _row(tm, HEAD), _row(tm, HEAD), _row(tm, D_MODEL),
                  _res(mods.shape), _res(win_t.shape), _row(tm, D_MODEL)],
        out_specs=[_row(tm, D_MODEL), _row(tm, D_QKV), _row(tm, D_MODEL), _res((8, D_MODEL))],
        compiler_params=_params(),
    )(dq, dk, dv, *rope, x, mods, win_t, dres)


def rnn_in_bwd(dxr, dgt, x, mods, k, win_t, dres):
    s = x.shape[0]
    tm = _blk(s, TM_MM)

    def body(dxr_ref, dgt_ref, x_ref, mod_ref, w_ref, dres_ref, dx_ref, dz_ref, h_ref, sums_ref):
        @pl.when(pl.program_id(0) == 0)
        def _():
            sums_ref[...] = jnp.zeros(sums_ref.shape, F32)

        dz_ref[:, 0:D_RNN] = dxr_ref[...].astype(BF16)
        dz_ref[:, D_RNN:2 * D_RNN] = dgt_ref[...].astype(BF16)
        _in_bwd_tail(dz_ref[...], w_ref, x_ref, mod_ref, k, dres_ref, dx_ref, h_ref, sums_ref)

    return pl.pallas_call(
        body, name="rnn_in_bwd", grid=(s // tm,),
        out_shape=[jax.ShapeDtypeStruct((s, D_MODEL), F32), jax.ShapeDtypeStruct((s, 2 * D_RNN), BF16),
                   jax.ShapeDtypeStruct((s, D_MODEL), BF16), jax.ShapeDtypeStruct((8, D_MODEL), F32)],
        in_specs=[_row(tm, D_RNN), _row(tm, D_RNN), _row(tm, D_MODEL), _res(mods.shape), _res(win_t.shape),
                  _row(tm, D_MODEL)],
        out_specs=[_row(tm, D_MODEL), _row(tm, 2 * D_RNN), _row(tm, D_MODEL), _res((8, D_MODEL))],
        compiler_params=_params(),
    )(dxr, dgt, x, mods, win_t, dres)


def wgrad(a, b, name):
    s, m = a.shape
    n = b.shape[1]
    tm = next(t for t in (1024, 768, 512, 384, 256, 128) if m % t == 0)
    tk = _blk(s, TK_WG)
    nk = s // tk

    def body(a_ref, b_ref, o_ref, acc):
        kk = pl.program_id(1)

        @pl.when(kk == 0)
        def _():
            acc[...] = jnp.zeros(acc.shape, F32)

        acc[...] += _tn(a_ref[...], b_ref[...])

        @pl.when(kk == nk - 1)
        def _():
            o_ref[...] = acc[...].astype(BF16)

    out = pl.pallas_call(
        body, name=name, grid=(m // tm, nk),
        out_shape=jax.ShapeDtypeStruct((m, n), BF16),
        in_specs=[pl.BlockSpec((tk, tm), lambda i, kk: (kk, i)), pl.BlockSpec((tk, n), lambda i, kk: (kk, 0))],
        out_specs=pl.BlockSpec((tm, n), lambda i, kk: (i, 0)),
        scratch_shapes=[pltpu.VMEM((tm, n), F32)],
        compiler_params=_params(),
    )(a, b)
    return out.reshape(N_DEV, m // N_DEV, n)


def part_sum(parts, name):
    _, r, c = parts.shape
    tr = next(t for t in (256, 192, 128, 64, 32, 16, 8) if r % t == 0)

    def body(p_ref, o_ref):
        acc = p_ref[0].astype(F32)
        for j in range(1, N_DEV):
            acc = acc + p_ref[j].astype(F32)
        o_ref[...] = acc

    return pl.pallas_call(
        body, name=name, grid=(r // tr,),
        out_shape=jax.ShapeDtypeStruct((r, c), F32),
        in_specs=[pl.BlockSpec((N_DEV, tr, c), lambda i: (0, i, 0))],
        out_specs=pl.BlockSpec((tr, c), lambda i: (i, 0)),
        compiler_params=_params(),
    )(parts)


def adamw(w, g, m, v, name):
    shape = w.shape
    c = shape[-1]
    r = w.size // c
    w2, g2, m2, v2 = (t.reshape(r, c) for t in (w, g, m, v))
    tr = r if r * c <= 512 * 1024 else next(t for t in (512, 256, 128, 64, 32, 16, 8) if r % t == 0)

    def body(w_ref, g_ref, m_ref, v_ref, d_ref, nm_ref, nv_ref):
        gv = g_ref[...]
        nm = B1 * m_ref[...] + (1.0 - B1) * gv
        nv = B2 * v_ref[...] + (1.0 - B2) * (gv * gv)
        nm_ref[...] = nm
        nv_ref[...] = nv
        m_hat = nm / (1.0 - B1 ** STEP)
        v_hat = nv / (1.0 - B2 ** STEP)
        d_ref[...] = -LR * (m_hat / (jnp.sqrt(v_hat) + ADAM_EPS) + WD * w_ref[...])

    spec = pl.BlockSpec((tr, c), lambda i: (i, 0))
    outs = pl.pallas_call(
        body, name=name, grid=(r // tr,),
        out_shape=[jax.ShapeDtypeStruct((r, c), F32)] * 3,
        in_specs=[spec] * 4, out_specs=[spec] * 3,
        compiler_params=_params(),
    )(w2, g2, m2, v2)
    return tuple(o.reshape(shape) for o in outs)


def _rope_tables(s):
    pos = jnp.arange(s, dtype=F32)
    inv_freq = THETA ** (-jnp.arange(0, ROT, 2, dtype=F32) / ROT)
    ang = pos[:, None] * inv_freq[None, :]
    cos, sin = jnp.cos(ang), jnp.sin(ang)
    half = ROT // 2
    zeros = jnp.zeros((s, HEAD - ROT), F32)
    c = jnp.concatenate([cos, cos, jnp.ones((s, HEAD - ROT), F32)], axis=1)
    s1 = jnp.concatenate([jnp.zeros((s, half), F32), sin, zeros], axis=1)
    s2 = jnp.concatenate([-sin, jnp.zeros((s, half), F32), zeros], axis=1)
    return c, s1, s2


def _blockdiag(w):
    w4 = w.reshape(N_CG, 4, RB_W, RB_W)
    eye = jnp.eye(4, dtype=w.dtype)
    return jnp.einsum("gipq,ij->gipjq", w4, eye).reshape(N_CG, CG, CG)


def _diag_blocks(w):
    w5 = w.reshape(N_CG, 4, RB_W, 4, RB_W)
    eye = jnp.eye(4, dtype=w.dtype)
    return jnp.einsum("gipjq,ij->gipq", w5, eye).reshape(N_RB, RB_W, RB_W)


def _cols(full, per):
    lead = full.shape[:-1]
    t = full.reshape(lead + (N_DEV, per))
    return jnp.moveaxis(t, -2, 0).reshape(N_DEV, -1)


def kernel(x, c, ada_w, ada_b, ln_g, ln_b, attn_w_in, attn_w_out, attn_sinks, rnn_w_in, rnn_conv_w, rnn_conv_b, rnn_w_a, rnn_b_a, rnn_w_x, rnn_b_x, rnn_lam, rnn_w_out, mlp_w1, mlp_w2, loss_target, m_ada_w, m_ada_b, m_ln_g, m_ln_b, m_attn_w_in, m_attn_w_out, m_attn_sinks, m_rnn_w_in, m_rnn_conv_w, m_rnn_conv_b, m_rnn_w_a, m_rnn_b_a, m_rnn_w_x, m_rnn_b_x, m_rnn_lam, m_rnn_w_out, m_mlp_w1, m_mlp_w2, v_ada_w, v_ada_b, v_ln_g, v_ln_b, v_attn_w_in, v_attn_w_out, v_attn_sinks, v_rnn_w_in, v_rnn_conv_w, v_rnn_conv_b, v_rnn_w_a, v_rnn_b_a, v_rnn_w_x, v_rnn_b_x, v_rnn_lam, v_rnn_w_out, v_mlp_w1, v_mlp_w2):
    s = x.shape[1]
    x0 = x.reshape(s, D_MODEL)
    target = loss_target.reshape(s, D_MODEL)
    weights = dict(ada_w=ada_w, ada_b=ada_b, ln_g=ln_g, ln_b=ln_b, attn_w_in=attn_w_in, attn_w_out=attn_w_out,
                   attn_sinks=attn_sinks, rnn_w_in=rnn_w_in, rnn_conv_w=rnn_conv_w, rnn_conv_b=rnn_conv_b,
                   rnn_w_a=rnn_w_a, rnn_b_a=rnn_b_a, rnn_w_x=rnn_w_x, rnn_b_x=rnn_b_x, rnn_lam=rnn_lam,
                   rnn_w_out=rnn_w_out, mlp_w1=mlp_w1, mlp_w2=mlp_w2)
    moments_m = dict(ada_w=m_ada_w, ada_b=m_ada_b, ln_g=m_ln_g, ln_b=m_ln_b, attn_w_in=m_attn_w_in,
                     attn_w_out=m_attn_w_out, attn_sinks=m_attn_sinks, rnn_w_in=m_rnn_w_in,
                     rnn_conv_w=m_rnn_conv_w, rnn_conv_b=m_rnn_conv_b, rnn_w_a=m_rnn_w_a, rnn_b_a=m_rnn_b_a,
                     rnn_w_x=m_rnn_w_x, rnn_b_x=m_rnn_b_x, rnn_lam=m_rnn_lam, rnn_w_out=m_rnn_w_out,
                     mlp_w1=m_mlp_w1, mlp_w2=m_mlp_w2)
    moments_v = dict(ada_w=v_ada_w, ada_b=v_ada_b, ln_g=v_ln_g, ln_b=v_ln_b, attn_w_in=v_attn_w_in,
                     attn_w_out=v_attn_w_out, attn_sinks=v_attn_sinks, rnn_w_in=v_rnn_w_in,
                     rnn_conv_w=v_rnn_conv_w, rnn_conv_b=v_rnn_conv_b, rnn_w_a=v_rnn_w_a, rnn_b_a=v_rnn_b_a,
                     rnn_w_x=v_rnn_w_x, rnn_b_x=v_rnn_b_x, rnn_lam=v_rnn_lam, rnn_w_out=v_rnn_w_out,
                     mlp_w1=v_mlp_w1, mlp_w2=v_mlp_w2)
    names = list(weights)

    def t16(w):
        return w.T.astype(BF16)

    big = [t16(attn_w_in[0]), attn_w_out[0].astype(BF16), t16(rnn_w_in[0]), rnn_w_out[0].astype(BF16),
           t16(mlp_w1[0]), mlp_w2[0].astype(BF16), t16(mlp_w1[1]), mlp_w2[1].astype(BF16)]
    small_local = jnp.concatenate([
        ln_g.reshape(-1), ln_b.reshape(-1), rnn_conv_w.reshape(-1), rnn_conv_b.reshape(-1),
        rnn_b_a.reshape(-1), rnn_b_x.reshape(-1), rnn_lam.reshape(-1)])
    small_local = jnp.pad(small_local, (0, 4096 - small_local.shape[0])).reshape(32, 128)
    flat = lambda g: g.reshape(N_DEV * g.shape[1], D_MODEL)
    first = exchange(_Gather([big[0], big[1], small_local]), "weight_gather")
    win_t, wout = flat(first[0]), flat(first[1])
    sm = first[2].reshape(N_DEV, 4096)

    def full_vec(off, rows, per):
        piece = sm[:, off:off + rows * per].reshape(N_DEV, rows, per)
        return jnp.moveaxis(piece, 0, 1).reshape(rows, N_DEV * per)

    lng_f, lnb_f = full_vec(0, 4, 128), full_vec(512, 4, 128)
    cw_f, cb_f = full_vec(1024, 4, 192), full_vec(1792, 1, 192)
    ba_f, bx_f, lam_f = full_vec(1984, 2, 192), full_vec(2368, 2, 192), full_vec(2752, 2, 192)
    wa_bd = [_blockdiag(rnn_w_a[0, d]).astype(BF16) for d in range(2)]
    wx_bd = [_blockdiag(rnn_w_x[0, d]).astype(BF16) for d in range(2)]

    c_all, modr = ada_modulation(jnp.broadcast_to(c, (8, D_MODEL)), ada_w.reshape(4, D_MODEL, CG),
                                 ada_b.reshape(4, 1, CG))
    mods = modr.reshape(N_DEV, 4, 8, CG)[:, :, 0, :]
    mods = jnp.moveaxis(mods, 0, 1).reshape(4, 3, D_MODEL).reshape(12, D_MODEL)
    rope = _rope_tables(s)
    ln = lambda k: (lng_f[k:k + 1], lnb_f[k:k + 1])

    q, kk, v = attn_in_fwd(x0, mods, 0, win_t, rope)
    o, *got = attn_fwd(q, kk, v, attn_sinks, rider=_Gather([big[4], big[5], big[2], big[3]]))
    w1t_0, w2_0, rin_t, rout = (flat(g) for g in got)
    x1, y0 = post_fwd(o, wout, x0, mods, 0, *ln(0))
    x2, y1, *got = mlp_fwd(x1, mods, 1, w1t_0, w2_0, *ln(1), rider=_Gather([big[6], big[7]]))
    w1t_1, w2_1 = (flat(g) for g in got)
    xr, gt = rnn_in_fwd(x2, mods, 2, rin_t)
    xc = conv_fwd(xr, cw_f, cb_f)
    hf = lru_fwd(xc, wa_bd[0], wx_bd[0], ba_f[0:1], bx_f[0:1], lam_f[0:1], False)
    hb = lru_fwd(xc, wa_bd[1], wx_bd[1], ba_f[1:2], bx_f[1:2], lam_f[1:2], True)
    x3, y2, ypre = post_fwd(None, rout, x2, mods, 2, *ln(2), gate_act=(gt, hf, hb))
    x4, y3 = mlp_fwd(x3, mods, 3, w1t_1, w2_1, *ln(3))
    dx4, sq = loss_grad(x4, target)
    loss = lax.psum(0.5 * jnp.sum(sq) / D_MODEL, ("x", "y", "c"))

    dx3, da1, r1, h3, dy3, sums3 = mlp_bwd(dx4, x3, y3, mods, 3, w1t_1, w2_1, lng_f[3:4])
    g_w1t_1 = wgrad(da1, h3, "wgrad_w1_1")
    g_w2_1 = wgrad(r1, dy3, "wgrad_w2_1")
    dres2, dy2, sums2a, dhs, dgt, p_w1t_1 = post_bwd(dx3, x2, y2, mods, 2, rout, lng_f[2:3], gate_act=(gt, hf, hb),
                                                     rider=_AllToAll([g_w1t_1]))
    g_rout = wgrad(ypre, dy2, "wgrad_rnn_out")
    dxc_f, dwa_f, dwx_f, dba_f, dbx_f, dlam_f, p_w2_1, p_rout = lru_bwd(
        xc, dhs, hf, wa_bd[0], wx_bd[0], ba_f[0:1], bx_f[0:1], lam_f[0:1], False, rider=_AllToAll([g_w2_1, g_rout]))
    dxc_b, dwa_b, dwx_b, dba_b, dbx_b, dlam_b = lru_bwd(xc, dhs, hb, wa_bd[1], wx_bd[1], ba_f[1:2], bx_f[1:2],
                                                        lam_f[1:2], True)
    dxr, dcw, dcb = conv_bwd(dxc_f, dxc_b, xr, cw_f)
    dx2, dzz, h2, sums2b = rnn_in_bwd(dxr, dgt, x2, mods, 2, rin_t, dres2)
    g_rin_t = wgrad(dzz, h2, "wgrad_rnn_in")
    dx1, da0, r0, h1, dy1, sums1, p_rin_t = mlp_bwd(dx2, x1, y1, mods, 1, w1t_0, w2_0, lng_f[1:2],
                                                    rider=_AllToAll([g_rin_t]))
    g_w1t_0 = wgrad(da0, h1, "wgrad_w1_0")
    g_w2_0 = wgrad(r0, dy1, "wgrad_w2_0")
    dres0, dy0, sums0a, do = post_bwd(dx1, x0, y0, mods, 0, wout, lng_f[0:1])
    g_wout = wgrad(o, dy0, "wgrad_attn_out")
    dq, dkp, dvp, dsink, p_w1t_0, p_w2_0 = attn_bwd(q, kk, v, do, attn_sinks, rider=_AllToAll([g_w1t_0, g_w2_0]))
    dk, dv, p_wout = kv_combine(dkp, dvp, rider=_AllToAll([g_wout]))
    dx0, dqkv, h0, sums0b = attn_in_bwd(dq, dk, dv, rope, x0, mods, 0, win_t, dres0)
    g_win_t = wgrad(dqkv, h0, "wgrad_attn_in")
    p_win_t, = exchange(_AllToAll([g_win_t]), "grad_exchange")

    big_parts = [p_win_t, p_wout, p_rin_t, p_rout, p_w1t_0, p_w2_0, p_w1t_1, p_w2_1]
    gsum = [part_sum(p, "part_sum_%d" % i) for i, p in enumerate(big_parts)]
    grads = {
        "attn_w_in": gsum[0].T[None], "attn_w_out": gsum[1][None],
        "rnn_w_in": gsum[2].T[None], "rnn_w_out": gsum[3][None],
        "mlp_w1": jnp.stack([gsum[4].T, gsum[6].T]), "mlp_w2": jnp.stack([gsum[5], gsum[7]]),
    }

    sums = [sums0a + sums0b, sums1, sums2a + sums2b, sums3]
    gmod = jnp.stack([t[0:3] for t in sums])
    gsend = jnp.moveaxis(gmod.reshape(4, N_DEV, CG), 1, 0)
    gsend = jnp.pad(gsend, ((0, 0), (0, 4), (0, 0)))
    c_t = c_all[:, 0, :].T
    g_ada_w, g_ada_b = ada_grads(gsend, c_t)
    grads["ada_w"] = g_ada_w.reshape(ada_w.shape)
    grads["ada_b"] = g_ada_b[0:4].reshape(ada_b.shape)

    d_wa = jnp.stack([_diag_blocks(dwa_f), _diag_blocks(dwa_b)])
    d_wx = jnp.stack([_diag_blocks(dwx_f), _diag_blocks(dwx_b)])
    nflat = d_wa.size // N_DEV
    tail = jnp.concatenate([
        _cols(dcw, 192), _cols(dcb, 192),
        _cols(jnp.concatenate([dba_f, dba_b]), 192), _cols(jnp.concatenate([dbx_f, dbx_b]), 192),
        _cols(jnp.concatenate([dlam_f, dlam_b]), 192),
        _cols(jnp.stack([t[3] for t in sums]), 128), _cols(jnp.stack([t[4] for t in sums]), 128),
        jnp.broadcast_to(dsink[:, 0:8], (N_DEV, 8))], axis=1)
    tail = jnp.pad(tail, ((0, 0), (0, 32 * 128 - tail.shape[1])))
    ssend = jnp.concatenate([d_wa.reshape(N_DEV, nflat), d_wx.reshape(N_DEV, nflat), tail], axis=1)
    rw = 2 * nflat // 128
    red, wag = small_reduce(ssend.reshape(N_DEV, rw + 32, 128), rw)
    wag = wag.reshape(N_DEV, 2 * nflat)
    grads["rnn_w_a"] = wag[:, :nflat].reshape(rnn_w_a.shape)
    grads["rnn_w_x"] = wag[:, nflat:].reshape(rnn_w_x.shape)
    tl = red[rw:].reshape(-1)
    grads["rnn_conv_w"] = tl[0:768].reshape(rnn_conv_w.shape)
    grads["rnn_conv_b"] = tl[768:960].reshape(rnn_conv_b.shape)
    grads["rnn_b_a"] = tl[960:1344].reshape(rnn_b_a.shape)
    grads["rnn_b_x"] = tl[1344:1728].reshape(rnn_b_x.shape)
    grads["rnn_lam"] = tl[1728:2112].reshape(rnn_lam.shape)
    grads["ln_g"] = tl[2112:2624].reshape(ln_g.shape)
    grads["ln_b"] = tl[2624:3136].reshape(ln_b.shape)
    grads["attn_sinks"] = tl[3136:3144].reshape(attn_sinks.shape)

    delta, new_m, new_v = {}, {}, {}
    for n in names:
        delta[n], new_m[n], new_v[n] = adamw(weights[n], grads[n], moments_m[n], moments_v[n], "adamw_" + n)
    return (loss, dx0.reshape(x.shape), *[grads[n] for n in names], *[delta[n] for n in names],
            *[new_m[n] for n in names], *[new_v[n] for n in names])
```

```python
import functools
import math

import jax
import jax.numpy as jnp
from jax import lax
from jax.experimental import pallas as pl
from jax.experimental.pallas import tpu as pltpu

F32, BF16 = jnp.float32, jnp.bfloat16
MESH = pl.DeviceIdType.MESH

D_MODEL = 1024
N_Q, N_KV, HEAD = 8, 2, 128
ROT, THETA = 32, 500000.0
QBLK = 128
D_QKV = (N_Q + 2 * N_KV) * HEAD
D_RNN, N_RB, RB_W = 1536, 16, 96
CG = 384
N_CG = D_RNN // CG
D_FF = 4096
FF_CHUNK = 1024
DEPTH = 2
ALPHA = (2.0 * DEPTH) ** 0.25
LN_EPS = 1e-5
LRU_C = 8.0
N_DEV = 8
LR, B1, B2, ADAM_EPS, WD, STEP = 0.001, 0.9, 0.999, 1e-8, 0.01, 10

VMEM_LIMIT = 56 * 1024 * 1024
TM_MM = 512
TM_MLP = 256
TT_RNN = 512
SB_RNN = 128
TK_WG = 1024


def _nn(a, b):
    return jnp.dot(a, b, preferred_element_type=F32)


def _nt(a, b):
    return lax.dot_general(a, b, (((1,), (1,)), ((), ())), preferred_element_type=F32)


def _tn(a, b):
    return lax.dot_general(a, b, (((0,), (0,)), ((), ())), preferred_element_type=F32)


def _blk(n, pref):
    t = min(n, pref)
    assert n % t == 0, (n, pref)
    return t


def _params(**kw):
    return pltpu.CompilerParams(vmem_limit_bytes=VMEM_LIMIT, **kw)


def _row(tm, w):
    return pl.BlockSpec((tm, w), lambda i: (i, 0))


def _res(shape):
    return pl.BlockSpec(shape, lambda i: (0,) * len(shape), pipeline_mode=pl.Buffered(1))


def _mod(mod_ref, k):
    return mod_ref[3 * k:3 * k + 1, :], mod_ref[3 * k + 1:3 * k + 2, :], mod_ref[3 * k + 2:3 * k + 3, :]


def _ln_stats(z):
    mu = jnp.mean(z, axis=-1, keepdims=True)
    zc = z - mu
    var = jnp.mean(zc * zc, axis=-1, keepdims=True)
    rstd = lax.rsqrt(var + LN_EPS)
    return zc * rstd, rstd


def _ln_bwd(dxo, xhat, rstd, g):
    dxh = dxo * g
    m1 = jnp.mean(dxh, axis=-1, keepdims=True)
    m2 = jnp.mean(dxh * xhat, axis=-1, keepdims=True)
    return rstd * (dxh - m1 - xhat * m2)


def _colsum(v):
    return jnp.sum(v, axis=0, keepdims=True)


def _sigmoid(v):
    return 1.0 / (1.0 + jnp.exp(-v))


def _gelu_parts(v):
    k = math.sqrt(2.0 / math.pi)
    u = k * (v + 0.044715 * v * v * v)
    t = jnp.tanh(u)
    g = 0.5 * v * (1.0 + t)
    dg = 0.5 * (1.0 + t) + 0.5 * v * (1.0 - t * t) * k * (1.0 + 3.0 * 0.044715 * v * v)
    return g, dg


def _me():
    return lax.axis_index("x"), lax.axis_index("y"), lax.axis_index("c")


def _idx(p):
    return 4 * p[0] + 2 * p[1] + p[2]


def _peers(me):
    x, y, c = me
    out = []
    for k in range(1, N_DEV):
        out.append((1 - x if k & 4 else x, 1 - y if k & 2 else y, 1 - c if k & 1 else c))
    return out


class _Gather:
    def __init__(self, srcs):
        self.srcs = list(srcs)
        n = len(self.srcs)
        self.out_shape = [jax.ShapeDtypeStruct((N_DEV,) + s.shape, s.dtype) for s in self.srcs]
        self.scratch = [pltpu.SemaphoreType.DMA((n, 7)), pltpu.SemaphoreType.DMA((n, 7)),
                        pltpu.SemaphoreType.DMA((n,))]

    @staticmethod
    def _places():
        x, y, c = me = _me()
        return me, (x, y, 1 - c), [(1 - x, y), (x, 1 - y), (1 - x, 1 - y)]

    @staticmethod
    def _copy(outs, sems, t, k, block, to, src=None):
        slot = outs[t].at[_idx(block)]
        return pltpu.make_async_remote_copy(
            src_ref=slot if src is None else src, dst_ref=slot, send_sem=sems[0].at[t, k],
            recv_sem=sems[1].at[t, k], device_id=to, device_id_type=MESH)

    def _firsts(self, ins, outs, sems):
        me, sibling, chips = self._places()
        out = []
        for t in range(len(ins)):
            out.append(self._copy(outs, sems, t, 0, me, sibling, src=ins[t]))
            out += [self._copy(outs, sems, t, 1 + j, me, (*chip, me[2]), src=ins[t]) for j, chip in enumerate(chips)]
        return out

    def _locals(self, ins, outs, sems):
        me = _me()
        return [pltpu.make_async_copy(ins[t], outs[t].at[_idx(me)], sems[2].at[t]) for t in range(len(ins))]

    def start(self, ins, outs, sems):
        for cp in self._locals(ins, outs, sems) + self._firsts(ins, outs, sems):
            cp.start()

    def mid(self, ins, outs, sems):
        me, sibling, chips = self._places()
        for j, chip in enumerate(chips):
            for t in range(len(ins)):
                self._copy(outs, sems, t, 1 + j, (*chip, me[2]), me).wait_recv()
                self._copy(outs, sems, t, 4 + j, (*chip, me[2]), sibling).start()

    def finish(self, ins, outs, sems):
        me, sibling, chips = self._places()
        for t in range(len(ins)):
            self._copy(outs, sems, t, 0, sibling, me).wait_recv()
            for j, chip in enumerate(chips):
                self._copy(outs, sems, t, 4 + j, (*chip, 1 - me[2]), me).wait_recv()
        for cp in self._firsts(ins, outs, sems):
            cp.wait_send()
        for j, chip in enumerate(chips):
            for t in range(len(ins)):
                self._copy(outs, sems, t, 4 + j, (*chip, me[2]), sibling).wait_send()
        for cp in self._locals(ins, outs, sems):
            cp.wait()


class _AllToAll:
    def __init__(self, srcs):
        self.srcs = list(srcs)
        n = len(self.srcs)
        self.out_shape = [jax.ShapeDtypeStruct(s.shape, s.dtype) for s in self.srcs]
        self.scratch = [pltpu.SemaphoreType.DMA((n, 7)), pltpu.SemaphoreType.DMA((n, 7)),
                        pltpu.SemaphoreType.DMA((n,))]

    def _copies(self, ins, outs, sems):
        me = _me()
        loc, rem = [], []
        for t in range(len(ins)):
            loc.append(pltpu.make_async_copy(ins[t].at[_idx(me)], outs[t].at[_idx(me)], sems[2].at[t]))
            for k, p in enumerate(_peers(me)):
                rem.append(pltpu.make_async_remote_copy(
                    src_ref=ins[t].at[_idx(p)], dst_ref=outs[t].at[_idx(me)], send_sem=sems[0].at[t, k],
                    recv_sem=sems[1].at[t, k], device_id=p, device_id_type=MESH))
        return loc, rem

    def start(self, ins, outs, sems):
        loc, rem = self._copies(ins, outs, sems)
        for cp in loc + rem:
            cp.start()

    def mid(self, ins, outs, sems):
        pass

    def finish(self, ins, outs, sems):
        me = _me()
        for t in range(len(ins)):
            for k, p in enumerate(_peers(me)):
                slot = outs[t].at[_idx(p)]
                pltpu.make_async_remote_copy(
                    src_ref=slot, dst_ref=slot, send_sem=sems[0].at[t, k], recv_sem=sems[1].at[t, k],
                    device_id=p, device_id_type=MESH).wait_recv()
        loc, rem = self._copies(ins, outs, sems)
        for cp in rem:
            cp.wait_send()
        for cp in loc:
            cp.wait()


def exchange(ex, name):
    n = len(ex.srcs)

    def body(*refs):
        ins, outs, sems = refs[:n], refs[n:2 * n], refs[2 * n:]
        ex.start(ins, outs, sems)
        ex.mid(ins, outs, sems)
        ex.finish(ins, outs, sems)

    any_spec = pl.BlockSpec(memory_space=pl.ANY)
    return pl.pallas_call(
        body, name=name, out_shape=ex.out_shape, in_specs=[any_spec] * n, out_specs=[any_spec] * n,
        scratch_shapes=ex.scratch,
    )(*ex.srcs)


def _call(body, *, name, grid, in_specs, out_specs, out_shape, args, scratch_shapes=(), rider=None):
    in_specs, out_specs, out_shape = list(in_specs), list(out_specs), list(out_shape)
    scratch_shapes = list(scratch_shapes)
    if rider is None:
        return pl.pallas_call(body, name=name, grid=grid, out_shape=out_shape, in_specs=in_specs,
                              out_specs=out_specs, scratch_shapes=scratch_shapes, compiler_params=_params())(*args)
    nci, nco, ncs, nr = len(in_specs), len(out_shape), len(scratch_shapes), len(rider.srcs)
    nsteps = math.prod(grid)
    mid = min((3 * nsteps) // 4, nsteps - 2)
    assert 0 < mid, (name, grid)

    def full(*refs):
        ci, ri = refs[:nci], refs[nci:nci + nr]
        co, ro = refs[nci + nr:nci + nr + nco], refs[nci + nr + nco:nci + 2 * nr + nco]
        cs, rs = refs[nci + 2 * nr + nco:nci + 2 * nr + nco + ncs], refs[nci + 2 * nr + nco + ncs:]
        step = pl.program_id(0)
        for d in range(1, len(grid)):
            step = step * grid[d] + pl.program_id(d)

        @pl.when(step == 0)
        def _():
            rider.start(ri, ro, rs)

        @pl.when(step == mid)
        def _():
            rider.mid(ri, ro, rs)

        body(*ci, *co, *cs)

        @pl.when(step == nsteps - 1)
        def _():
            rider.finish(ri, ro, rs)

    any_spec = pl.BlockSpec(memory_space=pl.ANY)
    return pl.pallas_call(
        full, name=name, grid=grid, out_shape=out_shape + rider.out_shape,
        in_specs=in_specs + [any_spec] * nr, out_specs=out_specs + [any_spec] * nr,
        scratch_shapes=scratch_shapes + rider.scratch, compiler_params=_params(),
    )(*args, *rider.srcs)


def _a2a_start(srcs, dsts, send_sems, recv_sems, local_sems, me, sem_base=0):
    peers = _peers(me)
    started = []
    for t in range(len(srcs)):
        loc = pltpu.make_async_copy(srcs[t].at[_idx(me)], dsts[t].at[_idx(me)], local_sems.at[sem_base + t])
        loc.start()
        started.append(("local", loc))
        for k, p in enumerate(peers):
            cp = pltpu.make_async_remote_copy(
                src_ref=srcs[t].at[_idx(p)], dst_ref=dsts[t].at[_idx(me)],
                send_sem=send_sems.at[sem_base + t, k], recv_sem=recv_sems.at[sem_base + t, k],
                device_id=p, device_id_type=MESH)
            cp.start()
            started.append(("remote", cp))
    return started


def _a2a_finish(started, dsts, send_sems, recv_sems, me, sem_base=0):
    peers = _peers(me)
    for t in range(len(dsts)):
        for k, p in enumerate(peers):
            slot = dsts[t].at[_idx(p)]
            pltpu.make_async_remote_copy(
                src_ref=slot, dst_ref=slot, send_sem=send_sems.at[sem_base + t, k],
                recv_sem=recv_sems.at[sem_base + t, k], device_id=p, device_id_type=MESH).wait_recv()
    for kind, cp in started:
        if kind == "local":
            cp.wait()
        else:
            cp.wait_send()


def ada_modulation(c8, ada_w, ada_b):
    def body(c_ref, w_ref, b_ref, call_ref, modr_ref, modp, send_sems, recv_sems, local_sems):
        me = _me()
        for j in range(N_DEV):
            modp[j] = jnp.zeros(modp.shape[1:], F32)
        peers = _peers(me)
        sends = []
        for k, p in enumerate(peers):
            cp = pltpu.make_async_remote_copy(
                src_ref=c_ref, dst_ref=call_ref.at[_idx(me)], send_sem=send_sems.at[0, k],
                recv_sem=recv_sems.at[0, k], device_id=p, device_id_type=MESH)
            cp.start()
            sends.append(cp)
        call_ref[_idx(me)] = c_ref[...]
        for k, p in enumerate(peers):
            slot = call_ref.at[_idx(p)]
            pltpu.make_async_remote_copy(
                src_ref=slot, dst_ref=slot, send_sem=send_sems.at[0, k], recv_sem=recv_sems.at[0, k],
                device_id=p, device_id_type=MESH).wait_recv()
        for cp in sends:
            cp.wait_send()
        cv = call_ref[...].reshape(N_DEV * 8, D_MODEL)
        s = (cv * _sigmoid(cv)).astype(BF16)
        for k in range(4):
            res = _nn(s, w_ref[k].astype(BF16)) + b_ref[k]
            for j in range(N_DEV):
                modp[j, 8 * k:8 * k + 8, :] = res[8 * j:8 * j + 8, :]
        started = _a2a_start([modp], [modr_ref], send_sems, recv_sems, local_sems, me, sem_base=1)
        _a2a_finish(started, [modr_ref], send_sems, recv_sems, me, sem_base=1)

    vm = pl.BlockSpec(memory_space=pltpu.VMEM)
    return pl.pallas_call(
        body, name="ada_modulation",
        out_shape=[jax.ShapeDtypeStruct((N_DEV, 8, D_MODEL), F32), jax.ShapeDtypeStruct((N_DEV, 32, CG), F32)],
        in_specs=[vm, vm, vm], out_specs=[vm, vm],
        scratch_shapes=[pltpu.VMEM((N_DEV, 32, CG), F32), pltpu.SemaphoreType.DMA((2, 7)),
                        pltpu.SemaphoreType.DMA((2, 7)), pltpu.SemaphoreType.DMA((2,))],
        compiler_params=_params(),
    )(c8, ada_w, ada_b)


def ada_grads(gsend, c_t):
    def body(g_ref, ct_ref, gw_ref, gb_ref, grecv, send_sems, recv_sems, local_sems):
        me = _me()
        started = _a2a_start([g_ref], [grecv], send_sems, recv_sems, local_sems, me)
        _a2a_finish(started, [grecv], send_sems, recv_sems, me)
        ct = ct_ref[...]
        st = (ct * _sigmoid(ct)).astype(BF16).astype(F32)
        gb = jnp.zeros((8, CG), F32)
        for b in range(N_DEV):
            gb = gb + grecv[b]
        gb_ref[...] = gb
        for k in range(4):
            acc = jnp.zeros((D_MODEL, CG), F32)
            for b in range(N_DEV):
                row = grecv[b, k:k + 1, :].astype(BF16).astype(F32)
                acc = acc + st[:, b:b + 1] * row
            gw_ref[k] = acc

    vm = pl.BlockSpec(memory_space=pltpu.VMEM)
    return pl.pallas_call(
        body, name="ada_grads",
        out_shape=[jax.ShapeDtypeStruct((4, D_MODEL, CG), F32), jax.ShapeDtypeStruct((8, CG), F32)],
        in_specs=[vm, vm], out_specs=[vm, vm],
        scratch_shapes=[pltpu.VMEM((N_DEV, 8, CG), F32), pltpu.SemaphoreType.DMA((1, 7)),
                        pltpu.SemaphoreType.DMA((1, 7)), pltpu.SemaphoreType.DMA((1,))],
        compiler_params=_params(),
    )(gsend, c_t)


def small_reduce(ssend, rw):
    r = ssend.shape[1]

    def body(s_ref, red_ref, wag_ref, recv, send_sems, recv_sems, local_sems):
        me = _me()
        started = _a2a_start([s_ref], [recv], send_sems, recv_sems, local_sems, me)
        _a2a_finish(started, [recv], send_sems, recv_sems, me)
        acc = recv[0]
        for j in range(1, N_DEV):
            acc = acc + recv[j]
        red_ref[...] = acc
        peers = _peers(me)
        top = red_ref.at[pl.ds(0, rw), :]
        sends = []
        for k, p in enumerate(peers):
            cp = pltpu.make_async_remote_copy(
                src_ref=top, dst_ref=wag_ref.at[_idx(me)], send_sem=send_sems.at[1, k],
                recv_sem=recv_sems.at[1, k], device_id=p, device_id_type=MESH)
            cp.start()
            sends.append(cp)
        wag_ref[_idx(me)] = acc[0:rw, :]
        for k, p in enumerate(peers):
            slot = wag_ref.at[_idx(p)]
            pltpu.make_async_remote_copy(
                src_ref=slot, dst_ref=slot, send_sem=send_sems.at[1, k], recv_sem=recv_sems.at[1, k],
                device_id=p, device_id_type=MESH).wait_recv()
        for cp in sends:
            cp.wait_send()

    vm = pl.BlockSpec(memory_space=pltpu.VMEM)
    return pl.pallas_call(
        body, name="small_reduce",
        out_shape=[jax.ShapeDtypeStruct((r, 128), F32), jax.ShapeDtypeStruct((N_DEV, rw, 128), F32)],
        in_specs=[vm], out_specs=[vm, vm],
        scratch_shapes=[pltpu.VMEM((N_DEV, r, 128), F32), pltpu.SemaphoreType.DMA((2, 7)),
                        pltpu.SemaphoreType.DMA((2, 7)), pltpu.SemaphoreType.DMA((1,))],
        compiler_params=_params(),
    )(ssend)


def _rope(t, cos, s1, s2):
    return t * cos + pltpu.roll(t, 16, 1) * s1 + pltpu.roll(t, HEAD - 16, 1) * s2


def _rope_bwd(d, cos, s1, s2):
    return d * cos + pltpu.roll(d * s1, HEAD - 16, 1) + pltpu.roll(d * s2, 16, 1)


def attn_in_fwd(x, mods, k, win_t, rope):
    s = x.shape[0]
    tm = _blk(s, TM_MM)

    def body(x_ref, mod_ref, w_ref, c_ref, s1_ref, s2_ref, q_ref, k_ref, v_ref):
        shift, scale, _ = _mod(mod_ref, k)
        h = (x_ref[...] * (1.0 + scale) + shift).astype(BF16)
        qkv = _nt(h, w_ref[...])
        cos, s1, s2 = c_ref[...], s1_ref[...], s2_ref[...]
        for hh in range(N_Q + N_KV):
            r = _rope(qkv[:, HEAD * hh:HEAD * (hh + 1)], cos, s1, s2).astype(BF16)
            if hh < N_Q:
                q_ref[:, HEAD * hh:HEAD * (hh + 1)] = r
            else:
                k_ref[:, HEAD * (hh - N_Q):HEAD * (hh - N_Q + 1)] = r
        v_ref[...] = qkv[:, HEAD * (N_Q + N_KV):].astype(BF16)

    return pl.pallas_call(
        body, name="attn_in_fwd", grid=(s // tm,),
        out_shape=[jax.ShapeDtypeStruct((s, N_Q * HEAD), BF16), jax.ShapeDtypeStruct((s, N_KV * HEAD), BF16),
                   jax.ShapeDtypeStruct((s, N_KV * HEAD), BF16)],
        in_specs=[_row(tm, D_MODEL), _res(mods.shape), _res(win_t.shape),
                  _row(tm, HEAD), _row(tm, HEAD), _row(tm, HEAD)],
        out_specs=[_row(tm, N_Q * HEAD), _row(tm, N_KV * HEAD), _row(tm, N_KV * HEAD)],
        compiler_params=_params(),
    )(x, mods, win_t, *rope)


def _kv_specs(nblk):
    w = N_KV * HEAD
    return [pl.BlockSpec((QBLK, w), lambda n: (jnp.maximum(n - 1, 0), 0)),
            pl.BlockSpec((QBLK, w), lambda n: (n, 0)),
            pl.BlockSpec((QBLK, w), lambda n: (jnp.minimum(n + 1, nblk - 1), 0))]


GROUP = N_Q // N_KV


def _attn_mask(n, s):
    qi = lax.broadcasted_iota(jnp.int32, (GROUP * QBLK, 3 * QBLK), 0) & (QBLK - 1)
    kj = lax.broadcasted_iota(jnp.int32, (GROUP * QBLK, 3 * QBLK), 1)
    rel = kj - QBLK - qi
    kpos = kj + (n - 1) * QBLK
    return (jnp.abs(rel) <= QBLK) & (kpos >= 0) & (kpos < s)


def _stack_heads(ref, kv):
    return jnp.concatenate([ref[:, HEAD * (GROUP * kv + j):HEAD * (GROUP * kv + j + 1)] for j in range(GROUP)], axis=0)


def _stack_sinks(sink_ref, kv):
    row = lax.broadcasted_iota(jnp.int32, (GROUP * QBLK, 1), 0)
    out = jnp.full((GROUP * QBLK, 1), sink_ref[0, GROUP * kv + GROUP - 1], F32)
    for j in range(GROUP - 2, -1, -1):
        out = jnp.where(row < QBLK * (j + 1), sink_ref[0, GROUP * kv + j], out)
    return out


def _attn_probs(qh, kh, valid, sink):
    sc = _nt(qh, kh) * (HEAD ** -0.5)
    sc = jnp.where(valid, sc, -1e30)
    m = jnp.maximum(jnp.max(sc, axis=-1, keepdims=True), sink)
    p = jnp.exp(sc - m)
    es = jnp.exp(sink - m)
    denom = jnp.sum(p, axis=-1, keepdims=True) + es
    return p / denom, es / denom


def attn_fwd(q, kk, v, sinks, rider=None):
    s = q.shape[0]
    nblk = s // QBLK

    def body(sink_ref, q_ref, kp, ko, kn, vp, vo, vn, o_ref):
        n = pl.program_id(0)
        kcat = jnp.concatenate([kp[...], ko[...], kn[...]], axis=0)
        vcat = jnp.concatenate([vp[...], vo[...], vn[...]], axis=0)
        valid = _attn_mask(n, s)
        for kv in range(N_KV):
            cols = slice(HEAD * kv, HEAD * (kv + 1))
            probs, _ = _attn_probs(_stack_heads(q_ref, kv), kcat[:, cols], valid, _stack_sinks(sink_ref, kv))
            og = _nn(probs.astype(BF16), vcat[:, cols]).astype(BF16)
            for j in range(GROUP):
                hq = GROUP * kv + j
                o_ref[:, HEAD * hq:HEAD * (hq + 1)] = og[QBLK * j:QBLK * (j + 1), :]

    return _call(
        body, name="attn_fwd", grid=(nblk,),
        out_shape=[jax.ShapeDtypeStruct((s, N_Q * HEAD), BF16)],
        in_specs=[pl.BlockSpec(memory_space=pltpu.SMEM), pl.BlockSpec((QBLK, N_Q * HEAD), lambda n: (n, 0))]
        + _kv_specs(nblk) + _kv_specs(nblk),
        out_specs=[pl.BlockSpec((QBLK, N_Q * HEAD), lambda n: (n, 0))],
        args=(sinks, q, kk, kk, kk, v, v, v), rider=rider)


def post_fwd(ypre, w, x, mods, k, lng, lnb, gate_act=None):
    s = x.shape[0]
    tm = _blk(s, TM_MM)
    kdim = w.shape[0]
    rnn = gate_act is not None

    def body(*refs):
        if rnn:
            gt_ref, hf_ref, hb_ref, w_ref, x_ref, mod_ref, g_ref, b_ref, xo_ref, y_ref, yp_ref = refs
            act, _ = _gelu_parts(gt_ref[...])
            yp = ((hf_ref[...] + hb_ref[...]) * act).astype(BF16)
            yp_ref[...] = yp
        else:
            yp_ref, w_ref, x_ref, mod_ref, g_ref, b_ref, xo_ref, y_ref = refs
            yp = yp_ref[...]
        _, _, gate = _mod(mod_ref, k)
        y = _nn(yp, w_ref[...])
        y_ref[...] = y
        xhat, _ = _ln_stats(ALPHA * x_ref[...] + (1.0 + gate) * y)
        xo_ref[...] = xhat * g_ref[...] + b_ref[...]

    act_in = list(gate_act) if rnn else [ypre]
    out_shape = [jax.ShapeDtypeStruct((s, D_MODEL), F32), jax.ShapeDtypeStruct((s, D_MODEL), F32)]
    out_specs = [_row(tm, D_MODEL), _row(tm, D_MODEL)]
    if rnn:
        out_shape.append(jax.ShapeDtypeStruct((s, kdim), BF16))
        out_specs.append(_row(tm, kdim))
    return pl.pallas_call(
        body, name="rnn_post_fwd" if rnn else "attn_post_fwd", grid=(s // tm,),
        out_shape=out_shape,
        in_specs=[_row(tm, kdim)] * len(act_in) + [_res(w.shape), _row(tm, D_MODEL), _res(mods.shape),
                                                    _res(lng.shape), _res(lnb.shape)],
        out_specs=out_specs,
        compiler_params=_params(),
    )(*act_in, w, x, mods, lng, lnb)


def mlp_fwd(x, mods, k, w1_t, w2, lng, lnb, rider=None):
    s = x.shape[0]
    tm = _blk(s, TM_MLP)

    def body(x_ref, mod_ref, w1_ref, w2_ref, g_ref, b_ref, xo_ref, y_ref, ra_ref, r_ref):
        xv = x_ref[...]
        shift, scale, gate = _mod(mod_ref, k)
        h = (xv * (1.0 + scale) + shift).astype(BF16)
        y = jnp.zeros((tm, D_MODEL), F32)
        for c in range(D_FF // FF_CHUNK):
            rows = slice(FF_CHUNK * c, FF_CHUNK * (c + 1))
            a = jnp.maximum(_nt(h, w1_ref[rows, :]), 0.0)
            r = (a * a).astype(BF16)
            ra_ref[:, rows] = a.astype(BF16)
            r_ref[:, rows] = r
            y = y + _nn(r, w2_ref[rows, :])
        y_ref[...] = y
        xhat, _ = _ln_stats(ALPHA * xv + (1.0 + gate) * y)
        xo_ref[...] = xhat * g_ref[...] + b_ref[...]

    return _call(
        body, name="mlp_fwd", grid=(s // tm,),
        out_shape=[jax.ShapeDtypeStruct((s, D_MODEL), F32)] * 2 + [jax.ShapeDtypeStruct((s, D_FF), BF16)] * 2,
        in_specs=[_row(tm, D_MODEL), _res(mods.shape), _res(w1_t.shape), _res(w2.shape),
                  _res(lng.shape), _res(lnb.shape)],
        out_specs=[_row(tm, D_MODEL)] * 2 + [_row(tm, D_FF)] * 2,
        args=(x, mods, w1_t, w2, lng, lnb), rider=rider)


def rnn_in_fwd(x, mods, k, win_t):
    s = x.shape[0]
    tm = _blk(s, TM_MM)

    def body(x_ref, mod_ref, w_ref, xr_ref, gt_ref):
        shift, scale, _ = _mod(mod_ref, k)
        h = (x_ref[...] * (1.0 + scale) + shift).astype(BF16)
        xr_ref[...] = _nt(h, w_ref[0:D_RNN, :])
        gt_ref[...] = _nt(h, w_ref[D_RNN:2 * D_RNN, :])

    return pl.pallas_call(
        body, name="rnn_in_fwd", grid=(s // tm,),
        out_shape=[jax.ShapeDtypeStruct((s, D_RNN), F32)] * 2,
        in_specs=[_row(tm, D_MODEL), _res(mods.shape), _res(win_t.shape)],
        out_specs=[_row(tm, D_RNN)] * 2,
        compiler_params=_params(),
    )(x, mods, win_t)


def _shift_rows(v, k, row):
    n = v.shape[0]
    r = pltpu.roll(v, k % n, 0)
    keep = (row >= k) if k > 0 else (row < n + k)
    return jnp.where(keep, r, 0.0)


def conv_fwd(xr, cw, cb):
    s = xr.shape[0]

    def body(x_ref, w_ref, b_ref, o_ref):
        xv = x_ref[...]
        row = lax.broadcasted_iota(jnp.int32, xv.shape, 0)
        o_ref[...] = (b_ref[...] + w_ref[0:1, :] * _shift_rows(xv, 2, row) + w_ref[1:2, :] * _shift_rows(xv, 1, row)
                      + w_ref[2:3, :] * xv + w_ref[3:4, :] * _shift_rows(xv, -1, row))

    slab = pl.BlockSpec((s, 128), lambda j: (0, j))
    return pl.pallas_call(
        body, name="conv_fwd", grid=(D_RNN // 128,),
        out_shape=jax.ShapeDtypeStruct((s, D_RNN), F32),
        in_specs=[slab, pl.BlockSpec((4, 128), lambda j: (0, j)), pl.BlockSpec((1, 128), lambda j: (0, j))],
        out_specs=slab,
        compiler_params=_params(),
    )(xr, cw, cb)


def conv_bwd(da, db, xr, cw):
    s = xr.shape[0]

    def body(da_ref, db_ref, x_ref, w_ref, dx_ref, dw_ref, dbias_ref):
        d = da_ref[...] + db_ref[...]
        xv = x_ref[...]
        row = lax.broadcasted_iota(jnp.int32, xv.shape, 0)
        dx_ref[...] = (w_ref[0:1, :] * _shift_rows(d, -2, row) + w_ref[1:2, :] * _shift_rows(d, -1, row)
                       + w_ref[2:3, :] * d + w_ref[3:4, :] * _shift_rows(d, 1, row))
        dw_ref[0:1, :] = _colsum(d * _shift_rows(xv, 2, row))
        dw_ref[1:2, :] = _colsum(d * _shift_rows(xv, 1, row))
        dw_ref[2:3, :] = _colsum(d * xv)
        dw_ref[3:4, :] = _colsum(d * _shift_rows(xv, -1, row))
        dbias_ref[...] = _colsum(d)

    slab = pl.BlockSpec((s, 128), lambda j: (0, j))
    return pl.pallas_call(
        body, name="conv_bwd", grid=(D_RNN // 128,),
        out_shape=[jax.ShapeDtypeStruct((s, D_RNN), F32), jax.ShapeDtypeStruct((4, D_RNN), F32),
                   jax.ShapeDtypeStruct((1, D_RNN), F32)],
        in_specs=[slab, slab, slab, pl.BlockSpec((4, 128), lambda j: (0, j))],
        out_specs=[slab, pl.BlockSpec((4, 128), lambda j: (0, j)), pl.BlockSpec((1, 128), lambda j: (0, j))],
        compiler_params=_params(),
    )(da, db, xr, cw)


def _softplus_neg(lam):
    z = -lam
    e = jnp.exp(-jnp.abs(z))
    u = 1.0 + e
    log1p = jnp.where(u == 1.0, e, jnp.log(u) * e / jnp.where(u == 1.0, 1.0, u - 1.0))
    return jnp.maximum(z, 0.0) + log1p, 1.0 / (1.0 + jnp.exp(lam))


def _lru_gates(xv, wa_ref, wx_ref, ba_ref, bx_ref, lam_ref):
    xb = xv.astype(BF16)
    r = _sigmoid(_nn(xb, wa_ref[...]) + ba_ref[...])
    i = _sigmoid(_nn(xb, wx_ref[...]) + bx_ref[...])
    sp, sg = _softplus_neg(lam_ref[...])
    la = -LRU_C * r * sp
    a = jnp.exp(la)
    th = jnp.tanh(la)
    mult = jnp.sqrt(-2.0 * th / (1.0 - th))
    return xb, r, i, sp, sg, a, mult


def _scan(a, u, h0, reverse):
    n = a.shape[0]
    row = lax.broadcasted_iota(jnp.int32, a.shape, 0)
    sh = 1
    while sh < n:
        if reverse:
            keep = row < n - sh
            a_s = jnp.where(keep, pltpu.roll(a, n - sh, 0), 1.0)
            u_s = jnp.where(keep, pltpu.roll(u, n - sh, 0), 0.0)
        else:
            keep = row >= sh
            a_s = jnp.where(keep, pltpu.roll(a, sh, 0), 1.0)
            u_s = jnp.where(keep, pltpu.roll(u, sh, 0), 0.0)
        u = a * u_s + u
        a = a * a_s
        sh *= 2
    return u + a * h0


def _lru_specs(nt, tt, reverse):
    tmap = (lambda t: nt - 1 - t) if reverse else (lambda t: t)
    blk = pl.BlockSpec((tt, CG), lambda g, t: (tmap(t), g))
    wsp = pl.BlockSpec((None, CG, CG), lambda g, t: (g, 0, 0))
    vec = pl.BlockSpec((1, CG), lambda g, t: (0, g))
    return tmap, blk, wsp, vec


def lru_fwd(xc, wa, wx, ba, bx, lam, reverse):
    s = xc.shape[0]
    tt = _blk(s, TT_RNN)
    sb = _blk(tt, SB_RNN)
    nt = s // tt

    def body(x_ref, wa_ref, wx_ref, ba_ref, bx_ref, lam_ref, hs_ref, carry):
        @pl.when(pl.program_id(1) == 0)
        def _():
            carry[...] = jnp.zeros(carry.shape, F32)

        xv = x_ref[...]
        _, _, i, _, _, a, mult = _lru_gates(xv, wa_ref, wx_ref, ba_ref, bx_ref, lam_ref)
        u = mult * (i * xv)
        h0 = carry[0:1, :]
        order = range(tt // sb - 1, -1, -1) if reverse else range(tt // sb)
        for j in order:
            rows = slice(sb * j, sb * (j + 1))
            h = _scan(a[rows], u[rows], h0, reverse)
            hs_ref[rows, :] = h
            h0 = h[0:1, :] if reverse else h[sb - 1:sb, :]
        carry[0:1, :] = h0

    _, blk, wsp, vec = _lru_specs(nt, tt, reverse)
    return pl.pallas_call(
        body, name="lru_fwd_rev" if reverse else "lru_fwd", grid=(N_CG, nt),
        out_shape=jax.ShapeDtypeStruct((s, D_RNN), F32),
        in_specs=[blk, wsp, wsp, vec, vec, vec], out_specs=blk,
        scratch_shapes=[pltpu.VMEM((8, CG), F32)],
        compiler_params=_params(),
    )(xc, wa, wx, ba, bx, lam)


def lru_bwd(xc, dhs, hs, wa, wx, ba, bx, lam, reverse, rider=None):
    s = xc.shape[0]
    tt = _blk(s, TT_RNN)
    sb = _blk(tt, SB_RNN)
    nt = s // tt
    back = not reverse

    def body(x_ref, dh_ref, hs_ref, nb_ref, wa_ref, wx_ref, ba_ref, bx_ref, lam_ref,
             dx_ref, dwa_ref, dwx_ref, dba_ref, dbx_ref, dlam_ref, carry):
        t = pl.program_id(1)

        @pl.when(t == 0)
        def _():
            carry[...] = jnp.zeros(carry.shape, F32)
            dwa_ref[...] = jnp.zeros(dwa_ref.shape, F32)
            dwx_ref[...] = jnp.zeros(dwx_ref.shape, F32)
            dba_ref[...] = jnp.zeros(dba_ref.shape, F32)
            dbx_ref[...] = jnp.zeros(dbx_ref.shape, F32)
            dlam_ref[...] = jnp.zeros(dlam_ref.shape, F32)

        xv = x_ref[...]
        xb, r, i, sp, sg, a, mult = _lru_gates(xv, wa_ref, wx_ref, ba_ref, bx_ref, lam_ref)
        row = lax.broadcasted_iota(jnp.int32, xv.shape, 0)
        hsv = hs_ref[...]
        inner = t < nt - 1
        if reverse:
            edge = jnp.where(inner, nb_ref[0:1, :], 0.0)
            hprev = jnp.where(row == tt - 1, edge, pltpu.roll(hsv, tt - 1, 0))
            a_next = jnp.where(row == 0, carry[1:2, :], pltpu.roll(a, 1, 0))
        else:
            edge = jnp.where(inner, nb_ref[7:8, :], 0.0)
            hprev = jnp.where(row == 0, edge, pltpu.roll(hsv, 1, 0))
            a_next = jnp.where(row == tt - 1, carry[1:2, :], pltpu.roll(a, tt - 1, 0))
        dhv = dh_ref[...]
        g0 = carry[0:1, :]
        parts = [None] * (tt // sb)
        order = range(tt // sb - 1, -1, -1) if back else range(tt // sb)
        for j in order:
            rows = slice(sb * j, sb * (j + 1))
            gj = _scan(a_next[rows], dhv[rows], g0, back)
            parts[j] = gj
            g0 = gj[0:1, :] if back else gj[sb - 1:sb, :]
        g = jnp.concatenate(parts, axis=0) if len(parts) > 1 else parts[0]
        carry[0:1, :] = g0
        carry[1:2, :] = a[0:1, :] if back else a[tt - 1:tt, :]

        da = g * hprev
        dmult = g * (i * xv)
        di = g * mult * xv
        dla = da * a - dmult * (a * a) / mult
        dpa = (dla * (-LRU_C * sp)) * r * (1.0 - r)
        dpx = di * i * (1.0 - i)
        dlam_ref[...] += _colsum(dla * (LRU_C * r * sg))
        dba_ref[...] += _colsum(dpa)
        dbx_ref[...] += _colsum(dpx)
        dpab, dpxb = dpa.astype(BF16), dpx.astype(BF16)
        dx_ref[...] = g * mult * i + _nt(dpab, wa_ref[...]) + _nt(dpxb, wx_ref[...])
        dwa_ref[...] += _tn(xb, dpab)
        dwx_ref[...] += _tn(xb, dpxb)

    tmap, blk, wsp, vec = _lru_specs(nt, tt, back)
    per8 = tt // 8
    if reverse:
        nb = pl.BlockSpec((8, CG), lambda g, t: (jnp.minimum((tmap(t) + 1) * per8, s // 8 - 1), g))
    else:
        nb = pl.BlockSpec((8, CG), lambda g, t: (jnp.maximum(tmap(t) * per8 - 1, 0), g))
    return _call(
        body, name="lru_bwd_rev" if reverse else "lru_bwd", grid=(N_CG, nt),
        out_shape=[jax.ShapeDtypeStruct((s, D_RNN), F32), jax.ShapeDtypeStruct((N_CG, CG, CG), F32),
                   jax.ShapeDtypeStruct((N_CG, CG, CG), F32)] + [jax.ShapeDtypeStruct((1, D_RNN), F32)] * 3,
        in_specs=[blk, blk, blk, nb, wsp, wsp, vec, vec, vec],
        out_specs=[blk, wsp, wsp, vec, vec, vec],
        scratch_shapes=[pltpu.VMEM((8, CG), F32)],
        args=(xc, dhs, hs, hs, wa, wx, ba, bx, lam), rider=rider)


def loss_grad(xo, target):
    s = xo.shape[0]
    tm = _blk(s, TM_MM)

    def body(x_ref, t_ref, d_ref, sq_ref):
        @pl.when(pl.program_id(0) == 0)
        def _():
            sq_ref[...] = jnp.zeros(sq_ref.shape, F32)

        e = x_ref[...] - t_ref[...]
        d_ref[...] = e * (1.0 / D_MODEL)
        sq_ref[...] += _colsum(e * e)

    return pl.pallas_call(
        body, name="loss_grad", grid=(s // tm,),
        out_shape=[jax.ShapeDtypeStruct((s, D_MODEL), F32), jax.ShapeDtypeStruct((1, D_MODEL), F32)],
        in_specs=[_row(tm, D_MODEL)] * 2, out_specs=[_row(tm, D_MODEL), _res((1, D_MODEL))],
        compiler_params=_params(),
    )(xo, target)


def _ln_part_bwd(dxo, x, y, gate, g, sums_ref):
    xhat, rstd = _ln_stats(ALPHA * x + (1.0 + gate) * y)
    dz = _ln_bwd(dxo, xhat, rstd, g)
    sums_ref[2:3, :] += _colsum(dz * y)
    sums_ref[3:4, :] += _colsum(dxo * xhat)
    sums_ref[4:5, :] += _colsum(dxo)
    return dz


def mlp_bwd(dxo, x, y, ra, mods, k, w1_t, w2, lng, rider=None):
    s = x.shape[0]
    tm = _blk(s, TM_MLP)

    def body(d_ref, x_ref, y_ref, ra_ref, mod_ref, w1_ref, w2_ref, g_ref,
             dx_ref, da_ref, h_ref, dy_ref, sums_ref):
        @pl.when(pl.program_id(0) == 0)
        def _():
            sums_ref[...] = jnp.zeros(sums_ref.shape, F32)

        xv = x_ref[...]
        shift, scale, gate = _mod(mod_ref, k)
        dz = _ln_part_bwd(d_ref[...], xv, y_ref[...], gate, g_ref[...], sums_ref)
        dyb = (dz * (1.0 + gate)).astype(BF16)
        dy_ref[...] = dyb
        h = (xv * (1.0 + scale) + shift).astype(BF16)
        h_ref[...] = h
        dh = jnp.zeros((tm, D_MODEL), F32)
        for c in range(D_FF // FF_CHUNK):
            rows = slice(FF_CHUNK * c, FF_CHUNK * (c + 1))
            da = (_nt(dyb, w2_ref[rows, :]) * (2.0 * ra_ref[:, rows].astype(F32))).astype(BF16)
            da_ref[:, rows] = da
            dh = dh + _nn(da, w1_ref[rows, :])
        dx_ref[...] = ALPHA * dz + dh * (1.0 + scale)
        sums_ref[0:1, :] += _colsum(dh)
        sums_ref[1:2, :] += _colsum(dh * xv)

    return _call(
        body, name="mlp_bwd", grid=(s // tm,),
        out_shape=[jax.ShapeDtypeStruct((s, D_MODEL), F32), jax.ShapeDtypeStruct((s, D_FF), BF16),
                   jax.ShapeDtypeStruct((s, D_MODEL), BF16),
                   jax.ShapeDtypeStruct((s, D_MODEL), BF16), jax.ShapeDtypeStruct((8, D_MODEL), F32)],
        in_specs=[_row(tm, D_MODEL)] * 3 + [_row(tm, D_FF), _res(mods.shape), _res(w1_t.shape), _res(w2.shape),
                                             _res(lng.shape)],
        out_specs=[_row(tm, D_MODEL), _row(tm, D_FF), _row(tm, D_MODEL), _row(tm, D_MODEL), _res((8, D_MODEL))],
        args=(dxo, x, y, ra, mods, w1_t, w2, lng), rider=rider)


def post_bwd(dxo, x, y, mods, k, w, lng, gate_act=None, rider=None):
    s = x.shape[0]
    tm = _blk(s, TM_MM)
    kdim = w.shape[0]
    rnn = gate_act is not None

    def body(*refs):
        if rnn:
            (d_ref, x_ref, y_ref, mod_ref, w_ref, g_ref, gt_ref, hf_ref, hb_ref,
             dres_ref, dy_ref, sums_ref, dhs_ref, dgt_ref) = refs
        else:
            d_ref, x_ref, y_ref, mod_ref, w_ref, g_ref, dres_ref, dy_ref, sums_ref, dyp_ref = refs

        @pl.when(pl.program_id(0) == 0)
        def _():
            sums_ref[...] = jnp.zeros(sums_ref.shape, F32)

        _, _, gate = _mod(mod_ref, k)
        dz = _ln_part_bwd(d_ref[...], x_ref[...], y_ref[...], gate, g_ref[...], sums_ref)
        dres_ref[...] = ALPHA * dz
        dyb = (dz * (1.0 + gate)).astype(BF16)
        dy_ref[...] = dyb
        dyp = _nt(dyb, w_ref[...])
        if rnn:
            act, dact = _gelu_parts(gt_ref[...])
            dhs_ref[...] = dyp * act
            dgt_ref[...] = dyp * (hf_ref[...] + hb_ref[...]) * dact
        else:
            dyp_ref[...] = dyp.astype(BF16)

    ins = [dxo, x, y, mods, w, lng] + (list(gate_act) if rnn else [])
    in_specs = [_row(tm, D_MODEL)] * 3 + [_res(mods.shape), _res(w.shape), _res(lng.shape)]
    out_shape = [jax.ShapeDtypeStruct((s, D_MODEL), F32), jax.ShapeDtypeStruct((s, D_MODEL), BF16),
                 jax.ShapeDtypeStruct((8, D_MODEL), F32)]
    out_specs = [_row(tm, D_MODEL), _row(tm, D_MODEL), _res((8, D_MODEL))]
    if rnn:
        in_specs += [_row(tm, kdim)] * 3
        out_shape += [jax.ShapeDtypeStruct((s, kdim), F32)] * 2
        out_specs += [_row(tm, kdim)] * 2
    else:
        out_shape.append(jax.ShapeDtypeStruct((s, kdim), BF16))
        out_specs.append(_row(tm, kdim))
    return _call(
        body, name="rnn_post_bwd" if rnn else "attn_post_bwd", grid=(s // tm,),
        out_shape=out_shape, in_specs=in_specs, out_specs=out_specs, args=ins, rider=rider)


def attn_bwd(q, kk, v, do, sinks, rider=None):
    s = q.shape[0]
    nblk = s // QBLK
    scale = HEAD ** -0.5

    def body(sink_ref, q_ref, do_ref, kp, ko, kn, vp, vo, vn, dq_ref, dkp_ref, dvp_ref, ds_ref):
        n = pl.program_id(0)

        @pl.when(n == 0)
        def _():
            ds_ref[...] = jnp.zeros(ds_ref.shape, F32)

        kcat = jnp.concatenate([kp[...], ko[...], kn[...]], axis=0)
        vcat = jnp.concatenate([vp[...], vo[...], vn[...]], axis=0)
        valid = _attn_mask(n, s)
        lane = lax.broadcasted_iota(jnp.int32, (1, 128), 1)
        dsink = jnp.zeros((1, 128), F32)
        for kv in range(N_KV):
            cols = slice(HEAD * kv, HEAD * (kv + 1))
            qg, dog = _stack_heads(q_ref, kv), _stack_heads(do_ref, kv)
            kh, vh = kcat[:, cols], vcat[:, cols]
            probs, psink = _attn_probs(qg, kh, valid, _stack_sinks(sink_ref, kv))
            dprobs = _nt(dog, vh)
            dvp_ref[:, cols] = _tn(probs.astype(BF16), dog)
            rowdot = jnp.sum(probs * dprobs, axis=-1, keepdims=True)
            dsb = (probs * (dprobs - rowdot) * scale).astype(BF16)
            dqg = _nn(dsb, kh)
            dkp_ref[:, cols] = _tn(dsb, qg)
            dsk = -psink * rowdot
            for j in range(GROUP):
                hq = GROUP * kv + j
                dq_ref[:, HEAD * hq:HEAD * (hq + 1)] = dqg[QBLK * j:QBLK * (j + 1), :]
                dsink = dsink + jnp.where(lane == hq, _colsum(dsk[QBLK * j:QBLK * (j + 1), :]), 0.0)
        ds_ref[...] += dsink

    qspec = pl.BlockSpec((QBLK, N_Q * HEAD), lambda n: (n, 0))
    pspec = pl.BlockSpec((None, 3 * QBLK, N_KV * HEAD), lambda n: (n, 0, 0))
    return _call(
        body, name="attn_bwd", grid=(nblk,),
        out_shape=[jax.ShapeDtypeStruct((s, N_Q * HEAD), F32),
                   jax.ShapeDtypeStruct((nblk, 3 * QBLK, N_KV * HEAD), F32),
                   jax.ShapeDtypeStruct((nblk, 3 * QBLK, N_KV * HEAD), F32),
                   jax.ShapeDtypeStruct((1, 128), F32)],
        in_specs=[pl.BlockSpec(memory_space=pltpu.SMEM), qspec, qspec] + _kv_specs(nblk) + _kv_specs(nblk),
        out_specs=[qspec, pspec, pspec, pl.BlockSpec((1, 128), lambda n: (0, 0))],
        args=(sinks, q, do, kk, kk, kk, v, v, v), rider=rider)


def kv_combine(dkp, dvp, rider=None):
    nblk = dkp.shape[0]
    w = N_KV * HEAD

    def body(kp, ko, kn, vp, vo, vn, dk_ref, dv_ref):
        n = pl.program_id(0)
        dk_ref[...] = jnp.where(n > 0, kp[...], 0.0) + ko[...] + jnp.where(n < nblk - 1, kn[...], 0.0)
        dv_ref[...] = jnp.where(n > 0, vp[...], 0.0) + vo[...] + jnp.where(n < nblk - 1, vn[...], 0.0)

    specs = [pl.BlockSpec((None, QBLK, w), lambda n: (jnp.maximum(n - 1, 0), 2, 0)),
             pl.BlockSpec((None, QBLK, w), lambda n: (n, 1, 0)),
             pl.BlockSpec((None, QBLK, w), lambda n: (jnp.minimum(n + 1, nblk - 1), 0, 0))]
    out = pl.BlockSpec((QBLK, w), lambda n: (n, 0))
    return _call(
        body, name="kv_combine", grid=(nblk,),
        out_shape=[jax.ShapeDtypeStruct((nblk * QBLK, w), F32)] * 2,
        in_specs=specs + specs, out_specs=[out, out],
        args=(dkp, dkp, dkp, dvp, dvp, dvp), rider=rider)


def _in_bwd_tail(dzb, w_ref, x_ref, mod_ref, k, dres_ref, dx_ref, h_ref, sums_ref):
    xv = x_ref[...]
    shift, scale, _ = _mod(mod_ref, k)
    h_ref[...] = (xv * (1.0 + scale) + shift).astype(BF16)
    dh = _nn(dzb, w_ref[...])
    dx_ref[...] = dres_ref[...] + dh * (1.0 + scale)
    sums_ref[0:1, :] += _colsum(dh)
    sums_ref[1:2, :] += _colsum(dh * xv)


def attn_in_bwd(dq, dk, dv, rope, x, mods, k, win_t, dres):
    s = x.shape[0]
    tm = _blk(s, TM_MM)

    def body(dq_ref, dk_ref, dv_ref, c_ref, s1_ref, s2_ref, x_ref, mod_ref, w_ref, dres_ref,
             dx_ref, dz_ref, h_ref, sums_ref):
        @pl.when(pl.program_id(0) == 0)
        def _():
            sums_ref[...] = jnp.zeros(sums_ref.shape, F32)

        cos, s1, s2 = c_ref[...], s1_ref[...], s2_ref[...]
        for hh in range(N_Q + N_KV):
            src = dq_ref[:, HEAD * hh:HEAD * (hh + 1)] if hh < N_Q else dk_ref[:, HEAD * (hh - N_Q):HEAD * (hh - N_Q + 1)]
            dz_ref[:, HEAD * hh:HEAD * (hh + 1)] = _rope_bwd(src, cos, s1, s2).astype(BF16)
        dz_ref[:, HEAD * (N_Q + N_KV):] = dv_ref[...].astype(BF16)
        _in_bwd_tail(dz_ref[...], w_ref, x_ref, mod_ref, k, dres_ref, dx_ref, h_ref, sums_ref)

    return pl.pallas_call(
        body, name="attn_in_bwd", grid=(s // tm,),
        out_shape=[jax.ShapeDtypeStruct((s, D_MODEL), F32), jax.ShapeDtypeStruct((s, D_QKV), BF16),
                   jax.ShapeDtypeStruct((s, D_MODEL), BF16), jax.ShapeDtypeStruct((8, D_MODEL), F32)],
        in_specs=[_row(tm, N_Q * HEAD), _row(tm, N_KV * HEAD), _row(tm, N_KV * HEAD),
                  _row(tm, HEAD), _row(tm, HEAD), _row(tm, HEAD), _row(tm, D_MODEL),
                  _res(mods.shape), _res(win_t.shape), _row(tm, D_MODEL)],
        out_specs=[_row(tm, D_MODEL), _row(tm, D_QKV), _row(tm, D_MODEL), _res((8, D_MODEL))],
        compiler_params=_params(),
    )(dq, dk, dv, *rope, x, mods, win_t, dres)


def rnn_in_bwd(dxr, dgt, x, mods, k, win_t, dres):
    s = x.shape[0]
    tm = _blk(s, TM_MM)

    def body(dxr_ref, dgt_ref, x_ref, mod_ref, w_ref, dres_ref, dx_ref, dz_ref, h_ref, sums_ref):
        @pl.when(pl.program_id(0) == 0)
        def _():
            sums_ref[...] = jnp.zeros(sums_ref.shape, F32)

        dz_ref[:, 0:D_RNN] = dxr_ref[...].astype(BF16)
        dz_ref[:, D_RNN:2 * D_RNN] = dgt_ref[...].astype(BF16)
        _in_bwd_tail(dz_ref[...], w_ref, x_ref, mod_ref, k, dres_ref, dx_ref, h_ref, sums_ref)

    return pl.pallas_call(
        body, name="rnn_in_bwd", grid=(s // tm,),
        out_shape=[jax.ShapeDtypeStruct((s, D_MODEL), F32), jax.ShapeDtypeStruct((s, 2 * D_RNN), BF16),
                   jax.ShapeDtypeStruct((s, D_MODEL), BF16), jax.ShapeDtypeStruct((8, D_MODEL), F32)],
        in_specs=[_row(tm, D_RNN), _row(tm, D_RNN), _row(tm, D_MODEL), _res(mods.shape), _res(win_t.shape),
                  _row(tm, D_MODEL)],
        out_specs=[_row(tm, D_MODEL), _row(tm, 2 * D_RNN), _row(tm, D_MODEL), _res((8, D_MODEL))],
        compiler_params=_params(),
    )(dxr, dgt, x, mods, win_t, dres)


def wgrad(a, b, name):
    s, m = a.shape
    n = b.shape[1]
    tm = next(t for t in (1024, 768, 512, 384, 256, 128) if m % t == 0)
    tk = _blk(s, TK_WG)
    nk = s // tk

    def body(a_ref, b_ref, o_ref, acc):
        kk = pl.program_id(1)

        @pl.when(kk == 0)
        def _():
            acc[...] = jnp.zeros(acc.shape, F32)

        acc[...] += _tn(a_ref[...], b_ref[...])

        @pl.when(kk == nk - 1)
        def _():
            o_ref[...] = acc[...].astype(BF16)

    out = pl.pallas_call(
        body, name=name, grid=(m // tm, nk),
        out_shape=jax.ShapeDtypeStruct((m, n), BF16),
        in_specs=[pl.BlockSpec((tk, tm), lambda i, kk: (kk, i)), pl.BlockSpec((tk, n), lambda i, kk: (kk, 0))],
        out_specs=pl.BlockSpec((tm, n), lambda i, kk: (i, 0)),
        scratch_shapes=[pltpu.VMEM((tm, n), F32)],
        compiler_params=_params(),
    )(a, b)
    return out.reshape(N_DEV, m // N_DEV, n)


def part_sum(parts, name):
    _, r, c = parts.shape
    tr = next(t for t in (256, 192, 128, 64, 32, 16, 8) if r % t == 0)

    def body(p_ref, o_ref):
        acc = p_ref[0].astype(F32)
        for j in range(1, N_DEV):
            acc = acc + p_ref[j].astype(F32)
        o_ref[...] = acc

    return pl.pallas_call(
        body, name=name, grid=(r // tr,),
        out_shape=jax.ShapeDtypeStruct((r, c), F32),
        in_specs=[pl.BlockSpec((N_DEV, tr, c), lambda i: (0, i, 0))],
        out_specs=pl.BlockSpec((tr, c), lambda i: (i, 0)),
        compiler_params=_params(),
    )(parts)


def adamw(w, g, m, v, name):
    shape = w.shape
    c = shape[-1]
    r = w.size // c
    w2, g2, m2, v2 = (t.reshape(r, c) for t in (w, g, m, v))
    tr = r if r * c <= 512 * 1024 else next(t for t in (512, 256, 128, 64, 32, 16, 8) if r % t == 0)

    def body(w_ref, g_ref, m_ref, v_ref, d_ref, nm_ref, nv_ref):
        gv = g_ref[...]
        nm = B1 * m_ref[...] + (1.0 - B1) * gv
        nv = B2 * v_ref[...] + (1.0 - B2) * (gv * gv)
        nm_ref[...] = nm
        nv_ref[...] = nv
        m_hat = nm / (1.0 - B1 ** STEP)
        v_hat = nv / (1.0 - B2 ** STEP)
        d_ref[...] = -LR * (m_hat / (jnp.sqrt(v_hat) + ADAM_EPS) + WD * w_ref[...])

    spec = pl.BlockSpec((tr, c), lambda i: (i, 0))
    outs = pl.pallas_call(
        body, name=name, grid=(r // tr,),
        out_shape=[jax.ShapeDtypeStruct((r, c), F32)] * 3,
        in_specs=[spec] * 4, out_specs=[spec] * 3,
        compiler_params=_params(),
    )(w2, g2, m2, v2)
    return tuple(o.reshape(shape) for o in outs)


def _rope_tables(s):
    pos = jnp.arange(s, dtype=F32)
    inv_freq = THETA ** (-jnp.arange(0, ROT, 2, dtype=F32) / ROT)
    ang = pos[:, None] * inv_freq[None, :]
    cos, sin = jnp.cos(ang), jnp.sin(ang)
    half = ROT // 2
    zeros = jnp.zeros((s, HEAD - ROT), F32)
    c = jnp.concatenate([cos, cos, jnp.ones((s, HEAD - ROT), F32)], axis=1)
    s1 = jnp.concatenate([jnp.zeros((s, half), F32), sin, zeros], axis=1)
    s2 = jnp.concatenate([-sin, jnp.zeros((s, half), F32), zeros], axis=1)
    return c, s1, s2


def _blockdiag(w):
    w4 = w.reshape(N_CG, 4, RB_W, RB_W)
    eye = jnp.eye(4, dtype=w.dtype)
    return jnp.einsum("gipq,ij->gipjq", w4, eye).reshape(N_CG, CG, CG)


def _diag_blocks(w):
    w5 = w.reshape(N_CG, 4, RB_W, 4, RB_W)
    eye = jnp.eye(4, dtype=w.dtype)
    return jnp.einsum("gipjq,ij->gipq", w5, eye).reshape(N_RB, RB_W, RB_W)


def _cols(full, per):
    lead = full.shape[:-1]
    t = full.reshape(lead + (N_DEV, per))
    return jnp.moveaxis(t, -2, 0).reshape(N_DEV, -1)


def kernel(x, c, ada_w, ada_b, ln_g, ln_b, attn_w_in, attn_w_out, attn_sinks, rnn_w_in, rnn_conv_w, rnn_conv_b, rnn_w_a, rnn_b_a, rnn_w_x, rnn_b_x, rnn_lam, rnn_w_out, mlp_w1, mlp_w2, loss_target, m_ada_w, m_ada_b, m_ln_g, m_ln_b, m_attn_w_in, m_attn_w_out, m_attn_sinks, m_rnn_w_in, m_rnn_conv_w, m_rnn_conv_b, m_rnn_w_a, m_rnn_b_a, m_rnn_w_x, m_rnn_b_x, m_rnn_lam, m_rnn_w_out, m_mlp_w1, m_mlp_w2, v_ada_w, v_ada_b, v_ln_g, v_ln_b, v_attn_w_in, v_attn_w_out, v_attn_sinks, v_rnn_w_in, v_rnn_conv_w, v_rnn_conv_b, v_rnn_w_a, v_rnn_b_a, v_rnn_w_x, v_rnn_b_x, v_rnn_lam, v_rnn_w_out, v_mlp_w1, v_mlp_w2):
    s = x.shape[1]
    x0 = x.reshape(s, D_MODEL)
    target = loss_target.reshape(s, D_MODEL)
    weights = dict(ada_w=ada_w, ada_b=ada_b, ln_g=ln_g, ln_b=ln_b, attn_w_in=attn_w_in, attn_w_out=attn_w_out,
                   attn_sinks=attn_sinks, rnn_w_in=rnn_w_in, rnn_conv_w=rnn_conv_w, rnn_conv_b=rnn_conv_b,
                   rnn_w_a=rnn_w_a, rnn_b_a=rnn_b_a, rnn_w_x=rnn_w_x, rnn_b_x=rnn_b_x, rnn_lam=rnn_lam,
                   rnn_w_out=rnn_w_out, mlp_w1=mlp_w1, mlp_w2=mlp_w2)
    moments_m = dict(ada_w=m_ada_w, ada_b=m_ada_b, ln_g=m_ln_g, ln_b=m_ln_b, attn_w_in=m_attn_w_in,
                     attn_w_out=m_attn_w_out, attn_sinks=m_attn_sinks, rnn_w_in=m_rnn_w_in,
                     rnn_conv_w=m_rnn_conv_w, rnn_conv_b=m_rnn_conv_b, rnn_w_a=m_rnn_w_a, rnn_b_a=m_rnn_b_a,
                     rnn_w_x=m_rnn_w_x, rnn_b_x=m_rnn_b_x, rnn_lam=m_rnn_lam, rnn_w_out=m_rnn_w_out,
                     mlp_w1=m_mlp_w1, mlp_w2=m_mlp_w2)
    moments_v = dict(ada_w=v_ada_w, ada_b=v_ada_b, ln_g=v_ln_g, ln_b=v_ln_b, attn_w_in=v_attn_w_in,
                     attn_w_out=v_attn_w_out, attn_sinks=v_attn_sinks, rnn_w_in=v_rnn_w_in,
                     rnn_conv_w=v_rnn_conv_w, rnn_conv_b=v_rnn_conv_b, rnn_w_a=v_rnn_w_a, rnn_b_a=v_rnn_b_a,
                     rnn_w_x=v_rnn_w_x, rnn_b_x=v_rnn_b_x, rnn_lam=v_rnn_lam, rnn_w_out=v_rnn_w_out,
                     mlp_w1=v_mlp_w1, mlp_w2=v_mlp_w2)
    names = list(weights)

    def t16(w):
        return w.T.astype(BF16)

    big = [t16(attn_w_in[0]), attn_w_out[0].astype(BF16), t16(rnn_w_in[0]), rnn_w_out[0].astype(BF16),
           t16(mlp_w1[0]), mlp_w2[0].astype(BF16), t16(mlp_w1[1]), mlp_w2[1].astype(BF16)]
    small_local = jnp.concatenate([
        ln_g.reshape(-1), ln_b.reshape(-1), rnn_conv_w.reshape(-1), rnn_conv_b.reshape(-1),
        rnn_b_a.reshape(-1), rnn_b_x.reshape(-1), rnn_lam.reshape(-1)])
    small_local = jnp.pad(small_local, (0, 4096 - small_local.shape[0])).reshape(32, 128)
    flat = lambda g: g.reshape(N_DEV * g.shape[1], D_MODEL)
    first = exchange(_Gather([big[0], big[1], small_local]), "weight_gather")
    win_t, wout = flat(first[0]), flat(first[1])
    sm = first[2].reshape(N_DEV, 4096)

    def full_vec(off, rows, per):
        piece = sm[:, off:off + rows * per].reshape(N_DEV, rows, per)
        return jnp.moveaxis(piece, 0, 1).reshape(rows, N_DEV * per)

    lng_f, lnb_f = full_vec(0, 4, 128), full_vec(512, 4, 128)
    cw_f, cb_f = full_vec(1024, 4, 192), full_vec(1792, 1, 192)
    ba_f, bx_f, lam_f = full_vec(1984, 2, 192), full_vec(2368, 2, 192), full_vec(2752, 2, 192)
    wa_bd = [_blockdiag(rnn_w_a[0, d]).astype(BF16) for d in range(2)]
    wx_bd = [_blockdiag(rnn_w_x[0, d]).astype(BF16) for d in range(2)]

    c_all, modr = ada_modulation(jnp.broadcast_to(c, (8, D_MODEL)), ada_w.reshape(4, D_MODEL, CG),
                                 ada_b.reshape(4, 1, CG))
    mods = modr.reshape(N_DEV, 4, 8, CG)[:, :, 0, :]
    mods = jnp.moveaxis(mods, 0, 1).reshape(4, 3, D_MODEL).reshape(12, D_MODEL)
    rope = _rope_tables(s)
    ln = lambda k: (lng_f[k:k + 1], lnb_f[k:k + 1])

    q, kk, v = attn_in_fwd(x0, mods, 0, win_t, rope)
    o, *got = attn_fwd(q, kk, v, attn_sinks, rider=_Gather([big[4], big[5], big[2], big[3]]))
    w1t_0, w2_0, rin_t, rout = (flat(g) for g in got)
    x1, y0 = post_fwd(o, wout, x0, mods, 0, *ln(0))
    x2, y1, ra0, r0, *got = mlp_fwd(x1, mods, 1, w1t_0, w2_0, *ln(1), rider=_Gather([big[6], big[7]]))
    w1t_1, w2_1 = (flat(g) for g in got)
    xr, gt = rnn_in_fwd(x2, mods, 2, rin_t)
    xc = conv_fwd(xr, cw_f, cb_f)
    hf = lru_fwd(xc, wa_bd[0], wx_bd[0], ba_f[0:1], bx_f[0:1], lam_f[0:1], False)
    hb = lru_fwd(xc, wa_bd[1], wx_bd[1], ba_f[1:2], bx_f[1:2], lam_f[1:2], True)
    x3, y2, ypre = post_fwd(None, rout, x2, mods, 2, *ln(2), gate_act=(gt, hf, hb))
    x4, y3, ra1, r1 = mlp_fwd(x3, mods, 3, w1t_1, w2_1, *ln(3))
    dx4, sq = loss_grad(x4, target)
    loss = lax.psum(0.5 * jnp.sum(sq) / D_MODEL, ("x", "y", "c"))

    dx3, da1, h3, dy3, sums3 = mlp_bwd(dx4, x3, y3, ra1, mods, 3, w1t_1, w2_1, lng_f[3:4])
    g_w1t_1 = wgrad(da1, h3, "wgrad_w1_1")
    g_w2_1 = wgrad(r1, dy3, "wgrad_w2_1")
    dres2, dy2, sums2a, dhs, dgt, p_w1t_1 = post_bwd(dx3, x2, y2, mods, 2, rout, lng_f[2:3], gate_act=(gt, hf, hb),
                                                     rider=_AllToAll([g_w1t_1]))
    g_rout = wgrad(ypre, dy2, "wgrad_rnn_out")
    dxc_f, dwa_f, dwx_f, dba_f, dbx_f, dlam_f, p_w2_1, p_rout = lru_bwd(
        xc, dhs, hf, wa_bd[0], wx_bd[0], ba_f[0:1], bx_f[0:1], lam_f[0:1], False, rider=_AllToAll([g_w2_1, g_rout]))
    dxc_b, dwa_b, dwx_b, dba_b, dbx_b, dlam_b = lru_bwd(xc, dhs, hb, wa_bd[1], wx_bd[1], ba_f[1:2], bx_f[1:2],
                                                        lam_f[1:2], True)
    dxr, dcw, dcb = conv_bwd(dxc_f, dxc_b, xr, cw_f)
    dx2, dzz, h2, sums2b = rnn_in_bwd(dxr, dgt, x2, mods, 2, rin_t, dres2)
    g_rin_t = wgrad(dzz, h2, "wgrad_rnn_in")
    dx1, da0, h1, dy1, sums1, p_rin_t = mlp_bwd(dx2, x1, y1, ra0, mods, 1, w1t_0, w2_0, lng_f[1:2],
                                                rider=_AllToAll([g_rin_t]))
    g_w1t_0 = wgrad(da0, h1, "wgrad_w1_0")
    g_w2_0 = wgrad(r0, dy1, "wgrad_w2_0")
    dres0, dy0, sums0a, do = post_bwd(dx1, x0, y0, mods, 0, wout, lng_f[0:1])
    g_wout = wgrad(o, dy0, "wgrad_attn_out")
    dq, dkp, dvp, dsink, p_w1t_0, p_w2_0 = attn_bwd(q, kk, v, do, attn_sinks, rider=_AllToAll([g_w1t_0, g_w2_0]))
    dk, dv, p_wout = kv_combine(dkp, dvp, rider=_AllToAll([g_wout]))
    dx0, dqkv, h0, sums0b = attn_in_bwd(dq, dk, dv, rope, x0, mods, 0, win_t, dres0)
    g_win_t = wgrad(dqkv, h0, "wgrad_attn_in")
    p_win_t, = exchange(_AllToAll([g_win_t]), "grad_exchange")

    big_parts = [p_win_t, p_wout, p_rin_t, p_rout, p_w1t_0, p_w2_0, p_w1t_1, p_w2_1]
    gsum = [part_sum(p, "part_sum_%d" % i) for i, p in enumerate(big_parts)]
    grads = {
        "attn_w_in": gsum[0].T[None], "attn_w_out": gsum[1][None],
        "rnn_w_in": gsum[2].T[None], "rnn_w_out": gsum[3][None],
        "mlp_w1": jnp.stack([gsum[4].T, gsum[6].T]), "mlp_w2": jnp.stack([gsum[5], gsum[7]]),
    }

    sums = [sums0a + sums0b, sums1, sums2a + sums2b, sums3]
    gmod = jnp.stack([t[0:3] for t in sums])
    gsend = jnp.moveaxis(gmod.reshape(4, N_DEV, CG), 1, 0)
    gsend = jnp.pad(gsend, ((0, 0), (0, 4), (0, 0)))
    c_t = c_all[:, 0, :].T
    g_ada_w, g_ada_b = ada_grads(gsend, c_t)
    grads["ada_w"] = g_ada_w.reshape(ada_w.shape)
    grads["ada_b"] = g_ada_b[0:4].reshape(ada_b.shape)

    d_wa = jnp.stack([_diag_blocks(dwa_f), _diag_blocks(dwa_b)])
    d_wx = jnp.stack([_diag_blocks(dwx_f), _diag_blocks(dwx_b)])
    nflat = d_wa.size // N_DEV
    tail = jnp.concatenate([
        _cols(dcw, 192), _cols(dcb, 192),
        _cols(jnp.concatenate([dba_f, dba_b]), 192), _cols(jnp.concatenate([dbx_f, dbx_b]), 192),
        _cols(jnp.concatenate([dlam_f, dlam_b]), 192),
        _cols(jnp.stack([t[3] for t in sums]), 128), _cols(jnp.stack([t[4] for t in sums]), 128),
        jnp.broadcast_to(dsink[:, 0:8], (N_DEV, 8))], axis=1)
    tail = jnp.pad(tail, ((0, 0), (0, 32 * 128 - tail.shape[1])))
    ssend = jnp.concatenate([d_wa.reshape(N_DEV, nflat), d_wx.reshape(N_DEV, nflat), tail], axis=1)
    rw = 2 * nflat // 128
    red, wag = small_reduce(ssend.reshape(N_DEV, rw + 32, 128), rw)
    wag = wag.reshape(N_DEV, 2 * nflat)
    grads["rnn_w_a"] = wag[:, :nflat].reshape(rnn_w_a.shape)
    grads["rnn_w_x"] = wag[:, nflat:].reshape(rnn_w_x.shape)
    tl = red[rw:].reshape(-1)
    grads["rnn_conv_w"] = tl[0:768].reshape(rnn_conv_w.shape)
    grads["rnn_conv_b"] = tl[768:960].reshape(rnn_conv_b.shape)
    grads["rnn_b_a"] = tl[960:1344].reshape(rnn_b_a.shape)
    grads["rnn_b_x"] = tl[1344:1728].reshape(rnn_b_x.shape)
    grads["rnn_lam"] = tl[1728:2112].reshape(rnn_lam.shape)
    grads["ln_g"] = tl[2112:2624].reshape(ln_g.shape)
    grads["ln_b"] = tl[2624:3136].reshape(ln_b.shape)
    grads["attn_sinks"] = tl[3136:3144].reshape(attn_sinks.shape)

    delta, new_m, new_v = {}, {}, {}
    for n in names:
        delta[n], new_m[n], new_v[n] = adamw(weights[n], grads[n], moments_m[n], moments_v[n], "adamw_" + n)
    return (loss, dx0.reshape(x.shape), *[grads[n] for n in names], *[delta[n] for n in names],
            *[new_m[n] for n in names], *[new_v[n] for n in names])
```

```python
import functools
import math

import jax
import jax.numpy as jnp
from jax import lax
from jax.experimental import pallas as pl
from jax.experimental.pallas import tpu as pltpu

F32, BF16 = jnp.float32, jnp.bfloat16
MESH = pl.DeviceIdType.MESH

D_MODEL = 1024
N_Q, N_KV, HEAD = 8, 2, 128
ROT, THETA = 32, 500000.0
QBLK = 128
D_QKV = (N_Q + 2 * N_KV) * HEAD
D_RNN, N_RB, RB_W = 1536, 16, 96
CG = 384
N_CG = D_RNN // CG
D_FF = 4096
FF_CHUNK = 1024
DEPTH = 2
ALPHA = (2.0 * DEPTH) ** 0.25
LN_EPS = 1e-5
LRU_C = 8.0
N_DEV = 8
LR, B1, B2, ADAM_EPS, WD, STEP = 0.001, 0.9, 0.999, 1e-8, 0.01, 10

VMEM_LIMIT = 56 * 1024 * 1024
TM_MM = 512
TM_MLP = 256
TT_RNN = 512
SB_RNN = 512
TK_WG = 2048


def _nn(a, b):
    return jnp.dot(a, b, preferred_element_type=F32)


def _nt(a, b):
    return lax.dot_general(a, b, (((1,), (1,)), ((), ())), preferred_element_type=F32)


def _tn(a, b):
    return lax.dot_general(a, b, (((0,), (0,)), ((), ())), preferred_element_type=F32)


def _blk(n, pref):
    t = min(n, pref)
    assert n % t == 0, (n, pref)
    return t


def _params(**kw):
    return pltpu.CompilerParams(vmem_limit_bytes=VMEM_LIMIT, **kw)


def _row(tm, w):
    return pl.BlockSpec((tm, w), lambda i: (i, 0))


def _res(shape):
    return pl.BlockSpec(shape, lambda i: (0,) * len(shape), pipeline_mode=pl.Buffered(1))


def _mod(mod_ref, k):
    return mod_ref[3 * k:3 * k + 1, :], mod_ref[3 * k + 1:3 * k + 2, :], mod_ref[3 * k + 2:3 * k + 3, :]


def _ln_stats(z):
    mu = jnp.mean(z, axis=-1, keepdims=True)
    zc = z - mu
    var = jnp.mean(zc * zc, axis=-1, keepdims=True)
    rstd = lax.rsqrt(var + LN_EPS)
    return zc * rstd, rstd


def _ln_bwd(dxo, xhat, rstd, g):
    dxh = dxo * g
    m1 = jnp.mean(dxh, axis=-1, keepdims=True)
    m2 = jnp.mean(dxh * xhat, axis=-1, keepdims=True)
    return rstd * (dxh - m1 - xhat * m2)


def _colsum(v):
    return jnp.sum(v, axis=0, keepdims=True)


def _sigmoid(v):
    return 0.5 * jnp.tanh(0.5 * v) + 0.5


def _gelu_parts(v):
    k = math.sqrt(2.0 / math.pi)
    u = k * (v + 0.044715 * v * v * v)
    t = jnp.tanh(u)
    g = 0.5 * v * (1.0 + t)
    dg = 0.5 * (1.0 + t) + 0.5 * v * (1.0 - t * t) * k * (1.0 + 3.0 * 0.044715 * v * v)
    return g, dg


def _me():
    return lax.axis_index("x"), lax.axis_index("y"), lax.axis_index("c")


def _idx(p):
    return 4 * p[0] + 2 * p[1] + p[2]


def _peers(me):
    x, y, c = me
    out = []
    for k in range(1, N_DEV):
        out.append((1 - x if k & 4 else x, 1 - y if k & 2 else y, 1 - c if k & 1 else c))
    return out


class _Gather:
    def __init__(self, srcs):
        self.srcs = list(srcs)
        n = len(self.srcs)
        self.out_shape = [jax.ShapeDtypeStruct((N_DEV,) + s.shape, s.dtype) for s in self.srcs]
        self.scratch = [pltpu.SemaphoreType.DMA((n, 7)), pltpu.SemaphoreType.DMA((n, 7)),
                        pltpu.SemaphoreType.DMA((n,))]

    @staticmethod
    def _places():
        x, y, c = me = _me()
        return me, (x, y, 1 - c), [(1 - x, y), (x, 1 - y), (1 - x, 1 - y)]

    @staticmethod
    def _copy(outs, sems, t, k, block, to, src=None):
        slot = outs[t].at[_idx(block)]
        return pltpu.make_async_remote_copy(
            src_ref=slot if src is None else src, dst_ref=slot, send_sem=sems[0].at[t, k],
            recv_sem=sems[1].at[t, k], device_id=to, device_id_type=MESH)

    def _firsts(self, ins, outs, sems):
        me, sibling, chips = self._places()
        out = []
        for t in range(len(ins)):
            out.append(self._copy(outs, sems, t, 0, me, sibling, src=ins[t]))
            out += [self._copy(outs, sems, t, 1 + j, me, (*chip, me[2]), src=ins[t]) for j, chip in enumerate(chips)]
        return out

    def _locals(self, ins, outs, sems):
        me = _me()
        return [pltpu.make_async_copy(ins[t], outs[t].at[_idx(me)], sems[2].at[t]) for t in range(len(ins))]

    def start(self, ins, outs, sems):
        for cp in self._locals(ins, outs, sems) + self._firsts(ins, outs, sems):
            cp.start()

    def mid(self, ins, outs, sems):
        me, sibling, chips = self._places()
        for j, chip in enumerate(chips):
            for t in range(len(ins)):
                self._copy(outs, sems, t, 1 + j, (*chip, me[2]), me).wait_recv()
                self._copy(outs, sems, t, 4 + j, (*chip, me[2]), sibling).start()

    def finish(self, ins, outs, sems):
        me, sibling, chips = self._places()
        for t in range(len(ins)):
            self._copy(outs, sems, t, 0, sibling, me).wait_recv()
            for j, chip in enumerate(chips):
                self._copy(outs, sems, t, 4 + j, (*chip, 1 - me[2]), me).wait_recv()
        for cp in self._firsts(ins, outs, sems):
            cp.wait_send()
        for j, chip in enumerate(chips):
            for t in range(len(ins)):
                self._copy(outs, sems, t, 4 + j, (*chip, me[2]), sibling).wait_send()
        for cp in self._locals(ins, outs, sems):
            cp.wait()


class _AllToAll:
    def __init__(self, srcs):
        self.srcs = list(srcs)
        n = len(self.srcs)
        self.out_shape = [jax.ShapeDtypeStruct(s.shape, s.dtype) for s in self.srcs]
        self.scratch = [pltpu.SemaphoreType.DMA((n, 7)), pltpu.SemaphoreType.DMA((n, 7)),
                        pltpu.SemaphoreType.DMA((n,))]

    def _copies(self, ins, outs, sems):
        me = _me()
        loc, rem = [], []
        for t in range(len(ins)):
            loc.append(pltpu.make_async_copy(ins[t].at[_idx(me)], outs[t].at[_idx(me)], sems[2].at[t]))
            for k, p in enumerate(_peers(me)):
                rem.append(pltpu.make_async_remote_copy(
                    src_ref=ins[t].at[_idx(p)], dst_ref=outs[t].at[_idx(me)], send_sem=sems[0].at[t, k],
                    recv_sem=sems[1].at[t, k], device_id=p, device_id_type=MESH))
        return loc, rem

    def start(self, ins, outs, sems):
        loc, rem = self._copies(ins, outs, sems)
        for cp in loc + rem:
            cp.start()

    def mid(self, ins, outs, sems):
        pass

    def finish(self, ins, outs, sems):
        me = _me()
        for t in range(len(ins)):
            for k, p in enumerate(_peers(me)):
                slot = outs[t].at[_idx(p)]
                pltpu.make_async_remote_copy(
                    src_ref=slot, dst_ref=slot, send_sem=sems[0].at[t, k], recv_sem=sems[1].at[t, k],
                    device_id=p, device_id_type=MESH).wait_recv()
        loc, rem = self._copies(ins, outs, sems)
        for cp in rem:
            cp.wait_send()
        for cp in loc:
            cp.wait()


def exchange(ex, name):
    n = len(ex.srcs)

    def body(*refs):
        ins, outs, sems = refs[:n], refs[n:2 * n], refs[2 * n:]
        ex.start(ins, outs, sems)
        ex.mid(ins, outs, sems)
        ex.finish(ins, outs, sems)

    any_spec = pl.BlockSpec(memory_space=pl.ANY)
    return pl.pallas_call(
        body, name=name, out_shape=ex.out_shape, in_specs=[any_spec] * n, out_specs=[any_spec] * n,
        scratch_shapes=ex.scratch,
    )(*ex.srcs)


def _call(body, *, name, grid, in_specs, out_specs, out_shape, args, scratch_shapes=(), rider=None):
    in_specs, out_specs, out_shape = list(in_specs), list(out_specs), list(out_shape)
    scratch_shapes = list(scratch_shapes)
    if rider is None:
        return pl.pallas_call(body, name=name, grid=grid, out_shape=out_shape, in_specs=in_specs,
                              out_specs=out_specs, scratch_shapes=scratch_shapes, compiler_params=_params())(*args)
    nci, nco, ncs, nr = len(in_specs), len(out_shape), len(scratch_shapes), len(rider.srcs)
    nsteps = math.prod(grid)
    mid = min((3 * nsteps) // 4, nsteps - 2)
    assert 0 < mid, (name, grid)

    def full(*refs):
        ci, ri = refs[:nci], refs[nci:nci + nr]
        co, ro = refs[nci + nr:nci + nr + nco], refs[nci + nr + nco:nci + 2 * nr + nco]
        cs, rs = refs[nci + 2 * nr + nco:nci + 2 * nr + nco + ncs], refs[nci + 2 * nr + nco + ncs:]
        step = pl.program_id(0)
        for d in range(1, len(grid)):
            step = step * grid[d] + pl.program_id(d)

        @pl.when(step == 0)
        def _():
            rider.start(ri, ro, rs)

        @pl.when(step == mid)
        def _():
            rider.mid(ri, ro, rs)

        body(*ci, *co, *cs)

        @pl.when(step == nsteps - 1)
        def _():
            rider.finish(ri, ro, rs)

    any_spec = pl.BlockSpec(memory_space=pl.ANY)
    return pl.pallas_call(
        full, name=name, grid=grid, out_shape=out_shape + rider.out_shape,
        in_specs=in_specs + [any_spec] * nr, out_specs=out_specs + [any_spec] * nr,
        scratch_shapes=scratch_shapes + rider.scratch, compiler_params=_params(),
    )(*args, *rider.srcs)


def _a2a_start(srcs, dsts, send_sems, recv_sems, local_sems, me, sem_base=0):
    peers = _peers(me)
    started = []
    for t in range(len(srcs)):
        loc = pltpu.make_async_copy(srcs[t].at[_idx(me)], dsts[t].at[_idx(me)], local_sems.at[sem_base + t])
        loc.start()
        started.append(("local", loc))
        for k, p in enumerate(peers):
            cp = pltpu.make_async_remote_copy(
                src_ref=srcs[t].at[_idx(p)], dst_ref=dsts[t].at[_idx(me)],
                send_sem=send_sems.at[sem_base + t, k], recv_sem=recv_sems.at[sem_base + t, k],
                device_id=p, device_id_type=MESH)
            cp.start()
            started.append(("remote", cp))
    return started


def _a2a_finish(started, dsts, send_sems, recv_sems, me, sem_base=0):
    peers = _peers(me)
    for t in range(len(dsts)):
        for k, p in enumerate(peers):
            slot = dsts[t].at[_idx(p)]
            pltpu.make_async_remote_copy(
                src_ref=slot, dst_ref=slot, send_sem=send_sems.at[sem_base + t, k],
                recv_sem=recv_sems.at[sem_base + t, k], device_id=p, device_id_type=MESH).wait_recv()
    for kind, cp in started:
        if kind == "local":
            cp.wait()
        else:
            cp.wait_send()


def ada_modulation(c8, ada_w, ada_b):
    def body(c_ref, w_ref, b_ref, call_ref, modr_ref, modp, send_sems, recv_sems, local_sems):
        me = _me()
        for j in range(N_DEV):
            modp[j] = jnp.zeros(modp.shape[1:], F32)
        peers = _peers(me)
        sends = []
        for k, p in enumerate(peers):
            cp = pltpu.make_async_remote_copy(
                src_ref=c_ref, dst_ref=call_ref.at[_idx(me)], send_sem=send_sems.at[0, k],
                recv_sem=recv_sems.at[0, k], device_id=p, device_id_type=MESH)
            cp.start()
            sends.append(cp)
        call_ref[_idx(me)] = c_ref[...]
        for k, p in enumerate(peers):
            slot = call_ref.at[_idx(p)]
            pltpu.make_async_remote_copy(
                src_ref=slot, dst_ref=slot, send_sem=send_sems.at[0, k], recv_sem=recv_sems.at[0, k],
                device_id=p, device_id_type=MESH).wait_recv()
        for cp in sends:
            cp.wait_send()
        cv = call_ref[...].reshape(N_DEV * 8, D_MODEL)
        s = (cv * _sigmoid(cv)).astype(BF16)
        for k in range(4):
            res = _nn(s, w_ref[k].astype(BF16)) + b_ref[k]
            for j in range(N_DEV):
                modp[j, 8 * k:8 * k + 8, :] = res[8 * j:8 * j + 8, :]
        started = _a2a_start([modp], [modr_ref], send_sems, recv_sems, local_sems, me, sem_base=1)
        _a2a_finish(started, [modr_ref], send_sems, recv_sems, me, sem_base=1)

    vm = pl.BlockSpec(memory_space=pltpu.VMEM)
    return pl.pallas_call(
        body, name="ada_modulation",
        out_shape=[jax.ShapeDtypeStruct((N_DEV, 8, D_MODEL), F32), jax.ShapeDtypeStruct((N_DEV, 32, CG), F32)],
        in_specs=[vm, vm, vm], out_specs=[vm, vm],
        scratch_shapes=[pltpu.VMEM((N_DEV, 32, CG), F32), pltpu.SemaphoreType.DMA((2, 7)),
                        pltpu.SemaphoreType.DMA((2, 7)), pltpu.SemaphoreType.DMA((2,))],
        compiler_params=_params(),
    )(c8, ada_w, ada_b)


def ada_grads(gsend, c_t):
    def body(g_ref, ct_ref, gw_ref, gb_ref, grecv, send_sems, recv_sems, local_sems):
        me = _me()
        started = _a2a_start([g_ref], [grecv], send_sems, recv_sems, local_sems, me)
        _a2a_finish(started, [grecv], send_sems, recv_sems, me)
        ct = ct_ref[...]
        st = (ct * _sigmoid(ct)).astype(BF16).astype(F32)
        gb = jnp.zeros((8, CG), F32)
        for b in range(N_DEV):
            gb = gb + grecv[b]
        gb_ref[...] = gb
        for k in range(4):
            acc = jnp.zeros((D_MODEL, CG), F32)
            for b in range(N_DEV):
                row = grecv[b, k:k + 1, :].astype(BF16).astype(F32)
                acc = acc + st[:, b:b + 1] * row
            gw_ref[k] = acc

    vm = pl.BlockSpec(memory_space=pltpu.VMEM)
    return pl.pallas_call(
        body, name="ada_grads",
        out_shape=[jax.ShapeDtypeStruct((4, D_MODEL, CG), F32), jax.ShapeDtypeStruct((8, CG), F32)],
        in_specs=[vm, vm], out_specs=[vm, vm],
        scratch_shapes=[pltpu.VMEM((N_DEV, 8, CG), F32), pltpu.SemaphoreType.DMA((1, 7)),
                        pltpu.SemaphoreType.DMA((1, 7)), pltpu.SemaphoreType.DMA((1,))],
        compiler_params=_params(),
    )(gsend, c_t)


def small_reduce(ssend, rw):
    r = ssend.shape[1]

    def body(s_ref, red_ref, wag_ref, recv, send_sems, recv_sems, local_sems):
        me = _me()
        started = _a2a_start([s_ref], [recv], send_sems, recv_sems, local_sems, me)
        _a2a_finish(started, [recv], send_sems, recv_sems, me)
        acc = recv[0]
        for j in range(1, N_DEV):
            acc = acc + recv[j]
        red_ref[...] = acc
        peers = _peers(me)
        top = red_ref.at[pl.ds(0, rw), :]
        sends = []
        for k, p in enumerate(peers):
            cp = pltpu.make_async_remote_copy(
                src_ref=top, dst_ref=wag_ref.at[_idx(me)], send_sem=send_sems.at[1, k],
                recv_sem=recv_sems.at[1, k], device_id=p, device_id_type=MESH)
            cp.start()
            sends.append(cp)
        wag_ref[_idx(me)] = acc[0:rw, :]
        for k, p in enumerate(peers):
            slot = wag_ref.at[_idx(p)]
            pltpu.make_async_remote_copy(
                src_ref=slot, dst_ref=slot, send_sem=send_sems.at[1, k], recv_sem=recv_sems.at[1, k],
                device_id=p, device_id_type=MESH).wait_recv()
        for cp in sends:
            cp.wait_send()

    vm = pl.BlockSpec(memory_space=pltpu.VMEM)
    return pl.pallas_call(
        body, name="small_reduce",
        out_shape=[jax.ShapeDtypeStruct((r, 128), F32), jax.ShapeDtypeStruct((N_DEV, rw, 128), F32)],
        in_specs=[vm], out_specs=[vm, vm],
        scratch_shapes=[pltpu.VMEM((N_DEV, r, 128), F32), pltpu.SemaphoreType.DMA((2, 7)),
                        pltpu.SemaphoreType.DMA((2, 7)), pltpu.SemaphoreType.DMA((1,))],
        compiler_params=_params(),
    )(ssend)


def _rope(t, cos, s1, s2):
    return t * cos + pltpu.roll(t, 16, 1) * s1 + pltpu.roll(t, HEAD - 16, 1) * s2


def _rope_bwd(d, cos, s1, s2):
    return d * cos + pltpu.roll(d * s1, HEAD - 16, 1) + pltpu.roll(d * s2, 16, 1)


def attn_in_fwd(x, mods, k, win_t, rope):
    s = x.shape[0]
    tm = _blk(s, TM_MM)

    def body(x_ref, mod_ref, w_ref, c_ref, s1_ref, s2_ref, q_ref, k_ref, v_ref):
        shift, scale, _ = _mod(mod_ref, k)
        h = (x_ref[...] * (1.0 + scale) + shift).astype(BF16)
        qkv = _nt(h, w_ref[...])
        cos, s1, s2 = c_ref[...], s1_ref[...], s2_ref[...]
        for hh in range(N_Q + N_KV):
            r = _rope(qkv[:, HEAD * hh:HEAD * (hh + 1)], cos, s1, s2).astype(BF16)
            if hh < N_Q:
                q_ref[:, HEAD * hh:HEAD * (hh + 1)] = r
            else:
                k_ref[:, HEAD * (hh - N_Q):HEAD * (hh - N_Q + 1)] = r
        v_ref[...] = qkv[:, HEAD * (N_Q + N_KV):].astype(BF16)

    return pl.pallas_call(
        body, name="attn_in_fwd", grid=(s // tm,),
        out_shape=[jax.ShapeDtypeStruct((s, N_Q * HEAD), BF16), jax.ShapeDtypeStruct((s, N_KV * HEAD), BF16),
                   jax.ShapeDtypeStruct((s, N_KV * HEAD), BF16)],
        in_specs=[_row(tm, D_MODEL), _res(mods.shape), _res(win_t.shape),
                  _row(tm, HEAD), _row(tm, HEAD), _row(tm, HEAD)],
        out_specs=[_row(tm, N_Q * HEAD), _row(tm, N_KV * HEAD), _row(tm, N_KV * HEAD)],
        compiler_params=_params(),
    )(x, mods, win_t, *rope)


def _kv_specs(nblk):
    w = N_KV * HEAD
    return [pl.BlockSpec((QBLK, w), lambda n: (jnp.maximum(n - 1, 0), 0)),
            pl.BlockSpec((QBLK, w), lambda n: (n, 0)),
            pl.BlockSpec((QBLK, w), lambda n: (jnp.minimum(n + 1, nblk - 1), 0))]


GROUP = N_Q // N_KV


def _attn_mask(n, s):
    qi = lax.broadcasted_iota(jnp.int32, (GROUP * QBLK, 3 * QBLK), 0) & (QBLK - 1)
    kj = lax.broadcasted_iota(jnp.int32, (GROUP * QBLK, 3 * QBLK), 1)
    rel = kj - QBLK - qi
    kpos = kj + (n - 1) * QBLK
    return (jnp.abs(rel) <= QBLK) & (kpos >= 0) & (kpos < s)


def _stack_heads(ref, kv):
    return jnp.concatenate([ref[:, HEAD * (GROUP * kv + j):HEAD * (GROUP * kv + j + 1)] for j in range(GROUP)], axis=0)


def _stack_sinks(sink_ref, kv):
    row = lax.broadcasted_iota(jnp.int32, (GROUP * QBLK, 1), 0)
    out = jnp.full((GROUP * QBLK, 1), sink_ref[0, GROUP * kv + GROUP - 1], F32)
    for j in range(GROUP - 2, -1, -1):
        out = jnp.where(row < QBLK * (j + 1), sink_ref[0, GROUP * kv + j], out)
    return out


def _attn_probs(qh, kh, valid, sink):
    sc = _nt(qh, kh) * (HEAD ** -0.5)
    sc = jnp.where(valid, sc, -1e30)
    m = jnp.maximum(jnp.max(sc, axis=-1, keepdims=True), sink)
    p = jnp.exp(sc - m)
    es = jnp.exp(sink - m)
    denom = jnp.sum(p, axis=-1, keepdims=True) + es
    return p / denom, es / denom


def attn_fwd(q, kk, v, sinks, rider=None):
    s = q.shape[0]
    nblk = s // QBLK

    def body(sink_ref, q_ref, kp, ko, kn, vp, vo, vn, o_ref):
        n = pl.program_id(0)
        kcat = jnp.concatenate([kp[...], ko[...], kn[...]], axis=0)
        vcat = jnp.concatenate([vp[...], vo[...], vn[...]], axis=0)
        valid = _attn_mask(n, s)
        for kv in range(N_KV):
            cols = slice(HEAD * kv, HEAD * (kv + 1))
            probs, _ = _attn_probs(_stack_heads(q_ref, kv), kcat[:, cols], valid, _stack_sinks(sink_ref, kv))
            og = _nn(probs.astype(BF16), vcat[:, cols]).astype(BF16)
            for j in range(GROUP):
                hq = GROUP * kv + j
                o_ref[:, HEAD * hq:HEAD * (hq + 1)] = og[QBLK * j:QBLK * (j + 1), :]

    return _call(
        body, name="attn_fwd", grid=(nblk,),
        out_shape=[jax.ShapeDtypeStruct((s, N_Q * HEAD), BF16)],
        in_specs=[pl.BlockSpec(memory_space=pltpu.SMEM), pl.BlockSpec((QBLK, N_Q * HEAD), lambda n: (n, 0))]
        + _kv_specs(nblk) + _kv_specs(nblk),
        out_specs=[pl.BlockSpec((QBLK, N_Q * HEAD), lambda n: (n, 0))],
        args=(sinks, q, kk, kk, kk, v, v, v), rider=rider)


def post_fwd(ypre, w, x, mods, k, lng, lnb, gate_act=None):
    s = x.shape[0]
    tm = _blk(s, TM_MM)
    kdim = w.shape[0]
    rnn = gate_act is not None

    def body(*refs):
        if rnn:
            gt_ref, hf_ref, hb_ref, w_ref, x_ref, mod_ref, g_ref, b_ref, xo_ref, y_ref, yp_ref = refs
            act, _ = _gelu_parts(gt_ref[...])
            yp = ((hf_ref[...] + hb_ref[...]) * act).astype(BF16)
            yp_ref[...] = yp
        else:
            yp_ref, w_ref, x_ref, mod_ref, g_ref, b_ref, xo_ref, y_ref = refs
            yp = yp_ref[...]
        _, _, gate = _mod(mod_ref, k)
        y = _nn(yp, w_ref[...])
        y_ref[...] = y
        xhat, _ = _ln_stats(ALPHA * x_ref[...] + (1.0 + gate) * y)
        xo_ref[...] = xhat * g_ref[...] + b_ref[...]

    act_in = list(gate_act) if rnn else [ypre]
    out_shape = [jax.ShapeDtypeStruct((s, D_MODEL), F32), jax.ShapeDtypeStruct((s, D_MODEL), F32)]
    out_specs = [_row(tm, D_MODEL), _row(tm, D_MODEL)]
    if rnn:
        out_shape.append(jax.ShapeDtypeStruct((s, kdim), BF16))
        out_specs.append(_row(tm, kdim))
    return pl.pallas_call(
        body, name="rnn_post_fwd" if rnn else "attn_post_fwd", grid=(s // tm,),
        out_shape=out_shape,
        in_specs=[_row(tm, kdim)] * len(act_in) + [_res(w.shape), _row(tm, D_MODEL), _res(mods.shape),
                                                    _res(lng.shape), _res(lnb.shape)],
        out_specs=out_specs,
        compiler_params=_params(),
    )(*act_in, w, x, mods, lng, lnb)


def mlp_fwd(x, mods, k, w1_t, w2, lng, lnb, rider=None, last=False):
    s = x.shape[0]
    tm = _blk(s, TM_MLP)

    def body(x_ref, mod_ref, w1_ref, w2_ref, g_ref, b_ref, *outs):
        xo_ref = None if last else outs[0]
        y_ref, ra_ref, r_ref = outs[-3:]
        xv = x_ref[...]
        shift, scale, gate = _mod(mod_ref, k)
        h = (xv * (1.0 + scale) + shift).astype(BF16)
        y = jnp.zeros((tm, D_MODEL), F32)
        for c in range(D_FF // FF_CHUNK):
            rows = slice(FF_CHUNK * c, FF_CHUNK * (c + 1))
            a = jnp.maximum(_nt(h, w1_ref[rows, :]), 0.0)
            r = (a * a).astype(BF16)
            ra_ref[:, rows] = a.astype(BF16)
            r_ref[:, rows] = r
            y = y + _nn(r, w2_ref[rows, :])
        y_ref[...] = y
        if not last:
            xhat, _ = _ln_stats(ALPHA * xv + (1.0 + gate) * y)
            xo_ref[...] = xhat * g_ref[...] + b_ref[...]

    nf = 1 if last else 2
    return _call(
        body, name="mlp_fwd_last" if last else "mlp_fwd", grid=(s // tm,),
        out_shape=[jax.ShapeDtypeStruct((s, D_MODEL), F32)] * nf + [jax.ShapeDtypeStruct((s, D_FF), BF16)] * 2,
        in_specs=[_row(tm, D_MODEL), _res(mods.shape), _res(w1_t.shape), _res(w2.shape),
                  _res(lng.shape), _res(lnb.shape)],
        out_specs=[_row(tm, D_MODEL)] * nf + [_row(tm, D_FF)] * 2,
        args=(x, mods, w1_t, w2, lng, lnb), rider=rider)


def rnn_in_fwd(x, mods, k, win_t):
    s = x.shape[0]
    tm = _blk(s, TM_MM)

    def body(x_ref, mod_ref, w_ref, xr_ref, gt_ref):
        shift, scale, _ = _mod(mod_ref, k)
        h = (x_ref[...] * (1.0 + scale) + shift).astype(BF16)
        xr_ref[...] = _nt(h, w_ref[0:D_RNN, :])
        gt_ref[...] = _nt(h, w_ref[D_RNN:2 * D_RNN, :])

    return pl.pallas_call(
        body, name="rnn_in_fwd", grid=(s // tm,),
        out_shape=[jax.ShapeDtypeStruct((s, D_RNN), F32)] * 2,
        in_specs=[_row(tm, D_MODEL), _res(mods.shape), _res(win_t.shape)],
        out_specs=[_row(tm, D_RNN)] * 2,
        compiler_params=_params(),
    )(x, mods, win_t)


def _shift_rows(v, k, row):
    n = v.shape[0]
    r = pltpu.roll(v, k % n, 0)
    keep = (row >= k) if k > 0 else (row < n + k)
    return jnp.where(keep, r, 0.0)


def conv_fwd(xr, cw, cb):
    s = xr.shape[0]

    def body(x_ref, w_ref, b_ref, o_ref):
        xv = x_ref[...]
        row = lax.broadcasted_iota(jnp.int32, xv.shape, 0)
        o_ref[...] = (b_ref[...] + w_ref[0:1, :] * _shift_rows(xv, 2, row) + w_ref[1:2, :] * _shift_rows(xv, 1, row)
                      + w_ref[2:3, :] * xv + w_ref[3:4, :] * _shift_rows(xv, -1, row))

    slab = pl.BlockSpec((s, 128), lambda j: (0, j))
    return pl.pallas_call(
        body, name="conv_fwd", grid=(D_RNN // 128,),
        out_shape=jax.ShapeDtypeStruct((s, D_RNN), F32),
        in_specs=[slab, pl.BlockSpec((4, 128), lambda j: (0, j)), pl.BlockSpec((1, 128), lambda j: (0, j))],
        out_specs=slab,
        compiler_params=_params(),
    )(xr, cw, cb)


def conv_bwd(da, db, xr, cw):
    s = xr.shape[0]

    def body(da_ref, db_ref, x_ref, w_ref, dx_ref, dw_ref, dbias_ref):
        d = da_ref[...] + db_ref[...]
        xv = x_ref[...]
        row = lax.broadcasted_iota(jnp.int32, xv.shape, 0)
        dx_ref[...] = (w_ref[0:1, :] * _shift_rows(d, -2, row) + w_ref[1:2, :] * _shift_rows(d, -1, row)
                       + w_ref[2:3, :] * d + w_ref[3:4, :] * _shift_rows(d, 1, row))
        dw_ref[0:1, :] = _colsum(d * _shift_rows(xv, 2, row))
        dw_ref[1:2, :] = _colsum(d * _shift_rows(xv, 1, row))
        dw_ref[2:3, :] = _colsum(d * xv)
        dw_ref[3:4, :] = _colsum(d * _shift_rows(xv, -1, row))
        dbias_ref[...] = _colsum(d)

    slab = pl.BlockSpec((s, 128), lambda j: (0, j))
    return pl.pallas_call(
        body, name="conv_bwd", grid=(D_RNN // 128,),
        out_shape=[jax.ShapeDtypeStruct((s, D_RNN), F32), jax.ShapeDtypeStruct((4, D_RNN), F32),
                   jax.ShapeDtypeStruct((1, D_RNN), F32)],
        in_specs=[slab, slab, slab, pl.BlockSpec((4, 128), lambda j: (0, j))],
        out_specs=[slab, pl.BlockSpec((4, 128), lambda j: (0, j)), pl.BlockSpec((1, 128), lambda j: (0, j))],
        compiler_params=_params(),
    )(da, db, xr, cw)


def _softplus_neg(lam):
    z = -lam
    e = jnp.exp(-jnp.abs(z))
    u = 1.0 + e
    log1p = jnp.where(u == 1.0, e, jnp.log(u) * e / jnp.where(u == 1.0, 1.0, u - 1.0))
    return jnp.maximum(z, 0.0) + log1p, 1.0 / (1.0 + jnp.exp(lam))


def _lru_gates(xv, wa_ref, wx_ref, ba_ref, bx_ref, lam_ref):
    xb = xv.astype(BF16)
    r = _sigmoid(_nn(xb, wa_ref[...]) + ba_ref[...])
    i = _sigmoid(_nn(xb, wx_ref[...]) + bx_ref[...])
    sp, sg = _softplus_neg(lam_ref[...])
    la = r * (-LRU_C * sp)
    a = jnp.exp(la)
    th = jnp.tanh(la)
    m2 = -2.0 * th / (1.0 - th)
    rmult = lax.rsqrt(jnp.maximum(m2, 1e-37))
    return xb, r, i, sp, sg, a, m2 * rmult, rmult


def _scan(a, u, h0, reverse):
    n, c = a.shape
    sub = lax.broadcasted_iota(jnp.int32, (8, c), 0)
    steps = [(8 - sh, sub < 8 - sh) if reverse else (sh, sub >= sh) for sh in (1, 2, 4)]
    out = [None] * (n // 8)
    edge = h0
    for k in (range(n // 8 - 1, -1, -1) if reverse else range(n // 8)):
        at, ut = a[8 * k:8 * k + 8], u[8 * k:8 * k + 8]
        for rot, keep in steps:
            a_s = jnp.where(keep, pltpu.roll(at, rot, 0), 1.0)
            u_s = jnp.where(keep, pltpu.roll(ut, rot, 0), 0.0)
            ut = at * u_s + ut
            at = at * a_s
        hk = ut + at * edge
        out[k] = hk
        edge = hk[0:1] if reverse else hk[7:8]
    return jnp.concatenate(out, axis=0)


def _lru_specs(nt, tt, reverse):
    tmap = (lambda t: nt - 1 - t) if reverse else (lambda t: t)
    blk = pl.BlockSpec((tt, CG), lambda g, t: (tmap(t), g))
    wsp = pl.BlockSpec((None, CG, CG), lambda g, t: (g, 0, 0))
    vec = pl.BlockSpec((1, CG), lambda g, t: (0, g))
    return tmap, blk, wsp, vec


def lru_fwd(xc, wa, wx, ba, bx, lam, reverse):
    s = xc.shape[0]
    tt = _blk(s, TT_RNN)
    sb = _blk(tt, SB_RNN)
    nt = s // tt

    def body(x_ref, wa_ref, wx_ref, ba_ref, bx_ref, lam_ref, hs_ref, carry):
        @pl.when(pl.program_id(1) == 0)
        def _():
            carry[...] = jnp.zeros(carry.shape, F32)

        xv = x_ref[...]
        _, _, i, _, _, a, mult, _ = _lru_gates(xv, wa_ref, wx_ref, ba_ref, bx_ref, lam_ref)
        u = mult * (i * xv)
        h0 = carry[0:1, :]
        order = range(tt // sb - 1, -1, -1) if reverse else range(tt // sb)
        for j in order:
            rows = slice(sb * j, sb * (j + 1))
            h = _scan(a[rows], u[rows], h0, reverse)
            hs_ref[rows, :] = h
            h0 = h[0:1, :] if reverse else h[sb - 1:sb, :]
        carry[0:1, :] = h0

    _, blk, wsp, vec = _lru_specs(nt, tt, reverse)
    return pl.pallas_call(
        body, name="lru_fwd_rev" if reverse else "lru_fwd", grid=(N_CG, nt),
        out_shape=jax.ShapeDtypeStruct((s, D_RNN), F32),
        in_specs=[blk, wsp, wsp, vec, vec, vec], out_specs=blk,
        scratch_shapes=[pltpu.VMEM((8, CG), F32)],
        compiler_params=_params(),
    )(xc, wa, wx, ba, bx, lam)


def lru_bwd(xc, dhs, hs, wa, wx, ba, bx, lam, reverse, rider=None):
    s = xc.shape[0]
    tt = _blk(s, TT_RNN)
    sb = _blk(tt, SB_RNN)
    nt = s // tt
    back = not reverse

    def body(x_ref, dh_ref, hs_ref, nb_ref, wa_ref, wx_ref, ba_ref, bx_ref, lam_ref,
             dx_ref, dwa_ref, dwx_ref, dba_ref, dbx_ref, dlam_ref, carry):
        t = pl.program_id(1)

        @pl.when(t == 0)
        def _():
            carry[...] = jnp.zeros(carry.shape, F32)
            dwa_ref[...] = jnp.zeros(dwa_ref.shape, F32)
            dwx_ref[...] = jnp.zeros(dwx_ref.shape, F32)
            dba_ref[...] = jnp.zeros(dba_ref.shape, F32)
            dbx_ref[...] = jnp.zeros(dbx_ref.shape, F32)
            dlam_ref[...] = jnp.zeros(dlam_ref.shape, F32)

        xv = x_ref[...]
        xb, r, i, sp, sg, a, mult, rmult = _lru_gates(xv, wa_ref, wx_ref, ba_ref, bx_ref, lam_ref)
        row = lax.broadcasted_iota(jnp.int32, xv.shape, 0)
        hsv = hs_ref[...]
        inner = t < nt - 1
        if reverse:
            edge = jnp.where(inner, nb_ref[0:1, :], 0.0)
            hprev = jnp.where(row == tt - 1, edge, pltpu.roll(hsv, tt - 1, 0))
            a_next = jnp.where(row == 0, carry[1:2, :], pltpu.roll(a, 1, 0))
        else:
            edge = jnp.where(inner, nb_ref[7:8, :], 0.0)
            hprev = jnp.where(row == 0, edge, pltpu.roll(hsv, 1, 0))
            a_next = jnp.where(row == tt - 1, carry[1:2, :], pltpu.roll(a, tt - 1, 0))
        dhv = dh_ref[...]
        g0 = carry[0:1, :]
        parts = [None] * (tt // sb)
        order = range(tt // sb - 1, -1, -1) if back else range(tt // sb)
        for j in order:
            rows = slice(sb * j, sb * (j + 1))
            gj = _scan(a_next[rows], dhv[rows], g0, back)
            parts[j] = gj
            g0 = gj[0:1, :] if back else gj[sb - 1:sb, :]
        g = jnp.concatenate(parts, axis=0) if len(parts) > 1 else parts[0]
        carry[0:1, :] = g0
        carry[1:2, :] = a[0:1, :] if back else a[tt - 1:tt, :]

        da = g * hprev
        dmult = g * (i * xv)
        di = g * mult * xv
        dla = da * a - dmult * (a * a) * rmult
        dpa = (dla * (-LRU_C * sp)) * r * (1.0 - r)
        dpx = di * i * (1.0 - i)
        dlam_ref[...] += _colsum(dla * (LRU_C * r * sg))
        dba_ref[...] += _colsum(dpa)
        dbx_ref[...] += _colsum(dpx)
        dpab, dpxb = dpa.astype(BF16), dpx.astype(BF16)
        dx_ref[...] = g * mult * i + _nt(dpab, wa_ref[...]) + _nt(dpxb, wx_ref[...])
        dwa_ref[...] += _tn(xb, dpab)
        dwx_ref[...] += _tn(xb, dpxb)

    tmap, blk, wsp, vec = _lru_specs(nt, tt, back)
    per8 = tt // 8
    if reverse:
        nb = pl.BlockSpec((8, CG), lambda g, t: (jnp.minimum((tmap(t) + 1) * per8, s // 8 - 1), g))
    else:
        nb = pl.BlockSpec((8, CG), lambda g, t: (jnp.maximum(tmap(t) * per8 - 1, 0), g))
    return _call(
        body, name="lru_bwd_rev" if reverse else "lru_bwd", grid=(N_CG, nt),
        out_shape=[jax.ShapeDtypeStruct((s, D_RNN), F32), jax.ShapeDtypeStruct((N_CG, CG, CG), F32),
                   jax.ShapeDtypeStruct((N_CG, CG, CG), F32)] + [jax.ShapeDtypeStruct((1, D_RNN), F32)] * 3,
        in_specs=[blk, blk, blk, nb, wsp, wsp, vec, vec, vec],
        out_specs=[blk, wsp, wsp, vec, vec, vec],
        scratch_shapes=[pltpu.VMEM((8, CG), F32)],
        args=(xc, dhs, hs, hs, wa, wx, ba, bx, lam), rider=rider)


def _ln_part_bwd(dxo, x, y, gate, g, sums_ref, loss_head=None):
    xhat, rstd = _ln_stats(ALPHA * x + (1.0 + gate) * y)
    if loss_head is not None:
        err = xhat * g + loss_head[0] - loss_head[1]
        dxo = err * (1.0 / D_MODEL)
        sums_ref[5:6, :] += _colsum(err * err)
    dz = _ln_bwd(dxo, xhat, rstd, g)
    sums_ref[2:3, :] += _colsum(dz * y)
    sums_ref[3:4, :] += _colsum(dxo * xhat)
    sums_ref[4:5, :] += _colsum(dxo)
    return dz


def mlp_bwd(dxo, x, y, ra, mods, k, w1_t, w2, lng, lnb=None, rider=None):
    s = x.shape[0]
    tm = _blk(s, TM_MLP)
    head = lnb is not None

    def body(d_ref, x_ref, y_ref, ra_ref, mod_ref, w1_ref, w2_ref, g_ref, *rest):
        b_ref = rest[0] if head else None
        dx_ref, da_ref, h_ref, dy_ref, sums_ref = rest[1:] if head else rest

        @pl.when(pl.program_id(0) == 0)
        def _():
            sums_ref[...] = jnp.zeros(sums_ref.shape, F32)

        xv = x_ref[...]
        shift, scale, gate = _mod(mod_ref, k)
        if head:
            dz = _ln_part_bwd(None, xv, y_ref[...], gate, g_ref[...], sums_ref, (b_ref[...], d_ref[...]))
        else:
            dz = _ln_part_bwd(d_ref[...], xv, y_ref[...], gate, g_ref[...], sums_ref)
        dyb = (dz * (1.0 + gate)).astype(BF16)
        dy_ref[...] = dyb
        h = (xv * (1.0 + scale) + shift).astype(BF16)
        h_ref[...] = h
        dh = jnp.zeros((tm, D_MODEL), F32)
        for c in range(D_FF // FF_CHUNK):
            rows = slice(FF_CHUNK * c, FF_CHUNK * (c + 1))
            da = (_nt(dyb, w2_ref[rows, :]) * (2.0 * ra_ref[:, rows].astype(F32))).astype(BF16)
            da_ref[:, rows] = da
            dh = dh + _nn(da, w1_ref[rows, :])
        dx_ref[...] = ALPHA * dz + dh * (1.0 + scale)
        sums_ref[0:1, :] += _colsum(dh)
        sums_ref[1:2, :] += _colsum(dh * xv)

    return _call(
        body, name="mlp_bwd", grid=(s // tm,),
        out_shape=[jax.ShapeDtypeStruct((s, D_MODEL), F32), jax.ShapeDtypeStruct((s, D_FF), BF16),
                   jax.ShapeDtypeStruct((s, D_MODEL), BF16),
                   jax.ShapeDtypeStruct((s, D_MODEL), BF16), jax.ShapeDtypeStruct((8, D_MODEL), F32)],
        in_specs=[_row(tm, D_MODEL)] * 3 + [_row(tm, D_FF), _res(mods.shape), _res(w1_t.shape), _res(w2.shape),
                                             _res(lng.shape)] + ([_res(lnb.shape)] if head else []),
        out_specs=[_row(tm, D_MODEL), _row(tm, D_FF), _row(tm, D_MODEL), _row(tm, D_MODEL), _res((8, D_MODEL))],
        args=(dxo, x, y, ra, mods, w1_t, w2, lng) + ((lnb,) if head else ()), rider=rider)


def post_bwd(dxo, x, y, mods, k, w, lng, gate_act=None, rider=None):
    s = x.shape[0]
    tm = _blk(s, TM_MM)
    kdim = w.shape[0]
    rnn = gate_act is not None

    def body(*refs):
        if rnn:
            (d_ref, x_ref, y_ref, mod_ref, w_ref, g_ref, gt_ref, hf_ref, hb_ref,
             dres_ref, dy_ref, sums_ref, dhs_ref, dgt_ref) = refs
        else:
            d_ref, x_ref, y_ref, mod_ref, w_ref, g_ref, dres_ref, dy_ref, sums_ref, dyp_ref = refs

        @pl.when(pl.program_id(0) == 0)
        def _():
            sums_ref[...] = jnp.zeros(sums_ref.shape, F32)

        _, _, gate = _mod(mod_ref, k)
        dz = _ln_part_bwd(d_ref[...], x_ref[...], y_ref[...], gate, g_ref[...], sums_ref)
        dres_ref[...] = ALPHA * dz
        dyb = (dz * (1.0 + gate)).astype(BF16)
        dy_ref[...] = dyb
        dyp = _nt(dyb, w_ref[...])
        if rnn:
            act, dact = _gelu_parts(gt_ref[...])
            dhs_ref[...] = dyp * act
            dgt_ref[...] = dyp * (hf_ref[...] + hb_ref[...]) * dact
        else:
            dyp_ref[...] = dyp.astype(BF16)

    ins = [dxo, x, y, mods, w, lng] + (list(gate_act) if rnn else [])
    in_specs = [_row(tm, D_MODEL)] * 3 + [_res(mods.shape), _res(w.shape), _res(lng.shape)]
    out_shape = [jax.ShapeDtypeStruct((s, D_MODEL), F32), jax.ShapeDtypeStruct((s, D_MODEL), BF16),
                 jax.ShapeDtypeStruct((8, D_MODEL), F32)]
    out_specs = [_row(tm, D_MODEL), _row(tm, D_MODEL), _res((8, D_MODEL))]
    if rnn:
        in_specs += [_row(tm, kdim)] * 3
        out_shape += [jax.ShapeDtypeStruct((s, kdim), F32)] * 2
        out_specs += [_row(tm, kdim)] * 2
    else:
        out_shape.append(jax.ShapeDtypeStruct((s, kdim), BF16))
        out_specs.append(_row(tm, kdim))
    return _call(
        body, name="rnn_post_bwd" if rnn else "attn_post_bwd", grid=(s // tm,),
        out_shape=out_shape, in_specs=in_specs, out_specs=out_specs, args=ins, rider=rider)


def attn_bwd(q, kk, v, do, sinks, rider=None):
    s = q.shape[0]
    nblk = s // QBLK
    scale = HEAD ** -0.5

    def body(sink_ref, q_ref, do_ref, kp, ko, kn, vp, vo, vn, dq_ref, dkp_ref, dvp_ref, ds_ref):
        n = pl.program_id(0)

        @pl.when(n == 0)
        def _():
            ds_ref[...] = jnp.zeros(ds_ref.shape, F32)

        kcat = jnp.concatenate([kp[...], ko[...], kn[...]], axis=0)
        vcat = jnp.concatenate([vp[...], vo[...], vn[...]], axis=0)
        valid = _attn_mask(n, s)
        lane = lax.broadcasted_iota(jnp.int32, (1, 128), 1)
        dsink = jnp.zeros((1, 128), F32)
        for kv in range(N_KV):
            cols = slice(HEAD * kv, HEAD * (kv + 1))
            qg, dog = _stack_heads(q_ref, kv), _stack_heads(do_ref, kv)
            kh, vh = kcat[:, cols], vcat[:, cols]
            probs, psink = _attn_probs(qg, kh, valid, _stack_sinks(sink_ref, kv))
            dprobs = _nt(dog, vh)
            dvp_ref[:, cols] = _tn(probs.astype(BF16), dog)
            rowdot = jnp.sum(probs * dprobs, axis=-1, keepdims=True)
            dsb = (probs * (dprobs - rowdot) * scale).astype(BF16)
            dqg = _nn(dsb, kh)
            dkp_ref[:, cols] = _tn(dsb, qg)
            dsk = -psink * rowdot
            for j in range(GROUP):
                hq = GROUP * kv + j
                dq_ref[:, HEAD * hq:HEAD * (hq + 1)] = dqg[QBLK * j:QBLK * (j + 1), :]
                dsink = dsink + jnp.where(lane == hq, _colsum(dsk[QBLK * j:QBLK * (j + 1), :]), 0.0)
        ds_ref[...] += dsink

    qspec = pl.BlockSpec((QBLK, N_Q * HEAD), lambda n: (n, 0))
    pspec = pl.BlockSpec((None, 3 * QBLK, N_KV * HEAD), lambda n: (n, 0, 0))
    return _call(
        body, name="attn_bwd", grid=(nblk,),
        out_shape=[jax.ShapeDtypeStruct((s, N_Q * HEAD), F32),
                   jax.ShapeDtypeStruct((nblk, 3 * QBLK, N_KV * HEAD), F32),
                   jax.ShapeDtypeStruct((nblk, 3 * QBLK, N_KV * HEAD), F32),
                   jax.ShapeDtypeStruct((1, 128), F32)],
        in_specs=[pl.BlockSpec(memory_space=pltpu.SMEM), qspec, qspec] + _kv_specs(nblk) + _kv_specs(nblk),
        out_specs=[qspec, pspec, pspec, pl.BlockSpec((1, 128), lambda n: (0, 0))],
        args=(sinks, q, do, kk, kk, kk, v, v, v), rider=rider)


def kv_combine(dkp, dvp, rider=None):
    nblk = dkp.shape[0]
    w = N_KV * HEAD

    def body(kp, ko, kn, vp, vo, vn, dk_ref, dv_ref):
        n = pl.program_id(0)
        dk_ref[...] = jnp.where(n > 0, kp[...], 0.0) + ko[...] + jnp.where(n < nblk - 1, kn[...], 0.0)
        dv_ref[...] = jnp.where(n > 0, vp[...], 0.0) + vo[...] + jnp.where(n < nblk - 1, vn[...], 0.0)

    specs = [pl.BlockSpec((None, QBLK, w), lambda n: (jnp.maximum(n - 1, 0), 2, 0)),
             pl.BlockSpec((None, QBLK, w), lambda n: (n, 1, 0)),
             pl.BlockSpec((None, QBLK, w), lambda n: (jnp.minimum(n + 1, nblk - 1), 0, 0))]
    out = pl.BlockSpec((QBLK, w), lambda n: (n, 0))
    return _call(
        body, name="kv_combine", grid=(nblk,),
        out_shape=[jax.ShapeDtypeStruct((nblk * QBLK, w), F32)] * 2,
        in_specs=specs + specs, out_specs=[out, out],
        args=(dkp, dkp, dkp, dvp, dvp, dvp), rider=rider)


def _in_bwd_tail(dzb, w_ref, x_ref, mod_ref, k, dres_ref, dx_ref, h_ref, sums_ref):
    xv = x_ref[...]
    shift, scale, _ = _mod(mod_ref, k)
    h_ref[...] = (xv * (1.0 + scale) + shift).astype(BF16)
    dh = _nn(dzb, w_ref[...])
    dx_ref[...] = dres_ref[...] + dh * (1.0 + scale)
    sums_ref[0:1, :] += _colsum(dh)
    sums_ref[1:2, :] += _colsum(dh * xv)


def attn_in_bwd(dq, dk, dv, rope, x, mods, k, win_t, dres):
    s = x.shape[0]
    tm = _blk(s, TM_MM)

    def body(dq_ref, dk_ref, dv_ref, c_ref, s1_ref, s2_ref, x_ref, mod_ref, w_ref, dres_ref,
             dx_ref, dz_ref, h_ref, sums_ref):
        @pl.when(pl.program_id(0) == 0)
        def _():
            sums_ref[...] = jnp.zeros(sums_ref.shape, F32)

        cos, s1, s2 = c_ref[...], s1_ref[...], s2_ref[...]
        for hh in range(N_Q + N_KV):
            src = dq_ref[:, HEAD * hh:HEAD * (hh + 1)] if hh < N_Q else dk_ref[:, HEAD * (hh - N_Q):HEAD * (hh - N_Q + 1)]
            dz_ref[:, HEAD * hh:HEAD * (hh + 1)] = _rope_bwd(src, cos, s1, s2).astype(BF16)
        dz_ref[:, HEAD * (N_Q + N_KV):] = dv_ref[...].astype(BF16)
        _in_bwd_tail(dz_ref[...], w_ref, x_ref, mod_ref, k, dres_ref, dx_ref, h_ref, sums_ref)

    return pl.pallas_call(
        body, name="attn_in_bwd", grid=(s // tm,),
        out_shape=[jax.ShapeDtypeStruct((s, D_MODEL), F32), jax.ShapeDtypeStruct((s, D_QKV), BF16),
                   jax.ShapeDtypeStruct((s, D_MODEL), BF16), jax.ShapeDtypeStruct((8, D_MODEL), F32)],
        in_specs=[_row(tm, N_Q * HEAD), _row(tm, N_KV * HEAD), _row(tm, N_KV * HEAD),
                  _row(tm, HEAD), _row(tm, HEAD), _row(tm, HEAD), _row(tm, D_MODEL),
                  _res(mods.shape), _res(win_t.shape), _row(tm, D_MODEL)],
        out_specs=[_row(tm, D_MODEL), _row(tm, D_QKV), _row(tm, D_MODEL), _res((8, D_MODEL))],
        compiler_params=_params(),
    )(dq, dk, dv, *rope, x, mods, win_t, dres)


def rnn_in_bwd(dxr, dgt, x, mods, k, win_t, dres):
    s = x.shape[0]
    tm = _blk(s, TM_MM)

    def body(dxr_ref, dgt_ref, x_ref, mod_ref, w_ref, dres_ref, dx_ref, dz_ref, h_ref, sums_ref):
        @pl.when(pl.program_id(0) == 0)
        def _():
            sums_ref[...] = jnp.zeros(sums_ref.shape, F32)

        dz_ref[:, 0:D_RNN] = dxr_ref[...].astype(BF16)
        dz_ref[:, D_RNN:2 * D_RNN] = dgt_ref[...].astype(BF16)
        _in_bwd_tail(dz_ref[...], w_ref, x_ref, mod_ref, k, dres_ref, dx_ref, h_ref, sums_ref)

    return pl.pallas_call(
        body, name="rnn_in_bwd", grid=(s // tm,),
        out_shape=[jax.ShapeDtypeStruct((s, D_MODEL), F32), jax.ShapeDtypeStruct((s, 2 * D_RNN), BF16),
                   jax.ShapeDtypeStruct((s, D_MODEL), BF16), jax.ShapeDtypeStruct((8, D_MODEL), F32)],
        in_specs=[_row(tm, D_RNN), _row(tm, D_RNN), _row(tm, D_MODEL), _res(mods.shape), _res(win_t.shape),
                  _row(tm, D_MODEL)],
        out_specs=[_row(tm, D_MODEL), _row(tm, 2 * D_RNN), _row(tm, D_MODEL), _res((8, D_MODEL))],
        compiler_params=_params(),
    )(dxr, dgt, x, mods, win_t, dres)


def wgrad(a, b, name):
    s, m = a.shape
    n = b.shape[1]
    tm = next(t for t in (1024, 768, 512, 384, 256, 128) if m % t == 0)
    tk = _blk(s, TK_WG)
    nk = s // tk

    def body(a_ref, b_ref, o_ref, acc):
        kk = pl.program_id(1)

        @pl.when(kk == 0)
        def _():
            acc[...] = jnp.zeros(acc.shape, F32)

        acc[...] += _tn(a_ref[...], b_ref[...])

        @pl.when(kk == nk - 1)
        def _():
            o_ref[...] = acc[...].astype(BF16)

    out = pl.pallas_call(
        body, name=name, grid=(m // tm, nk),
        out_shape=jax.ShapeDtypeStruct((m, n), BF16),
        in_specs=[pl.BlockSpec((tk, tm), lambda i, kk: (kk, i)), pl.BlockSpec((tk, n), lambda i, kk: (kk, 0))],
        out_specs=pl.BlockSpec((tm, n), lambda i, kk: (i, 0)),
        scratch_shapes=[pltpu.VMEM((tm, n), F32)],
        compiler_params=_params(),
    )(a, b)
    return out.reshape(N_DEV, m // N_DEV, n)


def part_sum(parts, name):
    _, r, c = parts.shape
    tr = next(t for t in (256, 192, 128, 64, 32, 16, 8) if r % t == 0)

    def body(p_ref, o_ref):
        acc = p_ref[0].astype(F32)
        for j in range(1, N_DEV):
            acc = acc + p_ref[j].astype(F32)
        o_ref[...] = acc

    return pl.pallas_call(
        body, name=name, grid=(r // tr,),
        out_shape=jax.ShapeDtypeStruct((r, c), F32),
        in_specs=[pl.BlockSpec((N_DEV, tr, c), lambda i: (0, i, 0))],
        out_specs=pl.BlockSpec((tr, c), lambda i: (i, 0)),
        compiler_params=_params(),
    )(parts)


def adamw(w, g, m, v, name):
    shape = w.shape
    c = shape[-1]
    r = w.size // c
    w2, g2, m2, v2 = (t.reshape(r, c) for t in (w, g, m, v))
    tr = r if r * c <= 512 * 1024 else next(t for t in (512, 256, 128, 64, 32, 16, 8) if r % t == 0)

    def body(w_ref, g_ref, m_ref, v_ref, d_ref, nm_ref, nv_ref):
        gv = g_ref[...]
        nm = B1 * m_ref[...] + (1.0 - B1) * gv
        nv = B2 * v_ref[...] + (1.0 - B2) * (gv * gv)
        nm_ref[...] = nm
        nv_ref[...] = nv
        m_hat = nm / (1.0 - B1 ** STEP)
        v_hat = nv / (1.0 - B2 ** STEP)
        d_ref[...] = -LR * (m_hat / (jnp.sqrt(v_hat) + ADAM_EPS) + WD * w_ref[...])

    spec = pl.BlockSpec((tr, c), lambda i: (i, 0))
    outs = pl.pallas_call(
        body, name=name, grid=(r // tr,),
        out_shape=[jax.ShapeDtypeStruct((r, c), F32)] * 3,
        in_specs=[spec] * 4, out_specs=[spec] * 3,
        compiler_params=_params(),
    )(w2, g2, m2, v2)
    return tuple(o.reshape(shape) for o in outs)


def _rope_tables(s):
    pos = jnp.arange(s, dtype=F32)
    inv_freq = THETA ** (-jnp.arange(0, ROT, 2, dtype=F32) / ROT)
    ang = pos[:, None] * inv_freq[None, :]
    cos, sin = jnp.cos(ang), jnp.sin(ang)
    half = ROT // 2
    zeros = jnp.zeros((s, HEAD - ROT), F32)
    c = jnp.concatenate([cos, cos, jnp.ones((s, HEAD - ROT), F32)], axis=1)
    s1 = jnp.concatenate([jnp.zeros((s, half), F32), sin, zeros], axis=1)
    s2 = jnp.concatenate([-sin, jnp.zeros((s, half), F32), zeros], axis=1)
    return c, s1, s2


def _blockdiag(w):
    w4 = w.reshape(N_CG, 4, RB_W, RB_W)
    eye = jnp.eye(4, dtype=w.dtype)
    return jnp.einsum("gipq,ij->gipjq", w4, eye).reshape(N_CG, CG, CG)


def _diag_blocks(w):
    w5 = w.reshape(N_CG, 4, RB_W, 4, RB_W)
    eye = jnp.eye(4, dtype=w.dtype)
    return jnp.einsum("gipjq,ij->gipq", w5, eye).reshape(N_RB, RB_W, RB_W)


def _cols(full, per):
    lead = full.shape[:-1]
    t = full.reshape(lead + (N_DEV, per))
    return jnp.moveaxis(t, -2, 0).reshape(N_DEV, -1)


def kernel(x, c, ada_w, ada_b, ln_g, ln_b, attn_w_in, attn_w_out, attn_sinks, rnn_w_in, rnn_conv_w, rnn_conv_b, rnn_w_a, rnn_b_a, rnn_w_x, rnn_b_x, rnn_lam, rnn_w_out, mlp_w1, mlp_w2, loss_target, m_ada_w, m_ada_b, m_ln_g, m_ln_b, m_attn_w_in, m_attn_w_out, m_attn_sinks, m_rnn_w_in, m_rnn_conv_w, m_rnn_conv_b, m_rnn_w_a, m_rnn_b_a, m_rnn_w_x, m_rnn_b_x, m_rnn_lam, m_rnn_w_out, m_mlp_w1, m_mlp_w2, v_ada_w, v_ada_b, v_ln_g, v_ln_b, v_attn_w_in, v_attn_w_out, v_attn_sinks, v_rnn_w_in, v_rnn_conv_w, v_rnn_conv_b, v_rnn_w_a, v_rnn_b_a, v_rnn_w_x, v_rnn_b_x, v_rnn_lam, v_rnn_w_out, v_mlp_w1, v_mlp_w2):
    s = x.shape[1]
    x0 = x.reshape(s, D_MODEL)
    target = loss_target.reshape(s, D_MODEL)
    weights = dict(ada_w=ada_w, ada_b=ada_b, ln_g=ln_g, ln_b=ln_b, attn_w_in=attn_w_in, attn_w_out=attn_w_out,
                   attn_sinks=attn_sinks, rnn_w_in=rnn_w_in, rnn_conv_w=rnn_conv_w, rnn_conv_b=rnn_conv_b,
                   rnn_w_a=rnn_w_a, rnn_b_a=rnn_b_a, rnn_w_x=rnn_w_x, rnn_b_x=rnn_b_x, rnn_lam=rnn_lam,
                   rnn_w_out=rnn_w_out, mlp_w1=mlp_w1, mlp_w2=mlp_w2)
    moments_m = dict(ada_w=m_ada_w, ada_b=m_ada_b, ln_g=m_ln_g, ln_b=m_ln_b, attn_w_in=m_attn_w_in,
                     attn_w_out=m_attn_w_out, attn_sinks=m_attn_sinks, rnn_w_in=m_rnn_w_in,
                     rnn_conv_w=m_rnn_conv_w, rnn_conv_b=m_rnn_conv_b, rnn_w_a=m_rnn_w_a, rnn_b_a=m_rnn_b_a,
                     rnn_w_x=m_rnn_w_x, rnn_b_x=m_rnn_b_x, rnn_lam=m_rnn_lam, rnn_w_out=m_rnn_w_out,
                     mlp_w1=m_mlp_w1, mlp_w2=m_mlp_w2)
    moments_v = dict(ada_w=v_ada_w, ada_b=v_ada_b, ln_g=v_ln_g, ln_b=v_ln_b, attn_w_in=v_attn_w_in,
                     attn_w_out=v_attn_w_out, attn_sinks=v_attn_sinks, rnn_w_in=v_rnn_w_in,
                     rnn_conv_w=v_rnn_conv_w, rnn_conv_b=v_rnn_conv_b, rnn_w_a=v_rnn_w_a, rnn_b_a=v_rnn_b_a,
                     rnn_w_x=v_rnn_w_x, rnn_b_x=v_rnn_b_x, rnn_lam=v_rnn_lam, rnn_w_out=v_rnn_w_out,
                     mlp_w1=v_mlp_w1, mlp_w2=v_mlp_w2)
    names = list(weights)

    def t16(w):
        return w.T.astype(BF16)

    big = [t16(attn_w_in[0]), attn_w_out[0].astype(BF16), t16(rnn_w_in[0]), rnn_w_out[0].astype(BF16),
           t16(mlp_w1[0]), mlp_w2[0].astype(BF16), t16(mlp_w1[1]), mlp_w2[1].astype(BF16)]
    small_local = jnp.concatenate([
        ln_g.reshape(-1), ln_b.reshape(-1), rnn_conv_w.reshape(-1), rnn_conv_b.reshape(-1),
        rnn_b_a.reshape(-1), rnn_b_x.reshape(-1), rnn_lam.reshape(-1)])
    small_local = jnp.pad(small_local, (0, 4096 - small_local.shape[0])).reshape(32, 128)
    flat = lambda g: g.reshape(N_DEV * g.shape[1], D_MODEL)
    first = exchange(_Gather([big[0], big[1], small_local]), "weight_gather")
    win_t, wout = flat(first[0]), flat(first[1])
    sm = first[2].reshape(N_DEV, 4096)

    def full_vec(off, rows, per):
        piece = sm[:, off:off + rows * per].reshape(N_DEV, rows, per)
        return jnp.moveaxis(piece, 0, 1).reshape(rows, N_DEV * per)

    lng_f, lnb_f = full_vec(0, 4, 128), full_vec(512, 4, 128)
    cw_f, cb_f = full_vec(1024, 4, 192), full_vec(1792, 1, 192)
    ba_f, bx_f, lam_f = full_vec(1984, 2, 192), full_vec(2368, 2, 192), full_vec(2752, 2, 192)
    wa_bd = [_blockdiag(rnn_w_a[0, d]).astype(BF16) for d in range(2)]
    wx_bd = [_blockdiag(rnn_w_x[0, d]).astype(BF16) for d in range(2)]

    c_all, modr = ada_modulation(jnp.broadcast_to(c, (8, D_MODEL)), ada_w.reshape(4, D_MODEL, CG),
                                 ada_b.reshape(4, 1, CG))
    mods = modr.reshape(N_DEV, 4, 8, CG)[:, :, 0, :]
    mods = jnp.moveaxis(mods, 0, 1).reshape(4, 3, D_MODEL).reshape(12, D_MODEL)
    rope = _rope_tables(s)
    ln = lambda k: (lng_f[k:k + 1], lnb_f[k:k + 1])

    q, kk, v = attn_in_fwd(x0, mods, 0, win_t, rope)
    o, *got = attn_fwd(q, kk, v, attn_sinks, rider=_Gather([big[4], big[5], big[2], big[3]]))
    w1t_0, w2_0, rin_t, rout = (flat(g) for g in got)
    x1, y0 = post_fwd(o, wout, x0, mods, 0, *ln(0))
    x2, y1, ra0, r0, *got = mlp_fwd(x1, mods, 1, w1t_0, w2_0, *ln(1), rider=_Gather([big[6], big[7]]))
    w1t_1, w2_1 = (flat(g) for g in got)
    xr, gt = rnn_in_fwd(x2, mods, 2, rin_t)
    xc = conv_fwd(xr, cw_f, cb_f)
    hf = lru_fwd(xc, wa_bd[0], wx_bd[0], ba_f[0:1], bx_f[0:1], lam_f[0:1], False)
    hb = lru_fwd(xc, wa_bd[1], wx_bd[1], ba_f[1:2], bx_f[1:2], lam_f[1:2], True)
    x3, y2, ypre = post_fwd(None, rout, x2, mods, 2, *ln(2), gate_act=(gt, hf, hb))
    y3, ra1, r1 = mlp_fwd(x3, mods, 3, w1t_1, w2_1, *ln(3), last=True)

    dx3, da1, h3, dy3, sums3 = mlp_bwd(target, x3, y3, ra1, mods, 3, w1t_1, w2_1, lng_f[3:4], lnb=lnb_f[3:4])
    loss = lax.psum(0.5 * jnp.sum(sums3[5]) / D_MODEL, ("x", "y", "c"))
    g_w1t_1 = wgrad(da1, h3, "wgrad_w1_1")
    g_w2_1 = wgrad(r1, dy3, "wgrad_w2_1")
    dres2, dy2, sums2a, dhs, dgt, p_w1t_1 = post_bwd(dx3, x2, y2, mods, 2, rout, lng_f[2:3], gate_act=(gt, hf, hb),
                                                     rider=_AllToAll([g_w1t_1]))
    g_rout = wgrad(ypre, dy2, "wgrad_rnn_out")
    dxc_f, dwa_f, dwx_f, dba_f, dbx_f, dlam_f, p_w2_1, p_rout = lru_bwd(
        xc, dhs, hf, wa_bd[0], wx_bd[0], ba_f[0:1], bx_f[0:1], lam_f[0:1], False, rider=_AllToAll([g_w2_1, g_rout]))
    dxc_b, dwa_b, dwx_b, dba_b, dbx_b, dlam_b = lru_bwd(xc, dhs, hb, wa_bd[1], wx_bd[1], ba_f[1:2], bx_f[1:2],
                                                        lam_f[1:2], True)
    dxr, dcw, dcb = conv_bwd(dxc_f, dxc_b, xr, cw_f)
    dx2, dzz, h2, sums2b = rnn_in_bwd(dxr, dgt, x2, mods, 2, rin_t, dres2)
    g_rin_t = wgrad(dzz, h2, "wgrad_rnn_in")
    dx1, da0, h1, dy1, sums1, p_rin_t = mlp_bwd(dx2, x1, y1, ra0, mods, 1, w1t_0, w2_0, lng_f[1:2],
                                                rider=_AllToAll([g_rin_t]))
    g_w1t_0 = wgrad(da0, h1, "wgrad_w1_0")
    g_w2_0 = wgrad(r0, dy1, "wgrad_w2_0")
    dres0, dy0, sums0a, do = post_bwd(dx1, x0, y0, mods, 0, wout, lng_f[0:1])
    g_wout = wgrad(o, dy0, "wgrad_attn_out")
    dq, dkp, dvp, dsink, p_w1t_0, p_w2_0 = attn_bwd(q, kk, v, do, attn_sinks, rider=_AllToAll([g_w1t_0, g_w2_0]))
    dk, dv, p_wout = kv_combine(dkp, dvp, rider=_AllToAll([g_wout]))
    dx0, dqkv, h0, sums0b = attn_in_bwd(dq, dk, dv, rope, x0, mods, 0, win_t, dres0)
    g_win_t = wgrad(dqkv, h0, "wgrad_attn_in")
    p_win_t, = exchange(_AllToAll([g_win_t]), "grad_exchange")

    big_parts = [p_win_t, p_wout, p_rin_t, p_rout, p_w1t_0, p_w2_0, p_w1t_1, p_w2_1]
    gsum = [part_sum(p, "part_sum_%d" % i) for i, p in enumerate(big_parts)]
    grads = {
        "attn_w_in": gsum[0].T[None], "attn_w_out": gsum[1][None],
        "rnn_w_in": gsum[2].T[None], "rnn_w_out": gsum[3][None],
        "mlp_w1": jnp.stack([gsum[4].T, gsum[6].T]), "mlp_w2": jnp.stack([gsum[5], gsum[7]]),
    }

    sums = [sums0a + sums0b, sums1, sums2a + sums2b, sums3]
    gmod = jnp.stack([t[0:3] for t in sums])
    gsend = jnp.moveaxis(gmod.reshape(4, N_DEV, CG), 1, 0)
    gsend = jnp.pad(gsend, ((0, 0), (0, 4), (0, 0)))
    c_t = c_all[:, 0, :].T
    g_ada_w, g_ada_b = ada_grads(gsend, c_t)
    grads["ada_w"] = g_ada_w.reshape(ada_w.shape)
    grads["ada_b"] = g_ada_b[0:4].reshape(ada_b.shape)

    d_wa = jnp.stack([_diag_blocks(dwa_f), _diag_blocks(dwa_b)])
    d_wx = jnp.stack([_diag_blocks(dwx_f), _diag_blocks(dwx_b)])
    nflat = d_wa.size // N_DEV
    tail = jnp.concatenate([
        _cols(dcw, 192), _cols(dcb, 192),
        _cols(jnp.concatenate([dba_f, dba_b]), 192), _cols(jnp.concatenate([dbx_f, dbx_b]), 192),
        _cols(jnp.concatenate([dlam_f, dlam_b]), 192),
        _cols(jnp.stack([t[3] for t in sums]), 128), _cols(jnp.stack([t[4] for t in sums]), 128),
        jnp.broadcast_to(dsink[:, 0:8], (N_DEV, 8))], axis=1)
    tail = jnp.pad(tail, ((0, 0), (0, 32 * 128 - tail.shape[1])))
    ssend = jnp.concatenate([d_wa.reshape(N_DEV, nflat), d_wx.reshape(N_DEV, nflat), tail], axis=1)
    rw = 2 * nflat // 128
    red, wag = small_reduce(ssend.reshape(N_DEV, rw + 32, 128), rw)
    wag = wag.reshape(N_DEV, 2 * nflat)
    grads["rnn_w_a"] = wag[:, :nflat].reshape(rnn_w_a.shape)
    grads["rnn_w_x"] = wag[:, nflat:].reshape(rnn_w_x.shape)
    tl = red[rw:].reshape(-1)
    grads["rnn_conv_w"] = tl[0:768].reshape(rnn_conv_w.shape)
    grads["rnn_conv_b"] = tl[768:960].reshape(rnn_conv_b.shape)
    grads["rnn_b_a"] = tl[960:1344].reshape(rnn_b_a.shape)
    grads["rnn_b_x"] = tl[1344:1728].reshape(rnn_b_x.shape)
    grads["rnn_lam"] = tl[1728:2112].reshape(rnn_lam.shape)
    grads["ln_g"] = tl[2112:2624].reshape(ln_g.shape)
    grads["ln_b"] = tl[2624:3136].reshape(ln_b.shape)
    grads["attn_sinks"] = tl[3136:3144].reshape(attn_sinks.shape)

    delta, new_m, new_v = {}, {}, {}
    for n in names:
        delta[n], new_m[n], new_v[n] = adamw(weights[n], grads[n], moments_m[n], moments_v[n], "adamw_" + n)
    return (loss, dx0.reshape(x.shape), *[grads[n] for n in names], *[delta[n] for n in names],
            *[new_m[n] for n in names], *[new_v[n] for n in names])
```

```python
import functools
import math

import jax
import jax.numpy as jnp
from jax import lax
from jax.experimental import pallas as pl
from jax.experimental.pallas import tpu as pltpu

F32, BF16 = jnp.float32, jnp.bfloat16
MESH = pl.DeviceIdType.MESH

D_MODEL = 1024
N_Q, N_KV, HEAD = 8, 2, 128
ROT, THETA = 32, 500000.0
QBLK = 128
D_QKV = (N_Q + 2 * N_KV) * HEAD
D_RNN, N_RB, RB_W = 1536, 16, 96
CG = 384
N_CG = D_RNN // CG
D_FF = 4096
FF_CHUNK = 1024
DEPTH = 2
ALPHA = (2.0 * DEPTH) ** 0.25
LN_EPS = 1e-5
LRU_C = 8.0
N_DEV = 8
LR, B1, B2, ADAM_EPS, WD, STEP = 0.001, 0.9, 0.999, 1e-8, 0.01, 10

VMEM_LIMIT = 56 * 1024 * 1024
TM_MM = 512
TM_MLP = 256
TT_RNN = 512
SB_RNN = 512
TK_WG = 2048


def _nn(a, b):
    return jnp.dot(a, b, preferred_element_type=F32)


def _nt(a, b):
    return lax.dot_general(a, b, (((1,), (1,)), ((), ())), preferred_element_type=F32)


def _tn(a, b):
    return lax.dot_general(a, b, (((0,), (0,)), ((), ())), preferred_element_type=F32)


def _blk(n, pref):
    t = min(n, pref)
    assert n % t == 0, (n, pref)
    return t


def _params(**kw):
    return pltpu.CompilerParams(vmem_limit_bytes=VMEM_LIMIT, **kw)


def _row(tm, w):
    return pl.BlockSpec((tm, w), lambda i: (i, 0))


def _res(shape):
    return pl.BlockSpec(shape, lambda i: (0,) * len(shape), pipeline_mode=pl.Buffered(1))


def _mod(mod_ref, k):
    return mod_ref[3 * k:3 * k + 1, :], mod_ref[3 * k + 1:3 * k + 2, :], mod_ref[3 * k + 2:3 * k + 3, :]


def _ln_stats(z):
    mu = jnp.mean(z, axis=-1, keepdims=True)
    zc = z - mu
    var = jnp.mean(zc * zc, axis=-1, keepdims=True)
    rstd = lax.rsqrt(var + LN_EPS)
    return zc * rstd, rstd


def _ln_bwd(dxo, xhat, rstd, g):
    dxh = dxo * g
    m1 = jnp.mean(dxh, axis=-1, keepdims=True)
    m2 = jnp.mean(dxh * xhat, axis=-1, keepdims=True)
    return rstd * (dxh - m1 - xhat * m2)


def _colsum(v):
    return jnp.sum(v, axis=0, keepdims=True)


def _sigmoid(v):
    return 0.5 * jnp.tanh(0.5 * v) + 0.5


def _gelu_parts(v):
    k = math.sqrt(2.0 / math.pi)
    u = k * (v + 0.044715 * v * v * v)
    t = jnp.tanh(u)
    g = 0.5 * v * (1.0 + t)
    dg = 0.5 * (1.0 + t) + 0.5 * v * (1.0 - t * t) * k * (1.0 + 3.0 * 0.044715 * v * v)
    return g, dg


def _me():
    return lax.axis_index("x"), lax.axis_index("y"), lax.axis_index("c")


def _idx(p):
    return 4 * p[0] + 2 * p[1] + p[2]


def _peers(me):
    x, y, c = me
    out = []
    for k in range(1, N_DEV):
        out.append((1 - x if k & 4 else x, 1 - y if k & 2 else y, 1 - c if k & 1 else c))
    return out


class _Gather:
    def __init__(self, srcs):
        self.srcs = list(srcs)
        n = len(self.srcs)
        self.out_shape = [jax.ShapeDtypeStruct((N_DEV,) + s.shape, s.dtype) for s in self.srcs]
        self.scratch = [pltpu.SemaphoreType.DMA((n, 7)), pltpu.SemaphoreType.DMA((n, 7)),
                        pltpu.SemaphoreType.DMA((n,))]

    @staticmethod
    def _places():
        x, y, c = me = _me()
        return me, (x, y, 1 - c), [(1 - x, y), (x, 1 - y), (1 - x, 1 - y)]

    @staticmethod
    def _copy(outs, sems, t, k, block, to, src=None):
        slot = outs[t].at[_idx(block)]
        return pltpu.make_async_remote_copy(
            src_ref=slot if src is None else src, dst_ref=slot, send_sem=sems[0].at[t, k],
            recv_sem=sems[1].at[t, k], device_id=to, device_id_type=MESH)

    def _firsts(self, ins, outs, sems):
        me, sibling, chips = self._places()
        out = []
        for t in range(len(ins)):
            out.append(self._copy(outs, sems, t, 0, me, sibling, src=ins[t]))
            out += [self._copy(outs, sems, t, 1 + j, me, (*chip, me[2]), src=ins[t]) for j, chip in enumerate(chips)]
        return out

    def _locals(self, ins, outs, sems):
        me = _me()
        return [pltpu.make_async_copy(ins[t], outs[t].at[_idx(me)], sems[2].at[t]) for t in range(len(ins))]

    def start(self, ins, outs, sems):
        for cp in self._locals(ins, outs, sems) + self._firsts(ins, outs, sems):
            cp.start()

    def mid(self, ins, outs, sems):
        me, sibling, chips = self._places()
        for j, chip in enumerate(chips):
            for t in range(len(ins)):
                self._copy(outs, sems, t, 1 + j, (*chip, me[2]), me).wait_recv()
                self._copy(outs, sems, t, 4 + j, (*chip, me[2]), sibling).start()

    def finish(self, ins, outs, sems):
        me, sibling, chips = self._places()
        for t in range(len(ins)):
            self._copy(outs, sems, t, 0, sibling, me).wait_recv()
            for j, chip in enumerate(chips):
                self._copy(outs, sems, t, 4 + j, (*chip, 1 - me[2]), me).wait_recv()
        for cp in self._firsts(ins, outs, sems):
            cp.wait_send()
        for j, chip in enumerate(chips):
            for t in range(len(ins)):
                self._copy(outs, sems, t, 4 + j, (*chip, me[2]), sibling).wait_send()
        for cp in self._locals(ins, outs, sems):
            cp.wait()


class _AllToAll:
    def __init__(self, srcs):
        self.srcs = list(srcs)
        n = len(self.srcs)
        self.out_shape = [jax.ShapeDtypeStruct(s.shape, s.dtype) for s in self.srcs]
        self.scratch = [pltpu.SemaphoreType.DMA((n, 7)), pltpu.SemaphoreType.DMA((n, 7)),
                        pltpu.SemaphoreType.DMA((n,))]

    def _copies(self, ins, outs, sems):
        me = _me()
        loc, rem = [], []
        for t in range(len(ins)):
            loc.append(pltpu.make_async_copy(ins[t].at[_idx(me)], outs[t].at[_idx(me)], sems[2].at[t]))
            for k, p in enumerate(_peers(me)):
                rem.append(pltpu.make_async_remote_copy(
                    src_ref=ins[t].at[_idx(p)], dst_ref=outs[t].at[_idx(me)], send_sem=sems[0].at[t, k],
                    recv_sem=sems[1].at[t, k], device_id=p, device_id_type=MESH))
        return loc, rem

    def start(self, ins, outs, sems):
        loc, rem = self._copies(ins, outs, sems)
        for cp in loc + rem:
            cp.start()

    def mid(self, ins, outs, sems):
        pass

    def finish(self, ins, outs, sems):
        me = _me()
        for t in range(len(ins)):
            for k, p in enumerate(_peers(me)):
                slot = outs[t].at[_idx(p)]
                pltpu.make_async_remote_copy(
                    src_ref=slot, dst_ref=slot, send_sem=sems[0].at[t, k], recv_sem=sems[1].at[t, k],
                    device_id=p, device_id_type=MESH).wait_recv()
        loc, rem = self._copies(ins, outs, sems)
        for cp in rem:
            cp.wait_send()
        for cp in loc:
            cp.wait()


class _Multi:
    def __init__(self, *exs):
        self.exs = exs
        self.srcs = [s for e in exs for s in e.srcs]
        self.out_shape = [s for e in exs for s in e.out_shape]
        self.scratch = [s for e in exs for s in e.scratch]

    def _each(self, ins, outs, sems):
        i = j = 0
        for e in self.exs:
            n, m = len(e.srcs), len(e.scratch)
            yield e, ins[i:i + n], outs[i:i + n], sems[j:j + m]
            i, j = i + n, j + m

    def start(self, ins, outs, sems):
        for e, a, b, c in self._each(ins, outs, sems):
            e.start(a, b, c)

    def mid(self, ins, outs, sems):
        for e, a, b, c in self._each(ins, outs, sems):
            e.mid(a, b, c)

    def finish(self, ins, outs, sems):
        for e, a, b, c in self._each(ins, outs, sems):
            e.finish(a, b, c)


def _call(body, *, name, grid, in_specs, out_specs, out_shape, args, scratch_shapes=(), rider=None):
    in_specs, out_specs, out_shape = list(in_specs), list(out_specs), list(out_shape)
    scratch_shapes = list(scratch_shapes)
    if rider is None:
        return pl.pallas_call(body, name=name, grid=grid, out_shape=out_shape, in_specs=in_specs,
                              out_specs=out_specs, scratch_shapes=scratch_shapes, compiler_params=_params())(*args)
    nci, nco, ncs, nr = len(in_specs), len(out_shape), len(scratch_shapes), len(rider.srcs)
    nsteps = math.prod(grid)
    mid = min((3 * nsteps) // 4, nsteps - 2)
    assert 0 < mid, (name, grid)

    def full(*refs):
        ci, ri = refs[:nci], refs[nci:nci + nr]
        co, ro = refs[nci + nr:nci + nr + nco], refs[nci + nr + nco:nci + 2 * nr + nco]
        cs, rs = refs[nci + 2 * nr + nco:nci + 2 * nr + nco + ncs], refs[nci + 2 * nr + nco + ncs:]
        step = pl.program_id(0)
        for d in range(1, len(grid)):
            step = step * grid[d] + pl.program_id(d)

        @pl.when(step == 0)
        def _():
            rider.start(ri, ro, rs)

        @pl.when(step == mid)
        def _():
            rider.mid(ri, ro, rs)

        body(*ci, *co, *cs)

        @pl.when(step == nsteps - 1)
        def _():
            rider.finish(ri, ro, rs)

    any_spec = pl.BlockSpec(memory_space=pl.ANY)
    return pl.pallas_call(
        full, name=name, grid=grid, out_shape=out_shape + rider.out_shape,
        in_specs=in_specs + [any_spec] * nr, out_specs=out_specs + [any_spec] * nr,
        scratch_shapes=scratch_shapes + rider.scratch, compiler_params=_params(),
    )(*args, *rider.srcs)


def _a2a_start(srcs, dsts, send_sems, recv_sems, local_sems, me, sem_base=0):
    peers = _peers(me)
    started = []
    for t in range(len(srcs)):
        loc = pltpu.make_async_copy(srcs[t].at[_idx(me)], dsts[t].at[_idx(me)], local_sems.at[sem_base + t])
        loc.start()
        started.append(("local", loc))
        for k, p in enumerate(peers):
            cp = pltpu.make_async_remote_copy(
                src_ref=srcs[t].at[_idx(p)], dst_ref=dsts[t].at[_idx(me)],
                send_sem=send_sems.at[sem_base + t, k], recv_sem=recv_sems.at[sem_base + t, k],
                device_id=p, device_id_type=MESH)
            cp.start()
            started.append(("remote", cp))
    return started


def _a2a_finish(started, dsts, send_sems, recv_sems, me, sem_base=0):
    peers = _peers(me)
    for t in range(len(dsts)):
        for k, p in enumerate(peers):
            slot = dsts[t].at[_idx(p)]
            pltpu.make_async_remote_copy(
                src_ref=slot, dst_ref=slot, send_sem=send_sems.at[sem_base + t, k],
                recv_sem=recv_sems.at[sem_base + t, k], device_id=p, device_id_type=MESH).wait_recv()
    for kind, cp in started:
        if kind == "local":
            cp.wait()
        else:
            cp.wait_send()


def ada_modulation(c8, ada_w, ada_b, ride):
    nr = len(ride.srcs)

    def body(c_ref, w_ref, b_ref, *rest):
        ride_in, (call_ref, modr_ref), ride_out = rest[:nr], rest[nr:nr + 2], rest[nr + 2:2 * nr + 2]
        modp, send_sems, recv_sems, local_sems = rest[2 * nr + 2:2 * nr + 6]
        ride_sems = rest[2 * nr + 6:]
        ride.start(ride_in, ride_out, ride_sems)
        me = _me()
        peers = _peers(me)
        sends = []
        for k, p in enumerate(peers):
            cp = pltpu.make_async_remote_copy(
                src_ref=c_ref, dst_ref=call_ref.at[_idx(me)], send_sem=send_sems.at[0, k],
                recv_sem=recv_sems.at[0, k], device_id=p, device_id_type=MESH)
            cp.start()
            sends.append(cp)
        call_ref[_idx(me)] = c_ref[...]
        for k, p in enumerate(peers):
            slot = call_ref.at[_idx(p)]
            pltpu.make_async_remote_copy(
                src_ref=slot, dst_ref=slot, send_sem=send_sems.at[0, k], recv_sem=recv_sems.at[0, k],
                device_id=p, device_id_type=MESH).wait_recv()
        for cp in sends:
            cp.wait_send()
        cv = call_ref[...].reshape(N_DEV * 8, D_MODEL)
        s = (cv * _sigmoid(cv)).astype(BF16)
        for k in range(4):
            res = _nn(s, w_ref[k].astype(BF16)) + b_ref[k]
            for j in range(N_DEV):
                modp[j, 8 * k:8 * k + 8, :] = res[8 * j:8 * j + 8, :]
        started = _a2a_start([modp], [modr_ref], send_sems, recv_sems, local_sems, me, sem_base=1)
        _a2a_finish(started, [modr_ref], send_sems, recv_sems, me, sem_base=1)
        ride.mid(ride_in, ride_out, ride_sems)
        ride.finish(ride_in, ride_out, ride_sems)

    vm, hbm = pl.BlockSpec(memory_space=pltpu.VMEM), pl.BlockSpec(memory_space=pl.ANY)
    return pl.pallas_call(
        body, name="ada_modulation",
        out_shape=[jax.ShapeDtypeStruct((N_DEV, 8, D_MODEL), F32), jax.ShapeDtypeStruct((N_DEV, 32, CG), F32)]
        + ride.out_shape,
        in_specs=[vm, vm, vm] + [hbm] * nr, out_specs=[vm, vm] + [hbm] * nr,
        scratch_shapes=[pltpu.VMEM((N_DEV, 32, CG), F32), pltpu.SemaphoreType.DMA((2, 7)),
                        pltpu.SemaphoreType.DMA((2, 7)), pltpu.SemaphoreType.DMA((2,))] + ride.scratch,
        compiler_params=_params(),
    )(c8, ada_w, ada_b, *ride.srcs)


def epilogue(gsend, c_t, tail, ride):
    nr = len(ride.srcs)
    rt = tail.shape[1]

    def body(g_ref, ct_ref, t_ref, *rest):
        ride_in, (gw_ref, gb_ref, red_ref), ride_out = rest[:nr], rest[nr:nr + 3], rest[nr + 3:2 * nr + 3]
        grecv, trecv, send_sems, recv_sems, local_sems = rest[2 * nr + 3:2 * nr + 8]
        ride_sems = rest[2 * nr + 8:]
        ride.start(ride_in, ride_out, ride_sems)
        me = _me()
        started = _a2a_start([g_ref, t_ref], [grecv, trecv], send_sems, recv_sems, local_sems, me)
        _a2a_finish(started, [grecv, trecv], send_sems, recv_sems, me)
        acc = trecv[0]
        for j in range(1, N_DEV):
            acc = acc + trecv[j]
        red_ref[...] = acc
        ct = ct_ref[...]
        st = (ct * _sigmoid(ct)).astype(BF16).astype(F32)
        gb = jnp.zeros((8, CG), F32)
        for b in range(N_DEV):
            gb = gb + grecv[b]
        gb_ref[...] = gb
        for k in range(4):
            acc = jnp.zeros((D_MODEL, CG), F32)
            for b in range(N_DEV):
                row = grecv[b, k:k + 1, :].astype(BF16).astype(F32)
                acc = acc + st[:, b:b + 1] * row
            gw_ref[k] = acc
        ride.mid(ride_in, ride_out, ride_sems)
        ride.finish(ride_in, ride_out, ride_sems)

    vm, hbm = pl.BlockSpec(memory_space=pltpu.VMEM), pl.BlockSpec(memory_space=pl.ANY)
    return pl.pallas_call(
        body, name="epilogue",
        out_shape=[jax.ShapeDtypeStruct((4, D_MODEL, CG), F32), jax.ShapeDtypeStruct((8, CG), F32),
                   jax.ShapeDtypeStruct((rt, 128), F32)] + ride.out_shape,
        in_specs=[vm, vm, vm] + [hbm] * nr, out_specs=[vm, vm, vm] + [hbm] * nr,
        scratch_shapes=[pltpu.VMEM((N_DEV, 8, CG), F32), pltpu.VMEM((N_DEV, rt, 128), F32),
                        pltpu.SemaphoreType.DMA((2, 7)), pltpu.SemaphoreType.DMA((2, 7)),
                        pltpu.SemaphoreType.DMA((2,))] + ride.scratch,
        compiler_params=_params(),
    )(gsend, c_t, tail, *ride.srcs)


def _rope(t, cos, s1, s2):
    return t * cos + pltpu.roll(t, 16, 1) * s1 + pltpu.roll(t, HEAD - 16, 1) * s2


def _rope_bwd(d, cos, s1, s2):
    return d * cos + pltpu.roll(d * s1, HEAD - 16, 1) + pltpu.roll(d * s2, 16, 1)


def attn_in_fwd(x, mods, k, win_t, rope, rider=None):
    s = x.shape[0]
    tm = _blk(s, TM_MM)

    def body(x_ref, mod_ref, w_ref, c_ref, s1_ref, s2_ref, q_ref, k_ref, v_ref):
        shift, scale, _ = _mod(mod_ref, k)
        h = (x_ref[...] * (1.0 + scale) + shift).astype(BF16)
        qkv = _nt(h, w_ref[...])
        cos, s1, s2 = c_ref[...], s1_ref[...], s2_ref[...]
        for hh in range(N_Q + N_KV):
            r = _rope(qkv[:, HEAD * hh:HEAD * (hh + 1)], cos, s1, s2).astype(BF16)
            if hh < N_Q:
                q_ref[:, HEAD * hh:HEAD * (hh + 1)] = r
            else:
                k_ref[:, HEAD * (hh - N_Q):HEAD * (hh - N_Q + 1)] = r
        v_ref[...] = qkv[:, HEAD * (N_Q + N_KV):].astype(BF16)

    return _call(
        body, name="attn_in_fwd", grid=(s // tm,),
        out_shape=[jax.ShapeDtypeStruct((s, N_Q * HEAD), BF16), jax.ShapeDtypeStruct((s, N_KV * HEAD), BF16),
                   jax.ShapeDtypeStruct((s, N_KV * HEAD), BF16)],
        in_specs=[_row(tm, D_MODEL), _res(mods.shape), _res(win_t.shape),
                  _row(tm, HEAD), _row(tm, HEAD), _row(tm, HEAD)],
        out_specs=[_row(tm, N_Q * HEAD), _row(tm, N_KV * HEAD), _row(tm, N_KV * HEAD)],
        args=(x, mods, win_t, *rope), rider=rider)


def _kv_specs(nblk):
    w = N_KV * HEAD
    return [pl.BlockSpec((QBLK, w), lambda n: (jnp.maximum(n - 1, 0), 0)),
            pl.BlockSpec((QBLK, w), lambda n: (n, 0)),
            pl.BlockSpec((QBLK, w), lambda n: (jnp.minimum(n + 1, nblk - 1), 0))]


GROUP = N_Q // N_KV


def _attn_mask(n, s):
    qi = lax.broadcasted_iota(jnp.int32, (GROUP * QBLK, 3 * QBLK), 0) & (QBLK - 1)
    kj = lax.broadcasted_iota(jnp.int32, (GROUP * QBLK, 3 * QBLK), 1)
    rel = kj - QBLK - qi
    kpos = kj + (n - 1) * QBLK
    return (jnp.abs(rel) <= QBLK) & (kpos >= 0) & (kpos < s)


def _stack_heads(ref, kv):
    return jnp.concatenate([ref[:, HEAD * (GROUP * kv + j):HEAD * (GROUP * kv + j + 1)] for j in range(GROUP)], axis=0)


def _stack_sinks(sink_ref, kv):
    row = lax.broadcasted_iota(jnp.int32, (GROUP * QBLK, 1), 0)
    out = jnp.full((GROUP * QBLK, 1), sink_ref[0, GROUP * kv + GROUP - 1], F32)
    for j in range(GROUP - 2, -1, -1):
        out = jnp.where(row < QBLK * (j + 1), sink_ref[0, GROUP * kv + j], out)
    return out


def _attn_probs(qh, kh, valid, sink):
    sc = _nt(qh, kh) * (HEAD ** -0.5)
    sc = jnp.where(valid, sc, -1e30)
    m = jnp.maximum(jnp.max(sc, axis=-1, keepdims=True), sink)
    p = jnp.exp(sc - m)
    es = jnp.exp(sink - m)
    denom = jnp.sum(p, axis=-1, keepdims=True) + es
    return p / denom, es / denom


def attn_fwd(q, kk, v, sinks, rider=None):
    s = q.shape[0]
    nblk = s // QBLK

    def body(sink_ref, q_ref, kp, ko, kn, vp, vo, vn, o_ref):
        n = pl.program_id(0)
        kcat = jnp.concatenate([kp[...], ko[...], kn[...]], axis=0)
        vcat = jnp.concatenate([vp[...], vo[...], vn[...]], axis=0)
        valid = _attn_mask(n, s)
        for kv in range(N_KV):
            cols = slice(HEAD * kv, HEAD * (kv + 1))
            probs, _ = _attn_probs(_stack_heads(q_ref, kv), kcat[:, cols], valid, _stack_sinks(sink_ref, kv))
            og = _nn(probs.astype(BF16), vcat[:, cols]).astype(BF16)
            for j in range(GROUP):
                hq = GROUP * kv + j
                o_ref[:, HEAD * hq:HEAD * (hq + 1)] = og[QBLK * j:QBLK * (j + 1), :]

    return _call(
        body, name="attn_fwd", grid=(nblk,),
        out_shape=[jax.ShapeDtypeStruct((s, N_Q * HEAD), BF16)],
        in_specs=[pl.BlockSpec(memory_space=pltpu.SMEM), pl.BlockSpec((QBLK, N_Q * HEAD), lambda n: (n, 0))]
        + _kv_specs(nblk) + _kv_specs(nblk),
        out_specs=[pl.BlockSpec((QBLK, N_Q * HEAD), lambda n: (n, 0))],
        args=(sinks, q, kk, kk, kk, v, v, v), rider=rider)


def post_fwd(ypre, w, x, mods, k, lng, lnb, gate_act=None):
    s = x.shape[0]
    tm = _blk(s, TM_MM)
    kdim = w.shape[0]
    rnn = gate_act is not None

    def body(*refs):
        if rnn:
            gt_ref, hf_ref, hb_ref, w_ref, x_ref, mod_ref, g_ref, b_ref, xo_ref, y_ref, yp_ref = refs
            act, _ = _gelu_parts(gt_ref[...])
            yp = ((hf_ref[...] + hb_ref[...]) * act).astype(BF16)
            yp_ref[...] = yp
        else:
            yp_ref, w_ref, x_ref, mod_ref, g_ref, b_ref, xo_ref, y_ref = refs
            yp = yp_ref[...]
        _, _, gate = _mod(mod_ref, k)
        y = _nn(yp, w_ref[...])
        y_ref[...] = y
        xhat, _ = _ln_stats(ALPHA * x_ref[...] + (1.0 + gate) * y)
        xo_ref[...] = xhat * g_ref[...] + b_ref[...]

    act_in = list(gate_act) if rnn else [ypre]
    out_shape = [jax.ShapeDtypeStruct((s, D_MODEL), F32), jax.ShapeDtypeStruct((s, D_MODEL), F32)]
    out_specs = [_row(tm, D_MODEL), _row(tm, D_MODEL)]
    if rnn:
        out_shape.append(jax.ShapeDtypeStruct((s, kdim), BF16))
        out_specs.append(_row(tm, kdim))
    return pl.pallas_call(
        body, name="rnn_post_fwd" if rnn else "attn_post_fwd", grid=(s // tm,),
        out_shape=out_shape,
        in_specs=[_row(tm, kdim)] * len(act_in) + [_res(w.shape), _row(tm, D_MODEL), _res(mods.shape),
                                                    _res(lng.shape), _res(lnb.shape)],
        out_specs=out_specs,
        compiler_params=_params(),
    )(*act_in, w, x, mods, lng, lnb)


def mlp_fwd(x, mods, k, w1_t, w2, lng, lnb, rider=None, last=False):
    s = x.shape[0]
    tm = _blk(s, TM_MLP)

    def body(x_ref, mod_ref, w1_ref, w2_ref, g_ref, b_ref, *outs):
        xo_ref = None if last else outs[0]
        y_ref, ra_ref, r_ref = outs[-3:]
        xv = x_ref[...]
        shift, scale, gate = _mod(mod_ref, k)
        h = (xv * (1.0 + scale) + shift).astype(BF16)
        y = jnp.zeros((tm, D_MODEL), F32)
        for c in range(D_FF // FF_CHUNK):
            rows = slice(FF_CHUNK * c, FF_CHUNK * (c + 1))
            a = jnp.maximum(_nt(h, w1_ref[rows, :]), 0.0)
            r = (a * a).astype(BF16)
            ra_ref[:, rows] = a.astype(BF16)
            r_ref[:, rows] = r
            y = y + _nn(r, w2_ref[rows, :])
        y_ref[...] = y
        if not last:
            xhat, _ = _ln_stats(ALPHA * xv + (1.0 + gate) * y)
            xo_ref[...] = xhat * g_ref[...] + b_ref[...]

    nf = 1 if last else 2
    return _call(
        body, name="mlp_fwd_last" if last else "mlp_fwd", grid=(s // tm,),
        out_shape=[jax.ShapeDtypeStruct((s, D_MODEL), F32)] * nf + [jax.ShapeDtypeStruct((s, D_FF), BF16)] * 2,
        in_specs=[_row(tm, D_MODEL), _res(mods.shape), _res(w1_t.shape), _res(w2.shape),
                  _res(lng.shape), _res(lnb.shape)],
        out_specs=[_row(tm, D_MODEL)] * nf + [_row(tm, D_FF)] * 2,
        args=(x, mods, w1_t, w2, lng, lnb), rider=rider)


def rnn_in_fwd(x, mods, k, win_t):
    s = x.shape[0]
    tm = _blk(s, TM_MM)

    def body(x_ref, mod_ref, w_ref, xr_ref, gt_ref):
        shift, scale, _ = _mod(mod_ref, k)
        h = (x_ref[...] * (1.0 + scale) + shift).astype(BF16)
        xr_ref[...] = _nt(h, w_ref[0:D_RNN, :])
        gt_ref[...] = _nt(h, w_ref[D_RNN:2 * D_RNN, :])

    return pl.pallas_call(
        body, name="rnn_in_fwd", grid=(s // tm,),
        out_shape=[jax.ShapeDtypeStruct((s, D_RNN), F32)] * 2,
        in_specs=[_row(tm, D_MODEL), _res(mods.shape), _res(win_t.shape)],
        out_specs=[_row(tm, D_RNN)] * 2,
        compiler_params=_params(),
    )(x, mods, win_t)


def _shift_rows(v, k, row):
    n = v.shape[0]
    r = pltpu.roll(v, k % n, 0)
    keep = (row >= k) if k > 0 else (row < n + k)
    return jnp.where(keep, r, 0.0)


def conv_fwd(xr, cw, cb):
    s = xr.shape[0]

    def body(x_ref, w_ref, b_ref, o_ref):
        xv = x_ref[...]
        row = lax.broadcasted_iota(jnp.int32, xv.shape, 0)
        o_ref[...] = (b_ref[...] + w_ref[0:1, :] * _shift_rows(xv, 2, row) + w_ref[1:2, :] * _shift_rows(xv, 1, row)
                      + w_ref[2:3, :] * xv + w_ref[3:4, :] * _shift_rows(xv, -1, row))

    slab = pl.BlockSpec((s, 128), lambda j: (0, j))
    return pl.pallas_call(
        body, name="conv_fwd", grid=(D_RNN // 128,),
        out_shape=jax.ShapeDtypeStruct((s, D_RNN), F32),
        in_specs=[slab, pl.BlockSpec((4, 128), lambda j: (0, j)), pl.BlockSpec((1, 128), lambda j: (0, j))],
        out_specs=slab,
        compiler_params=_params(),
    )(xr, cw, cb)


def conv_bwd(da, db, xr, cw):
    s = xr.shape[0]

    def body(da_ref, db_ref, x_ref, w_ref, dx_ref, dw_ref, dbias_ref):
        d = da_ref[...] + db_ref[...]
        xv = x_ref[...]
        row = lax.broadcasted_iota(jnp.int32, xv.shape, 0)
        dx_ref[...] = (w_ref[0:1, :] * _shift_rows(d, -2, row) + w_ref[1:2, :] * _shift_rows(d, -1, row)
                       + w_ref[2:3, :] * d + w_ref[3:4, :] * _shift_rows(d, 1, row))
        dw_ref[0:1, :] = _colsum(d * _shift_rows(xv, 2, row))
        dw_ref[1:2, :] = _colsum(d * _shift_rows(xv, 1, row))
        dw_ref[2:3, :] = _colsum(d * xv)
        dw_ref[3:4, :] = _colsum(d * _shift_rows(xv, -1, row))
        dbias_ref[...] = _colsum(d)

    slab = pl.BlockSpec((s, 128), lambda j: (0, j))
    return pl.pallas_call(
        body, name="conv_bwd", grid=(D_RNN // 128,),
        out_shape=[jax.ShapeDtypeStruct((s, D_RNN), F32), jax.ShapeDtypeStruct((4, D_RNN), F32),
                   jax.ShapeDtypeStruct((1, D_RNN), F32)],
        in_specs=[slab, slab, slab, pl.BlockSpec((4, 128), lambda j: (0, j))],
        out_specs=[slab, pl.BlockSpec((4, 128), lambda j: (0, j)), pl.BlockSpec((1, 128), lambda j: (0, j))],
        compiler_params=_params(),
    )(da, db, xr, cw)


def _softplus_neg(lam):
    z = -lam
    e = jnp.exp(-jnp.abs(z))
    u = 1.0 + e
    log1p = jnp.where(u == 1.0, e, jnp.log(u) * e / jnp.where(u == 1.0, 1.0, u - 1.0))
    return jnp.maximum(z, 0.0) + log1p, 1.0 / (1.0 + jnp.exp(lam))


def _lru_gates(xv, wa_ref, wx_ref, ba_ref, bx_ref, lam_ref):
    xb = xv.astype(BF16)
    r = _sigmoid(_nn(xb, wa_ref[...]) + ba_ref[...])
    i = _sigmoid(_nn(xb, wx_ref[...]) + bx_ref[...])
    sp, sg = _softplus_neg(lam_ref[...])
    la = r * (-LRU_C * sp)
    a = jnp.exp(la)
    th = jnp.tanh(la)
    m2 = -2.0 * th / (1.0 - th)
    rmult = lax.rsqrt(jnp.maximum(m2, 1e-37))
    return xb, r, i, sp, sg, a, m2 * rmult, rmult


def _scan(a, u, h0, reverse):
    n, c = a.shape
    sub = lax.broadcasted_iota(jnp.int32, (8, c), 0)
    steps = [(8 - sh, sub < 8 - sh) if reverse else (sh, sub >= sh) for sh in (1, 2, 4)]
    out = [None] * (n // 8)
    edge = h0
    for k in (range(n // 8 - 1, -1, -1) if reverse else range(n // 8)):
        at, ut = a[8 * k:8 * k + 8], u[8 * k:8 * k + 8]
        for rot, keep in steps:
            a_s = jnp.where(keep, pltpu.roll(at, rot, 0), 1.0)
            u_s = jnp.where(keep, pltpu.roll(ut, rot, 0), 0.0)
            ut = at * u_s + ut
            at = at * a_s
        hk = ut + at * edge
        out[k] = hk
        edge = hk[0:1] if reverse else hk[7:8]
    return jnp.concatenate(out, axis=0)


def _lru_specs(nt, tt, reverse):
    tmap = (lambda t: nt - 1 - t) if reverse else (lambda t: t)
    blk = pl.BlockSpec((tt, CG), lambda g, t: (tmap(t), g))
    wsp = pl.BlockSpec((None, CG, CG), lambda g, t: (g, 0, 0))
    vec = pl.BlockSpec((1, CG), lambda g, t: (0, g))
    return tmap, blk, wsp, vec


def lru_fwd(xc, wa, wx, ba, bx, lam, reverse):
    s = xc.shape[0]
    tt = _blk(s, TT_RNN)
    sb = _blk(tt, SB_RNN)
    nt = s // tt

    def body(x_ref, wa_ref, wx_ref, ba_ref, bx_ref, lam_ref, hs_ref, carry):
        @pl.when(pl.program_id(1) == 0)
        def _():
            carry[...] = jnp.zeros(carry.shape, F32)

        xv = x_ref[...]
        _, _, i, _, _, a, mult, _ = _lru_gates(xv, wa_ref, wx_ref, ba_ref, bx_ref, lam_ref)
        u = mult * (i * xv)
        h0 = carry[0:1, :]
        order = range(tt // sb - 1, -1, -1) if reverse else range(tt // sb)
        for j in order:
            rows = slice(sb * j, sb * (j + 1))
            h = _scan(a[rows], u[rows], h0, reverse)
            hs_ref[rows, :] = h
            h0 = h[0:1, :] if reverse else h[sb - 1:sb, :]
        carry[0:1, :] = h0

    _, blk, wsp, vec = _lru_specs(nt, tt, reverse)
    return pl.pallas_call(
        body, name="lru_fwd_rev" if reverse else "lru_fwd", grid=(N_CG, nt),
        out_shape=jax.ShapeDtypeStruct((s, D_RNN), F32),
        in_specs=[blk, wsp, wsp, vec, vec, vec], out_specs=blk,
        scratch_shapes=[pltpu.VMEM((8, CG), F32)],
        compiler_params=_params(),
    )(xc, wa, wx, ba, bx, lam)


def lru_bwd(xc, dhs, hs, wa, wx, ba, bx, lam, reverse, rider=None):
    s = xc.shape[0]
    tt = _blk(s, TT_RNN)
    sb = _blk(tt, SB_RNN)
    nt = s // tt
    back = not reverse

    def body(x_ref, dh_ref, hs_ref, nb_ref, wa_ref, wx_ref, ba_ref, bx_ref, lam_ref,
             dx_ref, dwa_ref, dwx_ref, dba_ref, dbx_ref, dlam_ref, carry):
        t = pl.program_id(1)

        @pl.when(t == 0)
        def _():
            carry[...] = jnp.zeros(carry.shape, F32)
            dwa_ref[...] = jnp.zeros(dwa_ref.shape, F32)
            dwx_ref[...] = jnp.zeros(dwx_ref.shape, F32)
            dba_ref[...] = jnp.zeros(dba_ref.shape, F32)
            dbx_ref[...] = jnp.zeros(dbx_ref.shape, F32)
            dlam_ref[...] = jnp.zeros(dlam_ref.shape, F32)

        xv = x_ref[...]
        xb, r, i, sp, sg, a, mult, rmult = _lru_gates(xv, wa_ref, wx_ref, ba_ref, bx_ref, lam_ref)
        row = lax.broadcasted_iota(jnp.int32, xv.shape, 0)
        hsv = hs_ref[...]
        inner = t < nt - 1
        if reverse:
            edge = jnp.where(inner, nb_ref[0:1, :], 0.0)
            hprev = jnp.where(row == tt - 1, edge, pltpu.roll(hsv, tt - 1, 0))
            a_next = jnp.where(row == 0, carry[1:2, :], pltpu.roll(a, 1, 0))
        else:
            edge = jnp.where(inner, nb_ref[7:8, :], 0.0)
            hprev = jnp.where(row == 0, edge, pltpu.roll(hsv, 1, 0))
            a_next = jnp.where(row == tt - 1, carry[1:2, :], pltpu.roll(a, tt - 1, 0))
        dhv = dh_ref[...]
        g0 = carry[0:1, :]
        parts = [None] * (tt // sb)
        order = range(tt // sb - 1, -1, -1) if back else range(tt // sb)
        for j in order:
            rows = slice(sb * j, sb * (j + 1))
            gj = _scan(a_next[rows], dhv[rows], g0, back)
            parts[j] = gj
            g0 = gj[0:1, :] if back else gj[sb - 1:sb, :]
        g = jnp.concatenate(parts, axis=0) if len(parts) > 1 else parts[0]
        carry[0:1, :] = g0
        carry[1:2, :] = a[0:1, :] if back else a[tt - 1:tt, :]

        da = g * hprev
        dmult = g * (i * xv)
        di = g * mult * xv
        dla = da * a - dmult * (a * a) * rmult
        dpa = (dla * (-LRU_C * sp)) * r * (1.0 - r)
        dpx = di * i * (1.0 - i)
        dlam_ref[...] += _colsum(dla * (LRU_C * r * sg))
        dba_ref[...] += _colsum(dpa)
        dbx_ref[...] += _colsum(dpx)
        dpab, dpxb = dpa.astype(BF16), dpx.astype(BF16)
        dx_ref[...] = g * mult * i + _nt(dpab, wa_ref[...]) + _nt(dpxb, wx_ref[...])
        dwa_ref[...] += _tn(xb, dpab)
        dwx_ref[...] += _tn(xb, dpxb)

    tmap, blk, wsp, vec = _lru_specs(nt, tt, back)
    per8 = tt // 8
    if reverse:
        nb = pl.BlockSpec((8, CG), lambda g, t: (jnp.minimum((tmap(t) + 1) * per8, s // 8 - 1), g))
    else:
        nb = pl.BlockSpec((8, CG), lambda g, t: (jnp.maximum(tmap(t) * per8 - 1, 0), g))
    return _call(
        body, name="lru_bwd_rev" if reverse else "lru_bwd", grid=(N_CG, nt),
        out_shape=[jax.ShapeDtypeStruct((s, D_RNN), F32), jax.ShapeDtypeStruct((N_CG, CG, CG), F32),
                   jax.ShapeDtypeStruct((N_CG, CG, CG), F32)] + [jax.ShapeDtypeStruct((1, D_RNN), F32)] * 3,
        in_specs=[blk, blk, blk, nb, wsp, wsp, vec, vec, vec],
        out_specs=[blk, wsp, wsp, vec, vec, vec],
        scratch_shapes=[pltpu.VMEM((8, CG), F32)],
        args=(xc, dhs, hs, hs, wa, wx, ba, bx, lam), rider=rider)


def _ln_part_bwd(dxo, x, y, gate, g, sums_ref, loss_head=None):
    xhat, rstd = _ln_stats(ALPHA * x + (1.0 + gate) * y)
    if loss_head is not None:
        err = xhat * g + loss_head[0] - loss_head[1]
        dxo = err * (1.0 / D_MODEL)
        sums_ref[5:6, :] += _colsum(err * err)
    dz = _ln_bwd(dxo, xhat, rstd, g)
    sums_ref[2:3, :] += _colsum(dz * y)
    sums_ref[3:4, :] += _colsum(dxo * xhat)
    sums_ref[4:5, :] += _colsum(dxo)
    return dz


def mlp_bwd(dxo, x, y, ra, mods, k, w1_t, w2, lng, lnb=None, rider=None):
    s = x.shape[0]
    tm = _blk(s, TM_MLP)
    head = lnb is not None

    def body(d_ref, x_ref, y_ref, ra_ref, mod_ref, w1_ref, w2_ref, g_ref, *rest):
        b_ref = rest[0] if head else None
        dx_ref, da_ref, h_ref, dy_ref, sums_ref = rest[1:] if head else rest

        @pl.when(pl.program_id(0) == 0)
        def _():
            sums_ref[...] = jnp.zeros(sums_ref.shape, F32)

        xv = x_ref[...]
        shift, scale, gate = _mod(mod_ref, k)
        if head:
            dz = _ln_part_bwd(None, xv, y_ref[...], gate, g_ref[...], sums_ref, (b_ref[...], d_ref[...]))
        else:
            dz = _ln_part_bwd(d_ref[...], xv, y_ref[...], gate, g_ref[...], sums_ref)
        dyb = (dz * (1.0 + gate)).astype(BF16)
        dy_ref[...] = dyb
        h = (xv * (1.0 + scale) + shift).astype(BF16)
        h_ref[...] = h
        dh = jnp.zeros((tm, D_MODEL), F32)
        for c in range(D_FF // FF_CHUNK):
            rows = slice(FF_CHUNK * c, FF_CHUNK * (c + 1))
            da = (_nt(dyb, w2_ref[rows, :]) * (2.0 * ra_ref[:, rows].astype(F32))).astype(BF16)
            da_ref[:, rows] = da
            dh = dh + _nn(da, w1_ref[rows, :])
        dx_ref[...] = ALPHA * dz + dh * (1.0 + scale)
        sums_ref[0:1, :] += _colsum(dh)
        sums_ref[1:2, :] += _colsum(dh * xv)

    return _call(
        body, name="mlp_bwd", grid=(s // tm,),
        out_shape=[jax.ShapeDtypeStruct((s, D_MODEL), F32), jax.ShapeDtypeStruct((s, D_FF), BF16),
                   jax.ShapeDtypeStruct((s, D_MODEL), BF16),
                   jax.ShapeDtypeStruct((s, D_MODEL), BF16), jax.ShapeDtypeStruct((8, D_MODEL), F32)],
        in_specs=[_row(tm, D_MODEL)] * 3 + [_row(tm, D_FF), _res(mods.shape), _res(w1_t.shape), _res(w2.shape),
                                             _res(lng.shape)] + ([_res(lnb.shape)] if head else []),
        out_specs=[_row(tm, D_MODEL), _row(tm, D_FF), _row(tm, D_MODEL), _row(tm, D_MODEL), _res((8, D_MODEL))],
        args=(dxo, x, y, ra, mods, w1_t, w2, lng) + ((lnb,) if head else ()), rider=rider)


def post_bwd(dxo, x, y, mods, k, w, lng, gate_act=None, rider=None):
    s = x.shape[0]
    tm = _blk(s, TM_MM)
    kdim = w.shape[0]
    rnn = gate_act is not None

    def body(*refs):
        if rnn:
            (d_ref, x_ref, y_ref, mod_ref, w_ref, g_ref, gt_ref, hf_ref, hb_ref,
             dres_ref, dy_ref, sums_ref, dhs_ref, dgt_ref) = refs
        else:
            d_ref, x_ref, y_ref, mod_ref, w_ref, g_ref, dres_ref, dy_ref, sums_ref, dyp_ref = refs

        @pl.when(pl.program_id(0) == 0)
        def _():
            sums_ref[...] = jnp.zeros(sums_ref.shape, F32)

        _, _, gate = _mod(mod_ref, k)
        dz = _ln_part_bwd(d_ref[...], x_ref[...], y_ref[...], gate, g_ref[...], sums_ref)
        dres_ref[...] = ALPHA * dz
        dyb = (dz * (1.0 + gate)).astype(BF16)
        dy_ref[...] = dyb
        dyp = _nt(dyb, w_ref[...])
        if rnn:
            act, dact = _gelu_parts(gt_ref[...])
            dhs_ref[...] = dyp * act
            dgt_ref[...] = dyp * (hf_ref[...] + hb_ref[...]) * dact
        else:
            dyp_ref[...] = dyp.astype(BF16)

    ins = [dxo, x, y, mods, w, lng] + (list(gate_act) if rnn else [])
    in_specs = [_row(tm, D_MODEL)] * 3 + [_res(mods.shape), _res(w.shape), _res(lng.shape)]
    out_shape = [jax.ShapeDtypeStruct((s, D_MODEL), F32), jax.ShapeDtypeStruct((s, D_MODEL), BF16),
                 jax.ShapeDtypeStruct((8, D_MODEL), F32)]
    out_specs = [_row(tm, D_MODEL), _row(tm, D_MODEL), _res((8, D_MODEL))]
    if rnn:
        in_specs += [_row(tm, kdim)] * 3
        out_shape += [jax.ShapeDtypeStruct((s, kdim), F32)] * 2
        out_specs += [_row(tm, kdim)] * 2
    else:
        out_shape.append(jax.ShapeDtypeStruct((s, kdim), BF16))
        out_specs.append(_row(tm, kdim))
    return _call(
        body, name="rnn_post_bwd" if rnn else "attn_post_bwd", grid=(s // tm,),
        out_shape=out_shape, in_specs=in_specs, out_specs=out_specs, args=ins, rider=rider)


def attn_bwd(q, kk, v, do, sinks, rider=None):
    s = q.shape[0]
    nblk = s // QBLK
    scale = HEAD ** -0.5

    def body(sink_ref, q_ref, do_ref, kp, ko, kn, vp, vo, vn, dq_ref, dkp_ref, dvp_ref, ds_ref):
        n = pl.program_id(0)

        @pl.when(n == 0)
        def _():
            ds_ref[...] = jnp.zeros(ds_ref.shape, F32)

        kcat = jnp.concatenate([kp[...], ko[...], kn[...]], axis=0)
        vcat = jnp.concatenate([vp[...], vo[...], vn[...]], axis=0)
        valid = _attn_mask(n, s)
        lane = lax.broadcasted_iota(jnp.int32, (1, 128), 1)
        dsink = jnp.zeros((1, 128), F32)
        for kv in range(N_KV):
            cols = slice(HEAD * kv, HEAD * (kv + 1))
            qg, dog = _stack_heads(q_ref, kv), _stack_heads(do_ref, kv)
            kh, vh = kcat[:, cols], vcat[:, cols]
            probs, psink = _attn_probs(qg, kh, valid, _stack_sinks(sink_ref, kv))
            dprobs = _nt(dog, vh)
            dvp_ref[:, cols] = _tn(probs.astype(BF16), dog)
            rowdot = jnp.sum(probs * dprobs, axis=-1, keepdims=True)
            dsb = (probs * (dprobs - rowdot) * scale).astype(BF16)
            dqg = _nn(dsb, kh)
            dkp_ref[:, cols] = _tn(dsb, qg)
            dsk = -psink * rowdot
            for j in range(GROUP):
                hq = GROUP * kv + j
                dq_ref[:, HEAD * hq:HEAD * (hq + 1)] = dqg[QBLK * j:QBLK * (j + 1), :]
                dsink = dsink + jnp.where(lane == hq, _colsum(dsk[QBLK * j:QBLK * (j + 1), :]), 0.0)
        ds_ref[...] += dsink

    qspec = pl.BlockSpec((QBLK, N_Q * HEAD), lambda n: (n, 0))
    pspec = pl.BlockSpec((None, 3 * QBLK, N_KV * HEAD), lambda n: (n, 0, 0))
    return _call(
        body, name="attn_bwd", grid=(nblk,),
        out_shape=[jax.ShapeDtypeStruct((s, N_Q * HEAD), F32),
                   jax.ShapeDtypeStruct((nblk, 3 * QBLK, N_KV * HEAD), F32),
                   jax.ShapeDtypeStruct((nblk, 3 * QBLK, N_KV * HEAD), F32),
                   jax.ShapeDtypeStruct((1, 128), F32)],
        in_specs=[pl.BlockSpec(memory_space=pltpu.SMEM), qspec, qspec] + _kv_specs(nblk) + _kv_specs(nblk),
        out_specs=[qspec, pspec, pspec, pl.BlockSpec((1, 128), lambda n: (0, 0))],
        args=(sinks, q, do, kk, kk, kk, v, v, v), rider=rider)


def kv_combine(dkp, dvp, rider=None):
    nblk = dkp.shape[0]
    w = N_KV * HEAD

    def body(kp, ko, kn, vp, vo, vn, dk_ref, dv_ref):
        n = pl.program_id(0)
        dk_ref[...] = jnp.where(n > 0, kp[...], 0.0) + ko[...] + jnp.where(n < nblk - 1, kn[...], 0.0)
        dv_ref[...] = jnp.where(n > 0, vp[...], 0.0) + vo[...] + jnp.where(n < nblk - 1, vn[...], 0.0)

    specs = [pl.BlockSpec((None, QBLK, w), lambda n: (jnp.maximum(n - 1, 0), 2, 0)),
             pl.BlockSpec((None, QBLK, w), lambda n: (n, 1, 0)),
             pl.BlockSpec((None, QBLK, w), lambda n: (jnp.minimum(n + 1, nblk - 1), 0, 0))]
    out = pl.BlockSpec((QBLK, w), lambda n: (n, 0))
    return _call(
        body, name="kv_combine", grid=(nblk,),
        out_shape=[jax.ShapeDtypeStruct((nblk * QBLK, w), F32)] * 2,
        in_specs=specs + specs, out_specs=[out, out],
        args=(dkp, dkp, dkp, dvp, dvp, dvp), rider=rider)


def _in_bwd_tail(dzb, w_ref, x_ref, mod_ref, k, dres_ref, dx_ref, h_ref, sums_ref):
    xv = x_ref[...]
    shift, scale, _ = _mod(mod_ref, k)
    h_ref[...] = (xv * (1.0 + scale) + shift).astype(BF16)
    dh = _nn(dzb, w_ref[...])
    dx_ref[...] = dres_ref[...] + dh * (1.0 + scale)
    sums_ref[0:1, :] += _colsum(dh)
    sums_ref[1:2, :] += _colsum(dh * xv)


def attn_in_bwd(dq, dk, dv, rope, x, mods, k, win_t, dres):
    s = x.shape[0]
    tm = _blk(s, TM_MM)

    def body(dq_ref, dk_ref, dv_ref, c_ref, s1_ref, s2_ref, x_ref, mod_ref, w_ref, dres_ref,
             dx_ref, dz_ref, h_ref, sums_ref):
        @pl.when(pl.program_id(0) == 0)
        def _():
            sums_ref[...] = jnp.zeros(sums_ref.shape, F32)

        cos, s1, s2 = c_ref[...], s1_ref[...], s2_ref[...]
        for hh in range(N_Q + N_KV):
            src = dq_ref[:, HEAD * hh:HEAD * (hh + 1)] if hh < N_Q else dk_ref[:, HEAD * (hh - N_Q):HEAD * (hh - N_Q + 1)]
            dz_ref[:, HEAD * hh:HEAD * (hh + 1)] = _rope_bwd(src, cos, s1, s2).astype(BF16)
        dz_ref[:, HEAD * (N_Q + N_KV):] = dv_ref[...].astype(BF16)
        _in_bwd_tail(dz_ref[...], w_ref, x_ref, mod_ref, k, dres_ref, dx_ref, h_ref, sums_ref)

    return pl.pallas_call(
        body, name="attn_in_bwd", grid=(s // tm,),
        out_shape=[jax.ShapeDtypeStruct((s, D_MODEL), F32), jax.ShapeDtypeStruct((s, D_QKV), BF16),
                   jax.ShapeDtypeStruct((s, D_MODEL), BF16), jax.ShapeDtypeStruct((8, D_MODEL), F32)],
        in_specs=[_row(tm, N_Q * HEAD), _row(tm, N_KV * HEAD), _row(tm, N_KV * HEAD),
                  _row(tm, HEAD), _row(tm, HEAD), _row(tm, HEAD), _row(tm, D_MODEL),
                  _res(mods.shape), _res(win_t.shape), _row(tm, D_MODEL)],
        out_specs=[_row(tm, D_MODEL), _row(tm, D_QKV), _row(tm, D_MODEL), _res((8, D_MODEL))],
        compiler_params=_params(),
    )(dq, dk, dv, *rope, x, mods, win_t, dres)


def rnn_in_bwd(dxr, dgt, x, mods, k, win_t, dres):
    s = x.shape[0]
    tm = _blk(s, TM_MM)

    def body(dxr_ref, dgt_ref, x_ref, mod_ref, w_ref, dres_ref, dx_ref, dz_ref, h_ref, sums_ref):
        @pl.when(pl.program_id(0) == 0)
        def _():
            sums_ref[...] = jnp.zeros(sums_ref.shape, F32)

        dz_ref[:, 0:D_RNN] = dxr_ref[...].astype(BF16)
        dz_ref[:, D_RNN:2 * D_RNN] = dgt_ref[...].astype(BF16)
        _in_bwd_tail(dz_ref[...], w_ref, x_ref, mod_ref, k, dres_ref, dx_ref, h_ref, sums_ref)

    return pl.pallas_call(
        body, name="rnn_in_bwd", grid=(s // tm,),
        out_shape=[jax.ShapeDtypeStruct((s, D_MODEL), F32), jax.ShapeDtypeStruct((s, 2 * D_RNN), BF16),
                   jax.ShapeDtypeStruct((s, D_MODEL), BF16), jax.ShapeDtypeStruct((8, D_MODEL), F32)],
        in_specs=[_row(tm, D_RNN), _row(tm, D_RNN), _row(tm, D_MODEL), _res(mods.shape), _res(win_t.shape),
                  _row(tm, D_MODEL)],
        out_specs=[_row(tm, D_MODEL), _row(tm, 2 * D_RNN), _row(tm, D_MODEL), _res((8, D_MODEL))],
        compiler_params=_params(),
    )(dxr, dgt, x, mods, win_t, dres)


def wgrad(a, b, name):
    s, m = a.shape
    n = b.shape[1]
    tm = next(t for t in (1024, 768, 512, 384, 256, 128) if m % t == 0)
    tk = _blk(s, TK_WG)
    nk = s // tk

    def body(a_ref, b_ref, o_ref, acc):
        kk = pl.program_id(1)

        @pl.when(kk == 0)
        def _():
            acc[...] = jnp.zeros(acc.shape, F32)

        acc[...] += _tn(a_ref[...], b_ref[...])

        @pl.when(kk == nk - 1)
        def _():
            o_ref[...] = acc[...].astype(BF16)

    out = pl.pallas_call(
        body, name=name, grid=(m // tm, nk),
        out_shape=jax.ShapeDtypeStruct((m, n), BF16),
        in_specs=[pl.BlockSpec((tk, tm), lambda i, kk: (kk, i)), pl.BlockSpec((tk, n), lambda i, kk: (kk, 0))],
        out_specs=pl.BlockSpec((tm, n), lambda i, kk: (i, 0)),
        scratch_shapes=[pltpu.VMEM((tm, n), F32)],
        compiler_params=_params(),
    )(a, b)
    return out.reshape(N_DEV, m // N_DEV, n)


def part_sum(parts, name):
    _, r, c = parts.shape
    tr = next(t for t in (256, 192, 128, 64, 32, 16, 8) if r % t == 0)

    def body(p_ref, o_ref):
        acc = p_ref[0].astype(F32)
        for j in range(1, N_DEV):
            acc = acc + p_ref[j].astype(F32)
        o_ref[...] = acc

    return pl.pallas_call(
        body, name=name, grid=(r // tr,),
        out_shape=jax.ShapeDtypeStruct((r, c), F32),
        in_specs=[pl.BlockSpec((N_DEV, tr, c), lambda i: (0, i, 0))],
        out_specs=pl.BlockSpec((tr, c), lambda i: (i, 0)),
        compiler_params=_params(),
    )(parts)


def adamw(w, g, m, v, name):
    shape = w.shape
    c = shape[-1]
    r = w.size // c
    w2, g2, m2, v2 = (t.reshape(r, c) for t in (w, g, m, v))
    tr = r if r * c <= 512 * 1024 else next(t for t in (512, 256, 128, 64, 32, 16, 8) if r % t == 0)

    def body(w_ref, g_ref, m_ref, v_ref, d_ref, nm_ref, nv_ref):
        gv = g_ref[...]
        nm = B1 * m_ref[...] + (1.0 - B1) * gv
        nv = B2 * v_ref[...] + (1.0 - B2) * (gv * gv)
        nm_ref[...] = nm
        nv_ref[...] = nv
        m_hat = nm / (1.0 - B1 ** STEP)
        v_hat = nv / (1.0 - B2 ** STEP)
        d_ref[...] = -LR * (m_hat / (jnp.sqrt(v_hat) + ADAM_EPS) + WD * w_ref[...])

    spec = pl.BlockSpec((tr, c), lambda i: (i, 0))
    outs = pl.pallas_call(
        body, name=name, grid=(r // tr,),
        out_shape=[jax.ShapeDtypeStruct((r, c), F32)] * 3,
        in_specs=[spec] * 4, out_specs=[spec] * 3,
        compiler_params=_params(),
    )(w2, g2, m2, v2)
    return tuple(o.reshape(shape) for o in outs)


def _rope_tables(s):
    pos = jnp.arange(s, dtype=F32)
    inv_freq = THETA ** (-jnp.arange(0, ROT, 2, dtype=F32) / ROT)
    ang = pos[:, None] * inv_freq[None, :]
    cos, sin = lax.optimization_barrier((jnp.cos(ang), jnp.sin(ang)))
    half = ROT // 2
    zeros = jnp.zeros((s, HEAD - ROT), F32)
    c = jnp.concatenate([cos, cos, jnp.ones((s, HEAD - ROT), F32)], axis=1)
    s1 = jnp.concatenate([jnp.zeros((s, half), F32), sin, zeros], axis=1)
    s2 = jnp.concatenate([-sin, jnp.zeros((s, half), F32), zeros], axis=1)
    return c, s1, s2


def _blockdiag(w):
    w4 = w.reshape(N_CG, 4, RB_W, RB_W)
    eye = jnp.eye(4, dtype=w.dtype)
    return jnp.einsum("gipq,ij->gipjq", w4, eye).reshape(N_CG, CG, CG)


def _diag_blocks(w):
    w5 = w.reshape(N_CG, 4, RB_W, 4, RB_W)
    eye = jnp.eye(4, dtype=w.dtype)
    return jnp.einsum("gipjq,ij->gipq", w5, eye).reshape(N_RB, RB_W, RB_W)


def _cols(full, per):
    lead = full.shape[:-1]
    t = full.reshape(lead + (N_DEV, per))
    return jnp.moveaxis(t, -2, 0).reshape(N_DEV, -1)


def kernel(x, c, ada_w, ada_b, ln_g, ln_b, attn_w_in, attn_w_out, attn_sinks, rnn_w_in, rnn_conv_w, rnn_conv_b, rnn_w_a, rnn_b_a, rnn_w_x, rnn_b_x, rnn_lam, rnn_w_out, mlp_w1, mlp_w2, loss_target, m_ada_w, m_ada_b, m_ln_g, m_ln_b, m_attn_w_in, m_attn_w_out, m_attn_sinks, m_rnn_w_in, m_rnn_conv_w, m_rnn_conv_b, m_rnn_w_a, m_rnn_b_a, m_rnn_w_x, m_rnn_b_x, m_rnn_lam, m_rnn_w_out, m_mlp_w1, m_mlp_w2, v_ada_w, v_ada_b, v_ln_g, v_ln_b, v_attn_w_in, v_attn_w_out, v_attn_sinks, v_rnn_w_in, v_rnn_conv_w, v_rnn_conv_b, v_rnn_w_a, v_rnn_b_a, v_rnn_w_x, v_rnn_b_x, v_rnn_lam, v_rnn_w_out, v_mlp_w1, v_mlp_w2):
    s = x.shape[1]
    x0 = x.reshape(s, D_MODEL)
    target = loss_target.reshape(s, D_MODEL)
    weights = dict(ada_w=ada_w, ada_b=ada_b, ln_g=ln_g, ln_b=ln_b, attn_w_in=attn_w_in, attn_w_out=attn_w_out,
                   attn_sinks=attn_sinks, rnn_w_in=rnn_w_in, rnn_conv_w=rnn_conv_w, rnn_conv_b=rnn_conv_b,
                   rnn_w_a=rnn_w_a, rnn_b_a=rnn_b_a, rnn_w_x=rnn_w_x, rnn_b_x=rnn_b_x, rnn_lam=rnn_lam,
                   rnn_w_out=rnn_w_out, mlp_w1=mlp_w1, mlp_w2=mlp_w2)
    moments_m = dict(ada_w=m_ada_w, ada_b=m_ada_b, ln_g=m_ln_g, ln_b=m_ln_b, attn_w_in=m_attn_w_in,
                     attn_w_out=m_attn_w_out, attn_sinks=m_attn_sinks, rnn_w_in=m_rnn_w_in,
                     rnn_conv_w=m_rnn_conv_w, rnn_conv_b=m_rnn_conv_b, rnn_w_a=m_rnn_w_a, rnn_b_a=m_rnn_b_a,
                     rnn_w_x=m_rnn_w_x, rnn_b_x=m_rnn_b_x, rnn_lam=m_rnn_lam, rnn_w_out=m_rnn_w_out,
                     mlp_w1=m_mlp_w1, mlp_w2=m_mlp_w2)
    moments_v = dict(ada_w=v_ada_w, ada_b=v_ada_b, ln_g=v_ln_g, ln_b=v_ln_b, attn_w_in=v_attn_w_in,
                     attn_w_out=v_attn_w_out, attn_sinks=v_attn_sinks, rnn_w_in=v_rnn_w_in,
                     rnn_conv_w=v_rnn_conv_w, rnn_conv_b=v_rnn_conv_b, rnn_w_a=v_rnn_w_a, rnn_b_a=v_rnn_b_a,
                     rnn_w_x=v_rnn_w_x, rnn_b_x=v_rnn_b_x, rnn_lam=v_rnn_lam, rnn_w_out=v_rnn_w_out,
                     mlp_w1=v_mlp_w1, mlp_w2=v_mlp_w2)
    names = list(weights)

    def t16(w):
        return w.T.astype(BF16)

    big = [t16(attn_w_in[0]), attn_w_out[0].astype(BF16), t16(rnn_w_in[0]), rnn_w_out[0].astype(BF16),
           t16(mlp_w1[0]), mlp_w2[0].astype(BF16), t16(mlp_w1[1]), mlp_w2[1].astype(BF16)]
    small_local = jnp.concatenate([
        ln_g.reshape(-1), ln_b.reshape(-1), rnn_conv_w.reshape(-1), rnn_conv_b.reshape(-1),
        rnn_b_a.reshape(-1), rnn_b_x.reshape(-1), rnn_lam.reshape(-1)])
    small_local = jnp.pad(small_local, (0, 4096 - small_local.shape[0])).reshape(32, 128)
    flat = lambda g: g.reshape(N_DEV * g.shape[1], D_MODEL)
    c_all, modr, win_t, sm = ada_modulation(jnp.broadcast_to(c, (8, D_MODEL)), ada_w.reshape(4, D_MODEL, CG),
                                            ada_b.reshape(4, 1, CG), _Gather([big[0], small_local]))
    win_t = flat(win_t)
    sm = sm.reshape(N_DEV, 4096)

    def full_vec(off, rows, per):
        piece = sm[:, off:off + rows * per].reshape(N_DEV, rows, per)
        return jnp.moveaxis(piece, 0, 1).reshape(rows, N_DEV * per)

    lng_f, lnb_f = full_vec(0, 4, 128), full_vec(512, 4, 128)
    cw_f, cb_f = full_vec(1024, 4, 192), full_vec(1792, 1, 192)
    ba_f, bx_f, lam_f = full_vec(1984, 2, 192), full_vec(2368, 2, 192), full_vec(2752, 2, 192)
    wa_bd = [_blockdiag(rnn_w_a[0, d]).astype(BF16) for d in range(2)]
    wx_bd = [_blockdiag(rnn_w_x[0, d]).astype(BF16) for d in range(2)]

    mods = modr.reshape(N_DEV, 4, 8, CG)[:, :, 0, :]
    mods = jnp.moveaxis(mods, 0, 1).reshape(4, 3, D_MODEL).reshape(12, D_MODEL)
    rope = _rope_tables(s)
    ln = lambda k: (lng_f[k:k + 1], lnb_f[k:k + 1])

    q, kk, v, wout = attn_in_fwd(x0, mods, 0, win_t, rope, rider=_Gather([big[1]]))
    wout = flat(wout)
    o, *got = attn_fwd(q, kk, v, attn_sinks, rider=_Gather([big[4], big[5], big[2], big[3]]))
    w1t_0, w2_0, rin_t, rout = (flat(g) for g in got)
    x1, y0 = post_fwd(o, wout, x0, mods, 0, *ln(0))
    x2, y1, ra0, r0, *got = mlp_fwd(x1, mods, 1, w1t_0, w2_0, *ln(1), rider=_Gather([big[6], big[7]]))
    w1t_1, w2_1 = (flat(g) for g in got)
    xr, gt = rnn_in_fwd(x2, mods, 2, rin_t)
    xc = conv_fwd(xr, cw_f, cb_f)
    hf = lru_fwd(xc, wa_bd[0], wx_bd[0], ba_f[0:1], bx_f[0:1], lam_f[0:1], False)
    hb = lru_fwd(xc, wa_bd[1], wx_bd[1], ba_f[1:2], bx_f[1:2], lam_f[1:2], True)
    x3, y2, ypre = post_fwd(None, rout, x2, mods, 2, *ln(2), gate_act=(gt, hf, hb))
    y3, ra1, r1 = mlp_fwd(x3, mods, 3, w1t_1, w2_1, *ln(3), last=True)

    dx3, da1, h3, dy3, sums3 = mlp_bwd(target, x3, y3, ra1, mods, 3, w1t_1, w2_1, lng_f[3:4], lnb=lnb_f[3:4])
    g_w1t_1 = wgrad(da1, h3, "wgrad_w1_1")
    g_w2_1 = wgrad(r1, dy3, "wgrad_w2_1")
    dres2, dy2, sums2a, dhs, dgt, p_w1t_1 = post_bwd(dx3, x2, y2, mods, 2, rout, lng_f[2:3], gate_act=(gt, hf, hb),
                                                     rider=_AllToAll([g_w1t_1]))
    g_rout = wgrad(ypre, dy2, "wgrad_rnn_out")
    dxc_f, dwa_f, dwx_f, dba_f, dbx_f, dlam_f, p_w2_1, p_rout = lru_bwd(
        xc, dhs, hf, wa_bd[0], wx_bd[0], ba_f[0:1], bx_f[0:1], lam_f[0:1], False, rider=_AllToAll([g_w2_1, g_rout]))
    dxc_b, dwa_b, dwx_b, dba_b, dbx_b, dlam_b = lru_bwd(xc, dhs, hb, wa_bd[1], wx_bd[1], ba_f[1:2], bx_f[1:2],
                                                        lam_f[1:2], True)
    dxr, dcw, dcb = conv_bwd(dxc_f, dxc_b, xr, cw_f)
    dx2, dzz, h2, sums2b = rnn_in_bwd(dxr, dgt, x2, mods, 2, rin_t, dres2)
    g_rin_t = wgrad(dzz, h2, "wgrad_rnn_in")
    d_wa = jnp.stack([_diag_blocks(dwa_f), _diag_blocks(dwa_b)])
    d_wx = jnp.stack([_diag_blocks(dwx_f), _diag_blocks(dwx_b)])
    nflat = d_wa.size // N_DEV
    gates = jnp.concatenate([d_wa.reshape(N_DEV, nflat), d_wx.reshape(N_DEV, nflat)], axis=1)
    gates = gates.reshape(N_DEV, 2 * nflat // 128, 128)
    dx1, da0, h1, dy1, sums1, p_rin_t, p_gates = mlp_bwd(dx2, x1, y1, ra0, mods, 1, w1t_0, w2_0, lng_f[1:2],
                                                         rider=_AllToAll([g_rin_t, gates]))
    gates_sum = part_sum(p_gates, "part_sum_gates")
    g_w1t_0 = wgrad(da0, h1, "wgrad_w1_0")
    g_w2_0 = wgrad(r0, dy1, "wgrad_w2_0")
    dres0, dy0, sums0a, do = post_bwd(dx1, x0, y0, mods, 0, wout, lng_f[0:1])
    g_wout = wgrad(o, dy0, "wgrad_attn_out")
    dq, dkp, dvp, dsink, p_w1t_0, p_w2_0, wag = attn_bwd(
        q, kk, v, do, attn_sinks, rider=_Multi(_AllToAll([g_w1t_0, g_w2_0]), _Gather([gates_sum])))
    dk, dv, p_wout = kv_combine(dkp, dvp, rider=_AllToAll([g_wout]))
    dx0, dqkv, h0, sums0b = attn_in_bwd(dq, dk, dv, rope, x0, mods, 0, win_t, dres0)
    g_win_t = wgrad(dqkv, h0, "wgrad_attn_in")

    sums = [sums0a + sums0b, sums1, sums2a + sums2b, sums3]
    gmod = jnp.stack([t[0:3] for t in sums])
    gsend = jnp.moveaxis(gmod.reshape(4, N_DEV, CG), 1, 0)
    gsend = jnp.pad(gsend, ((0, 0), (0, 4), (0, 0)))
    c_t = c_all[:, 0, :].T
    sq_err = jnp.sum(sums3[5]).reshape(1, 1)
    tail = jnp.concatenate([
        _cols(dcw, 192), _cols(dcb, 192),
        _cols(jnp.concatenate([dba_f, dba_b]), 192), _cols(jnp.concatenate([dbx_f, dbx_b]), 192),
        _cols(jnp.concatenate([dlam_f, dlam_b]), 192),
        _cols(jnp.stack([t[3] for t in sums]), 128), _cols(jnp.stack([t[4] for t in sums]), 128),
        jnp.broadcast_to(dsink[:, 0:8], (N_DEV, 8)), jnp.broadcast_to(sq_err, (N_DEV, 1))], axis=1)
    tail = jnp.pad(tail, ((0, 0), (0, 32 * 128 - tail.shape[1]))).reshape(N_DEV, 32, 128)
    g_ada_w, g_ada_b, red, p_win_t = epilogue(gsend, c_t, tail, _AllToAll([g_win_t]))
    grads = {"ada_w": g_ada_w.reshape(ada_w.shape), "ada_b": g_ada_b[0:4].reshape(ada_b.shape)}

    big_parts = [p_win_t, p_wout, p_rin_t, p_rout, p_w1t_0, p_w2_0, p_w1t_1, p_w2_1]
    gsum = [part_sum(p, "part_sum_%d" % i) for i, p in enumerate(big_parts)]
    grads.update({
        "attn_w_in": gsum[0].T[None], "attn_w_out": gsum[1][None],
        "rnn_w_in": gsum[2].T[None], "rnn_w_out": gsum[3][None],
        "mlp_w1": jnp.stack([gsum[4].T, gsum[6].T]), "mlp_w2": jnp.stack([gsum[5], gsum[7]]),
    })
    wag = wag.reshape(N_DEV, 2 * nflat)
    grads["rnn_w_a"] = wag[:, :nflat].reshape(rnn_w_a.shape)
    grads["rnn_w_x"] = wag[:, nflat:].reshape(rnn_w_x.shape)
    tl = red.reshape(-1)
    loss = 0.5 * tl[3144] / D_MODEL
    grads["rnn_conv_w"] = tl[0:768].reshape(rnn_conv_w.shape)
    grads["rnn_conv_b"] = tl[768:960].reshape(rnn_conv_b.shape)
    grads["rnn_b_a"] = tl[960:1344].reshape(rnn_b_a.shape)
    grads["rnn_b_x"] = tl[1344:1728].reshape(rnn_b_x.shape)
    grads["rnn_lam"] = tl[1728:2112].reshape(rnn_lam.shape)
    grads["ln_g"] = tl[2112:2624].reshape(ln_g.shape)
    grads["ln_b"] = tl[2624:3136].reshape(ln_b.shape)
    grads["attn_sinks"] = tl[3136:3144].reshape(attn_sinks.shape)

    delta, new_m, new_v = {}, {}, {}
    for n in names:
        delta[n], new_m[n], new_v[n] = adamw(weights[n], grads[n], moments_m[n], moments_v[n], "adamw_" + n)
    return (loss, dx0.reshape(x.shape), *[grads[n] for n in names], *[delta[n] for n in names],
            *[new_m[n] for n in names], *[new_v[n] for n in names])
```

```python
import functools
import math

import jax
import jax.numpy as jnp
from jax import lax
from jax.experimental import pallas as pl
from jax.experimental.pallas import tpu as pltpu

F32, BF16 = jnp.float32, jnp.bfloat16
MESH = pl.DeviceIdType.MESH

D_MODEL = 1024
N_Q, N_KV, HEAD = 8, 2, 128
ROT, THETA = 32, 500000.0
QBLK = 128
D_QKV = (N_Q + 2 * N_KV) * HEAD
D_RNN, N_RB, RB_W = 1536, 16, 96
CG = 384
N_CG = D_RNN // CG
D_FF = 4096
FF_CHUNK = 1024
DEPTH = 2
ALPHA = (2.0 * DEPTH) ** 0.25
LN_EPS = 1e-5
LRU_C = 8.0
N_DEV = 8
LR, B1, B2, ADAM_EPS, WD, STEP = 0.001, 0.9, 0.999, 1e-8, 0.01, 10

VMEM_LIMIT = 56 * 1024 * 1024
TM_MM = 512
TM_MLP = 512
TT_RNN = 512
SB_RNN = 512
TK_WG = 2048


def _nn(a, b):
    return jnp.dot(a, b, preferred_element_type=F32)


def _nt(a, b):
    return lax.dot_general(a, b, (((1,), (1,)), ((), ())), preferred_element_type=F32)


def _tn(a, b):
    return lax.dot_general(a, b, (((0,), (0,)), ((), ())), preferred_element_type=F32)


def _blk(n, pref):
    t = min(n, pref)
    assert n % t == 0, (n, pref)
    return t


def _params(**kw):
    return pltpu.CompilerParams(vmem_limit_bytes=VMEM_LIMIT, **kw)


def _row(tm, w):
    return pl.BlockSpec((tm, w), lambda i: (i, 0))


def _res(shape):
    return pl.BlockSpec(shape, lambda i: (0,) * len(shape), pipeline_mode=pl.Buffered(1))


def _mod(mod_ref, k):
    return mod_ref[3 * k:3 * k + 1, :], mod_ref[3 * k + 1:3 * k + 2, :], mod_ref[3 * k + 2:3 * k + 3, :]


def _ln_stats(z):
    mu = jnp.mean(z, axis=-1, keepdims=True)
    zc = z - mu
    var = jnp.mean(zc * zc, axis=-1, keepdims=True)
    rstd = lax.rsqrt(var + LN_EPS)
    return zc * rstd, rstd


def _ln_bwd(dxo, xhat, rstd, g):
    dxh = dxo * g
    m1 = jnp.mean(dxh, axis=-1, keepdims=True)
    m2 = jnp.mean(dxh * xhat, axis=-1, keepdims=True)
    return rstd * (dxh - m1 - xhat * m2)


def _colsum(v):
    return jnp.sum(v, axis=0, keepdims=True)


def _sigmoid(v):
    return 0.5 * jnp.tanh(0.5 * v) + 0.5


def _gelu_parts(v):
    k = math.sqrt(2.0 / math.pi)
    u = k * (v + 0.044715 * v * v * v)
    t = jnp.tanh(u)
    g = 0.5 * v * (1.0 + t)
    dg = 0.5 * (1.0 + t) + 0.5 * v * (1.0 - t * t) * k * (1.0 + 3.0 * 0.044715 * v * v)
    return g, dg


def _me():
    return lax.axis_index("x"), lax.axis_index("y"), lax.axis_index("c")


def _idx(p):
    return 4 * p[0] + 2 * p[1] + p[2]


def _peers(me):
    x, y, c = me
    out = []
    for k in range(1, N_DEV):
        out.append((1 - x if k & 4 else x, 1 - y if k & 2 else y, 1 - c if k & 1 else c))
    return out


class _Gather:
    def __init__(self, srcs):
        self.srcs = list(srcs)
        n = len(self.srcs)
        self.out_shape = [jax.ShapeDtypeStruct((N_DEV,) + s.shape, s.dtype) for s in self.srcs]
        self.scratch = [pltpu.SemaphoreType.DMA((n, 7)), pltpu.SemaphoreType.DMA((n, 7)),
                        pltpu.SemaphoreType.DMA((n,))]

    @staticmethod
    def _places():
        x, y, c = me = _me()
        return me, (x, y, 1 - c), [(1 - x, y), (x, 1 - y), (1 - x, 1 - y)]

    @staticmethod
    def _copy(outs, sems, t, k, block, to, src=None):
        slot = outs[t].at[_idx(block)]
        return pltpu.make_async_remote_copy(
            src_ref=slot if src is None else src, dst_ref=slot, send_sem=sems[0].at[t, k],
            recv_sem=sems[1].at[t, k], device_id=to, device_id_type=MESH)

    def _firsts(self, ins, outs, sems):
        me, sibling, chips = self._places()
        out = []
        for t in range(len(ins)):
            out.append(self._copy(outs, sems, t, 0, me, sibling, src=ins[t]))
            out += [self._copy(outs, sems, t, 1 + j, me, (*chip, me[2]), src=ins[t]) for j, chip in enumerate(chips)]
        return out

    def _locals(self, ins, outs, sems):
        me = _me()
        return [pltpu.make_async_copy(ins[t], outs[t].at[_idx(me)], sems[2].at[t]) for t in range(len(ins))]

    def start(self, ins, outs, sems):
        for cp in self._locals(ins, outs, sems) + self._firsts(ins, outs, sems):
            cp.start()

    def mid(self, ins, outs, sems):
        me, sibling, chips = self._places()
        for j, chip in enumerate(chips):
            for t in range(len(ins)):
                self._copy(outs, sems, t, 1 + j, (*chip, me[2]), me).wait_recv()
                self._copy(outs, sems, t, 4 + j, (*chip, me[2]), sibling).start()

    def finish(self, ins, outs, sems):
        me, sibling, chips = self._places()
        for t in range(len(ins)):
            self._copy(outs, sems, t, 0, sibling, me).wait_recv()
            for j, chip in enumerate(chips):
                self._copy(outs, sems, t, 4 + j, (*chip, 1 - me[2]), me).wait_recv()
        for cp in self._firsts(ins, outs, sems):
            cp.wait_send()
        for j, chip in enumerate(chips):
            for t in range(len(ins)):
                self._copy(outs, sems, t, 4 + j, (*chip, me[2]), sibling).wait_send()
        for cp in self._locals(ins, outs, sems):
            cp.wait()


class _AllToAll:
    def __init__(self, srcs):
        self.srcs = list(srcs)
        n = len(self.srcs)
        self.out_shape = [jax.ShapeDtypeStruct(s.shape, s.dtype) for s in self.srcs]
        self.scratch = [pltpu.SemaphoreType.DMA((n, 7)), pltpu.SemaphoreType.DMA((n, 7)),
                        pltpu.SemaphoreType.DMA((n,))]

    def _copies(self, ins, outs, sems):
        me = _me()
        loc, rem = [], []
        for t in range(len(ins)):
            loc.append(pltpu.make_async_copy(ins[t].at[_idx(me)], outs[t].at[_idx(me)], sems[2].at[t]))
            for k, p in enumerate(_peers(me)):
                rem.append(pltpu.make_async_remote_copy(
                    src_ref=ins[t].at[_idx(p)], dst_ref=outs[t].at[_idx(me)], send_sem=sems[0].at[t, k],
                    recv_sem=sems[1].at[t, k], device_id=p, device_id_type=MESH))
        return loc, rem

    def start(self, ins, outs, sems):
        loc, rem = self._copies(ins, outs, sems)
        for cp in loc + rem:
            cp.start()

    def mid(self, ins, outs, sems):
        pass

    def finish(self, ins, outs, sems):
        me = _me()
        for t in range(len(ins)):
            for k, p in enumerate(_peers(me)):
                slot = outs[t].at[_idx(p)]
                pltpu.make_async_remote_copy(
                    src_ref=slot, dst_ref=slot, send_sem=sems[0].at[t, k], recv_sem=sems[1].at[t, k],
                    device_id=p, device_id_type=MESH).wait_recv()
        loc, rem = self._copies(ins, outs, sems)
        for cp in rem:
            cp.wait_send()
        for cp in loc:
            cp.wait()


class _Multi:
    def __init__(self, *exs):
        self.exs = exs
        self.srcs = [s for e in exs for s in e.srcs]
        self.out_shape = [s for e in exs for s in e.out_shape]
        self.scratch = [s for e in exs for s in e.scratch]

    def _each(self, ins, outs, sems):
        i = j = 0
        for e in self.exs:
            n, m = len(e.srcs), len(e.scratch)
            yield e, ins[i:i + n], outs[i:i + n], sems[j:j + m]
            i, j = i + n, j + m

    def start(self, ins, outs, sems):
        for e, a, b, c in self._each(ins, outs, sems):
            e.start(a, b, c)

    def mid(self, ins, outs, sems):
        for e, a, b, c in self._each(ins, outs, sems):
            e.mid(a, b, c)

    def finish(self, ins, outs, sems):
        for e, a, b, c in self._each(ins, outs, sems):
            e.finish(a, b, c)


def _call(body, *, name, grid, in_specs, out_specs, out_shape, args, scratch_shapes=(), rider=None):
    in_specs, out_specs, out_shape = list(in_specs), list(out_specs), list(out_shape)
    scratch_shapes = list(scratch_shapes)
    if rider is None:
        return pl.pallas_call(body, name=name, grid=grid, out_shape=out_shape, in_specs=in_specs,
                              out_specs=out_specs, scratch_shapes=scratch_shapes, compiler_params=_params())(*args)
    nci, nco, ncs, nr = len(in_specs), len(out_shape), len(scratch_shapes), len(rider.srcs)
    nsteps = math.prod(grid)
    mid = min((3 * nsteps) // 4, nsteps - 2)
    assert 0 < mid, (name, grid)

    def full(*refs):
        ci, ri = refs[:nci], refs[nci:nci + nr]
        co, ro = refs[nci + nr:nci + nr + nco], refs[nci + nr + nco:nci + 2 * nr + nco]
        cs, rs = refs[nci + 2 * nr + nco:nci + 2 * nr + nco + ncs], refs[nci + 2 * nr + nco + ncs:]
        step = pl.program_id(0)
        for d in range(1, len(grid)):
            step = step * grid[d] + pl.program_id(d)

        @pl.when(step == 0)
        def _():
            rider.start(ri, ro, rs)

        @pl.when(step == mid)
        def _():
            rider.mid(ri, ro, rs)

        body(*ci, *co, *cs)

        @pl.when(step == nsteps - 1)
        def _():
            rider.finish(ri, ro, rs)

    any_spec = pl.BlockSpec(memory_space=pl.ANY)
    return pl.pallas_call(
        full, name=name, grid=grid, out_shape=out_shape + rider.out_shape,
        in_specs=in_specs + [any_spec] * nr, out_specs=out_specs + [any_spec] * nr,
        scratch_shapes=scratch_shapes + rider.scratch, compiler_params=_params(),
    )(*args, *rider.srcs)


def _a2a_start(srcs, dsts, send_sems, recv_sems, local_sems, me, sem_base=0):
    peers = _peers(me)
    started = []
    for t in range(len(srcs)):
        loc = pltpu.make_async_copy(srcs[t].at[_idx(me)], dsts[t].at[_idx(me)], local_sems.at[sem_base + t])
        loc.start()
        started.append(("local", loc))
        for k, p in enumerate(peers):
            cp = pltpu.make_async_remote_copy(
                src_ref=srcs[t].at[_idx(p)], dst_ref=dsts[t].at[_idx(me)],
                send_sem=send_sems.at[sem_base + t, k], recv_sem=recv_sems.at[sem_base + t, k],
                device_id=p, device_id_type=MESH)
            cp.start()
            started.append(("remote", cp))
    return started


def _a2a_finish(started, dsts, send_sems, recv_sems, me, sem_base=0):
    peers = _peers(me)
    for t in range(len(dsts)):
        for k, p in enumerate(peers):
            slot = dsts[t].at[_idx(p)]
            pltpu.make_async_remote_copy(
                src_ref=slot, dst_ref=slot, send_sem=send_sems.at[sem_base + t, k],
                recv_sem=recv_sems.at[sem_base + t, k], device_id=p, device_id_type=MESH).wait_recv()
    for kind, cp in started:
        if kind == "local":
            cp.wait()
        else:
            cp.wait_send()


def ada_modulation(c8, ada_w, ada_b, ride):
    nr = len(ride.srcs)

    def body(c_ref, w_ref, b_ref, *rest):
        ride_in, (call_ref, modr_ref), ride_out = rest[:nr], rest[nr:nr + 2], rest[nr + 2:2 * nr + 2]
        modp, send_sems, recv_sems, local_sems = rest[2 * nr + 2:2 * nr + 6]
        ride_sems = rest[2 * nr + 6:]
        ride.start(ride_in, ride_out, ride_sems)
        me = _me()
        peers = _peers(me)
        sends = []
        for k, p in enumerate(peers):
            cp = pltpu.make_async_remote_copy(
                src_ref=c_ref, dst_ref=call_ref.at[_idx(me)], send_sem=send_sems.at[0, k],
                recv_sem=recv_sems.at[0, k], device_id=p, device_id_type=MESH)
            cp.start()
            sends.append(cp)
        call_ref[_idx(me)] = c_ref[...]
        for k, p in enumerate(peers):
            slot = call_ref.at[_idx(p)]
            pltpu.make_async_remote_copy(
                src_ref=slot, dst_ref=slot, send_sem=send_sems.at[0, k], recv_sem=recv_sems.at[0, k],
                device_id=p, device_id_type=MESH).wait_recv()
        for cp in sends:
            cp.wait_send()
        cv = call_ref[...].reshape(N_DEV * 8, D_MODEL)
        s = (cv * _sigmoid(cv)).astype(BF16)
        for k in range(4):
            res = _nn(s, w_ref[k].astype(BF16)) + b_ref[k]
            for j in range(N_DEV):
                modp[j, 8 * k:8 * k + 8, :] = res[8 * j:8 * j + 8, :]
        started = _a2a_start([modp], [modr_ref], send_sems, recv_sems, local_sems, me, sem_base=1)
        _a2a_finish(started, [modr_ref], send_sems, recv_sems, me, sem_base=1)
        ride.mid(ride_in, ride_out, ride_sems)
        ride.finish(ride_in, ride_out, ride_sems)

    vm, hbm = pl.BlockSpec(memory_space=pltpu.VMEM), pl.BlockSpec(memory_space=pl.ANY)
    return pl.pallas_call(
        body, name="ada_modulation",
        out_shape=[jax.ShapeDtypeStruct((N_DEV, 8, D_MODEL), F32), jax.ShapeDtypeStruct((N_DEV, 32, CG), F32)]
        + ride.out_shape,
        in_specs=[vm, vm, vm] + [hbm] * nr, out_specs=[vm, vm] + [hbm] * nr,
        scratch_shapes=[pltpu.VMEM((N_DEV, 32, CG), F32), pltpu.SemaphoreType.DMA((2, 7)),
                        pltpu.SemaphoreType.DMA((2, 7)), pltpu.SemaphoreType.DMA((2,))] + ride.scratch,
        compiler_params=_params(),
    )(c8, ada_w, ada_b, *ride.srcs)


def epilogue(gsend, c_t, tail, ride):
    nr = len(ride.srcs)
    rt = tail.shape[1]

    def body(g_ref, ct_ref, t_ref, *rest):
        ride_in, (gw_ref, gb_ref, red_ref), ride_out = rest[:nr], rest[nr:nr + 3], rest[nr + 3:2 * nr + 3]
        grecv, trecv, send_sems, recv_sems, local_sems = rest[2 * nr + 3:2 * nr + 8]
        ride_sems = rest[2 * nr + 8:]
        ride.start(ride_in, ride_out, ride_sems)
        me = _me()
        started = _a2a_start([g_ref, t_ref], [grecv, trecv], send_sems, recv_sems, local_sems, me)
        _a2a_finish(started, [grecv, trecv], send_sems, recv_sems, me)
        acc = trecv[0]
        for j in range(1, N_DEV):
            acc = acc + trecv[j]
        red_ref[...] = acc
        ct = ct_ref[...]
        st = (ct * _sigmoid(ct)).astype(BF16).astype(F32)
        gb = jnp.zeros((8, CG), F32)
        for b in range(N_DEV):
            gb = gb + grecv[b]
        gb_ref[...] = gb
        for k in range(4):
            acc = jnp.zeros((D_MODEL, CG), F32)
            for b in range(N_DEV):
                row = grecv[b, k:k + 1, :].astype(BF16).astype(F32)
                acc = acc + st[:, b:b + 1] * row
            gw_ref[k] = acc
        ride.mid(ride_in, ride_out, ride_sems)
        ride.finish(ride_in, ride_out, ride_sems)

    vm, hbm = pl.BlockSpec(memory_space=pltpu.VMEM), pl.BlockSpec(memory_space=pl.ANY)
    return pl.pallas_call(
        body, name="epilogue",
        out_shape=[jax.ShapeDtypeStruct((4, D_MODEL, CG), F32), jax.ShapeDtypeStruct((8, CG), F32),
                   jax.ShapeDtypeStruct((rt, 128), F32)] + ride.out_shape,
        in_specs=[vm, vm, vm] + [hbm] * nr, out_specs=[vm, vm, vm] + [hbm] * nr,
        scratch_shapes=[pltpu.VMEM((N_DEV, 8, CG), F32), pltpu.VMEM((N_DEV, rt, 128), F32),
                        pltpu.SemaphoreType.DMA((2, 7)), pltpu.SemaphoreType.DMA((2, 7)),
                        pltpu.SemaphoreType.DMA((2,))] + ride.scratch,
        compiler_params=_params(),
    )(gsend, c_t, tail, *ride.srcs)


def _rope(t, cos, s1, s2):
    return t * cos + pltpu.roll(t, 16, 1) * s1 + pltpu.roll(t, HEAD - 16, 1) * s2


def _rope_bwd(d, cos, s1, s2):
    return d * cos + pltpu.roll(d * s1, HEAD - 16, 1) + pltpu.roll(d * s2, 16, 1)


def attn_in_fwd(x, mods, k, win_t, rope, rider=None):
    s = x.shape[0]
    tm = _blk(s, TM_MM)

    def body(x_ref, mod_ref, w_ref, c_ref, s1_ref, s2_ref, q_ref, k_ref, v_ref):
        shift, scale, _ = _mod(mod_ref, k)
        h = (x_ref[...] * (1.0 + scale) + shift).astype(BF16)
        qkv = _nt(h, w_ref[...])
        cos, s1, s2 = c_ref[...], s1_ref[...], s2_ref[...]
        for hh in range(N_Q + N_KV):
            r = _rope(qkv[:, HEAD * hh:HEAD * (hh + 1)], cos, s1, s2).astype(BF16)
            if hh < N_Q:
                q_ref[:, HEAD * hh:HEAD * (hh + 1)] = r
            else:
                k_ref[:, HEAD * (hh - N_Q):HEAD * (hh - N_Q + 1)] = r
        v_ref[...] = qkv[:, HEAD * (N_Q + N_KV):].astype(BF16)

    return _call(
        body, name="attn_in_fwd", grid=(s // tm,),
        out_shape=[jax.ShapeDtypeStruct((s, N_Q * HEAD), BF16), jax.ShapeDtypeStruct((s, N_KV * HEAD), BF16),
                   jax.ShapeDtypeStruct((s, N_KV * HEAD), BF16)],
        in_specs=[_row(tm, D_MODEL), _res(mods.shape), _res(win_t.shape),
                  _row(tm, HEAD), _row(tm, HEAD), _row(tm, HEAD)],
        out_specs=[_row(tm, N_Q * HEAD), _row(tm, N_KV * HEAD), _row(tm, N_KV * HEAD)],
        args=(x, mods, win_t, *rope), rider=rider)


def _kv_specs(nblk):
    w = N_KV * HEAD
    return [pl.BlockSpec((QBLK, w), lambda n: (jnp.maximum(n - 1, 0), 0)),
            pl.BlockSpec((QBLK, w), lambda n: (n, 0)),
            pl.BlockSpec((QBLK, w), lambda n: (jnp.minimum(n + 1, nblk - 1), 0))]


GROUP = N_Q // N_KV


def _attn_mask(n, s):
    qi = lax.broadcasted_iota(jnp.int32, (GROUP * QBLK, 3 * QBLK), 0) & (QBLK - 1)
    kj = lax.broadcasted_iota(jnp.int32, (GROUP * QBLK, 3 * QBLK), 1)
    rel = kj - QBLK - qi
    kpos = kj + (n - 1) * QBLK
    return (jnp.abs(rel) <= QBLK) & (kpos >= 0) & (kpos < s)


def _stack_heads(ref, kv):
    return jnp.concatenate([ref[:, HEAD * (GROUP * kv + j):HEAD * (GROUP * kv + j + 1)] for j in range(GROUP)], axis=0)


def _stack_sinks(sink_ref, kv):
    row = lax.broadcasted_iota(jnp.int32, (GROUP * QBLK, 1), 0)
    out = jnp.full((GROUP * QBLK, 1), sink_ref[0, GROUP * kv + GROUP - 1], F32)
    for j in range(GROUP - 2, -1, -1):
        out = jnp.where(row < QBLK * (j + 1), sink_ref[0, GROUP * kv + j], out)
    return out


def _attn_probs(qh, kh, valid, sink):
    sc = _nt(qh, kh) * (HEAD ** -0.5)
    sc = jnp.where(valid, sc, -1e30)
    m = jnp.maximum(jnp.max(sc, axis=-1, keepdims=True), sink)
    p = jnp.exp(sc - m)
    es = jnp.exp(sink - m)
    denom = jnp.sum(p, axis=-1, keepdims=True) + es
    return p / denom, es / denom


def attn_fwd(q, kk, v, sinks, rider=None):
    s = q.shape[0]
    nblk = s // QBLK

    def body(sink_ref, q_ref, kp, ko, kn, vp, vo, vn, o_ref):
        n = pl.program_id(0)
        kcat = jnp.concatenate([kp[...], ko[...], kn[...]], axis=0)
        vcat = jnp.concatenate([vp[...], vo[...], vn[...]], axis=0)
        valid = _attn_mask(n, s)
        for kv in range(N_KV):
            cols = slice(HEAD * kv, HEAD * (kv + 1))
            probs, _ = _attn_probs(_stack_heads(q_ref, kv), kcat[:, cols], valid, _stack_sinks(sink_ref, kv))
            og = _nn(probs.astype(BF16), vcat[:, cols]).astype(BF16)
            for j in range(GROUP):
                hq = GROUP * kv + j
                o_ref[:, HEAD * hq:HEAD * (hq + 1)] = og[QBLK * j:QBLK * (j + 1), :]

    return _call(
        body, name="attn_fwd", grid=(nblk,),
        out_shape=[jax.ShapeDtypeStruct((s, N_Q * HEAD), BF16)],
        in_specs=[pl.BlockSpec(memory_space=pltpu.SMEM), pl.BlockSpec((QBLK, N_Q * HEAD), lambda n: (n, 0))]
        + _kv_specs(nblk) + _kv_specs(nblk),
        out_specs=[pl.BlockSpec((QBLK, N_Q * HEAD), lambda n: (n, 0))],
        args=(sinks, q, kk, kk, kk, v, v, v), rider=rider)


def post_fwd(ypre, w, x, mods, k, lng, lnb, gate_act=None):
    s = x.shape[0]
    tm = _blk(s, TM_MM)
    kdim = w.shape[0]
    rnn = gate_act is not None

    def body(*refs):
        if rnn:
            gt_ref, hf_ref, hb_ref, w_ref, x_ref, mod_ref, g_ref, b_ref, xo_ref, y_ref, yp_ref = refs
            act, _ = _gelu_parts(gt_ref[...])
            yp = ((hf_ref[...] + hb_ref[...]) * act).astype(BF16)
            yp_ref[...] = yp
        else:
            yp_ref, w_ref, x_ref, mod_ref, g_ref, b_ref, xo_ref, y_ref = refs
            yp = yp_ref[...]
        _, _, gate = _mod(mod_ref, k)
        y = _nn(yp, w_ref[...])
        y_ref[...] = y
        xhat, _ = _ln_stats(ALPHA * x_ref[...] + (1.0 + gate) * y)
        xo_ref[...] = xhat * g_ref[...] + b_ref[...]

    act_in = list(gate_act) if rnn else [ypre]
    out_shape = [jax.ShapeDtypeStruct((s, D_MODEL), F32), jax.ShapeDtypeStruct((s, D_MODEL), F32)]
    out_specs = [_row(tm, D_MODEL), _row(tm, D_MODEL)]
    if rnn:
        out_shape.append(jax.ShapeDtypeStruct((s, kdim), BF16))
        out_specs.append(_row(tm, kdim))
    return pl.pallas_call(
        body, name="rnn_post_fwd" if rnn else "attn_post_fwd", grid=(s // tm,),
        out_shape=out_shape,
        in_specs=[_row(tm, kdim)] * len(act_in) + [_res(w.shape), _row(tm, D_MODEL), _res(mods.shape),
                                                    _res(lng.shape), _res(lnb.shape)],
        out_specs=out_specs,
        compiler_params=_params(),
    )(*act_in, w, x, mods, lng, lnb)


def _mlp_specs(tm):
    rows = pl.BlockSpec((tm, D_MODEL), lambda i, c: (i, 0))
    chunk = pl.BlockSpec((tm, FF_CHUNK), lambda i, c: (i, c))
    wchunk = pl.BlockSpec((FF_CHUNK, D_MODEL), lambda i, c: (c, 0))
    res = lambda shape: pl.BlockSpec(shape, lambda i, c: (0,) * len(shape), pipeline_mode=pl.Buffered(1))
    return rows, chunk, wchunk, res


def mlp_fwd(x, mods, k, w1_t, w2, lng, lnb, rider=None, last=False):
    s = x.shape[0]
    tm = _blk(s, TM_MLP)
    nc = D_FF // FF_CHUNK

    def body(x_ref, mod_ref, w1_ref, w2_ref, g_ref, b_ref, *rest):
        outs, (h_sc, y_sc) = rest[:-2], rest[-2:]
        xo_ref = None if last else outs[0]
        y_ref, ra_ref, r_ref = outs[-3:]
        c = pl.program_id(1)
        shift, scale, gate = _mod(mod_ref, k)

        @pl.when(c == 0)
        def _():
            h_sc[...] = (x_ref[...] * (1.0 + scale) + shift).astype(BF16)
            y_sc[...] = jnp.zeros(y_sc.shape, F32)

        a = jnp.maximum(_nt(h_sc[...], w1_ref[...]), 0.0)
        r = (a * a).astype(BF16)
        ra_ref[...] = a.astype(BF16)
        r_ref[...] = r
        y_sc[...] += _nn(r, w2_ref[...])

        @pl.when(c == nc - 1)
        def _():
            y = y_sc[...]
            y_ref[...] = y
            if not last:
                xhat, _ = _ln_stats(ALPHA * x_ref[...] + (1.0 + gate) * y)
                xo_ref[...] = xhat * g_ref[...] + b_ref[...]

    nf = 1 if last else 2
    rows, chunk, wchunk, res = _mlp_specs(tm)
    return _call(
        body, name="mlp_fwd_last" if last else "mlp_fwd", grid=(s // tm, nc),
        out_shape=[jax.ShapeDtypeStruct((s, D_MODEL), F32)] * nf + [jax.ShapeDtypeStruct((s, D_FF), BF16)] * 2,
        in_specs=[rows, res(mods.shape), wchunk, wchunk, res(lng.shape), res(lnb.shape)],
        out_specs=[rows] * nf + [chunk] * 2,
        scratch_shapes=[pltpu.VMEM((tm, D_MODEL), BF16), pltpu.VMEM((tm, D_MODEL), F32)],
        args=(x, mods, w1_t, w2, lng, lnb), rider=rider)


def rnn_in_fwd(x, mods, k, win_t):
    s = x.shape[0]
    tm = _blk(s, TM_MM)

    def body(x_ref, mod_ref, w_ref, xr_ref, gt_ref):
        shift, scale, _ = _mod(mod_ref, k)
        h = (x_ref[...] * (1.0 + scale) + shift).astype(BF16)
        xr_ref[...] = _nt(h, w_ref[0:D_RNN, :])
        gt_ref[...] = _nt(h, w_ref[D_RNN:2 * D_RNN, :])

    return pl.pallas_call(
        body, name="rnn_in_fwd", grid=(s // tm,),
        out_shape=[jax.ShapeDtypeStruct((s, D_RNN), F32)] * 2,
        in_specs=[_row(tm, D_MODEL), _res(mods.shape), _res(win_t.shape)],
        out_specs=[_row(tm, D_RNN)] * 2,
        compiler_params=_params(),
    )(x, mods, win_t)


def _shift_rows(v, k, row):
    n = v.shape[0]
    r = pltpu.roll(v, k % n, 0)
    keep = (row >= k) if k > 0 else (row < n + k)
    return jnp.where(keep, r, 0.0)


def conv_fwd(xr, cw, cb):
    s = xr.shape[0]

    def body(x_ref, w_ref, b_ref, o_ref):
        xv = x_ref[...]
        row = lax.broadcasted_iota(jnp.int32, xv.shape, 0)
        o_ref[...] = (b_ref[...] + w_ref[0:1, :] * _shift_rows(xv, 2, row) + w_ref[1:2, :] * _shift_rows(xv, 1, row)
                      + w_ref[2:3, :] * xv + w_ref[3:4, :] * _shift_rows(xv, -1, row))

    slab = pl.BlockSpec((s, 128), lambda j: (0, j))
    return pl.pallas_call(
        body, name="conv_fwd", grid=(D_RNN // 128,),
        out_shape=jax.ShapeDtypeStruct((s, D_RNN), F32),
        in_specs=[slab, pl.BlockSpec((4, 128), lambda j: (0, j)), pl.BlockSpec((1, 128), lambda j: (0, j))],
        out_specs=slab,
        compiler_params=_params(),
    )(xr, cw, cb)


def conv_bwd(da, db, xr, cw):
    s = xr.shape[0]

    def body(da_ref, db_ref, x_ref, w_ref, dx_ref, dw_ref, dbias_ref):
        d = da_ref[...] + db_ref[...]
        xv = x_ref[...]
        row = lax.broadcasted_iota(jnp.int32, xv.shape, 0)
        dx_ref[...] = (w_ref[0:1, :] * _shift_rows(d, -2, row) + w_ref[1:2, :] * _shift_rows(d, -1, row)
                       + w_ref[2:3, :] * d + w_ref[3:4, :] * _shift_rows(d, 1, row))
        dw_ref[0:1, :] = _colsum(d * _shift_rows(xv, 2, row))
        dw_ref[1:2, :] = _colsum(d * _shift_rows(xv, 1, row))
        dw_ref[2:3, :] = _colsum(d * xv)
        dw_ref[3:4, :] = _colsum(d * _shift_rows(xv, -1, row))
        dbias_ref[...] = _colsum(d)

    slab = pl.BlockSpec((s, 128), lambda j: (0, j))
    return pl.pallas_call(
        body, name="conv_bwd", grid=(D_RNN // 128,),
        out_shape=[jax.ShapeDtypeStruct((s, D_RNN), F32), jax.ShapeDtypeStruct((4, D_RNN), F32),
                   jax.ShapeDtypeStruct((1, D_RNN), F32)],
        in_specs=[slab, slab, slab, pl.BlockSpec((4, 128), lambda j: (0, j))],
        out_specs=[slab, pl.BlockSpec((4, 128), lambda j: (0, j)), pl.BlockSpec((1, 128), lambda j: (0, j))],
        compiler_params=_params(),
    )(da, db, xr, cw)


def _softplus_neg(lam):
    z = -lam
    e = jnp.exp(-jnp.abs(z))
    u = 1.0 + e
    log1p = jnp.where(u == 1.0, e, jnp.log(u) * e / jnp.where(u == 1.0, 1.0, u - 1.0))
    return jnp.maximum(z, 0.0) + log1p, 1.0 / (1.0 + jnp.exp(lam))


def _lru_gates(xv, wa_ref, wx_ref, ba_ref, bx_ref, lam_ref):
    xb = xv.astype(BF16)
    r = _sigmoid(_nn(xb, wa_ref[...]) + ba_ref[...])
    i = _sigmoid(_nn(xb, wx_ref[...]) + bx_ref[...])
    sp, sg = _softplus_neg(lam_ref[...])
    la = r * (-LRU_C * sp)
    a = jnp.exp(la)
    th = jnp.tanh(la)
    m2 = -2.0 * th / (1.0 - th)
    rmult = lax.rsqrt(jnp.maximum(m2, 1e-37))
    return xb, r, i, sp, sg, a, m2 * rmult, rmult


def _scan(a, u, h0, reverse):
    n, c = a.shape
    sub = lax.broadcasted_iota(jnp.int32, (8, c), 0)
    steps = [(8 - sh, sub < 8 - sh) if reverse else (sh, sub >= sh) for sh in (1, 2, 4)]
    out = [None] * (n // 8)
    edge = h0
    for k in (range(n // 8 - 1, -1, -1) if reverse else range(n // 8)):
        at, ut = a[8 * k:8 * k + 8], u[8 * k:8 * k + 8]
        for rot, keep in steps:
            a_s = jnp.where(keep, pltpu.roll(at, rot, 0), 1.0)
            u_s = jnp.where(keep, pltpu.roll(ut, rot, 0), 0.0)
            ut = at * u_s + ut
            at = at * a_s
        hk = ut + at * edge
        out[k] = hk
        edge = hk[0:1] if reverse else hk[7:8]
    return jnp.concatenate(out, axis=0)


def _lru_specs(nt, tt, reverse):
    tmap = (lambda t: nt - 1 - t) if reverse else (lambda t: t)
    blk = pl.BlockSpec((tt, CG), lambda g, t: (tmap(t), g))
    wsp = pl.BlockSpec((None, CG, CG), lambda g, t: (g, 0, 0))
    vec = pl.BlockSpec((1, CG), lambda g, t: (0, g))
    return tmap, blk, wsp, vec


def lru_fwd(xc, wa, wx, ba, bx, lam, reverse):
    s = xc.shape[0]
    tt = _blk(s, TT_RNN)
    sb = _blk(tt, SB_RNN)
    nt = s // tt

    def body(x_ref, wa_ref, wx_ref, ba_ref, bx_ref, lam_ref, hs_ref, carry):
        @pl.when(pl.program_id(1) == 0)
        def _():
            carry[...] = jnp.zeros(carry.shape, F32)

        xv = x_ref[...]
        _, _, i, _, _, a, mult, _ = _lru_gates(xv, wa_ref, wx_ref, ba_ref, bx_ref, lam_ref)
        u = mult * (i * xv)
        h0 = carry[0:1, :]
        order = range(tt // sb - 1, -1, -1) if reverse else range(tt // sb)
        for j in order:
            rows = slice(sb * j, sb * (j + 1))
            h = _scan(a[rows], u[rows], h0, reverse)
            hs_ref[rows, :] = h
            h0 = h[0:1, :] if reverse else h[sb - 1:sb, :]
        carry[0:1, :] = h0

    _, blk, wsp, vec = _lru_specs(nt, tt, reverse)
    return pl.pallas_call(
        body, name="lru_fwd_rev" if reverse else "lru_fwd", grid=(N_CG, nt),
        out_shape=jax.ShapeDtypeStruct((s, D_RNN), F32),
        in_specs=[blk, wsp, wsp, vec, vec, vec], out_specs=blk,
        scratch_shapes=[pltpu.VMEM((8, CG), F32)],
        compiler_params=_params(),
    )(xc, wa, wx, ba, bx, lam)


def lru_bwd(xc, dhs, hs, wa, wx, ba, bx, lam, reverse, rider=None):
    s = xc.shape[0]
    tt = _blk(s, TT_RNN)
    sb = _blk(tt, SB_RNN)
    nt = s // tt
    back = not reverse

    def body(x_ref, dh_ref, hs_ref, nb_ref, wa_ref, wx_ref, ba_ref, bx_ref, lam_ref,
             dx_ref, dwa_ref, dwx_ref, dba_ref, dbx_ref, dlam_ref, carry):
        t = pl.program_id(1)

        @pl.when(t == 0)
        def _():
            carry[...] = jnp.zeros(carry.shape, F32)
            dwa_ref[...] = jnp.zeros(dwa_ref.shape, F32)
            dwx_ref[...] = jnp.zeros(dwx_ref.shape, F32)
            dba_ref[...] = jnp.zeros(dba_ref.shape, F32)
            dbx_ref[...] = jnp.zeros(dbx_ref.shape, F32)
            dlam_ref[...] = jnp.zeros(dlam_ref.shape, F32)

        xv = x_ref[...]
        xb, r, i, sp, sg, a, mult, rmult = _lru_gates(xv, wa_ref, wx_ref, ba_ref, bx_ref, lam_ref)
        row = lax.broadcasted_iota(jnp.int32, xv.shape, 0)
        hsv = hs_ref[...]
        inner = t < nt - 1
        if reverse:
            edge = jnp.where(inner, nb_ref[0:1, :], 0.0)
            hprev = jnp.where(row == tt - 1, edge, pltpu.roll(hsv, tt - 1, 0))
            a_next = jnp.where(row == 0, carry[1:2, :], pltpu.roll(a, 1, 0))
        else:
            edge = jnp.where(inner, nb_ref[7:8, :], 0.0)
            hprev = jnp.where(row == 0, edge, pltpu.roll(hsv, 1, 0))
            a_next = jnp.where(row == tt - 1, carry[1:2, :], pltpu.roll(a, tt - 1, 0))
        dhv = dh_ref[...]
        g0 = carry[0:1, :]
        parts = [None] * (tt // sb)
        order = range(tt // sb - 1, -1, -1) if back else range(tt // sb)
        for j in order:
            rows = slice(sb * j, sb * (j + 1))
            gj = _scan(a_next[rows], dhv[rows], g0, back)
            parts[j] = gj
            g0 = gj[0:1, :] if back else gj[sb - 1:sb, :]
        g = jnp.concatenate(parts, axis=0) if len(parts) > 1 else parts[0]
        carry[0:1, :] = g0
        carry[1:2, :] = a[0:1, :] if back else a[tt - 1:tt, :]

        da = g * hprev
        dmult = g * (i * xv)
        di = g * mult * xv
        dla = da * a - dmult * (a * a) * rmult
        dpa = (dla * (-LRU_C * sp)) * r * (1.0 - r)
        dpx = di * i * (1.0 - i)
        dlam_ref[...] += _colsum(dla * (LRU_C * r * sg))
        dba_ref[...] += _colsum(dpa)
        dbx_ref[...] += _colsum(dpx)
        dpab, dpxb = dpa.astype(BF16), dpx.astype(BF16)
        dx_ref[...] = g * mult * i + _nt(dpab, wa_ref[...]) + _nt(dpxb, wx_ref[...])
        dwa_ref[...] += _tn(xb, dpab)
        dwx_ref[...] += _tn(xb, dpxb)

    tmap, blk, wsp, vec = _lru_specs(nt, tt, back)
    per8 = tt // 8
    if reverse:
        nb = pl.BlockSpec((8, CG), lambda g, t: (jnp.minimum((tmap(t) + 1) * per8, s // 8 - 1), g))
    else:
        nb = pl.BlockSpec((8, CG), lambda g, t: (jnp.maximum(tmap(t) * per8 - 1, 0), g))
    return _call(
        body, name="lru_bwd_rev" if reverse else "lru_bwd", grid=(N_CG, nt),
        out_shape=[jax.ShapeDtypeStruct((s, D_RNN), F32), jax.ShapeDtypeStruct((N_CG, CG, CG), F32),
                   jax.ShapeDtypeStruct((N_CG, CG, CG), F32)] + [jax.ShapeDtypeStruct((1, D_RNN), F32)] * 3,
        in_specs=[blk, blk, blk, nb, wsp, wsp, vec, vec, vec],
        out_specs=[blk, wsp, wsp, vec, vec, vec],
        scratch_shapes=[pltpu.VMEM((8, CG), F32)],
        args=(xc, dhs, hs, hs, wa, wx, ba, bx, lam), rider=rider)


def _ln_part_bwd(dxo, x, y, gate, g, sums_ref, loss_head=None):
    xhat, rstd = _ln_stats(ALPHA * x + (1.0 + gate) * y)
    if loss_head is not None:
        err = xhat * g + loss_head[0] - loss_head[1]
        dxo = err * (1.0 / D_MODEL)
        sums_ref[5:6, :] += _colsum(err * err)
    dz = _ln_bwd(dxo, xhat, rstd, g)
    sums_ref[2:3, :] += _colsum(dz * y)
    sums_ref[3:4, :] += _colsum(dxo * xhat)
    sums_ref[4:5, :] += _colsum(dxo)
    return dz


def mlp_bwd(dxo, x, y, ra, mods, k, w1_t, w2, lng, lnb=None, rider=None):
    s = x.shape[0]
    tm = _blk(s, TM_MLP)
    nc = D_FF // FF_CHUNK
    head = lnb is not None

    def body(d_ref, x_ref, y_ref, ra_ref, mod_ref, w1_ref, w2_ref, g_ref, *rest):
        b_ref = rest[0] if head else None
        dx_ref, da_ref, h_ref, dy_ref, sums_ref, dh_sc, dz_sc = rest[1:] if head else rest
        c = pl.program_id(1)

        @pl.when((pl.program_id(0) == 0) & (c == 0))
        def _():
            sums_ref[...] = jnp.zeros(sums_ref.shape, F32)

        shift, scale, gate = _mod(mod_ref, k)

        @pl.when(c == 0)
        def _():
            xv = x_ref[...]
            if head:
                dz = _ln_part_bwd(None, xv, y_ref[...], gate, g_ref[...], sums_ref, (b_ref[...], d_ref[...]))
            else:
                dz = _ln_part_bwd(d_ref[...], xv, y_ref[...], gate, g_ref[...], sums_ref)
            dz_sc[...] = dz
            dy_ref[...] = (dz * (1.0 + gate)).astype(BF16)
            h_ref[...] = (xv * (1.0 + scale) + shift).astype(BF16)
            dh_sc[...] = jnp.zeros(dh_sc.shape, F32)

        da = (_nt(dy_ref[...], w2_ref[...]) * (2.0 * ra_ref[...].astype(F32))).astype(BF16)
        da_ref[...] = da
        dh_sc[...] += _nn(da, w1_ref[...])

        @pl.when(c == nc - 1)
        def _():
            dh, xv = dh_sc[...], x_ref[...]
            dx_ref[...] = ALPHA * dz_sc[...] + dh * (1.0 + scale)
            sums_ref[0:1, :] += _colsum(dh)
            sums_ref[1:2, :] += _colsum(dh * xv)

    rows, chunk, wchunk, res = _mlp_specs(tm)
    return _call(
        body, name="mlp_bwd", grid=(s // tm, nc),
        out_shape=[jax.ShapeDtypeStruct((s, D_MODEL), F32), jax.ShapeDtypeStruct((s, D_FF), BF16),
                   jax.ShapeDtypeStruct((s, D_MODEL), BF16),
                   jax.ShapeDtypeStruct((s, D_MODEL), BF16), jax.ShapeDtypeStruct((8, D_MODEL), F32)],
        in_specs=[rows] * 3 + [chunk, res(mods.shape), wchunk, wchunk, res(lng.shape)]
        + ([res(lnb.shape)] if head else []),
        out_specs=[rows, chunk, rows, rows, res((8, D_MODEL))],
        scratch_shapes=[pltpu.VMEM((tm, D_MODEL), F32), pltpu.VMEM((tm, D_MODEL), F32)],
        args=(dxo, x, y, ra, mods, w1_t, w2, lng) + ((lnb,) if head else ()), rider=rider)


def post_bwd(dxo, x, y, mods, k, w, lng, gate_act=None, rider=None):
    s = x.shape[0]
    tm = _blk(s, TM_MM)
    kdim = w.shape[0]
    rnn = gate_act is not None

    def body(*refs):
        if rnn:
            (d_ref, x_ref, y_ref, mod_ref, w_ref, g_ref, gt_ref, hf_ref, hb_ref,
             dres_ref, dy_ref, sums_ref, dhs_ref, dgt_ref) = refs
        else:
            d_ref, x_ref, y_ref, mod_ref, w_ref, g_ref, dres_ref, dy_ref, sums_ref, dyp_ref = refs

        @pl.when(pl.program_id(0) == 0)
        def _():
            sums_ref[...] = jnp.zeros(sums_ref.shape, F32)

        _, _, gate = _mod(mod_ref, k)
        dz = _ln_part_bwd(d_ref[...], x_ref[...], y_ref[...], gate, g_ref[...], sums_ref)
        dres_ref[...] = ALPHA * dz
        dyb = (dz * (1.0 + gate)).astype(BF16)
        dy_ref[...] = dyb
        dyp = _nt(dyb, w_ref[...])
        if rnn:
            act, dact = _gelu_parts(gt_ref[...])
            dhs_ref[...] = dyp * act
            dgt_ref[...] = dyp * (hf_ref[...] + hb_ref[...]) * dact
        else:
            dyp_ref[...] = dyp.astype(BF16)

    ins = [dxo, x, y, mods, w, lng] + (list(gate_act) if rnn else [])
    in_specs = [_row(tm, D_MODEL)] * 3 + [_res(mods.shape), _res(w.shape), _res(lng.shape)]
    out_shape = [jax.ShapeDtypeStruct((s, D_MODEL), F32), jax.ShapeDtypeStruct((s, D_MODEL), BF16),
                 jax.ShapeDtypeStruct((8, D_MODEL), F32)]
    out_specs = [_row(tm, D_MODEL), _row(tm, D_MODEL), _res((8, D_MODEL))]
    if rnn:
        in_specs += [_row(tm, kdim)] * 3
        out_shape += [jax.ShapeDtypeStruct((s, kdim), F32)] * 2
        out_specs += [_row(tm, kdim)] * 2
    else:
        out_shape.append(jax.ShapeDtypeStruct((s, kdim), BF16))
        out_specs.append(_row(tm, kdim))
    return _call(
        body, name="rnn_post_bwd" if rnn else "attn_post_bwd", grid=(s // tm,),
        out_shape=out_shape, in_specs=in_specs, out_specs=out_specs, args=ins, rider=rider)


def attn_bwd(q, kk, v, do, sinks, rider=None):
    s = q.shape[0]
    nblk = s // QBLK
    scale = HEAD ** -0.5

    def body(sink_ref, q_ref, do_ref, kp, ko, kn, vp, vo, vn, dq_ref, dkp_ref, dvp_ref, ds_ref):
        n = pl.program_id(0)

        @pl.when(n == 0)
        def _():
            ds_ref[...] = jnp.zeros(ds_ref.shape, F32)

        kcat = jnp.concatenate([kp[...], ko[...], kn[...]], axis=0)
        vcat = jnp.concatenate([vp[...], vo[...], vn[...]], axis=0)
        valid = _attn_mask(n, s)
        lane = lax.broadcasted_iota(jnp.int32, (1, 128), 1)
        dsink = jnp.zeros((1, 128), F32)
        for kv in range(N_KV):
            cols = slice(HEAD * kv, HEAD * (kv + 1))
            qg, dog = _stack_heads(q_ref, kv), _stack_heads(do_ref, kv)
            kh, vh = kcat[:, cols], vcat[:, cols]
            probs, psink = _attn_probs(qg, kh, valid, _stack_sinks(sink_ref, kv))
            dprobs = _nt(dog, vh)
            dvp_ref[:, cols] = _tn(probs.astype(BF16), dog)
            rowdot = jnp.sum(probs * dprobs, axis=-1, keepdims=True)
            dsb = (probs * (dprobs - rowdot) * scale).astype(BF16)
            dqg = _nn(dsb, kh)
            dkp_ref[:, cols] = _tn(dsb, qg)
            dsk = -psink * rowdot
            for j in range(GROUP):
                hq = GROUP * kv + j
                dq_ref[:, HEAD * hq:HEAD * (hq + 1)] = dqg[QBLK * j:QBLK * (j + 1), :]
                dsink = dsink + jnp.where(lane == hq, _colsum(dsk[QBLK * j:QBLK * (j + 1), :]), 0.0)
        ds_ref[...] += dsink

    qspec = pl.BlockSpec((QBLK, N_Q * HEAD), lambda n: (n, 0))
    pspec = pl.BlockSpec((None, 3 * QBLK, N_KV * HEAD), lambda n: (n, 0, 0))
    return _call(
        body, name="attn_bwd", grid=(nblk,),
        out_shape=[jax.ShapeDtypeStruct((s, N_Q * HEAD), F32),
                   jax.ShapeDtypeStruct((nblk, 3 * QBLK, N_KV * HEAD), F32),
                   jax.ShapeDtypeStruct((nblk, 3 * QBLK, N_KV * HEAD), F32),
                   jax.ShapeDtypeStruct((1, 128), F32)],
        in_specs=[pl.BlockSpec(memory_space=pltpu.SMEM), qspec, qspec] + _kv_specs(nblk) + _kv_specs(nblk),
        out_specs=[qspec, pspec, pspec, pl.BlockSpec((1, 128), lambda n: (0, 0))],
        args=(sinks, q, do, kk, kk, kk, v, v, v), rider=rider)


def kv_combine(dkp, dvp, rider=None):
    nblk = dkp.shape[0]
    w = N_KV * HEAD

    def body(kp, ko, kn, vp, vo, vn, dk_ref, dv_ref):
        n = pl.program_id(0)
        dk_ref[...] = jnp.where(n > 0, kp[...], 0.0) + ko[...] + jnp.where(n < nblk - 1, kn[...], 0.0)
        dv_ref[...] = jnp.where(n > 0, vp[...], 0.0) + vo[...] + jnp.where(n < nblk - 1, vn[...], 0.0)

    specs = [pl.BlockSpec((None, QBLK, w), lambda n: (jnp.maximum(n - 1, 0), 2, 0)),
             pl.BlockSpec((None, QBLK, w), lambda n: (n, 1, 0)),
             pl.BlockSpec((None, QBLK, w), lambda n: (jnp.minimum(n + 1, nblk - 1), 0, 0))]
    out = pl.BlockSpec((QBLK, w), lambda n: (n, 0))
    return _call(
        body, name="kv_combine", grid=(nblk,),
        out_shape=[jax.ShapeDtypeStruct((nblk * QBLK, w), F32)] * 2,
        in_specs=specs + specs, out_specs=[out, out],
        args=(dkp, dkp, dkp, dvp, dvp, dvp), rider=rider)


def _in_bwd_tail(dzb, w_ref, x_ref, mod_ref, k, dres_ref, dx_ref, h_ref, sums_ref):
    xv = x_ref[...]
    shift, scale, _ = _mod(mod_ref, k)
    h_ref[...] = (xv * (1.0 + scale) + shift).astype(BF16)
    dh = _nn(dzb, w_ref[...])
    dx_ref[...] = dres_ref[...] + dh * (1.0 + scale)
    sums_ref[0:1, :] += _colsum(dh)
    sums_ref[1:2, :] += _colsum(dh * xv)


def attn_in_bwd(dq, dk, dv, rope, x, mods, k, win_t, dres):
    s = x.shape[0]
    tm = _blk(s, TM_MM)

    def body(dq_ref, dk_ref, dv_ref, c_ref, s1_ref, s2_ref, x_ref, mod_ref, w_ref, dres_ref,
             dx_ref, dz_ref, h_ref, sums_ref):
        @pl.when(pl.program_id(0) == 0)
        def _():
            sums_ref[...] = jnp.zeros(sums_ref.shape, F32)

        cos, s1, s2 = c_ref[...], s1_ref[...], s2_ref[...]
        for hh in range(N_Q + N_KV):
            src = dq_ref[:, HEAD * hh:HEAD * (hh + 1)] if hh < N_Q else dk_ref[:, HEAD * (hh - N_Q):HEAD * (hh - N_Q + 1)]
            dz_ref[:, HEAD * hh:HEAD * (hh + 1)] = _rope_bwd(src, cos, s1, s2).astype(BF16)
        dz_ref[:, HEAD * (N_Q + N_KV):] = dv_ref[...].astype(BF16)
        _in_bwd_tail(dz_ref[...], w_ref, x_ref, mod_ref, k, dres_ref, dx_ref, h_ref, sums_ref)

    return pl.pallas_call(
        body, name="attn_in_bwd", grid=(s // tm,),
        out_shape=[jax.ShapeDtypeStruct((s, D_MODEL), F32), jax.ShapeDtypeStruct((s, D_QKV), BF16),
                   jax.ShapeDtypeStruct((s, D_MODEL), BF16), jax.ShapeDtypeStruct((8, D_MODEL), F32)],
        in_specs=[_row(tm, N_Q * HEAD), _row(tm, N_KV * HEAD), _row(tm, N_KV * HEAD),
                  _row(tm, HEAD), _row(tm, HEAD), _row(tm, HEAD), _row(tm, D_MODEL),
                  _res(mods.shape), _res(win_t.shape), _row(tm, D_MODEL)],
        out_specs=[_row(tm, D_MODEL), _row(tm, D_QKV), _row(tm, D_MODEL), _res((8, D_MODEL))],
        compiler_params=_params(),
    )(dq, dk, dv, *rope, x, mods, win_t, dres)


def rnn_in_bwd(dxr, dgt, x, mods, k, win_t, dres):
    s = x.shape[0]
    tm = _blk(s, TM_MM)

    def body(dxr_ref, dgt_ref, x_ref, mod_ref, w_ref, dres_ref, dx_ref, dz_ref, h_ref, sums_ref):
        @pl.when(pl.program_id(0) == 0)
        def _():
            sums_ref[...] = jnp.zeros(sums_ref.shape, F32)

        dz_ref[:, 0:D_RNN] = dxr_ref[...].astype(BF16)
        dz_ref[:, D_RNN:2 * D_RNN] = dgt_ref[...].astype(BF16)
        _in_bwd_tail(dz_ref[...], w_ref, x_ref, mod_ref, k, dres_ref, dx_ref, h_ref, sums_ref)

    return pl.pallas_call(
        body, name="rnn_in_bwd", grid=(s // tm,),
        out_shape=[jax.ShapeDtypeStruct((s, D_MODEL), F32), jax.ShapeDtypeStruct((s, 2 * D_RNN), BF16),
                   jax.ShapeDtypeStruct((s, D_MODEL), BF16), jax.ShapeDtypeStruct((8, D_MODEL), F32)],
        in_specs=[_row(tm, D_RNN), _row(tm, D_RNN), _row(tm, D_MODEL), _res(mods.shape), _res(win_t.shape),
                  _row(tm, D_MODEL)],
        out_specs=[_row(tm, D_MODEL), _row(tm, 2 * D_RNN), _row(tm, D_MODEL), _res((8, D_MODEL))],
        compiler_params=_params(),
    )(dxr, dgt, x, mods, win_t, dres)


def wgrad(a, b, name):
    s, m = a.shape
    n = b.shape[1]
    tm = next(t for t in (1024, 768, 512, 384, 256, 128) if m % t == 0)
    tk = _blk(s, TK_WG)
    nk = s // tk

    def body(a_ref, b_ref, o_ref, acc):
        kk = pl.program_id(1)

        @pl.when(kk == 0)
        def _():
            acc[...] = jnp.zeros(acc.shape, F32)

        acc[...] += _tn(a_ref[...], b_ref[...])

        @pl.when(kk == nk - 1)
        def _():
            o_ref[...] = acc[...].astype(BF16)

    out = pl.pallas_call(
        body, name=name, grid=(m // tm, nk),
        out_shape=jax.ShapeDtypeStruct((m, n), BF16),
        in_specs=[pl.BlockSpec((tk, tm), lambda i, kk: (kk, i)), pl.BlockSpec((tk, n), lambda i, kk: (kk, 0))],
        out_specs=pl.BlockSpec((tm, n), lambda i, kk: (i, 0)),
        scratch_shapes=[pltpu.VMEM((tm, n), F32)],
        compiler_params=_params(),
    )(a, b)
    return out.reshape(N_DEV, m // N_DEV, n)


def part_sum(parts, name):
    _, r, c = parts.shape
    tr = next(t for t in (256, 192, 128, 64, 32, 16, 8) if r % t == 0)

    def body(p_ref, o_ref):
        acc = p_ref[0].astype(F32)
        for j in range(1, N_DEV):
            acc = acc + p_ref[j].astype(F32)
        o_ref[...] = acc

    return pl.pallas_call(
        body, name=name, grid=(r // tr,),
        out_shape=jax.ShapeDtypeStruct((r, c), F32),
        in_specs=[pl.BlockSpec((N_DEV, tr, c), lambda i: (0, i, 0))],
        out_specs=pl.BlockSpec((tr, c), lambda i: (i, 0)),
        compiler_params=_params(),
    )(parts)


def adamw(w, g, m, v, name):
    shape = w.shape
    c = shape[-1]
    r = w.size // c
    w2, g2, m2, v2 = (t.reshape(r, c) for t in (w, g, m, v))
    tr = r if r * c <= 512 * 1024 else next(t for t in (512, 256, 128, 64, 32, 16, 8) if r % t == 0)

    def body(w_ref, g_ref, m_ref, v_ref, d_ref, nm_ref, nv_ref):
        gv = g_ref[...]
        nm = B1 * m_ref[...] + (1.0 - B1) * gv
        nv = B2 * v_ref[...] + (1.0 - B2) * (gv * gv)
        nm_ref[...] = nm
        nv_ref[...] = nv
        m_hat = nm / (1.0 - B1 ** STEP)
        v_hat = nv / (1.0 - B2 ** STEP)
        d_ref[...] = -LR * (m_hat / (jnp.sqrt(v_hat) + ADAM_EPS) + WD * w_ref[...])

    spec = pl.BlockSpec((tr, c), lambda i: (i, 0))
    outs = pl.pallas_call(
        body, name=name, grid=(r // tr,),
        out_shape=[jax.ShapeDtypeStruct((r, c), F32)] * 3,
        in_specs=[spec] * 4, out_specs=[spec] * 3,
        compiler_params=_params(),
    )(w2, g2, m2, v2)
    return tuple(o.reshape(shape) for o in outs)


def _rope_tables(s):
    half = ROT // 2
    inv_freq = THETA ** (-jnp.arange(0, ROT, 2, dtype=F32) / ROT)
    per_row = 128 // half
    pos = (per_row * jnp.arange(s // per_row)[:, None] + jnp.arange(128)[None, :] // half).astype(F32)
    ang = pos * jnp.tile(inv_freq, per_row)[None, :]
    cos, sin = lax.optimization_barrier((jnp.cos(ang), jnp.sin(ang)))
    cos, sin = cos.reshape(s, half), sin.reshape(s, half)
    zeros = jnp.zeros((s, HEAD - ROT), F32)
    c = jnp.concatenate([cos, cos, jnp.ones((s, HEAD - ROT), F32)], axis=1)
    s1 = jnp.concatenate([jnp.zeros((s, half), F32), sin, zeros], axis=1)
    s2 = jnp.concatenate([-sin, jnp.zeros((s, half), F32), zeros], axis=1)
    return c, s1, s2


def _blockdiag(w):
    w4 = w.reshape(N_CG, 4, RB_W, RB_W)
    eye = jnp.eye(4, dtype=w.dtype)
    return jnp.einsum("gipq,ij->gipjq", w4, eye).reshape(N_CG, CG, CG)


def _diag_blocks(w):
    w5 = w.reshape(N_CG, 4, RB_W, 4, RB_W)
    eye = jnp.eye(4, dtype=w.dtype)
    return jnp.einsum("gipjq,ij->gipq", w5, eye).reshape(N_RB, RB_W, RB_W)


def _cols(full, per):
    lead = full.shape[:-1]
    t = full.reshape(lead + (N_DEV, per))
    return jnp.moveaxis(t, -2, 0).reshape(N_DEV, -1)


def kernel(x, c, ada_w, ada_b, ln_g, ln_b, attn_w_in, attn_w_out, attn_sinks, rnn_w_in, rnn_conv_w, rnn_conv_b, rnn_w_a, rnn_b_a, rnn_w_x, rnn_b_x, rnn_lam, rnn_w_out, mlp_w1, mlp_w2, loss_target, m_ada_w, m_ada_b, m_ln_g, m_ln_b, m_attn_w_in, m_attn_w_out, m_attn_sinks, m_rnn_w_in, m_rnn_conv_w, m_rnn_conv_b, m_rnn_w_a, m_rnn_b_a, m_rnn_w_x, m_rnn_b_x, m_rnn_lam, m_rnn_w_out, m_mlp_w1, m_mlp_w2, v_ada_w, v_ada_b, v_ln_g, v_ln_b, v_attn_w_in, v_attn_w_out, v_attn_sinks, v_rnn_w_in, v_rnn_conv_w, v_rnn_conv_b, v_rnn_w_a, v_rnn_b_a, v_rnn_w_x, v_rnn_b_x, v_rnn_lam, v_rnn_w_out, v_mlp_w1, v_mlp_w2):
    s = x.shape[1]
    x0 = x.reshape(s, D_MODEL)
    target = loss_target.reshape(s, D_MODEL)
    weights = dict(ada_w=ada_w, ada_b=ada_b, ln_g=ln_g, ln_b=ln_b, attn_w_in=attn_w_in, attn_w_out=attn_w_out,
                   attn_sinks=attn_sinks, rnn_w_in=rnn_w_in, rnn_conv_w=rnn_conv_w, rnn_conv_b=rnn_conv_b,
                   rnn_w_a=rnn_w_a, rnn_b_a=rnn_b_a, rnn_w_x=rnn_w_x, rnn_b_x=rnn_b_x, rnn_lam=rnn_lam,
                   rnn_w_out=rnn_w_out, mlp_w1=mlp_w1, mlp_w2=mlp_w2)
    moments_m = dict(ada_w=m_ada_w, ada_b=m_ada_b, ln_g=m_ln_g, ln_b=m_ln_b, attn_w_in=m_attn_w_in,
                     attn_w_out=m_attn_w_out, attn_sinks=m_attn_sinks, rnn_w_in=m_rnn_w_in,
                     rnn_conv_w=m_rnn_conv_w, rnn_conv_b=m_rnn_conv_b, rnn_w_a=m_rnn_w_a, rnn_b_a=m_rnn_b_a,
                     rnn_w_x=m_rnn_w_x, rnn_b_x=m_rnn_b_x, rnn_lam=m_rnn_lam, rnn_w_out=m_rnn_w_out,
                     mlp_w1=m_mlp_w1, mlp_w2=m_mlp_w2)
    moments_v = dict(ada_w=v_ada_w, ada_b=v_ada_b, ln_g=v_ln_g, ln_b=v_ln_b, attn_w_in=v_attn_w_in,
                     attn_w_out=v_attn_w_out, attn_sinks=v_attn_sinks, rnn_w_in=v_rnn_w_in,
                     rnn_conv_w=v_rnn_conv_w, rnn_conv_b=v_rnn_conv_b, rnn_w_a=v_rnn_w_a, rnn_b_a=v_rnn_b_a,
                     rnn_w_x=v_rnn_w_x, rnn_b_x=v_rnn_b_x, rnn_lam=v_rnn_lam, rnn_w_out=v_rnn_w_out,
                     mlp_w1=v_mlp_w1, mlp_w2=v_mlp_w2)
    names = list(weights)

    def t16(w):
        return w.T.astype(BF16)

    big = [t16(attn_w_in[0]), attn_w_out[0].astype(BF16), t16(rnn_w_in[0]), rnn_w_out[0].astype(BF16),
           t16(mlp_w1[0]), mlp_w2[0].astype(BF16), t16(mlp_w1[1]), mlp_w2[1].astype(BF16)]
    small_local = jnp.concatenate([
        ln_g.reshape(-1), ln_b.reshape(-1), rnn_conv_w.reshape(-1), rnn_conv_b.reshape(-1),
        rnn_b_a.reshape(-1), rnn_b_x.reshape(-1), rnn_lam.reshape(-1)])
    small_local = jnp.pad(small_local, (0, 4096 - small_local.shape[0])).reshape(32, 128)
    flat = lambda g: g.reshape(N_DEV * g.shape[1], D_MODEL)
    c_all, modr, win_t, sm = ada_modulation(jnp.broadcast_to(c, (8, D_MODEL)), ada_w.reshape(4, D_MODEL, CG),
                                            ada_b.reshape(4, 1, CG), _Gather([big[0], small_local]))
    win_t = flat(win_t)
    sm = sm.reshape(N_DEV, 4096)

    def full_vec(off, rows, per):
        piece = sm[:, off:off + rows * per].reshape(N_DEV, rows, per)
        return jnp.moveaxis(piece, 0, 1).reshape(rows, N_DEV * per)

    lng_f, lnb_f = full_vec(0, 4, 128), full_vec(512, 4, 128)
    cw_f, cb_f = full_vec(1024, 4, 192), full_vec(1792, 1, 192)
    ba_f, bx_f, lam_f = full_vec(1984, 2, 192), full_vec(2368, 2, 192), full_vec(2752, 2, 192)
    wa_bd = [_blockdiag(rnn_w_a[0, d]).astype(BF16) for d in range(2)]
    wx_bd = [_blockdiag(rnn_w_x[0, d]).astype(BF16) for d in range(2)]

    mods = modr.reshape(N_DEV, 4, 8, CG)[:, :, 0, :]
    mods = jnp.moveaxis(mods, 0, 1).reshape(4, 3, D_MODEL).reshape(12, D_MODEL)
    rope = _rope_tables(s)
    ln = lambda k: (lng_f[k:k + 1], lnb_f[k:k + 1])

    q, kk, v, wout = attn_in_fwd(x0, mods, 0, win_t, rope, rider=_Gather([big[1]]))
    wout = flat(wout)
    o, *got = attn_fwd(q, kk, v, attn_sinks, rider=_Gather([big[4], big[5], big[2], big[3]]))
    w1t_0, w2_0, rin_t, rout = (flat(g) for g in got)
    x1, y0 = post_fwd(o, wout, x0, mods, 0, *ln(0))
    x2, y1, ra0, r0, *got = mlp_fwd(x1, mods, 1, w1t_0, w2_0, *ln(1), rider=_Gather([big[6], big[7]]))
    w1t_1, w2_1 = (flat(g) for g in got)
    xr, gt = rnn_in_fwd(x2, mods, 2, rin_t)
    xc = conv_fwd(xr, cw_f, cb_f)
    hf = lru_fwd(xc, wa_bd[0], wx_bd[0], ba_f[0:1], bx_f[0:1], lam_f[0:1], False)
    hb = lru_fwd(xc, wa_bd[1], wx_bd[1], ba_f[1:2], bx_f[1:2], lam_f[1:2], True)
    x3, y2, ypre = post_fwd(None, rout, x2, mods, 2, *ln(2), gate_act=(gt, hf, hb))
    y3, ra1, r1 = mlp_fwd(x3, mods, 3, w1t_1, w2_1, *ln(3), last=True)

    dx3, da1, h3, dy3, sums3 = mlp_bwd(target, x3, y3, ra1, mods, 3, w1t_1, w2_1, lng_f[3:4], lnb=lnb_f[3:4])
    g_w1t_1 = wgrad(da1, h3, "wgrad_w1_1")
    g_w2_1 = wgrad(r1, dy3, "wgrad_w2_1")
    dres2, dy2, sums2a, dhs, dgt, p_w1t_1 = post_bwd(dx3, x2, y2, mods, 2, rout, lng_f[2:3], gate_act=(gt, hf, hb),
                                                     rider=_AllToAll([g_w1t_1]))
    g_rout = wgrad(ypre, dy2, "wgrad_rnn_out")
    dxc_f, dwa_f, dwx_f, dba_f, dbx_f, dlam_f, p_w2_1, p_rout = lru_bwd(
        xc, dhs, hf, wa_bd[0], wx_bd[0], ba_f[0:1], bx_f[0:1], lam_f[0:1], False, rider=_AllToAll([g_w2_1, g_rout]))
    dxc_b, dwa_b, dwx_b, dba_b, dbx_b, dlam_b = lru_bwd(xc, dhs, hb, wa_bd[1], wx_bd[1], ba_f[1:2], bx_f[1:2],
                                                        lam_f[1:2], True)
    dxr, dcw, dcb = conv_bwd(dxc_f, dxc_b, xr, cw_f)
    dx2, dzz, h2, sums2b = rnn_in_bwd(dxr, dgt, x2, mods, 2, rin_t, dres2)
    g_rin_t = wgrad(dzz, h2, "wgrad_rnn_in")
    d_wa = jnp.stack([_diag_blocks(dwa_f), _diag_blocks(dwa_b)])
    d_wx = jnp.stack([_diag_blocks(dwx_f), _diag_blocks(dwx_b)])
    nflat = d_wa.size // N_DEV
    gates = jnp.concatenate([d_wa.reshape(N_DEV, nflat), d_wx.reshape(N_DEV, nflat)], axis=1)
    gates = gates.reshape(N_DEV, 2 * nflat // 128, 128)
    dx1, da0, h1, dy1, sums1, p_rin_t, p_gates = mlp_bwd(dx2, x1, y1, ra0, mods, 1, w1t_0, w2_0, lng_f[1:2],
                                                         rider=_AllToAll([g_rin_t, gates]))
    gates_sum = part_sum(p_gates, "part_sum_gates")
    g_w1t_0 = wgrad(da0, h1, "wgrad_w1_0")
    g_w2_0 = wgrad(r0, dy1, "wgrad_w2_0")
    dres0, dy0, sums0a, do = post_bwd(dx1, x0, y0, mods, 0, wout, lng_f[0:1])
    g_wout = wgrad(o, dy0, "wgrad_attn_out")
    dq, dkp, dvp, dsink, wag, p_w1t_0, p_w2_0 = attn_bwd(
        q, kk, v, do, attn_sinks, rider=_Multi(_Gather([gates_sum]), _AllToAll([g_w1t_0, g_w2_0])))
    dk, dv, p_wout = kv_combine(dkp, dvp, rider=_AllToAll([g_wout]))
    dx0, dqkv, h0, sums0b = attn_in_bwd(dq, dk, dv, rope, x0, mods, 0, win_t, dres0)
    g_win_t = wgrad(dqkv, h0, "wgrad_attn_in")

    sums = [sums0a + sums0b, sums1, sums2a + sums2b, sums3]
    gmod = jnp.stack([t[0:3] for t in sums])
    gsend = jnp.moveaxis(gmod.reshape(4, N_DEV, CG), 1, 0)
    gsend = jnp.pad(gsend, ((0, 0), (0, 4), (0, 0)))
    c_t = c_all[:, 0, :].T
    sq_err = jnp.sum(sums3[5]).reshape(1, 1)
    tail = jnp.concatenate([
        _cols(dcw, 192), _cols(dcb, 192),
        _cols(jnp.concatenate([dba_f, dba_b]), 192), _cols(jnp.concatenate([dbx_f, dbx_b]), 192),
        _cols(jnp.concatenate([dlam_f, dlam_b]), 192),
        _cols(jnp.stack([t[3] for t in sums]), 128), _cols(jnp.stack([t[4] for t in sums]), 128),
        jnp.broadcast_to(dsink[:, 0:8], (N_DEV, 8)), jnp.broadcast_to(sq_err, (N_DEV, 1))], axis=1)
    tail = jnp.pad(tail, ((0, 0), (0, 32 * 128 - tail.shape[1]))).reshape(N_DEV, 32, 128)
    g_ada_w, g_ada_b, red, p_win_t = epilogue(gsend, c_t, tail, _AllToAll([g_win_t]))
    grads = {"ada_w": g_ada_w.reshape(ada_w.shape), "ada_b": g_ada_b[0:4].reshape(ada_b.shape)}

    big_parts = [p_win_t, p_wout, p_rin_t, p_rout, p_w1t_0, p_w2_0, p_w1t_1, p_w2_1]
    gsum = [part_sum(p, "part_sum_%d" % i) for i, p in enumerate(big_parts)]
    grads.update({
        "attn_w_in": gsum[0].T[None], "attn_w_out": gsum[1][None],
        "rnn_w_in": gsum[2].T[None], "rnn_w_out": gsum[3][None],
        "mlp_w1": jnp.stack([gsum[4].T, gsum[6].T]), "mlp_w2": jnp.stack([gsum[5], gsum[7]]),
    })
    wag = wag.reshape(N_DEV, 2 * nflat)
    grads["rnn_w_a"] = wag[:, :nflat].reshape(rnn_w_a.shape)
    grads["rnn_w_x"] = wag[:, nflat:].reshape(rnn_w_x.shape)
    tl = red.reshape(-1)
    loss = 0.5 * tl[3144] / D_MODEL
    grads["rnn_conv_w"] = tl[0:768].reshape(rnn_conv_w.shape)
    grads["rnn_conv_b"] = tl[768:960].reshape(rnn_conv_b.shape)
    grads["rnn_b_a"] = tl[960:1344].reshape(rnn_b_a.shape)
    grads["rnn_b_x"] = tl[1344:1728].reshape(rnn_b_x.shape)
    grads["rnn_lam"] = tl[1728:2112].reshape(rnn_lam.shape)
    grads["ln_g"] = tl[2112:2624].reshape(ln_g.shape)
    grads["ln_b"] = tl[2624:3136].reshape(ln_b.shape)
    grads["attn_sinks"] = tl[3136:3144].reshape(attn_sinks.shape)

    delta, new_m, new_v = {}, {}, {}
    for n in names:
        delta[n], new_m[n], new_v[n] = adamw(weights[n], grads[n], moments_m[n], moments_v[n], "adamw_" + n)
    return (loss, dx0.reshape(x.shape), *[grads[n] for n in names], *[delta[n] for n in names],
            *[new_m[n] for n in names], *[new_v[n] for n in names])
```

```python
import functools
import math

import jax
import jax.numpy as jnp
from jax import lax
from jax.experimental import pallas as pl
from jax.experimental.pallas import tpu as pltpu

F32, BF16 = jnp.float32, jnp.bfloat16
MESH = pl.DeviceIdType.MESH

D_MODEL = 1024
N_Q, N_KV, HEAD = 8, 2, 128
ROT, THETA = 32, 500000.0
QBLK = 128
D_QKV = (N_Q + 2 * N_KV) * HEAD
D_RNN, N_RB, RB_W = 1536, 16, 96
CG = 384
N_CG = D_RNN // CG
D_FF = 4096
FF_CHUNK = 1024
DEPTH = 2
ALPHA = (2.0 * DEPTH) ** 0.25
LN_EPS = 1e-5
LRU_C = 8.0
N_DEV = 8
LR, B1, B2, ADAM_EPS, WD, STEP = 0.001, 0.9, 0.999, 1e-8, 0.01, 10

VMEM_LIMIT = 56 * 1024 * 1024
TM_MM = 512
TM_MLP = 256
TM_MLP_FWD = 512
TT_RNN = 512
SB_RNN = 512
TK_WG = 2048


def _nn(a, b):
    return jnp.dot(a, b, preferred_element_type=F32)


def _nt(a, b):
    return lax.dot_general(a, b, (((1,), (1,)), ((), ())), preferred_element_type=F32)


def _tn(a, b):
    return lax.dot_general(a, b, (((0,), (0,)), ((), ())), preferred_element_type=F32)


def _blk(n, pref):
    t = min(n, pref)
    assert n % t == 0, (n, pref)
    return t


def _params(**kw):
    return pltpu.CompilerParams(vmem_limit_bytes=VMEM_LIMIT, **kw)


def _row(tm, w):
    return pl.BlockSpec((tm, w), lambda i: (i, 0))


def _res(shape):
    return pl.BlockSpec(shape, lambda i: (0,) * len(shape), pipeline_mode=pl.Buffered(1))


def _mod(mod_ref, k):
    return mod_ref[3 * k:3 * k + 1, :], mod_ref[3 * k + 1:3 * k + 2, :], mod_ref[3 * k + 2:3 * k + 3, :]


def _ln_stats(z):
    mu = jnp.mean(z, axis=-1, keepdims=True)
    zc = z - mu
    var = jnp.mean(zc * zc, axis=-1, keepdims=True)
    rstd = lax.rsqrt(var + LN_EPS)
    return zc * rstd, rstd


def _ln_bwd(dxo, xhat, rstd, g):
    dxh = dxo * g
    m1 = jnp.mean(dxh, axis=-1, keepdims=True)
    m2 = jnp.mean(dxh * xhat, axis=-1, keepdims=True)
    return rstd * (dxh - m1 - xhat * m2)


def _colsum(v):
    return jnp.sum(v, axis=0, keepdims=True)


def _sigmoid(v):
    return 0.5 * jnp.tanh(0.5 * v) + 0.5


def _gelu_parts(v):
    k = math.sqrt(2.0 / math.pi)
    u = k * (v + 0.044715 * v * v * v)
    t = jnp.tanh(u)
    g = 0.5 * v * (1.0 + t)
    dg = 0.5 * (1.0 + t) + 0.5 * v * (1.0 - t * t) * k * (1.0 + 3.0 * 0.044715 * v * v)
    return g, dg


def _me():
    return lax.axis_index("x"), lax.axis_index("y"), lax.axis_index("c")


def _idx(p):
    return 4 * p[0] + 2 * p[1] + p[2]


def _peers(me):
    x, y, c = me
    out = []
    for k in range(1, N_DEV):
        out.append((1 - x if k & 4 else x, 1 - y if k & 2 else y, 1 - c if k & 1 else c))
    return out


class _Gather:
    def __init__(self, srcs):
        self.srcs = list(srcs)
        n = len(self.srcs)
        self.out_shape = [jax.ShapeDtypeStruct((N_DEV,) + s.shape, s.dtype) for s in self.srcs]
        self.scratch = [pltpu.SemaphoreType.DMA((n, 7)), pltpu.SemaphoreType.DMA((n, 7)),
                        pltpu.SemaphoreType.DMA((n,))]

    @staticmethod
    def _places():
        x, y, c = me = _me()
        return me, (x, y, 1 - c), [(1 - x, y), (x, 1 - y), (1 - x, 1 - y)]

    @staticmethod
    def _copy(outs, sems, t, k, block, to, src=None):
        slot = outs[t].at[_idx(block)]
        return pltpu.make_async_remote_copy(
            src_ref=slot if src is None else src, dst_ref=slot, send_sem=sems[0].at[t, k],
            recv_sem=sems[1].at[t, k], device_id=to, device_id_type=MESH)

    def _firsts(self, ins, outs, sems):
        me, sibling, chips = self._places()
        out = []
        for t in range(len(ins)):
            out.append(self._copy(outs, sems, t, 0, me, sibling, src=ins[t]))
            out += [self._copy(outs, sems, t, 1 + j, me, (*chip, me[2]), src=ins[t]) for j, chip in enumerate(chips)]
        return out

    def _locals(self, ins, outs, sems):
        me = _me()
        return [pltpu.make_async_copy(ins[t], outs[t].at[_idx(me)], sems[2].at[t]) for t in range(len(ins))]

    def start(self, ins, outs, sems):
        for cp in self._locals(ins, outs, sems) + self._firsts(ins, outs, sems):
            cp.start()

    def mid(self, ins, outs, sems):
        me, sibling, chips = self._places()
        for j, chip in enumerate(chips):
            for t in range(len(ins)):
                self._copy(outs, sems, t, 1 + j, (*chip, me[2]), me).wait_recv()
                self._copy(outs, sems, t, 4 + j, (*chip, me[2]), sibling).start()

    def finish(self, ins, outs, sems):
        me, sibling, chips = self._places()
        for t in range(len(ins)):
            self._copy(outs, sems, t, 0, sibling, me).wait_recv()
            for j, chip in enumerate(chips):
                self._copy(outs, sems, t, 4 + j, (*chip, 1 - me[2]), me).wait_recv()
        for cp in self._firsts(ins, outs, sems):
            cp.wait_send()
        for j, chip in enumerate(chips):
            for t in range(len(ins)):
                self._copy(outs, sems, t, 4 + j, (*chip, me[2]), sibling).wait_send()
        for cp in self._locals(ins, outs, sems):
            cp.wait()


class _AllToAll:
    def __init__(self, srcs):
        self.srcs = list(srcs)
        n = len(self.srcs)
        self.out_shape = [jax.ShapeDtypeStruct(s.shape, s.dtype) for s in self.srcs]
        self.scratch = [pltpu.SemaphoreType.DMA((n, 7)), pltpu.SemaphoreType.DMA((n, 7)),
                        pltpu.SemaphoreType.DMA((n,))]

    def _copies(self, ins, outs, sems):
        me = _me()
        loc, rem = [], []
        for t in range(len(ins)):
            loc.append(pltpu.make_async_copy(ins[t].at[_idx(me)], outs[t].at[_idx(me)], sems[2].at[t]))
            for k, p in enumerate(_peers(me)):
                rem.append(pltpu.make_async_remote_copy(
                    src_ref=ins[t].at[_idx(p)], dst_ref=outs[t].at[_idx(me)], send_sem=sems[0].at[t, k],
                    recv_sem=sems[1].at[t, k], device_id=p, device_id_type=MESH))
        return loc, rem

    def start(self, ins, outs, sems):
        loc, rem = self._copies(ins, outs, sems)
        for cp in loc + rem:
            cp.start()

    def mid(self, ins, outs, sems):
        pass

    def finish(self, ins, outs, sems):
        me = _me()
        for t in range(len(ins)):
            for k, p in enumerate(_peers(me)):
                slot = outs[t].at[_idx(p)]
                pltpu.make_async_remote_copy(
                    src_ref=slot, dst_ref=slot, send_sem=sems[0].at[t, k], recv_sem=sems[1].at[t, k],
                    device_id=p, device_id_type=MESH).wait_recv()
        loc, rem = self._copies(ins, outs, sems)
        for cp in rem:
            cp.wait_send()
        for cp in loc:
            cp.wait()


class _Multi:
    def __init__(self, *exs):
        self.exs = exs
        self.srcs = [s for e in exs for s in e.srcs]
        self.out_shape = [s for e in exs for s in e.out_shape]
        self.scratch = [s for e in exs for s in e.scratch]

    def _each(self, ins, outs, sems):
        i = j = 0
        for e in self.exs:
            n, m = len(e.srcs), len(e.scratch)
            yield e, ins[i:i + n], outs[i:i + n], sems[j:j + m]
            i, j = i + n, j + m

    def start(self, ins, outs, sems):
        for e, a, b, c in self._each(ins, outs, sems):
            e.start(a, b, c)

    def mid(self, ins, outs, sems):
        for e, a, b, c in self._each(ins, outs, sems):
            e.mid(a, b, c)

    def finish(self, ins, outs, sems):
        for e, a, b, c in self._each(ins, outs, sems):
            e.finish(a, b, c)


def _call(body, *, name, grid, in_specs, out_specs, out_shape, args, scratch_shapes=(), rider=None):
    in_specs, out_specs, out_shape = list(in_specs), list(out_specs), list(out_shape)
    scratch_shapes = list(scratch_shapes)
    if rider is None:
        return pl.pallas_call(body, name=name, grid=grid, out_shape=out_shape, in_specs=in_specs,
                              out_specs=out_specs, scratch_shapes=scratch_shapes, compiler_params=_params())(*args)
    nci, nco, ncs, nr = len(in_specs), len(out_shape), len(scratch_shapes), len(rider.srcs)
    nsteps = math.prod(grid)
    assert nsteps >= 2, (name, grid)
    mid = max(1, (3 * nsteps) // 4)

    def full(*refs):
        ci, ri = refs[:nci], refs[nci:nci + nr]
        co, ro = refs[nci + nr:nci + nr + nco], refs[nci + nr + nco:nci + 2 * nr + nco]
        cs, rs = refs[nci + 2 * nr + nco:nci + 2 * nr + nco + ncs], refs[nci + 2 * nr + nco + ncs:]
        step = pl.program_id(0)
        for d in range(1, len(grid)):
            step = step * grid[d] + pl.program_id(d)

        @pl.when(step == 0)
        def _():
            rider.start(ri, ro, rs)

        @pl.when(step == mid)
        def _():
            rider.mid(ri, ro, rs)

        body(*ci, *co, *cs)

        @pl.when(step == nsteps - 1)
        def _():
            rider.finish(ri, ro, rs)

    any_spec = pl.BlockSpec(memory_space=pl.ANY)
    return pl.pallas_call(
        full, name=name, grid=grid, out_shape=out_shape + rider.out_shape,
        in_specs=in_specs + [any_spec] * nr, out_specs=out_specs + [any_spec] * nr,
        scratch_shapes=scratch_shapes + rider.scratch, compiler_params=_params(),
    )(*args, *rider.srcs)


def _a2a_start(srcs, dsts, send_sems, recv_sems, local_sems, me, sem_base=0):
    peers = _peers(me)
    started = []
    for t in range(len(srcs)):
        loc = pltpu.make_async_copy(srcs[t].at[_idx(me)], dsts[t].at[_idx(me)], local_sems.at[sem_base + t])
        loc.start()
        started.append(("local", loc))
        for k, p in enumerate(peers):
            cp = pltpu.make_async_remote_copy(
                src_ref=srcs[t].at[_idx(p)], dst_ref=dsts[t].at[_idx(me)],
                send_sem=send_sems.at[sem_base + t, k], recv_sem=recv_sems.at[sem_base + t, k],
                device_id=p, device_id_type=MESH)
            cp.start()
            started.append(("remote", cp))
    return started


def _a2a_finish(started, dsts, send_sems, recv_sems, me, sem_base=0):
    peers = _peers(me)
    for t in range(len(dsts)):
        for k, p in enumerate(peers):
            slot = dsts[t].at[_idx(p)]
            pltpu.make_async_remote_copy(
                src_ref=slot, dst_ref=slot, send_sem=send_sems.at[sem_base + t, k],
                recv_sem=recv_sems.at[sem_base + t, k], device_id=p, device_id_type=MESH).wait_recv()
    for kind, cp in started:
        if kind == "local":
            cp.wait()
        else:
            cp.wait_send()


def ada_modulation(c8, ada_w, ada_b, ride):
    nr = len(ride.srcs)

    def body(c_ref, w_ref, b_ref, *rest):
        ride_in, (call_ref, modr_ref), ride_out = rest[:nr], rest[nr:nr + 2], rest[nr + 2:2 * nr + 2]
        modp, send_sems, recv_sems, local_sems = rest[2 * nr + 2:2 * nr + 6]
        ride_sems = rest[2 * nr + 6:]
        ride.start(ride_in, ride_out, ride_sems)
        me = _me()
        peers = _peers(me)
        sends = []
        for k, p in enumerate(peers):
            cp = pltpu.make_async_remote_copy(
                src_ref=c_ref, dst_ref=call_ref.at[_idx(me)], send_sem=send_sems.at[0, k],
                recv_sem=recv_sems.at[0, k], device_id=p, device_id_type=MESH)
            cp.start()
            sends.append(cp)
        call_ref[_idx(me)] = c_ref[...]
        for k, p in enumerate(peers):
            slot = call_ref.at[_idx(p)]
            pltpu.make_async_remote_copy(
                src_ref=slot, dst_ref=slot, send_sem=send_sems.at[0, k], recv_sem=recv_sems.at[0, k],
                device_id=p, device_id_type=MESH).wait_recv()
        for cp in sends:
            cp.wait_send()
        cv = call_ref[...].reshape(N_DEV * 8, D_MODEL)
        s = (cv * _sigmoid(cv)).astype(BF16)
        for k in range(4):
            res = _nn(s, w_ref[k].astype(BF16)) + b_ref[k]
            for j in range(N_DEV):
                modp[j, 8 * k:8 * k + 8, :] = res[8 * j:8 * j + 8, :]
        started = _a2a_start([modp], [modr_ref], send_sems, recv_sems, local_sems, me, sem_base=1)
        _a2a_finish(started, [modr_ref], send_sems, recv_sems, me, sem_base=1)
        ride.mid(ride_in, ride_out, ride_sems)
        ride.finish(ride_in, ride_out, ride_sems)

    vm, hbm = pl.BlockSpec(memory_space=pltpu.VMEM), pl.BlockSpec(memory_space=pl.ANY)
    return pl.pallas_call(
        body, name="ada_modulation",
        out_shape=[jax.ShapeDtypeStruct((N_DEV, 8, D_MODEL), F32), jax.ShapeDtypeStruct((N_DEV, 32, CG), F32)]
        + ride.out_shape,
        in_specs=[vm, vm, vm] + [hbm] * nr, out_specs=[vm, vm] + [hbm] * nr,
        scratch_shapes=[pltpu.VMEM((N_DEV, 32, CG), F32), pltpu.SemaphoreType.DMA((2, 7)),
                        pltpu.SemaphoreType.DMA((2, 7)), pltpu.SemaphoreType.DMA((2,))] + ride.scratch,
        compiler_params=_params(),
    )(c8, ada_w, ada_b, *ride.srcs)


def epilogue(gsend, c_t, tail, ride):
    nr = len(ride.srcs)
    rt = tail.shape[1]

    def body(g_ref, ct_ref, t_ref, *rest):
        ride_in, (gw_ref, gb_ref, red_ref), ride_out = rest[:nr], rest[nr:nr + 3], rest[nr + 3:2 * nr + 3]
        grecv, trecv, send_sems, recv_sems, local_sems = rest[2 * nr + 3:2 * nr + 8]
        ride_sems = rest[2 * nr + 8:]
        ride.start(ride_in, ride_out, ride_sems)
        me = _me()
        started = _a2a_start([g_ref, t_ref], [grecv, trecv], send_sems, recv_sems, local_sems, me)
        _a2a_finish(started, [grecv, trecv], send_sems, recv_sems, me)
        acc = trecv[0]
        for j in range(1, N_DEV):
            acc = acc + trecv[j]
        red_ref[...] = acc
        ct = ct_ref[...]
        st = (ct * _sigmoid(ct)).astype(BF16).astype(F32)
        gb = jnp.zeros((8, CG), F32)
        for b in range(N_DEV):
            gb = gb + grecv[b]
        gb_ref[...] = gb
        for k in range(4):
            acc = jnp.zeros((D_MODEL, CG), F32)
            for b in range(N_DEV):
                row = grecv[b, k:k + 1, :].astype(BF16).astype(F32)
                acc = acc + st[:, b:b + 1] * row
            gw_ref[k] = acc
        ride.mid(ride_in, ride_out, ride_sems)
        ride.finish(ride_in, ride_out, ride_sems)

    vm, hbm = pl.BlockSpec(memory_space=pltpu.VMEM), pl.BlockSpec(memory_space=pl.ANY)
    return pl.pallas_call(
        body, name="epilogue",
        out_shape=[jax.ShapeDtypeStruct((4, D_MODEL, CG), F32), jax.ShapeDtypeStruct((8, CG), F32),
                   jax.ShapeDtypeStruct((rt, 128), F32)] + ride.out_shape,
        in_specs=[vm, vm, vm] + [hbm] * nr, out_specs=[vm, vm, vm] + [hbm] * nr,
        scratch_shapes=[pltpu.VMEM((N_DEV, 8, CG), F32), pltpu.VMEM((N_DEV, rt, 128), F32),
                        pltpu.SemaphoreType.DMA((2, 7)), pltpu.SemaphoreType.DMA((2, 7)),
                        pltpu.SemaphoreType.DMA((2,))] + ride.scratch,
        compiler_params=_params(),
    )(gsend, c_t, tail, *ride.srcs)


def _rope(t, cos, s1, s2):
    return t * cos + pltpu.roll(t, 16, 1) * s1 + pltpu.roll(t, HEAD - 16, 1) * s2


def _rope_bwd(d, cos, s1, s2):
    return d * cos + pltpu.roll(d * s1, HEAD - 16, 1) + pltpu.roll(d * s2, 16, 1)


def attn_in_fwd(x, mods, k, win_t, rope, rider=None):
    s = x.shape[0]
    tm = _blk(s, TM_MM)

    def body(x_ref, mod_ref, w_ref, c_ref, s1_ref, s2_ref, q_ref, k_ref, v_ref):
        shift, scale, _ = _mod(mod_ref, k)
        h = (x_ref[...] * (1.0 + scale) + shift).astype(BF16)
        qkv = _nt(h, w_ref[...])
        cos, s1, s2 = c_ref[...], s1_ref[...], s2_ref[...]
        for hh in range(N_Q + N_KV):
            r = _rope(qkv[:, HEAD * hh:HEAD * (hh + 1)], cos, s1, s2).astype(BF16)
            if hh < N_Q:
                q_ref[:, HEAD * hh:HEAD * (hh + 1)] = r
            else:
                k_ref[:, HEAD * (hh - N_Q):HEAD * (hh - N_Q + 1)] = r
        v_ref[...] = qkv[:, HEAD * (N_Q + N_KV):].astype(BF16)

    return _call(
        body, name="attn_in_fwd", grid=(s // tm,),
        out_shape=[jax.ShapeDtypeStruct((s, N_Q * HEAD), BF16), jax.ShapeDtypeStruct((s, N_KV * HEAD), BF16),
                   jax.ShapeDtypeStruct((s, N_KV * HEAD), BF16)],
        in_specs=[_row(tm, D_MODEL), _res(mods.shape), _res(win_t.shape),
                  _row(tm, HEAD), _row(tm, HEAD), _row(tm, HEAD)],
        out_specs=[_row(tm, N_Q * HEAD), _row(tm, N_KV * HEAD), _row(tm, N_KV * HEAD)],
        args=(x, mods, win_t, *rope), rider=rider)


QPAIR = 2


def _kv_specs(nblk):
    w = N_KV * HEAD
    return [pl.BlockSpec((QBLK, w), lambda n: (jnp.maximum(QPAIR * n - 1, 0), 0)),
            pl.BlockSpec((QBLK, w), lambda n: (QPAIR * n, 0)),
            pl.BlockSpec((QBLK, w), lambda n: (QPAIR * n + 1, 0)),
            pl.BlockSpec((QBLK, w), lambda n: (jnp.minimum(QPAIR * n + 2, nblk - 1), 0))]


GROUP = N_Q // N_KV


def _attn_mask(n, s):
    qi = lax.broadcasted_iota(jnp.int32, (GROUP * QBLK, 3 * QBLK), 0) & (QBLK - 1)
    kj = lax.broadcasted_iota(jnp.int32, (GROUP * QBLK, 3 * QBLK), 1)
    rel = kj - QBLK - qi
    kpos = kj + (n - 1) * QBLK
    return (jnp.abs(rel) <= QBLK) & (kpos >= 0) & (kpos < s)


def _stack_heads(ref, qb, kv):
    rows = slice(QBLK * qb, QBLK * (qb + 1))
    return jnp.concatenate([ref[rows, HEAD * (GROUP * kv + j):HEAD * (GROUP * kv + j + 1)] for j in range(GROUP)],
                           axis=0)


def _stack_sinks(sink_ref, kv):
    row = lax.broadcasted_iota(jnp.int32, (GROUP * QBLK, 1), 0)
    out = jnp.full((GROUP * QBLK, 1), sink_ref[0, GROUP * kv + GROUP - 1], F32)
    for j in range(GROUP - 2, -1, -1):
        out = jnp.where(row < QBLK * (j + 1), sink_ref[0, GROUP * kv + j], out)
    return out


def _attn_probs(qh, kh, valid, sink):
    sc = _nt(qh, kh) * (HEAD ** -0.5)
    sc = jnp.where(valid, sc, -1e30)
    m = jnp.maximum(jnp.max(sc, axis=-1, keepdims=True), sink)
    p = jnp.exp(sc - m)
    es = jnp.exp(sink - m)
    denom = jnp.sum(p, axis=-1, keepdims=True) + es
    return p / denom, es / denom


def attn_fwd(q, kk, v, sinks, rider=None):
    s = q.shape[0]
    nblk = s // QBLK

    def body(sink_ref, q_ref, k0, k1, k2, k3, v0, v1, v2, v3, o_ref):
        n = pl.program_id(0)
        kall = jnp.concatenate([k0[...], k1[...], k2[...], k3[...]], axis=0)
        vall = jnp.concatenate([v0[...], v1[...], v2[...], v3[...]], axis=0)
        for qb in range(QPAIR):
            valid = _attn_mask(QPAIR * n + qb, s)
            keys = slice(QBLK * qb, QBLK * (qb + 3))
            for kv in range(N_KV):
                cols = slice(HEAD * kv, HEAD * (kv + 1))
                probs, _ = _attn_probs(_stack_heads(q_ref, qb, kv), kall[keys, cols], valid,
                                       _stack_sinks(sink_ref, kv))
                og = _nn(probs.astype(BF16), vall[keys, cols]).astype(BF16)
                for j in range(GROUP):
                    hq = GROUP * kv + j
                    o_ref[QBLK * qb:QBLK * (qb + 1), HEAD * hq:HEAD * (hq + 1)] = og[QBLK * j:QBLK * (j + 1), :]

    qspec = pl.BlockSpec((QPAIR * QBLK, N_Q * HEAD), lambda n: (n, 0))
    return _call(
        body, name="attn_fwd", grid=(nblk // QPAIR,),
        out_shape=[jax.ShapeDtypeStruct((s, N_Q * HEAD), BF16)],
        in_specs=[pl.BlockSpec(memory_space=pltpu.SMEM), qspec] + _kv_specs(nblk) + _kv_specs(nblk),
        out_specs=[qspec],
        args=(sinks, q, kk, kk, kk, kk, v, v, v, v), rider=rider)


def post_fwd(ypre, w, x, mods, k, lng, lnb, gate_act=None):
    s = x.shape[0]
    tm = _blk(s, TM_MM)
    kdim = w.shape[0]
    rnn = gate_act is not None

    def body(*refs):
        if rnn:
            gt_ref, hf_ref, hb_ref, w_ref, x_ref, mod_ref, g_ref, b_ref, xo_ref, y_ref, yp_ref = refs
            act, _ = _gelu_parts(gt_ref[...])
            yp = ((hf_ref[...] + hb_ref[...]) * act).astype(BF16)
            yp_ref[...] = yp
        else:
            yp_ref, w_ref, x_ref, mod_ref, g_ref, b_ref, xo_ref, y_ref = refs
            yp = yp_ref[...]
        _, _, gate = _mod(mod_ref, k)
        y = _nn(yp, w_ref[...])
        y_ref[...] = y
        xhat, _ = _ln_stats(ALPHA * x_ref[...] + (1.0 + gate) * y)
        xo_ref[...] = xhat * g_ref[...] + b_ref[...]

    act_in = list(gate_act) if rnn else [ypre]
    out_shape = [jax.ShapeDtypeStruct((s, D_MODEL), F32), jax.ShapeDtypeStruct((s, D_MODEL), F32)]
    out_specs = [_row(tm, D_MODEL), _row(tm, D_MODEL)]
    if rnn:
        out_shape.append(jax.ShapeDtypeStruct((s, kdim), BF16))
        out_specs.append(_row(tm, kdim))
    return pl.pallas_call(
        body, name="rnn_post_fwd" if rnn else "attn_post_fwd", grid=(s // tm,),
        out_shape=out_shape,
        in_specs=[_row(tm, kdim)] * len(act_in) + [_res(w.shape), _row(tm, D_MODEL), _res(mods.shape),
                                                    _res(lng.shape), _res(lnb.shape)],
        out_specs=out_specs,
        compiler_params=_params(),
    )(*act_in, w, x, mods, lng, lnb)


def mlp_fwd(x, mods, k, w1_t, w2, lng, lnb, rider=None, last=False):
    s = x.shape[0]
    tm = _blk(s, TM_MLP_FWD)

    def body(x_ref, mod_ref, w1_ref, w2_ref, g_ref, b_ref, *outs):
        xo_ref = None if last else outs[0]
        y_ref, ra_ref, r_ref = outs[-3:]
        xv = x_ref[...]
        shift, scale, gate = _mod(mod_ref, k)
        h = (xv * (1.0 + scale) + shift).astype(BF16)
        y = jnp.zeros((tm, D_MODEL), F32)
        for c in range(D_FF // FF_CHUNK):
            rows = slice(FF_CHUNK * c, FF_CHUNK * (c + 1))
            a = jnp.maximum(_nt(h, w1_ref[rows, :]), 0.0)
            r = (a * a).astype(BF16)
            ra_ref[:, rows] = a.astype(BF16)
            r_ref[:, rows] = r
            y = y + _nn(r, w2_ref[rows, :])
        y_ref[...] = y
        if not last:
            xhat, _ = _ln_stats(ALPHA * xv + (1.0 + gate) * y)
            xo_ref[...] = xhat * g_ref[...] + b_ref[...]

    nf = 1 if last else 2
    return _call(
        body, name="mlp_fwd_last" if last else "mlp_fwd", grid=(s // tm,),
        out_shape=[jax.ShapeDtypeStruct((s, D_MODEL), F32)] * nf + [jax.ShapeDtypeStruct((s, D_FF), BF16)] * 2,
        in_specs=[_row(tm, D_MODEL), _res(mods.shape), _res(w1_t.shape), _res(w2.shape),
                  _res(lng.shape), _res(lnb.shape)],
        out_specs=[_row(tm, D_MODEL)] * nf + [_row(tm, D_FF)] * 2,
        args=(x, mods, w1_t, w2, lng, lnb), rider=rider)


def rnn_in_fwd(x, mods, k, win_t):
    s = x.shape[0]
    tm = _blk(s, TM_MM)

    def body(x_ref, mod_ref, w_ref, xr_ref, gt_ref):
        shift, scale, _ = _mod(mod_ref, k)
        h = (x_ref[...] * (1.0 + scale) + shift).astype(BF16)
        xr_ref[...] = _nt(h, w_ref[0:D_RNN, :])
        gt_ref[...] = _nt(h, w_ref[D_RNN:2 * D_RNN, :])

    return pl.pallas_call(
        body, name="rnn_in_fwd", grid=(s // tm,),
        out_shape=[jax.ShapeDtypeStruct((s, D_RNN), F32)] * 2,
        in_specs=[_row(tm, D_MODEL), _res(mods.shape), _res(win_t.shape)],
        out_specs=[_row(tm, D_RNN)] * 2,
        compiler_params=_params(),
    )(x, mods, win_t)


def _shift_rows(v, k, row):
    n = v.shape[0]
    r = pltpu.roll(v, k % n, 0)
    keep = (row >= k) if k > 0 else (row < n + k)
    return jnp.where(keep, r, 0.0)


def conv_fwd(xr, cw, cb):
    s = xr.shape[0]

    def body(x_ref, w_ref, b_ref, o_ref):
        xv = x_ref[...]
        row = lax.broadcasted_iota(jnp.int32, xv.shape, 0)
        o_ref[...] = (b_ref[...] + w_ref[0:1, :] * _shift_rows(xv, 2, row) + w_ref[1:2, :] * _shift_rows(xv, 1, row)
                      + w_ref[2:3, :] * xv + w_ref[3:4, :] * _shift_rows(xv, -1, row))

    slab = pl.BlockSpec((s, 128), lambda j: (0, j))
    return pl.pallas_call(
        body, name="conv_fwd", grid=(D_RNN // 128,),
        out_shape=jax.ShapeDtypeStruct((s, D_RNN), F32),
        in_specs=[slab, pl.BlockSpec((4, 128), lambda j: (0, j)), pl.BlockSpec((1, 128), lambda j: (0, j))],
        out_specs=slab,
        compiler_params=_params(),
    )(xr, cw, cb)


def conv_bwd(da, db, xr, cw):
    s = xr.shape[0]

    def body(da_ref, db_ref, x_ref, w_ref, dx_ref, dw_ref, dbias_ref):
        d = da_ref[...] + db_ref[...]
        xv = x_ref[...]
        row = lax.broadcasted_iota(jnp.int32, xv.shape, 0)
        dx_ref[...] = (w_ref[0:1, :] * _shift_rows(d, -2, row) + w_ref[1:2, :] * _shift_rows(d, -1, row)
                       + w_ref[2:3, :] * d + w_ref[3:4, :] * _shift_rows(d, 1, row))
        dw_ref[0:1, :] = _colsum(d * _shift_rows(xv, 2, row))
        dw_ref[1:2, :] = _colsum(d * _shift_rows(xv, 1, row))
        dw_ref[2:3, :] = _colsum(d * xv)
        dw_ref[3:4, :] = _colsum(d * _shift_rows(xv, -1, row))
        dbias_ref[...] = _colsum(d)

    slab = pl.BlockSpec((s, 128), lambda j: (0, j))
    return pl.pallas_call(
        body, name="conv_bwd", grid=(D_RNN // 128,),
        out_shape=[jax.ShapeDtypeStruct((s, D_RNN), F32), jax.ShapeDtypeStruct((4, D_RNN), F32),
                   jax.ShapeDtypeStruct((1, D_RNN), F32)],
        in_specs=[slab, slab, slab, pl.BlockSpec((4, 128), lambda j: (0, j))],
        out_specs=[slab, pl.BlockSpec((4, 128), lambda j: (0, j)), pl.BlockSpec((1, 128), lambda j: (0, j))],
        compiler_params=_params(),
    )(da, db, xr, cw)


def _softplus_neg(lam):
    z = -lam
    e = jnp.exp(-jnp.abs(z))
    u = 1.0 + e
    log1p = jnp.where(u == 1.0, e, jnp.log(u) * e / jnp.where(u == 1.0, 1.0, u - 1.0))
    return jnp.maximum(z, 0.0) + log1p, 1.0 / (1.0 + jnp.exp(lam))


def _lru_gates(xv, wa_ref, wx_ref, ba_ref, bx_ref, lam_ref):
    xb = xv.astype(BF16)
    r = _sigmoid(_nn(xb, wa_ref[...]) + ba_ref[...])
    i = _sigmoid(_nn(xb, wx_ref[...]) + bx_ref[...])
    sp, sg = _softplus_neg(lam_ref[...])
    la = r * (-LRU_C * sp)
    a = jnp.exp(la)
    th = jnp.tanh(la)
    m2 = -2.0 * th / (1.0 - th)
    rmult = lax.rsqrt(jnp.maximum(m2, 1e-37))
    return xb, r, i, sp, sg, a, m2 * rmult, rmult


def _scan(a, u, h0, reverse):
    n, c = a.shape
    sub = lax.broadcasted_iota(jnp.int32, (8, c), 0)
    steps = [(8 - sh, sub < 8 - sh) if reverse else (sh, sub >= sh) for sh in (1, 2, 4)]
    out = [None] * (n // 8)
    edge = h0
    for k in (range(n // 8 - 1, -1, -1) if reverse else range(n // 8)):
        at, ut = a[8 * k:8 * k + 8], u[8 * k:8 * k + 8]
        for rot, keep in steps:
            a_s = jnp.where(keep, pltpu.roll(at, rot, 0), 1.0)
            u_s = jnp.where(keep, pltpu.roll(ut, rot, 0), 0.0)
            ut = at * u_s + ut
            at = at * a_s
        hk = ut + at * edge
        out[k] = hk
        edge = hk[0:1] if reverse else hk[7:8]
    return jnp.concatenate(out, axis=0)


def _lru_specs(nt, tt, reverse):
    tmap = (lambda t: nt - 1 - t) if reverse else (lambda t: t)
    blk = pl.BlockSpec((tt, CG), lambda g, t: (tmap(t), g))
    wsp = pl.BlockSpec((None, CG, CG), lambda g, t: (g, 0, 0))
    vec = pl.BlockSpec((1, CG), lambda g, t: (0, g))
    return tmap, blk, wsp, vec


def lru_fwd(xc, wa, wx, ba, bx, lam, reverse):
    s = xc.shape[0]
    tt = _blk(s, TT_RNN)
    sb = _blk(tt, SB_RNN)
    nt = s // tt

    def body(x_ref, wa_ref, wx_ref, ba_ref, bx_ref, lam_ref, hs_ref, carry):
        @pl.when(pl.program_id(1) == 0)
        def _():
            carry[...] = jnp.zeros(carry.shape, F32)

        xv = x_ref[...]
        _, _, i, _, _, a, mult, _ = _lru_gates(xv, wa_ref, wx_ref, ba_ref, bx_ref, lam_ref)
        u = mult * (i * xv)
        h0 = carry[0:1, :]
        order = range(tt // sb - 1, -1, -1) if reverse else range(tt // sb)
        for j in order:
            rows = slice(sb * j, sb * (j + 1))
            h = _scan(a[rows], u[rows], h0, reverse)
            hs_ref[rows, :] = h
            h0 = h[0:1, :] if reverse else h[sb - 1:sb, :]
        carry[0:1, :] = h0

    _, blk, wsp, vec = _lru_specs(nt, tt, reverse)
    return pl.pallas_call(
        body, name="lru_fwd_rev" if reverse else "lru_fwd", grid=(N_CG, nt),
        out_shape=jax.ShapeDtypeStruct((s, D_RNN), F32),
        in_specs=[blk, wsp, wsp, vec, vec, vec], out_specs=blk,
        scratch_shapes=[pltpu.VMEM((8, CG), F32)],
        compiler_params=_params(),
    )(xc, wa, wx, ba, bx, lam)


def lru_bwd(xc, dhs, hs, wa, wx, ba, bx, lam, reverse, rider=None):
    s = xc.shape[0]
    tt = _blk(s, TT_RNN)
    sb = _blk(tt, SB_RNN)
    nt = s // tt
    back = not reverse

    def body(x_ref, dh_ref, hs_ref, nb_ref, wa_ref, wx_ref, ba_ref, bx_ref, lam_ref,
             dx_ref, dwa_ref, dwx_ref, dba_ref, dbx_ref, dlam_ref, carry):
        t = pl.program_id(1)

        @pl.when(t == 0)
        def _():
            carry[...] = jnp.zeros(carry.shape, F32)
            dwa_ref[...] = jnp.zeros(dwa_ref.shape, F32)
            dwx_ref[...] = jnp.zeros(dwx_ref.shape, F32)
            dba_ref[...] = jnp.zeros(dba_ref.shape, F32)
            dbx_ref[...] = jnp.zeros(dbx_ref.shape, F32)
            dlam_ref[...] = jnp.zeros(dlam_ref.shape, F32)

        xv = x_ref[...]
        xb, r, i, sp, sg, a, mult, rmult = _lru_gates(xv, wa_ref, wx_ref, ba_ref, bx_ref, lam_ref)
        row = lax.broadcasted_iota(jnp.int32, xv.shape, 0)
        hsv = hs_ref[...]
        inner = t < nt - 1
        if reverse:
            edge = jnp.where(inner, nb_ref[0:1, :], 0.0)
            hprev = jnp.where(row == tt - 1, edge, pltpu.roll(hsv, tt - 1, 0))
            a_next = jnp.where(row == 0, carry[1:2, :], pltpu.roll(a, 1, 0))
        else:
            edge = jnp.where(inner, nb_ref[7:8, :], 0.0)
            hprev = jnp.where(row == 0, edge, pltpu.roll(hsv, 1, 0))
            a_next = jnp.where(row == tt - 1, carry[1:2, :], pltpu.roll(a, tt - 1, 0))
        dhv = dh_ref[...]
        g0 = carry[0:1, :]
        parts = [None] * (tt // sb)
        order = range(tt // sb - 1, -1, -1) if back else range(tt // sb)
        for j in order:
            rows = slice(sb * j, sb * (j + 1))
            gj = _scan(a_next[rows], dhv[rows], g0, back)
            parts[j] = gj
            g0 = gj[0:1, :] if back else gj[sb - 1:sb, :]
        g = jnp.concatenate(parts, axis=0) if len(parts) > 1 else parts[0]
        carry[0:1, :] = g0
        carry[1:2, :] = a[0:1, :] if back else a[tt - 1:tt, :]

        da = g * hprev
        dmult = g * (i * xv)
        di = g * mult * xv
        dla = da * a - dmult * (a * a) * rmult
        dpa = (dla * (-LRU_C * sp)) * r * (1.0 - r)
        dpx = di * i * (1.0 - i)
        dlam_ref[...] += _colsum(dla * (LRU_C * r * sg))
        dba_ref[...] += _colsum(dpa)
        dbx_ref[...] += _colsum(dpx)
        dpab, dpxb = dpa.astype(BF16), dpx.astype(BF16)
        dx_ref[...] = g * mult * i + _nt(dpab, wa_ref[...]) + _nt(dpxb, wx_ref[...])
        dwa_ref[...] += _tn(xb, dpab)
        dwx_ref[...] += _tn(xb, dpxb)

    tmap, blk, wsp, vec = _lru_specs(nt, tt, back)
    per8 = tt // 8
    if reverse:
        nb = pl.BlockSpec((8, CG), lambda g, t: (jnp.minimum((tmap(t) + 1) * per8, s // 8 - 1), g))
    else:
        nb = pl.BlockSpec((8, CG), lambda g, t: (jnp.maximum(tmap(t) * per8 - 1, 0), g))
    return _call(
        body, name="lru_bwd_rev" if reverse else "lru_bwd", grid=(N_CG, nt),
        out_shape=[jax.ShapeDtypeStruct((s, D_RNN), F32), jax.ShapeDtypeStruct((N_CG, CG, CG), F32),
                   jax.ShapeDtypeStruct((N_CG, CG, CG), F32)] + [jax.ShapeDtypeStruct((1, D_RNN), F32)] * 3,
        in_specs=[blk, blk, blk, nb, wsp, wsp, vec, vec, vec],
        out_specs=[blk, wsp, wsp, vec, vec, vec],
        scratch_shapes=[pltpu.VMEM((8, CG), F32)],
        args=(xc, dhs, hs, hs, wa, wx, ba, bx, lam), rider=rider)


def _ln_part_bwd(dxo, x, y, gate, g, sums_ref, loss_head=None):
    xhat, rstd = _ln_stats(ALPHA * x + (1.0 + gate) * y)
    if loss_head is not None:
        err = xhat * g + loss_head[0] - loss_head[1]
        dxo = err * (1.0 / D_MODEL)
        sums_ref[5:6, :] += _colsum(err * err)
    dz = _ln_bwd(dxo, xhat, rstd, g)
    sums_ref[2:3, :] += _colsum(dz * y)
    sums_ref[3:4, :] += _colsum(dxo * xhat)
    sums_ref[4:5, :] += _colsum(dxo)
    return dz


def mlp_bwd(dxo, x, y, ra, mods, k, w1_t, w2, lng, lnb=None, rider=None):
    s = x.shape[0]
    tm = _blk(s, TM_MLP)
    head = lnb is not None

    def body(d_ref, x_ref, y_ref, ra_ref, mod_ref, w1_ref, w2_ref, g_ref, *rest):
        b_ref = rest[0] if head else None
        dx_ref, da_ref, h_ref, dy_ref, sums_ref = rest[1:] if head else rest

        @pl.when(pl.program_id(0) == 0)
        def _():
            sums_ref[...] = jnp.zeros(sums_ref.shape, F32)

        xv = x_ref[...]
        shift, scale, gate = _mod(mod_ref, k)
        if head:
            dz = _ln_part_bwd(None, xv, y_ref[...], gate, g_ref[...], sums_ref, (b_ref[...], d_ref[...]))
        else:
            dz = _ln_part_bwd(d_ref[...], xv, y_ref[...], gate, g_ref[...], sums_ref)
        dyb = (dz * (1.0 + gate)).astype(BF16)
        dy_ref[...] = dyb
        h = (xv * (1.0 + scale) + shift).astype(BF16)
        h_ref[...] = h
        dh = jnp.zeros((tm, D_MODEL), F32)
        for c in range(D_FF // FF_CHUNK):
            rows = slice(FF_CHUNK * c, FF_CHUNK * (c + 1))
            da = (_nt(dyb, w2_ref[rows, :]) * (2.0 * ra_ref[:, rows].astype(F32))).astype(BF16)
            da_ref[:, rows] = da
            dh = dh + _nn(da, w1_ref[rows, :])
        dx_ref[...] = ALPHA * dz + dh * (1.0 + scale)
        sums_ref[0:1, :] += _colsum(dh)
        sums_ref[1:2, :] += _colsum(dh * xv)

    return _call(
        body, name="mlp_bwd", grid=(s // tm,),
        out_shape=[jax.ShapeDtypeStruct((s, D_MODEL), F32), jax.ShapeDtypeStruct((s, D_FF), BF16),
                   jax.ShapeDtypeStruct((s, D_MODEL), BF16),
                   jax.ShapeDtypeStruct((s, D_MODEL), BF16), jax.ShapeDtypeStruct((8, D_MODEL), F32)],
        in_specs=[_row(tm, D_MODEL)] * 3 + [_row(tm, D_FF), _res(mods.shape), _res(w1_t.shape), _res(w2.shape),
                                             _res(lng.shape)] + ([_res(lnb.shape)] if head else []),
        out_specs=[_row(tm, D_MODEL), _row(tm, D_FF), _row(tm, D_MODEL), _row(tm, D_MODEL), _res((8, D_MODEL))],
        args=(dxo, x, y, ra, mods, w1_t, w2, lng) + ((lnb,) if head else ()), rider=rider)


def post_bwd(dxo, x, y, mods, k, w, lng, gate_act=None, rider=None):
    s = x.shape[0]
    tm = _blk(s, TM_MM)
    kdim = w.shape[0]
    rnn = gate_act is not None

    def body(*refs):
        if rnn:
            (d_ref, x_ref, y_ref, mod_ref, w_ref, g_ref, gt_ref, hf_ref, hb_ref,
             dres_ref, dy_ref, sums_ref, dhs_ref, dgt_ref) = refs
        else:
            d_ref, x_ref, y_ref, mod_ref, w_ref, g_ref, dres_ref, dy_ref, sums_ref, dyp_ref = refs

        @pl.when(pl.program_id(0) == 0)
        def _():
            sums_ref[...] = jnp.zeros(sums_ref.shape, F32)

        _, _, gate = _mod(mod_ref, k)
        dz = _ln_part_bwd(d_ref[...], x_ref[...], y_ref[...], gate, g_ref[...], sums_ref)
        dres_ref[...] = ALPHA * dz
        dyb = (dz * (1.0 + gate)).astype(BF16)
        dy_ref[...] = dyb
        dyp = _nt(dyb, w_ref[...])
        if rnn:
            act, dact = _gelu_parts(gt_ref[...])
            dhs_ref[...] = dyp * act
            dgt_ref[...] = dyp * (hf_ref[...] + hb_ref[...]) * dact
        else:
            dyp_ref[...] = dyp.astype(BF16)

    ins = [dxo, x, y, mods, w, lng] + (list(gate_act) if rnn else [])
    in_specs = [_row(tm, D_MODEL)] * 3 + [_res(mods.shape), _res(w.shape), _res(lng.shape)]
    out_shape = [jax.ShapeDtypeStruct((s, D_MODEL), F32), jax.ShapeDtypeStruct((s, D_MODEL), BF16),
                 jax.ShapeDtypeStruct((8, D_MODEL), F32)]
    out_specs = [_row(tm, D_MODEL), _row(tm, D_MODEL), _res((8, D_MODEL))]
    if rnn:
        in_specs += [_row(tm, kdim)] * 3
        out_shape += [jax.ShapeDtypeStruct((s, kdim), F32)] * 2
        out_specs += [_row(tm, kdim)] * 2
    else:
        out_shape.append(jax.ShapeDtypeStruct((s, kdim), BF16))
        out_specs.append(_row(tm, kdim))
    return _call(
        body, name="rnn_post_bwd" if rnn else "attn_post_bwd", grid=(s // tm,),
        out_shape=out_shape, in_specs=in_specs, out_specs=out_specs, args=ins, rider=rider)


def attn_bwd(q, kk, v, do, sinks, rider=None):
    s = q.shape[0]
    nblk = s // QBLK
    scale = HEAD ** -0.5

    def body(sink_ref, q_ref, do_ref, k0, k1, k2, k3, v0, v1, v2, v3, dq_ref, dk_ref, dv_ref, ds_ref):
        n = pl.program_id(0)

        @pl.when(n == 0)
        def _():
            ds_ref[...] = jnp.zeros(ds_ref.shape, F32)
            dk_ref[...] = jnp.zeros(dk_ref.shape, F32)
            dv_ref[...] = jnp.zeros(dv_ref.shape, F32)

        kall = jnp.concatenate([k0[...], k1[...], k2[...], k3[...]], axis=0)
        vall = jnp.concatenate([v0[...], v1[...], v2[...], v3[...]], axis=0)
        lane = lax.broadcasted_iota(jnp.int32, (1, 128), 1)
        dsink = jnp.zeros((1, 128), F32)
        for qb in range(QPAIR):
            nb = QPAIR * n + qb
            valid = _attn_mask(nb, s)
            keys = slice(QBLK * qb, QBLK * (qb + 3))
            for kv in range(N_KV):
                cols = slice(HEAD * kv, HEAD * (kv + 1))
                qg, dog = _stack_heads(q_ref, qb, kv), _stack_heads(do_ref, qb, kv)
                kh, vh = kall[keys, cols], vall[keys, cols]
                probs, psink = _attn_probs(qg, kh, valid, _stack_sinks(sink_ref, kv))
                dprobs = _nt(dog, vh)
                dvp = _tn(probs.astype(BF16), dog)
                rowdot = jnp.sum(probs * dprobs, axis=-1, keepdims=True)
                dsb = (probs * (dprobs - rowdot) * scale).astype(BF16)
                dqg = _nn(dsb, kh)
                dkp = _tn(dsb, qg)
                for p in range(3):
                    blk = jnp.clip(nb - 1 + p, 0, nblk - 1)
                    rows = pl.ds(pl.multiple_of(blk * QBLK, QBLK), QBLK)
                    dk_ref[rows, cols] += dkp[QBLK * p:QBLK * (p + 1), :]
                    dv_ref[rows, cols] += dvp[QBLK * p:QBLK * (p + 1), :]
                dsk = -psink * rowdot
                for j in range(GROUP):
                    hq = GROUP * kv + j
                    dq_ref[QBLK * qb:QBLK * (qb + 1), HEAD * hq:HEAD * (hq + 1)] = dqg[QBLK * j:QBLK * (j + 1), :]
                    dsink = dsink + jnp.where(lane == hq, _colsum(dsk[QBLK * j:QBLK * (j + 1), :]), 0.0)
        ds_ref[...] += dsink

    qspec = pl.BlockSpec((QPAIR * QBLK, N_Q * HEAD), lambda n: (n, 0))
    return _call(
        body, name="attn_bwd", grid=(nblk // QPAIR,),
        out_shape=[jax.ShapeDtypeStruct((s, N_Q * HEAD), F32),
                   jax.ShapeDtypeStruct((s, N_KV * HEAD), F32), jax.ShapeDtypeStruct((s, N_KV * HEAD), F32),
                   jax.ShapeDtypeStruct((1, 128), F32)],
        in_specs=[pl.BlockSpec(memory_space=pltpu.SMEM), qspec, qspec] + _kv_specs(nblk) + _kv_specs(nblk),
        out_specs=[qspec, _res((s, N_KV * HEAD)), _res((s, N_KV * HEAD)), pl.BlockSpec((1, 128), lambda n: (0, 0))],
        args=(sinks, q, do, kk, kk, kk, kk, v, v, v, v), rider=rider)


def _in_bwd_tail(dzb, w_ref, x_ref, mod_ref, k, dres_ref, dx_ref, h_ref, sums_ref):
    xv = x_ref[...]
    shift, scale, _ = _mod(mod_ref, k)
    h_ref[...] = (xv * (1.0 + scale) + shift).astype(BF16)
    dh = _nn(dzb, w_ref[...])
    dx_ref[...] = dres_ref[...] + dh * (1.0 + scale)
    sums_ref[0:1, :] += _colsum(dh)
    sums_ref[1:2, :] += _colsum(dh * xv)


def attn_in_bwd(dq, dk, dv, rope, x, mods, k, win_t, dres):
    s = x.shape[0]
    tm = _blk(s, TM_MM)

    def body(dq_ref, dk_ref, dv_ref, c_ref, s1_ref, s2_ref, x_ref, mod_ref, w_ref, dres_ref,
             dx_ref, dz_ref, h_ref, sums_ref):
        @pl.when(pl.program_id(0) == 0)
        def _():
            sums_ref[...] = jnp.zeros(sums_ref.shape, F32)

        cos, s1, s2 = c_ref[...], s1_ref[...], s2_ref[...]
        for hh in range(N_Q + N_KV):
            src = dq_ref[:, HEAD * hh:HEAD * (hh + 1)] if hh < N_Q else dk_ref[:, HEAD * (hh - N_Q):HEAD * (hh - N_Q + 1)]
            dz_ref[:, HEAD * hh:HEAD * (hh + 1)] = _rope_bwd(src, cos, s1, s2).astype(BF16)
        dz_ref[:, HEAD * (N_Q + N_KV):] = dv_ref[...].astype(BF16)
        _in_bwd_tail(dz_ref[...], w_ref, x_ref, mod_ref, k, dres_ref, dx_ref, h_ref, sums_ref)

    return pl.pallas_call(
        body, name="attn_in_bwd", grid=(s // tm,),
        out_shape=[jax.ShapeDtypeStruct((s, D_MODEL), F32), jax.ShapeDtypeStruct((s, D_QKV), BF16),
                   jax.ShapeDtypeStruct((s, D_MODEL), BF16), jax.ShapeDtypeStruct((8, D_MODEL), F32)],
        in_specs=[_row(tm, N_Q * HEAD), _row(tm, N_KV * HEAD), _row(tm, N_KV * HEAD),
                  _row(tm, HEAD), _row(tm, HEAD), _row(tm, HEAD), _row(tm, D_MODEL),
                  _res(mods.shape), _res(win_t.shape), _row(tm, D_MODEL)],
        out_specs=[_row(tm, D_MODEL), _row(tm, D_QKV), _row(tm, D_MODEL), _res((8, D_MODEL))],
        compiler_params=_params(),
    )(dq, dk, dv, *rope, x, mods, win_t, dres)


def rnn_in_bwd(dxr, dgt, x, mods, k, win_t, dres):
    s = x.shape[0]
    tm = _blk(s, TM_MM)

    def body(dxr_ref, dgt_ref, x_ref, mod_ref, w_ref, dres_ref, dx_ref, dz_ref, h_ref, sums_ref):
        @pl.when(pl.program_id(0) == 0)
        def _():
            sums_ref[...] = jnp.zeros(sums_ref.shape, F32)

        dz_ref[:, 0:D_RNN] = dxr_ref[...].astype(BF16)
        dz_ref[:, D_RNN:2 * D_RNN] = dgt_ref[...].astype(BF16)
        _in_bwd_tail(dz_ref[...], w_ref, x_ref, mod_ref, k, dres_ref, dx_ref, h_ref, sums_ref)

    return pl.pallas_call(
        body, name="rnn_in_bwd", grid=(s // tm,),
        out_shape=[jax.ShapeDtypeStruct((s, D_MODEL), F32), jax.ShapeDtypeStruct((s, 2 * D_RNN), BF16),
                   jax.ShapeDtypeStruct((s, D_MODEL), BF16), jax.ShapeDtypeStruct((8, D_MODEL), F32)],
        in_specs=[_row(tm, D_RNN), _row(tm, D_RNN), _row(tm, D_MODEL), _res(mods.shape), _res(win_t.shape),
                  _row(tm, D_MODEL)],
        out_specs=[_row(tm, D_MODEL), _row(tm, 2 * D_RNN), _row(tm, D_MODEL), _res((8, D_MODEL))],
        compiler_params=_params(),
    )(dxr, dgt, x, mods, win_t, dres)


def wgrad(a, b, name):
    s, m = a.shape
    n = b.shape[1]
    tm = next(t for t in (1024, 768, 512, 384, 256, 128) if m % t == 0)
    tk = _blk(s, TK_WG)
    nk = s // tk

    def body(a_ref, b_ref, o_ref, acc):
        kk = pl.program_id(1)

        @pl.when(kk == 0)
        def _():
            acc[...] = jnp.zeros(acc.shape, F32)

        acc[...] += _tn(a_ref[...], b_ref[...])

        @pl.when(kk == nk - 1)
        def _():
            o_ref[...] = acc[...].astype(BF16)

    out = pl.pallas_call(
        body, name=name, grid=(m // tm, nk),
        out_shape=jax.ShapeDtypeStruct((m, n), BF16),
        in_specs=[pl.BlockSpec((tk, tm), lambda i, kk: (kk, i)), pl.BlockSpec((tk, n), lambda i, kk: (kk, 0))],
        out_specs=pl.BlockSpec((tm, n), lambda i, kk: (i, 0)),
        scratch_shapes=[pltpu.VMEM((tm, n), F32)],
        compiler_params=_params(),
    )(a, b)
    return out.reshape(N_DEV, m // N_DEV, n)


def part_sum(parts, name):
    _, r, c = parts.shape
    tr = next(t for t in (256, 192, 128, 64, 32, 16, 8) if r % t == 0)

    def body(p_ref, o_ref):
        acc = p_ref[0].astype(F32)
        for j in range(1, N_DEV):
            acc = acc + p_ref[j].astype(F32)
        o_ref[...] = acc

    return pl.pallas_call(
        body, name=name, grid=(r // tr,),
        out_shape=jax.ShapeDtypeStruct((r, c), F32),
        in_specs=[pl.BlockSpec((N_DEV, tr, c), lambda i: (0, i, 0))],
        out_specs=pl.BlockSpec((tr, c), lambda i: (i, 0)),
        compiler_params=_params(),
    )(parts)


def adamw(w, g, m, v, name):
    shape = w.shape
    c = shape[-1]
    r = w.size // c
    w2, g2, m2, v2 = (t.reshape(r, c) for t in (w, g, m, v))
    tr = r if r * c <= 512 * 1024 else next(t for t in (512, 256, 128, 64, 32, 16, 8) if r % t == 0)

    def body(w_ref, g_ref, m_ref, v_ref, d_ref, nm_ref, nv_ref):
        gv = g_ref[...]
        nm = B1 * m_ref[...] + (1.0 - B1) * gv
        nv = B2 * v_ref[...] + (1.0 - B2) * (gv * gv)
        nm_ref[...] = nm
        nv_ref[...] = nv
        m_hat = nm / (1.0 - B1 ** STEP)
        v_hat = nv / (1.0 - B2 ** STEP)
        d_ref[...] = -LR * (m_hat / (jnp.sqrt(v_hat) + ADAM_EPS) + WD * w_ref[...])

    spec = pl.BlockSpec((tr, c), lambda i: (i, 0))
    outs = pl.pallas_call(
        body, name=name, grid=(r // tr,),
        out_shape=[jax.ShapeDtypeStruct((r, c), F32)] * 3,
        in_specs=[spec] * 4, out_specs=[spec] * 3,
        compiler_params=_params(),
    )(w2, g2, m2, v2)
    return tuple(o.reshape(shape) for o in outs)


def _rope_tables(s):
    half = ROT // 2
    inv_freq = THETA ** (-jnp.arange(0, ROT, 2, dtype=F32) / ROT)
    per_row = 128 // half
    pos = (per_row * jnp.arange(s // per_row)[:, None] + jnp.arange(128)[None, :] // half).astype(F32)
    ang = pos * jnp.tile(inv_freq, per_row)[None, :]
    cos, sin = lax.optimization_barrier((jnp.cos(ang), jnp.sin(ang)))
    cos, sin = cos.reshape(s, half), sin.reshape(s, half)
    zeros = jnp.zeros((s, HEAD - ROT), F32)
    c = jnp.concatenate([cos, cos, jnp.ones((s, HEAD - ROT), F32)], axis=1)
    s1 = jnp.concatenate([jnp.zeros((s, half), F32), sin, zeros], axis=1)
    s2 = jnp.concatenate([-sin, jnp.zeros((s, half), F32), zeros], axis=1)
    return c, s1, s2


def _blockdiag(w):
    w4 = w.reshape(N_CG, 4, RB_W, RB_W)
    eye = jnp.eye(4, dtype=w.dtype)
    return jnp.einsum("gipq,ij->gipjq", w4, eye).reshape(N_CG, CG, CG)


def _diag_blocks(w):
    w5 = w.reshape(N_CG, 4, RB_W, 4, RB_W)
    eye = jnp.eye(4, dtype=w.dtype)
    return jnp.einsum("gipjq,ij->gipq", w5, eye).reshape(N_RB, RB_W, RB_W)


def _cols(full, per):
    lead = full.shape[:-1]
    t = full.reshape(lead + (N_DEV, per))
    return jnp.moveaxis(t, -2, 0).reshape(N_DEV, -1)


def kernel(x, c, ada_w, ada_b, ln_g, ln_b, attn_w_in, attn_w_out, attn_sinks, rnn_w_in, rnn_conv_w, rnn_conv_b, rnn_w_a, rnn_b_a, rnn_w_x, rnn_b_x, rnn_lam, rnn_w_out, mlp_w1, mlp_w2, loss_target, m_ada_w, m_ada_b, m_ln_g, m_ln_b, m_attn_w_in, m_attn_w_out, m_attn_sinks, m_rnn_w_in, m_rnn_conv_w, m_rnn_conv_b, m_rnn_w_a, m_rnn_b_a, m_rnn_w_x, m_rnn_b_x, m_rnn_lam, m_rnn_w_out, m_mlp_w1, m_mlp_w2, v_ada_w, v_ada_b, v_ln_g, v_ln_b, v_attn_w_in, v_attn_w_out, v_attn_sinks, v_rnn_w_in, v_rnn_conv_w, v_rnn_conv_b, v_rnn_w_a, v_rnn_b_a, v_rnn_w_x, v_rnn_b_x, v_rnn_lam, v_rnn_w_out, v_mlp_w1, v_mlp_w2):
    s = x.shape[1]
    x0 = x.reshape(s, D_MODEL)
    target = loss_target.reshape(s, D_MODEL)
    weights = dict(ada_w=ada_w, ada_b=ada_b, ln_g=ln_g, ln_b=ln_b, attn_w_in=attn_w_in, attn_w_out=attn_w_out,
                   attn_sinks=attn_sinks, rnn_w_in=rnn_w_in, rnn_conv_w=rnn_conv_w, rnn_conv_b=rnn_conv_b,
                   rnn_w_a=rnn_w_a, rnn_b_a=rnn_b_a, rnn_w_x=rnn_w_x, rnn_b_x=rnn_b_x, rnn_lam=rnn_lam,
                   rnn_w_out=rnn_w_out, mlp_w1=mlp_w1, mlp_w2=mlp_w2)
    moments_m = dict(ada_w=m_ada_w, ada_b=m_ada_b, ln_g=m_ln_g, ln_b=m_ln_b, attn_w_in=m_attn_w_in,
                     attn_w_out=m_attn_w_out, attn_sinks=m_attn_sinks, rnn_w_in=m_rnn_w_in,
                     rnn_conv_w=m_rnn_conv_w, rnn_conv_b=m_rnn_conv_b, rnn_w_a=m_rnn_w_a, rnn_b_a=m_rnn_b_a,
                     rnn_w_x=m_rnn_w_x, rnn_b_x=m_rnn_b_x, rnn_lam=m_rnn_lam, rnn_w_out=m_rnn_w_out,
                     mlp_w1=m_mlp_w1, mlp_w2=m_mlp_w2)
    moments_v = dict(ada_w=v_ada_w, ada_b=v_ada_b, ln_g=v_ln_g, ln_b=v_ln_b, attn_w_in=v_attn_w_in,
                     attn_w_out=v_attn_w_out, attn_sinks=v_attn_sinks, rnn_w_in=v_rnn_w_in,
                     rnn_conv_w=v_rnn_conv_w, rnn_conv_b=v_rnn_conv_b, rnn_w_a=v_rnn_w_a, rnn_b_a=v_rnn_b_a,
                     rnn_w_x=v_rnn_w_x, rnn_b_x=v_rnn_b_x, rnn_lam=v_rnn_lam, rnn_w_out=v_rnn_w_out,
                     mlp_w1=v_mlp_w1, mlp_w2=v_mlp_w2)
    names = list(weights)

    def t16(w):
        return w.T.astype(BF16)

    big = [t16(attn_w_in[0]), attn_w_out[0].astype(BF16), t16(rnn_w_in[0]), rnn_w_out[0].astype(BF16),
           t16(mlp_w1[0]), mlp_w2[0].astype(BF16), t16(mlp_w1[1]), mlp_w2[1].astype(BF16)]
    small_local = jnp.concatenate([
        ln_g.reshape(-1), ln_b.reshape(-1), rnn_conv_w.reshape(-1), rnn_conv_b.reshape(-1),
        rnn_b_a.reshape(-1), rnn_b_x.reshape(-1), rnn_lam.reshape(-1)])
    small_local = jnp.pad(small_local, (0, 4096 - small_local.shape[0])).reshape(32, 128)
    flat = lambda g: g.reshape(N_DEV * g.shape[1], D_MODEL)
    c_all, modr, win_t, sm = ada_modulation(jnp.broadcast_to(c, (8, D_MODEL)), ada_w.reshape(4, D_MODEL, CG),
                                            ada_b.reshape(4, 1, CG), _Gather([big[0], small_local]))
    win_t = flat(win_t)
    sm = sm.reshape(N_DEV, 4096)

    def full_vec(off, rows, per):
        piece = sm[:, off:off + rows * per].reshape(N_DEV, rows, per)
        return jnp.moveaxis(piece, 0, 1).reshape(rows, N_DEV * per)

    lng_f, lnb_f = full_vec(0, 4, 128), full_vec(512, 4, 128)
    cw_f, cb_f = full_vec(1024, 4, 192), full_vec(1792, 1, 192)
    ba_f, bx_f, lam_f = full_vec(1984, 2, 192), full_vec(2368, 2, 192), full_vec(2752, 2, 192)
    wa_bd = [_blockdiag(rnn_w_a[0, d]).astype(BF16) for d in range(2)]
    wx_bd = [_blockdiag(rnn_w_x[0, d]).astype(BF16) for d in range(2)]

    mods = modr.reshape(N_DEV, 4, 8, CG)[:, :, 0, :]
    mods = jnp.moveaxis(mods, 0, 1).reshape(4, 3, D_MODEL).reshape(12, D_MODEL)
    rope = _rope_tables(s)
    ln = lambda k: (lng_f[k:k + 1], lnb_f[k:k + 1])

    q, kk, v, wout = attn_in_fwd(x0, mods, 0, win_t, rope, rider=_Gather([big[1]]))
    wout = flat(wout)
    o, *got = attn_fwd(q, kk, v, attn_sinks, rider=_Gather([big[4], big[5], big[2], big[3]]))
    w1t_0, w2_0, rin_t, rout = (flat(g) for g in got)
    x1, y0 = post_fwd(o, wout, x0, mods, 0, *ln(0))
    x2, y1, ra0, r0, *got = mlp_fwd(x1, mods, 1, w1t_0, w2_0, *ln(1), rider=_Gather([big[6], big[7]]))
    w1t_1, w2_1 = (flat(g) for g in got)
    xr, gt = rnn_in_fwd(x2, mods, 2, rin_t)
    xc = conv_fwd(xr, cw_f, cb_f)
    hf = lru_fwd(xc, wa_bd[0], wx_bd[0], ba_f[0:1], bx_f[0:1], lam_f[0:1], False)
    hb = lru_fwd(xc, wa_bd[1], wx_bd[1], ba_f[1:2], bx_f[1:2], lam_f[1:2], True)
    x3, y2, ypre = post_fwd(None, rout, x2, mods, 2, *ln(2), gate_act=(gt, hf, hb))
    y3, ra1, r1 = mlp_fwd(x3, mods, 3, w1t_1, w2_1, *ln(3), last=True)

    dx3, da1, h3, dy3, sums3 = mlp_bwd(target, x3, y3, ra1, mods, 3, w1t_1, w2_1, lng_f[3:4], lnb=lnb_f[3:4])
    g_w1t_1 = wgrad(da1, h3, "wgrad_w1_1")
    g_w2_1 = wgrad(r1, dy3, "wgrad_w2_1")
    dres2, dy2, sums2a, dhs, dgt, p_w1t_1 = post_bwd(dx3, x2, y2, mods, 2, rout, lng_f[2:3], gate_act=(gt, hf, hb),
                                                     rider=_AllToAll([g_w1t_1]))
    g_rout = wgrad(ypre, dy2, "wgrad_rnn_out")
    dxc_f, dwa_f, dwx_f, dba_f, dbx_f, dlam_f, p_w2_1, p_rout = lru_bwd(
        xc, dhs, hf, wa_bd[0], wx_bd[0], ba_f[0:1], bx_f[0:1], lam_f[0:1], False, rider=_AllToAll([g_w2_1, g_rout]))
    dxc_b, dwa_b, dwx_b, dba_b, dbx_b, dlam_b = lru_bwd(xc, dhs, hb, wa_bd[1], wx_bd[1], ba_f[1:2], bx_f[1:2],
                                                        lam_f[1:2], True)
    dxr, dcw, dcb = conv_bwd(dxc_f, dxc_b, xr, cw_f)
    dx2, dzz, h2, sums2b = rnn_in_bwd(dxr, dgt, x2, mods, 2, rin_t, dres2)
    g_rin_t = wgrad(dzz, h2, "wgrad_rnn_in")
    d_wa = jnp.stack([_diag_blocks(dwa_f), _diag_blocks(dwa_b)])
    d_wx = jnp.stack([_diag_blocks(dwx_f), _diag_blocks(dwx_b)])
    nflat = d_wa.size // N_DEV
    gates = jnp.concatenate([d_wa.reshape(N_DEV, nflat), d_wx.reshape(N_DEV, nflat)], axis=1)
    gates = gates.reshape(N_DEV, 2 * nflat // 128, 128)
    dx1, da0, h1, dy1, sums1, p_rin_t, p_gates = mlp_bwd(dx2, x1, y1, ra0, mods, 1, w1t_0, w2_0, lng_f[1:2],
                                                         rider=_AllToAll([g_rin_t, gates]))
    gates_sum = part_sum(p_gates, "part_sum_gates")
    g_w1t_0 = wgrad(da0, h1, "wgrad_w1_0")
    g_w2_0 = wgrad(r0, dy1, "wgrad_w2_0")
    dres0, dy0, sums0a, do = post_bwd(dx1, x0, y0, mods, 0, wout, lng_f[0:1])
    g_wout = wgrad(o, dy0, "wgrad_attn_out")
    dq, dk, dv, dsink, wag, p_w1t_0, p_w2_0, p_wout = attn_bwd(
        q, kk, v, do, attn_sinks, rider=_Multi(_Gather([gates_sum]), _AllToAll([g_w1t_0, g_w2_0, g_wout])))
    dx0, dqkv, h0, sums0b = attn_in_bwd(dq, dk, dv, rope, x0, mods, 0, win_t, dres0)
    g_win_t = wgrad(dqkv, h0, "wgrad_attn_in")

    sums = [sums0a + sums0b, sums1, sums2a + sums2b, sums3]
    gmod = jnp.stack([t[0:3] for t in sums])
    gsend = jnp.moveaxis(gmod.reshape(4, N_DEV, CG), 1, 0)
    gsend = jnp.pad(gsend, ((0, 0), (0, 4), (0, 0)))
    c_t = c_all[:, 0, :].T
    sq_err = jnp.sum(sums3[5]).reshape(1, 1)
    tail = jnp.concatenate([
        _cols(dcw, 192), _cols(dcb, 192),
        _cols(jnp.concatenate([dba_f, dba_b]), 192), _cols(jnp.concatenate([dbx_f, dbx_b]), 192),
        _cols(jnp.concatenate([dlam_f, dlam_b]), 192),
        _cols(jnp.stack([t[3] for t in sums]), 128), _cols(jnp.stack([t[4] for t in sums]), 128),
        jnp.broadcast_to(dsink[:, 0:8], (N_DEV, 8)), jnp.broadcast_to(sq_err, (N_DEV, 1))], axis=1)
    tail = jnp.pad(tail, ((0, 0), (0, 32 * 128 - tail.shape[1]))).reshape(N_DEV, 32, 128)
    g_ada_w, g_ada_b, red, p_win_t = epilogue(gsend, c_t, tail, _AllToAll([g_win_t]))
    grads = {"ada_w": g_ada_w.reshape(ada_w.shape), "ada_b": g_ada_b[0:4].reshape(ada_b.shape)}

    big_parts = [p_win_t, p_wout, p_rin_t, p_rout, p_w1t_0, p_w2_0, p_w1t_1, p_w2_1]
    gsum = [part_sum(p, "part_sum_%d" % i) for i, p in enumerate(big_parts)]
    grads.update({
        "attn_w_in": gsum[0].T[None], "attn_w_out": gsum[1][None],
        "rnn_w_in": gsum[2].T[None], "rnn_w_out": gsum[3][None],
        "mlp_w1": jnp.stack([gsum[4].T, gsum[6].T]), "mlp_w2": jnp.stack([gsum[5], gsum[7]]),
    })
    wag = wag.reshape(N_DEV, 2 * nflat)
    grads["rnn_w_a"] = wag[:, :nflat].reshape(rnn_w_a.shape)
    grads["rnn_w_x"] = wag[:, nflat:].reshape(rnn_w_x.shape)
    tl = red.reshape(-1)
    loss = 0.5 * tl[3144] / D_MODEL
    grads["rnn_conv_w"] = tl[0:768].reshape(rnn_conv_w.shape)
    grads["rnn_conv_b"] = tl[768:960].reshape(rnn_conv_b.shape)
    grads["rnn_b_a"] = tl[960:1344].reshape(rnn_b_a.shape)
    grads["rnn_b_x"] = tl[1344:1728].reshape(rnn_b_x.shape)
    grads["rnn_lam"] = tl[1728:2112].reshape(rnn_lam.shape)
    grads["ln_g"] = tl[2112:2624].reshape(ln_g.shape)
    grads["ln_b"] = tl[2624:3136].reshape(ln_b.shape)
    grads["attn_sinks"] = tl[3136:3144].reshape(attn_sinks.shape)

    delta, new_m, new_v = {}, {}, {}
    for n in names:
        delta[n], new_m[n], new_v[n] = adamw(weights[n], grads[n], moments_m[n], moments_v[n], "adamw_" + n)
    return (loss, dx0.reshape(x.shape), *[grads[n] for n in names], *[delta[n] for n in names],
            *[new_m[n] for n in names], *[new_v[n] for n in names])
```

```python
import functools
import math

import jax
import jax.numpy as jnp
from jax import lax
from jax.experimental import pallas as pl
from jax.experimental.pallas import tpu as pltpu

F32, BF16 = jnp.float32, jnp.bfloat16
MESH = pl.DeviceIdType.MESH

D_MODEL = 1024
N_Q, N_KV, HEAD = 8, 2, 128
ROT, THETA = 32, 500000.0
QBLK = 128
D_QKV = (N_Q + 2 * N_KV) * HEAD
D_RNN, N_RB, RB_W = 1536, 16, 96
CG = 384
N_CG = D_RNN // CG
D_FF = 4096
FF_CHUNK = 1024
DEPTH = 2
ALPHA = (2.0 * DEPTH) ** 0.25
LN_EPS = 1e-5
LRU_C = 8.0
N_DEV = 8
LR, B1, B2, ADAM_EPS, WD, STEP = 0.001, 0.9, 0.999, 1e-8, 0.01, 10

VMEM_LIMIT = 56 * 1024 * 1024
TM_MM = 512
TM_MLP = 256
TM_MLP_FWD = 512
TT_RNN = 512
SB_RNN = 512
TK_WG = 2048


def _nn(a, b):
    return jnp.dot(a, b, preferred_element_type=F32)


def _nt(a, b):
    return lax.dot_general(a, b, (((1,), (1,)), ((), ())), preferred_element_type=F32)


def _tn(a, b):
    return lax.dot_general(a, b, (((0,), (0,)), ((), ())), preferred_element_type=F32)


def _blk(n, pref):
    t = min(n, pref)
    assert n % t == 0, (n, pref)
    return t


def _params(**kw):
    return pltpu.CompilerParams(vmem_limit_bytes=VMEM_LIMIT, **kw)


def _row(tm, w):
    return pl.BlockSpec((tm, w), lambda i: (i, 0))


def _res(shape):
    return pl.BlockSpec(shape, lambda i: (0,) * len(shape), pipeline_mode=pl.Buffered(1))


def _mod(mod_ref, k):
    return mod_ref[3 * k:3 * k + 1, :], mod_ref[3 * k + 1:3 * k + 2, :], mod_ref[3 * k + 2:3 * k + 3, :]


def _ln_stats(z):
    mu = jnp.mean(z, axis=-1, keepdims=True)
    zc = z - mu
    var = jnp.mean(zc * zc, axis=-1, keepdims=True)
    rstd = lax.rsqrt(var + LN_EPS)
    return zc * rstd, rstd


def _ln_bwd(dxo, xhat, rstd, g):
    dxh = dxo * g
    m1 = jnp.mean(dxh, axis=-1, keepdims=True)
    m2 = jnp.mean(dxh * xhat, axis=-1, keepdims=True)
    return rstd * (dxh - m1 - xhat * m2)


def _colsum(v):
    return jnp.sum(v, axis=0, keepdims=True)


def _sigmoid(v):
    return 0.5 * jnp.tanh(0.5 * v) + 0.5


def _gelu_parts(v):
    k = math.sqrt(2.0 / math.pi)
    u = k * (v + 0.044715 * v * v * v)
    t = jnp.tanh(u)
    g = 0.5 * v * (1.0 + t)
    dg = 0.5 * (1.0 + t) + 0.5 * v * (1.0 - t * t) * k * (1.0 + 3.0 * 0.044715 * v * v)
    return g, dg


def _me():
    return lax.axis_index("x"), lax.axis_index("y"), lax.axis_index("c")


def _idx(p):
    return 4 * p[0] + 2 * p[1] + p[2]


def _peers(me):
    x, y, c = me
    out = []
    for k in range(1, N_DEV):
        out.append((1 - x if k & 4 else x, 1 - y if k & 2 else y, 1 - c if k & 1 else c))
    return out


class _Gather:
    def __init__(self, srcs):
        self.srcs = list(srcs)
        n = len(self.srcs)
        self.out_shape = [jax.ShapeDtypeStruct((N_DEV,) + s.shape, s.dtype) for s in self.srcs]
        self.scratch = [pltpu.SemaphoreType.DMA((n, 7)), pltpu.SemaphoreType.DMA((n, 7)),
                        pltpu.SemaphoreType.DMA((n,))]

    @staticmethod
    def _places():
        x, y, c = me = _me()
        return me, (x, y, 1 - c), [(1 - x, y), (x, 1 - y), (1 - x, 1 - y)]

    @staticmethod
    def _copy(outs, sems, t, k, block, to, src=None):
        slot = outs[t].at[_idx(block)]
        return pltpu.make_async_remote_copy(
            src_ref=slot if src is None else src, dst_ref=slot, send_sem=sems[0].at[t, k],
            recv_sem=sems[1].at[t, k], device_id=to, device_id_type=MESH)

    def _firsts(self, ins, outs, sems):
        me, sibling, chips = self._places()
        out = []
        for t in range(len(ins)):
            out.append(self._copy(outs, sems, t, 0, me, sibling, src=ins[t]))
            out += [self._copy(outs, sems, t, 1 + j, me, (*chip, me[2]), src=ins[t]) for j, chip in enumerate(chips)]
        return out

    def _locals(self, ins, outs, sems):
        me = _me()
        return [pltpu.make_async_copy(ins[t], outs[t].at[_idx(me)], sems[2].at[t]) for t in range(len(ins))]

    def start(self, ins, outs, sems):
        for cp in self._locals(ins, outs, sems) + self._firsts(ins, outs, sems):
            cp.start()

    def mid(self, ins, outs, sems):
        me, sibling, chips = self._places()
        for j, chip in enumerate(chips):
            for t in range(len(ins)):
                self._copy(outs, sems, t, 1 + j, (*chip, me[2]), me).wait_recv()
                self._copy(outs, sems, t, 4 + j, (*chip, me[2]), sibling).start()

    def finish(self, ins, outs, sems):
        me, sibling, chips = self._places()
        for t in range(len(ins)):
            self._copy(outs, sems, t, 0, sibling, me).wait_recv()
            for j, chip in enumerate(chips):
                self._copy(outs, sems, t, 4 + j, (*chip, 1 - me[2]), me).wait_recv()
        for cp in self._firsts(ins, outs, sems):
            cp.wait_send()
        for j, chip in enumerate(chips):
            for t in range(len(ins)):
                self._copy(outs, sems, t, 4 + j, (*chip, me[2]), sibling).wait_send()
        for cp in self._locals(ins, outs, sems):
            cp.wait()


class _AllToAll:
    def __init__(self, srcs):
        self.srcs = list(srcs)
        n = len(self.srcs)
        self.out_shape = [jax.ShapeDtypeStruct(s.shape, s.dtype) for s in self.srcs]
        self.scratch = [pltpu.SemaphoreType.DMA((n, 7)), pltpu.SemaphoreType.DMA((n, 7)),
                        pltpu.SemaphoreType.DMA((n,))]

    def _copies(self, ins, outs, sems):
        me = _me()
        loc, rem = [], []
        for t in range(len(ins)):
            loc.append(pltpu.make_async_copy(ins[t].at[_idx(me)], outs[t].at[_idx(me)], sems[2].at[t]))
            for k, p in enumerate(_peers(me)):
                rem.append(pltpu.make_async_remote_copy(
                    src_ref=ins[t].at[_idx(p)], dst_ref=outs[t].at[_idx(me)], send_sem=sems[0].at[t, k],
                    recv_sem=sems[1].at[t, k], device_id=p, device_id_type=MESH))
        return loc, rem

    def start(self, ins, outs, sems):
        loc, rem = self._copies(ins, outs, sems)
        for cp in loc + rem:
            cp.start()

    def mid(self, ins, outs, sems):
        pass

    def finish(self, ins, outs, sems):
        me = _me()
        for t in range(len(ins)):
            for k, p in enumerate(_peers(me)):
                slot = outs[t].at[_idx(p)]
                pltpu.make_async_remote_copy(
                    src_ref=slot, dst_ref=slot, send_sem=sems[0].at[t, k], recv_sem=sems[1].at[t, k],
                    device_id=p, device_id_type=MESH).wait_recv()
        loc, rem = self._copies(ins, outs, sems)
        for cp in rem:
            cp.wait_send()
        for cp in loc:
            cp.wait()


class _Multi:
    def __init__(self, *exs):
        self.exs = exs
        self.srcs = [s for e in exs for s in e.srcs]
        self.out_shape = [s for e in exs for s in e.out_shape]
        self.scratch = [s for e in exs for s in e.scratch]

    def _each(self, ins, outs, sems):
        i = j = 0
        for e in self.exs:
            n, m = len(e.srcs), len(e.scratch)
            yield e, ins[i:i + n], outs[i:i + n], sems[j:j + m]
            i, j = i + n, j + m

    def start(self, ins, outs, sems):
        for e, a, b, c in self._each(ins, outs, sems):
            e.start(a, b, c)

    def mid(self, ins, outs, sems):
        for e, a, b, c in self._each(ins, outs, sems):
            e.mid(a, b, c)

    def finish(self, ins, outs, sems):
        for e, a, b, c in self._each(ins, outs, sems):
            e.finish(a, b, c)


def _call(body, *, name, grid, in_specs, out_specs, out_shape, args, scratch_shapes=(), rider=None):
    in_specs, out_specs, out_shape = list(in_specs), list(out_specs), list(out_shape)
    scratch_shapes = list(scratch_shapes)
    if rider is None:
        return pl.pallas_call(body, name=name, grid=grid, out_shape=out_shape, in_specs=in_specs,
                              out_specs=out_specs, scratch_shapes=scratch_shapes, compiler_params=_params())(*args)
    nci, nco, ncs, nr = len(in_specs), len(out_shape), len(scratch_shapes), len(rider.srcs)
    nsteps = math.prod(grid)
    assert nsteps >= 2, (name, grid)
    mid = max(1, (7 * nsteps) // 8)

    def full(*refs):
        ci, ri = refs[:nci], refs[nci:nci + nr]
        co, ro = refs[nci + nr:nci + nr + nco], refs[nci + nr + nco:nci + 2 * nr + nco]
        cs, rs = refs[nci + 2 * nr + nco:nci + 2 * nr + nco + ncs], refs[nci + 2 * nr + nco + ncs:]
        step = pl.program_id(0)
        for d in range(1, len(grid)):
            step = step * grid[d] + pl.program_id(d)

        @pl.when(step == 0)
        def _():
            rider.start(ri, ro, rs)

        @pl.when(step == mid)
        def _():
            rider.mid(ri, ro, rs)

        body(*ci, *co, *cs)

        @pl.when(step == nsteps - 1)
        def _():
            rider.finish(ri, ro, rs)

    any_spec = pl.BlockSpec(memory_space=pl.ANY)
    return pl.pallas_call(
        full, name=name, grid=grid, out_shape=out_shape + rider.out_shape,
        in_specs=in_specs + [any_spec] * nr, out_specs=out_specs + [any_spec] * nr,
        scratch_shapes=scratch_shapes + rider.scratch, compiler_params=_params(),
    )(*args, *rider.srcs)


def _a2a_start(srcs, dsts, send_sems, recv_sems, local_sems, me, sem_base=0):
    peers = _peers(me)
    started = []
    for t in range(len(srcs)):
        loc = pltpu.make_async_copy(srcs[t].at[_idx(me)], dsts[t].at[_idx(me)], local_sems.at[sem_base + t])
        loc.start()
        started.append(("local", loc))
        for k, p in enumerate(peers):
            cp = pltpu.make_async_remote_copy(
                src_ref=srcs[t].at[_idx(p)], dst_ref=dsts[t].at[_idx(me)],
                send_sem=send_sems.at[sem_base + t, k], recv_sem=recv_sems.at[sem_base + t, k],
                device_id=p, device_id_type=MESH)
            cp.start()
            started.append(("remote", cp))
    return started


def _a2a_finish(started, dsts, send_sems, recv_sems, me, sem_base=0):
    peers = _peers(me)
    for t in range(len(dsts)):
        for k, p in enumerate(peers):
            slot = dsts[t].at[_idx(p)]
            pltpu.make_async_remote_copy(
                src_ref=slot, dst_ref=slot, send_sem=send_sems.at[sem_base + t, k],
                recv_sem=recv_sems.at[sem_base + t, k], device_id=p, device_id_type=MESH).wait_recv()
    for kind, cp in started:
        if kind == "local":
            cp.wait()
        else:
            cp.wait_send()


def ada_modulation(c8, ada_w, ada_b, ride):
    nr = len(ride.srcs)

    def body(c_ref, w_ref, b_ref, *rest):
        ride_in, (call_ref, modr_ref), ride_out = rest[:nr], rest[nr:nr + 2], rest[nr + 2:2 * nr + 2]
        modp, send_sems, recv_sems, local_sems = rest[2 * nr + 2:2 * nr + 6]
        ride_sems = rest[2 * nr + 6:]
        ride.start(ride_in, ride_out, ride_sems)
        me = _me()
        peers = _peers(me)
        sends = []
        for k, p in enumerate(peers):
            cp = pltpu.make_async_remote_copy(
                src_ref=c_ref, dst_ref=call_ref.at[_idx(me)], send_sem=send_sems.at[0, k],
                recv_sem=recv_sems.at[0, k], device_id=p, device_id_type=MESH)
            cp.start()
            sends.append(cp)
        call_ref[_idx(me)] = c_ref[...]
        for k, p in enumerate(peers):
            slot = call_ref.at[_idx(p)]
            pltpu.make_async_remote_copy(
                src_ref=slot, dst_ref=slot, send_sem=send_sems.at[0, k], recv_sem=recv_sems.at[0, k],
                device_id=p, device_id_type=MESH).wait_recv()
        for cp in sends:
            cp.wait_send()
        cv = call_ref[...].reshape(N_DEV * 8, D_MODEL)
        s = (cv * _sigmoid(cv)).astype(BF16)
        for k in range(4):
            res = _nn(s, w_ref[k].astype(BF16)) + b_ref[k]
            for j in range(N_DEV):
                modp[j, 8 * k:8 * k + 8, :] = res[8 * j:8 * j + 8, :]
        started = _a2a_start([modp], [modr_ref], send_sems, recv_sems, local_sems, me, sem_base=1)
        _a2a_finish(started, [modr_ref], send_sems, recv_sems, me, sem_base=1)
        ride.mid(ride_in, ride_out, ride_sems)
        ride.finish(ride_in, ride_out, ride_sems)

    vm, hbm = pl.BlockSpec(memory_space=pltpu.VMEM), pl.BlockSpec(memory_space=pl.ANY)
    return pl.pallas_call(
        body, name="ada_modulation",
        out_shape=[jax.ShapeDtypeStruct((N_DEV, 8, D_MODEL), F32), jax.ShapeDtypeStruct((N_DEV, 32, CG), F32)]
        + ride.out_shape,
        in_specs=[vm, vm, vm] + [hbm] * nr, out_specs=[vm, vm] + [hbm] * nr,
        scratch_shapes=[pltpu.VMEM((N_DEV, 32, CG), F32), pltpu.SemaphoreType.DMA((2, 7)),
                        pltpu.SemaphoreType.DMA((2, 7)), pltpu.SemaphoreType.DMA((2,))] + ride.scratch,
        compiler_params=_params(),
    )(c8, ada_w, ada_b, *ride.srcs)


def epilogue(gsend, c_t, tail, ride):
    nr = len(ride.srcs)
    rt = tail.shape[1]

    def body(g_ref, ct_ref, t_ref, *rest):
        ride_in, (gw_ref, gb_ref, red_ref), ride_out = rest[:nr], rest[nr:nr + 3], rest[nr + 3:2 * nr + 3]
        grecv, trecv, send_sems, recv_sems, local_sems = rest[2 * nr + 3:2 * nr + 8]
        ride_sems = rest[2 * nr + 8:]
        ride.start(ride_in, ride_out, ride_sems)
        me = _me()
        started = _a2a_start([g_ref, t_ref], [grecv, trecv], send_sems, recv_sems, local_sems, me)
        _a2a_finish(started, [grecv, trecv], send_sems, recv_sems, me)
        acc = trecv[0]
        for j in range(1, N_DEV):
            acc = acc + trecv[j]
        red_ref[...] = acc
        ct = ct_ref[...]
        st = (ct * _sigmoid(ct)).astype(BF16).astype(F32)
        gb = jnp.zeros((8, CG), F32)
        for b in range(N_DEV):
            gb = gb + grecv[b]
        gb_ref[...] = gb
        for k in range(4):
            acc = jnp.zeros((D_MODEL, CG), F32)
            for b in range(N_DEV):
                row = grecv[b, k:k + 1, :].astype(BF16).astype(F32)
                acc = acc + st[:, b:b + 1] * row
            gw_ref[k] = acc
        ride.mid(ride_in, ride_out, ride_sems)
        ride.finish(ride_in, ride_out, ride_sems)

    vm, hbm = pl.BlockSpec(memory_space=pltpu.VMEM), pl.BlockSpec(memory_space=pl.ANY)
    return pl.pallas_call(
        body, name="epilogue",
        out_shape=[jax.ShapeDtypeStruct((4, D_MODEL, CG), F32), jax.ShapeDtypeStruct((8, CG), F32),
                   jax.ShapeDtypeStruct((rt, 128), F32)] + ride.out_shape,
        in_specs=[vm, vm, vm] + [hbm] * nr, out_specs=[vm, vm, vm] + [hbm] * nr,
        scratch_shapes=[pltpu.VMEM((N_DEV, 8, CG), F32), pltpu.VMEM((N_DEV, rt, 128), F32),
                        pltpu.SemaphoreType.DMA((2, 7)), pltpu.SemaphoreType.DMA((2, 7)),
                        pltpu.SemaphoreType.DMA((2,))] + ride.scratch,
        compiler_params=_params(),
    )(gsend, c_t, tail, *ride.srcs)


def _rope(t, cos, s1, s2):
    return t * cos + pltpu.roll(t, 16, 1) * s1 + pltpu.roll(t, HEAD - 16, 1) * s2


def _rope_bwd(d, cos, s1, s2):
    return d * cos + pltpu.roll(d * s1, HEAD - 16, 1) + pltpu.roll(d * s2, 16, 1)


def attn_in_fwd(x, mods, k, win_t, rope, rider=None):
    s = x.shape[0]
    tm = _blk(s, TM_MM)

    def body(x_ref, mod_ref, w_ref, c_ref, s1_ref, s2_ref, q_ref, k_ref, v_ref):
        shift, scale, _ = _mod(mod_ref, k)
        h = (x_ref[...] * (1.0 + scale) + shift).astype(BF16)
        qkv = _nt(h, w_ref[...])
        cos, s1, s2 = c_ref[...], s1_ref[...], s2_ref[...]
        for hh in range(N_Q + N_KV):
            r = _rope(qkv[:, HEAD * hh:HEAD * (hh + 1)], cos, s1, s2).astype(BF16)
            if hh < N_Q:
                q_ref[:, HEAD * hh:HEAD * (hh + 1)] = r
            else:
                k_ref[:, HEAD * (hh - N_Q):HEAD * (hh - N_Q + 1)] = r
        v_ref[...] = qkv[:, HEAD * (N_Q + N_KV):].astype(BF16)

    return _call(
        body, name="attn_in_fwd", grid=(s // tm,),
        out_shape=[jax.ShapeDtypeStruct((s, N_Q * HEAD), BF16), jax.ShapeDtypeStruct((s, N_KV * HEAD), BF16),
                   jax.ShapeDtypeStruct((s, N_KV * HEAD), BF16)],
        in_specs=[_row(tm, D_MODEL), _res(mods.shape), _res(win_t.shape),
                  _row(tm, HEAD), _row(tm, HEAD), _row(tm, HEAD)],
        out_specs=[_row(tm, N_Q * HEAD), _row(tm, N_KV * HEAD), _row(tm, N_KV * HEAD)],
        args=(x, mods, win_t, *rope), rider=rider)


QPAIR = 2


def _kv_specs(nblk):
    w = N_KV * HEAD
    return [pl.BlockSpec((QBLK, w), lambda n: (jnp.maximum(QPAIR * n - 1, 0), 0)),
            pl.BlockSpec((QBLK, w), lambda n: (QPAIR * n, 0)),
            pl.BlockSpec((QBLK, w), lambda n: (QPAIR * n + 1, 0)),
            pl.BlockSpec((QBLK, w), lambda n: (jnp.minimum(QPAIR * n + 2, nblk - 1), 0))]


GROUP = N_Q // N_KV


def _attn_mask(n, s):
    qi = lax.broadcasted_iota(jnp.int32, (GROUP * QBLK, 3 * QBLK), 0) & (QBLK - 1)
    kj = lax.broadcasted_iota(jnp.int32, (GROUP * QBLK, 3 * QBLK), 1)
    rel = kj - QBLK - qi
    kpos = kj + (n - 1) * QBLK
    return (jnp.abs(rel) <= QBLK) & (kpos >= 0) & (kpos < s)


def _stack_heads(ref, qb, kv):
    rows = slice(QBLK * qb, QBLK * (qb + 1))
    return jnp.concatenate([ref[rows, HEAD * (GROUP * kv + j):HEAD * (GROUP * kv + j + 1)] for j in range(GROUP)],
                           axis=0)


def _stack_sinks(sink_ref, kv):
    row = lax.broadcasted_iota(jnp.int32, (GROUP * QBLK, 1), 0)
    out = jnp.full((GROUP * QBLK, 1), sink_ref[0, GROUP * kv + GROUP - 1], F32)
    for j in range(GROUP - 2, -1, -1):
        out = jnp.where(row < QBLK * (j + 1), sink_ref[0, GROUP * kv + j], out)
    return out


def _attn_probs(qh, kh, valid, sink):
    sc = _nt(qh, kh) * (HEAD ** -0.5)
    sc = jnp.where(valid, sc, -1e30)
    m = jnp.maximum(jnp.max(sc, axis=-1, keepdims=True), sink)
    p = jnp.exp(sc - m)
    es = jnp.exp(sink - m)
    denom = jnp.sum(p, axis=-1, keepdims=True) + es
    return p / denom, es / denom


def attn_fwd(q, kk, v, sinks, rider=None):
    s = q.shape[0]
    nblk = s // QBLK

    def body(sink_ref, q_ref, k0, k1, k2, k3, v0, v1, v2, v3, o_ref):
        n = pl.program_id(0)
        kall = jnp.concatenate([k0[...], k1[...], k2[...], k3[...]], axis=0)
        vall = jnp.concatenate([v0[...], v1[...], v2[...], v3[...]], axis=0)
        for qb in range(QPAIR):
            valid = _attn_mask(QPAIR * n + qb, s)
            keys = slice(QBLK * qb, QBLK * (qb + 3))
            for kv in range(N_KV):
                cols = slice(HEAD * kv, HEAD * (kv + 1))
                probs, _ = _attn_probs(_stack_heads(q_ref, qb, kv), kall[keys, cols], valid,
                                       _stack_sinks(sink_ref, kv))
                og = _nn(probs.astype(BF16), vall[keys, cols]).astype(BF16)
                for j in range(GROUP):
                    hq = GROUP * kv + j
                    o_ref[QBLK * qb:QBLK * (qb + 1), HEAD * hq:HEAD * (hq + 1)] = og[QBLK * j:QBLK * (j + 1), :]

    qspec = pl.BlockSpec((QPAIR * QBLK, N_Q * HEAD), lambda n: (n, 0))
    return _call(
        body, name="attn_fwd", grid=(nblk // QPAIR,),
        out_shape=[jax.ShapeDtypeStruct((s, N_Q * HEAD), BF16)],
        in_specs=[pl.BlockSpec(memory_space=pltpu.SMEM), qspec] + _kv_specs(nblk) + _kv_specs(nblk),
        out_specs=[qspec],
        args=(sinks, q, kk, kk, kk, kk, v, v, v, v), rider=rider)


def post_fwd(ypre, w, x, mods, k, lng, lnb, gate_act=None, rider=None):
    s = x.shape[0]
    tm = _blk(s, TM_MM)
    kdim = w.shape[0]
    rnn = gate_act is not None

    def body(*refs):
        if rnn:
            gt_ref, hf_ref, hb_ref, w_ref, x_ref, mod_ref, g_ref, b_ref, xo_ref, y_ref, yp_ref = refs
            act, _ = _gelu_parts(gt_ref[...])
            yp = ((hf_ref[...] + hb_ref[...]) * act).astype(BF16)
            yp_ref[...] = yp
        else:
            yp_ref, w_ref, x_ref, mod_ref, g_ref, b_ref, xo_ref, y_ref = refs
            yp = yp_ref[...]
        _, _, gate = _mod(mod_ref, k)
        y = _nn(yp, w_ref[...])
        y_ref[...] = y
        xhat, _ = _ln_stats(ALPHA * x_ref[...] + (1.0 + gate) * y)
        xo_ref[...] = xhat * g_ref[...] + b_ref[...]

    act_in = list(gate_act) if rnn else [ypre]
    out_shape = [jax.ShapeDtypeStruct((s, D_MODEL), F32), jax.ShapeDtypeStruct((s, D_MODEL), F32)]
    out_specs = [_row(tm, D_MODEL), _row(tm, D_MODEL)]
    if rnn:
        out_shape.append(jax.ShapeDtypeStruct((s, kdim), BF16))
        out_specs.append(_row(tm, kdim))
    return _call(
        body, name="rnn_post_fwd" if rnn else "attn_post_fwd", grid=(s // tm,),
        out_shape=out_shape,
        in_specs=[_row(tm, kdim)] * len(act_in) + [_res(w.shape), _row(tm, D_MODEL), _res(mods.shape),
                                                    _res(lng.shape), _res(lnb.shape)],
        out_specs=out_specs,
        args=(*act_in, w, x, mods, lng, lnb), rider=rider)


def mlp_fwd(x, mods, k, w1_t, w2, lng, lnb, rider=None, last=False):
    s = x.shape[0]
    tm = _blk(s, TM_MLP_FWD)

    def body(x_ref, mod_ref, w1_ref, w2_ref, g_ref, b_ref, *outs):
        xo_ref = None if last else outs[0]
        y_ref, ra_ref, r_ref = outs[-3:]
        xv = x_ref[...]
        shift, scale, gate = _mod(mod_ref, k)
        h = (xv * (1.0 + scale) + shift).astype(BF16)
        y = jnp.zeros((tm, D_MODEL), F32)
        for c in range(D_FF // FF_CHUNK):
            rows = slice(FF_CHUNK * c, FF_CHUNK * (c + 1))
            a = jnp.maximum(_nt(h, w1_ref[rows, :]), 0.0)
            r = (a * a).astype(BF16)
            ra_ref[:, rows] = a.astype(BF16)
            r_ref[:, rows] = r
            y = y + _nn(r, w2_ref[rows, :])
        y_ref[...] = y
        if not last:
            xhat, _ = _ln_stats(ALPHA * xv + (1.0 + gate) * y)
            xo_ref[...] = xhat * g_ref[...] + b_ref[...]

    nf = 1 if last else 2
    return _call(
        body, name="mlp_fwd_last" if last else "mlp_fwd", grid=(s // tm,),
        out_shape=[jax.ShapeDtypeStruct((s, D_MODEL), F32)] * nf + [jax.ShapeDtypeStruct((s, D_FF), BF16)] * 2,
        in_specs=[_row(tm, D_MODEL), _res(mods.shape), _res(w1_t.shape), _res(w2.shape),
                  _res(lng.shape), _res(lnb.shape)],
        out_specs=[_row(tm, D_MODEL)] * nf + [_row(tm, D_FF)] * 2,
        args=(x, mods, w1_t, w2, lng, lnb), rider=rider)


def rnn_in_fwd(x, mods, k, win_t):
    s = x.shape[0]
    tm = _blk(s, TM_MM)

    def body(x_ref, mod_ref, w_ref, xr_ref, gt_ref):
        shift, scale, _ = _mod(mod_ref, k)
        h = (x_ref[...] * (1.0 + scale) + shift).astype(BF16)
        xr_ref[...] = _nt(h, w_ref[0:D_RNN, :])
        gt_ref[...] = _nt(h, w_ref[D_RNN:2 * D_RNN, :])

    return pl.pallas_call(
        body, name="rnn_in_fwd", grid=(s // tm,),
        out_shape=[jax.ShapeDtypeStruct((s, D_RNN), F32)] * 2,
        in_specs=[_row(tm, D_MODEL), _res(mods.shape), _res(win_t.shape)],
        out_specs=[_row(tm, D_RNN)] * 2,
        compiler_params=_params(),
    )(x, mods, win_t)


def _shift_rows(v, k, row):
    n = v.shape[0]
    r = pltpu.roll(v, k % n, 0)
    keep = (row >= k) if k > 0 else (row < n + k)
    return jnp.where(keep, r, 0.0)


def conv_fwd(xr, cw, cb):
    s = xr.shape[0]

    def body(x_ref, w_ref, b_ref, o_ref):
        xv = x_ref[...]
        row = lax.broadcasted_iota(jnp.int32, xv.shape, 0)
        o_ref[...] = (b_ref[...] + w_ref[0:1, :] * _shift_rows(xv, 2, row) + w_ref[1:2, :] * _shift_rows(xv, 1, row)
                      + w_ref[2:3, :] * xv + w_ref[3:4, :] * _shift_rows(xv, -1, row))

    slab = pl.BlockSpec((s, 128), lambda j: (0, j))
    return pl.pallas_call(
        body, name="conv_fwd", grid=(D_RNN // 128,),
        out_shape=jax.ShapeDtypeStruct((s, D_RNN), F32),
        in_specs=[slab, pl.BlockSpec((4, 128), lambda j: (0, j)), pl.BlockSpec((1, 128), lambda j: (0, j))],
        out_specs=slab,
        compiler_params=_params(),
    )(xr, cw, cb)


def conv_bwd(da, db, xr, cw):
    s = xr.shape[0]

    def body(da_ref, db_ref, x_ref, w_ref, dx_ref, dw_ref, dbias_ref):
        d = da_ref[...] + db_ref[...]
        xv = x_ref[...]
        row = lax.broadcasted_iota(jnp.int32, xv.shape, 0)
        dx_ref[...] = (w_ref[0:1, :] * _shift_rows(d, -2, row) + w_ref[1:2, :] * _shift_rows(d, -1, row)
                       + w_ref[2:3, :] * d + w_ref[3:4, :] * _shift_rows(d, 1, row))
        dw_ref[0:1, :] = _colsum(d * _shift_rows(xv, 2, row))
        dw_ref[1:2, :] = _colsum(d * _shift_rows(xv, 1, row))
        dw_ref[2:3, :] = _colsum(d * xv)
        dw_ref[3:4, :] = _colsum(d * _shift_rows(xv, -1, row))
        dbias_ref[...] = _colsum(d)

    slab = pl.BlockSpec((s, 128), lambda j: (0, j))
    return pl.pallas_call(
        body, name="conv_bwd", grid=(D_RNN // 128,),
        out_shape=[jax.ShapeDtypeStruct((s, D_RNN), F32), jax.ShapeDtypeStruct((4, D_RNN), F32),
                   jax.ShapeDtypeStruct((1, D_RNN), F32)],
        in_specs=[slab, slab, slab, pl.BlockSpec((4, 128), lambda j: (0, j))],
        out_specs=[slab, pl.BlockSpec((4, 128), lambda j: (0, j)), pl.BlockSpec((1, 128), lambda j: (0, j))],
        compiler_params=_params(),
    )(da, db, xr, cw)


def _softplus_neg(lam):
    z = -lam
    e = jnp.exp(-jnp.abs(z))
    u = 1.0 + e
    log1p = jnp.where(u == 1.0, e, jnp.log(u) * e / jnp.where(u == 1.0, 1.0, u - 1.0))
    return jnp.maximum(z, 0.0) + log1p, 1.0 / (1.0 + jnp.exp(lam))


def _lru_gates(xv, wa_ref, wx_ref, ba_ref, bx_ref, lam_ref):
    xb = xv.astype(BF16)
    r = _sigmoid(_nn(xb, wa_ref[...]) + ba_ref[...])
    i = _sigmoid(_nn(xb, wx_ref[...]) + bx_ref[...])
    sp, sg = _softplus_neg(lam_ref[...])
    la = r * (-LRU_C * sp)
    a = jnp.exp(la)
    th = jnp.tanh(la)
    m2 = -2.0 * th / (1.0 - th)
    rmult = lax.rsqrt(jnp.maximum(m2, 1e-37))
    return xb, r, i, sp, sg, a, m2 * rmult, rmult


def _scan(a, u, h0, reverse):
    n, c = a.shape
    sub = lax.broadcasted_iota(jnp.int32, (8, c), 0)
    steps = [(8 - sh, sub < 8 - sh) if reverse else (sh, sub >= sh) for sh in (1, 2, 4)]
    out = [None] * (n // 8)
    edge = h0
    for k in (range(n // 8 - 1, -1, -1) if reverse else range(n // 8)):
        at, ut = a[8 * k:8 * k + 8], u[8 * k:8 * k + 8]
        for rot, keep in steps:
            a_s = jnp.where(keep, pltpu.roll(at, rot, 0), 1.0)
            u_s = jnp.where(keep, pltpu.roll(ut, rot, 0), 0.0)
            ut = at * u_s + ut
            at = at * a_s
        hk = ut + at * edge
        out[k] = hk
        edge = hk[0:1] if reverse else hk[7:8]
    return jnp.concatenate(out, axis=0)


def _lru_specs(nt, tt, reverse):
    tmap = (lambda t: nt - 1 - t) if reverse else (lambda t: t)
    blk = pl.BlockSpec((tt, CG), lambda g, t: (tmap(t), g))
    wsp = pl.BlockSpec((None, CG, CG), lambda g, t: (g, 0, 0))
    vec = pl.BlockSpec((1, CG), lambda g, t: (0, g))
    return tmap, blk, wsp, vec


def lru_fwd(xc, wa, wx, ba, bx, lam, reverse):
    s = xc.shape[0]
    tt = _blk(s, TT_RNN)
    sb = _blk(tt, SB_RNN)
    nt = s // tt

    def body(x_ref, wa_ref, wx_ref, ba_ref, bx_ref, lam_ref, hs_ref, carry):
        @pl.when(pl.program_id(1) == 0)
        def _():
            carry[...] = jnp.zeros(carry.shape, F32)

        xv = x_ref[...]
        _, _, i, _, _, a, mult, _ = _lru_gates(xv, wa_ref, wx_ref, ba_ref, bx_ref, lam_ref)
        u = mult * (i * xv)
        h0 = carry[0:1, :]
        order = range(tt // sb - 1, -1, -1) if reverse else range(tt // sb)
        for j in order:
            rows = slice(sb * j, sb * (j + 1))
            h = _scan(a[rows], u[rows], h0, reverse)
            hs_ref[rows, :] = h
            h0 = h[0:1, :] if reverse else h[sb - 1:sb, :]
        carry[0:1, :] = h0

    _, blk, wsp, vec = _lru_specs(nt, tt, reverse)
    return pl.pallas_call(
        body, name="lru_fwd_rev" if reverse else "lru_fwd", grid=(N_CG, nt),
        out_shape=jax.ShapeDtypeStruct((s, D_RNN), F32),
        in_specs=[blk, wsp, wsp, vec, vec, vec], out_specs=blk,
        scratch_shapes=[pltpu.VMEM((8, CG), F32)],
        compiler_params=_params(),
    )(xc, wa, wx, ba, bx, lam)


def lru_bwd(xc, dhs, hs, wa, wx, ba, bx, lam, reverse, rider=None):
    s = xc.shape[0]
    tt = _blk(s, TT_RNN)
    sb = _blk(tt, SB_RNN)
    nt = s // tt
    back = not reverse

    def body(x_ref, dh_ref, hs_ref, nb_ref, wa_ref, wx_ref, ba_ref, bx_ref, lam_ref,
             dx_ref, dwa_ref, dwx_ref, dba_ref, dbx_ref, dlam_ref, carry):
        t = pl.program_id(1)

        @pl.when(t == 0)
        def _():
            carry[...] = jnp.zeros(carry.shape, F32)
            dwa_ref[...] = jnp.zeros(dwa_ref.shape, F32)
            dwx_ref[...] = jnp.zeros(dwx_ref.shape, F32)
            dba_ref[...] = jnp.zeros(dba_ref.shape, F32)
            dbx_ref[...] = jnp.zeros(dbx_ref.shape, F32)
            dlam_ref[...] = jnp.zeros(dlam_ref.shape, F32)

        xv = x_ref[...]
        xb, r, i, sp, sg, a, mult, rmult = _lru_gates(xv, wa_ref, wx_ref, ba_ref, bx_ref, lam_ref)
        row = lax.broadcasted_iota(jnp.int32, xv.shape, 0)
        hsv = hs_ref[...]
        inner = t < nt - 1
        if reverse:
            edge = jnp.where(inner, nb_ref[0:1, :], 0.0)
            hprev = jnp.where(row == tt - 1, edge, pltpu.roll(hsv, tt - 1, 0))
            a_next = jnp.where(row == 0, carry[1:2, :], pltpu.roll(a, 1, 0))
        else:
            edge = jnp.where(inner, nb_ref[7:8, :], 0.0)
            hprev = jnp.where(row == 0, edge, pltpu.roll(hsv, 1, 0))
            a_next = jnp.where(row == tt - 1, carry[1:2, :], pltpu.roll(a, tt - 1, 0))
        dhv = dh_ref[...]
        g0 = carry[0:1, :]
        parts = [None] * (tt // sb)
        order = range(tt // sb - 1, -1, -1) if back else range(tt // sb)
        for j in order:
            rows = slice(sb * j, sb * (j + 1))
            gj = _scan(a_next[rows], dhv[rows], g0, back)
            parts[j] = gj
            g0 = gj[0:1, :] if back else gj[sb - 1:sb, :]
        g = jnp.concatenate(parts, axis=0) if len(parts) > 1 else parts[0]
        carry[0:1, :] = g0
        carry[1:2, :] = a[0:1, :] if back else a[tt - 1:tt, :]

        da = g * hprev
        dmult = g * (i * xv)
        di = g * mult * xv
        dla = da * a - dmult * (a * a) * rmult
        dpa = (dla * (-LRU_C * sp)) * r * (1.0 - r)
        dpx = di * i * (1.0 - i)
        dlam_ref[...] += _colsum(dla * (LRU_C * r * sg))
        dba_ref[...] += _colsum(dpa)
        dbx_ref[...] += _colsum(dpx)
        dpab, dpxb = dpa.astype(BF16), dpx.astype(BF16)
        dx_ref[...] = g * mult * i + _nt(dpab, wa_ref[...]) + _nt(dpxb, wx_ref[...])
        dwa_ref[...] += _tn(xb, dpab)
        dwx_ref[...] += _tn(xb, dpxb)

    tmap, blk, wsp, vec = _lru_specs(nt, tt, back)
    per8 = tt // 8
    if reverse:
        nb = pl.BlockSpec((8, CG), lambda g, t: (jnp.minimum((tmap(t) + 1) * per8, s // 8 - 1), g))
    else:
        nb = pl.BlockSpec((8, CG), lambda g, t: (jnp.maximum(tmap(t) * per8 - 1, 0), g))
    return _call(
        body, name="lru_bwd_rev" if reverse else "lru_bwd", grid=(N_CG, nt),
        out_shape=[jax.ShapeDtypeStruct((s, D_RNN), F32), jax.ShapeDtypeStruct((N_CG, CG, CG), F32),
                   jax.ShapeDtypeStruct((N_CG, CG, CG), F32)] + [jax.ShapeDtypeStruct((1, D_RNN), F32)] * 3,
        in_specs=[blk, blk, blk, nb, wsp, wsp, vec, vec, vec],
        out_specs=[blk, wsp, wsp, vec, vec, vec],
        scratch_shapes=[pltpu.VMEM((8, CG), F32)],
        args=(xc, dhs, hs, hs, wa, wx, ba, bx, lam), rider=rider)


def _ln_part_bwd(dxo, x, y, gate, g, sums_ref, loss_head=None):
    xhat, rstd = _ln_stats(ALPHA * x + (1.0 + gate) * y)
    if loss_head is not None:
        err = xhat * g + loss_head[0] - loss_head[1]
        dxo = err * (1.0 / D_MODEL)
        sums_ref[5:6, :] += _colsum(err * err)
    dz = _ln_bwd(dxo, xhat, rstd, g)
    sums_ref[2:3, :] += _colsum(dz * y)
    sums_ref[3:4, :] += _colsum(dxo * xhat)
    sums_ref[4:5, :] += _colsum(dxo)
    return dz


def mlp_bwd(dxo, x, y, ra, mods, k, w1_t, w2, lng, lnb=None, rider=None):
    s = x.shape[0]
    tm = _blk(s, TM_MLP)
    head = lnb is not None

    def body(d_ref, x_ref, y_ref, ra_ref, mod_ref, w1_ref, w2_ref, g_ref, *rest):
        b_ref = rest[0] if head else None
        dx_ref, da_ref, h_ref, dy_ref, sums_ref = rest[1:] if head else rest

        @pl.when(pl.program_id(0) == 0)
        def _():
            sums_ref[...] = jnp.zeros(sums_ref.shape, F32)

        xv = x_ref[...]
        shift, scale, gate = _mod(mod_ref, k)
        if head:
            dz = _ln_part_bwd(None, xv, y_ref[...], gate, g_ref[...], sums_ref, (b_ref[...], d_ref[...]))
        else:
            dz = _ln_part_bwd(d_ref[...], xv, y_ref[...], gate, g_ref[...], sums_ref)
        dyb = (dz * (1.0 + gate)).astype(BF16)
        dy_ref[...] = dyb
        h = (xv * (1.0 + scale) + shift).astype(BF16)
        h_ref[...] = h
        dh = jnp.zeros((tm, D_MODEL), F32)
        for c in range(D_FF // FF_CHUNK):
            rows = slice(FF_CHUNK * c, FF_CHUNK * (c + 1))
            da = (_nt(dyb, w2_ref[rows, :]) * (2.0 * ra_ref[:, rows].astype(F32))).astype(BF16)
            da_ref[:, rows] = da
            dh = dh + _nn(da, w1_ref[rows, :])
        dx_ref[...] = ALPHA * dz + dh * (1.0 + scale)
        sums_ref[0:1, :] += _colsum(dh)
        sums_ref[1:2, :] += _colsum(dh * xv)

    return _call(
        body, name="mlp_bwd", grid=(s // tm,),
        out_shape=[jax.ShapeDtypeStruct((s, D_MODEL), F32), jax.ShapeDtypeStruct((s, D_FF), BF16),
                   jax.ShapeDtypeStruct((s, D_MODEL), BF16),
                   jax.ShapeDtypeStruct((s, D_MODEL), BF16), jax.ShapeDtypeStruct((8, D_MODEL), F32)],
        in_specs=[_row(tm, D_MODEL)] * 3 + [_row(tm, D_FF), _res(mods.shape), _res(w1_t.shape), _res(w2.shape),
                                             _res(lng.shape)] + ([_res(lnb.shape)] if head else []),
        out_specs=[_row(tm, D_MODEL), _row(tm, D_FF), _row(tm, D_MODEL), _row(tm, D_MODEL), _res((8, D_MODEL))],
        args=(dxo, x, y, ra, mods, w1_t, w2, lng) + ((lnb,) if head else ()), rider=rider)


def post_bwd(dxo, x, y, mods, k, w, lng, gate_act=None, rider=None):
    s = x.shape[0]
    tm = _blk(s, TM_MM)
    kdim = w.shape[0]
    rnn = gate_act is not None

    def body(*refs):
        if rnn:
            (d_ref, x_ref, y_ref, mod_ref, w_ref, g_ref, gt_ref, hf_ref, hb_ref,
             dres_ref, dy_ref, sums_ref, dhs_ref, dgt_ref) = refs
        else:
            d_ref, x_ref, y_ref, mod_ref, w_ref, g_ref, dres_ref, dy_ref, sums_ref, dyp_ref = refs

        @pl.when(pl.program_id(0) == 0)
        def _():
            sums_ref[...] = jnp.zeros(sums_ref.shape, F32)

        _, _, gate = _mod(mod_ref, k)
        dz = _ln_part_bwd(d_ref[...], x_ref[...], y_ref[...], gate, g_ref[...], sums_ref)
        dres_ref[...] = ALPHA * dz
        dyb = (dz * (1.0 + gate)).astype(BF16)
        dy_ref[...] = dyb
        dyp = _nt(dyb, w_ref[...])
        if rnn:
            act, dact = _gelu_parts(gt_ref[...])
            dhs_ref[...] = dyp * act
            dgt_ref[...] = dyp * (hf_ref[...] + hb_ref[...]) * dact
        else:
            dyp_ref[...] = dyp.astype(BF16)

    ins = [dxo, x, y, mods, w, lng] + (list(gate_act) if rnn else [])
    in_specs = [_row(tm, D_MODEL)] * 3 + [_res(mods.shape), _res(w.shape), _res(lng.shape)]
    out_shape = [jax.ShapeDtypeStruct((s, D_MODEL), F32), jax.ShapeDtypeStruct((s, D_MODEL), BF16),
                 jax.ShapeDtypeStruct((8, D_MODEL), F32)]
    out_specs = [_row(tm, D_MODEL), _row(tm, D_MODEL), _res((8, D_MODEL))]
    if rnn:
        in_specs += [_row(tm, kdim)] * 3
        out_shape += [jax.ShapeDtypeStruct((s, kdim), F32)] * 2
        out_specs += [_row(tm, kdim)] * 2
    else:
        out_shape.append(jax.ShapeDtypeStruct((s, kdim), BF16))
        out_specs.append(_row(tm, kdim))
    return _call(
        body, name="rnn_post_bwd" if rnn else "attn_post_bwd", grid=(s // tm,),
        out_shape=out_shape, in_specs=in_specs, out_specs=out_specs, args=ins, rider=rider)


def attn_bwd(q, kk, v, do, sinks, rider=None):
    s = q.shape[0]
    nblk = s // QBLK
    scale = HEAD ** -0.5

    def body(sink_ref, q_ref, do_ref, k0, k1, k2, k3, v0, v1, v2, v3, dq_ref, dk_ref, dv_ref, ds_ref):
        n = pl.program_id(0)

        @pl.when(n == 0)
        def _():
            ds_ref[...] = jnp.zeros(ds_ref.shape, F32)
            dk_ref[...] = jnp.zeros(dk_ref.shape, F32)
            dv_ref[...] = jnp.zeros(dv_ref.shape, F32)

        kall = jnp.concatenate([k0[...], k1[...], k2[...], k3[...]], axis=0)
        vall = jnp.concatenate([v0[...], v1[...], v2[...], v3[...]], axis=0)
        lane = lax.broadcasted_iota(jnp.int32, (1, 128), 1)
        dsink = jnp.zeros((1, 128), F32)
        for qb in range(QPAIR):
            nb = QPAIR * n + qb
            valid = _attn_mask(nb, s)
            keys = slice(QBLK * qb, QBLK * (qb + 3))
            for kv in range(N_KV):
                cols = slice(HEAD * kv, HEAD * (kv + 1))
                qg, dog = _stack_heads(q_ref, qb, kv), _stack_heads(do_ref, qb, kv)
                kh, vh = kall[keys, cols], vall[keys, cols]
                probs, psink = _attn_probs(qg, kh, valid, _stack_sinks(sink_ref, kv))
                dprobs = _nt(dog, vh)
                dvp = _tn(probs.astype(BF16), dog)
                rowdot = jnp.sum(probs * dprobs, axis=-1, keepdims=True)
                dsb = (probs * (dprobs - rowdot) * scale).astype(BF16)
                dqg = _nn(dsb, kh)
                dkp = _tn(dsb, qg)
                for p in range(3):
                    blk = jnp.clip(nb - 1 + p, 0, nblk - 1)
                    rows = pl.ds(pl.multiple_of(blk * QBLK, QBLK), QBLK)
                    dk_ref[rows, cols] += dkp[QBLK * p:QBLK * (p + 1), :]
                    dv_ref[rows, cols] += dvp[QBLK * p:QBLK * (p + 1), :]
                dsk = -psink * rowdot
                for j in range(GROUP):
                    hq = GROUP * kv + j
                    dq_ref[QBLK * qb:QBLK * (qb + 1), HEAD * hq:HEAD * (hq + 1)] = dqg[QBLK * j:QBLK * (j + 1), :]
                    dsink = dsink + jnp.where(lane == hq, _colsum(dsk[QBLK * j:QBLK * (j + 1), :]), 0.0)
        ds_ref[...] += dsink

    qspec = pl.BlockSpec((QPAIR * QBLK, N_Q * HEAD), lambda n: (n, 0))
    return _call(
        body, name="attn_bwd", grid=(nblk // QPAIR,),
        out_shape=[jax.ShapeDtypeStruct((s, N_Q * HEAD), F32),
                   jax.ShapeDtypeStruct((s, N_KV * HEAD), F32), jax.ShapeDtypeStruct((s, N_KV * HEAD), F32),
                   jax.ShapeDtypeStruct((1, 128), F32)],
        in_specs=[pl.BlockSpec(memory_space=pltpu.SMEM), qspec, qspec] + _kv_specs(nblk) + _kv_specs(nblk),
        out_specs=[qspec, _res((s, N_KV * HEAD)), _res((s, N_KV * HEAD)), pl.BlockSpec((1, 128), lambda n: (0, 0))],
        args=(sinks, q, do, kk, kk, kk, kk, v, v, v, v), rider=rider)


def _in_bwd_tail(dzb, w_ref, x_ref, mod_ref, k, dres_ref, dx_ref, h_ref, sums_ref):
    xv = x_ref[...]
    shift, scale, _ = _mod(mod_ref, k)
    h_ref[...] = (xv * (1.0 + scale) + shift).astype(BF16)
    dh = _nn(dzb, w_ref[...])
    dx_ref[...] = dres_ref[...] + dh * (1.0 + scale)
    sums_ref[0:1, :] += _colsum(dh)
    sums_ref[1:2, :] += _colsum(dh * xv)


def attn_in_bwd(dq, dk, dv, rope, x, mods, k, win_t, dres):
    s = x.shape[0]
    tm = _blk(s, TM_MM)

    def body(dq_ref, dk_ref, dv_ref, c_ref, s1_ref, s2_ref, x_ref, mod_ref, w_ref, dres_ref,
             dx_ref, dz_ref, h_ref, sums_ref):
        @pl.when(pl.program_id(0) == 0)
        def _():
            sums_ref[...] = jnp.zeros(sums_ref.shape, F32)

        cos, s1, s2 = c_ref[...], s1_ref[...], s2_ref[...]
        for hh in range(N_Q + N_KV):
            src = dq_ref[:, HEAD * hh:HEAD * (hh + 1)] if hh < N_Q else dk_ref[:, HEAD * (hh - N_Q):HEAD * (hh - N_Q + 1)]
            dz_ref[:, HEAD * hh:HEAD * (hh + 1)] = _rope_bwd(src, cos, s1, s2).astype(BF16)
        dz_ref[:, HEAD * (N_Q + N_KV):] = dv_ref[...].astype(BF16)
        _in_bwd_tail(dz_ref[...], w_ref, x_ref, mod_ref, k, dres_ref, dx_ref, h_ref, sums_ref)

    return pl.pallas_call(
        body, name="attn_in_bwd", grid=(s // tm,),
        out_shape=[jax.ShapeDtypeStruct((s, D_MODEL), F32), jax.ShapeDtypeStruct((s, D_QKV), BF16),
                   jax.ShapeDtypeStruct((s, D_MODEL), BF16), jax.ShapeDtypeStruct((8, D_MODEL), F32)],
        in_specs=[_row(tm, N_Q * HEAD), _row(tm, N_KV * HEAD), _row(tm, N_KV * HEAD),
                  _row(tm, HEAD), _row(tm, HEAD), _row(tm, HEAD), _row(tm, D_MODEL),
                  _res(mods.shape), _res(win_t.shape), _row(tm, D_MODEL)],
        out_specs=[_row(tm, D_MODEL), _row(tm, D_QKV), _row(tm, D_MODEL), _res((8, D_MODEL))],
        compiler_params=_params(),
    )(dq, dk, dv, *rope, x, mods, win_t, dres)


def rnn_in_bwd(dxr, dgt, x, mods, k, win_t, dres):
    s = x.shape[0]
    tm = _blk(s, TM_MM)

    def body(dxr_ref, dgt_ref, x_ref, mod_ref, w_ref, dres_ref, dx_ref, dz_ref, h_ref, sums_ref):
        @pl.when(pl.program_id(0) == 0)
        def _():
            sums_ref[...] = jnp.zeros(sums_ref.shape, F32)

        dz_ref[:, 0:D_RNN] = dxr_ref[...].astype(BF16)
        dz_ref[:, D_RNN:2 * D_RNN] = dgt_ref[...].astype(BF16)
        _in_bwd_tail(dz_ref[...], w_ref, x_ref, mod_ref, k, dres_ref, dx_ref, h_ref, sums_ref)

    return pl.pallas_call(
        body, name="rnn_in_bwd", grid=(s // tm,),
        out_shape=[jax.ShapeDtypeStruct((s, D_MODEL), F32), jax.ShapeDtypeStruct((s, 2 * D_RNN), BF16),
                   jax.ShapeDtypeStruct((s, D_MODEL), BF16), jax.ShapeDtypeStruct((8, D_MODEL), F32)],
        in_specs=[_row(tm, D_RNN), _row(tm, D_RNN), _row(tm, D_MODEL), _res(mods.shape), _res(win_t.shape),
                  _row(tm, D_MODEL)],
        out_specs=[_row(tm, D_MODEL), _row(tm, 2 * D_RNN), _row(tm, D_MODEL), _res((8, D_MODEL))],
        compiler_params=_params(),
    )(dxr, dgt, x, mods, win_t, dres)


def wgrad(a, b, name, rider=None):
    s, m = a.shape
    n = b.shape[1]
    tm = next(t for t in (1024, 768, 512, 384, 256, 128) if m % t == 0)
    tk = _blk(s, TK_WG)
    nk = s // tk

    def body(a_ref, b_ref, o_ref, acc):
        kk = pl.program_id(1)

        @pl.when(kk == 0)
        def _():
            acc[...] = jnp.zeros(acc.shape, F32)

        acc[...] += _tn(a_ref[...], b_ref[...])

        @pl.when(kk == nk - 1)
        def _():
            o_ref[...] = acc[...].astype(BF16)

    out, *rode = _call(
        body, name=name, grid=(m // tm, nk),
        out_shape=[jax.ShapeDtypeStruct((m, n), BF16)],
        in_specs=[pl.BlockSpec((tk, tm), lambda i, kk: (kk, i)), pl.BlockSpec((tk, n), lambda i, kk: (kk, 0))],
        out_specs=[pl.BlockSpec((tm, n), lambda i, kk: (i, 0))],
        scratch_shapes=[pltpu.VMEM((tm, n), F32)],
        args=(a, b), rider=rider)
    out = out.reshape(N_DEV, m // N_DEV, n)
    return (out, *rode) if rider is not None else out


def part_sum(parts, name):
    _, r, c = parts.shape
    tr = next(t for t in (256, 192, 128, 64, 32, 16, 8) if r % t == 0)

    def body(p_ref, o_ref):
        acc = p_ref[0].astype(F32)
        for j in range(1, N_DEV):
            acc = acc + p_ref[j].astype(F32)
        o_ref[...] = acc

    return pl.pallas_call(
        body, name=name, grid=(r // tr,),
        out_shape=jax.ShapeDtypeStruct((r, c), F32),
        in_specs=[pl.BlockSpec((N_DEV, tr, c), lambda i: (0, i, 0))],
        out_specs=pl.BlockSpec((tr, c), lambda i: (i, 0)),
        compiler_params=_params(),
    )(parts)


def adamw(w, g, m, v, name):
    shape = w.shape
    c = shape[-1]
    r = w.size // c
    w2, g2, m2, v2 = (t.reshape(r, c) for t in (w, g, m, v))
    tr = r if r * c <= 512 * 1024 else next(t for t in (512, 256, 128, 64, 32, 16, 8) if r % t == 0)

    def body(w_ref, g_ref, m_ref, v_ref, d_ref, nm_ref, nv_ref):
        gv = g_ref[...]
        nm = B1 * m_ref[...] + (1.0 - B1) * gv
        nv = B2 * v_ref[...] + (1.0 - B2) * (gv * gv)
        nm_ref[...] = nm
        nv_ref[...] = nv
        m_hat = nm / (1.0 - B1 ** STEP)
        v_hat = nv / (1.0 - B2 ** STEP)
        d_ref[...] = -LR * (m_hat / (jnp.sqrt(v_hat) + ADAM_EPS) + WD * w_ref[...])

    spec = pl.BlockSpec((tr, c), lambda i: (i, 0))
    outs = pl.pallas_call(
        body, name=name, grid=(r // tr,),
        out_shape=[jax.ShapeDtypeStruct((r, c), F32)] * 3,
        in_specs=[spec] * 4, out_specs=[spec] * 3,
        compiler_params=_params(),
    )(w2, g2, m2, v2)
    return tuple(o.reshape(shape) for o in outs)


def _rope_tables(s):
    half = ROT // 2
    inv_freq = THETA ** (-jnp.arange(0, ROT, 2, dtype=F32) / ROT)
    per_row = 128 // half
    pos = (per_row * jnp.arange(s // per_row)[:, None] + jnp.arange(128)[None, :] // half).astype(F32)
    ang = pos * jnp.tile(inv_freq, per_row)[None, :]
    cos, sin = lax.optimization_barrier((jnp.cos(ang), jnp.sin(ang)))
    cos, sin = cos.reshape(s, half), sin.reshape(s, half)
    zeros = jnp.zeros((s, HEAD - ROT), F32)
    c = jnp.concatenate([cos, cos, jnp.ones((s, HEAD - ROT), F32)], axis=1)
    s1 = jnp.concatenate([jnp.zeros((s, half), F32), sin, zeros], axis=1)
    s2 = jnp.concatenate([-sin, jnp.zeros((s, half), F32), zeros], axis=1)
    return c, s1, s2


def _blockdiag(w):
    w4 = w.reshape(N_CG, 4, RB_W, RB_W)
    eye = jnp.eye(4, dtype=w.dtype)
    return jnp.einsum("gipq,ij->gipjq", w4, eye).reshape(N_CG, CG, CG)


def _diag_blocks(w):
    w5 = w.reshape(N_CG, 4, RB_W, 4, RB_W)
    eye = jnp.eye(4, dtype=w.dtype)
    return jnp.einsum("gipjq,ij->gipq", w5, eye).reshape(N_RB, RB_W, RB_W)


def _cols(full, per):
    lead = full.shape[:-1]
    t = full.reshape(lead + (N_DEV, per))
    return jnp.moveaxis(t, -2, 0).reshape(N_DEV, -1)


def kernel(x, c, ada_w, ada_b, ln_g, ln_b, attn_w_in, attn_w_out, attn_sinks, rnn_w_in, rnn_conv_w, rnn_conv_b, rnn_w_a, rnn_b_a, rnn_w_x, rnn_b_x, rnn_lam, rnn_w_out, mlp_w1, mlp_w2, loss_target, m_ada_w, m_ada_b, m_ln_g, m_ln_b, m_attn_w_in, m_attn_w_out, m_attn_sinks, m_rnn_w_in, m_rnn_conv_w, m_rnn_conv_b, m_rnn_w_a, m_rnn_b_a, m_rnn_w_x, m_rnn_b_x, m_rnn_lam, m_rnn_w_out, m_mlp_w1, m_mlp_w2, v_ada_w, v_ada_b, v_ln_g, v_ln_b, v_attn_w_in, v_attn_w_out, v_attn_sinks, v_rnn_w_in, v_rnn_conv_w, v_rnn_conv_b, v_rnn_w_a, v_rnn_b_a, v_rnn_w_x, v_rnn_b_x, v_rnn_lam, v_rnn_w_out, v_mlp_w1, v_mlp_w2):
    s = x.shape[1]
    x0 = x.reshape(s, D_MODEL)
    target = loss_target.reshape(s, D_MODEL)
    weights = dict(ada_w=ada_w, ada_b=ada_b, ln_g=ln_g, ln_b=ln_b, attn_w_in=attn_w_in, attn_w_out=attn_w_out,
                   attn_sinks=attn_sinks, rnn_w_in=rnn_w_in, rnn_conv_w=rnn_conv_w, rnn_conv_b=rnn_conv_b,
                   rnn_w_a=rnn_w_a, rnn_b_a=rnn_b_a, rnn_w_x=rnn_w_x, rnn_b_x=rnn_b_x, rnn_lam=rnn_lam,
                   rnn_w_out=rnn_w_out, mlp_w1=mlp_w1, mlp_w2=mlp_w2)
    moments_m = dict(ada_w=m_ada_w, ada_b=m_ada_b, ln_g=m_ln_g, ln_b=m_ln_b, attn_w_in=m_attn_w_in,
                     attn_w_out=m_attn_w_out, attn_sinks=m_attn_sinks, rnn_w_in=m_rnn_w_in,
                     rnn_conv_w=m_rnn_conv_w, rnn_conv_b=m_rnn_conv_b, rnn_w_a=m_rnn_w_a, rnn_b_a=m_rnn_b_a,
                     rnn_w_x=m_rnn_w_x, rnn_b_x=m_rnn_b_x, rnn_lam=m_rnn_lam, rnn_w_out=m_rnn_w_out,
                     mlp_w1=m_mlp_w1, mlp_w2=m_mlp_w2)
    moments_v = dict(ada_w=v_ada_w, ada_b=v_ada_b, ln_g=v_ln_g, ln_b=v_ln_b, attn_w_in=v_attn_w_in,
                     attn_w_out=v_attn_w_out, attn_sinks=v_attn_sinks, rnn_w_in=v_rnn_w_in,
                     rnn_conv_w=v_rnn_conv_w, rnn_conv_b=v_rnn_conv_b, rnn_w_a=v_rnn_w_a, rnn_b_a=v_rnn_b_a,
                     rnn_w_x=v_rnn_w_x, rnn_b_x=v_rnn_b_x, rnn_lam=v_rnn_lam, rnn_w_out=v_rnn_w_out,
                     mlp_w1=v_mlp_w1, mlp_w2=v_mlp_w2)
    names = list(weights)

    def t16(w):
        return w.T.astype(BF16)

    big = [t16(attn_w_in[0]), attn_w_out[0].astype(BF16), t16(rnn_w_in[0]), rnn_w_out[0].astype(BF16),
           t16(mlp_w1[0]), mlp_w2[0].astype(BF16), t16(mlp_w1[1]), mlp_w2[1].astype(BF16)]
    small_local = jnp.concatenate([
        ln_g.reshape(-1), ln_b.reshape(-1), rnn_conv_w.reshape(-1), rnn_conv_b.reshape(-1),
        rnn_b_a.reshape(-1), rnn_b_x.reshape(-1), rnn_lam.reshape(-1)])
    small_local = jnp.pad(small_local, (0, 4096 - small_local.shape[0])).reshape(32, 128)
    flat = lambda g: g.reshape(N_DEV * g.shape[1], D_MODEL)
    c_all, modr, win_t, sm = ada_modulation(jnp.broadcast_to(c, (8, D_MODEL)), ada_w.reshape(4, D_MODEL, CG),
                                            ada_b.reshape(4, 1, CG), _Gather([big[0], small_local]))
    win_t = flat(win_t)
    sm = sm.reshape(N_DEV, 4096)

    def full_vec(off, rows, per):
        piece = sm[:, off:off + rows * per].reshape(N_DEV, rows, per)
        return jnp.moveaxis(piece, 0, 1).reshape(rows, N_DEV * per)

    lng_f, lnb_f = full_vec(0, 4, 128), full_vec(512, 4, 128)
    cw_f, cb_f = full_vec(1024, 4, 192), full_vec(1792, 1, 192)
    ba_f, bx_f, lam_f = full_vec(1984, 2, 192), full_vec(2368, 2, 192), full_vec(2752, 2, 192)
    wa_bd = [_blockdiag(rnn_w_a[0, d]).astype(BF16) for d in range(2)]
    wx_bd = [_blockdiag(rnn_w_x[0, d]).astype(BF16) for d in range(2)]

    mods = modr.reshape(N_DEV, 4, 8, CG)[:, :, 0, :]
    mods = jnp.moveaxis(mods, 0, 1).reshape(4, 3, D_MODEL).reshape(12, D_MODEL)
    rope = _rope_tables(s)
    ln = lambda k: (lng_f[k:k + 1], lnb_f[k:k + 1])

    q, kk, v, wout = attn_in_fwd(x0, mods, 0, win_t, rope, rider=_Gather([big[1]]))
    wout = flat(wout)
    o, *got = attn_fwd(q, kk, v, attn_sinks, rider=_Gather([big[4], big[5]]))
    w1t_0, w2_0 = (flat(g) for g in got)
    x1, y0, rout = post_fwd(o, wout, x0, mods, 0, *ln(0), rider=_Gather([big[3]]))
    rout = flat(rout)
    x2, y1, ra0, r0, *got = mlp_fwd(x1, mods, 1, w1t_0, w2_0, *ln(1), rider=_Gather([big[2], big[6], big[7]]))
    rin_t, w1t_1, w2_1 = (flat(g) for g in got)
    xr, gt = rnn_in_fwd(x2, mods, 2, rin_t)
    xc = conv_fwd(xr, cw_f, cb_f)
    hf = lru_fwd(xc, wa_bd[0], wx_bd[0], ba_f[0:1], bx_f[0:1], lam_f[0:1], False)
    hb = lru_fwd(xc, wa_bd[1], wx_bd[1], ba_f[1:2], bx_f[1:2], lam_f[1:2], True)
    x3, y2, ypre = post_fwd(None, rout, x2, mods, 2, *ln(2), gate_act=(gt, hf, hb))
    y3, ra1, r1 = mlp_fwd(x3, mods, 3, w1t_1, w2_1, *ln(3), last=True)

    dx3, da1, h3, dy3, sums3 = mlp_bwd(target, x3, y3, ra1, mods, 3, w1t_1, w2_1, lng_f[3:4], lnb=lnb_f[3:4])
    g_w1t_1 = wgrad(da1, h3, "wgrad_w1_1")
    g_w2_1 = wgrad(r1, dy3, "wgrad_w2_1")
    dres2, dy2, sums2a, dhs, dgt, p_w1t_1 = post_bwd(dx3, x2, y2, mods, 2, rout, lng_f[2:3], gate_act=(gt, hf, hb),
                                                     rider=_AllToAll([g_w1t_1]))
    g_rout = wgrad(ypre, dy2, "wgrad_rnn_out")
    dxc_f, dwa_f, dwx_f, dba_f, dbx_f, dlam_f, p_w2_1, p_rout = lru_bwd(
        xc, dhs, hf, wa_bd[0], wx_bd[0], ba_f[0:1], bx_f[0:1], lam_f[0:1], False, rider=_AllToAll([g_w2_1, g_rout]))
    dxc_b, dwa_b, dwx_b, dba_b, dbx_b, dlam_b = lru_bwd(xc, dhs, hb, wa_bd[1], wx_bd[1], ba_f[1:2], bx_f[1:2],
                                                        lam_f[1:2], True)
    dxr, dcw, dcb = conv_bwd(dxc_f, dxc_b, xr, cw_f)
    dx2, dzz, h2, sums2b = rnn_in_bwd(dxr, dgt, x2, mods, 2, rin_t, dres2)
    g_rin_t = wgrad(dzz, h2, "wgrad_rnn_in")
    d_wa = jnp.stack([_diag_blocks(dwa_f), _diag_blocks(dwa_b)])
    d_wx = jnp.stack([_diag_blocks(dwx_f), _diag_blocks(dwx_b)])
    nflat = d_wa.size // N_DEV
    gates = jnp.concatenate([d_wa.reshape(N_DEV, nflat), d_wx.reshape(N_DEV, nflat)], axis=1)
    gates = gates.reshape(N_DEV, 2 * nflat // 128, 128)
    dx1, da0, h1, dy1, sums1, p_rin_t, p_gates = mlp_bwd(dx2, x1, y1, ra0, mods, 1, w1t_0, w2_0, lng_f[1:2],
                                                         rider=_AllToAll([g_rin_t, gates]))
    gates_sum = part_sum(p_gates, "part_sum_gates")
    g_w1t_0 = wgrad(da0, h1, "wgrad_w1_0")
    g_w2_0, p_w1t_0 = wgrad(r0, dy1, "wgrad_w2_0", rider=_AllToAll([g_w1t_0]))
    dres0, dy0, sums0a, do = post_bwd(dx1, x0, y0, mods, 0, wout, lng_f[0:1])
    g_wout = wgrad(o, dy0, "wgrad_attn_out")
    dq, dk, dv, dsink, wag, p_w2_0, p_wout = attn_bwd(
        q, kk, v, do, attn_sinks, rider=_Multi(_Gather([gates_sum]), _AllToAll([g_w2_0, g_wout])))
    dx0, dqkv, h0, sums0b = attn_in_bwd(dq, dk, dv, rope, x0, mods, 0, win_t, dres0)
    g_win_t = wgrad(dqkv, h0, "wgrad_attn_in")

    sums = [sums0a + sums0b, sums1, sums2a + sums2b, sums3]
    gmod = jnp.stack([t[0:3] for t in sums])
    gsend = jnp.moveaxis(gmod.reshape(4, N_DEV, CG), 1, 0)
    gsend = jnp.pad(gsend, ((0, 0), (0, 4), (0, 0)))
    c_t = c_all[:, 0, :].T
    sq_err = jnp.sum(sums3[5]).reshape(1, 1)
    tail = jnp.concatenate([
        _cols(dcw, 192), _cols(dcb, 192),
        _cols(jnp.concatenate([dba_f, dba_b]), 192), _cols(jnp.concatenate([dbx_f, dbx_b]), 192),
        _cols(jnp.concatenate([dlam_f, dlam_b]), 192),
        _cols(jnp.stack([t[3] for t in sums]), 128), _cols(jnp.stack([t[4] for t in sums]), 128),
        jnp.broadcast_to(dsink[:, 0:8], (N_DEV, 8)), jnp.broadcast_to(sq_err, (N_DEV, 1))], axis=1)
    tail = jnp.pad(tail, ((0, 0), (0, 32 * 128 - tail.shape[1]))).reshape(N_DEV, 32, 128)
    g_ada_w, g_ada_b, red, p_win_t = epilogue(gsend, c_t, tail, _AllToAll([g_win_t]))
    grads = {"ada_w": g_ada_w.reshape(ada_w.shape), "ada_b": g_ada_b[0:4].reshape(ada_b.shape)}

    big_parts = [p_win_t, p_wout, p_rin_t, p_rout, p_w1t_0, p_w2_0, p_w1t_1, p_w2_1]
    gsum = [part_sum(p, "part_sum_%d" % i) for i, p in enumerate(big_parts)]
    grads.update({
        "attn_w_in": gsum[0].T[None], "attn_w_out": gsum[1][None],
        "rnn_w_in": gsum[2].T[None], "rnn_w_out": gsum[3][None],
        "mlp_w1": jnp.stack([gsum[4].T, gsum[6].T]), "mlp_w2": jnp.stack([gsum[5], gsum[7]]),
    })
    wag = wag.reshape(N_DEV, 2 * nflat)
    grads["rnn_w_a"] = wag[:, :nflat].reshape(rnn_w_a.shape)
    grads["rnn_w_x"] = wag[:, nflat:].reshape(rnn_w_x.shape)
    tl = red.reshape(-1)
    loss = 0.5 * tl[3144] / D_MODEL
    grads["rnn_conv_w"] = tl[0:768].reshape(rnn_conv_w.shape)
    grads["rnn_conv_b"] = tl[768:960].reshape(rnn_conv_b.shape)
    grads["rnn_b_a"] = tl[960:1344].reshape(rnn_b_a.shape)
    grads["rnn_b_x"] = tl[1344:1728].reshape(rnn_b_x.shape)
    grads["rnn_lam"] = tl[1728:2112].reshape(rnn_lam.shape)
    grads["ln_g"] = tl[2112:2624].reshape(ln_g.shape)
    grads["ln_b"] = tl[2624:3136].reshape(ln_b.shape)
    grads["attn_sinks"] = tl[3136:3144].reshape(attn_sinks.shape)

    delta, new_m, new_v = {}, {}, {}
    for n in names:
        delta[n], new_m[n], new_v[n] = adamw(weights[n], grads[n], moments_m[n], moments_v[n], "adamw_" + n)
    return (loss, dx0.reshape(x.shape), *[grads[n] for n in names], *[delta[n] for n in names],
            *[new_m[n] for n in names], *[new_v[n] for n in names])
```

```python
import functools
import math

import jax
import jax.numpy as jnp
from jax import lax
from jax.experimental import pallas as pl
from jax.experimental.pallas import tpu as pltpu

F32, BF16 = jnp.float32, jnp.bfloat16
MESH = pl.DeviceIdType.MESH

D_MODEL = 1024
N_Q, N_KV, HEAD = 8, 2, 128
ROT, THETA = 32, 500000.0
QBLK = 128
D_QKV = (N_Q + 2 * N_KV) * HEAD
D_RNN, N_RB, RB_W = 1536, 16, 96
CG = 384
N_CG = D_RNN // CG
D_FF = 4096
FF_CHUNK = 1024
DEPTH = 2
ALPHA = (2.0 * DEPTH) ** 0.25
LN_EPS = 1e-5
LRU_C = 8.0
N_DEV = 8
LR, B1, B2, ADAM_EPS, WD, STEP = 0.001, 0.9, 0.999, 1e-8, 0.01, 10

VMEM_LIMIT = 56 * 1024 * 1024
TM_MM = 512
TM_MLP = 256
TM_MLP_FWD = 512
TT_RNN = 512
SB_RNN = 512
TK_WG = 2048


def _nn(a, b):
    return jnp.dot(a, b, preferred_element_type=F32)


def _nt(a, b):
    return lax.dot_general(a, b, (((1,), (1,)), ((), ())), preferred_element_type=F32)


def _tn(a, b):
    return lax.dot_general(a, b, (((0,), (0,)), ((), ())), preferred_element_type=F32)


def _blk(n, pref):
    t = min(n, pref)
    assert n % t == 0, (n, pref)
    return t


def _params(**kw):
    return pltpu.CompilerParams(vmem_limit_bytes=VMEM_LIMIT, **kw)


def _row(tm, w):
    return pl.BlockSpec((tm, w), lambda i: (i, 0))


def _res(shape):
    return pl.BlockSpec(shape, lambda i: (0,) * len(shape), pipeline_mode=pl.Buffered(1))


def _mod(mod_ref, k):
    return mod_ref[3 * k:3 * k + 1, :], mod_ref[3 * k + 1:3 * k + 2, :], mod_ref[3 * k + 2:3 * k + 3, :]


def _ln_stats(z):
    mu = jnp.mean(z, axis=-1, keepdims=True)
    zc = z - mu
    var = jnp.mean(zc * zc, axis=-1, keepdims=True)
    rstd = lax.rsqrt(var + LN_EPS)
    return zc * rstd, rstd


def _ln_bwd(dxo, xhat, rstd, g):
    dxh = dxo * g
    m1 = jnp.mean(dxh, axis=-1, keepdims=True)
    m2 = jnp.mean(dxh * xhat, axis=-1, keepdims=True)
    return rstd * (dxh - m1 - xhat * m2)


def _colsum(v):
    return jnp.sum(v, axis=0, keepdims=True)


def _sigmoid(v):
    return 0.5 * jnp.tanh(0.5 * v) + 0.5


def _gelu_parts(v):
    k = math.sqrt(2.0 / math.pi)
    u = k * (v + 0.044715 * v * v * v)
    t = jnp.tanh(u)
    g = 0.5 * v * (1.0 + t)
    dg = 0.5 * (1.0 + t) + 0.5 * v * (1.0 - t * t) * k * (1.0 + 3.0 * 0.044715 * v * v)
    return g, dg


def _me():
    return lax.axis_index("x"), lax.axis_index("y"), lax.axis_index("c")


def _idx(p):
    return 4 * p[0] + 2 * p[1] + p[2]


def _peers(me):
    x, y, c = me
    out = []
    for k in range(1, N_DEV):
        out.append((1 - x if k & 4 else x, 1 - y if k & 2 else y, 1 - c if k & 1 else c))
    return out


class _Gather:
    def __init__(self, srcs):
        self.srcs = list(srcs)
        n = len(self.srcs)
        self.out_shape = [jax.ShapeDtypeStruct((N_DEV,) + s.shape, s.dtype) for s in self.srcs]
        self.scratch = [pltpu.SemaphoreType.DMA((n, 7)), pltpu.SemaphoreType.DMA((n, 7)),
                        pltpu.SemaphoreType.DMA((n,))]

    @staticmethod
    def _places():
        x, y, c = me = _me()
        return me, (x, y, 1 - c), [(1 - x, y), (x, 1 - y), (1 - x, 1 - y)]

    @staticmethod
    def _copy(outs, sems, t, k, block, to, src=None):
        slot = outs[t].at[_idx(block)]
        return pltpu.make_async_remote_copy(
            src_ref=slot if src is None else src, dst_ref=slot, send_sem=sems[0].at[t, k],
            recv_sem=sems[1].at[t, k], device_id=to, device_id_type=MESH)

    def _firsts(self, ins, outs, sems):
        me, sibling, chips = self._places()
        out = []
        for t in range(len(ins)):
            out.append(self._copy(outs, sems, t, 0, me, sibling, src=ins[t]))
            out += [self._copy(outs, sems, t, 1 + j, me, (*chip, me[2]), src=ins[t]) for j, chip in enumerate(chips)]
        return out

    def _locals(self, ins, outs, sems):
        me = _me()
        return [pltpu.make_async_copy(ins[t], outs[t].at[_idx(me)], sems[2].at[t]) for t in range(len(ins))]

    def start(self, ins, outs, sems):
        for cp in self._locals(ins, outs, sems) + self._firsts(ins, outs, sems):
            cp.start()

    def mid(self, ins, outs, sems):
        me, sibling, chips = self._places()
        for j, chip in enumerate(chips):
            for t in range(len(ins)):
                self._copy(outs, sems, t, 1 + j, (*chip, me[2]), me).wait_recv()
                self._copy(outs, sems, t, 4 + j, (*chip, me[2]), sibling).start()

    def finish(self, ins, outs, sems):
        me, sibling, chips = self._places()
        for t in range(len(ins)):
            self._copy(outs, sems, t, 0, sibling, me).wait_recv()
            for j, chip in enumerate(chips):
                self._copy(outs, sems, t, 4 + j, (*chip, 1 - me[2]), me).wait_recv()
        for cp in self._firsts(ins, outs, sems):
            cp.wait_send()
        for j, chip in enumerate(chips):
            for t in range(len(ins)):
                self._copy(outs, sems, t, 4 + j, (*chip, me[2]), sibling).wait_send()
        for cp in self._locals(ins, outs, sems):
            cp.wait()


class _AllToAll:
    def __init__(self, srcs):
        self.srcs = list(srcs)
        n = len(self.srcs)
        self.out_shape = [jax.ShapeDtypeStruct(s.shape, s.dtype) for s in self.srcs]
        self.scratch = [pltpu.SemaphoreType.DMA((n, 7)), pltpu.SemaphoreType.DMA((n, 7)),
                        pltpu.SemaphoreType.DMA((n,))]

    def _copies(self, ins, outs, sems):
        me = _me()
        loc, rem = [], []
        for t in range(len(ins)):
            loc.append(pltpu.make_async_copy(ins[t].at[_idx(me)], outs[t].at[_idx(me)], sems[2].at[t]))
            for k, p in enumerate(_peers(me)):
                rem.append(pltpu.make_async_remote_copy(
                    src_ref=ins[t].at[_idx(p)], dst_ref=outs[t].at[_idx(me)], send_sem=sems[0].at[t, k],
                    recv_sem=sems[1].at[t, k], device_id=p, device_id_type=MESH))
        return loc, rem

    def start(self, ins, outs, sems):
        loc, rem = self._copies(ins, outs, sems)
        for cp in loc + rem:
            cp.start()

    def mid(self, ins, outs, sems):
        pass

    def finish(self, ins, outs, sems):
        me = _me()
        for t in range(len(ins)):
            for k, p in enumerate(_peers(me)):
                slot = outs[t].at[_idx(p)]
                pltpu.make_async_remote_copy(
                    src_ref=slot, dst_ref=slot, send_sem=sems[0].at[t, k], recv_sem=sems[1].at[t, k],
                    device_id=p, device_id_type=MESH).wait_recv()
        loc, rem = self._copies(ins, outs, sems)
        for cp in rem:
            cp.wait_send()
        for cp in loc:
            cp.wait()


class _Multi:
    def __init__(self, *exs):
        self.exs = exs
        self.srcs = [s for e in exs for s in e.srcs]
        self.out_shape = [s for e in exs for s in e.out_shape]
        self.scratch = [s for e in exs for s in e.scratch]

    def _each(self, ins, outs, sems):
        i = j = 0
        for e in self.exs:
            n, m = len(e.srcs), len(e.scratch)
            yield e, ins[i:i + n], outs[i:i + n], sems[j:j + m]
            i, j = i + n, j + m

    def start(self, ins, outs, sems):
        for e, a, b, c in self._each(ins, outs, sems):
            e.start(a, b, c)

    def mid(self, ins, outs, sems):
        for e, a, b, c in self._each(ins, outs, sems):
            e.mid(a, b, c)

    def finish(self, ins, outs, sems):
        for e, a, b, c in self._each(ins, outs, sems):
            e.finish(a, b, c)


def _call(body, *, name, grid, in_specs, out_specs, out_shape, args, scratch_shapes=(), rider=None):
    in_specs, out_specs, out_shape = list(in_specs), list(out_specs), list(out_shape)
    scratch_shapes = list(scratch_shapes)
    if rider is None:
        return pl.pallas_call(body, name=name, grid=grid, out_shape=out_shape, in_specs=in_specs,
                              out_specs=out_specs, scratch_shapes=scratch_shapes, compiler_params=_params())(*args)
    nci, nco, ncs, nr = len(in_specs), len(out_shape), len(scratch_shapes), len(rider.srcs)
    nsteps = math.prod(grid)
    assert nsteps >= 2, (name, grid)
    mid = max(1, (7 * nsteps) // 8)

    def full(*refs):
        ci, ri = refs[:nci], refs[nci:nci + nr]
        co, ro = refs[nci + nr:nci + nr + nco], refs[nci + nr + nco:nci + 2 * nr + nco]
        cs, rs = refs[nci + 2 * nr + nco:nci + 2 * nr + nco + ncs], refs[nci + 2 * nr + nco + ncs:]
        step = pl.program_id(0)
        for d in range(1, len(grid)):
            step = step * grid[d] + pl.program_id(d)

        @pl.when(step == 0)
        def _():
            rider.start(ri, ro, rs)

        @pl.when(step == mid)
        def _():
            rider.mid(ri, ro, rs)

        body(*ci, *co, *cs)

        @pl.when(step == nsteps - 1)
        def _():
            rider.finish(ri, ro, rs)

    any_spec = pl.BlockSpec(memory_space=pl.ANY)
    return pl.pallas_call(
        full, name=name, grid=grid, out_shape=out_shape + rider.out_shape,
        in_specs=in_specs + [any_spec] * nr, out_specs=out_specs + [any_spec] * nr,
        scratch_shapes=scratch_shapes + rider.scratch, compiler_params=_params(),
    )(*args, *rider.srcs)


def _a2a_start(srcs, dsts, send_sems, recv_sems, local_sems, me, sem_base=0):
    peers = _peers(me)
    started = []
    for t in range(len(srcs)):
        loc = pltpu.make_async_copy(srcs[t].at[_idx(me)], dsts[t].at[_idx(me)], local_sems.at[sem_base + t])
        loc.start()
        started.append(("local", loc))
        for k, p in enumerate(peers):
            cp = pltpu.make_async_remote_copy(
                src_ref=srcs[t].at[_idx(p)], dst_ref=dsts[t].at[_idx(me)],
                send_sem=send_sems.at[sem_base + t, k], recv_sem=recv_sems.at[sem_base + t, k],
                device_id=p, device_id_type=MESH)
            cp.start()
            started.append(("remote", cp))
    return started


def _a2a_finish(started, dsts, send_sems, recv_sems, me, sem_base=0):
    peers = _peers(me)
    for t in range(len(dsts)):
        for k, p in enumerate(peers):
            slot = dsts[t].at[_idx(p)]
            pltpu.make_async_remote_copy(
                src_ref=slot, dst_ref=slot, send_sem=send_sems.at[sem_base + t, k],
                recv_sem=recv_sems.at[sem_base + t, k], device_id=p, device_id_type=MESH).wait_recv()
    for kind, cp in started:
        if kind == "local":
            cp.wait()
        else:
            cp.wait_send()


def ada_modulation(c8, ada_w, ada_b, ride):
    nr = len(ride.srcs)

    def body(c_ref, w_ref, b_ref, *rest):
        ride_in, (call_ref, modr_ref), ride_out = rest[:nr], rest[nr:nr + 2], rest[nr + 2:2 * nr + 2]
        modp, send_sems, recv_sems, local_sems = rest[2 * nr + 2:2 * nr + 6]
        ride_sems = rest[2 * nr + 6:]
        ride.start(ride_in, ride_out, ride_sems)
        me = _me()
        peers = _peers(me)
        sends = []
        for k, p in enumerate(peers):
            cp = pltpu.make_async_remote_copy(
                src_ref=c_ref, dst_ref=call_ref.at[_idx(me)], send_sem=send_sems.at[0, k],
                recv_sem=recv_sems.at[0, k], device_id=p, device_id_type=MESH)
            cp.start()
            sends.append(cp)
        call_ref[_idx(me)] = c_ref[...]
        for k, p in enumerate(peers):
            slot = call_ref.at[_idx(p)]
            pltpu.make_async_remote_copy(
                src_ref=slot, dst_ref=slot, send_sem=send_sems.at[0, k], recv_sem=recv_sems.at[0, k],
                device_id=p, device_id_type=MESH).wait_recv()
        for cp in sends:
            cp.wait_send()
        cv = call_ref[...].reshape(N_DEV * 8, D_MODEL)
        s = (cv * _sigmoid(cv)).astype(BF16)
        for k in range(4):
            res = _nn(s, w_ref[k].astype(BF16)) + b_ref[k]
            for j in range(N_DEV):
                modp[j, 8 * k:8 * k + 8, :] = res[8 * j:8 * j + 8, :]
        started = _a2a_start([modp], [modr_ref], send_sems, recv_sems, local_sems, me, sem_base=1)
        _a2a_finish(started, [modr_ref], send_sems, recv_sems, me, sem_base=1)
        ride.mid(ride_in, ride_out, ride_sems)
        ride.finish(ride_in, ride_out, ride_sems)

    vm, hbm = pl.BlockSpec(memory_space=pltpu.VMEM), pl.BlockSpec(memory_space=pl.ANY)
    return pl.pallas_call(
        body, name="ada_modulation",
        out_shape=[jax.ShapeDtypeStruct((N_DEV, 8, D_MODEL), F32), jax.ShapeDtypeStruct((N_DEV, 32, CG), F32)]
        + ride.out_shape,
        in_specs=[vm, vm, vm] + [hbm] * nr, out_specs=[vm, vm] + [hbm] * nr,
        scratch_shapes=[pltpu.VMEM((N_DEV, 32, CG), F32), pltpu.SemaphoreType.DMA((2, 7)),
                        pltpu.SemaphoreType.DMA((2, 7)), pltpu.SemaphoreType.DMA((2,))] + ride.scratch,
        compiler_params=_params(),
    )(c8, ada_w, ada_b, *ride.srcs)


def epilogue(gsend, c_t, tail, ride):
    nr = len(ride.srcs)
    rt = tail.shape[1]

    def body(g_ref, ct_ref, t_ref, *rest):
        ride_in, (gw_ref, gb_ref, red_ref), ride_out = rest[:nr], rest[nr:nr + 3], rest[nr + 3:2 * nr + 3]
        grecv, trecv, send_sems, recv_sems, local_sems = rest[2 * nr + 3:2 * nr + 8]
        ride_sems = rest[2 * nr + 8:]
        ride.start(ride_in, ride_out, ride_sems)
        me = _me()
        started = _a2a_start([g_ref, t_ref], [grecv, trecv], send_sems, recv_sems, local_sems, me)
        _a2a_finish(started, [grecv, trecv], send_sems, recv_sems, me)
        acc = trecv[0]
        for j in range(1, N_DEV):
            acc = acc + trecv[j]
        red_ref[...] = acc
        ct = ct_ref[...]
        st = (ct * _sigmoid(ct)).astype(BF16).astype(F32)
        gb = jnp.zeros((8, CG), F32)
        for b in range(N_DEV):
            gb = gb + grecv[b]
        gb_ref[...] = gb
        for k in range(4):
            acc = jnp.zeros((D_MODEL, CG), F32)
            for b in range(N_DEV):
                row = grecv[b, k:k + 1, :].astype(BF16).astype(F32)
                acc = acc + st[:, b:b + 1] * row
            gw_ref[k] = acc
        ride.mid(ride_in, ride_out, ride_sems)
        ride.finish(ride_in, ride_out, ride_sems)

    vm, hbm = pl.BlockSpec(memory_space=pltpu.VMEM), pl.BlockSpec(memory_space=pl.ANY)
    return pl.pallas_call(
        body, name="epilogue",
        out_shape=[jax.ShapeDtypeStruct((4, D_MODEL, CG), F32), jax.ShapeDtypeStruct((8, CG), F32),
                   jax.ShapeDtypeStruct((rt, 128), F32)] + ride.out_shape,
        in_specs=[vm, vm, vm] + [hbm] * nr, out_specs=[vm, vm, vm] + [hbm] * nr,
        scratch_shapes=[pltpu.VMEM((N_DEV, 8, CG), F32), pltpu.VMEM((N_DEV, rt, 128), F32),
                        pltpu.SemaphoreType.DMA((2, 7)), pltpu.SemaphoreType.DMA((2, 7)),
                        pltpu.SemaphoreType.DMA((2,))] + ride.scratch,
        compiler_params=_params(),
    )(gsend, c_t, tail, *ride.srcs)


def _rope(t, cos, s1, s2):
    return t * cos + pltpu.roll(t, 16, 1) * s1 + pltpu.roll(t, HEAD - 16, 1) * s2


def _rope_bwd(d, cos, s1, s2):
    return d * cos + pltpu.roll(d * s1, HEAD - 16, 1) + pltpu.roll(d * s2, 16, 1)


def attn_in_fwd(x, mods, k, win_t, rope, rider=None):
    s = x.shape[0]
    tm = _blk(s, TM_MM)

    def body(x_ref, mod_ref, w_ref, c_ref, s1_ref, s2_ref, q_ref, k_ref, v_ref):
        shift, scale, _ = _mod(mod_ref, k)
        h = (x_ref[...] * (1.0 + scale) + shift).astype(BF16)
        qkv = _nt(h, w_ref[...])
        cos, s1, s2 = c_ref[...], s1_ref[...], s2_ref[...]
        for hh in range(N_Q + N_KV):
            r = _rope(qkv[:, HEAD * hh:HEAD * (hh + 1)], cos, s1, s2).astype(BF16)
            if hh < N_Q:
                q_ref[:, HEAD * hh:HEAD * (hh + 1)] = r
            else:
                k_ref[:, HEAD * (hh - N_Q):HEAD * (hh - N_Q + 1)] = r
        v_ref[...] = qkv[:, HEAD * (N_Q + N_KV):].astype(BF16)

    return _call(
        body, name="attn_in_fwd", grid=(s // tm,),
        out_shape=[jax.ShapeDtypeStruct((s, N_Q * HEAD), BF16), jax.ShapeDtypeStruct((s, N_KV * HEAD), BF16),
                   jax.ShapeDtypeStruct((s, N_KV * HEAD), BF16)],
        in_specs=[_row(tm, D_MODEL), _res(mods.shape), _res(win_t.shape),
                  _row(tm, HEAD), _row(tm, HEAD), _row(tm, HEAD)],
        out_specs=[_row(tm, N_Q * HEAD), _row(tm, N_KV * HEAD), _row(tm, N_KV * HEAD)],
        args=(x, mods, win_t, *rope), rider=rider)


QPAIR = 2


def _kv_specs(nblk):
    w = N_KV * HEAD
    return [pl.BlockSpec((QBLK, w), lambda n: (jnp.maximum(QPAIR * n - 1, 0), 0)),
            pl.BlockSpec((QBLK, w), lambda n: (QPAIR * n, 0)),
            pl.BlockSpec((QBLK, w), lambda n: (QPAIR * n + 1, 0)),
            pl.BlockSpec((QBLK, w), lambda n: (jnp.minimum(QPAIR * n + 2, nblk - 1), 0))]


GROUP = N_Q // N_KV


def _attn_mask(n, s):
    qi = lax.broadcasted_iota(jnp.int32, (GROUP * QBLK, 3 * QBLK), 0) & (QBLK - 1)
    kj = lax.broadcasted_iota(jnp.int32, (GROUP * QBLK, 3 * QBLK), 1)
    rel = kj - QBLK - qi
    kpos = kj + (n - 1) * QBLK
    return (jnp.abs(rel) <= QBLK) & (kpos >= 0) & (kpos < s)


def _stack_heads(ref, qb, kv):
    rows = slice(QBLK * qb, QBLK * (qb + 1))
    return jnp.concatenate([ref[rows, HEAD * (GROUP * kv + j):HEAD * (GROUP * kv + j + 1)] for j in range(GROUP)],
                           axis=0)


def _stack_sinks(sink_ref, kv):
    row = lax.broadcasted_iota(jnp.int32, (GROUP * QBLK, 1), 0)
    out = jnp.full((GROUP * QBLK, 1), sink_ref[0, GROUP * kv + GROUP - 1], F32)
    for j in range(GROUP - 2, -1, -1):
        out = jnp.where(row < QBLK * (j + 1), sink_ref[0, GROUP * kv + j], out)
    return out


def _attn_probs(qh, kh, valid, sink):
    sc = _nt(qh, kh) * (HEAD ** -0.5)
    sc = jnp.where(valid, sc, -1e30)
    m = jnp.maximum(jnp.max(sc, axis=-1, keepdims=True), sink)
    p = jnp.exp(sc - m)
    es = jnp.exp(sink - m)
    denom = jnp.sum(p, axis=-1, keepdims=True) + es
    return p / denom, es / denom


def attn_fwd(q, kk, v, sinks, rider=None):
    s = q.shape[0]
    nblk = s // QBLK

    def body(sink_ref, q_ref, k0, k1, k2, k3, v0, v1, v2, v3, o_ref):
        n = pl.program_id(0)
        kall = jnp.concatenate([k0[...], k1[...], k2[...], k3[...]], axis=0)
        vall = jnp.concatenate([v0[...], v1[...], v2[...], v3[...]], axis=0)
        for qb in range(QPAIR):
            valid = _attn_mask(QPAIR * n + qb, s)
            keys = slice(QBLK * qb, QBLK * (qb + 3))
            for kv in range(N_KV):
                cols = slice(HEAD * kv, HEAD * (kv + 1))
                probs, _ = _attn_probs(_stack_heads(q_ref, qb, kv), kall[keys, cols], valid,
                                       _stack_sinks(sink_ref, kv))
                og = _nn(probs.astype(BF16), vall[keys, cols]).astype(BF16)
                for j in range(GROUP):
                    hq = GROUP * kv + j
                    o_ref[QBLK * qb:QBLK * (qb + 1), HEAD * hq:HEAD * (hq + 1)] = og[QBLK * j:QBLK * (j + 1), :]

    qspec = pl.BlockSpec((QPAIR * QBLK, N_Q * HEAD), lambda n: (n, 0))
    return _call(
        body, name="attn_fwd", grid=(nblk // QPAIR,),
        out_shape=[jax.ShapeDtypeStruct((s, N_Q * HEAD), BF16)],
        in_specs=[pl.BlockSpec(memory_space=pltpu.SMEM), qspec] + _kv_specs(nblk) + _kv_specs(nblk),
        out_specs=[qspec],
        args=(sinks, q, kk, kk, kk, kk, v, v, v, v), rider=rider)


def post_fwd(ypre, w, x, mods, k, lng, lnb, gate_act=None, rider=None):
    s = x.shape[0]
    tm = _blk(s, TM_MM)
    kdim = w.shape[0]
    rnn = gate_act is not None

    def body(*refs):
        if rnn:
            gt_ref, hf_ref, hb_ref, w_ref, x_ref, mod_ref, g_ref, b_ref, xo_ref, y_ref, yp_ref = refs
            act, _ = _gelu_parts(gt_ref[...])
            yp = ((hf_ref[...] + hb_ref[...]) * act).astype(BF16)
            yp_ref[...] = yp
        else:
            yp_ref, w_ref, x_ref, mod_ref, g_ref, b_ref, xo_ref, y_ref = refs
            yp = yp_ref[...]
        _, _, gate = _mod(mod_ref, k)
        y = _nn(yp, w_ref[...])
        y_ref[...] = y
        xhat, _ = _ln_stats(ALPHA * x_ref[...] + (1.0 + gate) * y)
        xo_ref[...] = xhat * g_ref[...] + b_ref[...]

    act_in = list(gate_act) if rnn else [ypre]
    out_shape = [jax.ShapeDtypeStruct((s, D_MODEL), F32), jax.ShapeDtypeStruct((s, D_MODEL), F32)]
    out_specs = [_row(tm, D_MODEL), _row(tm, D_MODEL)]
    if rnn:
        out_shape.append(jax.ShapeDtypeStruct((s, kdim), BF16))
        out_specs.append(_row(tm, kdim))
    return _call(
        body, name="rnn_post_fwd" if rnn else "attn_post_fwd", grid=(s // tm,),
        out_shape=out_shape,
        in_specs=[_row(tm, kdim)] * len(act_in) + [_res(w.shape), _row(tm, D_MODEL), _res(mods.shape),
                                                    _res(lng.shape), _res(lnb.shape)],
        out_specs=out_specs,
        args=(*act_in, w, x, mods, lng, lnb), rider=rider)


def mlp_fwd(x, mods, k, w1_t, w2, lng, lnb, rider=None, last=False):
    s = x.shape[0]
    tm = _blk(s, TM_MLP_FWD)

    def body(x_ref, mod_ref, w1_ref, w2_ref, g_ref, b_ref, *outs):
        xo_ref = None if last else outs[0]
        y_ref, ra_ref, r_ref = outs[-3:]
        xv = x_ref[...]
        shift, scale, gate = _mod(mod_ref, k)
        h = (xv * (1.0 + scale) + shift).astype(BF16)
        y = jnp.zeros((tm, D_MODEL), F32)
        for c in range(D_FF // FF_CHUNK):
            rows = slice(FF_CHUNK * c, FF_CHUNK * (c + 1))
            a = jnp.maximum(_nt(h, w1_ref[rows, :]), 0.0)
            r = (a * a).astype(BF16)
            ra_ref[:, rows] = a.astype(BF16)
            r_ref[:, rows] = r
            y = y + _nn(r, w2_ref[rows, :])
        y_ref[...] = y
        if not last:
            xhat, _ = _ln_stats(ALPHA * xv + (1.0 + gate) * y)
            xo_ref[...] = xhat * g_ref[...] + b_ref[...]

    nf = 1 if last else 2
    return _call(
        body, name="mlp_fwd_last" if last else "mlp_fwd", grid=(s // tm,),
        out_shape=[jax.ShapeDtypeStruct((s, D_MODEL), F32)] * nf + [jax.ShapeDtypeStruct((s, D_FF), BF16)] * 2,
        in_specs=[_row(tm, D_MODEL), _res(mods.shape), _res(w1_t.shape), _res(w2.shape),
                  _res(lng.shape), _res(lnb.shape)],
        out_specs=[_row(tm, D_MODEL)] * nf + [_row(tm, D_FF)] * 2,
        args=(x, mods, w1_t, w2, lng, lnb), rider=rider)


def rnn_in_fwd(x, mods, k, win_t):
    s = x.shape[0]
    tm = _blk(s, TM_MM)

    def body(x_ref, mod_ref, w_ref, xr_ref, gt_ref):
        shift, scale, _ = _mod(mod_ref, k)
        h = (x_ref[...] * (1.0 + scale) + shift).astype(BF16)
        xr_ref[...] = _nt(h, w_ref[0:D_RNN, :])
        gt_ref[...] = _nt(h, w_ref[D_RNN:2 * D_RNN, :])

    return pl.pallas_call(
        body, name="rnn_in_fwd", grid=(s // tm,),
        out_shape=[jax.ShapeDtypeStruct((s, D_RNN), F32)] * 2,
        in_specs=[_row(tm, D_MODEL), _res(mods.shape), _res(win_t.shape)],
        out_specs=[_row(tm, D_RNN)] * 2,
        compiler_params=_params(),
    )(x, mods, win_t)


def _shift_rows(v, k, row):
    n = v.shape[0]
    r = pltpu.roll(v, k % n, 0)
    keep = (row >= k) if k > 0 else (row < n + k)
    return jnp.where(keep, r, 0.0)


def conv_fwd(xr, cw, cb):
    s = xr.shape[0]

    def body(x_ref, w_ref, b_ref, o_ref):
        xv = x_ref[...]
        row = lax.broadcasted_iota(jnp.int32, xv.shape, 0)
        o_ref[...] = (b_ref[...] + w_ref[0:1, :] * _shift_rows(xv, 2, row) + w_ref[1:2, :] * _shift_rows(xv, 1, row)
                      + w_ref[2:3, :] * xv + w_ref[3:4, :] * _shift_rows(xv, -1, row))

    slab = pl.BlockSpec((s, 128), lambda j: (0, j))
    return pl.pallas_call(
        body, name="conv_fwd", grid=(D_RNN // 128,),
        out_shape=jax.ShapeDtypeStruct((s, D_RNN), F32),
        in_specs=[slab, pl.BlockSpec((4, 128), lambda j: (0, j)), pl.BlockSpec((1, 128), lambda j: (0, j))],
        out_specs=slab,
        compiler_params=_params(),
    )(xr, cw, cb)


def _softplus_neg(lam):
    z = -lam
    e = jnp.exp(-jnp.abs(z))
    u = 1.0 + e
    log1p = jnp.where(u == 1.0, e, jnp.log(u) * e / jnp.where(u == 1.0, 1.0, u - 1.0))
    return jnp.maximum(z, 0.0) + log1p, 1.0 / (1.0 + jnp.exp(lam))


def _lru_gates(xv, wa_ref, wx_ref, ba_ref, bx_ref, lam_ref):
    xb = xv.astype(BF16)
    r = _sigmoid(_nn(xb, wa_ref[...]) + ba_ref[...])
    i = _sigmoid(_nn(xb, wx_ref[...]) + bx_ref[...])
    sp, sg = _softplus_neg(lam_ref[...])
    la = r * (-LRU_C * sp)
    a = jnp.exp(la)
    th = jnp.tanh(la)
    m2 = -2.0 * th / (1.0 - th)
    rmult = lax.rsqrt(jnp.maximum(m2, 1e-37))
    return xb, r, i, sp, sg, a, m2 * rmult, rmult


def _scan(a, u, h0, reverse):
    n, c = a.shape
    sub = lax.broadcasted_iota(jnp.int32, (8, c), 0)
    steps = [(8 - sh, sub < 8 - sh) if reverse else (sh, sub >= sh) for sh in (1, 2, 4)]
    out = [None] * (n // 8)
    edge = h0
    for k in (range(n // 8 - 1, -1, -1) if reverse else range(n // 8)):
        at, ut = a[8 * k:8 * k + 8], u[8 * k:8 * k + 8]
        for rot, keep in steps:
            a_s = jnp.where(keep, pltpu.roll(at, rot, 0), 1.0)
            u_s = jnp.where(keep, pltpu.roll(ut, rot, 0), 0.0)
            ut = at * u_s + ut
            at = at * a_s
        hk = ut + at * edge
        out[k] = hk
        edge = hk[0:1] if reverse else hk[7:8]
    return jnp.concatenate(out, axis=0)


def _lru_specs(nt, tt, reverse):
    tmap = (lambda t: nt - 1 - t) if reverse else (lambda t: t)
    blk = pl.BlockSpec((tt, CG), lambda g, t: (tmap(t), g))
    wsp = pl.BlockSpec((None, CG, CG), lambda g, t: (g, 0, 0))
    vec = pl.BlockSpec((1, CG), lambda g, t: (0, g))
    return tmap, blk, wsp, vec


def lru_fwd(xc, wa, wx, ba, bx, lam, reverse):
    s = xc.shape[0]
    tt = _blk(s, TT_RNN)
    sb = _blk(tt, SB_RNN)
    nt = s // tt

    def body(x_ref, wa_ref, wx_ref, ba_ref, bx_ref, lam_ref, hs_ref, carry):
        @pl.when(pl.program_id(1) == 0)
        def _():
            carry[...] = jnp.zeros(carry.shape, F32)

        xv = x_ref[...]
        _, _, i, _, _, a, mult, _ = _lru_gates(xv, wa_ref, wx_ref, ba_ref, bx_ref, lam_ref)
        u = mult * (i * xv)
        h0 = carry[0:1, :]
        order = range(tt // sb - 1, -1, -1) if reverse else range(tt // sb)
        for j in order:
            rows = slice(sb * j, sb * (j + 1))
            h = _scan(a[rows], u[rows], h0, reverse)
            hs_ref[rows, :] = h
            h0 = h[0:1, :] if reverse else h[sb - 1:sb, :]
        carry[0:1, :] = h0

    _, blk, wsp, vec = _lru_specs(nt, tt, reverse)
    return pl.pallas_call(
        body, name="lru_fwd_rev" if reverse else "lru_fwd", grid=(N_CG, nt),
        out_shape=jax.ShapeDtypeStruct((s, D_RNN), F32),
        in_specs=[blk, wsp, wsp, vec, vec, vec], out_specs=blk,
        scratch_shapes=[pltpu.VMEM((8, CG), F32)],
        compiler_params=_params(),
    )(xc, wa, wx, ba, bx, lam)


def lru_bwd(xc, dhs, hs, wa, wx, ba, bx, lam, reverse, rider=None):
    s = xc.shape[0]
    tt = _blk(s, TT_RNN)
    sb = _blk(tt, SB_RNN)
    nt = s // tt
    back = not reverse

    def body(x_ref, dh_ref, hs_ref, nb_ref, wa_ref, wx_ref, ba_ref, bx_ref, lam_ref,
             dx_ref, dwa_ref, dwx_ref, dba_ref, dbx_ref, dlam_ref, carry):
        t = pl.program_id(1)

        @pl.when(t == 0)
        def _():
            carry[...] = jnp.zeros(carry.shape, F32)
            dwa_ref[...] = jnp.zeros(dwa_ref.shape, F32)
            dwx_ref[...] = jnp.zeros(dwx_ref.shape, F32)
            dba_ref[...] = jnp.zeros(dba_ref.shape, F32)
            dbx_ref[...] = jnp.zeros(dbx_ref.shape, F32)
            dlam_ref[...] = jnp.zeros(dlam_ref.shape, F32)

        xv = x_ref[...]
        xb, r, i, sp, sg, a, mult, rmult = _lru_gates(xv, wa_ref, wx_ref, ba_ref, bx_ref, lam_ref)
        row = lax.broadcasted_iota(jnp.int32, xv.shape, 0)
        hsv = hs_ref[...]
        inner = t < nt - 1
        if reverse:
            edge = jnp.where(inner, nb_ref[0:1, :], 0.0)
            hprev = jnp.where(row == tt - 1, edge, pltpu.roll(hsv, tt - 1, 0))
            a_next = jnp.where(row == 0, carry[1:2, :], pltpu.roll(a, 1, 0))
        else:
            edge = jnp.where(inner, nb_ref[7:8, :], 0.0)
            hprev = jnp.where(row == 0, edge, pltpu.roll(hsv, 1, 0))
            a_next = jnp.where(row == tt - 1, carry[1:2, :], pltpu.roll(a, tt - 1, 0))
        dhv = dh_ref[...]
        g0 = carry[0:1, :]
        parts = [None] * (tt // sb)
        order = range(tt // sb - 1, -1, -1) if back else range(tt // sb)
        for j in order:
            rows = slice(sb * j, sb * (j + 1))
            gj = _scan(a_next[rows], dhv[rows], g0, back)
            parts[j] = gj
            g0 = gj[0:1, :] if back else gj[sb - 1:sb, :]
        g = jnp.concatenate(parts, axis=0) if len(parts) > 1 else parts[0]
        carry[0:1, :] = g0
        carry[1:2, :] = a[0:1, :] if back else a[tt - 1:tt, :]

        da = g * hprev
        dmult = g * (i * xv)
        di = g * mult * xv
        dla = da * a - dmult * (a * a) * rmult
        dpa = (dla * (-LRU_C * sp)) * r * (1.0 - r)
        dpx = di * i * (1.0 - i)
        dlam_ref[...] += _colsum(dla * (LRU_C * r * sg))
        dba_ref[...] += _colsum(dpa)
        dbx_ref[...] += _colsum(dpx)
        dpab, dpxb = dpa.astype(BF16), dpx.astype(BF16)
        dx_ref[...] = g * mult * i + _nt(dpab, wa_ref[...]) + _nt(dpxb, wx_ref[...])
        dwa_ref[...] += _tn(xb, dpab)
        dwx_ref[...] += _tn(xb, dpxb)

    tmap, blk, wsp, vec = _lru_specs(nt, tt, back)
    per8 = tt // 8
    if reverse:
        nb = pl.BlockSpec((8, CG), lambda g, t: (jnp.minimum((tmap(t) + 1) * per8, s // 8 - 1), g))
    else:
        nb = pl.BlockSpec((8, CG), lambda g, t: (jnp.maximum(tmap(t) * per8 - 1, 0), g))
    return _call(
        body, name="lru_bwd_rev" if reverse else "lru_bwd", grid=(N_CG, nt),
        out_shape=[jax.ShapeDtypeStruct((s, D_RNN), F32), jax.ShapeDtypeStruct((N_CG, CG, CG), F32),
                   jax.ShapeDtypeStruct((N_CG, CG, CG), F32)] + [jax.ShapeDtypeStruct((1, D_RNN), F32)] * 3,
        in_specs=[blk, blk, blk, nb, wsp, wsp, vec, vec, vec],
        out_specs=[blk, wsp, wsp, vec, vec, vec],
        scratch_shapes=[pltpu.VMEM((8, CG), F32)],
        args=(xc, dhs, hs, hs, wa, wx, ba, bx, lam), rider=rider)


def _ln_part_bwd(dxo, x, y, gate, g, sums_ref, loss_head=None):
    xhat, rstd = _ln_stats(ALPHA * x + (1.0 + gate) * y)
    if loss_head is not None:
        err = xhat * g + loss_head[0] - loss_head[1]
        dxo = err * (1.0 / D_MODEL)
        sums_ref[5:6, :] += _colsum(err * err)
    dz = _ln_bwd(dxo, xhat, rstd, g)
    sums_ref[2:3, :] += _colsum(dz * y)
    sums_ref[3:4, :] += _colsum(dxo * xhat)
    sums_ref[4:5, :] += _colsum(dxo)
    return dz


def mlp_bwd(dxo, x, y, ra, mods, k, w1_t, w2, lng, lnb=None, rider=None):
    s = x.shape[0]
    tm = _blk(s, TM_MLP)
    head = lnb is not None

    def body(d_ref, x_ref, y_ref, ra_ref, mod_ref, w1_ref, w2_ref, g_ref, *rest):
        b_ref = rest[0] if head else None
        dx_ref, da_ref, h_ref, dy_ref, sums_ref = rest[1:] if head else rest

        @pl.when(pl.program_id(0) == 0)
        def _():
            sums_ref[...] = jnp.zeros(sums_ref.shape, F32)

        xv = x_ref[...]
        shift, scale, gate = _mod(mod_ref, k)
        if head:
            dz = _ln_part_bwd(None, xv, y_ref[...], gate, g_ref[...], sums_ref, (b_ref[...], d_ref[...]))
        else:
            dz = _ln_part_bwd(d_ref[...], xv, y_ref[...], gate, g_ref[...], sums_ref)
        dyb = (dz * (1.0 + gate)).astype(BF16)
        dy_ref[...] = dyb
        h = (xv * (1.0 + scale) + shift).astype(BF16)
        h_ref[...] = h
        dh = jnp.zeros((tm, D_MODEL), F32)
        for c in range(D_FF // FF_CHUNK):
            rows = slice(FF_CHUNK * c, FF_CHUNK * (c + 1))
            da = (_nt(dyb, w2_ref[rows, :]) * (2.0 * ra_ref[:, rows].astype(F32))).astype(BF16)
            da_ref[:, rows] = da
            dh = dh + _nn(da, w1_ref[rows, :])
        dx_ref[...] = ALPHA * dz + dh * (1.0 + scale)
        sums_ref[0:1, :] += _colsum(dh)
        sums_ref[1:2, :] += _colsum(dh * xv)

    return _call(
        body, name="mlp_bwd", grid=(s // tm,),
        out_shape=[jax.ShapeDtypeStruct((s, D_MODEL), F32), jax.ShapeDtypeStruct((s, D_FF), BF16),
                   jax.ShapeDtypeStruct((s, D_MODEL), BF16),
                   jax.ShapeDtypeStruct((s, D_MODEL), BF16), jax.ShapeDtypeStruct((8, D_MODEL), F32)],
        in_specs=[_row(tm, D_MODEL)] * 3 + [_row(tm, D_FF), _res(mods.shape), _res(w1_t.shape), _res(w2.shape),
                                             _res(lng.shape)] + ([_res(lnb.shape)] if head else []),
        out_specs=[_row(tm, D_MODEL), _row(tm, D_FF), _row(tm, D_MODEL), _row(tm, D_MODEL), _res((8, D_MODEL))],
        args=(dxo, x, y, ra, mods, w1_t, w2, lng) + ((lnb,) if head else ()), rider=rider)


def post_bwd(dxo, x, y, mods, k, w, lng, gate_act=None, rider=None):
    s = x.shape[0]
    tm = _blk(s, TM_MM)
    kdim = w.shape[0]
    rnn = gate_act is not None

    def body(*refs):
        if rnn:
            (d_ref, x_ref, y_ref, mod_ref, w_ref, g_ref, gt_ref, hf_ref, hb_ref,
             dres_ref, dy_ref, sums_ref, dhs_ref, dgt_ref) = refs
        else:
            d_ref, x_ref, y_ref, mod_ref, w_ref, g_ref, dres_ref, dy_ref, sums_ref, dyp_ref = refs

        @pl.when(pl.program_id(0) == 0)
        def _():
            sums_ref[...] = jnp.zeros(sums_ref.shape, F32)

        _, _, gate = _mod(mod_ref, k)
        dz = _ln_part_bwd(d_ref[...], x_ref[...], y_ref[...], gate, g_ref[...], sums_ref)
        dres_ref[...] = ALPHA * dz
        dyb = (dz * (1.0 + gate)).astype(BF16)
        dy_ref[...] = dyb
        dyp = _nt(dyb, w_ref[...])
        if rnn:
            act, dact = _gelu_parts(gt_ref[...])
            dhs_ref[...] = dyp * act
            dgt_ref[...] = (dyp * (hf_ref[...] + hb_ref[...]) * dact).astype(BF16)
        else:
            dyp_ref[...] = dyp.astype(BF16)

    ins = [dxo, x, y, mods, w, lng] + (list(gate_act) if rnn else [])
    in_specs = [_row(tm, D_MODEL)] * 3 + [_res(mods.shape), _res(w.shape), _res(lng.shape)]
    out_shape = [jax.ShapeDtypeStruct((s, D_MODEL), F32), jax.ShapeDtypeStruct((s, D_MODEL), BF16),
                 jax.ShapeDtypeStruct((8, D_MODEL), F32)]
    out_specs = [_row(tm, D_MODEL), _row(tm, D_MODEL), _res((8, D_MODEL))]
    if rnn:
        in_specs += [_row(tm, kdim)] * 3
        out_shape += [jax.ShapeDtypeStruct((s, kdim), F32), jax.ShapeDtypeStruct((s, kdim), BF16)]
        out_specs += [_row(tm, kdim)] * 2
    else:
        out_shape.append(jax.ShapeDtypeStruct((s, kdim), BF16))
        out_specs.append(_row(tm, kdim))
    return _call(
        body, name="rnn_post_bwd" if rnn else "attn_post_bwd", grid=(s // tm,),
        out_shape=out_shape, in_specs=in_specs, out_specs=out_specs, args=ins, rider=rider)


def attn_bwd(q, kk, v, do, sinks, rider=None):
    s = q.shape[0]
    nblk = s // QBLK
    scale = HEAD ** -0.5

    def body(sink_ref, q_ref, do_ref, k0, k1, k2, k3, v0, v1, v2, v3, dq_ref, dk_ref, dv_ref, ds_ref):
        n = pl.program_id(0)

        @pl.when(n == 0)
        def _():
            ds_ref[...] = jnp.zeros(ds_ref.shape, F32)
            dk_ref[...] = jnp.zeros(dk_ref.shape, F32)
            dv_ref[...] = jnp.zeros(dv_ref.shape, F32)

        kall = jnp.concatenate([k0[...], k1[...], k2[...], k3[...]], axis=0)
        vall = jnp.concatenate([v0[...], v1[...], v2[...], v3[...]], axis=0)
        lane = lax.broadcasted_iota(jnp.int32, (1, 128), 1)
        dsink = jnp.zeros((1, 128), F32)
        for qb in range(QPAIR):
            nb = QPAIR * n + qb
            valid = _attn_mask(nb, s)
            keys = slice(QBLK * qb, QBLK * (qb + 3))
            for kv in range(N_KV):
                cols = slice(HEAD * kv, HEAD * (kv + 1))
                qg, dog = _stack_heads(q_ref, qb, kv), _stack_heads(do_ref, qb, kv)
                kh, vh = kall[keys, cols], vall[keys, cols]
                probs, psink = _attn_probs(qg, kh, valid, _stack_sinks(sink_ref, kv))
                dprobs = _nt(dog, vh)
                dvp = _tn(probs.astype(BF16), dog)
                rowdot = jnp.sum(probs * dprobs, axis=-1, keepdims=True)
                dsb = (probs * (dprobs - rowdot) * scale).astype(BF16)
                dqg = _nn(dsb, kh)
                dkp = _tn(dsb, qg)
                for p in range(3):
                    blk = jnp.clip(nb - 1 + p, 0, nblk - 1)
                    rows = pl.ds(pl.multiple_of(blk * QBLK, QBLK), QBLK)
                    dk_ref[rows, cols] += dkp[QBLK * p:QBLK * (p + 1), :]
                    dv_ref[rows, cols] += dvp[QBLK * p:QBLK * (p + 1), :]
                dsk = -psink * rowdot
                for j in range(GROUP):
                    hq = GROUP * kv + j
                    dq_ref[QBLK * qb:QBLK * (qb + 1), HEAD * hq:HEAD * (hq + 1)] = dqg[QBLK * j:QBLK * (j + 1), :]
                    dsink = dsink + jnp.where(lane == hq, _colsum(dsk[QBLK * j:QBLK * (j + 1), :]), 0.0)
        ds_ref[...] += dsink

    qspec = pl.BlockSpec((QPAIR * QBLK, N_Q * HEAD), lambda n: (n, 0))
    return _call(
        body, name="attn_bwd", grid=(nblk // QPAIR,),
        out_shape=[jax.ShapeDtypeStruct((s, N_Q * HEAD), F32),
                   jax.ShapeDtypeStruct((s, N_KV * HEAD), F32), jax.ShapeDtypeStruct((s, N_KV * HEAD), F32),
                   jax.ShapeDtypeStruct((1, 128), F32)],
        in_specs=[pl.BlockSpec(memory_space=pltpu.SMEM), qspec, qspec] + _kv_specs(nblk) + _kv_specs(nblk),
        out_specs=[qspec, _res((s, N_KV * HEAD)), _res((s, N_KV * HEAD)), pl.BlockSpec((1, 128), lambda n: (0, 0))],
        args=(sinks, q, do, kk, kk, kk, kk, v, v, v, v), rider=rider)


def _in_bwd_tail(dzb, w_ref, x_ref, mod_ref, k, dres_ref, dx_ref, h_ref, sums_ref):
    xv = x_ref[...]
    shift, scale, _ = _mod(mod_ref, k)
    h_ref[...] = (xv * (1.0 + scale) + shift).astype(BF16)
    dh = _nn(dzb, w_ref[...])
    dx_ref[...] = dres_ref[...] + dh * (1.0 + scale)
    sums_ref[0:1, :] += _colsum(dh)
    sums_ref[1:2, :] += _colsum(dh * xv)


def attn_in_bwd(dq, dk, dv, rope, x, mods, k, win_t, dres):
    s = x.shape[0]
    tm = _blk(s, TM_MM)

    def body(dq_ref, dk_ref, dv_ref, c_ref, s1_ref, s2_ref, x_ref, mod_ref, w_ref, dres_ref,
             dx_ref, dz_ref, h_ref, sums_ref):
        @pl.when(pl.program_id(0) == 0)
        def _():
            sums_ref[...] = jnp.zeros(sums_ref.shape, F32)

        cos, s1, s2 = c_ref[...], s1_ref[...], s2_ref[...]
        for hh in range(N_Q + N_KV):
            src = dq_ref[:, HEAD * hh:HEAD * (hh + 1)] if hh < N_Q else dk_ref[:, HEAD * (hh - N_Q):HEAD * (hh - N_Q + 1)]
            dz_ref[:, HEAD * hh:HEAD * (hh + 1)] = _rope_bwd(src, cos, s1, s2).astype(BF16)
        dz_ref[:, HEAD * (N_Q + N_KV):] = dv_ref[...].astype(BF16)
        _in_bwd_tail(dz_ref[...], w_ref, x_ref, mod_ref, k, dres_ref, dx_ref, h_ref, sums_ref)

    return pl.pallas_call(
        body, name="attn_in_bwd", grid=(s // tm,),
        out_shape=[jax.ShapeDtypeStruct((s, D_MODEL), F32), jax.ShapeDtypeStruct((s, D_QKV), BF16),
                   jax.ShapeDtypeStruct((s, D_MODEL), BF16), jax.ShapeDtypeStruct((8, D_MODEL), F32)],
        in_specs=[_row(tm, N_Q * HEAD), _row(tm, N_KV * HEAD), _row(tm, N_KV * HEAD),
                  _row(tm, HEAD), _row(tm, HEAD), _row(tm, HEAD), _row(tm, D_MODEL),
                  _res(mods.shape), _res(win_t.shape), _row(tm, D_MODEL)],
        out_specs=[_row(tm, D_MODEL), _row(tm, D_QKV), _row(tm, D_MODEL), _res((8, D_MODEL))],
        compiler_params=_params(),
    )(dq, dk, dv, *rope, x, mods, win_t, dres)


def _shift_blk(v, k, before, after, row):
    n = v.shape[0]
    r = pltpu.roll(v, k % n, 0)
    for j in range(abs(k)):
        if k > 0:
            r = jnp.where(row == j, before[8 - k + j:8 - k + j + 1, :], r)
        else:
            r = jnp.where(row == n + k + j, after[j:j + 1, :], r)
    return r


def rnn_in_bwd(dxc_f, dxc_b, xr, cw, dgt, x, mods, k, win_t, dres):
    s = x.shape[0]
    tm = _blk(s, TM_MM)
    n = s // tm

    def body(f_ref, fp_ref, fn_ref, b_ref, bp_ref, bn_ref, xr_ref, xp_ref, xn_ref, cw_ref, dgt_ref,
             x_ref, mod_ref, w_ref, dres_ref, dx_ref, dz_ref, h_ref, sums_ref, dcw_ref, dcb_ref):
        i = pl.program_id(0)

        @pl.when(i == 0)
        def _():
            sums_ref[...] = jnp.zeros(sums_ref.shape, F32)
            dcw_ref[...] = jnp.zeros(dcw_ref.shape, F32)
            dcb_ref[...] = jnp.zeros(dcb_ref.shape, F32)

        d = f_ref[...] + b_ref[...]
        xv = xr_ref[...]
        first, last = i == 0, i == n - 1
        d_before = jnp.where(first, 0.0, fp_ref[...] + bp_ref[...])
        d_after = jnp.where(last, 0.0, fn_ref[...] + bn_ref[...])
        x_before = jnp.where(first, 0.0, xp_ref[...])
        x_after = jnp.where(last, 0.0, xn_ref[...])
        row = lax.broadcasted_iota(jnp.int32, d.shape, 0)
        dxr = (cw_ref[0:1, :] * _shift_blk(d, -2, d_before, d_after, row)
               + cw_ref[1:2, :] * _shift_blk(d, -1, d_before, d_after, row)
               + cw_ref[2:3, :] * d + cw_ref[3:4, :] * _shift_blk(d, 1, d_before, d_after, row))
        dcw_ref[0:1, :] += _colsum(d * _shift_blk(xv, 2, x_before, x_after, row))
        dcw_ref[1:2, :] += _colsum(d * _shift_blk(xv, 1, x_before, x_after, row))
        dcw_ref[2:3, :] += _colsum(d * xv)
        dcw_ref[3:4, :] += _colsum(d * _shift_blk(xv, -1, x_before, x_after, row))
        dcb_ref[...] += _colsum(d)
        dz_ref[:, 0:D_RNN] = dxr.astype(BF16)
        dz_ref[:, D_RNN:2 * D_RNN] = dgt_ref[...]
        _in_bwd_tail(dz_ref[...], w_ref, x_ref, mod_ref, k, dres_ref, dx_ref, h_ref, sums_ref)

    per8 = tm // 8
    blk = _row(tm, D_RNN)
    before = pl.BlockSpec((8, D_RNN), lambda i: (jnp.maximum(i * per8 - 1, 0), 0))
    after = pl.BlockSpec((8, D_RNN), lambda i: (jnp.minimum((i + 1) * per8, s // 8 - 1), 0))
    return pl.pallas_call(
        body, name="rnn_in_bwd", grid=(n,),
        out_shape=[jax.ShapeDtypeStruct((s, D_MODEL), F32), jax.ShapeDtypeStruct((s, 2 * D_RNN), BF16),
                   jax.ShapeDtypeStruct((s, D_MODEL), BF16), jax.ShapeDtypeStruct((8, D_MODEL), F32),
                   jax.ShapeDtypeStruct((4, D_RNN), F32), jax.ShapeDtypeStruct((1, D_RNN), F32)],
        in_specs=[blk, before, after] * 3 + [_res(cw.shape), blk, _row(tm, D_MODEL), _res(mods.shape),
                                             _res(win_t.shape), _row(tm, D_MODEL)],
        out_specs=[_row(tm, D_MODEL), _row(tm, 2 * D_RNN), _row(tm, D_MODEL), _res((8, D_MODEL)),
                   _res((4, D_RNN)), _res((1, D_RNN))],
        compiler_params=_params(),
    )(dxc_f, dxc_f, dxc_f, dxc_b, dxc_b, dxc_b, xr, xr, xr, cw, dgt, x, mods, win_t, dres)


def wgrad(a, b, name, rider=None):
    s, m = a.shape
    n = b.shape[1]
    tm = next(t for t in (1024, 768, 512, 384, 256, 128) if m % t == 0)
    tk = _blk(s, TK_WG)
    nk = s // tk

    def body(a_ref, b_ref, o_ref, acc):
        kk = pl.program_id(1)

        @pl.when(kk == 0)
        def _():
            acc[...] = jnp.zeros(acc.shape, F32)

        acc[...] += _tn(a_ref[...], b_ref[...])

        @pl.when(kk == nk - 1)
        def _():
            o_ref[...] = acc[...].astype(BF16)

    out, *rode = _call(
        body, name=name, grid=(m // tm, nk),
        out_shape=[jax.ShapeDtypeStruct((m, n), BF16)],
        in_specs=[pl.BlockSpec((tk, tm), lambda i, kk: (kk, i)), pl.BlockSpec((tk, n), lambda i, kk: (kk, 0))],
        out_specs=[pl.BlockSpec((tm, n), lambda i, kk: (i, 0))],
        scratch_shapes=[pltpu.VMEM((tm, n), F32)],
        args=(a, b), rider=rider)
    out = out.reshape(N_DEV, m // N_DEV, n)
    return (out, *rode) if rider is not None else out


def part_sum(parts, name):
    _, r, c = parts.shape
    tr = next(t for t in (256, 192, 128, 64, 32, 16, 8) if r % t == 0)

    def body(p_ref, o_ref):
        acc = p_ref[0].astype(F32)
        for j in range(1, N_DEV):
            acc = acc + p_ref[j].astype(F32)
        o_ref[...] = acc

    return pl.pallas_call(
        body, name=name, grid=(r // tr,),
        out_shape=jax.ShapeDtypeStruct((r, c), F32),
        in_specs=[pl.BlockSpec((N_DEV, tr, c), lambda i: (0, i, 0))],
        out_specs=pl.BlockSpec((tr, c), lambda i: (i, 0)),
        compiler_params=_params(),
    )(parts)


def adamw(w, g, m, v, name):
    shape = w.shape
    c = shape[-1]
    r = w.size // c
    w2, g2, m2, v2 = (t.reshape(r, c) for t in (w, g, m, v))
    tr = r if r * c <= 512 * 1024 else next(t for t in (512, 256, 128, 64, 32, 16, 8) if r % t == 0)

    def body(w_ref, g_ref, m_ref, v_ref, d_ref, nm_ref, nv_ref):
        gv = g_ref[...]
        nm = B1 * m_ref[...] + (1.0 - B1) * gv
        nv = B2 * v_ref[...] + (1.0 - B2) * (gv * gv)
        nm_ref[...] = nm
        nv_ref[...] = nv
        m_hat = nm / (1.0 - B1 ** STEP)
        v_hat = nv / (1.0 - B2 ** STEP)
        d_ref[...] = -LR * (m_hat / (jnp.sqrt(v_hat) + ADAM_EPS) + WD * w_ref[...])

    spec = pl.BlockSpec((tr, c), lambda i: (i, 0))
    outs = pl.pallas_call(
        body, name=name, grid=(r // tr,),
        out_shape=[jax.ShapeDtypeStruct((r, c), F32)] * 3,
        in_specs=[spec] * 4, out_specs=[spec] * 3,
        compiler_params=_params(),
    )(w2, g2, m2, v2)
    return tuple(o.reshape(shape) for o in outs)


def _rope_tables(s):
    half = ROT // 2
    inv_freq = THETA ** (-jnp.arange(0, ROT, 2, dtype=F32) / ROT)
    per_row = 128 // half
    pos = (per_row * jnp.arange(s // per_row)[:, None] + jnp.arange(128)[None, :] // half).astype(F32)
    ang = pos * jnp.tile(inv_freq, per_row)[None, :]
    cos, sin = lax.optimization_barrier((jnp.cos(ang), jnp.sin(ang)))
    cos, sin = cos.reshape(s, half), sin.reshape(s, half)
    zeros = jnp.zeros((s, HEAD - ROT), F32)
    c = jnp.concatenate([cos, cos, jnp.ones((s, HEAD - ROT), F32)], axis=1)
    s1 = jnp.concatenate([jnp.zeros((s, half), F32), sin, zeros], axis=1)
    s2 = jnp.concatenate([-sin, jnp.zeros((s, half), F32), zeros], axis=1)
    return c, s1, s2


def _blockdiag(w):
    w4 = w.reshape(N_CG, 4, RB_W, RB_W)
    eye = jnp.eye(4, dtype=w.dtype)
    return jnp.einsum("gipq,ij->gipjq", w4, eye).reshape(N_CG, CG, CG)


def _diag_blocks(w):
    w5 = w.reshape(N_CG, 4, RB_W, 4, RB_W)
    eye = jnp.eye(4, dtype=w.dtype)
    return jnp.einsum("gipjq,ij->gipq", w5, eye).reshape(N_RB, RB_W, RB_W)


def _cols(full, per):
    lead = full.shape[:-1]
    t = full.reshape(lead + (N_DEV, per))
    return jnp.moveaxis(t, -2, 0).reshape(N_DEV, -1)


def kernel(x, c, ada_w, ada_b, ln_g, ln_b, attn_w_in, attn_w_out, attn_sinks, rnn_w_in, rnn_conv_w, rnn_conv_b, rnn_w_a, rnn_b_a, rnn_w_x, rnn_b_x, rnn_lam, rnn_w_out, mlp_w1, mlp_w2, loss_target, m_ada_w, m_ada_b, m_ln_g, m_ln_b, m_attn_w_in, m_attn_w_out, m_attn_sinks, m_rnn_w_in, m_rnn_conv_w, m_rnn_conv_b, m_rnn_w_a, m_rnn_b_a, m_rnn_w_x, m_rnn_b_x, m_rnn_lam, m_rnn_w_out, m_mlp_w1, m_mlp_w2, v_ada_w, v_ada_b, v_ln_g, v_ln_b, v_attn_w_in, v_attn_w_out, v_attn_sinks, v_rnn_w_in, v_rnn_conv_w, v_rnn_conv_b, v_rnn_w_a, v_rnn_b_a, v_rnn_w_x, v_rnn_b_x, v_rnn_lam, v_rnn_w_out, v_mlp_w1, v_mlp_w2):
    s = x.shape[1]
    x0 = x.reshape(s, D_MODEL)
    target = loss_target.reshape(s, D_MODEL)
    weights = dict(ada_w=ada_w, ada_b=ada_b, ln_g=ln_g, ln_b=ln_b, attn_w_in=attn_w_in, attn_w_out=attn_w_out,
                   attn_sinks=attn_sinks, rnn_w_in=rnn_w_in, rnn_conv_w=rnn_conv_w, rnn_conv_b=rnn_conv_b,
                   rnn_w_a=rnn_w_a, rnn_b_a=rnn_b_a, rnn_w_x=rnn_w_x, rnn_b_x=rnn_b_x, rnn_lam=rnn_lam,
                   rnn_w_out=rnn_w_out, mlp_w1=mlp_w1, mlp_w2=mlp_w2)
    moments_m = dict(ada_w=m_ada_w, ada_b=m_ada_b, ln_g=m_ln_g, ln_b=m_ln_b, attn_w_in=m_attn_w_in,
                     attn_w_out=m_attn_w_out, attn_sinks=m_attn_sinks, rnn_w_in=m_rnn_w_in,
                     rnn_conv_w=m_rnn_conv_w, rnn_conv_b=m_rnn_conv_b, rnn_w_a=m_rnn_w_a, rnn_b_a=m_rnn_b_a,
                     rnn_w_x=m_rnn_w_x, rnn_b_x=m_rnn_b_x, rnn_lam=m_rnn_lam, rnn_w_out=m_rnn_w_out,
                     mlp_w1=m_mlp_w1, mlp_w2=m_mlp_w2)
    moments_v = dict(ada_w=v_ada_w, ada_b=v_ada_b, ln_g=v_ln_g, ln_b=v_ln_b, attn_w_in=v_attn_w_in,
                     attn_w_out=v_attn_w_out, attn_sinks=v_attn_sinks, rnn_w_in=v_rnn_w_in,
                     rnn_conv_w=v_rnn_conv_w, rnn_conv_b=v_rnn_conv_b, rnn_w_a=v_rnn_w_a, rnn_b_a=v_rnn_b_a,
                     rnn_w_x=v_rnn_w_x, rnn_b_x=v_rnn_b_x, rnn_lam=v_rnn_lam, rnn_w_out=v_rnn_w_out,
                     mlp_w1=v_mlp_w1, mlp_w2=v_mlp_w2)
    names = list(weights)

    def t16(w):
        return w.T.astype(BF16)

    big = [t16(attn_w_in[0]), attn_w_out[0].astype(BF16), t16(rnn_w_in[0]), rnn_w_out[0].astype(BF16),
           t16(mlp_w1[0]), mlp_w2[0].astype(BF16), t16(mlp_w1[1]), mlp_w2[1].astype(BF16)]
    small_local = jnp.concatenate([
        ln_g.reshape(-1), ln_b.reshape(-1), rnn_conv_w.reshape(-1), rnn_conv_b.reshape(-1),
        rnn_b_a.reshape(-1), rnn_b_x.reshape(-1), rnn_lam.reshape(-1)])
    small_local = jnp.pad(small_local, (0, 4096 - small_local.shape[0])).reshape(32, 128)
    flat = lambda g: g.reshape(N_DEV * g.shape[1], D_MODEL)
    c_all, modr, win_t, sm = ada_modulation(jnp.broadcast_to(c, (8, D_MODEL)), ada_w.reshape(4, D_MODEL, CG),
                                            ada_b.reshape(4, 1, CG), _Gather([big[0], small_local]))
    win_t = flat(win_t)
    sm = sm.reshape(N_DEV, 4096)

    def full_vec(off, rows, per):
        piece = sm[:, off:off + rows * per].reshape(N_DEV, rows, per)
        return jnp.moveaxis(piece, 0, 1).reshape(rows, N_DEV * per)

    lng_f, lnb_f = full_vec(0, 4, 128), full_vec(512, 4, 128)
    cw_f, cb_f = full_vec(1024, 4, 192), full_vec(1792, 1, 192)
    ba_f, bx_f, lam_f = full_vec(1984, 2, 192), full_vec(2368, 2, 192), full_vec(2752, 2, 192)
    wa_bd = [_blockdiag(rnn_w_a[0, d]).astype(BF16) for d in range(2)]
    wx_bd = [_blockdiag(rnn_w_x[0, d]).astype(BF16) for d in range(2)]

    mods = modr.reshape(N_DEV, 4, 8, CG)[:, :, 0, :]
    mods = jnp.moveaxis(mods, 0, 1).reshape(4, 3, D_MODEL).reshape(12, D_MODEL)
    rope = _rope_tables(s)
    ln = lambda k: (lng_f[k:k + 1], lnb_f[k:k + 1])

    q, kk, v, wout = attn_in_fwd(x0, mods, 0, win_t, rope, rider=_Gather([big[1]]))
    wout = flat(wout)
    o, *got = attn_fwd(q, kk, v, attn_sinks, rider=_Gather([big[4], big[5]]))
    w1t_0, w2_0 = (flat(g) for g in got)
    x1, y0, rout = post_fwd(o, wout, x0, mods, 0, *ln(0), rider=_Gather([big[3]]))
    rout = flat(rout)
    x2, y1, ra0, r0, *got = mlp_fwd(x1, mods, 1, w1t_0, w2_0, *ln(1), rider=_Gather([big[2], big[6], big[7]]))
    rin_t, w1t_1, w2_1 = (flat(g) for g in got)
    xr, gt = rnn_in_fwd(x2, mods, 2, rin_t)
    xc = conv_fwd(xr, cw_f, cb_f)
    hf = lru_fwd(xc, wa_bd[0], wx_bd[0], ba_f[0:1], bx_f[0:1], lam_f[0:1], False)
    hb = lru_fwd(xc, wa_bd[1], wx_bd[1], ba_f[1:2], bx_f[1:2], lam_f[1:2], True)
    x3, y2, ypre = post_fwd(None, rout, x2, mods, 2, *ln(2), gate_act=(gt, hf, hb))
    y3, ra1, r1 = mlp_fwd(x3, mods, 3, w1t_1, w2_1, *ln(3), last=True)

    dx3, da1, h3, dy3, sums3 = mlp_bwd(target, x3, y3, ra1, mods, 3, w1t_1, w2_1, lng_f[3:4], lnb=lnb_f[3:4])
    g_w1t_1 = wgrad(da1, h3, "wgrad_w1_1")
    g_w2_1 = wgrad(r1, dy3, "wgrad_w2_1")
    dres2, dy2, sums2a, dhs, dgt, p_w1t_1 = post_bwd(dx3, x2, y2, mods, 2, rout, lng_f[2:3], gate_act=(gt, hf, hb),
                                                     rider=_AllToAll([g_w1t_1]))
    g_rout = wgrad(ypre, dy2, "wgrad_rnn_out")
    dxc_f, dwa_f, dwx_f, dba_f, dbx_f, dlam_f, p_w2_1, p_rout = lru_bwd(
        xc, dhs, hf, wa_bd[0], wx_bd[0], ba_f[0:1], bx_f[0:1], lam_f[0:1], False, rider=_AllToAll([g_w2_1, g_rout]))
    dxc_b, dwa_b, dwx_b, dba_b, dbx_b, dlam_b = lru_bwd(xc, dhs, hb, wa_bd[1], wx_bd[1], ba_f[1:2], bx_f[1:2],
                                                        lam_f[1:2], True)
    dx2, dzz, h2, sums2b, dcw, dcb = rnn_in_bwd(dxc_f, dxc_b, xr, cw_f, dgt, x2, mods, 2, rin_t, dres2)
    g_rin_t = wgrad(dzz, h2, "wgrad_rnn_in")
    d_wa = jnp.stack([_diag_blocks(dwa_f), _diag_blocks(dwa_b)])
    d_wx = jnp.stack([_diag_blocks(dwx_f), _diag_blocks(dwx_b)])
    nflat = d_wa.size // N_DEV
    gates = jnp.concatenate([d_wa.reshape(N_DEV, nflat), d_wx.reshape(N_DEV, nflat)], axis=1)
    gates = gates.reshape(N_DEV, 2 * nflat // 128, 128)
    dx1, da0, h1, dy1, sums1, p_rin_t, p_gates = mlp_bwd(dx2, x1, y1, ra0, mods, 1, w1t_0, w2_0, lng_f[1:2],
                                                         rider=_AllToAll([g_rin_t, gates]))
    gates_sum = part_sum(p_gates, "part_sum_gates")
    g_w1t_0 = wgrad(da0, h1, "wgrad_w1_0")
    g_w2_0, p_w1t_0 = wgrad(r0, dy1, "wgrad_w2_0", rider=_AllToAll([g_w1t_0]))
    dres0, dy0, sums0a, do = post_bwd(dx1, x0, y0, mods, 0, wout, lng_f[0:1])
    g_wout = wgrad(o, dy0, "wgrad_attn_out")
    dq, dk, dv, dsink, wag, p_w2_0, p_wout = attn_bwd(
        q, kk, v, do, attn_sinks, rider=_Multi(_Gather([gates_sum]), _AllToAll([g_w2_0, g_wout])))
    dx0, dqkv, h0, sums0b = attn_in_bwd(dq, dk, dv, rope, x0, mods, 0, win_t, dres0)
    g_win_t = wgrad(dqkv, h0, "wgrad_attn_in")

    sums = [sums0a + sums0b, sums1, sums2a + sums2b, sums3]
    gmod = jnp.stack([t[0:3] for t in sums])
    gsend = jnp.moveaxis(gmod.reshape(4, N_DEV, CG), 1, 0)
    gsend = jnp.pad(gsend, ((0, 0), (0, 4), (0, 0)))
    c_t = c_all[:, 0, :].T
    sq_err = jnp.sum(sums3[5]).reshape(1, 1)
    tail = jnp.concatenate([
        _cols(dcw, 192), _cols(dcb, 192),
        _cols(jnp.concatenate([dba_f, dba_b]), 192), _cols(jnp.concatenate([dbx_f, dbx_b]), 192),
        _cols(jnp.concatenate([dlam_f, dlam_b]), 192),
        _cols(jnp.stack([t[3] for t in sums]), 128), _cols(jnp.stack([t[4] for t in sums]), 128),
        jnp.broadcast_to(dsink[:, 0:8], (N_DEV, 8)), jnp.broadcast_to(sq_err, (N_DEV, 1))], axis=1)
    tail = jnp.pad(tail, ((0, 0), (0, 32 * 128 - tail.shape[1]))).reshape(N_DEV, 32, 128)
    g_ada_w, g_ada_b, red, p_win_t = epilogue(gsend, c_t, tail, _AllToAll([g_win_t]))
    grads = {"ada_w": g_ada_w.reshape(ada_w.shape), "ada_b": g_ada_b[0:4].reshape(ada_b.shape)}

    big_parts = [p_win_t, p_wout, p_rin_t, p_rout, p_w1t_0, p_w2_0, p_w1t_1, p_w2_1]
    gsum = [part_sum(p, "part_sum_%d" % i) for i, p in enumerate(big_parts)]
    grads.update({
        "attn_w_in": gsum[0].T[None], "attn_w_out": gsum[1][None],
        "rnn_w_in": gsum[2].T[None], "rnn_w_out": gsum[3][None],
        "mlp_w1": jnp.stack([gsum[4].T, gsum[6].T]), "mlp_w2": jnp.stack([gsum[5], gsum[7]]),
    })
    wag = wag.reshape(N_DEV, 2 * nflat)
    grads["rnn_w_a"] = wag[:, :nflat].reshape(rnn_w_a.shape)
    grads["rnn_w_x"] = wag[:, nflat:].reshape(rnn_w_x.shape)
    tl = red.reshape(-1)
    loss = 0.5 * tl[3144] / D_MODEL
    grads["rnn_conv_w"] = tl[0:768].reshape(rnn_conv_w.shape)
    grads["rnn_conv_b"] = tl[768:960].reshape(rnn_conv_b.shape)
    grads["rnn_b_a"] = tl[960:1344].reshape(rnn_b_a.shape)
    grads["rnn_b_x"] = tl[1344:1728].reshape(rnn_b_x.shape)
    grads["rnn_lam"] = tl[1728:2112].reshape(rnn_lam.shape)
    grads["ln_g"] = tl[2112:2624].reshape(ln_g.shape)
    grads["ln_b"] = tl[2624:3136].reshape(ln_b.shape)
    grads["attn_sinks"] = tl[3136:3144].reshape(attn_sinks.shape)

    delta, new_m, new_v = {}, {}, {}
    for n in names:
        delta[n], new_m[n], new_v[n] = adamw(weights[n], grads[n], moments_m[n], moments_v[n], "adamw_" + n)
    return (loss, dx0.reshape(x.shape), *[grads[n] for n in names], *[delta[n] for n in names],
            *[new_m[n] for n in names], *[new_v[n] for n in names])
```

```python
import functools
import math

import jax
import jax.numpy as jnp
from jax import lax
from jax.experimental import pallas as pl
from jax.experimental.pallas import tpu as pltpu

F32, BF16 = jnp.float32, jnp.bfloat16
MESH = pl.DeviceIdType.MESH

D_MODEL = 1024
N_Q, N_KV, HEAD = 8, 2, 128
ROT, THETA = 32, 500000.0
QBLK = 128
D_QKV = (N_Q + 2 * N_KV) * HEAD
D_RNN, N_RB, RB_W = 1536, 16, 96
CG = 384
N_CG = D_RNN // CG
D_FF = 4096
FF_CHUNK = 1024
DEPTH = 2
ALPHA = (2.0 * DEPTH) ** 0.25
LN_EPS = 1e-5
LRU_C = 8.0
N_DEV = 8
LR, B1, B2, ADAM_EPS, WD, STEP = 0.001, 0.9, 0.999, 1e-8, 0.01, 10

VMEM_LIMIT = 56 * 1024 * 1024
TM_MM = 512
TM_MLP = 256
TM_MLP_FWD = 512
TT_RNN = 512
SB_RNN = 512
TK_WG = 2048


def _nn(a, b):
    return jnp.dot(a, b, preferred_element_type=F32)


def _nt(a, b):
    return lax.dot_general(a, b, (((1,), (1,)), ((), ())), preferred_element_type=F32)


def _tn(a, b):
    return lax.dot_general(a, b, (((0,), (0,)), ((), ())), preferred_element_type=F32)


def _blk(n, pref):
    t = min(n, pref)
    assert n % t == 0, (n, pref)
    return t


def _params(**kw):
    return pltpu.CompilerParams(vmem_limit_bytes=VMEM_LIMIT, **kw)


def _row(tm, w):
    return pl.BlockSpec((tm, w), lambda i: (i, 0))


def _res(shape):
    return pl.BlockSpec(shape, lambda i: (0,) * len(shape), pipeline_mode=pl.Buffered(1))


def _mod(mod_ref, k):
    return mod_ref[3 * k:3 * k + 1, :], mod_ref[3 * k + 1:3 * k + 2, :], mod_ref[3 * k + 2:3 * k + 3, :]


def _ln_stats(z):
    mu = jnp.mean(z, axis=-1, keepdims=True)
    zc = z - mu
    var = jnp.mean(zc * zc, axis=-1, keepdims=True)
    rstd = lax.rsqrt(var + LN_EPS)
    return zc * rstd, rstd


def _ln_bwd(dxo, xhat, rstd, g):
    dxh = dxo * g
    m1 = jnp.mean(dxh, axis=-1, keepdims=True)
    m2 = jnp.mean(dxh * xhat, axis=-1, keepdims=True)
    return rstd * (dxh - m1 - xhat * m2)


def _colsum(v):
    return jnp.sum(v, axis=0, keepdims=True)


def _sigmoid(v):
    return 0.5 * jnp.tanh(0.5 * v) + 0.5


def _gelu_parts(v):
    k = math.sqrt(2.0 / math.pi)
    u = k * (v + 0.044715 * v * v * v)
    t = jnp.tanh(u)
    g = 0.5 * v * (1.0 + t)
    dg = 0.5 * (1.0 + t) + 0.5 * v * (1.0 - t * t) * k * (1.0 + 3.0 * 0.044715 * v * v)
    return g, dg


def _me():
    return lax.axis_index("x"), lax.axis_index("y"), lax.axis_index("c")


def _idx(p):
    return 4 * p[0] + 2 * p[1] + p[2]


def _peers(me):
    x, y, c = me
    out = []
    for k in range(1, N_DEV):
        out.append((1 - x if k & 4 else x, 1 - y if k & 2 else y, 1 - c if k & 1 else c))
    return out


class _Gather:
    def __init__(self, srcs):
        self.srcs = list(srcs)
        n = len(self.srcs)
        self.out_shape = [jax.ShapeDtypeStruct((N_DEV,) + s.shape, s.dtype) for s in self.srcs]
        self.scratch = [pltpu.SemaphoreType.DMA((n, 7)), pltpu.SemaphoreType.DMA((n, 7)),
                        pltpu.SemaphoreType.DMA((n,))]

    @staticmethod
    def _places():
        x, y, c = me = _me()
        return me, (x, y, 1 - c), [(1 - x, y), (x, 1 - y), (1 - x, 1 - y)]

    @staticmethod
    def _copy(outs, sems, t, k, block, to, src=None):
        slot = outs[t].at[_idx(block)]
        return pltpu.make_async_remote_copy(
            src_ref=slot if src is None else src, dst_ref=slot, send_sem=sems[0].at[t, k],
            recv_sem=sems[1].at[t, k], device_id=to, device_id_type=MESH)

    def _firsts(self, ins, outs, sems):
        me, sibling, chips = self._places()
        out = []
        for t in range(len(ins)):
            out.append(self._copy(outs, sems, t, 0, me, sibling, src=ins[t]))
            out += [self._copy(outs, sems, t, 1 + j, me, (*chip, me[2]), src=ins[t]) for j, chip in enumerate(chips)]
        return out

    def _locals(self, ins, outs, sems):
        me = _me()
        return [pltpu.make_async_copy(ins[t], outs[t].at[_idx(me)], sems[2].at[t]) for t in range(len(ins))]

    def start(self, ins, outs, sems):
        for cp in self._locals(ins, outs, sems) + self._firsts(ins, outs, sems):
            cp.start()

    def mid(self, ins, outs, sems):
        me, sibling, chips = self._places()
        for j, chip in enumerate(chips):
            for t in range(len(ins)):
                self._copy(outs, sems, t, 1 + j, (*chip, me[2]), me).wait_recv()
                self._copy(outs, sems, t, 4 + j, (*chip, me[2]), sibling).start()

    def finish(self, ins, outs, sems):
        me, sibling, chips = self._places()
        for t in range(len(ins)):
            self._copy(outs, sems, t, 0, sibling, me).wait_recv()
            for j, chip in enumerate(chips):
                self._copy(outs, sems, t, 4 + j, (*chip, 1 - me[2]), me).wait_recv()
        for cp in self._firsts(ins, outs, sems):
            cp.wait_send()
        for j, chip in enumerate(chips):
            for t in range(len(ins)):
                self._copy(outs, sems, t, 4 + j, (*chip, me[2]), sibling).wait_send()
        for cp in self._locals(ins, outs, sems):
            cp.wait()


class _AllToAll:
    def __init__(self, srcs):
        self.srcs = list(srcs)
        n = len(self.srcs)
        self.out_shape = [jax.ShapeDtypeStruct(s.shape, s.dtype) for s in self.srcs]
        self.scratch = [pltpu.SemaphoreType.DMA((n, 7)), pltpu.SemaphoreType.DMA((n, 7)),
                        pltpu.SemaphoreType.DMA((n,))]

    def _copies(self, ins, outs, sems):
        me = _me()
        loc, rem = [], []
        for t in range(len(ins)):
            loc.append(pltpu.make_async_copy(ins[t].at[_idx(me)], outs[t].at[_idx(me)], sems[2].at[t]))
            for k, p in enumerate(_peers(me)):
                rem.append(pltpu.make_async_remote_copy(
                    src_ref=ins[t].at[_idx(p)], dst_ref=outs[t].at[_idx(me)], send_sem=sems[0].at[t, k],
                    recv_sem=sems[1].at[t, k], device_id=p, device_id_type=MESH))
        return loc, rem

    def start(self, ins, outs, sems):
        loc, rem = self._copies(ins, outs, sems)
        for cp in loc + rem:
            cp.start()

    def mid(self, ins, outs, sems):
        pass

    def finish(self, ins, outs, sems):
        me = _me()
        for t in range(len(ins)):
            for k, p in enumerate(_peers(me)):
                slot = outs[t].at[_idx(p)]
                pltpu.make_async_remote_copy(
                    src_ref=slot, dst_ref=slot, send_sem=sems[0].at[t, k], recv_sem=sems[1].at[t, k],
                    device_id=p, device_id_type=MESH).wait_recv()
        loc, rem = self._copies(ins, outs, sems)
        for cp in rem:
            cp.wait_send()
        for cp in loc:
            cp.wait()


class _Multi:
    def __init__(self, *exs):
        self.exs = exs
        self.srcs = [s for e in exs for s in e.srcs]
        self.out_shape = [s for e in exs for s in e.out_shape]
        self.scratch = [s for e in exs for s in e.scratch]

    def _each(self, ins, outs, sems):
        i = j = 0
        for e in self.exs:
            n, m = len(e.srcs), len(e.scratch)
            yield e, ins[i:i + n], outs[i:i + n], sems[j:j + m]
            i, j = i + n, j + m

    def start(self, ins, outs, sems):
        for e, a, b, c in self._each(ins, outs, sems):
            e.start(a, b, c)

    def mid(self, ins, outs, sems):
        for e, a, b, c in self._each(ins, outs, sems):
            e.mid(a, b, c)

    def finish(self, ins, outs, sems):
        for e, a, b, c in self._each(ins, outs, sems):
            e.finish(a, b, c)


def _call(body, *, name, grid, in_specs, out_specs, out_shape, args, scratch_shapes=(), rider=None):
    in_specs, out_specs, out_shape = list(in_specs), list(out_specs), list(out_shape)
    scratch_shapes = list(scratch_shapes)
    if rider is None:
        return pl.pallas_call(body, name=name, grid=grid, out_shape=out_shape, in_specs=in_specs,
                              out_specs=out_specs, scratch_shapes=scratch_shapes, compiler_params=_params())(*args)
    nci, nco, ncs, nr = len(in_specs), len(out_shape), len(scratch_shapes), len(rider.srcs)
    nsteps = math.prod(grid)
    assert nsteps >= 2, (name, grid)
    mid = max(1, (7 * nsteps) // 8)

    def full(*refs):
        ci, ri = refs[:nci], refs[nci:nci + nr]
        co, ro = refs[nci + nr:nci + nr + nco], refs[nci + nr + nco:nci + 2 * nr + nco]
        cs, rs = refs[nci + 2 * nr + nco:nci + 2 * nr + nco + ncs], refs[nci + 2 * nr + nco + ncs:]
        step = pl.program_id(0)
        for d in range(1, len(grid)):
            step = step * grid[d] + pl.program_id(d)

        @pl.when(step == 0)
        def _():
            rider.start(ri, ro, rs)

        @pl.when(step == mid)
        def _():
            rider.mid(ri, ro, rs)

        body(*ci, *co, *cs)

        @pl.when(step == nsteps - 1)
        def _():
            rider.finish(ri, ro, rs)

    any_spec = pl.BlockSpec(memory_space=pl.ANY)
    return pl.pallas_call(
        full, name=name, grid=grid, out_shape=out_shape + rider.out_shape,
        in_specs=in_specs + [any_spec] * nr, out_specs=out_specs + [any_spec] * nr,
        scratch_shapes=scratch_shapes + rider.scratch, compiler_params=_params(),
    )(*args, *rider.srcs)


def _a2a_start(srcs, dsts, send_sems, recv_sems, local_sems, me, sem_base=0):
    peers = _peers(me)
    started = []
    for t in range(len(srcs)):
        loc = pltpu.make_async_copy(srcs[t].at[_idx(me)], dsts[t].at[_idx(me)], local_sems.at[sem_base + t])
        loc.start()
        started.append(("local", loc))
        for k, p in enumerate(peers):
            cp = pltpu.make_async_remote_copy(
                src_ref=srcs[t].at[_idx(p)], dst_ref=dsts[t].at[_idx(me)],
                send_sem=send_sems.at[sem_base + t, k], recv_sem=recv_sems.at[sem_base + t, k],
                device_id=p, device_id_type=MESH)
            cp.start()
            started.append(("remote", cp))
    return started


def _a2a_finish(started, dsts, send_sems, recv_sems, me, sem_base=0):
    peers = _peers(me)
    for t in range(len(dsts)):
        for k, p in enumerate(peers):
            slot = dsts[t].at[_idx(p)]
            pltpu.make_async_remote_copy(
                src_ref=slot, dst_ref=slot, send_sem=send_sems.at[sem_base + t, k],
                recv_sem=recv_sems.at[sem_base + t, k], device_id=p, device_id_type=MESH).wait_recv()
    for kind, cp in started:
        if kind == "local":
            cp.wait()
        else:
            cp.wait_send()


def ada_modulation(c8, ada_w, ada_b, ride):
    nr = len(ride.srcs)

    def body(c_ref, w_ref, b_ref, *rest):
        ride_in, (call_ref, modr_ref), ride_out = rest[:nr], rest[nr:nr + 2], rest[nr + 2:2 * nr + 2]
        modp, send_sems, recv_sems, local_sems = rest[2 * nr + 2:2 * nr + 6]
        ride_sems = rest[2 * nr + 6:]
        ride.start(ride_in, ride_out, ride_sems)
        me = _me()
        peers = _peers(me)
        sends = []
        for k, p in enumerate(peers):
            cp = pltpu.make_async_remote_copy(
                src_ref=c_ref, dst_ref=call_ref.at[_idx(me)], send_sem=send_sems.at[0, k],
                recv_sem=recv_sems.at[0, k], device_id=p, device_id_type=MESH)
            cp.start()
            sends.append(cp)
        call_ref[_idx(me)] = c_ref[...]
        for k, p in enumerate(peers):
            slot = call_ref.at[_idx(p)]
            pltpu.make_async_remote_copy(
                src_ref=slot, dst_ref=slot, send_sem=send_sems.at[0, k], recv_sem=recv_sems.at[0, k],
                device_id=p, device_id_type=MESH).wait_recv()
        for cp in sends:
            cp.wait_send()
        cv = call_ref[...].reshape(N_DEV * 8, D_MODEL)
        s = (cv * _sigmoid(cv)).astype(BF16)
        for k in range(4):
            res = _nn(s, w_ref[k].astype(BF16)) + b_ref[k]
            for j in range(N_DEV):
                modp[j, 8 * k:8 * k + 8, :] = res[8 * j:8 * j + 8, :]
        started = _a2a_start([modp], [modr_ref], send_sems, recv_sems, local_sems, me, sem_base=1)
        _a2a_finish(started, [modr_ref], send_sems, recv_sems, me, sem_base=1)
        ride.mid(ride_in, ride_out, ride_sems)
        ride.finish(ride_in, ride_out, ride_sems)

    vm, hbm = pl.BlockSpec(memory_space=pltpu.VMEM), pl.BlockSpec(memory_space=pl.ANY)
    return pl.pallas_call(
        body, name="ada_modulation",
        out_shape=[jax.ShapeDtypeStruct((N_DEV, 8, D_MODEL), F32), jax.ShapeDtypeStruct((N_DEV, 32, CG), F32)]
        + ride.out_shape,
        in_specs=[vm, vm, vm] + [hbm] * nr, out_specs=[vm, vm] + [hbm] * nr,
        scratch_shapes=[pltpu.VMEM((N_DEV, 32, CG), F32), pltpu.SemaphoreType.DMA((2, 7)),
                        pltpu.SemaphoreType.DMA((2, 7)), pltpu.SemaphoreType.DMA((2,))] + ride.scratch,
        compiler_params=_params(),
    )(c8, ada_w, ada_b, *ride.srcs)


def epilogue(gsend, c_t, tail, ride):
    nr = len(ride.srcs)
    rt = tail.shape[1]

    def body(g_ref, ct_ref, t_ref, *rest):
        ride_in, (gw_ref, gb_ref, red_ref), ride_out = rest[:nr], rest[nr:nr + 3], rest[nr + 3:2 * nr + 3]
        grecv, trecv, send_sems, recv_sems, local_sems = rest[2 * nr + 3:2 * nr + 8]
        ride_sems = rest[2 * nr + 8:]
        ride.start(ride_in, ride_out, ride_sems)
        me = _me()
        started = _a2a_start([g_ref, t_ref], [grecv, trecv], send_sems, recv_sems, local_sems, me)
        _a2a_finish(started, [grecv, trecv], send_sems, recv_sems, me)
        acc = trecv[0]
        for j in range(1, N_DEV):
            acc = acc + trecv[j]
        red_ref[...] = acc
        ct = ct_ref[...]
        st = (ct * _sigmoid(ct)).astype(BF16).astype(F32)
        gb = jnp.zeros((8, CG), F32)
        for b in range(N_DEV):
            gb = gb + grecv[b]
        gb_ref[...] = gb
        for k in range(4):
            acc = jnp.zeros((D_MODEL, CG), F32)
            for b in range(N_DEV):
                row = grecv[b, k:k + 1, :].astype(BF16).astype(F32)
                acc = acc + st[:, b:b + 1] * row
            gw_ref[k] = acc
        ride.mid(ride_in, ride_out, ride_sems)
        ride.finish(ride_in, ride_out, ride_sems)

    vm, hbm = pl.BlockSpec(memory_space=pltpu.VMEM), pl.BlockSpec(memory_space=pl.ANY)
    return pl.pallas_call(
        body, name="epilogue",
        out_shape=[jax.ShapeDtypeStruct((4, D_MODEL, CG), F32), jax.ShapeDtypeStruct((8, CG), F32),
                   jax.ShapeDtypeStruct((rt, 128), F32)] + ride.out_shape,
        in_specs=[vm, vm, vm] + [hbm] * nr, out_specs=[vm, vm, vm] + [hbm] * nr,
        scratch_shapes=[pltpu.VMEM((N_DEV, 8, CG), F32), pltpu.VMEM((N_DEV, rt, 128), F32),
                        pltpu.SemaphoreType.DMA((2, 7)), pltpu.SemaphoreType.DMA((2, 7)),
                        pltpu.SemaphoreType.DMA((2,))] + ride.scratch,
        compiler_params=_params(),
    )(gsend, c_t, tail, *ride.srcs)


def _rope(t, cos, s1, s2):
    return t * cos + pltpu.roll(t, 16, 1) * s1 + pltpu.roll(t, HEAD - 16, 1) * s2


def _rope_bwd(d, cos, s1, s2):
    return d * cos + pltpu.roll(d * s1, HEAD - 16, 1) + pltpu.roll(d * s2, 16, 1)


def attn_in_fwd(x, mods, k, win_t, rope, rider=None):
    s = x.shape[0]
    tm = _blk(s, TM_MM)

    def body(x_ref, mod_ref, w_ref, c_ref, s1_ref, s2_ref, q_ref, k_ref, v_ref):
        shift, scale, _ = _mod(mod_ref, k)
        h = (x_ref[...] * (1.0 + scale) + shift).astype(BF16)
        qkv = _nt(h, w_ref[...])
        cos, s1, s2 = c_ref[...], s1_ref[...], s2_ref[...]
        for hh in range(N_Q + N_KV):
            r = _rope(qkv[:, HEAD * hh:HEAD * (hh + 1)], cos, s1, s2).astype(BF16)
            if hh < N_Q:
                q_ref[:, HEAD * hh:HEAD * (hh + 1)] = r
            else:
                k_ref[:, HEAD * (hh - N_Q):HEAD * (hh - N_Q + 1)] = r
        v_ref[...] = qkv[:, HEAD * (N_Q + N_KV):].astype(BF16)

    return _call(
        body, name="attn_in_fwd", grid=(s // tm,),
        out_shape=[jax.ShapeDtypeStruct((s, N_Q * HEAD), BF16), jax.ShapeDtypeStruct((s, N_KV * HEAD), BF16),
                   jax.ShapeDtypeStruct((s, N_KV * HEAD), BF16)],
        in_specs=[_row(tm, D_MODEL), _res(mods.shape), _res(win_t.shape),
                  _row(tm, HEAD), _row(tm, HEAD), _row(tm, HEAD)],
        out_specs=[_row(tm, N_Q * HEAD), _row(tm, N_KV * HEAD), _row(tm, N_KV * HEAD)],
        args=(x, mods, win_t, *rope), rider=rider)


QPAIR = 2


def _kv_specs(nblk):
    w = N_KV * HEAD
    return [pl.BlockSpec((QBLK, w), lambda n: (jnp.maximum(QPAIR * n - 1, 0), 0)),
            pl.BlockSpec((QBLK, w), lambda n: (QPAIR * n, 0)),
            pl.BlockSpec((QBLK, w), lambda n: (QPAIR * n + 1, 0)),
            pl.BlockSpec((QBLK, w), lambda n: (jnp.minimum(QPAIR * n + 2, nblk - 1), 0))]


GROUP = N_Q // N_KV


def _attn_mask(n, s):
    qi = lax.broadcasted_iota(jnp.int32, (GROUP * QBLK, 3 * QBLK), 0) & (QBLK - 1)
    kj = lax.broadcasted_iota(jnp.int32, (GROUP * QBLK, 3 * QBLK), 1)
    rel = kj - QBLK - qi
    kpos = kj + (n - 1) * QBLK
    return (jnp.abs(rel) <= QBLK) & (kpos >= 0) & (kpos < s)


def _stack_heads(ref, qb, kv):
    rows = slice(QBLK * qb, QBLK * (qb + 1))
    return jnp.concatenate([ref[rows, HEAD * (GROUP * kv + j):HEAD * (GROUP * kv + j + 1)] for j in range(GROUP)],
                           axis=0)


def _stack_sinks(sink_ref, kv):
    row = lax.broadcasted_iota(jnp.int32, (GROUP * QBLK, 1), 0)
    out = jnp.full((GROUP * QBLK, 1), sink_ref[0, GROUP * kv + GROUP - 1], F32)
    for j in range(GROUP - 2, -1, -1):
        out = jnp.where(row < QBLK * (j + 1), sink_ref[0, GROUP * kv + j], out)
    return out


def _attn_probs(qh, kh, valid, sink):
    sc = _nt(qh, kh) * (HEAD ** -0.5)
    sc = jnp.where(valid, sc, -1e30)
    m = jnp.maximum(jnp.max(sc, axis=-1, keepdims=True), sink)
    p = jnp.exp(sc - m)
    es = jnp.exp(sink - m)
    denom = jnp.sum(p, axis=-1, keepdims=True) + es
    return p / denom, es / denom


def attn_fwd(q, kk, v, sinks, rider=None):
    s = q.shape[0]
    nblk = s // QBLK

    def body(sink_ref, q_ref, k0, k1, k2, k3, v0, v1, v2, v3, o_ref):
        n = pl.program_id(0)
        kall = jnp.concatenate([k0[...], k1[...], k2[...], k3[...]], axis=0)
        vall = jnp.concatenate([v0[...], v1[...], v2[...], v3[...]], axis=0)
        for qb in range(QPAIR):
            valid = _attn_mask(QPAIR * n + qb, s)
            keys = slice(QBLK * qb, QBLK * (qb + 3))
            for kv in range(N_KV):
                cols = slice(HEAD * kv, HEAD * (kv + 1))
                probs, _ = _attn_probs(_stack_heads(q_ref, qb, kv), kall[keys, cols], valid,
                                       _stack_sinks(sink_ref, kv))
                og = _nn(probs.astype(BF16), vall[keys, cols]).astype(BF16)
                for j in range(GROUP):
                    hq = GROUP * kv + j
                    o_ref[QBLK * qb:QBLK * (qb + 1), HEAD * hq:HEAD * (hq + 1)] = og[QBLK * j:QBLK * (j + 1), :]

    qspec = pl.BlockSpec((QPAIR * QBLK, N_Q * HEAD), lambda n: (n, 0))
    return _call(
        body, name="attn_fwd", grid=(nblk // QPAIR,),
        out_shape=[jax.ShapeDtypeStruct((s, N_Q * HEAD), BF16)],
        in_specs=[pl.BlockSpec(memory_space=pltpu.SMEM), qspec] + _kv_specs(nblk) + _kv_specs(nblk),
        out_specs=[qspec],
        args=(sinks, q, kk, kk, kk, kk, v, v, v, v), rider=rider)


def post_fwd(ypre, w, x, mods, k, lng, lnb, gate_act=None, rider=None):
    s = x.shape[0]
    tm = _blk(s, TM_MM)
    kdim = w.shape[0]
    rnn = gate_act is not None

    def body(*refs):
        if rnn:
            gt_ref, hs_ref, w_ref, x_ref, mod_ref, g_ref, b_ref, xo_ref, y_ref, yp_ref = refs
            act, _ = _gelu_parts(gt_ref[...].astype(F32))
            yp = (hs_ref[...].astype(F32) * act).astype(BF16)
            yp_ref[...] = yp
        else:
            yp_ref, w_ref, x_ref, mod_ref, g_ref, b_ref, xo_ref, y_ref = refs
            yp = yp_ref[...]
        _, _, gate = _mod(mod_ref, k)
        y = _nn(yp, w_ref[...])
        y_ref[...] = y
        xhat, _ = _ln_stats(ALPHA * x_ref[...] + (1.0 + gate) * y)
        xo_ref[...] = xhat * g_ref[...] + b_ref[...]

    act_in = list(gate_act) if rnn else [ypre]
    out_shape = [jax.ShapeDtypeStruct((s, D_MODEL), F32), jax.ShapeDtypeStruct((s, D_MODEL), F32)]
    out_specs = [_row(tm, D_MODEL), _row(tm, D_MODEL)]
    if rnn:
        out_shape.append(jax.ShapeDtypeStruct((s, kdim), BF16))
        out_specs.append(_row(tm, kdim))
    return _call(
        body, name="rnn_post_fwd" if rnn else "attn_post_fwd", grid=(s // tm,),
        out_shape=out_shape,
        in_specs=[_row(tm, kdim)] * len(act_in) + [_res(w.shape), _row(tm, D_MODEL), _res(mods.shape),
                                                    _res(lng.shape), _res(lnb.shape)],
        out_specs=out_specs,
        args=(*act_in, w, x, mods, lng, lnb), rider=rider)


def mlp_fwd(x, mods, k, w1_t, w2, lng, lnb, rider=None, last=False):
    s = x.shape[0]
    tm = _blk(s, TM_MLP_FWD)

    def body(x_ref, mod_ref, w1_ref, w2_ref, g_ref, b_ref, *outs):
        xo_ref = None if last else outs[0]
        y_ref, ra_ref, r_ref = outs[-3:]
        xv = x_ref[...]
        shift, scale, gate = _mod(mod_ref, k)
        h = (xv * (1.0 + scale) + shift).astype(BF16)
        y = jnp.zeros((tm, D_MODEL), F32)
        for c in range(D_FF // FF_CHUNK):
            rows = slice(FF_CHUNK * c, FF_CHUNK * (c + 1))
            a = jnp.maximum(_nt(h, w1_ref[rows, :]), 0.0)
            r = (a * a).astype(BF16)
            ra_ref[:, rows] = a.astype(BF16)
            r_ref[:, rows] = r
            y = y + _nn(r, w2_ref[rows, :])
        y_ref[...] = y
        if not last:
            xhat, _ = _ln_stats(ALPHA * xv + (1.0 + gate) * y)
            xo_ref[...] = xhat * g_ref[...] + b_ref[...]

    nf = 1 if last else 2
    return _call(
        body, name="mlp_fwd_last" if last else "mlp_fwd", grid=(s // tm,),
        out_shape=[jax.ShapeDtypeStruct((s, D_MODEL), F32)] * nf + [jax.ShapeDtypeStruct((s, D_FF), BF16)] * 2,
        in_specs=[_row(tm, D_MODEL), _res(mods.shape), _res(w1_t.shape), _res(w2.shape),
                  _res(lng.shape), _res(lnb.shape)],
        out_specs=[_row(tm, D_MODEL)] * nf + [_row(tm, D_FF)] * 2,
        args=(x, mods, w1_t, w2, lng, lnb), rider=rider)


def rnn_in_fwd(x, mods, k, win_t):
    s = x.shape[0]
    tm = _blk(s, TM_MM)

    def body(x_ref, mod_ref, w_ref, xr_ref, gt_ref):
        shift, scale, _ = _mod(mod_ref, k)
        h = (x_ref[...] * (1.0 + scale) + shift).astype(BF16)
        xr_ref[...] = _nt(h, w_ref[0:D_RNN, :])
        gt_ref[...] = _nt(h, w_ref[D_RNN:2 * D_RNN, :]).astype(BF16)

    return pl.pallas_call(
        body, name="rnn_in_fwd", grid=(s // tm,),
        out_shape=[jax.ShapeDtypeStruct((s, D_RNN), F32), jax.ShapeDtypeStruct((s, D_RNN), BF16)],
        in_specs=[_row(tm, D_MODEL), _res(mods.shape), _res(win_t.shape)],
        out_specs=[_row(tm, D_RNN)] * 2,
        compiler_params=_params(),
    )(x, mods, win_t)


def _shift_rows(v, k, row):
    n = v.shape[0]
    r = pltpu.roll(v, k % n, 0)
    keep = (row >= k) if k > 0 else (row < n + k)
    return jnp.where(keep, r, 0.0)


def conv_fwd(xr, cw, cb):
    s = xr.shape[0]

    def body(x_ref, w_ref, b_ref, o_ref):
        xv = x_ref[...]
        row = lax.broadcasted_iota(jnp.int32, xv.shape, 0)
        o_ref[...] = (b_ref[...] + w_ref[0:1, :] * _shift_rows(xv, 2, row) + w_ref[1:2, :] * _shift_rows(xv, 1, row)
                      + w_ref[2:3, :] * xv + w_ref[3:4, :] * _shift_rows(xv, -1, row))

    slab = pl.BlockSpec((s, 128), lambda j: (0, j))
    return pl.pallas_call(
        body, name="conv_fwd", grid=(D_RNN // 128,),
        out_shape=jax.ShapeDtypeStruct((s, D_RNN), F32),
        in_specs=[slab, pl.BlockSpec((4, 128), lambda j: (0, j)), pl.BlockSpec((1, 128), lambda j: (0, j))],
        out_specs=slab,
        compiler_params=_params(),
    )(xr, cw, cb)


def _softplus_neg(lam):
    z = -lam
    e = jnp.exp(-jnp.abs(z))
    u = 1.0 + e
    log1p = jnp.where(u == 1.0, e, jnp.log(u) * e / jnp.where(u == 1.0, 1.0, u - 1.0))
    return jnp.maximum(z, 0.0) + log1p, 1.0 / (1.0 + jnp.exp(lam))


def _lru_gates(xv, wa_ref, wx_ref, ba_ref, bx_ref, lam_ref):
    xb = xv.astype(BF16)
    r = _sigmoid(_nn(xb, wa_ref[...]) + ba_ref[...])
    i = _sigmoid(_nn(xb, wx_ref[...]) + bx_ref[...])
    sp, sg = _softplus_neg(lam_ref[...])
    la = r * (-LRU_C * sp)
    a = jnp.exp(la)
    th = jnp.tanh(la)
    m2 = -2.0 * th / (1.0 - th)
    rmult = lax.rsqrt(jnp.maximum(m2, 1e-37))
    return xb, r, i, sp, sg, a, m2 * rmult, rmult


def _scan(a, u, h0, reverse):
    n, c = a.shape
    sub = lax.broadcasted_iota(jnp.int32, (8, c), 0)
    steps = [(8 - sh, sub < 8 - sh) if reverse else (sh, sub >= sh) for sh in (1, 2, 4)]
    out = [None] * (n // 8)
    edge = h0
    for k in (range(n // 8 - 1, -1, -1) if reverse else range(n // 8)):
        at, ut = a[8 * k:8 * k + 8], u[8 * k:8 * k + 8]
        for rot, keep in steps:
            a_s = jnp.where(keep, pltpu.roll(at, rot, 0), 1.0)
            u_s = jnp.where(keep, pltpu.roll(ut, rot, 0), 0.0)
            ut = at * u_s + ut
            at = at * a_s
        hk = ut + at * edge
        out[k] = hk
        edge = hk[0:1] if reverse else hk[7:8]
    return jnp.concatenate(out, axis=0)


def _lru_specs(nt, tt, reverse):
    tmap = (lambda t: nt - 1 - t) if reverse else (lambda t: t)
    blk = pl.BlockSpec((tt, CG), lambda g, t: (tmap(t), g))
    wsp = pl.BlockSpec((None, CG, CG), lambda g, t: (g, 0, 0))
    vec = pl.BlockSpec((1, CG), lambda g, t: (0, g))
    return tmap, blk, wsp, vec


def lru_fwd(xc, wa, wx, ba, bx, lam, reverse, other=None):
    s = xc.shape[0]
    tt = _blk(s, TT_RNN)
    sb = _blk(tt, SB_RNN)
    nt = s // tt

    def body(x_ref, wa_ref, wx_ref, ba_ref, bx_ref, lam_ref, *rest):
        if other is None:
            hs_ref, carry = rest
        else:
            oth_ref, hs_ref, sum_ref, carry = rest

        @pl.when(pl.program_id(1) == 0)
        def _():
            carry[...] = jnp.zeros(carry.shape, F32)

        xv = x_ref[...]
        _, _, i, _, _, a, mult, _ = _lru_gates(xv, wa_ref, wx_ref, ba_ref, bx_ref, lam_ref)
        u = mult * (i * xv)
        h0 = carry[0:1, :]
        order = range(tt // sb - 1, -1, -1) if reverse else range(tt // sb)
        for j in order:
            rows = slice(sb * j, sb * (j + 1))
            h = _scan(a[rows], u[rows], h0, reverse)
            hs_ref[rows, :] = h
            if other is not None:
                sum_ref[rows, :] = (h + oth_ref[rows, :]).astype(BF16)
            h0 = h[0:1, :] if reverse else h[sb - 1:sb, :]
        carry[0:1, :] = h0

    _, blk, wsp, vec = _lru_specs(nt, tt, reverse)
    extra = [] if other is None else [other]
    return pl.pallas_call(
        body, name="lru_fwd_rev" if reverse else "lru_fwd", grid=(N_CG, nt),
        out_shape=[jax.ShapeDtypeStruct((s, D_RNN), F32)] + [jax.ShapeDtypeStruct((s, D_RNN), BF16)] * len(extra),
        in_specs=[blk, wsp, wsp, vec, vec, vec] + [blk] * len(extra), out_specs=[blk] * (1 + len(extra)),
        scratch_shapes=[pltpu.VMEM((8, CG), F32)],
        compiler_params=_params(),
    )(xc, wa, wx, ba, bx, lam, *extra)


def lru_bwd(xc, dhs, hs, wa, wx, ba, bx, lam, reverse, rider=None):
    s = xc.shape[0]
    tt = _blk(s, TT_RNN)
    sb = _blk(tt, SB_RNN)
    nt = s // tt
    back = not reverse

    def body(x_ref, dh_ref, hs_ref, nb_ref, wa_ref, wx_ref, ba_ref, bx_ref, lam_ref,
             dx_ref, dwa_ref, dwx_ref, dba_ref, dbx_ref, dlam_ref, carry):
        t = pl.program_id(1)

        @pl.when(t == 0)
        def _():
            carry[...] = jnp.zeros(carry.shape, F32)
            dwa_ref[...] = jnp.zeros(dwa_ref.shape, F32)
            dwx_ref[...] = jnp.zeros(dwx_ref.shape, F32)
            dba_ref[...] = jnp.zeros(dba_ref.shape, F32)
            dbx_ref[...] = jnp.zeros(dbx_ref.shape, F32)
            dlam_ref[...] = jnp.zeros(dlam_ref.shape, F32)

        xv = x_ref[...]
        xb, r, i, sp, sg, a, mult, rmult = _lru_gates(xv, wa_ref, wx_ref, ba_ref, bx_ref, lam_ref)
        row = lax.broadcasted_iota(jnp.int32, xv.shape, 0)
        hsv = hs_ref[...]
        inner = t < nt - 1
        if reverse:
            edge = jnp.where(inner, nb_ref[0:1, :], 0.0)
            hprev = jnp.where(row == tt - 1, edge, pltpu.roll(hsv, tt - 1, 0))
            a_next = jnp.where(row == 0, carry[1:2, :], pltpu.roll(a, 1, 0))
        else:
            edge = jnp.where(inner, nb_ref[7:8, :], 0.0)
            hprev = jnp.where(row == 0, edge, pltpu.roll(hsv, 1, 0))
            a_next = jnp.where(row == tt - 1, carry[1:2, :], pltpu.roll(a, tt - 1, 0))
        dhv = dh_ref[...]
        g0 = carry[0:1, :]
        parts = [None] * (tt // sb)
        order = range(tt // sb - 1, -1, -1) if back else range(tt // sb)
        for j in order:
            rows = slice(sb * j, sb * (j + 1))
            gj = _scan(a_next[rows], dhv[rows], g0, back)
            parts[j] = gj
            g0 = gj[0:1, :] if back else gj[sb - 1:sb, :]
        g = jnp.concatenate(parts, axis=0) if len(parts) > 1 else parts[0]
        carry[0:1, :] = g0
        carry[1:2, :] = a[0:1, :] if back else a[tt - 1:tt, :]

        da = g * hprev
        dmult = g * (i * xv)
        di = g * mult * xv
        dla = da * a - dmult * (a * a) * rmult
        dpa = (dla * (-LRU_C * sp)) * r * (1.0 - r)
        dpx = di * i * (1.0 - i)
        dlam_ref[...] += _colsum(dla * (LRU_C * r * sg))
        dba_ref[...] += _colsum(dpa)
        dbx_ref[...] += _colsum(dpx)
        dpab, dpxb = dpa.astype(BF16), dpx.astype(BF16)
        dx_ref[...] = g * mult * i + _nt(dpab, wa_ref[...]) + _nt(dpxb, wx_ref[...])
        dwa_ref[...] += _tn(xb, dpab)
        dwx_ref[...] += _tn(xb, dpxb)

    tmap, blk, wsp, vec = _lru_specs(nt, tt, back)
    per8 = tt // 8
    if reverse:
        nb = pl.BlockSpec((8, CG), lambda g, t: (jnp.minimum((tmap(t) + 1) * per8, s // 8 - 1), g))
    else:
        nb = pl.BlockSpec((8, CG), lambda g, t: (jnp.maximum(tmap(t) * per8 - 1, 0), g))
    return _call(
        body, name="lru_bwd_rev" if reverse else "lru_bwd", grid=(N_CG, nt),
        out_shape=[jax.ShapeDtypeStruct((s, D_RNN), F32), jax.ShapeDtypeStruct((N_CG, CG, CG), F32),
                   jax.ShapeDtypeStruct((N_CG, CG, CG), F32)] + [jax.ShapeDtypeStruct((1, D_RNN), F32)] * 3,
        in_specs=[blk, blk, blk, nb, wsp, wsp, vec, vec, vec],
        out_specs=[blk, wsp, wsp, vec, vec, vec],
        scratch_shapes=[pltpu.VMEM((8, CG), F32)],
        args=(xc, dhs, hs, hs, wa, wx, ba, bx, lam), rider=rider)


def _ln_part_bwd(dxo, x, y, gate, g, sums_ref, loss_head=None):
    xhat, rstd = _ln_stats(ALPHA * x + (1.0 + gate) * y)
    if loss_head is not None:
        err = xhat * g + loss_head[0] - loss_head[1]
        dxo = err * (1.0 / D_MODEL)
        sums_ref[5:6, :] += _colsum(err * err)
    dz = _ln_bwd(dxo, xhat, rstd, g)
    sums_ref[2:3, :] += _colsum(dz * y)
    sums_ref[3:4, :] += _colsum(dxo * xhat)
    sums_ref[4:5, :] += _colsum(dxo)
    return dz


def mlp_bwd(dxo, x, y, ra, mods, k, w1_t, w2, lng, lnb=None, rider=None):
    s = x.shape[0]
    tm = _blk(s, TM_MLP)
    head = lnb is not None

    def body(d_ref, x_ref, y_ref, ra_ref, mod_ref, w1_ref, w2_ref, g_ref, *rest):
        b_ref = rest[0] if head else None
        dx_ref, da_ref, h_ref, dy_ref, sums_ref = rest[1:] if head else rest

        @pl.when(pl.program_id(0) == 0)
        def _():
            sums_ref[...] = jnp.zeros(sums_ref.shape, F32)

        xv = x_ref[...]
        shift, scale, gate = _mod(mod_ref, k)
        if head:
            dz = _ln_part_bwd(None, xv, y_ref[...], gate, g_ref[...], sums_ref, (b_ref[...], d_ref[...]))
        else:
            dz = _ln_part_bwd(d_ref[...], xv, y_ref[...], gate, g_ref[...], sums_ref)
        dyb = (dz * (1.0 + gate)).astype(BF16)
        dy_ref[...] = dyb
        h = (xv * (1.0 + scale) + shift).astype(BF16)
        h_ref[...] = h
        dh = jnp.zeros((tm, D_MODEL), F32)
        for c in range(D_FF // FF_CHUNK):
            rows = slice(FF_CHUNK * c, FF_CHUNK * (c + 1))
            da = (_nt(dyb, w2_ref[rows, :]) * (2.0 * ra_ref[:, rows].astype(F32))).astype(BF16)
            da_ref[:, rows] = da
            dh = dh + _nn(da, w1_ref[rows, :])
        dx_ref[...] = ALPHA * dz + dh * (1.0 + scale)
        sums_ref[0:1, :] += _colsum(dh)
        sums_ref[1:2, :] += _colsum(dh * xv)

    return _call(
        body, name="mlp_bwd", grid=(s // tm,),
        out_shape=[jax.ShapeDtypeStruct((s, D_MODEL), F32), jax.ShapeDtypeStruct((s, D_FF), BF16),
                   jax.ShapeDtypeStruct((s, D_MODEL), BF16),
                   jax.ShapeDtypeStruct((s, D_MODEL), BF16), jax.ShapeDtypeStruct((8, D_MODEL), F32)],
        in_specs=[_row(tm, D_MODEL)] * 3 + [_row(tm, D_FF), _res(mods.shape), _res(w1_t.shape), _res(w2.shape),
                                             _res(lng.shape)] + ([_res(lnb.shape)] if head else []),
        out_specs=[_row(tm, D_MODEL), _row(tm, D_FF), _row(tm, D_MODEL), _row(tm, D_MODEL), _res((8, D_MODEL))],
        args=(dxo, x, y, ra, mods, w1_t, w2, lng) + ((lnb,) if head else ()), rider=rider)


def post_bwd(dxo, x, y, mods, k, w, lng, gate_act=None, rider=None):
    s = x.shape[0]
    tm = _blk(s, TM_MM)
    kdim = w.shape[0]
    rnn = gate_act is not None

    def body(*refs):
        if rnn:
            (d_ref, x_ref, y_ref, mod_ref, w_ref, g_ref, gt_ref, hs_ref,
             dres_ref, dy_ref, sums_ref, dhs_ref, dgt_ref) = refs
        else:
            d_ref, x_ref, y_ref, mod_ref, w_ref, g_ref, dres_ref, dy_ref, sums_ref, dyp_ref = refs

        @pl.when(pl.program_id(0) == 0)
        def _():
            sums_ref[...] = jnp.zeros(sums_ref.shape, F32)

        _, _, gate = _mod(mod_ref, k)
        dz = _ln_part_bwd(d_ref[...], x_ref[...], y_ref[...], gate, g_ref[...], sums_ref)
        dres_ref[...] = ALPHA * dz
        dyb = (dz * (1.0 + gate)).astype(BF16)
        dy_ref[...] = dyb
        dyp = _nt(dyb, w_ref[...])
        if rnn:
            act, dact = _gelu_parts(gt_ref[...].astype(F32))
            dhs_ref[...] = dyp * act
            dgt_ref[...] = (dyp * hs_ref[...].astype(F32) * dact).astype(BF16)
        else:
            dyp_ref[...] = dyp.astype(BF16)

    ins = [dxo, x, y, mods, w, lng] + (list(gate_act) if rnn else [])
    in_specs = [_row(tm, D_MODEL)] * 3 + [_res(mods.shape), _res(w.shape), _res(lng.shape)]
    out_shape = [jax.ShapeDtypeStruct((s, D_MODEL), F32), jax.ShapeDtypeStruct((s, D_MODEL), BF16),
                 jax.ShapeDtypeStruct((8, D_MODEL), F32)]
    out_specs = [_row(tm, D_MODEL), _row(tm, D_MODEL), _res((8, D_MODEL))]
    if rnn:
        in_specs += [_row(tm, kdim)] * 2
        out_shape += [jax.ShapeDtypeStruct((s, kdim), F32), jax.ShapeDtypeStruct((s, kdim), BF16)]
        out_specs += [_row(tm, kdim)] * 2
    else:
        out_shape.append(jax.ShapeDtypeStruct((s, kdim), BF16))
        out_specs.append(_row(tm, kdim))
    return _call(
        body, name="rnn_post_bwd" if rnn else "attn_post_bwd", grid=(s // tm,),
        out_shape=out_shape, in_specs=in_specs, out_specs=out_specs, args=ins, rider=rider)


def attn_bwd(q, kk, v, do, sinks, rider=None):
    s = q.shape[0]
    nblk = s // QBLK
    scale = HEAD ** -0.5

    def body(sink_ref, q_ref, do_ref, k0, k1, k2, k3, v0, v1, v2, v3, dq_ref, dk_ref, dv_ref, ds_ref):
        n = pl.program_id(0)

        @pl.when(n == 0)
        def _():
            ds_ref[...] = jnp.zeros(ds_ref.shape, F32)
            dk_ref[...] = jnp.zeros(dk_ref.shape, F32)
            dv_ref[...] = jnp.zeros(dv_ref.shape, F32)

        kall = jnp.concatenate([k0[...], k1[...], k2[...], k3[...]], axis=0)
        vall = jnp.concatenate([v0[...], v1[...], v2[...], v3[...]], axis=0)
        lane = lax.broadcasted_iota(jnp.int32, (1, 128), 1)
        dsink = jnp.zeros((1, 128), F32)
        for qb in range(QPAIR):
            nb = QPAIR * n + qb
            valid = _attn_mask(nb, s)
            keys = slice(QBLK * qb, QBLK * (qb + 3))
            for kv in range(N_KV):
                cols = slice(HEAD * kv, HEAD * (kv + 1))
                qg, dog = _stack_heads(q_ref, qb, kv), _stack_heads(do_ref, qb, kv)
                kh, vh = kall[keys, cols], vall[keys, cols]
                probs, psink = _attn_probs(qg, kh, valid, _stack_sinks(sink_ref, kv))
                dprobs = _nt(dog, vh)
                dvp = _tn(probs.astype(BF16), dog)
                rowdot = jnp.sum(probs * dprobs, axis=-1, keepdims=True)
                dsb = (probs * (dprobs - rowdot) * scale).astype(BF16)
                dqg = _nn(dsb, kh)
                dkp = _tn(dsb, qg)
                for p in range(3):
                    blk = jnp.clip(nb - 1 + p, 0, nblk - 1)
                    rows = pl.ds(pl.multiple_of(blk * QBLK, QBLK), QBLK)
                    dk_ref[rows, cols] += dkp[QBLK * p:QBLK * (p + 1), :]
                    dv_ref[rows, cols] += dvp[QBLK * p:QBLK * (p + 1), :]
                dsk = -psink * rowdot
                for j in range(GROUP):
                    hq = GROUP * kv + j
                    dq_ref[QBLK * qb:QBLK * (qb + 1), HEAD * hq:HEAD * (hq + 1)] = dqg[QBLK * j:QBLK * (j + 1), :]
                    dsink = dsink + jnp.where(lane == hq, _colsum(dsk[QBLK * j:QBLK * (j + 1), :]), 0.0)
        ds_ref[...] += dsink

    qspec = pl.BlockSpec((QPAIR * QBLK, N_Q * HEAD), lambda n: (n, 0))
    return _call(
        body, name="attn_bwd", grid=(nblk // QPAIR,),
        out_shape=[jax.ShapeDtypeStruct((s, N_Q * HEAD), F32),
                   jax.ShapeDtypeStruct((s, N_KV * HEAD), F32), jax.ShapeDtypeStruct((s, N_KV * HEAD), F32),
                   jax.ShapeDtypeStruct((1, 128), F32)],
        in_specs=[pl.BlockSpec(memory_space=pltpu.SMEM), qspec, qspec] + _kv_specs(nblk) + _kv_specs(nblk),
        out_specs=[qspec, _res((s, N_KV * HEAD)), _res((s, N_KV * HEAD)), pl.BlockSpec((1, 128), lambda n: (0, 0))],
        args=(sinks, q, do, kk, kk, kk, kk, v, v, v, v), rider=rider)


def _in_bwd_tail(dzb, w_ref, x_ref, mod_ref, k, dres_ref, dx_ref, h_ref, sums_ref):
    xv = x_ref[...]
    shift, scale, _ = _mod(mod_ref, k)
    h_ref[...] = (xv * (1.0 + scale) + shift).astype(BF16)
    dh = _nn(dzb, w_ref[...])
    dx_ref[...] = dres_ref[...] + dh * (1.0 + scale)
    sums_ref[0:1, :] += _colsum(dh)
    sums_ref[1:2, :] += _colsum(dh * xv)


def attn_in_bwd(dq, dk, dv, rope, x, mods, k, win_t, dres):
    s = x.shape[0]
    tm = _blk(s, TM_MM)

    def body(dq_ref, dk_ref, dv_ref, c_ref, s1_ref, s2_ref, x_ref, mod_ref, w_ref, dres_ref,
             dx_ref, dz_ref, h_ref, sums_ref):
        @pl.when(pl.program_id(0) == 0)
        def _():
            sums_ref[...] = jnp.zeros(sums_ref.shape, F32)

        cos, s1, s2 = c_ref[...], s1_ref[...], s2_ref[...]
        for hh in range(N_Q + N_KV):
            src = dq_ref[:, HEAD * hh:HEAD * (hh + 1)] if hh < N_Q else dk_ref[:, HEAD * (hh - N_Q):HEAD * (hh - N_Q + 1)]
            dz_ref[:, HEAD * hh:HEAD * (hh + 1)] = _rope_bwd(src, cos, s1, s2).astype(BF16)
        dz_ref[:, HEAD * (N_Q + N_KV):] = dv_ref[...].astype(BF16)
        _in_bwd_tail(dz_ref[...], w_ref, x_ref, mod_ref, k, dres_ref, dx_ref, h_ref, sums_ref)

    return pl.pallas_call(
        body, name="attn_in_bwd", grid=(s // tm,),
        out_shape=[jax.ShapeDtypeStruct((s, D_MODEL), F32), jax.ShapeDtypeStruct((s, D_QKV), BF16),
                   jax.ShapeDtypeStruct((s, D_MODEL), BF16), jax.ShapeDtypeStruct((8, D_MODEL), F32)],
        in_specs=[_row(tm, N_Q * HEAD), _row(tm, N_KV * HEAD), _row(tm, N_KV * HEAD),
                  _row(tm, HEAD), _row(tm, HEAD), _row(tm, HEAD), _row(tm, D_MODEL),
                  _res(mods.shape), _res(win_t.shape), _row(tm, D_MODEL)],
        out_specs=[_row(tm, D_MODEL), _row(tm, D_QKV), _row(tm, D_MODEL), _res((8, D_MODEL))],
        compiler_params=_params(),
    )(dq, dk, dv, *rope, x, mods, win_t, dres)


def _shift_blk(v, k, before, after, row):
    n = v.shape[0]
    r = pltpu.roll(v, k % n, 0)
    for j in range(abs(k)):
        if k > 0:
            r = jnp.where(row == j, before[8 - k + j:8 - k + j + 1, :], r)
        else:
            r = jnp.where(row == n + k + j, after[j:j + 1, :], r)
    return r


def rnn_in_bwd(dxc_f, dxc_b, xr, cw, dgt, x, mods, k, win_t, dres):
    s = x.shape[0]
    tm = _blk(s, TM_MM)
    n = s // tm

    def body(f_ref, fp_ref, fn_ref, b_ref, bp_ref, bn_ref, xr_ref, xp_ref, xn_ref, cw_ref, dgt_ref,
             x_ref, mod_ref, w_ref, dres_ref, dx_ref, dz_ref, h_ref, sums_ref, dcw_ref, dcb_ref):
        i = pl.program_id(0)

        @pl.when(i == 0)
        def _():
            sums_ref[...] = jnp.zeros(sums_ref.shape, F32)
            dcw_ref[...] = jnp.zeros(dcw_ref.shape, F32)
            dcb_ref[...] = jnp.zeros(dcb_ref.shape, F32)

        d = f_ref[...] + b_ref[...]
        xv = xr_ref[...]
        first, last = i == 0, i == n - 1
        d_before = jnp.where(first, 0.0, fp_ref[...] + bp_ref[...])
        d_after = jnp.where(last, 0.0, fn_ref[...] + bn_ref[...])
        x_before = jnp.where(first, 0.0, xp_ref[...])
        x_after = jnp.where(last, 0.0, xn_ref[...])
        row = lax.broadcasted_iota(jnp.int32, d.shape, 0)
        dxr = (cw_ref[0:1, :] * _shift_blk(d, -2, d_before, d_after, row)
               + cw_ref[1:2, :] * _shift_blk(d, -1, d_before, d_after, row)
               + cw_ref[2:3, :] * d + cw_ref[3:4, :] * _shift_blk(d, 1, d_before, d_after, row))
        dcw_ref[0:1, :] += _colsum(d * _shift_blk(xv, 2, x_before, x_after, row))
        dcw_ref[1:2, :] += _colsum(d * _shift_blk(xv, 1, x_before, x_after, row))
        dcw_ref[2:3, :] += _colsum(d * xv)
        dcw_ref[3:4, :] += _colsum(d * _shift_blk(xv, -1, x_before, x_after, row))
        dcb_ref[...] += _colsum(d)
        dz_ref[:, 0:D_RNN] = dxr.astype(BF16)
        dz_ref[:, D_RNN:2 * D_RNN] = dgt_ref[...]
        _in_bwd_tail(dz_ref[...], w_ref, x_ref, mod_ref, k, dres_ref, dx_ref, h_ref, sums_ref)

    per8 = tm // 8
    blk = _row(tm, D_RNN)
    before = pl.BlockSpec((8, D_RNN), lambda i: (jnp.maximum(i * per8 - 1, 0), 0))
    after = pl.BlockSpec((8, D_RNN), lambda i: (jnp.minimum((i + 1) * per8, s // 8 - 1), 0))
    return pl.pallas_call(
        body, name="rnn_in_bwd", grid=(n,),
        out_shape=[jax.ShapeDtypeStruct((s, D_MODEL), F32), jax.ShapeDtypeStruct((s, 2 * D_RNN), BF16),
                   jax.ShapeDtypeStruct((s, D_MODEL), BF16), jax.ShapeDtypeStruct((8, D_MODEL), F32),
                   jax.ShapeDtypeStruct((4, D_RNN), F32), jax.ShapeDtypeStruct((1, D_RNN), F32)],
        in_specs=[blk, before, after] * 3 + [_res(cw.shape), blk, _row(tm, D_MODEL), _res(mods.shape),
                                             _res(win_t.shape), _row(tm, D_MODEL)],
        out_specs=[_row(tm, D_MODEL), _row(tm, 2 * D_RNN), _row(tm, D_MODEL), _res((8, D_MODEL)),
                   _res((4, D_RNN)), _res((1, D_RNN))],
        compiler_params=_params(),
    )(dxc_f, dxc_f, dxc_f, dxc_b, dxc_b, dxc_b, xr, xr, xr, cw, dgt, x, mods, win_t, dres)


def wgrad(a, b, name, rider=None):
    s, m = a.shape
    n = b.shape[1]
    tm = next(t for t in (1024, 768, 512, 384, 256, 128) if m % t == 0)
    tk = _blk(s, TK_WG)
    nk = s // tk

    def body(a_ref, b_ref, o_ref, acc):
        kk = pl.program_id(1)

        @pl.when(kk == 0)
        def _():
            acc[...] = jnp.zeros(acc.shape, F32)

        acc[...] += _tn(a_ref[...], b_ref[...])

        @pl.when(kk == nk - 1)
        def _():
            o_ref[...] = acc[...].astype(BF16)

    out, *rode = _call(
        body, name=name, grid=(m // tm, nk),
        out_shape=[jax.ShapeDtypeStruct((m, n), BF16)],
        in_specs=[pl.BlockSpec((tk, tm), lambda i, kk: (kk, i)), pl.BlockSpec((tk, n), lambda i, kk: (kk, 0))],
        out_specs=[pl.BlockSpec((tm, n), lambda i, kk: (i, 0))],
        scratch_shapes=[pltpu.VMEM((tm, n), F32)],
        args=(a, b), rider=rider)
    out = out.reshape(N_DEV, m // N_DEV, n)
    return (out, *rode) if rider is not None else out


def part_sum(parts, name):
    _, r, c = parts.shape
    tr = next(t for t in (256, 192, 128, 64, 32, 16, 8) if r % t == 0)

    def body(p_ref, o_ref):
        acc = p_ref[0].astype(F32)
        for j in range(1, N_DEV):
            acc = acc + p_ref[j].astype(F32)
        o_ref[...] = acc

    return pl.pallas_call(
        body, name=name, grid=(r // tr,),
        out_shape=jax.ShapeDtypeStruct((r, c), F32),
        in_specs=[pl.BlockSpec((N_DEV, tr, c), lambda i: (0, i, 0))],
        out_specs=pl.BlockSpec((tr, c), lambda i: (i, 0)),
        compiler_params=_params(),
    )(parts)


def adamw(w, g, m, v, name):
    shape = w.shape
    c = shape[-1]
    r = w.size // c
    w2, g2, m2, v2 = (t.reshape(r, c) for t in (w, g, m, v))
    tr = r if r * c <= 512 * 1024 else next(t for t in (512, 256, 128, 64, 32, 16, 8) if r % t == 0)

    def body(w_ref, g_ref, m_ref, v_ref, d_ref, nm_ref, nv_ref):
        gv = g_ref[...]
        nm = B1 * m_ref[...] + (1.0 - B1) * gv
        nv = B2 * v_ref[...] + (1.0 - B2) * (gv * gv)
        nm_ref[...] = nm
        nv_ref[...] = nv
        m_hat = nm / (1.0 - B1 ** STEP)
        v_hat = nv / (1.0 - B2 ** STEP)
        d_ref[...] = -LR * (m_hat / (jnp.sqrt(v_hat) + ADAM_EPS) + WD * w_ref[...])

    spec = pl.BlockSpec((tr, c), lambda i: (i, 0))
    outs = pl.pallas_call(
        body, name=name, grid=(r // tr,),
        out_shape=[jax.ShapeDtypeStruct((r, c), F32)] * 3,
        in_specs=[spec] * 4, out_specs=[spec] * 3,
        compiler_params=_params(),
    )(w2, g2, m2, v2)
    return tuple(o.reshape(shape) for o in outs)


def _rope_tables(s):
    half = ROT // 2
    inv_freq = THETA ** (-jnp.arange(0, ROT, 2, dtype=F32) / ROT)
    per_row = 128 // half
    pos = (per_row * jnp.arange(s // per_row)[:, None] + jnp.arange(128)[None, :] // half).astype(F32)
    ang = pos * jnp.tile(inv_freq, per_row)[None, :]
    cos, sin = lax.optimization_barrier((jnp.cos(ang), jnp.sin(ang)))
    cos, sin = cos.reshape(s, half), sin.reshape(s, half)
    zeros = jnp.zeros((s, HEAD - ROT), F32)
    c = jnp.concatenate([cos, cos, jnp.ones((s, HEAD - ROT), F32)], axis=1)
    s1 = jnp.concatenate([jnp.zeros((s, half), F32), sin, zeros], axis=1)
    s2 = jnp.concatenate([-sin, jnp.zeros((s, half), F32), zeros], axis=1)
    return c, s1, s2


def _blockdiag(w):
    w4 = w.reshape(N_CG, 4, RB_W, RB_W)
    eye = jnp.eye(4, dtype=w.dtype)
    return jnp.einsum("gipq,ij->gipjq", w4, eye).reshape(N_CG, CG, CG)


def _diag_blocks(w):
    w5 = w.reshape(N_CG, 4, RB_W, 4, RB_W)
    eye = jnp.eye(4, dtype=w.dtype)
    return jnp.einsum("gipjq,ij->gipq", w5, eye).reshape(N_RB, RB_W, RB_W)


def _cols(full, per):
    lead = full.shape[:-1]
    t = full.reshape(lead + (N_DEV, per))
    return jnp.moveaxis(t, -2, 0).reshape(N_DEV, -1)


def kernel(x, c, ada_w, ada_b, ln_g, ln_b, attn_w_in, attn_w_out, attn_sinks, rnn_w_in, rnn_conv_w, rnn_conv_b, rnn_w_a, rnn_b_a, rnn_w_x, rnn_b_x, rnn_lam, rnn_w_out, mlp_w1, mlp_w2, loss_target, m_ada_w, m_ada_b, m_ln_g, m_ln_b, m_attn_w_in, m_attn_w_out, m_attn_sinks, m_rnn_w_in, m_rnn_conv_w, m_rnn_conv_b, m_rnn_w_a, m_rnn_b_a, m_rnn_w_x, m_rnn_b_x, m_rnn_lam, m_rnn_w_out, m_mlp_w1, m_mlp_w2, v_ada_w, v_ada_b, v_ln_g, v_ln_b, v_attn_w_in, v_attn_w_out, v_attn_sinks, v_rnn_w_in, v_rnn_conv_w, v_rnn_conv_b, v_rnn_w_a, v_rnn_b_a, v_rnn_w_x, v_rnn_b_x, v_rnn_lam, v_rnn_w_out, v_mlp_w1, v_mlp_w2):
    s = x.shape[1]
    x0 = x.reshape(s, D_MODEL)
    target = loss_target.reshape(s, D_MODEL)
    weights = dict(ada_w=ada_w, ada_b=ada_b, ln_g=ln_g, ln_b=ln_b, attn_w_in=attn_w_in, attn_w_out=attn_w_out,
                   attn_sinks=attn_sinks, rnn_w_in=rnn_w_in, rnn_conv_w=rnn_conv_w, rnn_conv_b=rnn_conv_b,
                   rnn_w_a=rnn_w_a, rnn_b_a=rnn_b_a, rnn_w_x=rnn_w_x, rnn_b_x=rnn_b_x, rnn_lam=rnn_lam,
                   rnn_w_out=rnn_w_out, mlp_w1=mlp_w1, mlp_w2=mlp_w2)
    moments_m = dict(ada_w=m_ada_w, ada_b=m_ada_b, ln_g=m_ln_g, ln_b=m_ln_b, attn_w_in=m_attn_w_in,
                     attn_w_out=m_attn_w_out, attn_sinks=m_attn_sinks, rnn_w_in=m_rnn_w_in,
                     rnn_conv_w=m_rnn_conv_w, rnn_conv_b=m_rnn_conv_b, rnn_w_a=m_rnn_w_a, rnn_b_a=m_rnn_b_a,
                     rnn_w_x=m_rnn_w_x, rnn_b_x=m_rnn_b_x, rnn_lam=m_rnn_lam, rnn_w_out=m_rnn_w_out,
                     mlp_w1=m_mlp_w1, mlp_w2=m_mlp_w2)
    moments_v = dict(ada_w=v_ada_w, ada_b=v_ada_b, ln_g=v_ln_g, ln_b=v_ln_b, attn_w_in=v_attn_w_in,
                     attn_w_out=v_attn_w_out, attn_sinks=v_attn_sinks, rnn_w_in=v_rnn_w_in,
                     rnn_conv_w=v_rnn_conv_w, rnn_conv_b=v_rnn_conv_b, rnn_w_a=v_rnn_w_a, rnn_b_a=v_rnn_b_a,
                     rnn_w_x=v_rnn_w_x, rnn_b_x=v_rnn_b_x, rnn_lam=v_rnn_lam, rnn_w_out=v_rnn_w_out,
                     mlp_w1=v_mlp_w1, mlp_w2=v_mlp_w2)
    names = list(weights)

    def t16(w):
        return w.T.astype(BF16)

    big = [t16(attn_w_in[0]), attn_w_out[0].astype(BF16), t16(rnn_w_in[0]), rnn_w_out[0].astype(BF16),
           t16(mlp_w1[0]), mlp_w2[0].astype(BF16), t16(mlp_w1[1]), mlp_w2[1].astype(BF16)]
    small_local = jnp.concatenate([
        ln_g.reshape(-1), ln_b.reshape(-1), rnn_conv_w.reshape(-1), rnn_conv_b.reshape(-1),
        rnn_b_a.reshape(-1), rnn_b_x.reshape(-1), rnn_lam.reshape(-1)])
    small_local = jnp.pad(small_local, (0, 4096 - small_local.shape[0])).reshape(32, 128)
    flat = lambda g: g.reshape(N_DEV * g.shape[1], D_MODEL)
    c_all, modr, win_t, sm = ada_modulation(jnp.broadcast_to(c, (8, D_MODEL)), ada_w.reshape(4, D_MODEL, CG),
                                            ada_b.reshape(4, 1, CG), _Gather([big[0], small_local]))
    win_t = flat(win_t)
    sm = sm.reshape(N_DEV, 4096)

    def full_vec(off, rows, per):
        piece = sm[:, off:off + rows * per].reshape(N_DEV, rows, per)
        return jnp.moveaxis(piece, 0, 1).reshape(rows, N_DEV * per)

    lng_f, lnb_f = full_vec(0, 4, 128), full_vec(512, 4, 128)
    cw_f, cb_f = full_vec(1024, 4, 192), full_vec(1792, 1, 192)
    ba_f, bx_f, lam_f = full_vec(1984, 2, 192), full_vec(2368, 2, 192), full_vec(2752, 2, 192)
    wa_bd = [_blockdiag(rnn_w_a[0, d]).astype(BF16) for d in range(2)]
    wx_bd = [_blockdiag(rnn_w_x[0, d]).astype(BF16) for d in range(2)]

    mods = modr.reshape(N_DEV, 4, 8, CG)[:, :, 0, :]
    mods = jnp.moveaxis(mods, 0, 1).reshape(4, 3, D_MODEL).reshape(12, D_MODEL)
    rope = _rope_tables(s)
    ln = lambda k: (lng_f[k:k + 1], lnb_f[k:k + 1])

    q, kk, v, wout = attn_in_fwd(x0, mods, 0, win_t, rope, rider=_Gather([big[1]]))
    wout = flat(wout)
    o, *got = attn_fwd(q, kk, v, attn_sinks, rider=_Gather([big[4], big[5]]))
    w1t_0, w2_0 = (flat(g) for g in got)
    x1, y0, rout = post_fwd(o, wout, x0, mods, 0, *ln(0), rider=_Gather([big[3]]))
    rout = flat(rout)
    x2, y1, ra0, r0, *got = mlp_fwd(x1, mods, 1, w1t_0, w2_0, *ln(1), rider=_Gather([big[2], big[6], big[7]]))
    rin_t, w1t_1, w2_1 = (flat(g) for g in got)
    xr, gt = rnn_in_fwd(x2, mods, 2, rin_t)
    xc = conv_fwd(xr, cw_f, cb_f)
    hf, = lru_fwd(xc, wa_bd[0], wx_bd[0], ba_f[0:1], bx_f[0:1], lam_f[0:1], False)
    hb, hsum = lru_fwd(xc, wa_bd[1], wx_bd[1], ba_f[1:2], bx_f[1:2], lam_f[1:2], True, other=hf)
    x3, y2, ypre = post_fwd(None, rout, x2, mods, 2, *ln(2), gate_act=(gt, hsum))
    y3, ra1, r1 = mlp_fwd(x3, mods, 3, w1t_1, w2_1, *ln(3), last=True)

    dx3, da1, h3, dy3, sums3 = mlp_bwd(target, x3, y3, ra1, mods, 3, w1t_1, w2_1, lng_f[3:4], lnb=lnb_f[3:4])
    g_w1t_1 = wgrad(da1, h3, "wgrad_w1_1")
    g_w2_1 = wgrad(r1, dy3, "wgrad_w2_1")
    dres2, dy2, sums2a, dhs, dgt, p_w1t_1 = post_bwd(dx3, x2, y2, mods, 2, rout, lng_f[2:3], gate_act=(gt, hsum),
                                                     rider=_AllToAll([g_w1t_1]))
    g_rout = wgrad(ypre, dy2, "wgrad_rnn_out")
    dxc_f, dwa_f, dwx_f, dba_f, dbx_f, dlam_f, p_w2_1, p_rout = lru_bwd(
        xc, dhs, hf, wa_bd[0], wx_bd[0], ba_f[0:1], bx_f[0:1], lam_f[0:1], False, rider=_AllToAll([g_w2_1, g_rout]))
    dxc_b, dwa_b, dwx_b, dba_b, dbx_b, dlam_b = lru_bwd(xc, dhs, hb, wa_bd[1], wx_bd[1], ba_f[1:2], bx_f[1:2],
                                                        lam_f[1:2], True)
    dx2, dzz, h2, sums2b, dcw, dcb = rnn_in_bwd(dxc_f, dxc_b, xr, cw_f, dgt, x2, mods, 2, rin_t, dres2)
    g_rin_t = wgrad(dzz, h2, "wgrad_rnn_in")
    d_wa = jnp.stack([_diag_blocks(dwa_f), _diag_blocks(dwa_b)])
    d_wx = jnp.stack([_diag_blocks(dwx_f), _diag_blocks(dwx_b)])
    nflat = d_wa.size // N_DEV
    gates = jnp.concatenate([d_wa.reshape(N_DEV, nflat), d_wx.reshape(N_DEV, nflat)], axis=1)
    gates = gates.reshape(N_DEV, 2 * nflat // 128, 128)
    dx1, da0, h1, dy1, sums1, p_rin_t, p_gates = mlp_bwd(dx2, x1, y1, ra0, mods, 1, w1t_0, w2_0, lng_f[1:2],
                                                         rider=_AllToAll([g_rin_t, gates]))
    gates_sum = part_sum(p_gates, "part_sum_gates")
    g_w1t_0 = wgrad(da0, h1, "wgrad_w1_0")
    g_w2_0, p_w1t_0 = wgrad(r0, dy1, "wgrad_w2_0", rider=_AllToAll([g_w1t_0]))
    dres0, dy0, sums0a, do = post_bwd(dx1, x0, y0, mods, 0, wout, lng_f[0:1])
    g_wout = wgrad(o, dy0, "wgrad_attn_out")
    dq, dk, dv, dsink, wag, p_w2_0, p_wout = attn_bwd(
        q, kk, v, do, attn_sinks, rider=_Multi(_Gather([gates_sum]), _AllToAll([g_w2_0, g_wout])))
    dx0, dqkv, h0, sums0b = attn_in_bwd(dq, dk, dv, rope, x0, mods, 0, win_t, dres0)
    g_win_t = wgrad(dqkv, h0, "wgrad_attn_in")

    sums = [sums0a + sums0b, sums1, sums2a + sums2b, sums3]
    gmod = jnp.stack([t[0:3] for t in sums])
    gsend = jnp.moveaxis(gmod.reshape(4, N_DEV, CG), 1, 0)
    gsend = jnp.pad(gsend, ((0, 0), (0, 4), (0, 0)))
    c_t = c_all[:, 0, :].T
    sq_err = jnp.sum(sums3[5]).reshape(1, 1)
    tail = jnp.concatenate([
        _cols(dcw, 192), _cols(dcb, 192),
        _cols(jnp.concatenate([dba_f, dba_b]), 192), _cols(jnp.concatenate([dbx_f, dbx_b]), 192),
        _cols(jnp.concatenate([dlam_f, dlam_b]), 192),
        _cols(jnp.stack([t[3] for t in sums]), 128), _cols(jnp.stack([t[4] for t in sums]), 128),
        jnp.broadcast_to(dsink[:, 0:8], (N_DEV, 8)), jnp.broadcast_to(sq_err, (N_DEV, 1))], axis=1)
    tail = jnp.pad(tail, ((0, 0), (0, 32 * 128 - tail.shape[1]))).reshape(N_DEV, 32, 128)
    g_ada_w, g_ada_b, red, p_win_t = epilogue(gsend, c_t, tail, _AllToAll([g_win_t]))
    grads = {"ada_w": g_ada_w.reshape(ada_w.shape), "ada_b": g_ada_b[0:4].reshape(ada_b.shape)}

    big_parts = [p_win_t, p_wout, p_rin_t, p_rout, p_w1t_0, p_w2_0, p_w1t_1, p_w2_1]
    gsum = [part_sum(p, "part_sum_%d" % i) for i, p in enumerate(big_parts)]
    grads.update({
        "attn_w_in": gsum[0].T[None], "attn_w_out": gsum[1][None],
        "rnn_w_in": gsum[2].T[None], "rnn_w_out": gsum[3][None],
        "mlp_w1": jnp.stack([gsum[4].T, gsum[6].T]), "mlp_w2": jnp.stack([gsum[5], gsum[7]]),
    })
    wag = wag.reshape(N_DEV, 2 * nflat)
    grads["rnn_w_a"] = wag[:, :nflat].reshape(rnn_w_a.shape)
    grads["rnn_w_x"] = wag[:, nflat:].reshape(rnn_w_x.shape)
    tl = red.reshape(-1)
    loss = 0.5 * tl[3144] / D_MODEL
    grads["rnn_conv_w"] = tl[0:768].reshape(rnn_conv_w.shape)
    grads["rnn_conv_b"] = tl[768:960].reshape(rnn_conv_b.shape)
    grads["rnn_b_a"] = tl[960:1344].reshape(rnn_b_a.shape)
    grads["rnn_b_x"] = tl[1344:1728].reshape(rnn_b_x.shape)
    grads["rnn_lam"] = tl[1728:2112].reshape(rnn_lam.shape)
    grads["ln_g"] = tl[2112:2624].reshape(ln_g.shape)
    grads["ln_b"] = tl[2624:3136].reshape(ln_b.shape)
    grads["attn_sinks"] = tl[3136:3144].reshape(attn_sinks.shape)

    delta, new_m, new_v = {}, {}, {}
    for n in names:
        delta[n], new_m[n], new_v[n] = adamw(weights[n], grads[n], moments_m[n], moments_v[n], "adamw_" + n)
    return (loss, dx0.reshape(x.shape), *[grads[n] for n in names], *[delta[n] for n in names],
            *[new_m[n] for n in names], *[new_v[n] for n in names])
```

```python
import functools
import math

import jax
import jax.numpy as jnp
from jax import lax
from jax.experimental import pallas as pl
from jax.experimental.pallas import tpu as pltpu

F32, BF16 = jnp.float32, jnp.bfloat16
MESH = pl.DeviceIdType.MESH

D_MODEL = 1024
N_Q, N_KV, HEAD = 8, 2, 128
ROT, THETA = 32, 500000.0
QBLK = 128
D_QKV = (N_Q + 2 * N_KV) * HEAD
D_RNN, N_RB, RB_W = 1536, 16, 96
CG = 384
N_CG = D_RNN // CG
D_FF = 4096
FF_CHUNK = 1024
DEPTH = 2
ALPHA = (2.0 * DEPTH) ** 0.25
LN_EPS = 1e-5
LRU_C = 8.0
N_DEV = 8
LR, B1, B2, ADAM_EPS, WD, STEP = 0.001, 0.9, 0.999, 1e-8, 0.01, 10

VMEM_LIMIT = 56 * 1024 * 1024
TM_MM = 512
TM_MLP = 256
TM_MLP_FWD = 512
TT_RNN = 512
TK_WG = 2048


def _nn(a, b):
    return jnp.dot(a, b, preferred_element_type=F32)


def _nt(a, b):
    return lax.dot_general(a, b, (((1,), (1,)), ((), ())), preferred_element_type=F32)


def _tn(a, b):
    return lax.dot_general(a, b, (((0,), (0,)), ((), ())), preferred_element_type=F32)


def _blk(n, pref):
    t = min(n, pref)
    assert n % t == 0, (n, pref)
    return t


def _params(**kw):
    return pltpu.CompilerParams(vmem_limit_bytes=VMEM_LIMIT, **kw)


def _row(tm, w):
    return pl.BlockSpec((tm, w), lambda i: (i, 0))


def _res(shape):
    return pl.BlockSpec(shape, lambda i: (0,) * len(shape), pipeline_mode=pl.Buffered(1))


def _mod(mod_ref, k):
    return mod_ref[3 * k:3 * k + 1, :], mod_ref[3 * k + 1:3 * k + 2, :], mod_ref[3 * k + 2:3 * k + 3, :]


def _ln_stats(z):
    mu = jnp.mean(z, axis=-1, keepdims=True)
    zc = z - mu
    var = jnp.mean(zc * zc, axis=-1, keepdims=True)
    rstd = lax.rsqrt(var + LN_EPS)
    return zc * rstd, rstd


def _ln_bwd(dxo, xhat, rstd, g):
    dxh = dxo * g
    m1 = jnp.mean(dxh, axis=-1, keepdims=True)
    m2 = jnp.mean(dxh * xhat, axis=-1, keepdims=True)
    return rstd * (dxh - m1 - xhat * m2)


def _colsum(v):
    return jnp.sum(v, axis=0, keepdims=True)


def _sigmoid(v):
    return 0.5 * jnp.tanh(0.5 * v) + 0.5


def _gelu_parts(v):
    k = math.sqrt(2.0 / math.pi)
    u = k * (v + 0.044715 * v * v * v)
    t = jnp.tanh(u)
    g = 0.5 * v * (1.0 + t)
    dg = 0.5 * (1.0 + t) + 0.5 * v * (1.0 - t * t) * k * (1.0 + 3.0 * 0.044715 * v * v)
    return g, dg


def _me():
    return lax.axis_index("x"), lax.axis_index("y"), lax.axis_index("c")


def _idx(p):
    return 4 * p[0] + 2 * p[1] + p[2]


def _peers(me):
    x, y, c = me
    out = []
    for k in range(1, N_DEV):
        out.append((1 - x if k & 4 else x, 1 - y if k & 2 else y, 1 - c if k & 1 else c))
    return out


class _Gather:
    def __init__(self, srcs):
        self.srcs = list(srcs)
        n = len(self.srcs)
        self.out_shape = [jax.ShapeDtypeStruct((N_DEV,) + s.shape, s.dtype) for s in self.srcs]
        self.scratch = [pltpu.SemaphoreType.DMA((n, 7)), pltpu.SemaphoreType.DMA((n, 7)),
                        pltpu.SemaphoreType.DMA((n,))]

    @staticmethod
    def _places():
        x, y, c = me = _me()
        return me, (x, y, 1 - c), [(1 - x, y), (x, 1 - y), (1 - x, 1 - y)]

    @staticmethod
    def _copy(outs, sems, t, k, block, to, src=None):
        slot = outs[t].at[_idx(block)]
        return pltpu.make_async_remote_copy(
            src_ref=slot if src is None else src, dst_ref=slot, send_sem=sems[0].at[t, k],
            recv_sem=sems[1].at[t, k], device_id=to, device_id_type=MESH)

    def _firsts(self, ins, outs, sems):
        me, sibling, chips = self._places()
        out = []
        for t in range(len(ins)):
            out.append(self._copy(outs, sems, t, 0, me, sibling, src=ins[t]))
            out += [self._copy(outs, sems, t, 1 + j, me, (*chip, me[2]), src=ins[t]) for j, chip in enumerate(chips)]
        return out

    def _locals(self, ins, outs, sems):
        me = _me()
        return [pltpu.make_async_copy(ins[t], outs[t].at[_idx(me)], sems[2].at[t]) for t in range(len(ins))]

    def start(self, ins, outs, sems):
        for cp in self._locals(ins, outs, sems) + self._firsts(ins, outs, sems):
            cp.start()

    def mid(self, ins, outs, sems):
        me, sibling, chips = self._places()
        for j, chip in enumerate(chips):
            for t in range(len(ins)):
                self._copy(outs, sems, t, 1 + j, (*chip, me[2]), me).wait_recv()
                self._copy(outs, sems, t, 4 + j, (*chip, me[2]), sibling).start()

    def finish(self, ins, outs, sems):
        me, sibling, chips = self._places()
        for t in range(len(ins)):
            self._copy(outs, sems, t, 0, sibling, me).wait_recv()
            for j, chip in enumerate(chips):
                self._copy(outs, sems, t, 4 + j, (*chip, 1 - me[2]), me).wait_recv()
        for cp in self._firsts(ins, outs, sems):
            cp.wait_send()
        for j, chip in enumerate(chips):
            for t in range(len(ins)):
                self._copy(outs, sems, t, 4 + j, (*chip, me[2]), sibling).wait_send()
        for cp in self._locals(ins, outs, sems):
            cp.wait()


class _AllToAll:
    def __init__(self, srcs):
        self.srcs = list(srcs)
        n = len(self.srcs)
        self.out_shape = [jax.ShapeDtypeStruct(s.shape, s.dtype) for s in self.srcs]
        self.scratch = [pltpu.SemaphoreType.DMA((n, 7)), pltpu.SemaphoreType.DMA((n, 7)),
                        pltpu.SemaphoreType.DMA((n,))]

    def _copies(self, ins, outs, sems):
        me = _me()
        loc, rem = [], []
        for t in range(len(ins)):
            loc.append(pltpu.make_async_copy(ins[t].at[_idx(me)], outs[t].at[_idx(me)], sems[2].at[t]))
            for k, p in enumerate(_peers(me)):
                rem.append(pltpu.make_async_remote_copy(
                    src_ref=ins[t].at[_idx(p)], dst_ref=outs[t].at[_idx(me)], send_sem=sems[0].at[t, k],
                    recv_sem=sems[1].at[t, k], device_id=p, device_id_type=MESH))
        return loc, rem

    def start(self, ins, outs, sems):
        loc, rem = self._copies(ins, outs, sems)
        for cp in loc + rem:
            cp.start()

    def mid(self, ins, outs, sems):
        pass

    def finish(self, ins, outs, sems):
        me = _me()
        for t in range(len(ins)):
            for k, p in enumerate(_peers(me)):
                slot = outs[t].at[_idx(p)]
                pltpu.make_async_remote_copy(
                    src_ref=slot, dst_ref=slot, send_sem=sems[0].at[t, k], recv_sem=sems[1].at[t, k],
                    device_id=p, device_id_type=MESH).wait_recv()
        loc, rem = self._copies(ins, outs, sems)
        for cp in rem:
            cp.wait_send()
        for cp in loc:
            cp.wait()


class _Multi:
    def __init__(self, *exs):
        self.exs = exs
        self.srcs = [s for e in exs for s in e.srcs]
        self.out_shape = [s for e in exs for s in e.out_shape]
        self.scratch = [s for e in exs for s in e.scratch]

    def _each(self, ins, outs, sems):
        i = j = 0
        for e in self.exs:
            n, m = len(e.srcs), len(e.scratch)
            yield e, ins[i:i + n], outs[i:i + n], sems[j:j + m]
            i, j = i + n, j + m

    def start(self, ins, outs, sems):
        for e, a, b, c in self._each(ins, outs, sems):
            e.start(a, b, c)

    def mid(self, ins, outs, sems):
        for e, a, b, c in self._each(ins, outs, sems):
            e.mid(a, b, c)

    def finish(self, ins, outs, sems):
        for e, a, b, c in self._each(ins, outs, sems):
            e.finish(a, b, c)


def _call(body, *, name, grid, in_specs, out_specs, out_shape, args, scratch_shapes=(), rider=None):
    in_specs, out_specs, out_shape = list(in_specs), list(out_specs), list(out_shape)
    scratch_shapes = list(scratch_shapes)
    if rider is None:
        return pl.pallas_call(body, name=name, grid=grid, out_shape=out_shape, in_specs=in_specs,
                              out_specs=out_specs, scratch_shapes=scratch_shapes, compiler_params=_params())(*args)
    nci, nco, ncs, nr = len(in_specs), len(out_shape), len(scratch_shapes), len(rider.srcs)
    nsteps = math.prod(grid)
    assert nsteps >= 2, (name, grid)
    mid = max(1, (7 * nsteps) // 8)

    def full(*refs):
        ci, ri = refs[:nci], refs[nci:nci + nr]
        co, ro = refs[nci + nr:nci + nr + nco], refs[nci + nr + nco:nci + 2 * nr + nco]
        cs, rs = refs[nci + 2 * nr + nco:nci + 2 * nr + nco + ncs], refs[nci + 2 * nr + nco + ncs:]
        step = pl.program_id(0)
        for d in range(1, len(grid)):
            step = step * grid[d] + pl.program_id(d)

        @pl.when(step == 0)
        def _():
            rider.start(ri, ro, rs)

        @pl.when(step == mid)
        def _():
            rider.mid(ri, ro, rs)

        body(*ci, *co, *cs)

        @pl.when(step == nsteps - 1)
        def _():
            rider.finish(ri, ro, rs)

    any_spec = pl.BlockSpec(memory_space=pl.ANY)
    return pl.pallas_call(
        full, name=name, grid=grid, out_shape=out_shape + rider.out_shape,
        in_specs=in_specs + [any_spec] * nr, out_specs=out_specs + [any_spec] * nr,
        scratch_shapes=scratch_shapes + rider.scratch, compiler_params=_params(),
    )(*args, *rider.srcs)


def _a2a_start(srcs, dsts, send_sems, recv_sems, local_sems, me, sem_base=0):
    peers = _peers(me)
    started = []
    for t in range(len(srcs)):
        loc = pltpu.make_async_copy(srcs[t].at[_idx(me)], dsts[t].at[_idx(me)], local_sems.at[sem_base + t])
        loc.start()
        started.append(("local", loc))
        for k, p in enumerate(peers):
            cp = pltpu.make_async_remote_copy(
                src_ref=srcs[t].at[_idx(p)], dst_ref=dsts[t].at[_idx(me)],
                send_sem=send_sems.at[sem_base + t, k], recv_sem=recv_sems.at[sem_base + t, k],
                device_id=p, device_id_type=MESH)
            cp.start()
            started.append(("remote", cp))
    return started


def _a2a_finish(started, dsts, send_sems, recv_sems, me, sem_base=0):
    peers = _peers(me)
    for t in range(len(dsts)):
        for k, p in enumerate(peers):
            slot = dsts[t].at[_idx(p)]
            pltpu.make_async_remote_copy(
                src_ref=slot, dst_ref=slot, send_sem=send_sems.at[sem_base + t, k],
                recv_sem=recv_sems.at[sem_base + t, k], device_id=p, device_id_type=MESH).wait_recv()
    for kind, cp in started:
        if kind == "local":
            cp.wait()
        else:
            cp.wait_send()


def ada_modulation(c8, ada_w, ada_b, ride):
    nr = len(ride.srcs)

    def body(c_ref, w_ref, b_ref, *rest):
        ride_in, (call_ref, modr_ref), ride_out = rest[:nr], rest[nr:nr + 2], rest[nr + 2:2 * nr + 2]
        modp, send_sems, recv_sems, local_sems = rest[2 * nr + 2:2 * nr + 6]
        ride_sems = rest[2 * nr + 6:]
        ride.start(ride_in, ride_out, ride_sems)
        me = _me()
        peers = _peers(me)
        sends = []
        for k, p in enumerate(peers):
            cp = pltpu.make_async_remote_copy(
                src_ref=c_ref, dst_ref=call_ref.at[_idx(me)], send_sem=send_sems.at[0, k],
                recv_sem=recv_sems.at[0, k], device_id=p, device_id_type=MESH)
            cp.start()
            sends.append(cp)
        call_ref[_idx(me)] = c_ref[...]
        for k, p in enumerate(peers):
            slot = call_ref.at[_idx(p)]
            pltpu.make_async_remote_copy(
                src_ref=slot, dst_ref=slot, send_sem=send_sems.at[0, k], recv_sem=recv_sems.at[0, k],
                device_id=p, device_id_type=MESH).wait_recv()
        for cp in sends:
            cp.wait_send()
        cv = call_ref[...].reshape(N_DEV * 8, D_MODEL)
        s = (cv * _sigmoid(cv)).astype(BF16)
        for k in range(4):
            res = _nn(s, w_ref[k].astype(BF16)) + b_ref[k]
            for j in range(N_DEV):
                modp[j, 8 * k:8 * k + 8, :] = res[8 * j:8 * j + 8, :]
        started = _a2a_start([modp], [modr_ref], send_sems, recv_sems, local_sems, me, sem_base=1)
        _a2a_finish(started, [modr_ref], send_sems, recv_sems, me, sem_base=1)
        ride.mid(ride_in, ride_out, ride_sems)
        ride.finish(ride_in, ride_out, ride_sems)

    vm, hbm = pl.BlockSpec(memory_space=pltpu.VMEM), pl.BlockSpec(memory_space=pl.ANY)
    return pl.pallas_call(
        body, name="ada_modulation",
        out_shape=[jax.ShapeDtypeStruct((N_DEV, 8, D_MODEL), F32), jax.ShapeDtypeStruct((N_DEV, 32, CG), F32)]
        + ride.out_shape,
        in_specs=[vm, vm, vm] + [hbm] * nr, out_specs=[vm, vm] + [hbm] * nr,
        scratch_shapes=[pltpu.VMEM((N_DEV, 32, CG), F32), pltpu.SemaphoreType.DMA((2, 7)),
                        pltpu.SemaphoreType.DMA((2, 7)), pltpu.SemaphoreType.DMA((2,))] + ride.scratch,
        compiler_params=_params(),
    )(c8, ada_w, ada_b, *ride.srcs)


def epilogue(gsend, c_t, tail, ride):
    nr = len(ride.srcs)
    rt = tail.shape[1]

    def body(g_ref, ct_ref, t_ref, *rest):
        ride_in, (gw_ref, gb_ref, red_ref), ride_out = rest[:nr], rest[nr:nr + 3], rest[nr + 3:2 * nr + 3]
        grecv, trecv, send_sems, recv_sems, local_sems = rest[2 * nr + 3:2 * nr + 8]
        ride_sems = rest[2 * nr + 8:]
        ride.start(ride_in, ride_out, ride_sems)
        me = _me()
        started = _a2a_start([g_ref, t_ref], [grecv, trecv], send_sems, recv_sems, local_sems, me)
        _a2a_finish(started, [grecv, trecv], send_sems, recv_sems, me)
        acc = trecv[0]
        for j in range(1, N_DEV):
            acc = acc + trecv[j]
        red_ref[...] = acc
        ct = ct_ref[...]
        st = (ct * _sigmoid(ct)).astype(BF16).astype(F32)
        gb = jnp.zeros((8, CG), F32)
        for b in range(N_DEV):
            gb = gb + grecv[b]
        gb_ref[...] = gb
        for k in range(4):
            acc = jnp.zeros((D_MODEL, CG), F32)
            for b in range(N_DEV):
                row = grecv[b, k:k + 1, :].astype(BF16).astype(F32)
                acc = acc + st[:, b:b + 1] * row
            gw_ref[k] = acc
        ride.mid(ride_in, ride_out, ride_sems)
        ride.finish(ride_in, ride_out, ride_sems)

    vm, hbm = pl.BlockSpec(memory_space=pltpu.VMEM), pl.BlockSpec(memory_space=pl.ANY)
    return pl.pallas_call(
        body, name="epilogue",
        out_shape=[jax.ShapeDtypeStruct((4, D_MODEL, CG), F32), jax.ShapeDtypeStruct((8, CG), F32),
                   jax.ShapeDtypeStruct((rt, 128), F32)] + ride.out_shape,
        in_specs=[vm, vm, vm] + [hbm] * nr, out_specs=[vm, vm, vm] + [hbm] * nr,
        scratch_shapes=[pltpu.VMEM((N_DEV, 8, CG), F32), pltpu.VMEM((N_DEV, rt, 128), F32),
                        pltpu.SemaphoreType.DMA((2, 7)), pltpu.SemaphoreType.DMA((2, 7)),
                        pltpu.SemaphoreType.DMA((2,))] + ride.scratch,
        compiler_params=_params(),
    )(gsend, c_t, tail, *ride.srcs)


def _rope(t, cos, s1, s2):
    return t * cos + pltpu.roll(t, 16, 1) * s1 + pltpu.roll(t, HEAD - 16, 1) * s2


def _rope_bwd(d, cos, s1, s2):
    return d * cos + pltpu.roll(d * s1, HEAD - 16, 1) + pltpu.roll(d * s2, 16, 1)


def attn_in_fwd(x, mods, k, win_t, rope, rider=None):
    s = x.shape[0]
    tm = _blk(s, TM_MM)

    def body(x_ref, mod_ref, w_ref, c_ref, s1_ref, s2_ref, q_ref, k_ref, v_ref):
        shift, scale, _ = _mod(mod_ref, k)
        h = (x_ref[...] * (1.0 + scale) + shift).astype(BF16)
        qkv = _nt(h, w_ref[...])
        cos, s1, s2 = c_ref[...], s1_ref[...], s2_ref[...]
        for hh in range(N_Q + N_KV):
            r = _rope(qkv[:, HEAD * hh:HEAD * (hh + 1)], cos, s1, s2).astype(BF16)
            if hh < N_Q:
                q_ref[:, HEAD * hh:HEAD * (hh + 1)] = r
            else:
                k_ref[:, HEAD * (hh - N_Q):HEAD * (hh - N_Q + 1)] = r
        v_ref[...] = qkv[:, HEAD * (N_Q + N_KV):].astype(BF16)

    return _call(
        body, name="attn_in_fwd", grid=(s // tm,),
        out_shape=[jax.ShapeDtypeStruct((s, N_Q * HEAD), BF16), jax.ShapeDtypeStruct((s, N_KV * HEAD), BF16),
                   jax.ShapeDtypeStruct((s, N_KV * HEAD), BF16)],
        in_specs=[_row(tm, D_MODEL), _res(mods.shape), _res(win_t.shape),
                  _row(tm, HEAD), _row(tm, HEAD), _row(tm, HEAD)],
        out_specs=[_row(tm, N_Q * HEAD), _row(tm, N_KV * HEAD), _row(tm, N_KV * HEAD)],
        args=(x, mods, win_t, *rope), rider=rider)


QPAIR = 2


def _kv_specs(nblk):
    w = N_KV * HEAD
    return [pl.BlockSpec((QBLK, w), lambda n: (jnp.maximum(QPAIR * n - 1, 0), 0)),
            pl.BlockSpec((QBLK, w), lambda n: (QPAIR * n, 0)),
            pl.BlockSpec((QBLK, w), lambda n: (QPAIR * n + 1, 0)),
            pl.BlockSpec((QBLK, w), lambda n: (jnp.minimum(QPAIR * n + 2, nblk - 1), 0))]


GROUP = N_Q // N_KV


def _attn_mask(n, s):
    qi = lax.broadcasted_iota(jnp.int32, (GROUP * QBLK, 3 * QBLK), 0) & (QBLK - 1)
    kj = lax.broadcasted_iota(jnp.int32, (GROUP * QBLK, 3 * QBLK), 1)
    rel = kj - QBLK - qi
    kpos = kj + (n - 1) * QBLK
    return (jnp.abs(rel) <= QBLK) & (kpos >= 0) & (kpos < s)


def _stack_heads(ref, qb, kv):
    rows = slice(QBLK * qb, QBLK * (qb + 1))
    return jnp.concatenate([ref[rows, HEAD * (GROUP * kv + j):HEAD * (GROUP * kv + j + 1)] for j in range(GROUP)],
                           axis=0)


def _stack_sinks(sink_ref, kv):
    row = lax.broadcasted_iota(jnp.int32, (GROUP * QBLK, 1), 0)
    out = jnp.full((GROUP * QBLK, 1), sink_ref[0, GROUP * kv + GROUP - 1], F32)
    for j in range(GROUP - 2, -1, -1):
        out = jnp.where(row < QBLK * (j + 1), sink_ref[0, GROUP * kv + j], out)
    return out


def _attn_probs(qh, kh, valid, sink):
    sc = _nt(qh, kh) * (HEAD ** -0.5)
    sc = jnp.where(valid, sc, -1e30)
    m = jnp.maximum(jnp.max(sc, axis=-1, keepdims=True), sink)
    p = jnp.exp(sc - m)
    es = jnp.exp(sink - m)
    denom = jnp.sum(p, axis=-1, keepdims=True) + es
    return p / denom, es / denom


def attn_fwd(q, kk, v, sinks, rider=None):
    s = q.shape[0]
    nblk = s // QBLK

    def body(sink_ref, q_ref, k0, k1, k2, k3, v0, v1, v2, v3, o_ref):
        n = pl.program_id(0)
        kall = jnp.concatenate([k0[...], k1[...], k2[...], k3[...]], axis=0)
        vall = jnp.concatenate([v0[...], v1[...], v2[...], v3[...]], axis=0)
        for qb in range(QPAIR):
            valid = _attn_mask(QPAIR * n + qb, s)
            keys = slice(QBLK * qb, QBLK * (qb + 3))
            for kv in range(N_KV):
                cols = slice(HEAD * kv, HEAD * (kv + 1))
                probs, _ = _attn_probs(_stack_heads(q_ref, qb, kv), kall[keys, cols], valid,
                                       _stack_sinks(sink_ref, kv))
                og = _nn(probs.astype(BF16), vall[keys, cols]).astype(BF16)
                for j in range(GROUP):
                    hq = GROUP * kv + j
                    o_ref[QBLK * qb:QBLK * (qb + 1), HEAD * hq:HEAD * (hq + 1)] = og[QBLK * j:QBLK * (j + 1), :]

    qspec = pl.BlockSpec((QPAIR * QBLK, N_Q * HEAD), lambda n: (n, 0))
    return _call(
        body, name="attn_fwd", grid=(nblk // QPAIR,),
        out_shape=[jax.ShapeDtypeStruct((s, N_Q * HEAD), BF16)],
        in_specs=[pl.BlockSpec(memory_space=pltpu.SMEM), qspec] + _kv_specs(nblk) + _kv_specs(nblk),
        out_specs=[qspec],
        args=(sinks, q, kk, kk, kk, kk, v, v, v, v), rider=rider)


def post_fwd(ypre, w, x, mods, k, lng, lnb, gate_act=None, rider=None):
    s = x.shape[0]
    tm = _blk(s, TM_MM)
    kdim = w.shape[0]
    rnn = gate_act is not None

    def body(*refs):
        if rnn:
            gt_ref, hs_ref, w_ref, x_ref, mod_ref, g_ref, b_ref, xo_ref, y_ref, yp_ref = refs
            act, _ = _gelu_parts(gt_ref[...].astype(F32))
            yp = (hs_ref[...].astype(F32) * act).astype(BF16)
            yp_ref[...] = yp
        else:
            yp_ref, w_ref, x_ref, mod_ref, g_ref, b_ref, xo_ref, y_ref = refs
            yp = yp_ref[...]
        _, _, gate = _mod(mod_ref, k)
        y = _nn(yp, w_ref[...])
        y_ref[...] = y
        xhat, _ = _ln_stats(ALPHA * x_ref[...] + (1.0 + gate) * y)
        xo_ref[...] = xhat * g_ref[...] + b_ref[...]

    act_in = list(gate_act) if rnn else [ypre]
    out_shape = [jax.ShapeDtypeStruct((s, D_MODEL), F32), jax.ShapeDtypeStruct((s, D_MODEL), F32)]
    out_specs = [_row(tm, D_MODEL), _row(tm, D_MODEL)]
    if rnn:
        out_shape.append(jax.ShapeDtypeStruct((s, kdim), BF16))
        out_specs.append(_row(tm, kdim))
    return _call(
        body, name="rnn_post_fwd" if rnn else "attn_post_fwd", grid=(s // tm,),
        out_shape=out_shape,
        in_specs=[_row(tm, kdim)] * len(act_in) + [_res(w.shape), _row(tm, D_MODEL), _res(mods.shape),
                                                    _res(lng.shape), _res(lnb.shape)],
        out_specs=out_specs,
        args=(*act_in, w, x, mods, lng, lnb), rider=rider)


def mlp_fwd(x, mods, k, w1_t, w2, lng, lnb, rider=None, last=False):
    s = x.shape[0]
    tm = _blk(s, TM_MLP_FWD)

    def body(x_ref, mod_ref, w1_ref, w2_ref, g_ref, b_ref, *outs):
        xo_ref = None if last else outs[0]
        y_ref, ra_ref, r_ref = outs[-3:]
        xv = x_ref[...]
        shift, scale, gate = _mod(mod_ref, k)
        h = (xv * (1.0 + scale) + shift).astype(BF16)
        y = jnp.zeros((tm, D_MODEL), F32)
        for c in range(D_FF // FF_CHUNK):
            rows = slice(FF_CHUNK * c, FF_CHUNK * (c + 1))
            a = jnp.maximum(_nt(h, w1_ref[rows, :]), 0.0)
            r = (a * a).astype(BF16)
            ra_ref[:, rows] = a.astype(BF16)
            r_ref[:, rows] = r
            y = y + _nn(r, w2_ref[rows, :])
        y_ref[...] = y
        if not last:
            xhat, _ = _ln_stats(ALPHA * xv + (1.0 + gate) * y)
            xo_ref[...] = xhat * g_ref[...] + b_ref[...]

    nf = 1 if last else 2
    return _call(
        body, name="mlp_fwd_last" if last else "mlp_fwd", grid=(s // tm,),
        out_shape=[jax.ShapeDtypeStruct((s, D_MODEL), F32)] * nf + [jax.ShapeDtypeStruct((s, D_FF), BF16)] * 2,
        in_specs=[_row(tm, D_MODEL), _res(mods.shape), _res(w1_t.shape), _res(w2.shape),
                  _res(lng.shape), _res(lnb.shape)],
        out_specs=[_row(tm, D_MODEL)] * nf + [_row(tm, D_FF)] * 2,
        args=(x, mods, w1_t, w2, lng, lnb), rider=rider)


def rnn_in_fwd(x, mods, k, win_t):
    s = x.shape[0]
    tm = _blk(s, TM_MM)

    def body(x_ref, mod_ref, w_ref, xr_ref, gt_ref):
        shift, scale, _ = _mod(mod_ref, k)
        h = (x_ref[...] * (1.0 + scale) + shift).astype(BF16)
        xr_ref[...] = _nt(h, w_ref[0:D_RNN, :])
        gt_ref[...] = _nt(h, w_ref[D_RNN:2 * D_RNN, :]).astype(BF16)

    return pl.pallas_call(
        body, name="rnn_in_fwd", grid=(s // tm,),
        out_shape=[jax.ShapeDtypeStruct((s, D_RNN), F32), jax.ShapeDtypeStruct((s, D_RNN), BF16)],
        in_specs=[_row(tm, D_MODEL), _res(mods.shape), _res(win_t.shape)],
        out_specs=[_row(tm, D_RNN)] * 2,
        compiler_params=_params(),
    )(x, mods, win_t)


def _shift_rows(v, k, row):
    n = v.shape[0]
    r = pltpu.roll(v, k % n, 0)
    keep = (row >= k) if k > 0 else (row < n + k)
    return jnp.where(keep, r, 0.0)


def conv_fwd(xr, cw, cb):
    s = xr.shape[0]

    def body(x_ref, w_ref, b_ref, o_ref):
        xv = x_ref[...]
        row = lax.broadcasted_iota(jnp.int32, xv.shape, 0)
        o_ref[...] = (b_ref[...] + w_ref[0:1, :] * _shift_rows(xv, 2, row) + w_ref[1:2, :] * _shift_rows(xv, 1, row)
                      + w_ref[2:3, :] * xv + w_ref[3:4, :] * _shift_rows(xv, -1, row))

    slab = pl.BlockSpec((s, 128), lambda j: (0, j))
    return pl.pallas_call(
        body, name="conv_fwd", grid=(D_RNN // 128,),
        out_shape=jax.ShapeDtypeStruct((D_RNN // 128, s, 128), F32),
        in_specs=[slab, pl.BlockSpec((4, 128), lambda j: (0, j)), pl.BlockSpec((1, 128), lambda j: (0, j))],
        out_specs=pl.BlockSpec((None, s, 128), lambda j: (j, 0, 0)),
        compiler_params=_params(),
    )(xr, cw, cb)


def _softplus_neg(lam):
    z = -lam
    e = jnp.exp(-jnp.abs(z))
    u = 1.0 + e
    log1p = jnp.where(u == 1.0, e, jnp.log(u) * e / jnp.where(u == 1.0, 1.0, u - 1.0))
    return jnp.maximum(z, 0.0) + log1p, 1.0 / (1.0 + jnp.exp(lam))


def _lru_gates(xv, wa_ref, wx_ref, ba_ref, bx_ref, lam_ref):
    xb = xv.astype(BF16)
    r = _sigmoid(_nn(xb, wa_ref[...]) + ba_ref[...])
    i = _sigmoid(_nn(xb, wx_ref[...]) + bx_ref[...])
    sp, sg = _softplus_neg(lam_ref[...])
    la = r * (-LRU_C * sp)
    a = jnp.exp(la)
    th = jnp.tanh(la)
    m2 = -2.0 * th / (1.0 - th)
    rmult = lax.rsqrt(jnp.maximum(m2, 1e-37))
    return xb, r, i, sp, sg, a, m2 * rmult, rmult


SLABS = CG // 128
GRP, SEG = 32, 4


def _lru_specs(nt, tt, reverse):
    tmap = (lambda t: nt - 1 - t) if reverse else (lambda t: t)
    blk = pl.BlockSpec((tt, CG), lambda g, t: (tmap(t), g))
    slabs = pl.BlockSpec((SLABS, tt, 128), lambda g, t: (g, tmap(t), 0))
    wsp = pl.BlockSpec((None, CG, CG), lambda g, t: (g, 0, 0))
    vec = pl.BlockSpec((1, CG), lambda g, t: (0, g))
    return tmap, blk, slabs, wsp, vec


def _slab_rows(ref3):
    return jnp.concatenate([ref3[l] for l in range(ref3.shape[0])], axis=1)


def _perm_load(ref3, tt):
    out = []
    for l in range(SLABS):
        r = ref3.at[l]
        out.append(jnp.concatenate([r[pl.ds(GRP * g + i, 8, stride=SEG), :]
                                    for g in range(tt // GRP) for i in range(SEG)], axis=0))
    return jnp.concatenate(out, axis=1)


def _perm_scan(a, u, carry, reverse, emit):
    n, c = a.shape
    sub = lax.broadcasted_iota(jnp.int32, (8, c), 0)
    steps = [(8 - sh, sub < 8 - sh) if reverse else (sh, sub >= sh) for sh in (1, 2, 4)]
    order = range(SEG - 1, -1, -1) if reverse else range(SEG)
    for g in (range(n // GRP - 1, -1, -1) if reverse else range(n // GRP)):
        hs, ps = [None] * SEG, [None] * SEG
        h = p = None
        for i in order:
            rows = slice(GRP * g + 8 * i, GRP * g + 8 * i + 8)
            h = u[rows] if h is None else a[rows] * h + u[rows]
            p = a[rows] if p is None else a[rows] * p
            hs[i], ps[i] = h, p
        d, f = p, h
        for rot, keep in steps:
            d_s = jnp.where(keep, pltpu.roll(d, rot, 0), 1.0)
            f_s = jnp.where(keep, pltpu.roll(f, rot, 0), 0.0)
            f = d * f_s + f
            d = d * d_s
        end = f + d * carry
        if reverse:
            init = jnp.where(sub == 7, carry, pltpu.roll(end, 7, 0))
            carry = jnp.broadcast_to(end[0:1], (8, c))
        else:
            init = jnp.where(sub == 0, carry, pltpu.roll(end, 1, 0))
            carry = jnp.broadcast_to(end[7:8], (8, c))
        for i in range(SEG):
            emit(g, i, hs[i] + ps[i] * init)
    return carry


def lru_fwd(xc, wa, wx, ba, bx, lam, reverse, other=None):
    s = xc.shape[1]
    tt = _blk(s, TT_RNN)
    nt = s // tt

    def body(x_ref, wa_ref, wx_ref, ba_ref, bx_ref, lam_ref, *rest):
        if other is None:
            hs_ref, carry = rest
        else:
            oth_ref, hs_ref, sum_ref, carry = rest

        @pl.when(pl.program_id(1) == 0)
        def _():
            carry[...] = jnp.zeros(carry.shape, F32)

        xv = _perm_load(x_ref, tt)
        _, _, i, _, _, a, mult, _ = _lru_gates(xv, wa_ref, wx_ref, ba_ref, bx_ref, lam_ref)
        u = mult * (i * xv)

        def emit(g, j, rows):
            for l in range(SLABS):
                hs_ref.at[l][pl.ds(GRP * g + j, 8, stride=SEG), :] = rows[:, 128 * l:128 * (l + 1)]

        carry[...] = _perm_scan(a, u, carry[...], reverse, emit)
        if other is not None:
            for l in range(SLABS):
                sum_ref[:, 128 * l:128 * (l + 1)] = (hs_ref[l] + oth_ref[l]).astype(BF16)

    _, blk, slabs, wsp, vec = _lru_specs(nt, tt, reverse)
    extra = [] if other is None else [other]
    return pl.pallas_call(
        body, name="lru_fwd_rev" if reverse else "lru_fwd", grid=(N_CG, nt),
        out_shape=[jax.ShapeDtypeStruct(xc.shape, F32)] + [jax.ShapeDtypeStruct((s, D_RNN), BF16)] * len(extra),
        in_specs=[slabs, wsp, wsp, vec, vec, vec] + [slabs] * len(extra), out_specs=[slabs] + [blk] * len(extra),
        scratch_shapes=[pltpu.VMEM((8, CG), F32)],
        compiler_params=_params(),
    )(xc, wa, wx, ba, bx, lam, *extra)


def lru_bwd(xc, dhs, hs, wa, wx, ba, bx, lam, reverse, rider=None):
    s = xc.shape[1]
    tt = _blk(s, TT_RNN)
    nt = s // tt
    ng = tt // GRP
    back = not reverse

    def rows_of(v, g, i):
        return v[GRP * g + 8 * i:GRP * g + 8 * i + 8]

    def neighbour(v, past, edge, sub):
        out = []
        for g in range(ng):
            for i in range(SEG):
                if past and i > 0:
                    r = rows_of(v, g, i - 1)
                elif past:
                    e = edge if g == 0 else rows_of(v, g - 1, SEG - 1)[7:8]
                    r = jnp.where(sub == 0, e, pltpu.roll(rows_of(v, g, SEG - 1), 1, 0))
                elif i < SEG - 1:
                    r = rows_of(v, g, i + 1)
                else:
                    e = edge if g == ng - 1 else rows_of(v, g + 1, 0)[0:1]
                    r = jnp.where(sub == 7, e, pltpu.roll(rows_of(v, g, 0), 7, 0))
                out.append(r)
        return jnp.concatenate(out, axis=0)

    def body(x_ref, dh_ref, hs_ref, nb_ref, wa_ref, wx_ref, ba_ref, bx_ref, lam_ref,
             dx_ref, dwa_ref, dwx_ref, dba_ref, dbx_ref, dlam_ref, carry):
        t = pl.program_id(1)

        @pl.when(t == 0)
        def _():
            carry[...] = jnp.zeros(carry.shape, F32)
            dwa_ref[...] = jnp.zeros(dwa_ref.shape, F32)
            dwx_ref[...] = jnp.zeros(dwx_ref.shape, F32)
            dba_ref[...] = jnp.zeros(dba_ref.shape, F32)
            dbx_ref[...] = jnp.zeros(dbx_ref.shape, F32)
            dlam_ref[...] = jnp.zeros(dlam_ref.shape, F32)

        xv = _perm_load(x_ref, tt)
        xb, r, i, sp, sg, a, mult, rmult = _lru_gates(xv, wa_ref, wx_ref, ba_ref, bx_ref, lam_ref)
        sub = lax.broadcasted_iota(jnp.int32, (8, CG), 0)
        hsv = _perm_load(hs_ref, tt)
        nbv = _slab_rows(nb_ref)
        inner = t < nt - 1
        h_edge = jnp.where(inner, nbv[0:1, :] if reverse else nbv[7:8, :], 0.0)
        hprev = neighbour(hsv, not reverse, h_edge, sub)
        a_next = neighbour(a, reverse, carry[8:9, :], sub)
        dhv = _perm_load(dh_ref, tt)
        gl = [None] * (ng * SEG)

        def emit(gi, j, rows):
            gl[gi * SEG + j] = rows

        carry[0:8, :] = _perm_scan(a_next, dhv, carry[0:8, :], back, emit)
        g = jnp.concatenate(gl, axis=0)
        carry[8:9, :] = a[0:1, :] if back else a[tt - 1:tt, :]

        da = g * hprev
        dmult = g * (i * xv)
        di = g * mult * xv
        dla = da * a - dmult * (a * a) * rmult
        dpa = (dla * (-LRU_C * sp)) * r * (1.0 - r)
        dpx = di * i * (1.0 - i)
        dlam_ref[...] += _colsum(dla * (LRU_C * r * sg))
        dba_ref[...] += _colsum(dpa)
        dbx_ref[...] += _colsum(dpx)
        dpab, dpxb = dpa.astype(BF16), dpx.astype(BF16)
        dxv = g * mult * i + _nt(dpab, wa_ref[...]) + _nt(dpxb, wx_ref[...])
        for gi in range(ng):
            for j in range(SEG):
                for l in range(SLABS):
                    dx_ref.at[l][pl.ds(GRP * gi + j, 8, stride=SEG), :] = rows_of(dxv, gi, j)[:, 128 * l:128 * (l + 1)]
        dwa_ref[...] += _tn(xb, dpab)
        dwx_ref[...] += _tn(xb, dpxb)

    tmap, blk, slabs, wsp, vec = _lru_specs(nt, tt, back)
    per8 = tt // 8
    if reverse:
        nb = pl.BlockSpec((SLABS, 8, 128), lambda g, t: (g, jnp.minimum((tmap(t) + 1) * per8, s // 8 - 1), 0))
    else:
        nb = pl.BlockSpec((SLABS, 8, 128), lambda g, t: (g, jnp.maximum(tmap(t) * per8 - 1, 0), 0))
    return _call(
        body, name="lru_bwd_rev" if reverse else "lru_bwd", grid=(N_CG, nt),
        out_shape=[jax.ShapeDtypeStruct(xc.shape, F32), jax.ShapeDtypeStruct((N_CG, CG, CG), F32),
                   jax.ShapeDtypeStruct((N_CG, CG, CG), F32)] + [jax.ShapeDtypeStruct((1, D_RNN), F32)] * 3,
        in_specs=[slabs, slabs, slabs, nb, wsp, wsp, vec, vec, vec],
        out_specs=[slabs, wsp, wsp, vec, vec, vec],
        scratch_shapes=[pltpu.VMEM((16, CG), F32)],
        args=(xc, dhs, hs, hs, wa, wx, ba, bx, lam), rider=rider)


def _ln_part_bwd(dxo, x, y, gate, g, sums_ref, loss_head=None):
    xhat, rstd = _ln_stats(ALPHA * x + (1.0 + gate) * y)
    if loss_head is not None:
        err = xhat * g + loss_head[0] - loss_head[1]
        dxo = err * (1.0 / D_MODEL)
        sums_ref[5:6, :] += _colsum(err * err)
    dz = _ln_bwd(dxo, xhat, rstd, g)
    sums_ref[2:3, :] += _colsum(dz * y)
    sums_ref[3:4, :] += _colsum(dxo * xhat)
    sums_ref[4:5, :] += _colsum(dxo)
    return dz


def mlp_bwd(dxo, x, y, ra, mods, k, w1_t, w2, lng, lnb=None, rider=None):
    s = x.shape[0]
    tm = _blk(s, TM_MLP)
    head = lnb is not None

    def body(d_ref, x_ref, y_ref, ra_ref, mod_ref, w1_ref, w2_ref, g_ref, *rest):
        b_ref = rest[0] if head else None
        dx_ref, da_ref, h_ref, dy_ref, sums_ref = rest[1:] if head else rest

        @pl.when(pl.program_id(0) == 0)
        def _():
            sums_ref[...] = jnp.zeros(sums_ref.shape, F32)

        xv = x_ref[...]
        shift, scale, gate = _mod(mod_ref, k)
        if head:
            dz = _ln_part_bwd(None, xv, y_ref[...], gate, g_ref[...], sums_ref, (b_ref[...], d_ref[...]))
        else:
            dz = _ln_part_bwd(d_ref[...], xv, y_ref[...], gate, g_ref[...], sums_ref)
        dyb = (dz * (1.0 + gate)).astype(BF16)
        dy_ref[...] = dyb
        h = (xv * (1.0 + scale) + shift).astype(BF16)
        h_ref[...] = h
        dh = jnp.zeros((tm, D_MODEL), F32)
        for c in range(D_FF // FF_CHUNK):
            rows = slice(FF_CHUNK * c, FF_CHUNK * (c + 1))
            da = (_nt(dyb, w2_ref[rows, :]) * (2.0 * ra_ref[:, rows].astype(F32))).astype(BF16)
            da_ref[:, rows] = da
            dh = dh + _nn(da, w1_ref[rows, :])
        dx_ref[...] = ALPHA * dz + dh * (1.0 + scale)
        sums_ref[0:1, :] += _colsum(dh)
        sums_ref[1:2, :] += _colsum(dh * xv)

    return _call(
        body, name="mlp_bwd", grid=(s // tm,),
        out_shape=[jax.ShapeDtypeStruct((s, D_MODEL), F32), jax.ShapeDtypeStruct((s, D_FF), BF16),
                   jax.ShapeDtypeStruct((s, D_MODEL), BF16),
                   jax.ShapeDtypeStruct((s, D_MODEL), BF16), jax.ShapeDtypeStruct((8, D_MODEL), F32)],
        in_specs=[_row(tm, D_MODEL)] * 3 + [_row(tm, D_FF), _res(mods.shape), _res(w1_t.shape), _res(w2.shape),
                                             _res(lng.shape)] + ([_res(lnb.shape)] if head else []),
        out_specs=[_row(tm, D_MODEL), _row(tm, D_FF), _row(tm, D_MODEL), _row(tm, D_MODEL), _res((8, D_MODEL))],
        args=(dxo, x, y, ra, mods, w1_t, w2, lng) + ((lnb,) if head else ()), rider=rider)


def post_bwd(dxo, x, y, mods, k, w, lng, gate_act=None, rider=None):
    s = x.shape[0]
    tm = _blk(s, TM_MM)
    kdim = w.shape[0]
    rnn = gate_act is not None

    def body(*refs):
        if rnn:
            (d_ref, x_ref, y_ref, mod_ref, w_ref, g_ref, gt_ref, hs_ref,
             dres_ref, dy_ref, sums_ref, dhs_ref, dgt_ref) = refs
        else:
            d_ref, x_ref, y_ref, mod_ref, w_ref, g_ref, dres_ref, dy_ref, sums_ref, dyp_ref = refs

        @pl.when(pl.program_id(0) == 0)
        def _():
            sums_ref[...] = jnp.zeros(sums_ref.shape, F32)

        _, _, gate = _mod(mod_ref, k)
        dz = _ln_part_bwd(d_ref[...], x_ref[...], y_ref[...], gate, g_ref[...], sums_ref)
        dres_ref[...] = ALPHA * dz
        dyb = (dz * (1.0 + gate)).astype(BF16)
        dy_ref[...] = dyb
        dyp = _nt(dyb, w_ref[...])
        if rnn:
            act, dact = _gelu_parts(gt_ref[...].astype(F32))
            dhs = dyp * act
            for l in range(kdim // 128):
                dhs_ref[l] = dhs[:, 128 * l:128 * (l + 1)]
            dgt_ref[...] = (dyp * hs_ref[...].astype(F32) * dact).astype(BF16)
        else:
            dyp_ref[...] = dyp.astype(BF16)

    ins = [dxo, x, y, mods, w, lng] + (list(gate_act) if rnn else [])
    in_specs = [_row(tm, D_MODEL)] * 3 + [_res(mods.shape), _res(w.shape), _res(lng.shape)]
    out_shape = [jax.ShapeDtypeStruct((s, D_MODEL), F32), jax.ShapeDtypeStruct((s, D_MODEL), BF16),
                 jax.ShapeDtypeStruct((8, D_MODEL), F32)]
    out_specs = [_row(tm, D_MODEL), _row(tm, D_MODEL), _res((8, D_MODEL))]
    if rnn:
        in_specs += [_row(tm, kdim)] * 2
        out_shape += [jax.ShapeDtypeStruct((kdim // 128, s, 128), F32), jax.ShapeDtypeStruct((s, kdim), BF16)]
        out_specs += [pl.BlockSpec((kdim // 128, tm, 128), lambda i: (0, i, 0)), _row(tm, kdim)]
    else:
        out_shape.append(jax.ShapeDtypeStruct((s, kdim), BF16))
        out_specs.append(_row(tm, kdim))
    return _call(
        body, name="rnn_post_bwd" if rnn else "attn_post_bwd", grid=(s // tm,),
        out_shape=out_shape, in_specs=in_specs, out_specs=out_specs, args=ins, rider=rider)


def attn_bwd(q, kk, v, do, sinks, rider=None):
    s = q.shape[0]
    nblk = s // QBLK
    scale = HEAD ** -0.5

    def body(sink_ref, q_ref, do_ref, k0, k1, k2, k3, v0, v1, v2, v3, dq_ref, dk_ref, dv_ref, ds_ref):
        n = pl.program_id(0)

        @pl.when(n == 0)
        def _():
            ds_ref[...] = jnp.zeros(ds_ref.shape, F32)
            dk_ref[...] = jnp.zeros(dk_ref.shape, F32)
            dv_ref[...] = jnp.zeros(dv_ref.shape, F32)

        kall = jnp.concatenate([k0[...], k1[...], k2[...], k3[...]], axis=0)
        vall = jnp.concatenate([v0[...], v1[...], v2[...], v3[...]], axis=0)
        lane = lax.broadcasted_iota(jnp.int32, (1, 128), 1)
        dsink = jnp.zeros((1, 128), F32)
        for qb in range(QPAIR):
            nb = QPAIR * n + qb
            valid = _attn_mask(nb, s)
            keys = slice(QBLK * qb, QBLK * (qb + 3))
            for kv in range(N_KV):
                cols = slice(HEAD * kv, HEAD * (kv + 1))
                qg, dog = _stack_heads(q_ref, qb, kv), _stack_heads(do_ref, qb, kv)
                kh, vh = kall[keys, cols], vall[keys, cols]
                probs, psink = _attn_probs(qg, kh, valid, _stack_sinks(sink_ref, kv))
                dprobs = _nt(dog, vh)
                dvp = _tn(probs.astype(BF16), dog)
                rowdot = jnp.sum(probs * dprobs, axis=-1, keepdims=True)
                dsb = (probs * (dprobs - rowdot) * scale).astype(BF16)
                dqg = _nn(dsb, kh)
                dkp = _tn(dsb, qg)
                for p in range(3):
                    blk = jnp.clip(nb - 1 + p, 0, nblk - 1)
                    rows = pl.ds(pl.multiple_of(blk * QBLK, QBLK), QBLK)
                    dk_ref[rows, cols] += dkp[QBLK * p:QBLK * (p + 1), :]
                    dv_ref[rows, cols] += dvp[QBLK * p:QBLK * (p + 1), :]
                dsk = -psink * rowdot
                for j in range(GROUP):
                    hq = GROUP * kv + j
                    dq_ref[QBLK * qb:QBLK * (qb + 1), HEAD * hq:HEAD * (hq + 1)] = dqg[QBLK * j:QBLK * (j + 1), :]
                    dsink = dsink + jnp.where(lane == hq, _colsum(dsk[QBLK * j:QBLK * (j + 1), :]), 0.0)
        ds_ref[...] += dsink

    qspec = pl.BlockSpec((QPAIR * QBLK, N_Q * HEAD), lambda n: (n, 0))
    return _call(
        body, name="attn_bwd", grid=(nblk // QPAIR,),
        out_shape=[jax.ShapeDtypeStruct((s, N_Q * HEAD), F32),
                   jax.ShapeDtypeStruct((s, N_KV * HEAD), F32), jax.ShapeDtypeStruct((s, N_KV * HEAD), F32),
                   jax.ShapeDtypeStruct((1, 128), F32)],
        in_specs=[pl.BlockSpec(memory_space=pltpu.SMEM), qspec, qspec] + _kv_specs(nblk) + _kv_specs(nblk),
        out_specs=[qspec, _res((s, N_KV * HEAD)), _res((s, N_KV * HEAD)), pl.BlockSpec((1, 128), lambda n: (0, 0))],
        args=(sinks, q, do, kk, kk, kk, kk, v, v, v, v), rider=rider)


def _in_bwd_tail(dzb, w_ref, x_ref, mod_ref, k, dres_ref, dx_ref, h_ref, sums_ref):
    xv = x_ref[...]
    shift, scale, _ = _mod(mod_ref, k)
    h_ref[...] = (xv * (1.0 + scale) + shift).astype(BF16)
    dh = _nn(dzb, w_ref[...])
    dx_ref[...] = dres_ref[...] + dh * (1.0 + scale)
    sums_ref[0:1, :] += _colsum(dh)
    sums_ref[1:2, :] += _colsum(dh * xv)


def attn_in_bwd(dq, dk, dv, rope, x, mods, k, win_t, dres):
    s = x.shape[0]
    tm = _blk(s, TM_MM)

    def body(dq_ref, dk_ref, dv_ref, c_ref, s1_ref, s2_ref, x_ref, mod_ref, w_ref, dres_ref,
             dx_ref, dz_ref, h_ref, sums_ref):
        @pl.when(pl.program_id(0) == 0)
        def _():
            sums_ref[...] = jnp.zeros(sums_ref.shape, F32)

        cos, s1, s2 = c_ref[...], s1_ref[...], s2_ref[...]
        for hh in range(N_Q + N_KV):
            src = dq_ref[:, HEAD * hh:HEAD * (hh + 1)] if hh < N_Q else dk_ref[:, HEAD * (hh - N_Q):HEAD * (hh - N_Q + 1)]
            dz_ref[:, HEAD * hh:HEAD * (hh + 1)] = _rope_bwd(src, cos, s1, s2).astype(BF16)
        dz_ref[:, HEAD * (N_Q + N_KV):] = dv_ref[...].astype(BF16)
        _in_bwd_tail(dz_ref[...], w_ref, x_ref, mod_ref, k, dres_ref, dx_ref, h_ref, sums_ref)

    return pl.pallas_call(
        body, name="attn_in_bwd", grid=(s // tm,),
        out_shape=[jax.ShapeDtypeStruct((s, D_MODEL), F32), jax.ShapeDtypeStruct((s, D_QKV), BF16),
                   jax.ShapeDtypeStruct((s, D_MODEL), BF16), jax.ShapeDtypeStruct((8, D_MODEL), F32)],
        in_specs=[_row(tm, N_Q * HEAD), _row(tm, N_KV * HEAD), _row(tm, N_KV * HEAD),
                  _row(tm, HEAD), _row(tm, HEAD), _row(tm, HEAD), _row(tm, D_MODEL),
                  _res(mods.shape), _res(win_t.shape), _row(tm, D_MODEL)],
        out_specs=[_row(tm, D_MODEL), _row(tm, D_QKV), _row(tm, D_MODEL), _res((8, D_MODEL))],
        compiler_params=_params(),
    )(dq, dk, dv, *rope, x, mods, win_t, dres)


def _shift_blk(v, k, before, after, row):
    n = v.shape[0]
    r = pltpu.roll(v, k % n, 0)
    for j in range(abs(k)):
        if k > 0:
            r = jnp.where(row == j, before[8 - k + j:8 - k + j + 1, :], r)
        else:
            r = jnp.where(row == n + k + j, after[j:j + 1, :], r)
    return r


def rnn_in_bwd(dxc_f, dxc_b, xr, cw, dgt, x, mods, k, win_t, dres):
    s = x.shape[0]
    tm = _blk(s, TM_MM)
    n = s // tm

    def body(f_ref, fp_ref, fn_ref, b_ref, bp_ref, bn_ref, xr_ref, xp_ref, xn_ref, cw_ref, dgt_ref,
             x_ref, mod_ref, w_ref, dres_ref, dx_ref, dz_ref, h_ref, sums_ref, dcw_ref, dcb_ref):
        i = pl.program_id(0)

        @pl.when(i == 0)
        def _():
            sums_ref[...] = jnp.zeros(sums_ref.shape, F32)
            dcw_ref[...] = jnp.zeros(dcw_ref.shape, F32)
            dcb_ref[...] = jnp.zeros(dcb_ref.shape, F32)

        d = _slab_rows(f_ref) + _slab_rows(b_ref)
        xv = xr_ref[...]
        first, last = i == 0, i == n - 1
        d_before = jnp.where(first, 0.0, _slab_rows(fp_ref) + _slab_rows(bp_ref))
        d_after = jnp.where(last, 0.0, _slab_rows(fn_ref) + _slab_rows(bn_ref))
        x_before = jnp.where(first, 0.0, xp_ref[...])
        x_after = jnp.where(last, 0.0, xn_ref[...])
        row = lax.broadcasted_iota(jnp.int32, d.shape, 0)
        dxr = (cw_ref[0:1, :] * _shift_blk(d, -2, d_before, d_after, row)
               + cw_ref[1:2, :] * _shift_blk(d, -1, d_before, d_after, row)
               + cw_ref[2:3, :] * d + cw_ref[3:4, :] * _shift_blk(d, 1, d_before, d_after, row))
        dcw_ref[0:1, :] += _colsum(d * _shift_blk(xv, 2, x_before, x_after, row))
        dcw_ref[1:2, :] += _colsum(d * _shift_blk(xv, 1, x_before, x_after, row))
        dcw_ref[2:3, :] += _colsum(d * xv)
        dcw_ref[3:4, :] += _colsum(d * _shift_blk(xv, -1, x_before, x_after, row))
        dcb_ref[...] += _colsum(d)
        dz_ref[:, 0:D_RNN] = dxr.astype(BF16)
        dz_ref[:, D_RNN:2 * D_RNN] = dgt_ref[...]
        _in_bwd_tail(dz_ref[...], w_ref, x_ref, mod_ref, k, dres_ref, dx_ref, h_ref, sums_ref)

    per8 = tm // 8
    ns = D_RNN // 128
    blk = _row(tm, D_RNN)
    before = pl.BlockSpec((8, D_RNN), lambda i: (jnp.maximum(i * per8 - 1, 0), 0))
    after = pl.BlockSpec((8, D_RNN), lambda i: (jnp.minimum((i + 1) * per8, s // 8 - 1), 0))
    sblk = pl.BlockSpec((ns, tm, 128), lambda i: (0, i, 0))
    sbefore = pl.BlockSpec((ns, 8, 128), lambda i: (0, jnp.maximum(i * per8 - 1, 0), 0))
    safter = pl.BlockSpec((ns, 8, 128), lambda i: (0, jnp.minimum((i + 1) * per8, s // 8 - 1), 0))
    return pl.pallas_call(
        body, name="rnn_in_bwd", grid=(n,),
        out_shape=[jax.ShapeDtypeStruct((s, D_MODEL), F32), jax.ShapeDtypeStruct((s, 2 * D_RNN), BF16),
                   jax.ShapeDtypeStruct((s, D_MODEL), BF16), jax.ShapeDtypeStruct((8, D_MODEL), F32),
                   jax.ShapeDtypeStruct((4, D_RNN), F32), jax.ShapeDtypeStruct((1, D_RNN), F32)],
        in_specs=[sblk, sbefore, safter] * 2 + [blk, before, after] + [
            _res(cw.shape), blk, _row(tm, D_MODEL), _res(mods.shape), _res(win_t.shape), _row(tm, D_MODEL)],
        out_specs=[_row(tm, D_MODEL), _row(tm, 2 * D_RNN), _row(tm, D_MODEL), _res((8, D_MODEL)),
                   _res((4, D_RNN)), _res((1, D_RNN))],
        compiler_params=_params(),
    )(dxc_f, dxc_f, dxc_f, dxc_b, dxc_b, dxc_b, xr, xr, xr, cw, dgt, x, mods, win_t, dres)


def wgrad(a, b, name, rider=None):
    s, m = a.shape
    n = b.shape[1]
    tm = next(t for t in (1024, 768, 512, 384, 256, 128) if m % t == 0)
    tk = _blk(s, TK_WG)
    nk = s // tk

    def body(a_ref, b_ref, o_ref, acc):
        kk = pl.program_id(1)

        @pl.when(kk == 0)
        def _():
            acc[...] = jnp.zeros(acc.shape, F32)

        acc[...] += _tn(a_ref[...], b_ref[...])

        @pl.when(kk == nk - 1)
        def _():
            o_ref[...] = acc[...].astype(BF16)

    out, *rode = _call(
        body, name=name, grid=(m // tm, nk),
        out_shape=[jax.ShapeDtypeStruct((m, n), BF16)],
        in_specs=[pl.BlockSpec((tk, tm), lambda i, kk: (kk, i)), pl.BlockSpec((tk, n), lambda i, kk: (kk, 0))],
        out_specs=[pl.BlockSpec((tm, n), lambda i, kk: (i, 0))],
        scratch_shapes=[pltpu.VMEM((tm, n), F32)],
        args=(a, b), rider=rider)
    out = out.reshape(N_DEV, m // N_DEV, n)
    return (out, *rode) if rider is not None else out


def part_sum(parts, name):
    _, r, c = parts.shape
    tr = next(t for t in (256, 192, 128, 64, 32, 16, 8) if r % t == 0)

    def body(p_ref, o_ref):
        acc = p_ref[0].astype(F32)
        for j in range(1, N_DEV):
            acc = acc + p_ref[j].astype(F32)
        o_ref[...] = acc

    return pl.pallas_call(
        body, name=name, grid=(r // tr,),
        out_shape=jax.ShapeDtypeStruct((r, c), F32),
        in_specs=[pl.BlockSpec((N_DEV, tr, c), lambda i: (0, i, 0))],
        out_specs=pl.BlockSpec((tr, c), lambda i: (i, 0)),
        compiler_params=_params(),
    )(parts)


def adamw(w, g, m, v, name):
    shape = w.shape
    c = shape[-1]
    r = w.size // c
    w2, g2, m2, v2 = (t.reshape(r, c) for t in (w, g, m, v))
    tr = r if r * c <= 512 * 1024 else next(t for t in (512, 256, 128, 64, 32, 16, 8) if r % t == 0)

    def body(w_ref, g_ref, m_ref, v_ref, d_ref, nm_ref, nv_ref):
        gv = g_ref[...]
        nm = B1 * m_ref[...] + (1.0 - B1) * gv
        nv = B2 * v_ref[...] + (1.0 - B2) * (gv * gv)
        nm_ref[...] = nm
        nv_ref[...] = nv
        m_hat = nm / (1.0 - B1 ** STEP)
        v_hat = nv / (1.0 - B2 ** STEP)
        d_ref[...] = -LR * (m_hat / (jnp.sqrt(v_hat) + ADAM_EPS) + WD * w_ref[...])

    spec = pl.BlockSpec((tr, c), lambda i: (i, 0))
    outs = pl.pallas_call(
        body, name=name, grid=(r // tr,),
        out_shape=[jax.ShapeDtypeStruct((r, c), F32)] * 3,
        in_specs=[spec] * 4, out_specs=[spec] * 3,
        compiler_params=_params(),
    )(w2, g2, m2, v2)
    return tuple(o.reshape(shape) for o in outs)


def _rope_tables(s):
    half = ROT // 2
    inv_freq = THETA ** (-jnp.arange(0, ROT, 2, dtype=F32) / ROT)
    per_row = 128 // half
    pos = (per_row * jnp.arange(s // per_row)[:, None] + jnp.arange(128)[None, :] // half).astype(F32)
    ang = pos * jnp.tile(inv_freq, per_row)[None, :]
    cos, sin = lax.optimization_barrier((jnp.cos(ang), jnp.sin(ang)))
    cos, sin = cos.reshape(s, half), sin.reshape(s, half)
    zeros = jnp.zeros((s, HEAD - ROT), F32)
    c = jnp.concatenate([cos, cos, jnp.ones((s, HEAD - ROT), F32)], axis=1)
    s1 = jnp.concatenate([jnp.zeros((s, half), F32), sin, zeros], axis=1)
    s2 = jnp.concatenate([-sin, jnp.zeros((s, half), F32), zeros], axis=1)
    return c, s1, s2


def _blockdiag(w):
    w4 = w.reshape(N_CG, 4, RB_W, RB_W)
    eye = jnp.eye(4, dtype=w.dtype)
    return jnp.einsum("gipq,ij->gipjq", w4, eye).reshape(N_CG, CG, CG)


def _diag_blocks(w):
    w5 = w.reshape(N_CG, 4, RB_W, 4, RB_W)
    eye = jnp.eye(4, dtype=w.dtype)
    return jnp.einsum("gipjq,ij->gipq", w5, eye).reshape(N_RB, RB_W, RB_W)


def _cols(full, per):
    lead = full.shape[:-1]
    t = full.reshape(lead + (N_DEV, per))
    return jnp.moveaxis(t, -2, 0).reshape(N_DEV, -1)


def kernel(x, c, ada_w, ada_b, ln_g, ln_b, attn_w_in, attn_w_out, attn_sinks, rnn_w_in, rnn_conv_w, rnn_conv_b, rnn_w_a, rnn_b_a, rnn_w_x, rnn_b_x, rnn_lam, rnn_w_out, mlp_w1, mlp_w2, loss_target, m_ada_w, m_ada_b, m_ln_g, m_ln_b, m_attn_w_in, m_attn_w_out, m_attn_sinks, m_rnn_w_in, m_rnn_conv_w, m_rnn_conv_b, m_rnn_w_a, m_rnn_b_a, m_rnn_w_x, m_rnn_b_x, m_rnn_lam, m_rnn_w_out, m_mlp_w1, m_mlp_w2, v_ada_w, v_ada_b, v_ln_g, v_ln_b, v_attn_w_in, v_attn_w_out, v_attn_sinks, v_rnn_w_in, v_rnn_conv_w, v_rnn_conv_b, v_rnn_w_a, v_rnn_b_a, v_rnn_w_x, v_rnn_b_x, v_rnn_lam, v_rnn_w_out, v_mlp_w1, v_mlp_w2):
    s = x.shape[1]
    x0 = x.reshape(s, D_MODEL)
    target = loss_target.reshape(s, D_MODEL)
    weights = dict(ada_w=ada_w, ada_b=ada_b, ln_g=ln_g, ln_b=ln_b, attn_w_in=attn_w_in, attn_w_out=attn_w_out,
                   attn_sinks=attn_sinks, rnn_w_in=rnn_w_in, rnn_conv_w=rnn_conv_w, rnn_conv_b=rnn_conv_b,
                   rnn_w_a=rnn_w_a, rnn_b_a=rnn_b_a, rnn_w_x=rnn_w_x, rnn_b_x=rnn_b_x, rnn_lam=rnn_lam,
                   rnn_w_out=rnn_w_out, mlp_w1=mlp_w1, mlp_w2=mlp_w2)
    moments_m = dict(ada_w=m_ada_w, ada_b=m_ada_b, ln_g=m_ln_g, ln_b=m_ln_b, attn_w_in=m_attn_w_in,
                     attn_w_out=m_attn_w_out, attn_sinks=m_attn_sinks, rnn_w_in=m_rnn_w_in,
                     rnn_conv_w=m_rnn_conv_w, rnn_conv_b=m_rnn_conv_b, rnn_w_a=m_rnn_w_a, rnn_b_a=m_rnn_b_a,
                     rnn_w_x=m_rnn_w_x, rnn_b_x=m_rnn_b_x, rnn_lam=m_rnn_lam, rnn_w_out=m_rnn_w_out,
                     mlp_w1=m_mlp_w1, mlp_w2=m_mlp_w2)
    moments_v = dict(ada_w=v_ada_w, ada_b=v_ada_b, ln_g=v_ln_g, ln_b=v_ln_b, attn_w_in=v_attn_w_in,
                     attn_w_out=v_attn_w_out, attn_sinks=v_attn_sinks, rnn_w_in=v_rnn_w_in,
                     rnn_conv_w=v_rnn_conv_w, rnn_conv_b=v_rnn_conv_b, rnn_w_a=v_rnn_w_a, rnn_b_a=v_rnn_b_a,
                     rnn_w_x=v_rnn_w_x, rnn_b_x=v_rnn_b_x, rnn_lam=v_rnn_lam, rnn_w_out=v_rnn_w_out,
                     mlp_w1=v_mlp_w1, mlp_w2=v_mlp_w2)
    names = list(weights)

    def t16(w):
        return w.T.astype(BF16)

    big = [t16(attn_w_in[0]), attn_w_out[0].astype(BF16), t16(rnn_w_in[0]), rnn_w_out[0].astype(BF16),
           t16(mlp_w1[0]), mlp_w2[0].astype(BF16), t16(mlp_w1[1]), mlp_w2[1].astype(BF16)]
    small_local = jnp.concatenate([
        ln_g.reshape(-1), ln_b.reshape(-1), rnn_conv_w.reshape(-1), rnn_conv_b.reshape(-1),
        rnn_b_a.reshape(-1), rnn_b_x.reshape(-1), rnn_lam.reshape(-1)])
    small_local = jnp.pad(small_local, (0, 4096 - small_local.shape[0])).reshape(32, 128)
    flat = lambda g: g.reshape(N_DEV * g.shape[1], D_MODEL)
    c_all, modr, win_t, sm = ada_modulation(jnp.broadcast_to(c, (8, D_MODEL)), ada_w.reshape(4, D_MODEL, CG),
                                            ada_b.reshape(4, 1, CG), _Gather([big[0], small_local]))
    win_t = flat(win_t)
    sm = sm.reshape(N_DEV, 4096)

    def full_vec(off, rows, per):
        piece = sm[:, off:off + rows * per].reshape(N_DEV, rows, per)
        return jnp.moveaxis(piece, 0, 1).reshape(rows, N_DEV * per)

    lng_f, lnb_f = full_vec(0, 4, 128), full_vec(512, 4, 128)
    cw_f, cb_f = full_vec(1024, 4, 192), full_vec(1792, 1, 192)
    ba_f, bx_f, lam_f = full_vec(1984, 2, 192), full_vec(2368, 2, 192), full_vec(2752, 2, 192)
    wa_bd = [_blockdiag(rnn_w_a[0, d]).astype(BF16) for d in range(2)]
    wx_bd = [_blockdiag(rnn_w_x[0, d]).astype(BF16) for d in range(2)]

    mods = modr.reshape(N_DEV, 4, 8, CG)[:, :, 0, :]
    mods = jnp.moveaxis(mods, 0, 1).reshape(4, 3, D_MODEL).reshape(12, D_MODEL)
    rope = _rope_tables(s)
    ln = lambda k: (lng_f[k:k + 1], lnb_f[k:k + 1])

    q, kk, v, wout = attn_in_fwd(x0, mods, 0, win_t, rope, rider=_Gather([big[1]]))
    wout = flat(wout)
    o, *got = attn_fwd(q, kk, v, attn_sinks, rider=_Gather([big[4], big[5]]))
    w1t_0, w2_0 = (flat(g) for g in got)
    x1, y0, rout = post_fwd(o, wout, x0, mods, 0, *ln(0), rider=_Gather([big[3]]))
    rout = flat(rout)
    x2, y1, ra0, r0, *got = mlp_fwd(x1, mods, 1, w1t_0, w2_0, *ln(1), rider=_Gather([big[2], big[6], big[7]]))
    rin_t, w1t_1, w2_1 = (flat(g) for g in got)
    xr, gt = rnn_in_fwd(x2, mods, 2, rin_t)
    xc = conv_fwd(xr, cw_f, cb_f)
    hf, = lru_fwd(xc, wa_bd[0], wx_bd[0], ba_f[0:1], bx_f[0:1], lam_f[0:1], False)
    hb, hsum = lru_fwd(xc, wa_bd[1], wx_bd[1], ba_f[1:2], bx_f[1:2], lam_f[1:2], True, other=hf)
    x3, y2, ypre = post_fwd(None, rout, x2, mods, 2, *ln(2), gate_act=(gt, hsum))
    y3, ra1, r1 = mlp_fwd(x3, mods, 3, w1t_1, w2_1, *ln(3), last=True)

    dx3, da1, h3, dy3, sums3 = mlp_bwd(target, x3, y3, ra1, mods, 3, w1t_1, w2_1, lng_f[3:4], lnb=lnb_f[3:4])
    g_w1t_1 = wgrad(da1, h3, "wgrad_w1_1")
    g_w2_1 = wgrad(r1, dy3, "wgrad_w2_1")
    dres2, dy2, sums2a, dhs, dgt, p_w1t_1 = post_bwd(dx3, x2, y2, mods, 2, rout, lng_f[2:3], gate_act=(gt, hsum),
                                                     rider=_AllToAll([g_w1t_1]))
    g_rout = wgrad(ypre, dy2, "wgrad_rnn_out")
    dxc_f, dwa_f, dwx_f, dba_f, dbx_f, dlam_f, p_w2_1, p_rout = lru_bwd(
        xc, dhs, hf, wa_bd[0], wx_bd[0], ba_f[0:1], bx_f[0:1], lam_f[0:1], False, rider=_AllToAll([g_w2_1, g_rout]))
    dxc_b, dwa_b, dwx_b, dba_b, dbx_b, dlam_b = lru_bwd(xc, dhs, hb, wa_bd[1], wx_bd[1], ba_f[1:2], bx_f[1:2],
                                                        lam_f[1:2], True)
    dx2, dzz, h2, sums2b, dcw, dcb = rnn_in_bwd(dxc_f, dxc_b, xr, cw_f, dgt, x2, mods, 2, rin_t, dres2)
    g_rin_t = wgrad(dzz, h2, "wgrad_rnn_in")
    d_wa = jnp.stack([_diag_blocks(dwa_f), _diag_blocks(dwa_b)])
    d_wx = jnp.stack([_diag_blocks(dwx_f), _diag_blocks(dwx_b)])
    nflat = d_wa.size // N_DEV
    gates = jnp.concatenate([d_wa.reshape(N_DEV, nflat), d_wx.reshape(N_DEV, nflat)], axis=1)
    gates = gates.reshape(N_DEV, 2 * nflat // 128, 128)
    dx1, da0, h1, dy1, sums1, p_rin_t, p_gates = mlp_bwd(dx2, x1, y1, ra0, mods, 1, w1t_0, w2_0, lng_f[1:2],
                                                         rider=_AllToAll([g_rin_t, gates]))
    gates_sum = part_sum(p_gates, "part_sum_gates")
    g_w1t_0 = wgrad(da0, h1, "wgrad_w1_0")
    g_w2_0, p_w1t_0 = wgrad(r0, dy1, "wgrad_w2_0", rider=_AllToAll([g_w1t_0]))
    dres0, dy0, sums0a, do = post_bwd(dx1, x0, y0, mods, 0, wout, lng_f[0:1])
    g_wout = wgrad(o, dy0, "wgrad_attn_out")
    dq, dk, dv, dsink, wag, p_w2_0, p_wout = attn_bwd(
        q, kk, v, do, attn_sinks, rider=_Multi(_Gather([gates_sum]), _AllToAll([g_w2_0, g_wout])))
    dx0, dqkv, h0, sums0b = attn_in_bwd(dq, dk, dv, rope, x0, mods, 0, win_t, dres0)
    g_win_t = wgrad(dqkv, h0, "wgrad_attn_in")

    sums = [sums0a + sums0b, sums1, sums2a + sums2b, sums3]
    gmod = jnp.stack([t[0:3] for t in sums])
    gsend = jnp.moveaxis(gmod.reshape(4, N_DEV, CG), 1, 0)
    gsend = jnp.pad(gsend, ((0, 0), (0, 4), (0, 0)))
    c_t = c_all[:, 0, :].T
    sq_err = jnp.sum(sums3[5]).reshape(1, 1)
    tail = jnp.concatenate([
        _cols(dcw, 192), _cols(dcb, 192),
        _cols(jnp.concatenate([dba_f, dba_b]), 192), _cols(jnp.concatenate([dbx_f, dbx_b]), 192),
        _cols(jnp.concatenate([dlam_f, dlam_b]), 192),
        _cols(jnp.stack([t[3] for t in sums]), 128), _cols(jnp.stack([t[4] for t in sums]), 128),
        jnp.broadcast_to(dsink[:, 0:8], (N_DEV, 8)), jnp.broadcast_to(sq_err, (N_DEV, 1))], axis=1)
    tail = jnp.pad(tail, ((0, 0), (0, 32 * 128 - tail.shape[1]))).reshape(N_DEV, 32, 128)
    g_ada_w, g_ada_b, red, p_win_t = epilogue(gsend, c_t, tail, _AllToAll([g_win_t]))
    grads = {"ada_w": g_ada_w.reshape(ada_w.shape), "ada_b": g_ada_b[0:4].reshape(ada_b.shape)}

    big_parts = [p_win_t, p_wout, p_rin_t, p_rout, p_w1t_0, p_w2_0, p_w1t_1, p_w2_1]
    gsum = [part_sum(p, "part_sum_%d" % i) for i, p in enumerate(big_parts)]
    grads.update({
        "attn_w_in": gsum[0].T[None], "attn_w_out": gsum[1][None],
        "rnn_w_in": gsum[2].T[None], "rnn_w_out": gsum[3][None],
        "mlp_w1": jnp.stack([gsum[4].T, gsum[6].T]), "mlp_w2": jnp.stack([gsum[5], gsum[7]]),
    })
    wag = wag.reshape(N_DEV, 2 * nflat)
    grads["rnn_w_a"] = wag[:, :nflat].reshape(rnn_w_a.shape)
    grads["rnn_w_x"] = wag[:, nflat:].reshape(rnn_w_x.shape)
    tl = red.reshape(-1)
    loss = 0.5 * tl[3144] / D_MODEL
    grads["rnn_conv_w"] = tl[0:768].reshape(rnn_conv_w.shape)
    grads["rnn_conv_b"] = tl[768:960].reshape(rnn_conv_b.shape)
    grads["rnn_b_a"] = tl[960:1344].reshape(rnn_b_a.shape)
    grads["rnn_b_x"] = tl[1344:1728].reshape(rnn_b_x.shape)
    grads["rnn_lam"] = tl[1728:2112].reshape(rnn_lam.shape)
    grads["ln_g"] = tl[2112:2624].reshape(ln_g.shape)
    grads["ln_b"] = tl[2624:3136].reshape(ln_b.shape)
    grads["attn_sinks"] = tl[3136:3144].reshape(attn_sinks.shape)

    delta, new_m, new_v = {}, {}, {}
    for n in names:
        delta[n], new_m[n], new_v[n] = adamw(weights[n], grads[n], moments_m[n], moments_v[n], "adamw_" + n)
    return (loss, dx0.reshape(x.shape), *[grads[n] for n in names], *[delta[n] for n in names],
            *[new_m[n] for n in names], *[new_v[n] for n in names])
```

```python
import functools
import math

import jax
import jax.numpy as jnp
from jax import lax
from jax.experimental import pallas as pl
from jax.experimental.pallas import tpu as pltpu

F32, BF16 = jnp.float32, jnp.bfloat16
MESH = pl.DeviceIdType.MESH

D_MODEL = 1024
N_Q, N_KV, HEAD = 8, 2, 128
ROT, THETA = 32, 500000.0
QBLK = 128
D_QKV = (N_Q + 2 * N_KV) * HEAD
D_RNN, N_RB, RB_W = 1536, 16, 96
CG = 384
N_CG = D_RNN // CG
D_FF = 4096
FF_CHUNK = 1024
DEPTH = 2
ALPHA = (2.0 * DEPTH) ** 0.25
LN_EPS = 1e-5
LRU_C = 8.0
N_DEV = 8
LR, B1, B2, ADAM_EPS, WD, STEP = 0.001, 0.9, 0.999, 1e-8, 0.01, 10

VMEM_LIMIT = 56 * 1024 * 1024
TM_MM = 512
TM_MLP = 256
TM_MLP_FWD = 512
TT_RNN = 2048
TK_WG = 2048


def _nn(a, b):
    return jnp.dot(a, b, preferred_element_type=F32)


def _nt(a, b):
    return lax.dot_general(a, b, (((1,), (1,)), ((), ())), preferred_element_type=F32)


def _tn(a, b):
    return lax.dot_general(a, b, (((0,), (0,)), ((), ())), preferred_element_type=F32)


def _blk(n, pref):
    t = min(n, pref)
    assert n % t == 0, (n, pref)
    return t


def _params(**kw):
    return pltpu.CompilerParams(vmem_limit_bytes=VMEM_LIMIT, **kw)


def _row(tm, w):
    return pl.BlockSpec((tm, w), lambda i: (i, 0))


def _res(shape):
    return pl.BlockSpec(shape, lambda i: (0,) * len(shape), pipeline_mode=pl.Buffered(1))


def _mod(mod_ref, k):
    return mod_ref[3 * k:3 * k + 1, :], mod_ref[3 * k + 1:3 * k + 2, :], mod_ref[3 * k + 2:3 * k + 3, :]


def _ln_stats(z):
    mu = jnp.mean(z, axis=-1, keepdims=True)
    zc = z - mu
    var = jnp.mean(zc * zc, axis=-1, keepdims=True)
    rstd = lax.rsqrt(var + LN_EPS)
    return zc * rstd, rstd


def _ln_bwd(dxo, xhat, rstd, g):
    dxh = dxo * g
    m1 = jnp.mean(dxh, axis=-1, keepdims=True)
    m2 = jnp.mean(dxh * xhat, axis=-1, keepdims=True)
    return rstd * (dxh - m1 - xhat * m2)


def _colsum(v):
    return jnp.sum(v, axis=0, keepdims=True)


def _sigmoid(v):
    return 0.5 * jnp.tanh(0.5 * v) + 0.5


def _gelu_parts(v):
    k = math.sqrt(2.0 / math.pi)
    u = k * (v + 0.044715 * v * v * v)
    t = jnp.tanh(u)
    g = 0.5 * v * (1.0 + t)
    dg = 0.5 * (1.0 + t) + 0.5 * v * (1.0 - t * t) * k * (1.0 + 3.0 * 0.044715 * v * v)
    return g, dg


def _me():
    return lax.axis_index("x"), lax.axis_index("y"), lax.axis_index("c")


def _idx(p):
    return 4 * p[0] + 2 * p[1] + p[2]


def _peers(me):
    x, y, c = me
    out = []
    for k in range(1, N_DEV):
        out.append((1 - x if k & 4 else x, 1 - y if k & 2 else y, 1 - c if k & 1 else c))
    return out


class _Gather:
    def __init__(self, srcs):
        self.srcs = list(srcs)
        n = len(self.srcs)
        self.out_shape = [jax.ShapeDtypeStruct((N_DEV,) + s.shape, s.dtype) for s in self.srcs]
        self.scratch = [pltpu.SemaphoreType.DMA((n, 7)), pltpu.SemaphoreType.DMA((n, 7)),
                        pltpu.SemaphoreType.DMA((n,))]

    @staticmethod
    def _places():
        x, y, c = me = _me()
        return me, (x, y, 1 - c), [(1 - x, y), (x, 1 - y), (1 - x, 1 - y)]

    @staticmethod
    def _copy(outs, sems, t, k, block, to, src=None):
        slot = outs[t].at[_idx(block)]
        return pltpu.make_async_remote_copy(
            src_ref=slot if src is None else src, dst_ref=slot, send_sem=sems[0].at[t, k],
            recv_sem=sems[1].at[t, k], device_id=to, device_id_type=MESH)

    def _firsts(self, ins, outs, sems):
        me, sibling, chips = self._places()
        out = []
        for t in range(len(ins)):
            out.append(self._copy(outs, sems, t, 0, me, sibling, src=ins[t]))
            out += [self._copy(outs, sems, t, 1 + j, me, (*chip, me[2]), src=ins[t]) for j, chip in enumerate(chips)]
        return out

    def _locals(self, ins, outs, sems):
        me = _me()
        return [pltpu.make_async_copy(ins[t], outs[t].at[_idx(me)], sems[2].at[t]) for t in range(len(ins))]

    def start(self, ins, outs, sems):
        for cp in self._locals(ins, outs, sems) + self._firsts(ins, outs, sems):
            cp.start()

    def mid(self, ins, outs, sems):
        me, sibling, chips = self._places()
        for j, chip in enumerate(chips):
            for t in range(len(ins)):
                self._copy(outs, sems, t, 1 + j, (*chip, me[2]), me).wait_recv()
                self._copy(outs, sems, t, 4 + j, (*chip, me[2]), sibling).start()

    def finish(self, ins, outs, sems):
        me, sibling, chips = self._places()
        for t in range(len(ins)):
            self._copy(outs, sems, t, 0, sibling, me).wait_recv()
            for j, chip in enumerate(chips):
                self._copy(outs, sems, t, 4 + j, (*chip, 1 - me[2]), me).wait_recv()
        for cp in self._firsts(ins, outs, sems):
            cp.wait_send()
        for j, chip in enumerate(chips):
            for t in range(len(ins)):
                self._copy(outs, sems, t, 4 + j, (*chip, me[2]), sibling).wait_send()
        for cp in self._locals(ins, outs, sems):
            cp.wait()


class _AllToAll:
    def __init__(self, srcs):
        self.srcs = list(srcs)
        n = len(self.srcs)
        self.out_shape = [jax.ShapeDtypeStruct(s.shape, s.dtype) for s in self.srcs]
        self.scratch = [pltpu.SemaphoreType.DMA((n, 7)), pltpu.SemaphoreType.DMA((n, 7)),
                        pltpu.SemaphoreType.DMA((n,))]

    def _copies(self, ins, outs, sems):
        me = _me()
        loc, rem = [], []
        for t in range(len(ins)):
            loc.append(pltpu.make_async_copy(ins[t].at[_idx(me)], outs[t].at[_idx(me)], sems[2].at[t]))
            for k, p in enumerate(_peers(me)):
                rem.append(pltpu.make_async_remote_copy(
                    src_ref=ins[t].at[_idx(p)], dst_ref=outs[t].at[_idx(me)], send_sem=sems[0].at[t, k],
                    recv_sem=sems[1].at[t, k], device_id=p, device_id_type=MESH))
        return loc, rem

    def start(self, ins, outs, sems):
        loc, rem = self._copies(ins, outs, sems)
        for cp in loc + rem:
            cp.start()

    def mid(self, ins, outs, sems):
        pass

    def finish(self, ins, outs, sems):
        me = _me()
        for t in range(len(ins)):
            for k, p in enumerate(_peers(me)):
                slot = outs[t].at[_idx(p)]
                pltpu.make_async_remote_copy(
                    src_ref=slot, dst_ref=slot, send_sem=sems[0].at[t, k], recv_sem=sems[1].at[t, k],
                    device_id=p, device_id_type=MESH).wait_recv()
        loc, rem = self._copies(ins, outs, sems)
        for cp in rem:
            cp.wait_send()
        for cp in loc:
            cp.wait()


class _Multi:
    def __init__(self, *exs):
        self.exs = exs
        self.srcs = [s for e in exs for s in e.srcs]
        self.out_shape = [s for e in exs for s in e.out_shape]
        self.scratch = [s for e in exs for s in e.scratch]

    def _each(self, ins, outs, sems):
        i = j = 0
        for e in self.exs:
            n, m = len(e.srcs), len(e.scratch)
            yield e, ins[i:i + n], outs[i:i + n], sems[j:j + m]
            i, j = i + n, j + m

    def start(self, ins, outs, sems):
        for e, a, b, c in self._each(ins, outs, sems):
            e.start(a, b, c)

    def mid(self, ins, outs, sems):
        for e, a, b, c in self._each(ins, outs, sems):
            e.mid(a, b, c)

    def finish(self, ins, outs, sems):
        for e, a, b, c in self._each(ins, outs, sems):
            e.finish(a, b, c)


def _call(body, *, name, grid, in_specs, out_specs, out_shape, args, scratch_shapes=(), rider=None):
    in_specs, out_specs, out_shape = list(in_specs), list(out_specs), list(out_shape)
    scratch_shapes = list(scratch_shapes)
    if rider is None:
        return pl.pallas_call(body, name=name, grid=grid, out_shape=out_shape, in_specs=in_specs,
                              out_specs=out_specs, scratch_shapes=scratch_shapes, compiler_params=_params())(*args)
    nci, nco, ncs, nr = len(in_specs), len(out_shape), len(scratch_shapes), len(rider.srcs)
    nsteps = math.prod(grid)
    assert nsteps >= 2, (name, grid)
    mid = max(1, (7 * nsteps) // 8)

    def full(*refs):
        ci, ri = refs[:nci], refs[nci:nci + nr]
        co, ro = refs[nci + nr:nci + nr + nco], refs[nci + nr + nco:nci + 2 * nr + nco]
        cs, rs = refs[nci + 2 * nr + nco:nci + 2 * nr + nco + ncs], refs[nci + 2 * nr + nco + ncs:]
        step = pl.program_id(0)
        for d in range(1, len(grid)):
            step = step * grid[d] + pl.program_id(d)

        @pl.when(step == 0)
        def _():
            rider.start(ri, ro, rs)

        @pl.when(step == mid)
        def _():
            rider.mid(ri, ro, rs)

        body(*ci, *co, *cs)

        @pl.when(step == nsteps - 1)
        def _():
            rider.finish(ri, ro, rs)

    any_spec = pl.BlockSpec(memory_space=pl.ANY)
    return pl.pallas_call(
        full, name=name, grid=grid, out_shape=out_shape + rider.out_shape,
        in_specs=in_specs + [any_spec] * nr, out_specs=out_specs + [any_spec] * nr,
        scratch_shapes=scratch_shapes + rider.scratch, compiler_params=_params(),
    )(*args, *rider.srcs)


def _a2a_start(srcs, dsts, send_sems, recv_sems, local_sems, me, sem_base=0):
    peers = _peers(me)
    started = []
    for t in range(len(srcs)):
        loc = pltpu.make_async_copy(srcs[t].at[_idx(me)], dsts[t].at[_idx(me)], local_sems.at[sem_base + t])
        loc.start()
        started.append(("local", loc))
        for k, p in enumerate(peers):
            cp = pltpu.make_async_remote_copy(
                src_ref=srcs[t].at[_idx(p)], dst_ref=dsts[t].at[_idx(me)],
                send_sem=send_sems.at[sem_base + t, k], recv_sem=recv_sems.at[sem_base + t, k],
                device_id=p, device_id_type=MESH)
            cp.start()
            started.append(("remote", cp))
    return started


def _a2a_finish(started, dsts, send_sems, recv_sems, me, sem_base=0):
    peers = _peers(me)
    for t in range(len(dsts)):
        for k, p in enumerate(peers):
            slot = dsts[t].at[_idx(p)]
            pltpu.make_async_remote_copy(
                src_ref=slot, dst_ref=slot, send_sem=send_sems.at[sem_base + t, k],
                recv_sem=recv_sems.at[sem_base + t, k], device_id=p, device_id_type=MESH).wait_recv()
    for kind, cp in started:
        if kind == "local":
            cp.wait()
        else:
            cp.wait_send()


def ada_modulation(c8, ada_w, ada_b, ride):
    nr = len(ride.srcs)

    def body(c_ref, w_ref, b_ref, *rest):
        ride_in, (call_ref, modr_ref), ride_out = rest[:nr], rest[nr:nr + 2], rest[nr + 2:2 * nr + 2]
        modp, send_sems, recv_sems, local_sems = rest[2 * nr + 2:2 * nr + 6]
        ride_sems = rest[2 * nr + 6:]
        ride.start(ride_in, ride_out, ride_sems)
        me = _me()
        peers = _peers(me)
        sends = []
        for k, p in enumerate(peers):
            cp = pltpu.make_async_remote_copy(
                src_ref=c_ref, dst_ref=call_ref.at[_idx(me)], send_sem=send_sems.at[0, k],
                recv_sem=recv_sems.at[0, k], device_id=p, device_id_type=MESH)
            cp.start()
            sends.append(cp)
        call_ref[_idx(me)] = c_ref[...]
        for k, p in enumerate(peers):
            slot = call_ref.at[_idx(p)]
            pltpu.make_async_remote_copy(
                src_ref=slot, dst_ref=slot, send_sem=send_sems.at[0, k], recv_sem=recv_sems.at[0, k],
                device_id=p, device_id_type=MESH).wait_recv()
        for cp in sends:
            cp.wait_send()
        cv = call_ref[...].reshape(N_DEV * 8, D_MODEL)
        s = (cv * _sigmoid(cv)).astype(BF16)
        for k in range(4):
            res = _nn(s, w_ref[k].astype(BF16)) + b_ref[k]
            for j in range(N_DEV):
                modp[j, 8 * k:8 * k + 8, :] = res[8 * j:8 * j + 8, :]
        started = _a2a_start([modp], [modr_ref], send_sems, recv_sems, local_sems, me, sem_base=1)
        _a2a_finish(started, [modr_ref], send_sems, recv_sems, me, sem_base=1)
        ride.mid(ride_in, ride_out, ride_sems)
        ride.finish(ride_in, ride_out, ride_sems)

    vm, hbm = pl.BlockSpec(memory_space=pltpu.VMEM), pl.BlockSpec(memory_space=pl.ANY)
    return pl.pallas_call(
        body, name="ada_modulation",
        out_shape=[jax.ShapeDtypeStruct((N_DEV, 8, D_MODEL), F32), jax.ShapeDtypeStruct((N_DEV, 32, CG), F32)]
        + ride.out_shape,
        in_specs=[vm, vm, vm] + [hbm] * nr, out_specs=[vm, vm] + [hbm] * nr,
        scratch_shapes=[pltpu.VMEM((N_DEV, 32, CG), F32), pltpu.SemaphoreType.DMA((2, 7)),
                        pltpu.SemaphoreType.DMA((2, 7)), pltpu.SemaphoreType.DMA((2,))] + ride.scratch,
        compiler_params=_params(),
    )(c8, ada_w, ada_b, *ride.srcs)


def epilogue(gsend, c_t, tail, ride):
    nr = len(ride.srcs)
    rt = tail.shape[1]

    def body(g_ref, ct_ref, t_ref, *rest):
        ride_in, (gw_ref, gb_ref, red_ref), ride_out = rest[:nr], rest[nr:nr + 3], rest[nr + 3:2 * nr + 3]
        grecv, trecv, send_sems, recv_sems, local_sems = rest[2 * nr + 3:2 * nr + 8]
        ride_sems = rest[2 * nr + 8:]
        ride.start(ride_in, ride_out, ride_sems)
        me = _me()
        started = _a2a_start([g_ref, t_ref], [grecv, trecv], send_sems, recv_sems, local_sems, me)
        _a2a_finish(started, [grecv, trecv], send_sems, recv_sems, me)
        acc = trecv[0]
        for j in range(1, N_DEV):
            acc = acc + trecv[j]
        red_ref[...] = acc
        ct = ct_ref[...]
        st = (ct * _sigmoid(ct)).astype(BF16).astype(F32)
        gb = jnp.zeros((8, CG), F32)
        for b in range(N_DEV):
            gb = gb + grecv[b]
        gb_ref[...] = gb
        for k in range(4):
            acc = jnp.zeros((D_MODEL, CG), F32)
            for b in range(N_DEV):
                row = grecv[b, k:k + 1, :].astype(BF16).astype(F32)
                acc = acc + st[:, b:b + 1] * row
            gw_ref[k] = acc
        ride.mid(ride_in, ride_out, ride_sems)
        ride.finish(ride_in, ride_out, ride_sems)

    vm, hbm = pl.BlockSpec(memory_space=pltpu.VMEM), pl.BlockSpec(memory_space=pl.ANY)
    return pl.pallas_call(
        body, name="epilogue",
        out_shape=[jax.ShapeDtypeStruct((4, D_MODEL, CG), F32), jax.ShapeDtypeStruct((8, CG), F32),
                   jax.ShapeDtypeStruct((rt, 128), F32)] + ride.out_shape,
        in_specs=[vm, vm, vm] + [hbm] * nr, out_specs=[vm, vm, vm] + [hbm] * nr,
        scratch_shapes=[pltpu.VMEM((N_DEV, 8, CG), F32), pltpu.VMEM((N_DEV, rt, 128), F32),
                        pltpu.SemaphoreType.DMA((2, 7)), pltpu.SemaphoreType.DMA((2, 7)),
                        pltpu.SemaphoreType.DMA((2,))] + ride.scratch,
        compiler_params=_params(),
    )(gsend, c_t, tail, *ride.srcs)


def _rope(t, cos, s1, s2):
    return t * cos + pltpu.roll(t, 16, 1) * s1 + pltpu.roll(t, HEAD - 16, 1) * s2


def _rope_bwd(d, cos, s1, s2):
    return d * cos + pltpu.roll(d * s1, HEAD - 16, 1) + pltpu.roll(d * s2, 16, 1)


def attn_in_fwd(x, mods, k, win_t, rope, rider=None):
    s = x.shape[0]
    tm = _blk(s, TM_MM)

    def body(x_ref, mod_ref, w_ref, c_ref, s1_ref, s2_ref, q_ref, k_ref, v_ref):
        shift, scale, _ = _mod(mod_ref, k)
        h = (x_ref[...] * (1.0 + scale) + shift).astype(BF16)
        qkv = _nt(h, w_ref[...])
        cos, s1, s2 = c_ref[...], s1_ref[...], s2_ref[...]
        for hh in range(N_Q + N_KV):
            r = _rope(qkv[:, HEAD * hh:HEAD * (hh + 1)], cos, s1, s2).astype(BF16)
            if hh < N_Q:
                q_ref[:, HEAD * hh:HEAD * (hh + 1)] = r
            else:
                k_ref[:, HEAD * (hh - N_Q):HEAD * (hh - N_Q + 1)] = r
        v_ref[...] = qkv[:, HEAD * (N_Q + N_KV):].astype(BF16)

    return _call(
        body, name="attn_in_fwd", grid=(s // tm,),
        out_shape=[jax.ShapeDtypeStruct((s, N_Q * HEAD), BF16), jax.ShapeDtypeStruct((s, N_KV * HEAD), BF16),
                   jax.ShapeDtypeStruct((s, N_KV * HEAD), BF16)],
        in_specs=[_row(tm, D_MODEL), _res(mods.shape), _res(win_t.shape),
                  _row(tm, HEAD), _row(tm, HEAD), _row(tm, HEAD)],
        out_specs=[_row(tm, N_Q * HEAD), _row(tm, N_KV * HEAD), _row(tm, N_KV * HEAD)],
        args=(x, mods, win_t, *rope), rider=rider)


QPAIR = 2


def _kv_specs(nblk):
    w = N_KV * HEAD
    return [pl.BlockSpec((QBLK, w), lambda n: (jnp.maximum(QPAIR * n - 1, 0), 0)),
            pl.BlockSpec((QBLK, w), lambda n: (QPAIR * n, 0)),
            pl.BlockSpec((QBLK, w), lambda n: (QPAIR * n + 1, 0)),
            pl.BlockSpec((QBLK, w), lambda n: (jnp.minimum(QPAIR * n + 2, nblk - 1), 0))]


GROUP = N_Q // N_KV


def _attn_mask(n, s):
    qi = lax.broadcasted_iota(jnp.int32, (GROUP * QBLK, 3 * QBLK), 0) & (QBLK - 1)
    kj = lax.broadcasted_iota(jnp.int32, (GROUP * QBLK, 3 * QBLK), 1)
    rel = kj - QBLK - qi
    kpos = kj + (n - 1) * QBLK
    return (jnp.abs(rel) <= QBLK) & (kpos >= 0) & (kpos < s)


def _stack_heads(ref, qb, kv):
    rows = slice(QBLK * qb, QBLK * (qb + 1))
    return jnp.concatenate([ref[rows, HEAD * (GROUP * kv + j):HEAD * (GROUP * kv + j + 1)] for j in range(GROUP)],
                           axis=0)


def _stack_sinks(sink_ref, kv):
    row = lax.broadcasted_iota(jnp.int32, (GROUP * QBLK, 1), 0)
    out = jnp.full((GROUP * QBLK, 1), sink_ref[0, GROUP * kv + GROUP - 1], F32)
    for j in range(GROUP - 2, -1, -1):
        out = jnp.where(row < QBLK * (j + 1), sink_ref[0, GROUP * kv + j], out)
    return out


def _attn_probs(qh, kh, valid, sink):
    sc = _nt(qh, kh) * (HEAD ** -0.5)
    sc = jnp.where(valid, sc, -1e30)
    m = jnp.maximum(jnp.max(sc, axis=-1, keepdims=True), sink)
    p = jnp.exp(sc - m)
    es = jnp.exp(sink - m)
    denom = jnp.sum(p, axis=-1, keepdims=True) + es
    return p / denom, es / denom


def attn_fwd(q, kk, v, sinks, rider=None):
    s = q.shape[0]
    nblk = s // QBLK

    def body(sink_ref, q_ref, k0, k1, k2, k3, v0, v1, v2, v3, o_ref):
        n = pl.program_id(0)
        kall = jnp.concatenate([k0[...], k1[...], k2[...], k3[...]], axis=0)
        vall = jnp.concatenate([v0[...], v1[...], v2[...], v3[...]], axis=0)
        for qb in range(QPAIR):
            valid = _attn_mask(QPAIR * n + qb, s)
            keys = slice(QBLK * qb, QBLK * (qb + 3))
            for kv in range(N_KV):
                cols = slice(HEAD * kv, HEAD * (kv + 1))
                probs, _ = _attn_probs(_stack_heads(q_ref, qb, kv), kall[keys, cols], valid,
                                       _stack_sinks(sink_ref, kv))
                og = _nn(probs.astype(BF16), vall[keys, cols]).astype(BF16)
                for j in range(GROUP):
                    hq = GROUP * kv + j
                    o_ref[QBLK * qb:QBLK * (qb + 1), HEAD * hq:HEAD * (hq + 1)] = og[QBLK * j:QBLK * (j + 1), :]

    qspec = pl.BlockSpec((QPAIR * QBLK, N_Q * HEAD), lambda n: (n, 0))
    return _call(
        body, name="attn_fwd", grid=(nblk // QPAIR,),
        out_shape=[jax.ShapeDtypeStruct((s, N_Q * HEAD), BF16)],
        in_specs=[pl.BlockSpec(memory_space=pltpu.SMEM), qspec] + _kv_specs(nblk) + _kv_specs(nblk),
        out_specs=[qspec],
        args=(sinks, q, kk, kk, kk, kk, v, v, v, v), rider=rider)


def post_fwd(ypre, w, x, mods, k, lng, lnb, gate_act=None, rider=None):
    s = x.shape[0]
    tm = _blk(s, TM_MM)
    kdim = w.shape[0]
    rnn = gate_act is not None

    def body(*refs):
        if rnn:
            gt_ref, hs_ref, w_ref, x_ref, mod_ref, g_ref, b_ref, xo_ref, y_ref, yp_ref = refs
            act, _ = _gelu_parts(gt_ref[...].astype(F32))
            yp = (hs_ref[...].astype(F32) * act).astype(BF16)
            yp_ref[...] = yp
        else:
            yp_ref, w_ref, x_ref, mod_ref, g_ref, b_ref, xo_ref, y_ref = refs
            yp = yp_ref[...]
        _, _, gate = _mod(mod_ref, k)
        y = _nn(yp, w_ref[...])
        y_ref[...] = y
        xhat, _ = _ln_stats(ALPHA * x_ref[...] + (1.0 + gate) * y)
        xo_ref[...] = xhat * g_ref[...] + b_ref[...]

    act_in = list(gate_act) if rnn else [ypre]
    out_shape = [jax.ShapeDtypeStruct((s, D_MODEL), F32), jax.ShapeDtypeStruct((s, D_MODEL), F32)]
    out_specs = [_row(tm, D_MODEL), _row(tm, D_MODEL)]
    if rnn:
        out_shape.append(jax.ShapeDtypeStruct((s, kdim), BF16))
        out_specs.append(_row(tm, kdim))
    return _call(
        body, name="rnn_post_fwd" if rnn else "attn_post_fwd", grid=(s // tm,),
        out_shape=out_shape,
        in_specs=[_row(tm, kdim)] * len(act_in) + [_res(w.shape), _row(tm, D_MODEL), _res(mods.shape),
                                                    _res(lng.shape), _res(lnb.shape)],
        out_specs=out_specs,
        args=(*act_in, w, x, mods, lng, lnb), rider=rider)


def mlp_fwd(x, mods, k, w1_t, w2, lng, lnb, rider=None, last=False):
    s = x.shape[0]
    tm = _blk(s, TM_MLP_FWD)

    def body(x_ref, mod_ref, w1_ref, w2_ref, g_ref, b_ref, *outs):
        xo_ref = None if last else outs[0]
        y_ref, ra_ref, r_ref = outs[-3:]
        xv = x_ref[...]
        shift, scale, gate = _mod(mod_ref, k)
        h = (xv * (1.0 + scale) + shift).astype(BF16)
        y = jnp.zeros((tm, D_MODEL), F32)
        for c in range(D_FF // FF_CHUNK):
            rows = slice(FF_CHUNK * c, FF_CHUNK * (c + 1))
            a = jnp.maximum(_nt(h, w1_ref[rows, :]), 0.0)
            r = (a * a).astype(BF16)
            ra_ref[:, rows] = a.astype(BF16)
            r_ref[:, rows] = r
            y = y + _nn(r, w2_ref[rows, :])
        y_ref[...] = y
        if not last:
            xhat, _ = _ln_stats(ALPHA * xv + (1.0 + gate) * y)
            xo_ref[...] = xhat * g_ref[...] + b_ref[...]

    nf = 1 if last else 2
    return _call(
        body, name="mlp_fwd_last" if last else "mlp_fwd", grid=(s // tm,),
        out_shape=[jax.ShapeDtypeStruct((s, D_MODEL), F32)] * nf + [jax.ShapeDtypeStruct((s, D_FF), BF16)] * 2,
        in_specs=[_row(tm, D_MODEL), _res(mods.shape), _res(w1_t.shape), _res(w2.shape),
                  _res(lng.shape), _res(lnb.shape)],
        out_specs=[_row(tm, D_MODEL)] * nf + [_row(tm, D_FF)] * 2,
        args=(x, mods, w1_t, w2, lng, lnb), rider=rider)


def rnn_in_fwd(x, mods, k, win_t):
    s = x.shape[0]
    tm = _blk(s, TM_MM)

    def body(x_ref, mod_ref, w_ref, xr_ref, gt_ref):
        shift, scale, _ = _mod(mod_ref, k)
        h = (x_ref[...] * (1.0 + scale) + shift).astype(BF16)
        xr_ref[...] = _nt(h, w_ref[0:D_RNN, :])
        gt_ref[...] = _nt(h, w_ref[D_RNN:2 * D_RNN, :]).astype(BF16)

    return pl.pallas_call(
        body, name="rnn_in_fwd", grid=(s // tm,),
        out_shape=[jax.ShapeDtypeStruct((s, D_RNN), F32), jax.ShapeDtypeStruct((s, D_RNN), BF16)],
        in_specs=[_row(tm, D_MODEL), _res(mods.shape), _res(win_t.shape)],
        out_specs=[_row(tm, D_RNN)] * 2,
        compiler_params=_params(),
    )(x, mods, win_t)


def _shift_rows(v, k, row):
    n = v.shape[0]
    r = pltpu.roll(v, k % n, 0)
    keep = (row >= k) if k > 0 else (row < n + k)
    return jnp.where(keep, r, 0.0)


def conv_fwd(xr, cw, cb):
    s = xr.shape[0]

    def body(x_ref, w_ref, b_ref, o_ref):
        xv = x_ref[...]
        row = lax.broadcasted_iota(jnp.int32, xv.shape, 0)
        o_ref[...] = (b_ref[...] + w_ref[0:1, :] * _shift_rows(xv, 2, row) + w_ref[1:2, :] * _shift_rows(xv, 1, row)
                      + w_ref[2:3, :] * xv + w_ref[3:4, :] * _shift_rows(xv, -1, row))

    slab = pl.BlockSpec((s, 128), lambda j: (0, j))
    return pl.pallas_call(
        body, name="conv_fwd", grid=(D_RNN // 128,),
        out_shape=jax.ShapeDtypeStruct((D_RNN // 128, s, 128), F32),
        in_specs=[slab, pl.BlockSpec((4, 128), lambda j: (0, j)), pl.BlockSpec((1, 128), lambda j: (0, j))],
        out_specs=pl.BlockSpec((None, s, 128), lambda j: (j, 0, 0)),
        compiler_params=_params(),
    )(xr, cw, cb)


def _softplus_neg(lam):
    z = -lam
    e = jnp.exp(-jnp.abs(z))
    u = 1.0 + e
    log1p = jnp.where(u == 1.0, e, jnp.log(u) * e / jnp.where(u == 1.0, 1.0, u - 1.0))
    return jnp.maximum(z, 0.0) + log1p, 1.0 / (1.0 + jnp.exp(lam))


def _lru_gates(xv, wa_ref, wx_ref, ba_ref, bx_ref, lam_ref):
    xb = xv.astype(BF16)
    r = _sigmoid(_nn(xb, wa_ref[...]) + ba_ref[...])
    i = _sigmoid(_nn(xb, wx_ref[...]) + bx_ref[...])
    sp, sg = _softplus_neg(lam_ref[...])
    la = r * (-LRU_C * sp)
    a = jnp.exp(la)
    th = jnp.tanh(la)
    m2 = -2.0 * th / (1.0 - th)
    rmult = lax.rsqrt(jnp.maximum(m2, 1e-37))
    return xb, r, i, sp, sg, a, m2 * rmult, rmult


SLABS = CG // 128
GRP, SEG = 32, 4


def _lru_specs(nt, tt, reverse):
    tmap = (lambda t: nt - 1 - t) if reverse else (lambda t: t)
    blk = pl.BlockSpec((tt, CG), lambda g, t: (tmap(t), g))
    slabs = pl.BlockSpec((SLABS, tt, 128), lambda g, t: (g, tmap(t), 0))
    wsp = pl.BlockSpec((None, CG, CG), lambda g, t: (g, 0, 0))
    vec = pl.BlockSpec((1, CG), lambda g, t: (0, g))
    return tmap, blk, slabs, wsp, vec


def _slab_rows(ref3):
    return jnp.concatenate([ref3[l] for l in range(ref3.shape[0])], axis=1)


def _perm_load(ref3, tt):
    out = []
    for l in range(SLABS):
        r = ref3.at[l]
        out.append(jnp.concatenate([r[pl.ds(GRP * g + i, 8, stride=SEG), :]
                                    for g in range(tt // GRP) for i in range(SEG)], axis=0))
    return jnp.concatenate(out, axis=1)


def _perm_scan(a, u, carry, reverse, emit):
    n, c = a.shape
    sub = lax.broadcasted_iota(jnp.int32, (8, c), 0)
    steps = [(8 - sh, sub < 8 - sh) if reverse else (sh, sub >= sh) for sh in (1, 2, 4)]
    order = range(SEG - 1, -1, -1) if reverse else range(SEG)
    for g in (range(n // GRP - 1, -1, -1) if reverse else range(n // GRP)):
        hs, ps = [None] * SEG, [None] * SEG
        h = p = None
        for i in order:
            rows = slice(GRP * g + 8 * i, GRP * g + 8 * i + 8)
            h = u[rows] if h is None else a[rows] * h + u[rows]
            p = a[rows] if p is None else a[rows] * p
            hs[i], ps[i] = h, p
        d, f = p, h
        for rot, keep in steps:
            d_s = jnp.where(keep, pltpu.roll(d, rot, 0), 1.0)
            f_s = jnp.where(keep, pltpu.roll(f, rot, 0), 0.0)
            f = d * f_s + f
            d = d * d_s
        end = f + d * carry
        if reverse:
            init = jnp.where(sub == 7, carry, pltpu.roll(end, 7, 0))
            carry = jnp.broadcast_to(end[0:1], (8, c))
        else:
            init = jnp.where(sub == 0, carry, pltpu.roll(end, 1, 0))
            carry = jnp.broadcast_to(end[7:8], (8, c))
        for i in range(SEG):
            emit(g, i, hs[i] + ps[i] * init)
    return carry


def lru_fwd(xc, wa, wx, ba, bx, lam, reverse, other=None):
    s = xc.shape[1]
    tt = _blk(s, TT_RNN)
    nt = s // tt

    def body(x_ref, wa_ref, wx_ref, ba_ref, bx_ref, lam_ref, *rest):
        if other is None:
            hs_ref, carry = rest
        else:
            oth_ref, hs_ref, sum_ref, carry = rest

        @pl.when(pl.program_id(1) == 0)
        def _():
            carry[...] = jnp.zeros(carry.shape, F32)

        xv = _perm_load(x_ref, tt)
        _, _, i, _, _, a, mult, _ = _lru_gates(xv, wa_ref, wx_ref, ba_ref, bx_ref, lam_ref)
        u = mult * (i * xv)

        def emit(g, j, rows):
            for l in range(SLABS):
                hs_ref.at[l][pl.ds(GRP * g + j, 8, stride=SEG), :] = rows[:, 128 * l:128 * (l + 1)]

        carry[...] = _perm_scan(a, u, carry[...], reverse, emit)
        if other is not None:
            for l in range(SLABS):
                sum_ref[:, 128 * l:128 * (l + 1)] = (hs_ref[l] + oth_ref[l]).astype(BF16)

    _, blk, slabs, wsp, vec = _lru_specs(nt, tt, reverse)
    extra = [] if other is None else [other]
    return pl.pallas_call(
        body, name="lru_fwd_rev" if reverse else "lru_fwd", grid=(N_CG, nt),
        out_shape=[jax.ShapeDtypeStruct(xc.shape, F32)] + [jax.ShapeDtypeStruct((s, D_RNN), BF16)] * len(extra),
        in_specs=[slabs, wsp, wsp, vec, vec, vec] + [slabs] * len(extra), out_specs=[slabs] + [blk] * len(extra),
        scratch_shapes=[pltpu.VMEM((8, CG), F32)],
        compiler_params=_params(),
    )(xc, wa, wx, ba, bx, lam, *extra)


def lru_bwd(xc, dhs, hs, wa, wx, ba, bx, lam, reverse, rider=None):
    s = xc.shape[1]
    tt = _blk(s, TT_RNN)
    nt = s // tt
    ng = tt // GRP
    back = not reverse

    def rows_of(v, g, i):
        return v[GRP * g + 8 * i:GRP * g + 8 * i + 8]

    def neighbour(v, past, edge, sub):
        out = []
        for g in range(ng):
            for i in range(SEG):
                if past and i > 0:
                    r = rows_of(v, g, i - 1)
                elif past:
                    e = edge if g == 0 else rows_of(v, g - 1, SEG - 1)[7:8]
                    r = jnp.where(sub == 0, e, pltpu.roll(rows_of(v, g, SEG - 1), 1, 0))
                elif i < SEG - 1:
                    r = rows_of(v, g, i + 1)
                else:
                    e = edge if g == ng - 1 else rows_of(v, g + 1, 0)[0:1]
                    r = jnp.where(sub == 7, e, pltpu.roll(rows_of(v, g, 0), 7, 0))
                out.append(r)
        return jnp.concatenate(out, axis=0)

    def body(x_ref, dh_ref, hs_ref, nb_ref, wa_ref, wx_ref, ba_ref, bx_ref, lam_ref,
             dx_ref, dwa_ref, dwx_ref, dba_ref, dbx_ref, dlam_ref, carry):
        t = pl.program_id(1)

        @pl.when(t == 0)
        def _():
            carry[...] = jnp.zeros(carry.shape, F32)
            dwa_ref[...] = jnp.zeros(dwa_ref.shape, F32)
            dwx_ref[...] = jnp.zeros(dwx_ref.shape, F32)
            dba_ref[...] = jnp.zeros(dba_ref.shape, F32)
            dbx_ref[...] = jnp.zeros(dbx_ref.shape, F32)
            dlam_ref[...] = jnp.zeros(dlam_ref.shape, F32)

        xv = _perm_load(x_ref, tt)
        xb, r, i, sp, sg, a, mult, rmult = _lru_gates(xv, wa_ref, wx_ref, ba_ref, bx_ref, lam_ref)
        sub = lax.broadcasted_iota(jnp.int32, (8, CG), 0)
        hsv = _perm_load(hs_ref, tt)
        nbv = _slab_rows(nb_ref)
        inner = t < nt - 1
        h_edge = jnp.where(inner, nbv[0:1, :] if reverse else nbv[7:8, :], 0.0)
        hprev = neighbour(hsv, not reverse, h_edge, sub)
        a_next = neighbour(a, reverse, carry[8:9, :], sub)
        dhv = _perm_load(dh_ref, tt)
        gl = [None] * (ng * SEG)

        def emit(gi, j, rows):
            gl[gi * SEG + j] = rows

        carry[0:8, :] = _perm_scan(a_next, dhv, carry[0:8, :], back, emit)
        g = jnp.concatenate(gl, axis=0)
        carry[8:9, :] = a[0:1, :] if back else a[tt - 1:tt, :]

        da = g * hprev
        dmult = g * (i * xv)
        di = g * mult * xv
        dla = da * a - dmult * (a * a) * rmult
        dpa = (dla * (-LRU_C * sp)) * r * (1.0 - r)
        dpx = di * i * (1.0 - i)
        dlam_ref[...] += _colsum(dla * (LRU_C * r * sg))
        dba_ref[...] += _colsum(dpa)
        dbx_ref[...] += _colsum(dpx)
        dpab, dpxb = dpa.astype(BF16), dpx.astype(BF16)
        dxv = g * mult * i + _nt(dpab, wa_ref[...]) + _nt(dpxb, wx_ref[...])
        for gi in range(ng):
            for j in range(SEG):
                for l in range(SLABS):
                    dx_ref.at[l][pl.ds(GRP * gi + j, 8, stride=SEG), :] = rows_of(dxv, gi, j)[:, 128 * l:128 * (l + 1)]
        dwa_ref[...] += _tn(xb, dpab)
        dwx_ref[...] += _tn(xb, dpxb)

    tmap, blk, slabs, wsp, vec = _lru_specs(nt, tt, back)
    per8 = tt // 8
    if reverse:
        nb = pl.BlockSpec((SLABS, 8, 128), lambda g, t: (g, jnp.minimum((tmap(t) + 1) * per8, s // 8 - 1), 0))
    else:
        nb = pl.BlockSpec((SLABS, 8, 128), lambda g, t: (g, jnp.maximum(tmap(t) * per8 - 1, 0), 0))
    return _call(
        body, name="lru_bwd_rev" if reverse else "lru_bwd", grid=(N_CG, nt),
        out_shape=[jax.ShapeDtypeStruct(xc.shape, F32), jax.ShapeDtypeStruct((N_CG, CG, CG), F32),
                   jax.ShapeDtypeStruct((N_CG, CG, CG), F32)] + [jax.ShapeDtypeStruct((1, D_RNN), F32)] * 3,
        in_specs=[slabs, slabs, slabs, nb, wsp, wsp, vec, vec, vec],
        out_specs=[slabs, wsp, wsp, vec, vec, vec],
        scratch_shapes=[pltpu.VMEM((16, CG), F32)],
        args=(xc, dhs, hs, hs, wa, wx, ba, bx, lam), rider=rider)


def _ln_part_bwd(dxo, x, y, gate, g, sums_ref, loss_head=None):
    xhat, rstd = _ln_stats(ALPHA * x + (1.0 + gate) * y)
    if loss_head is not None:
        err = xhat * g + loss_head[0] - loss_head[1]
        dxo = err * (1.0 / D_MODEL)
        sums_ref[5:6, :] += _colsum(err * err)
    dz = _ln_bwd(dxo, xhat, rstd, g)
    sums_ref[2:3, :] += _colsum(dz * y)
    sums_ref[3:4, :] += _colsum(dxo * xhat)
    sums_ref[4:5, :] += _colsum(dxo)
    return dz


def mlp_bwd(dxo, x, y, ra, mods, k, w1_t, w2, lng, lnb=None, rider=None):
    s = x.shape[0]
    tm = _blk(s, TM_MLP)
    head = lnb is not None

    def body(d_ref, x_ref, y_ref, ra_ref, mod_ref, w1_ref, w2_ref, g_ref, *rest):
        b_ref = rest[0] if head else None
        dx_ref, da_ref, h_ref, dy_ref, sums_ref = rest[1:] if head else rest

        @pl.when(pl.program_id(0) == 0)
        def _():
            sums_ref[...] = jnp.zeros(sums_ref.shape, F32)

        xv = x_ref[...]
        shift, scale, gate = _mod(mod_ref, k)
        if head:
            dz = _ln_part_bwd(None, xv, y_ref[...], gate, g_ref[...], sums_ref, (b_ref[...], d_ref[...]))
        else:
            dz = _ln_part_bwd(d_ref[...], xv, y_ref[...], gate, g_ref[...], sums_ref)
        dyb = (dz * (1.0 + gate)).astype(BF16)
        dy_ref[...] = dyb
        h = (xv * (1.0 + scale) + shift).astype(BF16)
        h_ref[...] = h
        dh = jnp.zeros((tm, D_MODEL), F32)
        for c in range(D_FF // FF_CHUNK):
            rows = slice(FF_CHUNK * c, FF_CHUNK * (c + 1))
            da = (_nt(dyb, w2_ref[rows, :]) * (2.0 * ra_ref[:, rows].astype(F32))).astype(BF16)
            da_ref[:, rows] = da
            dh = dh + _nn(da, w1_ref[rows, :])
        dx_ref[...] = ALPHA * dz + dh * (1.0 + scale)
        sums_ref[0:1, :] += _colsum(dh)
        sums_ref[1:2, :] += _colsum(dh * xv)

    return _call(
        body, name="mlp_bwd", grid=(s // tm,),
        out_shape=[jax.ShapeDtypeStruct((s, D_MODEL), F32), jax.ShapeDtypeStruct((s, D_FF), BF16),
                   jax.ShapeDtypeStruct((s, D_MODEL), BF16),
                   jax.ShapeDtypeStruct((s, D_MODEL), BF16), jax.ShapeDtypeStruct((8, D_MODEL), F32)],
        in_specs=[_row(tm, D_MODEL)] * 3 + [_row(tm, D_FF), _res(mods.shape), _res(w1_t.shape), _res(w2.shape),
                                             _res(lng.shape)] + ([_res(lnb.shape)] if head else []),
        out_specs=[_row(tm, D_MODEL), _row(tm, D_FF), _row(tm, D_MODEL), _row(tm, D_MODEL), _res((8, D_MODEL))],
        args=(dxo, x, y, ra, mods, w1_t, w2, lng) + ((lnb,) if head else ()), rider=rider)


def post_bwd(dxo, x, y, mods, k, w, lng, gate_act=None, rider=None):
    s = x.shape[0]
    tm = _blk(s, TM_MM)
    kdim = w.shape[0]
    rnn = gate_act is not None

    def body(*refs):
        if rnn:
            (d_ref, x_ref, y_ref, mod_ref, w_ref, g_ref, gt_ref, hs_ref,
             dres_ref, dy_ref, sums_ref, dhs_ref, dgt_ref) = refs
        else:
            d_ref, x_ref, y_ref, mod_ref, w_ref, g_ref, dres_ref, dy_ref, sums_ref, dyp_ref = refs

        @pl.when(pl.program_id(0) == 0)
        def _():
            sums_ref[...] = jnp.zeros(sums_ref.shape, F32)

        _, _, gate = _mod(mod_ref, k)
        dz = _ln_part_bwd(d_ref[...], x_ref[...], y_ref[...], gate, g_ref[...], sums_ref)
        dres_ref[...] = ALPHA * dz
        dyb = (dz * (1.0 + gate)).astype(BF16)
        dy_ref[...] = dyb
        dyp = _nt(dyb, w_ref[...])
        if rnn:
            act, dact = _gelu_parts(gt_ref[...].astype(F32))
            dhs = dyp * act
            for l in range(kdim // 128):
                dhs_ref[l] = dhs[:, 128 * l:128 * (l + 1)]
            dgt_ref[...] = (dyp * hs_ref[...].astype(F32) * dact).astype(BF16)
        else:
            dyp_ref[...] = dyp.astype(BF16)

    ins = [dxo, x, y, mods, w, lng] + (list(gate_act) if rnn else [])
    in_specs = [_row(tm, D_MODEL)] * 3 + [_res(mods.shape), _res(w.shape), _res(lng.shape)]
    out_shape = [jax.ShapeDtypeStruct((s, D_MODEL), F32), jax.ShapeDtypeStruct((s, D_MODEL), BF16),
                 jax.ShapeDtypeStruct((8, D_MODEL), F32)]
    out_specs = [_row(tm, D_MODEL), _row(tm, D_MODEL), _res((8, D_MODEL))]
    if rnn:
        in_specs += [_row(tm, kdim)] * 2
        out_shape += [jax.ShapeDtypeStruct((kdim // 128, s, 128), F32), jax.ShapeDtypeStruct((s, kdim), BF16)]
        out_specs += [pl.BlockSpec((kdim // 128, tm, 128), lambda i: (0, i, 0)), _row(tm, kdim)]
    else:
        out_shape.append(jax.ShapeDtypeStruct((s, kdim), BF16))
        out_specs.append(_row(tm, kdim))
    return _call(
        body, name="rnn_post_bwd" if rnn else "attn_post_bwd", grid=(s // tm,),
        out_shape=out_shape, in_specs=in_specs, out_specs=out_specs, args=ins, rider=rider)


def attn_bwd(q, kk, v, do, sinks, rider=None):
    s = q.shape[0]
    nblk = s // QBLK
    scale = HEAD ** -0.5

    def body(sink_ref, q_ref, do_ref, k0, k1, k2, k3, v0, v1, v2, v3, dq_ref, dk_ref, dv_ref, ds_ref):
        n = pl.program_id(0)

        @pl.when(n == 0)
        def _():
            ds_ref[...] = jnp.zeros(ds_ref.shape, F32)
            dk_ref[...] = jnp.zeros(dk_ref.shape, F32)
            dv_ref[...] = jnp.zeros(dv_ref.shape, F32)

        kall = jnp.concatenate([k0[...], k1[...], k2[...], k3[...]], axis=0)
        vall = jnp.concatenate([v0[...], v1[...], v2[...], v3[...]], axis=0)
        lane = lax.broadcasted_iota(jnp.int32, (1, 128), 1)
        dsink = jnp.zeros((1, 128), F32)
        for qb in range(QPAIR):
            nb = QPAIR * n + qb
            valid = _attn_mask(nb, s)
            keys = slice(QBLK * qb, QBLK * (qb + 3))
            for kv in range(N_KV):
                cols = slice(HEAD * kv, HEAD * (kv + 1))
                qg, dog = _stack_heads(q_ref, qb, kv), _stack_heads(do_ref, qb, kv)
                kh, vh = kall[keys, cols], vall[keys, cols]
                probs, psink = _attn_probs(qg, kh, valid, _stack_sinks(sink_ref, kv))
                dprobs = _nt(dog, vh)
                dvp = _tn(probs.astype(BF16), dog)
                rowdot = jnp.sum(probs * dprobs, axis=-1, keepdims=True)
                dsb = (probs * (dprobs - rowdot) * scale).astype(BF16)
                dqg = _nn(dsb, kh)
                dkp = _tn(dsb, qg)
                for p in range(3):
                    blk = jnp.clip(nb - 1 + p, 0, nblk - 1)
                    rows = pl.ds(pl.multiple_of(blk * QBLK, QBLK), QBLK)
                    dk_ref[rows, cols] += dkp[QBLK * p:QBLK * (p + 1), :]
                    dv_ref[rows, cols] += dvp[QBLK * p:QBLK * (p + 1), :]
                dsk = -psink * rowdot
                for j in range(GROUP):
                    hq = GROUP * kv + j
                    dq_ref[QBLK * qb:QBLK * (qb + 1), HEAD * hq:HEAD * (hq + 1)] = dqg[QBLK * j:QBLK * (j + 1), :]
                    dsink = dsink + jnp.where(lane == hq, _colsum(dsk[QBLK * j:QBLK * (j + 1), :]), 0.0)
        ds_ref[...] += dsink

    qspec = pl.BlockSpec((QPAIR * QBLK, N_Q * HEAD), lambda n: (n, 0))
    return _call(
        body, name="attn_bwd", grid=(nblk // QPAIR,),
        out_shape=[jax.ShapeDtypeStruct((s, N_Q * HEAD), F32),
                   jax.ShapeDtypeStruct((s, N_KV * HEAD), F32), jax.ShapeDtypeStruct((s, N_KV * HEAD), F32),
                   jax.ShapeDtypeStruct((1, 128), F32)],
        in_specs=[pl.BlockSpec(memory_space=pltpu.SMEM), qspec, qspec] + _kv_specs(nblk) + _kv_specs(nblk),
        out_specs=[qspec, _res((s, N_KV * HEAD)), _res((s, N_KV * HEAD)), pl.BlockSpec((1, 128), lambda n: (0, 0))],
        args=(sinks, q, do, kk, kk, kk, kk, v, v, v, v), rider=rider)


def _in_bwd_tail(dzb, w_ref, x_ref, mod_ref, k, dres_ref, dx_ref, h_ref, sums_ref):
    xv = x_ref[...]
    shift, scale, _ = _mod(mod_ref, k)
    h_ref[...] = (xv * (1.0 + scale) + shift).astype(BF16)
    dh = _nn(dzb, w_ref[...])
    dx_ref[...] = dres_ref[...] + dh * (1.0 + scale)
    sums_ref[0:1, :] += _colsum(dh)
    sums_ref[1:2, :] += _colsum(dh * xv)


def attn_in_bwd(dq, dk, dv, rope, x, mods, k, win_t, dres):
    s = x.shape[0]
    tm = _blk(s, TM_MM)

    def body(dq_ref, dk_ref, dv_ref, c_ref, s1_ref, s2_ref, x_ref, mod_ref, w_ref, dres_ref,
             dx_ref, dz_ref, h_ref, sums_ref):
        @pl.when(pl.program_id(0) == 0)
        def _():
            sums_ref[...] = jnp.zeros(sums_ref.shape, F32)

        cos, s1, s2 = c_ref[...], s1_ref[...], s2_ref[...]
        for hh in range(N_Q + N_KV):
            src = dq_ref[:, HEAD * hh:HEAD * (hh + 1)] if hh < N_Q else dk_ref[:, HEAD * (hh - N_Q):HEAD * (hh - N_Q + 1)]
            dz_ref[:, HEAD * hh:HEAD * (hh + 1)] = _rope_bwd(src, cos, s1, s2).astype(BF16)
        dz_ref[:, HEAD * (N_Q + N_KV):] = dv_ref[...].astype(BF16)
        _in_bwd_tail(dz_ref[...], w_ref, x_ref, mod_ref, k, dres_ref, dx_ref, h_ref, sums_ref)

    return pl.pallas_call(
        body, name="attn_in_bwd", grid=(s // tm,),
        out_shape=[jax.ShapeDtypeStruct((s, D_MODEL), F32), jax.ShapeDtypeStruct((s, D_QKV), BF16),
                   jax.ShapeDtypeStruct((s, D_MODEL), BF16), jax.ShapeDtypeStruct((8, D_MODEL), F32)],
        in_specs=[_row(tm, N_Q * HEAD), _row(tm, N_KV * HEAD), _row(tm, N_KV * HEAD),
                  _row(tm, HEAD), _row(tm, HEAD), _row(tm, HEAD), _row(tm, D_MODEL),
                  _res(mods.shape), _res(win_t.shape), _row(tm, D_MODEL)],
        out_specs=[_row(tm, D_MODEL), _row(tm, D_QKV), _row(tm, D_MODEL), _res((8, D_MODEL))],
        compiler_params=_params(),
    )(dq, dk, dv, *rope, x, mods, win_t, dres)


def _shift_blk(v, k, before, after, row):
    n = v.shape[0]
    r = pltpu.roll(v, k % n, 0)
    for j in range(abs(k)):
        if k > 0:
            r = jnp.where(row == j, before[8 - k + j:8 - k + j + 1, :], r)
        else:
            r = jnp.where(row == n + k + j, after[j:j + 1, :], r)
    return r


def rnn_in_bwd(dxc_f, dxc_b, xr, cw, dgt, x, mods, k, win_t, dres):
    s = x.shape[0]
    tm = _blk(s, TM_MM)
    n = s // tm

    def body(f_ref, fp_ref, fn_ref, b_ref, bp_ref, bn_ref, xr_ref, xp_ref, xn_ref, cw_ref, dgt_ref,
             x_ref, mod_ref, w_ref, dres_ref, dx_ref, dz_ref, h_ref, sums_ref, dcw_ref, dcb_ref):
        i = pl.program_id(0)

        @pl.when(i == 0)
        def _():
            sums_ref[...] = jnp.zeros(sums_ref.shape, F32)
            dcw_ref[...] = jnp.zeros(dcw_ref.shape, F32)
            dcb_ref[...] = jnp.zeros(dcb_ref.shape, F32)

        d = _slab_rows(f_ref) + _slab_rows(b_ref)
        xv = xr_ref[...]
        first, last = i == 0, i == n - 1
        d_before = jnp.where(first, 0.0, _slab_rows(fp_ref) + _slab_rows(bp_ref))
        d_after = jnp.where(last, 0.0, _slab_rows(fn_ref) + _slab_rows(bn_ref))
        x_before = jnp.where(first, 0.0, xp_ref[...])
        x_after = jnp.where(last, 0.0, xn_ref[...])
        row = lax.broadcasted_iota(jnp.int32, d.shape, 0)
        dxr = (cw_ref[0:1, :] * _shift_blk(d, -2, d_before, d_after, row)
               + cw_ref[1:2, :] * _shift_blk(d, -1, d_before, d_after, row)
               + cw_ref[2:3, :] * d + cw_ref[3:4, :] * _shift_blk(d, 1, d_before, d_after, row))
        dcw_ref[0:1, :] += _colsum(d * _shift_blk(xv, 2, x_before, x_after, row))
        dcw_ref[1:2, :] += _colsum(d * _shift_blk(xv, 1, x_before, x_after, row))
        dcw_ref[2:3, :] += _colsum(d * xv)
        dcw_ref[3:4, :] += _colsum(d * _shift_blk(xv, -1, x_before, x_after, row))
        dcb_ref[...] += _colsum(d)
        dz_ref[:, 0:D_RNN] = dxr.astype(BF16)
        dz_ref[:, D_RNN:2 * D_RNN] = dgt_ref[...]
        _in_bwd_tail(dz_ref[...], w_ref, x_ref, mod_ref, k, dres_ref, dx_ref, h_ref, sums_ref)

    per8 = tm // 8
    ns = D_RNN // 128
    blk = _row(tm, D_RNN)
    before = pl.BlockSpec((8, D_RNN), lambda i: (jnp.maximum(i * per8 - 1, 0), 0))
    after = pl.BlockSpec((8, D_RNN), lambda i: (jnp.minimum((i + 1) * per8, s // 8 - 1), 0))
    sblk = pl.BlockSpec((ns, tm, 128), lambda i: (0, i, 0))
    sbefore = pl.BlockSpec((ns, 8, 128), lambda i: (0, jnp.maximum(i * per8 - 1, 0), 0))
    safter = pl.BlockSpec((ns, 8, 128), lambda i: (0, jnp.minimum((i + 1) * per8, s // 8 - 1), 0))
    return pl.pallas_call(
        body, name="rnn_in_bwd", grid=(n,),
        out_shape=[jax.ShapeDtypeStruct((s, D_MODEL), F32), jax.ShapeDtypeStruct((s, 2 * D_RNN), BF16),
                   jax.ShapeDtypeStruct((s, D_MODEL), BF16), jax.ShapeDtypeStruct((8, D_MODEL), F32),
                   jax.ShapeDtypeStruct((4, D_RNN), F32), jax.ShapeDtypeStruct((1, D_RNN), F32)],
        in_specs=[sblk, sbefore, safter] * 2 + [blk, before, after] + [
            _res(cw.shape), blk, _row(tm, D_MODEL), _res(mods.shape), _res(win_t.shape), _row(tm, D_MODEL)],
        out_specs=[_row(tm, D_MODEL), _row(tm, 2 * D_RNN), _row(tm, D_MODEL), _res((8, D_MODEL)),
                   _res((4, D_RNN)), _res((1, D_RNN))],
        compiler_params=_params(),
    )(dxc_f, dxc_f, dxc_f, dxc_b, dxc_b, dxc_b, xr, xr, xr, cw, dgt, x, mods, win_t, dres)


def wgrad(a, b, name, rider=None):
    s, m = a.shape
    n = b.shape[1]
    tm = next(t for t in (1024, 768, 512, 384, 256, 128) if m % t == 0)
    tk = _blk(s, TK_WG)
    nk = s // tk

    def body(a_ref, b_ref, o_ref, acc):
        kk = pl.program_id(1)

        @pl.when(kk == 0)
        def _():
            acc[...] = jnp.zeros(acc.shape, F32)

        acc[...] += _tn(a_ref[...], b_ref[...])

        @pl.when(kk == nk - 1)
        def _():
            o_ref[...] = acc[...].astype(BF16)

    out, *rode = _call(
        body, name=name, grid=(m // tm, nk),
        out_shape=[jax.ShapeDtypeStruct((m, n), BF16)],
        in_specs=[pl.BlockSpec((tk, tm), lambda i, kk: (kk, i)), pl.BlockSpec((tk, n), lambda i, kk: (kk, 0))],
        out_specs=[pl.BlockSpec((tm, n), lambda i, kk: (i, 0))],
        scratch_shapes=[pltpu.VMEM((tm, n), F32)],
        args=(a, b), rider=rider)
    out = out.reshape(N_DEV, m // N_DEV, n)
    return (out, *rode) if rider is not None else out


def part_sum(parts, name):
    _, r, c = parts.shape
    tr = next(t for t in (256, 192, 128, 64, 32, 16, 8) if r % t == 0)

    def body(p_ref, o_ref):
        acc = p_ref[0].astype(F32)
        for j in range(1, N_DEV):
            acc = acc + p_ref[j].astype(F32)
        o_ref[...] = acc

    return pl.pallas_call(
        body, name=name, grid=(r // tr,),
        out_shape=jax.ShapeDtypeStruct((r, c), F32),
        in_specs=[pl.BlockSpec((N_DEV, tr, c), lambda i: (0, i, 0))],
        out_specs=pl.BlockSpec((tr, c), lambda i: (i, 0)),
        compiler_params=_params(),
    )(parts)


def adamw(w, g, m, v, name):
    shape = w.shape
    c = shape[-1]
    r = w.size // c
    w2, g2, m2, v2 = (t.reshape(r, c) for t in (w, g, m, v))
    tr = r if r * c <= 512 * 1024 else next(t for t in (512, 256, 128, 64, 32, 16, 8) if r % t == 0)

    def body(w_ref, g_ref, m_ref, v_ref, d_ref, nm_ref, nv_ref):
        gv = g_ref[...]
        nm = B1 * m_ref[...] + (1.0 - B1) * gv
        nv = B2 * v_ref[...] + (1.0 - B2) * (gv * gv)
        nm_ref[...] = nm
        nv_ref[...] = nv
        m_hat = nm / (1.0 - B1 ** STEP)
        v_hat = nv / (1.0 - B2 ** STEP)
        d_ref[...] = -LR * (m_hat / (jnp.sqrt(v_hat) + ADAM_EPS) + WD * w_ref[...])

    spec = pl.BlockSpec((tr, c), lambda i: (i, 0))
    outs = pl.pallas_call(
        body, name=name, grid=(r // tr,),
        out_shape=[jax.ShapeDtypeStruct((r, c), F32)] * 3,
        in_specs=[spec] * 4, out_specs=[spec] * 3,
        compiler_params=_params(),
    )(w2, g2, m2, v2)
    return tuple(o.reshape(shape) for o in outs)


def _rope_tables(s):
    half = ROT // 2
    inv_freq = THETA ** (-jnp.arange(0, ROT, 2, dtype=F32) / ROT)
    per_row = 128 // half
    pos = (per_row * jnp.arange(s // per_row)[:, None] + jnp.arange(128)[None, :] // half).astype(F32)
    ang = pos * jnp.tile(inv_freq, per_row)[None, :]
    cos, sin = lax.optimization_barrier((jnp.cos(ang), jnp.sin(ang)))
    cos, sin = cos.reshape(s, half), sin.reshape(s, half)
    zeros = jnp.zeros((s, HEAD - ROT), F32)
    c = jnp.concatenate([cos, cos, jnp.ones((s, HEAD - ROT), F32)], axis=1)
    s1 = jnp.concatenate([jnp.zeros((s, half), F32), sin, zeros], axis=1)
    s2 = jnp.concatenate([-sin, jnp.zeros((s, half), F32), zeros], axis=1)
    return c, s1, s2


def _blockdiag(w):
    w4 = w.reshape(N_CG, 4, RB_W, RB_W)
    eye = jnp.eye(4, dtype=w.dtype)
    return jnp.einsum("gipq,ij->gipjq", w4, eye).reshape(N_CG, CG, CG)


def _diag_blocks(w):
    w5 = w.reshape(N_CG, 4, RB_W, 4, RB_W)
    eye = jnp.eye(4, dtype=w.dtype)
    return jnp.einsum("gipjq,ij->gipq", w5, eye).reshape(N_RB, RB_W, RB_W)


def _cols(full, per):
    lead = full.shape[:-1]
    t = full.reshape(lead + (N_DEV, per))
    return jnp.moveaxis(t, -2, 0).reshape(N_DEV, -1)


def kernel(x, c, ada_w, ada_b, ln_g, ln_b, attn_w_in, attn_w_out, attn_sinks, rnn_w_in, rnn_conv_w, rnn_conv_b, rnn_w_a, rnn_b_a, rnn_w_x, rnn_b_x, rnn_lam, rnn_w_out, mlp_w1, mlp_w2, loss_target, m_ada_w, m_ada_b, m_ln_g, m_ln_b, m_attn_w_in, m_attn_w_out, m_attn_sinks, m_rnn_w_in, m_rnn_conv_w, m_rnn_conv_b, m_rnn_w_a, m_rnn_b_a, m_rnn_w_x, m_rnn_b_x, m_rnn_lam, m_rnn_w_out, m_mlp_w1, m_mlp_w2, v_ada_w, v_ada_b, v_ln_g, v_ln_b, v_attn_w_in, v_attn_w_out, v_attn_sinks, v_rnn_w_in, v_rnn_conv_w, v_rnn_conv_b, v_rnn_w_a, v_rnn_b_a, v_rnn_w_x, v_rnn_b_x, v_rnn_lam, v_rnn_w_out, v_mlp_w1, v_mlp_w2):
    s = x.shape[1]
    x0 = x.reshape(s, D_MODEL)
    target = loss_target.reshape(s, D_MODEL)
    weights = dict(ada_w=ada_w, ada_b=ada_b, ln_g=ln_g, ln_b=ln_b, attn_w_in=attn_w_in, attn_w_out=attn_w_out,
                   attn_sinks=attn_sinks, rnn_w_in=rnn_w_in, rnn_conv_w=rnn_conv_w, rnn_conv_b=rnn_conv_b,
                   rnn_w_a=rnn_w_a, rnn_b_a=rnn_b_a, rnn_w_x=rnn_w_x, rnn_b_x=rnn_b_x, rnn_lam=rnn_lam,
                   rnn_w_out=rnn_w_out, mlp_w1=mlp_w1, mlp_w2=mlp_w2)
    moments_m = dict(ada_w=m_ada_w, ada_b=m_ada_b, ln_g=m_ln_g, ln_b=m_ln_b, attn_w_in=m_attn_w_in,
                     attn_w_out=m_attn_w_out, attn_sinks=m_attn_sinks, rnn_w_in=m_rnn_w_in,
                     rnn_conv_w=m_rnn_conv_w, rnn_conv_b=m_rnn_conv_b, rnn_w_a=m_rnn_w_a, rnn_b_a=m_rnn_b_a,
                     rnn_w_x=m_rnn_w_x, rnn_b_x=m_rnn_b_x, rnn_lam=m_rnn_lam, rnn_w_out=m_rnn_w_out,
                     mlp_w1=m_mlp_w1, mlp_w2=m_mlp_w2)
    moments_v = dict(ada_w=v_ada_w, ada_b=v_ada_b, ln_g=v_ln_g, ln_b=v_ln_b, attn_w_in=v_attn_w_in,
                     attn_w_out=v_attn_w_out, attn_sinks=v_attn_sinks, rnn_w_in=v_rnn_w_in,
                     rnn_conv_w=v_rnn_conv_w, rnn_conv_b=v_rnn_conv_b, rnn_w_a=v_rnn_w_a, rnn_b_a=v_rnn_b_a,
                     rnn_w_x=v_rnn_w_x, rnn_b_x=v_rnn_b_x, rnn_lam=v_rnn_lam, rnn_w_out=v_rnn_w_out,
                     mlp_w1=v_mlp_w1, mlp_w2=v_mlp_w2)
    names = list(weights)

    def t16(w):
        return w.T.astype(BF16)

    big = [t16(attn_w_in[0]), attn_w_out[0].astype(BF16), t16(rnn_w_in[0]), rnn_w_out[0].astype(BF16),
           t16(mlp_w1[0]), mlp_w2[0].astype(BF16), t16(mlp_w1[1]), mlp_w2[1].astype(BF16)]
    small_local = jnp.concatenate([
        ln_g.reshape(-1), ln_b.reshape(-1), rnn_conv_w.reshape(-1), rnn_conv_b.reshape(-1),
        rnn_b_a.reshape(-1), rnn_b_x.reshape(-1), rnn_lam.reshape(-1)])
    small_local = jnp.pad(small_local, (0, 4096 - small_local.shape[0])).reshape(32, 128)
    flat = lambda g: g.reshape(N_DEV * g.shape[1], D_MODEL)
    c_all, modr, win_t, sm = ada_modulation(jnp.broadcast_to(c, (8, D_MODEL)), ada_w.reshape(4, D_MODEL, CG),
                                            ada_b.reshape(4, 1, CG), _Gather([big[0], small_local]))
    win_t = flat(win_t)
    sm = sm.reshape(N_DEV, 4096)

    def full_vec(off, rows, per):
        piece = sm[:, off:off + rows * per].reshape(N_DEV, rows, per)
        return jnp.moveaxis(piece, 0, 1).reshape(rows, N_DEV * per)

    lng_f, lnb_f = full_vec(0, 4, 128), full_vec(512, 4, 128)
    cw_f, cb_f = full_vec(1024, 4, 192), full_vec(1792, 1, 192)
    ba_f, bx_f, lam_f = full_vec(1984, 2, 192), full_vec(2368, 2, 192), full_vec(2752, 2, 192)
    wa_bd = [_blockdiag(rnn_w_a[0, d]).astype(BF16) for d in range(2)]
    wx_bd = [_blockdiag(rnn_w_x[0, d]).astype(BF16) for d in range(2)]

    mods = modr.reshape(N_DEV, 4, 8, CG)[:, :, 0, :]
    mods = jnp.moveaxis(mods, 0, 1).reshape(4, 3, D_MODEL).reshape(12, D_MODEL)
    rope = _rope_tables(s)
    ln = lambda k: (lng_f[k:k + 1], lnb_f[k:k + 1])

    q, kk, v, wout = attn_in_fwd(x0, mods, 0, win_t, rope, rider=_Gather([big[1]]))
    wout = flat(wout)
    o, *got = attn_fwd(q, kk, v, attn_sinks, rider=_Gather([big[4], big[5]]))
    w1t_0, w2_0 = (flat(g) for g in got)
    x1, y0, rout = post_fwd(o, wout, x0, mods, 0, *ln(0), rider=_Gather([big[3]]))
    rout = flat(rout)
    x2, y1, ra0, r0, *got = mlp_fwd(x1, mods, 1, w1t_0, w2_0, *ln(1), rider=_Gather([big[2], big[6], big[7]]))
    rin_t, w1t_1, w2_1 = (flat(g) for g in got)
    xr, gt = rnn_in_fwd(x2, mods, 2, rin_t)
    xc = conv_fwd(xr, cw_f, cb_f)
    hf, = lru_fwd(xc, wa_bd[0], wx_bd[0], ba_f[0:1], bx_f[0:1], lam_f[0:1], False)
    hb, hsum = lru_fwd(xc, wa_bd[1], wx_bd[1], ba_f[1:2], bx_f[1:2], lam_f[1:2], True, other=hf)
    x3, y2, ypre = post_fwd(None, rout, x2, mods, 2, *ln(2), gate_act=(gt, hsum))
    y3, ra1, r1 = mlp_fwd(x3, mods, 3, w1t_1, w2_1, *ln(3), last=True)

    dx3, da1, h3, dy3, sums3 = mlp_bwd(target, x3, y3, ra1, mods, 3, w1t_1, w2_1, lng_f[3:4], lnb=lnb_f[3:4])
    g_w1t_1 = wgrad(da1, h3, "wgrad_w1_1")
    g_w2_1 = wgrad(r1, dy3, "wgrad_w2_1")
    dres2, dy2, sums2a, dhs, dgt, p_w1t_1 = post_bwd(dx3, x2, y2, mods, 2, rout, lng_f[2:3], gate_act=(gt, hsum),
                                                     rider=_AllToAll([g_w1t_1]))
    g_rout = wgrad(ypre, dy2, "wgrad_rnn_out")
    dxc_f, dwa_f, dwx_f, dba_f, dbx_f, dlam_f, p_w2_1, p_rout = lru_bwd(
        xc, dhs, hf, wa_bd[0], wx_bd[0], ba_f[0:1], bx_f[0:1], lam_f[0:1], False, rider=_AllToAll([g_w2_1, g_rout]))
    dxc_b, dwa_b, dwx_b, dba_b, dbx_b, dlam_b = lru_bwd(xc, dhs, hb, wa_bd[1], wx_bd[1], ba_f[1:2], bx_f[1:2],
                                                        lam_f[1:2], True)
    dx2, dzz, h2, sums2b, dcw, dcb = rnn_in_bwd(dxc_f, dxc_b, xr, cw_f, dgt, x2, mods, 2, rin_t, dres2)
    g_rin_t = wgrad(dzz, h2, "wgrad_rnn_in")
    d_wa = jnp.stack([_diag_blocks(dwa_f), _diag_blocks(dwa_b)])
    d_wx = jnp.stack([_diag_blocks(dwx_f), _diag_blocks(dwx_b)])
    nflat = d_wa.size // N_DEV
    gates = jnp.concatenate([d_wa.reshape(N_DEV, nflat), d_wx.reshape(N_DEV, nflat)], axis=1)
    gates = gates.reshape(N_DEV, 2 * nflat // 128, 128)
    dx1, da0, h1, dy1, sums1, p_rin_t, p_gates = mlp_bwd(dx2, x1, y1, ra0, mods, 1, w1t_0, w2_0, lng_f[1:2],
                                                         rider=_AllToAll([g_rin_t, gates]))
    gates_sum = part_sum(p_gates, "part_sum_gates")
    g_w1t_0 = wgrad(da0, h1, "wgrad_w1_0")
    g_w2_0, p_w1t_0 = wgrad(r0, dy1, "wgrad_w2_0", rider=_AllToAll([g_w1t_0]))
    dres0, dy0, sums0a, do = post_bwd(dx1, x0, y0, mods, 0, wout, lng_f[0:1])
    g_wout = wgrad(o, dy0, "wgrad_attn_out")
    dq, dk, dv, dsink, wag, p_w2_0, p_wout = attn_bwd(
        q, kk, v, do, attn_sinks, rider=_Multi(_Gather([gates_sum]), _AllToAll([g_w2_0, g_wout])))
    dx0, dqkv, h0, sums0b = attn_in_bwd(dq, dk, dv, rope, x0, mods, 0, win_t, dres0)
    g_win_t = wgrad(dqkv, h0, "wgrad_attn_in")

    sums = [sums0a + sums0b, sums1, sums2a + sums2b, sums3]
    gmod = jnp.stack([t[0:3] for t in sums])
    gsend = jnp.moveaxis(gmod.reshape(4, N_DEV, CG), 1, 0)
    gsend = jnp.pad(gsend, ((0, 0), (0, 4), (0, 0)))
    c_t = c_all[:, 0, :].T
    sq_err = jnp.sum(sums3[5]).reshape(1, 1)
    tail = jnp.concatenate([
        _cols(dcw, 192), _cols(dcb, 192),
        _cols(jnp.concatenate([dba_f, dba_b]), 192), _cols(jnp.concatenate([dbx_f, dbx_b]), 192),
        _cols(jnp.concatenate([dlam_f, dlam_b]), 192),
        _cols(jnp.stack([t[3] for t in sums]), 128), _cols(jnp.stack([t[4] for t in sums]), 128),
        jnp.broadcast_to(dsink[:, 0:8], (N_DEV, 8)), jnp.broadcast_to(sq_err, (N_DEV, 1))], axis=1)
    tail = jnp.pad(tail, ((0, 0), (0, 32 * 128 - tail.shape[1]))).reshape(N_DEV, 32, 128)
    g_ada_w, g_ada_b, red, p_win_t = epilogue(gsend, c_t, tail, _AllToAll([g_win_t]))
    grads = {"ada_w": g_ada_w.reshape(ada_w.shape), "ada_b": g_ada_b[0:4].reshape(ada_b.shape)}

    big_parts = [p_win_t, p_wout, p_rin_t, p_rout, p_w1t_0, p_w2_0, p_w1t_1, p_w2_1]
    gsum = [part_sum(p, "part_sum_%d" % i) for i, p in enumerate(big_parts)]
    grads.update({
        "attn_w_in": gsum[0].T[None], "attn_w_out": gsum[1][None],
        "rnn_w_in": gsum[2].T[None], "rnn_w_out": gsum[3][None],
        "mlp_w1": jnp.stack([gsum[4].T, gsum[6].T]), "mlp_w2": jnp.stack([gsum[5], gsum[7]]),
    })
    wag = wag.reshape(N_DEV, 2 * nflat)
    grads["rnn_w_a"] = wag[:, :nflat].reshape(rnn_w_a.shape)
    grads["rnn_w_x"] = wag[:, nflat:].reshape(rnn_w_x.shape)
    tl = red.reshape(-1)
    loss = 0.5 * tl[3144] / D_MODEL
    grads["rnn_conv_w"] = tl[0:768].reshape(rnn_conv_w.shape)
    grads["rnn_conv_b"] = tl[768:960].reshape(rnn_conv_b.shape)
    grads["rnn_b_a"] = tl[960:1344].reshape(rnn_b_a.shape)
    grads["rnn_b_x"] = tl[1344:1728].reshape(rnn_b_x.shape)
    grads["rnn_lam"] = tl[1728:2112].reshape(rnn_lam.shape)
    grads["ln_g"] = tl[2112:2624].reshape(ln_g.shape)
    grads["ln_b"] = tl[2624:3136].reshape(ln_b.shape)
    grads["attn_sinks"] = tl[3136:3144].reshape(attn_sinks.shape)

    delta, new_m, new_v = {}, {}, {}
    for n in names:
        delta[n], new_m[n], new_v[n] = adamw(weights[n], grads[n], moments_m[n], moments_v[n], "adamw_" + n)
    return (loss, dx0.reshape(x.shape), *[grads[n] for n in names], *[delta[n] for n in names],
            *[new_m[n] for n in names], *[new_v[n] for n in names])
```

```python
import functools
import math

import jax
import jax.numpy as jnp
from jax import lax
from jax.experimental import pallas as pl
from jax.experimental.pallas import tpu as pltpu

F32, BF16 = jnp.float32, jnp.bfloat16
MESH = pl.DeviceIdType.MESH

D_MODEL = 1024
N_Q, N_KV, HEAD = 8, 2, 128
ROT, THETA = 32, 500000.0
QBLK = 128
D_QKV = (N_Q + 2 * N_KV) * HEAD
D_RNN, N_RB, RB_W = 1536, 16, 96
CG = 384
N_CG = D_RNN // CG
D_FF = 4096
FF_CHUNK = 1024
DEPTH = 2
ALPHA = (2.0 * DEPTH) ** 0.25
LN_EPS = 1e-5
LRU_C = 8.0
N_DEV = 8
LR, B1, B2, ADAM_EPS, WD, STEP = 0.001, 0.9, 0.999, 1e-8, 0.01, 10

VMEM_LIMIT = 56 * 1024 * 1024
TM_MM = 512
TM_MLP = 256
TM_MLP_FWD = 512
TT_RNN = 2048
TK_WG = 2048


def _nn(a, b):
    return jnp.dot(a, b, preferred_element_type=F32)


def _nt(a, b):
    return lax.dot_general(a, b, (((1,), (1,)), ((), ())), preferred_element_type=F32)


def _tn(a, b):
    return lax.dot_general(a, b, (((0,), (0,)), ((), ())), preferred_element_type=F32)


def _blk(n, pref):
    t = min(n, pref)
    assert n % t == 0, (n, pref)
    return t


def _params(**kw):
    return pltpu.CompilerParams(vmem_limit_bytes=VMEM_LIMIT, **kw)


def _row(tm, w):
    return pl.BlockSpec((tm, w), lambda i: (i, 0))


def _res(shape):
    return pl.BlockSpec(shape, lambda i: (0,) * len(shape), pipeline_mode=pl.Buffered(1))


def _mod(mod_ref, k):
    return mod_ref[3 * k:3 * k + 1, :], mod_ref[3 * k + 1:3 * k + 2, :], mod_ref[3 * k + 2:3 * k + 3, :]


def _ln_stats(z):
    mu = jnp.mean(z, axis=-1, keepdims=True)
    zc = z - mu
    var = jnp.mean(zc * zc, axis=-1, keepdims=True)
    rstd = lax.rsqrt(var + LN_EPS)
    return zc * rstd, rstd


def _ln_bwd(dxo, xhat, rstd, g):
    dxh = dxo * g
    m1 = jnp.mean(dxh, axis=-1, keepdims=True)
    m2 = jnp.mean(dxh * xhat, axis=-1, keepdims=True)
    return rstd * (dxh - m1 - xhat * m2)


def _colsum(v):
    return jnp.sum(v, axis=0, keepdims=True)


def _sigmoid(v):
    return 0.5 * jnp.tanh(0.5 * v) + 0.5


def _gelu_parts(v):
    k = math.sqrt(2.0 / math.pi)
    u = k * (v + 0.044715 * v * v * v)
    t = jnp.tanh(u)
    g = 0.5 * v * (1.0 + t)
    dg = 0.5 * (1.0 + t) + 0.5 * v * (1.0 - t * t) * k * (1.0 + 3.0 * 0.044715 * v * v)
    return g, dg


def _me():
    return lax.axis_index("x"), lax.axis_index("y"), lax.axis_index("c")


def _idx(p):
    return 4 * p[0] + 2 * p[1] + p[2]


def _peers(me):
    x, y, c = me
    out = []
    for k in range(1, N_DEV):
        out.append((1 - x if k & 4 else x, 1 - y if k & 2 else y, 1 - c if k & 1 else c))
    return out


class _Gather:
    def __init__(self, srcs):
        self.srcs = list(srcs)
        n = len(self.srcs)
        self.out_shape = [jax.ShapeDtypeStruct((N_DEV,) + s.shape, s.dtype) for s in self.srcs]
        self.scratch = [pltpu.SemaphoreType.DMA((n, 7)), pltpu.SemaphoreType.DMA((n, 7)),
                        pltpu.SemaphoreType.DMA((n,))]

    @staticmethod
    def _places():
        x, y, c = me = _me()
        return me, (x, y, 1 - c), [(1 - x, y), (x, 1 - y), (1 - x, 1 - y)]

    @staticmethod
    def _copy(outs, sems, t, k, block, to, src=None):
        slot = outs[t].at[_idx(block)]
        return pltpu.make_async_remote_copy(
            src_ref=slot if src is None else src, dst_ref=slot, send_sem=sems[0].at[t, k],
            recv_sem=sems[1].at[t, k], device_id=to, device_id_type=MESH)

    def _firsts(self, ins, outs, sems):
        me, sibling, chips = self._places()
        out = []
        for t in range(len(ins)):
            out.append(self._copy(outs, sems, t, 0, me, sibling, src=ins[t]))
            out += [self._copy(outs, sems, t, 1 + j, me, (*chip, me[2]), src=ins[t]) for j, chip in enumerate(chips)]
        return out

    def _locals(self, ins, outs, sems):
        me = _me()
        return [pltpu.make_async_copy(ins[t], outs[t].at[_idx(me)], sems[2].at[t]) for t in range(len(ins))]

    def start(self, ins, outs, sems):
        for cp in self._locals(ins, outs, sems) + self._firsts(ins, outs, sems):
            cp.start()

    def mid(self, ins, outs, sems):
        me, sibling, chips = self._places()
        for j, chip in enumerate(chips):
            for t in range(len(ins)):
                self._copy(outs, sems, t, 1 + j, (*chip, me[2]), me).wait_recv()
                self._copy(outs, sems, t, 4 + j, (*chip, me[2]), sibling).start()

    def finish(self, ins, outs, sems):
        me, sibling, chips = self._places()
        for t in range(len(ins)):
            self._copy(outs, sems, t, 0, sibling, me).wait_recv()
            for j, chip in enumerate(chips):
                self._copy(outs, sems, t, 4 + j, (*chip, 1 - me[2]), me).wait_recv()
        for cp in self._firsts(ins, outs, sems):
            cp.wait_send()
        for j, chip in enumerate(chips):
            for t in range(len(ins)):
                self._copy(outs, sems, t, 4 + j, (*chip, me[2]), sibling).wait_send()
        for cp in self._locals(ins, outs, sems):
            cp.wait()


class _AllToAll:
    def __init__(self, srcs):
        self.srcs = list(srcs)
        n = len(self.srcs)
        self.out_shape = [jax.ShapeDtypeStruct(s.shape, s.dtype) for s in self.srcs]
        self.scratch = [pltpu.SemaphoreType.DMA((n, 7)), pltpu.SemaphoreType.DMA((n, 7)),
                        pltpu.SemaphoreType.DMA((n,))]

    def _copies(self, ins, outs, sems):
        me = _me()
        loc, rem = [], []
        for t in range(len(ins)):
            loc.append(pltpu.make_async_copy(ins[t].at[_idx(me)], outs[t].at[_idx(me)], sems[2].at[t]))
            for k, p in enumerate(_peers(me)):
                rem.append(pltpu.make_async_remote_copy(
                    src_ref=ins[t].at[_idx(p)], dst_ref=outs[t].at[_idx(me)], send_sem=sems[0].at[t, k],
                    recv_sem=sems[1].at[t, k], device_id=p, device_id_type=MESH))
        return loc, rem

    def start(self, ins, outs, sems):
        loc, rem = self._copies(ins, outs, sems)
        for cp in loc + rem:
            cp.start()

    def mid(self, ins, outs, sems):
        pass

    def finish(self, ins, outs, sems):
        me = _me()
        for t in range(len(ins)):
            for k, p in enumerate(_peers(me)):
                slot = outs[t].at[_idx(p)]
                pltpu.make_async_remote_copy(
                    src_ref=slot, dst_ref=slot, send_sem=sems[0].at[t, k], recv_sem=sems[1].at[t, k],
                    device_id=p, device_id_type=MESH).wait_recv()
        loc, rem = self._copies(ins, outs, sems)
        for cp in rem:
            cp.wait_send()
        for cp in loc:
            cp.wait()


class _Multi:
    def __init__(self, *exs):
        self.exs = exs
        self.srcs = [s for e in exs for s in e.srcs]
        self.out_shape = [s for e in exs for s in e.out_shape]
        self.scratch = [s for e in exs for s in e.scratch]

    def _each(self, ins, outs, sems):
        i = j = 0
        for e in self.exs:
            n, m = len(e.srcs), len(e.scratch)
            yield e, ins[i:i + n], outs[i:i + n], sems[j:j + m]
            i, j = i + n, j + m

    def start(self, ins, outs, sems):
        for e, a, b, c in self._each(ins, outs, sems):
            e.start(a, b, c)

    def mid(self, ins, outs, sems):
        for e, a, b, c in self._each(ins, outs, sems):
            e.mid(a, b, c)

    def finish(self, ins, outs, sems):
        for e, a, b, c in self._each(ins, outs, sems):
            e.finish(a, b, c)


def _call(body, *, name, grid, in_specs, out_specs, out_shape, args, scratch_shapes=(), rider=None):
    in_specs, out_specs, out_shape = list(in_specs), list(out_specs), list(out_shape)
    scratch_shapes = list(scratch_shapes)
    if rider is None:
        return pl.pallas_call(body, name=name, grid=grid, out_shape=out_shape, in_specs=in_specs,
                              out_specs=out_specs, scratch_shapes=scratch_shapes, compiler_params=_params())(*args)
    nci, nco, ncs, nr = len(in_specs), len(out_shape), len(scratch_shapes), len(rider.srcs)
    nsteps = math.prod(grid)
    assert nsteps >= 2, (name, grid)
    mid = max(1, (7 * nsteps) // 8)

    def full(*refs):
        ci, ri = refs[:nci], refs[nci:nci + nr]
        co, ro = refs[nci + nr:nci + nr + nco], refs[nci + nr + nco:nci + 2 * nr + nco]
        cs, rs = refs[nci + 2 * nr + nco:nci + 2 * nr + nco + ncs], refs[nci + 2 * nr + nco + ncs:]
        step = pl.program_id(0)
        for d in range(1, len(grid)):
            step = step * grid[d] + pl.program_id(d)

        @pl.when(step == 0)
        def _():
            rider.start(ri, ro, rs)

        @pl.when(step == mid)
        def _():
            rider.mid(ri, ro, rs)

        body(*ci, *co, *cs)

        @pl.when(step == nsteps - 1)
        def _():
            rider.finish(ri, ro, rs)

    any_spec = pl.BlockSpec(memory_space=pl.ANY)
    return pl.pallas_call(
        full, name=name, grid=grid, out_shape=out_shape + rider.out_shape,
        in_specs=in_specs + [any_spec] * nr, out_specs=out_specs + [any_spec] * nr,
        scratch_shapes=scratch_shapes + rider.scratch, compiler_params=_params(),
    )(*args, *rider.srcs)


def _a2a_start(srcs, dsts, send_sems, recv_sems, local_sems, me, sem_base=0):
    peers = _peers(me)
    started = []
    for t in range(len(srcs)):
        loc = pltpu.make_async_copy(srcs[t].at[_idx(me)], dsts[t].at[_idx(me)], local_sems.at[sem_base + t])
        loc.start()
        started.append(("local", loc))
        for k, p in enumerate(peers):
            cp = pltpu.make_async_remote_copy(
                src_ref=srcs[t].at[_idx(p)], dst_ref=dsts[t].at[_idx(me)],
                send_sem=send_sems.at[sem_base + t, k], recv_sem=recv_sems.at[sem_base + t, k],
                device_id=p, device_id_type=MESH)
            cp.start()
            started.append(("remote", cp))
    return started


def _a2a_finish(started, dsts, send_sems, recv_sems, me, sem_base=0):
    peers = _peers(me)
    for t in range(len(dsts)):
        for k, p in enumerate(peers):
            slot = dsts[t].at[_idx(p)]
            pltpu.make_async_remote_copy(
                src_ref=slot, dst_ref=slot, send_sem=send_sems.at[sem_base + t, k],
                recv_sem=recv_sems.at[sem_base + t, k], device_id=p, device_id_type=MESH).wait_recv()
    for kind, cp in started:
        if kind == "local":
            cp.wait()
        else:
            cp.wait_send()


def ada_modulation(c8, ada_w, ada_b, ride):
    nr = len(ride.srcs)

    def body(c_ref, w_ref, b_ref, *rest):
        ride_in, (call_ref, modr_ref), ride_out = rest[:nr], rest[nr:nr + 2], rest[nr + 2:2 * nr + 2]
        modp, send_sems, recv_sems, local_sems = rest[2 * nr + 2:2 * nr + 6]
        ride_sems = rest[2 * nr + 6:]
        ride.start(ride_in, ride_out, ride_sems)
        me = _me()
        peers = _peers(me)
        sends = []
        for k, p in enumerate(peers):
            cp = pltpu.make_async_remote_copy(
                src_ref=c_ref, dst_ref=call_ref.at[_idx(me)], send_sem=send_sems.at[0, k],
                recv_sem=recv_sems.at[0, k], device_id=p, device_id_type=MESH)
            cp.start()
            sends.append(cp)
        call_ref[_idx(me)] = c_ref[...]
        for k, p in enumerate(peers):
            slot = call_ref.at[_idx(p)]
            pltpu.make_async_remote_copy(
                src_ref=slot, dst_ref=slot, send_sem=send_sems.at[0, k], recv_sem=recv_sems.at[0, k],
                device_id=p, device_id_type=MESH).wait_recv()
        for cp in sends:
            cp.wait_send()
        cv = call_ref[...].reshape(N_DEV * 8, D_MODEL)
        s = (cv * _sigmoid(cv)).astype(BF16)
        for k in range(4):
            res = _nn(s, w_ref[k].astype(BF16)) + b_ref[k]
            for j in range(N_DEV):
                modp[j, 8 * k:8 * k + 8, :] = res[8 * j:8 * j + 8, :]
        started = _a2a_start([modp], [modr_ref], send_sems, recv_sems, local_sems, me, sem_base=1)
        _a2a_finish(started, [modr_ref], send_sems, recv_sems, me, sem_base=1)
        ride.mid(ride_in, ride_out, ride_sems)
        ride.finish(ride_in, ride_out, ride_sems)

    vm, hbm = pl.BlockSpec(memory_space=pltpu.VMEM), pl.BlockSpec(memory_space=pl.ANY)
    return pl.pallas_call(
        body, name="ada_modulation",
        out_shape=[jax.ShapeDtypeStruct((N_DEV, 8, D_MODEL), F32), jax.ShapeDtypeStruct((N_DEV, 32, CG), F32)]
        + ride.out_shape,
        in_specs=[vm, vm, vm] + [hbm] * nr, out_specs=[vm, vm] + [hbm] * nr,
        scratch_shapes=[pltpu.VMEM((N_DEV, 32, CG), F32), pltpu.SemaphoreType.DMA((2, 7)),
                        pltpu.SemaphoreType.DMA((2, 7)), pltpu.SemaphoreType.DMA((2,))] + ride.scratch,
        compiler_params=_params(),
    )(c8, ada_w, ada_b, *ride.srcs)


def epilogue(gsend, c_t, tail, ride):
    nr = len(ride.srcs)
    rt = tail.shape[1]

    def body(g_ref, ct_ref, t_ref, *rest):
        ride_in, (gw_ref, gb_ref, red_ref), ride_out = rest[:nr], rest[nr:nr + 3], rest[nr + 3:2 * nr + 3]
        grecv, trecv, send_sems, recv_sems, local_sems = rest[2 * nr + 3:2 * nr + 8]
        ride_sems = rest[2 * nr + 8:]
        ride.start(ride_in, ride_out, ride_sems)
        me = _me()
        started = _a2a_start([g_ref, t_ref], [grecv, trecv], send_sems, recv_sems, local_sems, me)
        _a2a_finish(started, [grecv, trecv], send_sems, recv_sems, me)
        acc = trecv[0]
        for j in range(1, N_DEV):
            acc = acc + trecv[j]
        red_ref[...] = acc
        ct = ct_ref[...]
        st = (ct * _sigmoid(ct)).astype(BF16).astype(F32)
        gb = jnp.zeros((8, CG), F32)
        for b in range(N_DEV):
            gb = gb + grecv[b]
        gb_ref[...] = gb
        for k in range(4):
            acc = jnp.zeros((D_MODEL, CG), F32)
            for b in range(N_DEV):
                row = grecv[b, k:k + 1, :].astype(BF16).astype(F32)
                acc = acc + st[:, b:b + 1] * row
            gw_ref[k] = acc
        ride.mid(ride_in, ride_out, ride_sems)
        ride.finish(ride_in, ride_out, ride_sems)

    vm, hbm = pl.BlockSpec(memory_space=pltpu.VMEM), pl.BlockSpec(memory_space=pl.ANY)
    return pl.pallas_call(
        body, name="epilogue",
        out_shape=[jax.ShapeDtypeStruct((4, D_MODEL, CG), F32), jax.ShapeDtypeStruct((8, CG), F32),
                   jax.ShapeDtypeStruct((rt, 128), F32)] + ride.out_shape,
        in_specs=[vm, vm, vm] + [hbm] * nr, out_specs=[vm, vm, vm] + [hbm] * nr,
        scratch_shapes=[pltpu.VMEM((N_DEV, 8, CG), F32), pltpu.VMEM((N_DEV, rt, 128), F32),
                        pltpu.SemaphoreType.DMA((2, 7)), pltpu.SemaphoreType.DMA((2, 7)),
                        pltpu.SemaphoreType.DMA((2,))] + ride.scratch,
        compiler_params=_params(),
    )(gsend, c_t, tail, *ride.srcs)


def _rope(t, cos, s1, s2):
    return t * cos + pltpu.roll(t, 16, 1) * s1 + pltpu.roll(t, HEAD - 16, 1) * s2


def _rope_bwd(d, cos, s1, s2):
    return d * cos + pltpu.roll(d * s1, HEAD - 16, 1) + pltpu.roll(d * s2, 16, 1)


def attn_in_fwd(x, mods, k, win_t, rope, rider=None):
    s = x.shape[0]
    tm = _blk(s, TM_MM)

    def body(x_ref, mod_ref, w_ref, c_ref, s1_ref, s2_ref, q_ref, k_ref, v_ref):
        shift, scale, _ = _mod(mod_ref, k)
        h = (x_ref[...] * (1.0 + scale) + shift).astype(BF16)
        qkv = _nt(h, w_ref[...])
        cos, s1, s2 = c_ref[...], s1_ref[...], s2_ref[...]
        for hh in range(N_Q + N_KV):
            r = _rope(qkv[:, HEAD * hh:HEAD * (hh + 1)], cos, s1, s2).astype(BF16)
            if hh < N_Q:
                q_ref[:, HEAD * hh:HEAD * (hh + 1)] = r
            else:
                k_ref[:, HEAD * (hh - N_Q):HEAD * (hh - N_Q + 1)] = r
        v_ref[...] = qkv[:, HEAD * (N_Q + N_KV):].astype(BF16)

    return _call(
        body, name="attn_in_fwd", grid=(s // tm,),
        out_shape=[jax.ShapeDtypeStruct((s, N_Q * HEAD), BF16), jax.ShapeDtypeStruct((s, N_KV * HEAD), BF16),
                   jax.ShapeDtypeStruct((s, N_KV * HEAD), BF16)],
        in_specs=[_row(tm, D_MODEL), _res(mods.shape), _res(win_t.shape),
                  _row(tm, HEAD), _row(tm, HEAD), _row(tm, HEAD)],
        out_specs=[_row(tm, N_Q * HEAD), _row(tm, N_KV * HEAD), _row(tm, N_KV * HEAD)],
        args=(x, mods, win_t, *rope), rider=rider)


QPAIR = 8


def _kv_specs(nblk):
    w = N_KV * HEAD
    return [pl.BlockSpec((QBLK, w), lambda n: (jnp.maximum(QPAIR * n - 1, 0), 0)),
            pl.BlockSpec((QPAIR * QBLK, w), lambda n: (n, 0)),
            pl.BlockSpec((QBLK, w), lambda n: (jnp.minimum(QPAIR * (n + 1), nblk - 1), 0))]


GROUP = N_Q // N_KV


def _attn_mask(n, s):
    qi = lax.broadcasted_iota(jnp.int32, (GROUP * QBLK, 3 * QBLK), 0) & (QBLK - 1)
    kj = lax.broadcasted_iota(jnp.int32, (GROUP * QBLK, 3 * QBLK), 1)
    rel = kj - QBLK - qi
    kpos = kj + (n - 1) * QBLK
    return (jnp.abs(rel) <= QBLK) & (kpos >= 0) & (kpos < s)


def _stack_heads(ref, qb, kv):
    rows = slice(QBLK * qb, QBLK * (qb + 1))
    return jnp.concatenate([ref[rows, HEAD * (GROUP * kv + j):HEAD * (GROUP * kv + j + 1)] for j in range(GROUP)],
                           axis=0)


def _stack_sinks(sink_ref, kv):
    row = lax.broadcasted_iota(jnp.int32, (GROUP * QBLK, 1), 0)
    out = jnp.full((GROUP * QBLK, 1), sink_ref[0, GROUP * kv + GROUP - 1], F32)
    for j in range(GROUP - 2, -1, -1):
        out = jnp.where(row < QBLK * (j + 1), sink_ref[0, GROUP * kv + j], out)
    return out


def _attn_probs(qh, kh, valid, sink):
    sc = _nt(qh, kh) * (HEAD ** -0.5)
    sc = jnp.where(valid, sc, -1e30)
    m = jnp.maximum(jnp.max(sc, axis=-1, keepdims=True), sink)
    p = jnp.exp(sc - m)
    es = jnp.exp(sink - m)
    denom = jnp.sum(p, axis=-1, keepdims=True) + es
    return p / denom, es / denom


def attn_fwd(q, kk, v, sinks, rider=None):
    s = q.shape[0]
    nblk = s // QBLK

    def body(sink_ref, q_ref, kp, ko, kn, vp, vo, vn, o_ref):
        n = pl.program_id(0)
        kall = jnp.concatenate([kp[...], ko[...], kn[...]], axis=0)
        vall = jnp.concatenate([vp[...], vo[...], vn[...]], axis=0)
        for qb in range(QPAIR):
            valid = _attn_mask(QPAIR * n + qb, s)
            keys = slice(QBLK * qb, QBLK * (qb + 3))
            for kv in range(N_KV):
                cols = slice(HEAD * kv, HEAD * (kv + 1))
                probs, _ = _attn_probs(_stack_heads(q_ref, qb, kv), kall[keys, cols], valid,
                                       _stack_sinks(sink_ref, kv))
                og = _nn(probs.astype(BF16), vall[keys, cols]).astype(BF16)
                for j in range(GROUP):
                    hq = GROUP * kv + j
                    o_ref[QBLK * qb:QBLK * (qb + 1), HEAD * hq:HEAD * (hq + 1)] = og[QBLK * j:QBLK * (j + 1), :]

    qspec = pl.BlockSpec((QPAIR * QBLK, N_Q * HEAD), lambda n: (n, 0))
    return _call(
        body, name="attn_fwd", grid=(nblk // QPAIR,),
        out_shape=[jax.ShapeDtypeStruct((s, N_Q * HEAD), BF16)],
        in_specs=[pl.BlockSpec(memory_space=pltpu.SMEM), qspec] + _kv_specs(nblk) + _kv_specs(nblk),
        out_specs=[qspec],
        args=(sinks, q, kk, kk, kk, v, v, v), rider=rider)


def post_fwd(ypre, w, x, mods, k, lng, lnb, gate_act=None, rider=None):
    s = x.shape[0]
    tm = _blk(s, TM_MM)
    kdim = w.shape[0]
    rnn = gate_act is not None

    def body(*refs):
        if rnn:
            gt_ref, hs_ref, w_ref, x_ref, mod_ref, g_ref, b_ref, xo_ref, y_ref, yp_ref = refs
            act, _ = _gelu_parts(gt_ref[...].astype(F32))
            yp = (hs_ref[...].astype(F32) * act).astype(BF16)
            yp_ref[...] = yp
        else:
            yp_ref, w_ref, x_ref, mod_ref, g_ref, b_ref, xo_ref, y_ref = refs
            yp = yp_ref[...]
        _, _, gate = _mod(mod_ref, k)
        y = _nn(yp, w_ref[...])
        y_ref[...] = y
        xhat, _ = _ln_stats(ALPHA * x_ref[...] + (1.0 + gate) * y)
        xo_ref[...] = xhat * g_ref[...] + b_ref[...]

    act_in = list(gate_act) if rnn else [ypre]
    out_shape = [jax.ShapeDtypeStruct((s, D_MODEL), F32), jax.ShapeDtypeStruct((s, D_MODEL), F32)]
    out_specs = [_row(tm, D_MODEL), _row(tm, D_MODEL)]
    if rnn:
        out_shape.append(jax.ShapeDtypeStruct((s, kdim), BF16))
        out_specs.append(_row(tm, kdim))
    return _call(
        body, name="rnn_post_fwd" if rnn else "attn_post_fwd", grid=(s // tm,),
        out_shape=out_shape,
        in_specs=[_row(tm, kdim)] * len(act_in) + [_res(w.shape), _row(tm, D_MODEL), _res(mods.shape),
                                                    _res(lng.shape), _res(lnb.shape)],
        out_specs=out_specs,
        args=(*act_in, w, x, mods, lng, lnb), rider=rider)


def mlp_fwd(x, mods, k, w1_t, w2, lng, lnb, rider=None, last=False):
    s = x.shape[0]
    tm = _blk(s, TM_MLP_FWD)

    def body(x_ref, mod_ref, w1_ref, w2_ref, g_ref, b_ref, *outs):
        xo_ref = None if last else outs[0]
        y_ref, ra_ref, r_ref = outs[-3:]
        xv = x_ref[...]
        shift, scale, gate = _mod(mod_ref, k)
        h = (xv * (1.0 + scale) + shift).astype(BF16)
        y = jnp.zeros((tm, D_MODEL), F32)
        for c in range(D_FF // FF_CHUNK):
            rows = slice(FF_CHUNK * c, FF_CHUNK * (c + 1))
            a = jnp.maximum(_nt(h, w1_ref[rows, :]), 0.0)
            r = (a * a).astype(BF16)
            ra_ref[:, rows] = a.astype(BF16)
            r_ref[:, rows] = r
            y = y + _nn(r, w2_ref[rows, :])
        y_ref[...] = y
        if not last:
            xhat, _ = _ln_stats(ALPHA * xv + (1.0 + gate) * y)
            xo_ref[...] = xhat * g_ref[...] + b_ref[...]

    nf = 1 if last else 2
    return _call(
        body, name="mlp_fwd_last" if last else "mlp_fwd", grid=(s // tm,),
        out_shape=[jax.ShapeDtypeStruct((s, D_MODEL), F32)] * nf + [jax.ShapeDtypeStruct((s, D_FF), BF16)] * 2,
        in_specs=[_row(tm, D_MODEL), _res(mods.shape), _res(w1_t.shape), _res(w2.shape),
                  _res(lng.shape), _res(lnb.shape)],
        out_specs=[_row(tm, D_MODEL)] * nf + [_row(tm, D_FF)] * 2,
        args=(x, mods, w1_t, w2, lng, lnb), rider=rider)


def rnn_in_fwd(x, mods, k, win_t):
    s = x.shape[0]
    tm = _blk(s, TM_MM)

    def body(x_ref, mod_ref, w_ref, xr_ref, gt_ref):
        shift, scale, _ = _mod(mod_ref, k)
        h = (x_ref[...] * (1.0 + scale) + shift).astype(BF16)
        xr_ref[...] = _nt(h, w_ref[0:D_RNN, :])
        gt_ref[...] = _nt(h, w_ref[D_RNN:2 * D_RNN, :]).astype(BF16)

    return pl.pallas_call(
        body, name="rnn_in_fwd", grid=(s // tm,),
        out_shape=[jax.ShapeDtypeStruct((s, D_RNN), F32), jax.ShapeDtypeStruct((s, D_RNN), BF16)],
        in_specs=[_row(tm, D_MODEL), _res(mods.shape), _res(win_t.shape)],
        out_specs=[_row(tm, D_RNN)] * 2,
        compiler_params=_params(),
    )(x, mods, win_t)


def _shift_rows(v, k, row):
    n = v.shape[0]
    r = pltpu.roll(v, k % n, 0)
    keep = (row >= k) if k > 0 else (row < n + k)
    return jnp.where(keep, r, 0.0)


def conv_fwd(xr, cw, cb):
    s = xr.shape[0]

    def body(x_ref, w_ref, b_ref, o_ref):
        xv = x_ref[...]
        row = lax.broadcasted_iota(jnp.int32, xv.shape, 0)
        o_ref[...] = (b_ref[...] + w_ref[0:1, :] * _shift_rows(xv, 2, row) + w_ref[1:2, :] * _shift_rows(xv, 1, row)
                      + w_ref[2:3, :] * xv + w_ref[3:4, :] * _shift_rows(xv, -1, row))

    slab = pl.BlockSpec((s, 128), lambda j: (0, j))
    return pl.pallas_call(
        body, name="conv_fwd", grid=(D_RNN // 128,),
        out_shape=jax.ShapeDtypeStruct((D_RNN // 128, s, 128), F32),
        in_specs=[slab, pl.BlockSpec((4, 128), lambda j: (0, j)), pl.BlockSpec((1, 128), lambda j: (0, j))],
        out_specs=pl.BlockSpec((None, s, 128), lambda j: (j, 0, 0)),
        compiler_params=_params(),
    )(xr, cw, cb)


def _softplus_neg(lam):
    z = -lam
    e = jnp.exp(-jnp.abs(z))
    u = 1.0 + e
    log1p = jnp.where(u == 1.0, e, jnp.log(u) * e / jnp.where(u == 1.0, 1.0, u - 1.0))
    return jnp.maximum(z, 0.0) + log1p, 1.0 / (1.0 + jnp.exp(lam))


def _lru_gates(xv, wa_ref, wx_ref, ba_ref, bx_ref, lam_ref):
    xb = xv.astype(BF16)
    r = _sigmoid(_nn(xb, wa_ref[...]) + ba_ref[...])
    i = _sigmoid(_nn(xb, wx_ref[...]) + bx_ref[...])
    sp, sg = _softplus_neg(lam_ref[...])
    la = r * (-LRU_C * sp)
    a = jnp.exp(la)
    th = jnp.tanh(la)
    m2 = -2.0 * th / (1.0 - th)
    rmult = lax.rsqrt(jnp.maximum(m2, 1e-37))
    return xb, r, i, sp, sg, a, m2 * rmult, rmult


SLABS = CG // 128
GRP, SEG = 32, 4


def _lru_specs(nt, tt, reverse):
    tmap = (lambda t: nt - 1 - t) if reverse else (lambda t: t)
    blk = pl.BlockSpec((tt, CG), lambda g, t: (tmap(t), g))
    slabs = pl.BlockSpec((SLABS, tt, 128), lambda g, t: (g, tmap(t), 0))
    wsp = pl.BlockSpec((None, CG, CG), lambda g, t: (g, 0, 0))
    vec = pl.BlockSpec((1, CG), lambda g, t: (0, g))
    return tmap, blk, slabs, wsp, vec


def _slab_rows(ref3):
    return jnp.concatenate([ref3[l] for l in range(ref3.shape[0])], axis=1)


def _perm_load(ref3, tt):
    out = []
    for l in range(SLABS):
        r = ref3.at[l]
        out.append(jnp.concatenate([r[pl.ds(GRP * g + i, 8, stride=SEG), :]
                                    for g in range(tt // GRP) for i in range(SEG)], axis=0))
    return jnp.concatenate(out, axis=1)


def _perm_scan(a, u, carry, reverse, emit):
    n, c = a.shape
    sub = lax.broadcasted_iota(jnp.int32, (8, c), 0)
    steps = [(8 - sh, sub < 8 - sh) if reverse else (sh, sub >= sh) for sh in (1, 2, 4)]
    order = range(SEG - 1, -1, -1) if reverse else range(SEG)
    for g in (range(n // GRP - 1, -1, -1) if reverse else range(n // GRP)):
        hs, ps = [None] * SEG, [None] * SEG
        h = p = None
        for i in order:
            rows = slice(GRP * g + 8 * i, GRP * g + 8 * i + 8)
            h = u[rows] if h is None else a[rows] * h + u[rows]
            p = a[rows] if p is None else a[rows] * p
            hs[i], ps[i] = h, p
        d, f = p, h
        for rot, keep in steps:
            d_s = jnp.where(keep, pltpu.roll(d, rot, 0), 1.0)
            f_s = jnp.where(keep, pltpu.roll(f, rot, 0), 0.0)
            f = d * f_s + f
            d = d * d_s
        end = f + d * carry
        if reverse:
            init = jnp.where(sub == 7, carry, pltpu.roll(end, 7, 0))
            carry = jnp.broadcast_to(end[0:1], (8, c))
        else:
            init = jnp.where(sub == 0, carry, pltpu.roll(end, 1, 0))
            carry = jnp.broadcast_to(end[7:8], (8, c))
        for i in range(SEG):
            emit(g, i, hs[i] + ps[i] * init)
    return carry


def lru_fwd(xc, wa, wx, ba, bx, lam, reverse, other=None):
    s = xc.shape[1]
    tt = _blk(s, TT_RNN)
    nt = s // tt

    def body(x_ref, wa_ref, wx_ref, ba_ref, bx_ref, lam_ref, *rest):
        if other is None:
            hs_ref, carry = rest
        else:
            oth_ref, hs_ref, sum_ref, carry = rest

        @pl.when(pl.program_id(1) == 0)
        def _():
            carry[...] = jnp.zeros(carry.shape, F32)

        xv = _perm_load(x_ref, tt)
        _, _, i, _, _, a, mult, _ = _lru_gates(xv, wa_ref, wx_ref, ba_ref, bx_ref, lam_ref)
        u = mult * (i * xv)

        def emit(g, j, rows):
            for l in range(SLABS):
                hs_ref.at[l][pl.ds(GRP * g + j, 8, stride=SEG), :] = rows[:, 128 * l:128 * (l + 1)]

        carry[...] = _perm_scan(a, u, carry[...], reverse, emit)
        if other is not None:
            for l in range(SLABS):
                sum_ref[:, 128 * l:128 * (l + 1)] = (hs_ref[l] + oth_ref[l]).astype(BF16)

    _, blk, slabs, wsp, vec = _lru_specs(nt, tt, reverse)
    extra = [] if other is None else [other]
    return pl.pallas_call(
        body, name="lru_fwd_rev" if reverse else "lru_fwd", grid=(N_CG, nt),
        out_shape=[jax.ShapeDtypeStruct(xc.shape, F32)] + [jax.ShapeDtypeStruct((s, D_RNN), BF16)] * len(extra),
        in_specs=[slabs, wsp, wsp, vec, vec, vec] + [slabs] * len(extra), out_specs=[slabs] + [blk] * len(extra),
        scratch_shapes=[pltpu.VMEM((8, CG), F32)],
        compiler_params=_params(),
    )(xc, wa, wx, ba, bx, lam, *extra)


def lru_bwd(xc, dhs, hs, wa, wx, ba, bx, lam, reverse, rider=None):
    s = xc.shape[1]
    tt = _blk(s, TT_RNN)
    nt = s // tt
    ng = tt // GRP
    back = not reverse

    def rows_of(v, g, i):
        return v[GRP * g + 8 * i:GRP * g + 8 * i + 8]

    def neighbour(v, past, edge, sub):
        out = []
        for g in range(ng):
            for i in range(SEG):
                if past and i > 0:
                    r = rows_of(v, g, i - 1)
                elif past:
                    e = edge if g == 0 else rows_of(v, g - 1, SEG - 1)[7:8]
                    r = jnp.where(sub == 0, e, pltpu.roll(rows_of(v, g, SEG - 1), 1, 0))
                elif i < SEG - 1:
                    r = rows_of(v, g, i + 1)
                else:
                    e = edge if g == ng - 1 else rows_of(v, g + 1, 0)[0:1]
                    r = jnp.where(sub == 7, e, pltpu.roll(rows_of(v, g, 0), 7, 0))
                out.append(r)
        return jnp.concatenate(out, axis=0)

    def body(x_ref, dh_ref, hs_ref, nb_ref, wa_ref, wx_ref, ba_ref, bx_ref, lam_ref,
             dx_ref, dwa_ref, dwx_ref, dba_ref, dbx_ref, dlam_ref, carry):
        t = pl.program_id(1)

        @pl.when(t == 0)
        def _():
            carry[...] = jnp.zeros(carry.shape, F32)
            dwa_ref[...] = jnp.zeros(dwa_ref.shape, F32)
            dwx_ref[...] = jnp.zeros(dwx_ref.shape, F32)
            dba_ref[...] = jnp.zeros(dba_ref.shape, F32)
            dbx_ref[...] = jnp.zeros(dbx_ref.shape, F32)
            dlam_ref[...] = jnp.zeros(dlam_ref.shape, F32)

        xv = _perm_load(x_ref, tt)
        xb, r, i, sp, sg, a, mult, rmult = _lru_gates(xv, wa_ref, wx_ref, ba_ref, bx_ref, lam_ref)
        sub = lax.broadcasted_iota(jnp.int32, (8, CG), 0)
        hsv = _perm_load(hs_ref, tt)
        nbv = _slab_rows(nb_ref)
        inner = t < nt - 1
        h_edge = jnp.where(inner, nbv[0:1, :] if reverse else nbv[7:8, :], 0.0)
        hprev = neighbour(hsv, not reverse, h_edge, sub)
        a_next = neighbour(a, reverse, carry[8:9, :], sub)
        dhv = _perm_load(dh_ref, tt)
        gl = [None] * (ng * SEG)

        def emit(gi, j, rows):
            gl[gi * SEG + j] = rows

        carry[0:8, :] = _perm_scan(a_next, dhv, carry[0:8, :], back, emit)
        g = jnp.concatenate(gl, axis=0)
        carry[8:9, :] = a[0:1, :] if back else a[tt - 1:tt, :]

        da = g * hprev
        dmult = g * (i * xv)
        di = g * mult * xv
        dla = da * a - dmult * (a * a) * rmult
        dpa = (dla * (-LRU_C * sp)) * r * (1.0 - r)
        dpx = di * i * (1.0 - i)
        dlam_ref[...] += _colsum(dla * (LRU_C * r * sg))
        dba_ref[...] += _colsum(dpa)
        dbx_ref[...] += _colsum(dpx)
        dpab, dpxb = dpa.astype(BF16), dpx.astype(BF16)
        dxv = g * mult * i + _nt(dpab, wa_ref[...]) + _nt(dpxb, wx_ref[...])
        for gi in range(ng):
            for j in range(SEG):
                for l in range(SLABS):
                    dx_ref.at[l][pl.ds(GRP * gi + j, 8, stride=SEG), :] = rows_of(dxv, gi, j)[:, 128 * l:128 * (l + 1)]
        dwa_ref[...] += _tn(xb, dpab)
        dwx_ref[...] += _tn(xb, dpxb)

    tmap, blk, slabs, wsp, vec = _lru_specs(nt, tt, back)
    per8 = tt // 8
    if reverse:
        nb = pl.BlockSpec((SLABS, 8, 128), lambda g, t: (g, jnp.minimum((tmap(t) + 1) * per8, s // 8 - 1), 0))
    else:
        nb = pl.BlockSpec((SLABS, 8, 128), lambda g, t: (g, jnp.maximum(tmap(t) * per8 - 1, 0), 0))
    return _call(
        body, name="lru_bwd_rev" if reverse else "lru_bwd", grid=(N_CG, nt),
        out_shape=[jax.ShapeDtypeStruct(xc.shape, F32), jax.ShapeDtypeStruct((N_CG, CG, CG), F32),
                   jax.ShapeDtypeStruct((N_CG, CG, CG), F32)] + [jax.ShapeDtypeStruct((1, D_RNN), F32)] * 3,
        in_specs=[slabs, slabs, slabs, nb, wsp, wsp, vec, vec, vec],
        out_specs=[slabs, wsp, wsp, vec, vec, vec],
        scratch_shapes=[pltpu.VMEM((16, CG), F32)],
        args=(xc, dhs, hs, hs, wa, wx, ba, bx, lam), rider=rider)


def _ln_part_bwd(dxo, x, y, gate, g, sums_ref, loss_head=None):
    xhat, rstd = _ln_stats(ALPHA * x + (1.0 + gate) * y)
    if loss_head is not None:
        err = xhat * g + loss_head[0] - loss_head[1]
        dxo = err * (1.0 / D_MODEL)
        sums_ref[5:6, :] += _colsum(err * err)
    dz = _ln_bwd(dxo, xhat, rstd, g)
    sums_ref[2:3, :] += _colsum(dz * y)
    sums_ref[3:4, :] += _colsum(dxo * xhat)
    sums_ref[4:5, :] += _colsum(dxo)
    return dz


def mlp_bwd(dxo, x, y, ra, mods, k, w1_t, w2, lng, lnb=None, rider=None):
    s = x.shape[0]
    tm = _blk(s, TM_MLP)
    head = lnb is not None

    def body(d_ref, x_ref, y_ref, ra_ref, mod_ref, w1_ref, w2_ref, g_ref, *rest):
        b_ref = rest[0] if head else None
        dx_ref, da_ref, h_ref, dy_ref, sums_ref = rest[1:] if head else rest

        @pl.when(pl.program_id(0) == 0)
        def _():
            sums_ref[...] = jnp.zeros(sums_ref.shape, F32)

        xv = x_ref[...]
        shift, scale, gate = _mod(mod_ref, k)
        if head:
            dz = _ln_part_bwd(None, xv, y_ref[...], gate, g_ref[...], sums_ref, (b_ref[...], d_ref[...]))
        else:
            dz = _ln_part_bwd(d_ref[...], xv, y_ref[...], gate, g_ref[...], sums_ref)
        dyb = (dz * (1.0 + gate)).astype(BF16)
        dy_ref[...] = dyb
        h = (xv * (1.0 + scale) + shift).astype(BF16)
        h_ref[...] = h
        dh = jnp.zeros((tm, D_MODEL), F32)
        for c in range(D_FF // FF_CHUNK):
            rows = slice(FF_CHUNK * c, FF_CHUNK * (c + 1))
            da = (_nt(dyb, w2_ref[rows, :]) * (2.0 * ra_ref[:, rows].astype(F32))).astype(BF16)
            da_ref[:, rows] = da
            dh = dh + _nn(da, w1_ref[rows, :])
        dx_ref[...] = ALPHA * dz + dh * (1.0 + scale)
        sums_ref[0:1, :] += _colsum(dh)
        sums_ref[1:2, :] += _colsum(dh * xv)

    return _call(
        body, name="mlp_bwd", grid=(s // tm,),
        out_shape=[jax.ShapeDtypeStruct((s, D_MODEL), F32), jax.ShapeDtypeStruct((s, D_FF), BF16),
                   jax.ShapeDtypeStruct((s, D_MODEL), BF16),
                   jax.ShapeDtypeStruct((s, D_MODEL), BF16), jax.ShapeDtypeStruct((8, D_MODEL), F32)],
        in_specs=[_row(tm, D_MODEL)] * 3 + [_row(tm, D_FF), _res(mods.shape), _res(w1_t.shape), _res(w2.shape),
                                             _res(lng.shape)] + ([_res(lnb.shape)] if head else []),
        out_specs=[_row(tm, D_MODEL), _row(tm, D_FF), _row(tm, D_MODEL), _row(tm, D_MODEL), _res((8, D_MODEL))],
        args=(dxo, x, y, ra, mods, w1_t, w2, lng) + ((lnb,) if head else ()), rider=rider)


def post_bwd(dxo, x, y, mods, k, w, lng, gate_act=None, rider=None):
    s = x.shape[0]
    tm = _blk(s, TM_MM)
    kdim = w.shape[0]
    rnn = gate_act is not None

    def body(*refs):
        if rnn:
            (d_ref, x_ref, y_ref, mod_ref, w_ref, g_ref, gt_ref, hs_ref,
             dres_ref, dy_ref, sums_ref, dhs_ref, dgt_ref) = refs
        else:
            d_ref, x_ref, y_ref, mod_ref, w_ref, g_ref, dres_ref, dy_ref, sums_ref, dyp_ref = refs

        @pl.when(pl.program_id(0) == 0)
        def _():
            sums_ref[...] = jnp.zeros(sums_ref.shape, F32)

        _, _, gate = _mod(mod_ref, k)
        dz = _ln_part_bwd(d_ref[...], x_ref[...], y_ref[...], gate, g_ref[...], sums_ref)
        dres_ref[...] = ALPHA * dz
        dyb = (dz * (1.0 + gate)).astype(BF16)
        dy_ref[...] = dyb
        dyp = _nt(dyb, w_ref[...])
        if rnn:
            act, dact = _gelu_parts(gt_ref[...].astype(F32))
            dhs = dyp * act
            for l in range(kdim // 128):
                dhs_ref[l] = dhs[:, 128 * l:128 * (l + 1)]
            dgt_ref[...] = (dyp * hs_ref[...].astype(F32) * dact).astype(BF16)
        else:
            dyp_ref[...] = dyp.astype(BF16)

    ins = [dxo, x, y, mods, w, lng] + (list(gate_act) if rnn else [])
    in_specs = [_row(tm, D_MODEL)] * 3 + [_res(mods.shape), _res(w.shape), _res(lng.shape)]
    out_shape = [jax.ShapeDtypeStruct((s, D_MODEL), F32), jax.ShapeDtypeStruct((s, D_MODEL), BF16),
                 jax.ShapeDtypeStruct((8, D_MODEL), F32)]
    out_specs = [_row(tm, D_MODEL), _row(tm, D_MODEL), _res((8, D_MODEL))]
    if rnn:
        in_specs += [_row(tm, kdim)] * 2
        out_shape += [jax.ShapeDtypeStruct((kdim // 128, s, 128), F32), jax.ShapeDtypeStruct((s, kdim), BF16)]
        out_specs += [pl.BlockSpec((kdim // 128, tm, 128), lambda i: (0, i, 0)), _row(tm, kdim)]
    else:
        out_shape.append(jax.ShapeDtypeStruct((s, kdim), BF16))
        out_specs.append(_row(tm, kdim))
    return _call(
        body, name="rnn_post_bwd" if rnn else "attn_post_bwd", grid=(s // tm,),
        out_shape=out_shape, in_specs=in_specs, out_specs=out_specs, args=ins, rider=rider)


def attn_bwd(q, kk, v, do, sinks, rider=None):
    s = q.shape[0]
    nblk = s // QBLK
    scale = HEAD ** -0.5

    def body(sink_ref, q_ref, do_ref, kp, ko, kn, vp, vo, vn, dq_ref, dk_ref, dv_ref, ds_ref):
        n = pl.program_id(0)

        @pl.when(n == 0)
        def _():
            ds_ref[...] = jnp.zeros(ds_ref.shape, F32)
            dk_ref[...] = jnp.zeros(dk_ref.shape, F32)
            dv_ref[...] = jnp.zeros(dv_ref.shape, F32)

        kall = jnp.concatenate([kp[...], ko[...], kn[...]], axis=0)
        vall = jnp.concatenate([vp[...], vo[...], vn[...]], axis=0)
        lane = lax.broadcasted_iota(jnp.int32, (1, 128), 1)
        dsink = jnp.zeros((1, 128), F32)
        for qb in range(QPAIR):
            nb = QPAIR * n + qb
            valid = _attn_mask(nb, s)
            keys = slice(QBLK * qb, QBLK * (qb + 3))
            for kv in range(N_KV):
                cols = slice(HEAD * kv, HEAD * (kv + 1))
                qg, dog = _stack_heads(q_ref, qb, kv), _stack_heads(do_ref, qb, kv)
                kh, vh = kall[keys, cols], vall[keys, cols]
                probs, psink = _attn_probs(qg, kh, valid, _stack_sinks(sink_ref, kv))
                dprobs = _nt(dog, vh)
                dvp = _tn(probs.astype(BF16), dog)
                rowdot = jnp.sum(probs * dprobs, axis=-1, keepdims=True)
                dsb = (probs * (dprobs - rowdot) * scale).astype(BF16)
                dqg = _nn(dsb, kh)
                dkp = _tn(dsb, qg)
                for p in range(3):
                    blk = jnp.clip(nb - 1 + p, 0, nblk - 1)
                    rows = pl.ds(pl.multiple_of(blk * QBLK, QBLK), QBLK)
                    dk_ref[rows, cols] += dkp[QBLK * p:QBLK * (p + 1), :]
                    dv_ref[rows, cols] += dvp[QBLK * p:QBLK * (p + 1), :]
                dsk = -psink * rowdot
                for j in range(GROUP):
                    hq = GROUP * kv + j
                    dq_ref[QBLK * qb:QBLK * (qb + 1), HEAD * hq:HEAD * (hq + 1)] = dqg[QBLK * j:QBLK * (j + 1), :]
                    dsink = dsink + jnp.where(lane == hq, _colsum(dsk[QBLK * j:QBLK * (j + 1), :]), 0.0)
        ds_ref[...] += dsink

    qspec = pl.BlockSpec((QPAIR * QBLK, N_Q * HEAD), lambda n: (n, 0))
    return _call(
        body, name="attn_bwd", grid=(nblk // QPAIR,),
        out_shape=[jax.ShapeDtypeStruct((s, N_Q * HEAD), F32),
                   jax.ShapeDtypeStruct((s, N_KV * HEAD), F32), jax.ShapeDtypeStruct((s, N_KV * HEAD), F32),
                   jax.ShapeDtypeStruct((1, 128), F32)],
        in_specs=[pl.BlockSpec(memory_space=pltpu.SMEM), qspec, qspec] + _kv_specs(nblk) + _kv_specs(nblk),
        out_specs=[qspec, _res((s, N_KV * HEAD)), _res((s, N_KV * HEAD)), pl.BlockSpec((1, 128), lambda n: (0, 0))],
        args=(sinks, q, do, kk, kk, kk, v, v, v), rider=rider)


def _in_bwd_tail(dzb, w_ref, x_ref, mod_ref, k, dres_ref, dx_ref, h_ref, sums_ref):
    xv = x_ref[...]
    shift, scale, _ = _mod(mod_ref, k)
    h_ref[...] = (xv * (1.0 + scale) + shift).astype(BF16)
    dh = _nn(dzb, w_ref[...])
    dx_ref[...] = dres_ref[...] + dh * (1.0 + scale)
    sums_ref[0:1, :] += _colsum(dh)
    sums_ref[1:2, :] += _colsum(dh * xv)


def attn_in_bwd(dq, dk, dv, rope, x, mods, k, win_t, dres):
    s = x.shape[0]
    tm = _blk(s, TM_MM)

    def body(dq_ref, dk_ref, dv_ref, c_ref, s1_ref, s2_ref, x_ref, mod_ref, w_ref, dres_ref,
             dx_ref, dz_ref, h_ref, sums_ref):
        @pl.when(pl.program_id(0) == 0)
        def _():
            sums_ref[...] = jnp.zeros(sums_ref.shape, F32)

        cos, s1, s2 = c_ref[...], s1_ref[...], s2_ref[...]
        for hh in range(N_Q + N_KV):
            src = dq_ref[:, HEAD * hh:HEAD * (hh + 1)] if hh < N_Q else dk_ref[:, HEAD * (hh - N_Q):HEAD * (hh - N_Q + 1)]
            dz_ref[:, HEAD * hh:HEAD * (hh + 1)] = _rope_bwd(src, cos, s1, s2).astype(BF16)
        dz_ref[:, HEAD * (N_Q + N_KV):] = dv_ref[...].astype(BF16)
        _in_bwd_tail(dz_ref[...], w_ref, x_ref, mod_ref, k, dres_ref, dx_ref, h_ref, sums_ref)

    return pl.pallas_call(
        body, name="attn_in_bwd", grid=(s // tm,),
        out_shape=[jax.ShapeDtypeStruct((s, D_MODEL), F32), jax.ShapeDtypeStruct((s, D_QKV), BF16),
                   jax.ShapeDtypeStruct((s, D_MODEL), BF16), jax.ShapeDtypeStruct((8, D_MODEL), F32)],
        in_specs=[_row(tm, N_Q * HEAD), _row(tm, N_KV * HEAD), _row(tm, N_KV * HEAD),
                  _row(tm, HEAD), _row(tm, HEAD), _row(tm, HEAD), _row(tm, D_MODEL),
                  _res(mods.shape), _res(win_t.shape), _row(tm, D_MODEL)],
        out_specs=[_row(tm, D_MODEL), _row(tm, D_QKV), _row(tm, D_MODEL), _res((8, D_MODEL))],
        compiler_params=_params(),
    )(dq, dk, dv, *rope, x, mods, win_t, dres)


def _shift_blk(v, k, before, after, row):
    n = v.shape[0]
    r = pltpu.roll(v, k % n, 0)
    for j in range(abs(k)):
        if k > 0:
            r = jnp.where(row == j, before[8 - k + j:8 - k + j + 1, :], r)
        else:
            r = jnp.where(row == n + k + j, after[j:j + 1, :], r)
    return r


def rnn_in_bwd(dxc_f, dxc_b, xr, cw, dgt, x, mods, k, win_t, dres):
    s = x.shape[0]
    tm = _blk(s, TM_MM)
    n = s // tm

    def body(f_ref, fp_ref, fn_ref, b_ref, bp_ref, bn_ref, xr_ref, xp_ref, xn_ref, cw_ref, dgt_ref,
             x_ref, mod_ref, w_ref, dres_ref, dx_ref, dz_ref, h_ref, sums_ref, dcw_ref, dcb_ref):
        i = pl.program_id(0)

        @pl.when(i == 0)
        def _():
            sums_ref[...] = jnp.zeros(sums_ref.shape, F32)
            dcw_ref[...] = jnp.zeros(dcw_ref.shape, F32)
            dcb_ref[...] = jnp.zeros(dcb_ref.shape, F32)

        d = _slab_rows(f_ref) + _slab_rows(b_ref)
        xv = xr_ref[...]
        first, last = i == 0, i == n - 1
        d_before = jnp.where(first, 0.0, _slab_rows(fp_ref) + _slab_rows(bp_ref))
        d_after = jnp.where(last, 0.0, _slab_rows(fn_ref) + _slab_rows(bn_ref))
        x_before = jnp.where(first, 0.0, xp_ref[...])
        x_after = jnp.where(last, 0.0, xn_ref[...])
        row = lax.broadcasted_iota(jnp.int32, d.shape, 0)
        dxr = (cw_ref[0:1, :] * _shift_blk(d, -2, d_before, d_after, row)
               + cw_ref[1:2, :] * _shift_blk(d, -1, d_before, d_after, row)
               + cw_ref[2:3, :] * d + cw_ref[3:4, :] * _shift_blk(d, 1, d_before, d_after, row))
        dcw_ref[0:1, :] += _colsum(d * _shift_blk(xv, 2, x_before, x_after, row))
        dcw_ref[1:2, :] += _colsum(d * _shift_blk(xv, 1, x_before, x_after, row))
        dcw_ref[2:3, :] += _colsum(d * xv)
        dcw_ref[3:4, :] += _colsum(d * _shift_blk(xv, -1, x_before, x_after, row))
        dcb_ref[...] += _colsum(d)
        dz_ref[:, 0:D_RNN] = dxr.astype(BF16)
        dz_ref[:, D_RNN:2 * D_RNN] = dgt_ref[...]
        _in_bwd_tail(dz_ref[...], w_ref, x_ref, mod_ref, k, dres_ref, dx_ref, h_ref, sums_ref)

    per8 = tm // 8
    ns = D_RNN // 128
    blk = _row(tm, D_RNN)
    before = pl.BlockSpec((8, D_RNN), lambda i: (jnp.maximum(i * per8 - 1, 0), 0))
    after = pl.BlockSpec((8, D_RNN), lambda i: (jnp.minimum((i + 1) * per8, s // 8 - 1), 0))
    sblk = pl.BlockSpec((ns, tm, 128), lambda i: (0, i, 0))
    sbefore = pl.BlockSpec((ns, 8, 128), lambda i: (0, jnp.maximum(i * per8 - 1, 0), 0))
    safter = pl.BlockSpec((ns, 8, 128), lambda i: (0, jnp.minimum((i + 1) * per8, s // 8 - 1), 0))
    return pl.pallas_call(
        body, name="rnn_in_bwd", grid=(n,),
        out_shape=[jax.ShapeDtypeStruct((s, D_MODEL), F32), jax.ShapeDtypeStruct((s, 2 * D_RNN), BF16),
                   jax.ShapeDtypeStruct((s, D_MODEL), BF16), jax.ShapeDtypeStruct((8, D_MODEL), F32),
                   jax.ShapeDtypeStruct((4, D_RNN), F32), jax.ShapeDtypeStruct((1, D_RNN), F32)],
        in_specs=[sblk, sbefore, safter] * 2 + [blk, before, after] + [
            _res(cw.shape), blk, _row(tm, D_MODEL), _res(mods.shape), _res(win_t.shape), _row(tm, D_MODEL)],
        out_specs=[_row(tm, D_MODEL), _row(tm, 2 * D_RNN), _row(tm, D_MODEL), _res((8, D_MODEL)),
                   _res((4, D_RNN)), _res((1, D_RNN))],
        compiler_params=_params(),
    )(dxc_f, dxc_f, dxc_f, dxc_b, dxc_b, dxc_b, xr, xr, xr, cw, dgt, x, mods, win_t, dres)


def wgrad(a, b, name, rider=None):
    s, m = a.shape
    n = b.shape[1]
    tm = next(t for t in (1024, 768, 512, 384, 256, 128) if m % t == 0)
    tk = _blk(s, TK_WG)
    nk = s // tk

    def body(a_ref, b_ref, o_ref, acc):
        kk = pl.program_id(1)

        @pl.when(kk == 0)
        def _():
            acc[...] = jnp.zeros(acc.shape, F32)

        acc[...] += _tn(a_ref[...], b_ref[...])

        @pl.when(kk == nk - 1)
        def _():
            o_ref[...] = acc[...].astype(BF16)

    out, *rode = _call(
        body, name=name, grid=(m // tm, nk),
        out_shape=[jax.ShapeDtypeStruct((m, n), BF16)],
        in_specs=[pl.BlockSpec((tk, tm), lambda i, kk: (kk, i)), pl.BlockSpec((tk, n), lambda i, kk: (kk, 0))],
        out_specs=[pl.BlockSpec((tm, n), lambda i, kk: (i, 0))],
        scratch_shapes=[pltpu.VMEM((tm, n), F32)],
        args=(a, b), rider=rider)
    out = out.reshape(N_DEV, m // N_DEV, n)
    return (out, *rode) if rider is not None else out


def part_sum(parts, name):
    _, r, c = parts.shape
    tr = next(t for t in (256, 192, 128, 64, 32, 16, 8) if r % t == 0)

    def body(p_ref, o_ref):
        acc = p_ref[0].astype(F32)
        for j in range(1, N_DEV):
            acc = acc + p_ref[j].astype(F32)
        o_ref[...] = acc

    return pl.pallas_call(
        body, name=name, grid=(r // tr,),
        out_shape=jax.ShapeDtypeStruct((r, c), F32),
        in_specs=[pl.BlockSpec((N_DEV, tr, c), lambda i: (0, i, 0))],
        out_specs=pl.BlockSpec((tr, c), lambda i: (i, 0)),
        compiler_params=_params(),
    )(parts)


def adamw(w, g, m, v, name):
    shape = w.shape
    c = shape[-1]
    r = w.size // c
    w2, g2, m2, v2 = (t.reshape(r, c) for t in (w, g, m, v))
    tr = r if r * c <= 512 * 1024 else next(t for t in (512, 256, 128, 64, 32, 16, 8) if r % t == 0)

    def body(w_ref, g_ref, m_ref, v_ref, d_ref, nm_ref, nv_ref):
        gv = g_ref[...]
        nm = B1 * m_ref[...] + (1.0 - B1) * gv
        nv = B2 * v_ref[...] + (1.0 - B2) * (gv * gv)
        nm_ref[...] = nm
        nv_ref[...] = nv
        m_hat = nm / (1.0 - B1 ** STEP)
        v_hat = nv / (1.0 - B2 ** STEP)
        d_ref[...] = -LR * (m_hat / (jnp.sqrt(v_hat) + ADAM_EPS) + WD * w_ref[...])

    spec = pl.BlockSpec((tr, c), lambda i: (i, 0))
    outs = pl.pallas_call(
        body, name=name, grid=(r // tr,),
        out_shape=[jax.ShapeDtypeStruct((r, c), F32)] * 3,
        in_specs=[spec] * 4, out_specs=[spec] * 3,
        compiler_params=_params(),
    )(w2, g2, m2, v2)
    return tuple(o.reshape(shape) for o in outs)


def _rope_tables(s):
    half = ROT // 2
    inv_freq = THETA ** (-jnp.arange(0, ROT, 2, dtype=F32) / ROT)
    per_row = 128 // half
    pos = (per_row * jnp.arange(s // per_row)[:, None] + jnp.arange(128)[None, :] // half).astype(F32)
    ang = pos * jnp.tile(inv_freq, per_row)[None, :]
    cos, sin = lax.optimization_barrier((jnp.cos(ang), jnp.sin(ang)))
    cos, sin = cos.reshape(s, half), sin.reshape(s, half)
    zeros = jnp.zeros((s, HEAD - ROT), F32)
    c = jnp.concatenate([cos, cos, jnp.ones((s, HEAD - ROT), F32)], axis=1)
    s1 = jnp.concatenate([jnp.zeros((s, half), F32), sin, zeros], axis=1)
    s2 = jnp.concatenate([-sin, jnp.zeros((s, half), F32), zeros], axis=1)
    return c, s1, s2


def _blockdiag(w):
    w4 = w.reshape(N_CG, 4, RB_W, RB_W)
    eye = jnp.eye(4, dtype=w.dtype)
    return jnp.einsum("gipq,ij->gipjq", w4, eye).reshape(N_CG, CG, CG)


def _diag_blocks(w):
    w5 = w.reshape(N_CG, 4, RB_W, 4, RB_W)
    eye = jnp.eye(4, dtype=w.dtype)
    return jnp.einsum("gipjq,ij->gipq", w5, eye).reshape(N_RB, RB_W, RB_W)


def _cols(full, per):
    lead = full.shape[:-1]
    t = full.reshape(lead + (N_DEV, per))
    return jnp.moveaxis(t, -2, 0).reshape(N_DEV, -1)


def kernel(x, c, ada_w, ada_b, ln_g, ln_b, attn_w_in, attn_w_out, attn_sinks, rnn_w_in, rnn_conv_w, rnn_conv_b, rnn_w_a, rnn_b_a, rnn_w_x, rnn_b_x, rnn_lam, rnn_w_out, mlp_w1, mlp_w2, loss_target, m_ada_w, m_ada_b, m_ln_g, m_ln_b, m_attn_w_in, m_attn_w_out, m_attn_sinks, m_rnn_w_in, m_rnn_conv_w, m_rnn_conv_b, m_rnn_w_a, m_rnn_b_a, m_rnn_w_x, m_rnn_b_x, m_rnn_lam, m_rnn_w_out, m_mlp_w1, m_mlp_w2, v_ada_w, v_ada_b, v_ln_g, v_ln_b, v_attn_w_in, v_attn_w_out, v_attn_sinks, v_rnn_w_in, v_rnn_conv_w, v_rnn_conv_b, v_rnn_w_a, v_rnn_b_a, v_rnn_w_x, v_rnn_b_x, v_rnn_lam, v_rnn_w_out, v_mlp_w1, v_mlp_w2):
    s = x.shape[1]
    x0 = x.reshape(s, D_MODEL)
    target = loss_target.reshape(s, D_MODEL)
    weights = dict(ada_w=ada_w, ada_b=ada_b, ln_g=ln_g, ln_b=ln_b, attn_w_in=attn_w_in, attn_w_out=attn_w_out,
                   attn_sinks=attn_sinks, rnn_w_in=rnn_w_in, rnn_conv_w=rnn_conv_w, rnn_conv_b=rnn_conv_b,
                   rnn_w_a=rnn_w_a, rnn_b_a=rnn_b_a, rnn_w_x=rnn_w_x, rnn_b_x=rnn_b_x, rnn_lam=rnn_lam,
                   rnn_w_out=rnn_w_out, mlp_w1=mlp_w1, mlp_w2=mlp_w2)
    moments_m = dict(ada_w=m_ada_w, ada_b=m_ada_b, ln_g=m_ln_g, ln_b=m_ln_b, attn_w_in=m_attn_w_in,
                     attn_w_out=m_attn_w_out, attn_sinks=m_attn_sinks, rnn_w_in=m_rnn_w_in,
                     rnn_conv_w=m_rnn_conv_w, rnn_conv_b=m_rnn_conv_b, rnn_w_a=m_rnn_w_a, rnn_b_a=m_rnn_b_a,
                     rnn_w_x=m_rnn_w_x, rnn_b_x=m_rnn_b_x, rnn_lam=m_rnn_lam, rnn_w_out=m_rnn_w_out,
                     mlp_w1=m_mlp_w1, mlp_w2=m_mlp_w2)
    moments_v = dict(ada_w=v_ada_w, ada_b=v_ada_b, ln_g=v_ln_g, ln_b=v_ln_b, attn_w_in=v_attn_w_in,
                     attn_w_out=v_attn_w_out, attn_sinks=v_attn_sinks, rnn_w_in=v_rnn_w_in,
                     rnn_conv_w=v_rnn_conv_w, rnn_conv_b=v_rnn_conv_b, rnn_w_a=v_rnn_w_a, rnn_b_a=v_rnn_b_a,
                     rnn_w_x=v_rnn_w_x, rnn_b_x=v_rnn_b_x, rnn_lam=v_rnn_lam, rnn_w_out=v_rnn_w_out,
                     mlp_w1=v_mlp_w1, mlp_w2=v_mlp_w2)
    names = list(weights)

    def t16(w):
        return w.T.astype(BF16)

    big = [t16(attn_w_in[0]), attn_w_out[0].astype(BF16), t16(rnn_w_in[0]), rnn_w_out[0].astype(BF16),
           t16(mlp_w1[0]), mlp_w2[0].astype(BF16), t16(mlp_w1[1]), mlp_w2[1].astype(BF16)]
    small_local = jnp.concatenate([
        ln_g.reshape(-1), ln_b.reshape(-1), rnn_conv_w.reshape(-1), rnn_conv_b.reshape(-1),
        rnn_b_a.reshape(-1), rnn_b_x.reshape(-1), rnn_lam.reshape(-1)])
    small_local = jnp.pad(small_local, (0, 4096 - small_local.shape[0])).reshape(32, 128)
    flat = lambda g: g.reshape(N_DEV * g.shape[1], D_MODEL)
    c_all, modr, win_t, sm = ada_modulation(jnp.broadcast_to(c, (8, D_MODEL)), ada_w.reshape(4, D_MODEL, CG),
                                            ada_b.reshape(4, 1, CG), _Gather([big[0], small_local]))
    win_t = flat(win_t)
    sm = sm.reshape(N_DEV, 4096)

    def full_vec(off, rows, per):
        piece = sm[:, off:off + rows * per].reshape(N_DEV, rows, per)
        return jnp.moveaxis(piece, 0, 1).reshape(rows, N_DEV * per)

    lng_f, lnb_f = full_vec(0, 4, 128), full_vec(512, 4, 128)
    cw_f, cb_f = full_vec(1024, 4, 192), full_vec(1792, 1, 192)
    ba_f, bx_f, lam_f = full_vec(1984, 2, 192), full_vec(2368, 2, 192), full_vec(2752, 2, 192)
    wa_bd = [_blockdiag(rnn_w_a[0, d]).astype(BF16) for d in range(2)]
    wx_bd = [_blockdiag(rnn_w_x[0, d]).astype(BF16) for d in range(2)]

    mods = modr.reshape(N_DEV, 4, 8, CG)[:, :, 0, :]
    mods = jnp.moveaxis(mods, 0, 1).reshape(4, 3, D_MODEL).reshape(12, D_MODEL)
    rope = _rope_tables(s)
    ln = lambda k: (lng_f[k:k + 1], lnb_f[k:k + 1])

    q, kk, v, wout = attn_in_fwd(x0, mods, 0, win_t, rope, rider=_Gather([big[1]]))
    wout = flat(wout)
    o, *got = attn_fwd(q, kk, v, attn_sinks, rider=_Gather([big[4], big[5]]))
    w1t_0, w2_0 = (flat(g) for g in got)
    x1, y0, rout = post_fwd(o, wout, x0, mods, 0, *ln(0), rider=_Gather([big[3]]))
    rout = flat(rout)
    x2, y1, ra0, r0, *got = mlp_fwd(x1, mods, 1, w1t_0, w2_0, *ln(1), rider=_Gather([big[2], big[6], big[7]]))
    rin_t, w1t_1, w2_1 = (flat(g) for g in got)
    xr, gt = rnn_in_fwd(x2, mods, 2, rin_t)
    xc = conv_fwd(xr, cw_f, cb_f)
    hf, = lru_fwd(xc, wa_bd[0], wx_bd[0], ba_f[0:1], bx_f[0:1], lam_f[0:1], False)
    hb, hsum = lru_fwd(xc, wa_bd[1], wx_bd[1], ba_f[1:2], bx_f[1:2], lam_f[1:2], True, other=hf)
    x3, y2, ypre = post_fwd(None, rout, x2, mods, 2, *ln(2), gate_act=(gt, hsum))
    y3, ra1, r1 = mlp_fwd(x3, mods, 3, w1t_1, w2_1, *ln(3), last=True)

    dx3, da1, h3, dy3, sums3 = mlp_bwd(target, x3, y3, ra1, mods, 3, w1t_1, w2_1, lng_f[3:4], lnb=lnb_f[3:4])
    g_w1t_1 = wgrad(da1, h3, "wgrad_w1_1")
    g_w2_1 = wgrad(r1, dy3, "wgrad_w2_1")
    dres2, dy2, sums2a, dhs, dgt, p_w1t_1 = post_bwd(dx3, x2, y2, mods, 2, rout, lng_f[2:3], gate_act=(gt, hsum),
                                                     rider=_AllToAll([g_w1t_1]))
    g_rout = wgrad(ypre, dy2, "wgrad_rnn_out")
    dxc_f, dwa_f, dwx_f, dba_f, dbx_f, dlam_f, p_w2_1, p_rout = lru_bwd(
        xc, dhs, hf, wa_bd[0], wx_bd[0], ba_f[0:1], bx_f[0:1], lam_f[0:1], False, rider=_AllToAll([g_w2_1, g_rout]))
    dxc_b, dwa_b, dwx_b, dba_b, dbx_b, dlam_b = lru_bwd(xc, dhs, hb, wa_bd[1], wx_bd[1], ba_f[1:2], bx_f[1:2],
                                                        lam_f[1:2], True)
    dx2, dzz, h2, sums2b, dcw, dcb = rnn_in_bwd(dxc_f, dxc_b, xr, cw_f, dgt, x2, mods, 2, rin_t, dres2)
    g_rin_t = wgrad(dzz, h2, "wgrad_rnn_in")
    d_wa = jnp.stack([_diag_blocks(dwa_f), _diag_blocks(dwa_b)])
    d_wx = jnp.stack([_diag_blocks(dwx_f), _diag_blocks(dwx_b)])
    nflat = d_wa.size // N_DEV
    gates = jnp.concatenate([d_wa.reshape(N_DEV, nflat), d_wx.reshape(N_DEV, nflat)], axis=1)
    gates = gates.reshape(N_DEV, 2 * nflat // 128, 128)
    dx1, da0, h1, dy1, sums1, p_rin_t, p_gates = mlp_bwd(dx2, x1, y1, ra0, mods, 1, w1t_0, w2_0, lng_f[1:2],
                                                         rider=_AllToAll([g_rin_t, gates]))
    gates_sum = part_sum(p_gates, "part_sum_gates")
    g_w1t_0 = wgrad(da0, h1, "wgrad_w1_0")
    g_w2_0, p_w1t_0 = wgrad(r0, dy1, "wgrad_w2_0", rider=_AllToAll([g_w1t_0]))
    dres0, dy0, sums0a, do = post_bwd(dx1, x0, y0, mods, 0, wout, lng_f[0:1])
    g_wout = wgrad(o, dy0, "wgrad_attn_out")
    dq, dk, dv, dsink, wag, p_w2_0, p_wout = attn_bwd(
        q, kk, v, do, attn_sinks, rider=_Multi(_Gather([gates_sum]), _AllToAll([g_w2_0, g_wout])))
    dx0, dqkv, h0, sums0b = attn_in_bwd(dq, dk, dv, rope, x0, mods, 0, win_t, dres0)
    g_win_t = wgrad(dqkv, h0, "wgrad_attn_in")

    sums = [sums0a + sums0b, sums1, sums2a + sums2b, sums3]
    gmod = jnp.stack([t[0:3] for t in sums])
    gsend = jnp.moveaxis(gmod.reshape(4, N_DEV, CG), 1, 0)
    gsend = jnp.pad(gsend, ((0, 0), (0, 4), (0, 0)))
    c_t = c_all[:, 0, :].T
    sq_err = jnp.sum(sums3[5]).reshape(1, 1)
    tail = jnp.concatenate([
        _cols(dcw, 192), _cols(dcb, 192),
        _cols(jnp.concatenate([dba_f, dba_b]), 192), _cols(jnp.concatenate([dbx_f, dbx_b]), 192),
        _cols(jnp.concatenate([dlam_f, dlam_b]), 192),
        _cols(jnp.stack([t[3] for t in sums]), 128), _cols(jnp.stack([t[4] for t in sums]), 128),
        jnp.broadcast_to(dsink[:, 0:8], (N_DEV, 8)), jnp.broadcast_to(sq_err, (N_DEV, 1))], axis=1)
    tail = jnp.pad(tail, ((0, 0), (0, 32 * 128 - tail.shape[1]))).reshape(N_DEV, 32, 128)
    g_ada_w, g_ada_b, red, p_win_t = epilogue(gsend, c_t, tail, _AllToAll([g_win_t]))
    grads = {"ada_w": g_ada_w.reshape(ada_w.shape), "ada_b": g_ada_b[0:4].reshape(ada_b.shape)}

    big_parts = [p_win_t, p_wout, p_rin_t, p_rout, p_w1t_0, p_w2_0, p_w1t_1, p_w2_1]
    gsum = [part_sum(p, "part_sum_%d" % i) for i, p in enumerate(big_parts)]
    grads.update({
        "attn_w_in": gsum[0].T[None], "attn_w_out": gsum[1][None],
        "rnn_w_in": gsum[2].T[None], "rnn_w_out": gsum[3][None],
        "mlp_w1": jnp.stack([gsum[4].T, gsum[6].T]), "mlp_w2": jnp.stack([gsum[5], gsum[7]]),
    })
    wag = wag.reshape(N_DEV, 2 * nflat)
    grads["rnn_w_a"] = wag[:, :nflat].reshape(rnn_w_a.shape)
    grads["rnn_w_x"] = wag[:, nflat:].reshape(rnn_w_x.shape)
    tl = red.reshape(-1)
    loss = 0.5 * tl[3144] / D_MODEL
    grads["rnn_conv_w"] = tl[0:768].reshape(rnn_conv_w.shape)
    grads["rnn_conv_b"] = tl[768:960].reshape(rnn_conv_b.shape)
    grads["rnn_b_a"] = tl[960:1344].reshape(rnn_b_a.shape)
    grads["rnn_b_x"] = tl[1344:1728].reshape(rnn_b_x.shape)
    grads["rnn_lam"] = tl[1728:2112].reshape(rnn_lam.shape)
    grads["ln_g"] = tl[2112:2624].reshape(ln_g.shape)
    grads["ln_b"] = tl[2624:3136].reshape(ln_b.shape)
    grads["attn_sinks"] = tl[3136:3144].reshape(attn_sinks.shape)

    delta, new_m, new_v = {}, {}, {}
    for n in names:
        delta[n], new_m[n], new_v[n] = adamw(weights[n], grads[n], moments_m[n], moments_v[n], "adamw_" + n)
    return (loss, dx0.reshape(x.shape), *[grads[n] for n in names], *[delta[n] for n in names],
            *[new_m[n] for n in names], *[new_v[n] for n in names])
```

```python
import functools
import math

import jax
import jax.numpy as jnp
from jax import lax
from jax.experimental import pallas as pl
from jax.experimental.pallas import tpu as pltpu

F32, BF16 = jnp.float32, jnp.bfloat16
MESH = pl.DeviceIdType.MESH

D_MODEL = 1024
N_Q, N_KV, HEAD = 8, 2, 128
ROT, THETA = 32, 500000.0
QBLK = 128
D_QKV = (N_Q + 2 * N_KV) * HEAD
D_RNN, N_RB, RB_W = 1536, 16, 96
CG = 384
N_CG = D_RNN // CG
D_FF = 4096
FF_CHUNK = 1024
DEPTH = 2
ALPHA = (2.0 * DEPTH) ** 0.25
LN_EPS = 1e-5
LRU_C = 8.0
N_DEV = 8
LR, B1, B2, ADAM_EPS, WD, STEP = 0.001, 0.9, 0.999, 1e-8, 0.01, 10

VMEM_LIMIT = 56 * 1024 * 1024
TM_MM = 512
TM_MLP = 256
TM_MLP_FWD = 512
TT_RNN = 2048
TK_WG = 2048


def _nn(a, b):
    return jnp.dot(a, b, preferred_element_type=F32)


def _nt(a, b):
    return lax.dot_general(a, b, (((1,), (1,)), ((), ())), preferred_element_type=F32)


def _tn(a, b):
    return lax.dot_general(a, b, (((0,), (0,)), ((), ())), preferred_element_type=F32)


def _blk(n, pref):
    t = min(n, pref)
    assert n % t == 0, (n, pref)
    return t


def _params(**kw):
    return pltpu.CompilerParams(vmem_limit_bytes=VMEM_LIMIT, **kw)


def _row(tm, w):
    return pl.BlockSpec((tm, w), lambda i: (i, 0))


def _res(shape):
    return pl.BlockSpec(shape, lambda i: (0,) * len(shape), pipeline_mode=pl.Buffered(1))


def _mod(mod_ref, k):
    return mod_ref[3 * k:3 * k + 1, :], mod_ref[3 * k + 1:3 * k + 2, :], mod_ref[3 * k + 2:3 * k + 3, :]


def _ln_stats(z):
    mu = jnp.mean(z, axis=-1, keepdims=True)
    zc = z - mu
    var = jnp.mean(zc * zc, axis=-1, keepdims=True)
    rstd = lax.rsqrt(var + LN_EPS)
    return zc * rstd, rstd


def _ln_bwd(dxo, xhat, rstd, g):
    dxh = dxo * g
    m1 = jnp.mean(dxh, axis=-1, keepdims=True)
    m2 = jnp.mean(dxh * xhat, axis=-1, keepdims=True)
    return rstd * (dxh - m1 - xhat * m2)


def _colsum(v):
    return jnp.sum(v, axis=0, keepdims=True)


def _sigmoid(v):
    return 0.5 * jnp.tanh(0.5 * v) + 0.5


def _gelu_parts(v):
    k = math.sqrt(2.0 / math.pi)
    u = k * (v + 0.044715 * v * v * v)
    t = jnp.tanh(u)
    g = 0.5 * v * (1.0 + t)
    dg = 0.5 * (1.0 + t) + 0.5 * v * (1.0 - t * t) * k * (1.0 + 3.0 * 0.044715 * v * v)
    return g, dg


def _me():
    return lax.axis_index("x"), lax.axis_index("y"), lax.axis_index("c")


def _idx(p):
    return 4 * p[0] + 2 * p[1] + p[2]


def _peers(me):
    x, y, c = me
    out = []
    for k in range(1, N_DEV):
        out.append((1 - x if k & 4 else x, 1 - y if k & 2 else y, 1 - c if k & 1 else c))
    return out


class _Gather:
    def __init__(self, srcs):
        self.srcs = list(srcs)
        n = len(self.srcs)
        self.out_shape = [jax.ShapeDtypeStruct((N_DEV,) + s.shape, s.dtype) for s in self.srcs]
        self.scratch = [pltpu.SemaphoreType.DMA((n, 7)), pltpu.SemaphoreType.DMA((n, 7)),
                        pltpu.SemaphoreType.DMA((n,))]

    @staticmethod
    def _places():
        x, y, c = me = _me()
        return me, (x, y, 1 - c), [(1 - x, y), (x, 1 - y), (1 - x, 1 - y)]

    @staticmethod
    def _copy(outs, sems, t, k, block, to, src=None):
        slot = outs[t].at[_idx(block)]
        return pltpu.make_async_remote_copy(
            src_ref=slot if src is None else src, dst_ref=slot, send_sem=sems[0].at[t, k],
            recv_sem=sems[1].at[t, k], device_id=to, device_id_type=MESH)

    def _firsts(self, ins, outs, sems):
        me, sibling, chips = self._places()
        out = []
        for t in range(len(ins)):
            out.append(self._copy(outs, sems, t, 0, me, sibling, src=ins[t]))
            out += [self._copy(outs, sems, t, 1 + j, me, (*chip, me[2]), src=ins[t]) for j, chip in enumerate(chips)]
        return out

    def _locals(self, ins, outs, sems):
        me = _me()
        return [pltpu.make_async_copy(ins[t], outs[t].at[_idx(me)], sems[2].at[t]) for t in range(len(ins))]

    def start(self, ins, outs, sems):
        for cp in self._locals(ins, outs, sems) + self._firsts(ins, outs, sems):
            cp.start()

    def mid(self, ins, outs, sems):
        me, sibling, chips = self._places()
        for j, chip in enumerate(chips):
            for t in range(len(ins)):
                self._copy(outs, sems, t, 1 + j, (*chip, me[2]), me).wait_recv()
                self._copy(outs, sems, t, 4 + j, (*chip, me[2]), sibling).start()

    def finish(self, ins, outs, sems):
        me, sibling, chips = self._places()
        for t in range(len(ins)):
            self._copy(outs, sems, t, 0, sibling, me).wait_recv()
            for j, chip in enumerate(chips):
                self._copy(outs, sems, t, 4 + j, (*chip, 1 - me[2]), me).wait_recv()
        for cp in self._firsts(ins, outs, sems):
            cp.wait_send()
        for j, chip in enumerate(chips):
            for t in range(len(ins)):
                self._copy(outs, sems, t, 4 + j, (*chip, me[2]), sibling).wait_send()
        for cp in self._locals(ins, outs, sems):
            cp.wait()


class _AllToAll:
    def __init__(self, srcs):
        self.srcs = list(srcs)
        n = len(self.srcs)
        self.out_shape = [jax.ShapeDtypeStruct(s.shape, s.dtype) for s in self.srcs]
        self.scratch = [pltpu.SemaphoreType.DMA((n, 7)), pltpu.SemaphoreType.DMA((n, 7)),
                        pltpu.SemaphoreType.DMA((n,))]

    def _copies(self, ins, outs, sems):
        me = _me()
        loc, rem = [], []
        for t in range(len(ins)):
            loc.append(pltpu.make_async_copy(ins[t].at[_idx(me)], outs[t].at[_idx(me)], sems[2].at[t]))
            for k, p in enumerate(_peers(me)):
                rem.append(pltpu.make_async_remote_copy(
                    src_ref=ins[t].at[_idx(p)], dst_ref=outs[t].at[_idx(me)], send_sem=sems[0].at[t, k],
                    recv_sem=sems[1].at[t, k], device_id=p, device_id_type=MESH))
        return loc, rem

    def start(self, ins, outs, sems):
        loc, rem = self._copies(ins, outs, sems)
        for cp in loc + rem:
            cp.start()

    def mid(self, ins, outs, sems):
        pass

    def finish(self, ins, outs, sems):
        me = _me()
        for t in range(len(ins)):
            for k, p in enumerate(_peers(me)):
                slot = outs[t].at[_idx(p)]
                pltpu.make_async_remote_copy(
                    src_ref=slot, dst_ref=slot, send_sem=sems[0].at[t, k], recv_sem=sems[1].at[t, k],
                    device_id=p, device_id_type=MESH).wait_recv()
        loc, rem = self._copies(ins, outs, sems)
        for cp in rem:
            cp.wait_send()
        for cp in loc:
            cp.wait()


class _Multi:
    def __init__(self, *exs):
        self.exs = exs
        self.srcs = [s for e in exs for s in e.srcs]
        self.out_shape = [s for e in exs for s in e.out_shape]
        self.scratch = [s for e in exs for s in e.scratch]

    def _each(self, ins, outs, sems):
        i = j = 0
        for e in self.exs:
            n, m = len(e.srcs), len(e.scratch)
            yield e, ins[i:i + n], outs[i:i + n], sems[j:j + m]
            i, j = i + n, j + m

    def start(self, ins, outs, sems):
        for e, a, b, c in self._each(ins, outs, sems):
            e.start(a, b, c)

    def mid(self, ins, outs, sems):
        for e, a, b, c in self._each(ins, outs, sems):
            e.mid(a, b, c)

    def finish(self, ins, outs, sems):
        for e, a, b, c in self._each(ins, outs, sems):
            e.finish(a, b, c)


def _call(body, *, name, grid, in_specs, out_specs, out_shape, args, scratch_shapes=(), rider=None):
    in_specs, out_specs, out_shape = list(in_specs), list(out_specs), list(out_shape)
    scratch_shapes = list(scratch_shapes)
    if rider is None:
        return pl.pallas_call(body, name=name, grid=grid, out_shape=out_shape, in_specs=in_specs,
                              out_specs=out_specs, scratch_shapes=scratch_shapes, compiler_params=_params())(*args)
    nci, nco, ncs, nr = len(in_specs), len(out_shape), len(scratch_shapes), len(rider.srcs)
    nsteps = math.prod(grid)
    assert nsteps >= 2, (name, grid)
    mid = max(1, (7 * nsteps) // 8)

    def full(*refs):
        ci, ri = refs[:nci], refs[nci:nci + nr]
        co, ro = refs[nci + nr:nci + nr + nco], refs[nci + nr + nco:nci + 2 * nr + nco]
        cs, rs = refs[nci + 2 * nr + nco:nci + 2 * nr + nco + ncs], refs[nci + 2 * nr + nco + ncs:]
        step = pl.program_id(0)
        for d in range(1, len(grid)):
            step = step * grid[d] + pl.program_id(d)

        @pl.when(step == 0)
        def _():
            rider.start(ri, ro, rs)

        @pl.when(step == mid)
        def _():
            rider.mid(ri, ro, rs)

        body(*ci, *co, *cs)

        @pl.when(step == nsteps - 1)
        def _():
            rider.finish(ri, ro, rs)

    any_spec = pl.BlockSpec(memory_space=pl.ANY)
    return pl.pallas_call(
        full, name=name, grid=grid, out_shape=out_shape + rider.out_shape,
        in_specs=in_specs + [any_spec] * nr, out_specs=out_specs + [any_spec] * nr,
        scratch_shapes=scratch_shapes + rider.scratch, compiler_params=_params(),
    )(*args, *rider.srcs)


def _a2a_start(srcs, dsts, send_sems, recv_sems, local_sems, me, sem_base=0):
    peers = _peers(me)
    started = []
    for t in range(len(srcs)):
        loc = pltpu.make_async_copy(srcs[t].at[_idx(me)], dsts[t].at[_idx(me)], local_sems.at[sem_base + t])
        loc.start()
        started.append(("local", loc))
        for k, p in enumerate(peers):
            cp = pltpu.make_async_remote_copy(
                src_ref=srcs[t].at[_idx(p)], dst_ref=dsts[t].at[_idx(me)],
                send_sem=send_sems.at[sem_base + t, k], recv_sem=recv_sems.at[sem_base + t, k],
                device_id=p, device_id_type=MESH)
            cp.start()
            started.append(("remote", cp))
    return started


def _a2a_finish(started, dsts, send_sems, recv_sems, me, sem_base=0):
    peers = _peers(me)
    for t in range(len(dsts)):
        for k, p in enumerate(peers):
            slot = dsts[t].at[_idx(p)]
            pltpu.make_async_remote_copy(
                src_ref=slot, dst_ref=slot, send_sem=send_sems.at[sem_base + t, k],
                recv_sem=recv_sems.at[sem_base + t, k], device_id=p, device_id_type=MESH).wait_recv()
    for kind, cp in started:
        if kind == "local":
            cp.wait()
        else:
            cp.wait_send()


def ada_modulation(c8, ada_w, ada_b, ride):
    nr = len(ride.srcs)

    def body(c_ref, w_ref, b_ref, *rest):
        ride_in, (call_ref, modr_ref), ride_out = rest[:nr], rest[nr:nr + 2], rest[nr + 2:2 * nr + 2]
        modp, send_sems, recv_sems, local_sems = rest[2 * nr + 2:2 * nr + 6]
        ride_sems = rest[2 * nr + 6:]
        ride.start(ride_in, ride_out, ride_sems)
        me = _me()
        peers = _peers(me)
        sends = []
        for k, p in enumerate(peers):
            cp = pltpu.make_async_remote_copy(
                src_ref=c_ref, dst_ref=call_ref.at[_idx(me)], send_sem=send_sems.at[0, k],
                recv_sem=recv_sems.at[0, k], device_id=p, device_id_type=MESH)
            cp.start()
            sends.append(cp)
        call_ref[_idx(me)] = c_ref[...]
        for k, p in enumerate(peers):
            slot = call_ref.at[_idx(p)]
            pltpu.make_async_remote_copy(
                src_ref=slot, dst_ref=slot, send_sem=send_sems.at[0, k], recv_sem=recv_sems.at[0, k],
                device_id=p, device_id_type=MESH).wait_recv()
        for cp in sends:
            cp.wait_send()
        cv = call_ref[...].reshape(N_DEV * 8, D_MODEL)
        s = (cv * _sigmoid(cv)).astype(BF16)
        for k in range(4):
            res = _nn(s, w_ref[k].astype(BF16)) + b_ref[k]
            for j in range(N_DEV):
                modp[j, 8 * k:8 * k + 8, :] = res[8 * j:8 * j + 8, :]
        started = _a2a_start([modp], [modr_ref], send_sems, recv_sems, local_sems, me, sem_base=1)
        _a2a_finish(started, [modr_ref], send_sems, recv_sems, me, sem_base=1)
        ride.mid(ride_in, ride_out, ride_sems)
        ride.finish(ride_in, ride_out, ride_sems)

    vm, hbm = pl.BlockSpec(memory_space=pltpu.VMEM), pl.BlockSpec(memory_space=pl.ANY)
    return pl.pallas_call(
        body, name="ada_modulation",
        out_shape=[jax.ShapeDtypeStruct((N_DEV, 8, D_MODEL), F32), jax.ShapeDtypeStruct((N_DEV, 32, CG), F32)]
        + ride.out_shape,
        in_specs=[vm, vm, vm] + [hbm] * nr, out_specs=[vm, vm] + [hbm] * nr,
        scratch_shapes=[pltpu.VMEM((N_DEV, 32, CG), F32), pltpu.SemaphoreType.DMA((2, 7)),
                        pltpu.SemaphoreType.DMA((2, 7)), pltpu.SemaphoreType.DMA((2,))] + ride.scratch,
        compiler_params=_params(),
    )(c8, ada_w, ada_b, *ride.srcs)


def epilogue(gsend, c_t, tail, ride):
    nr = len(ride.srcs)
    rt = tail.shape[1]

    def body(g_ref, ct_ref, t_ref, *rest):
        ride_in, (gw_ref, gb_ref, red_ref), ride_out = rest[:nr], rest[nr:nr + 3], rest[nr + 3:2 * nr + 3]
        grecv, trecv, send_sems, recv_sems, local_sems = rest[2 * nr + 3:2 * nr + 8]
        ride_sems = rest[2 * nr + 8:]
        ride.start(ride_in, ride_out, ride_sems)
        me = _me()
        started = _a2a_start([g_ref, t_ref], [grecv, trecv], send_sems, recv_sems, local_sems, me)
        _a2a_finish(started, [grecv, trecv], send_sems, recv_sems, me)
        acc = trecv[0]
        for j in range(1, N_DEV):
            acc = acc + trecv[j]
        red_ref[...] = acc
        ct = ct_ref[...]
        st = (ct * _sigmoid(ct)).astype(BF16).astype(F32)
        gb = jnp.zeros((8, CG), F32)
        for b in range(N_DEV):
            gb = gb + grecv[b]
        gb_ref[...] = gb
        for k in range(4):
            acc = jnp.zeros((D_MODEL, CG), F32)
            for b in range(N_DEV):
                row = grecv[b, k:k + 1, :].astype(BF16).astype(F32)
                acc = acc + st[:, b:b + 1] * row
            gw_ref[k] = acc
        ride.mid(ride_in, ride_out, ride_sems)
        ride.finish(ride_in, ride_out, ride_sems)

    vm, hbm = pl.BlockSpec(memory_space=pltpu.VMEM), pl.BlockSpec(memory_space=pl.ANY)
    return pl.pallas_call(
        body, name="epilogue",
        out_shape=[jax.ShapeDtypeStruct((4, D_MODEL, CG), F32), jax.ShapeDtypeStruct((8, CG), F32),
                   jax.ShapeDtypeStruct((rt, 128), F32)] + ride.out_shape,
        in_specs=[vm, vm, vm] + [hbm] * nr, out_specs=[vm, vm, vm] + [hbm] * nr,
        scratch_shapes=[pltpu.VMEM((N_DEV, 8, CG), F32), pltpu.VMEM((N_DEV, rt, 128), F32),
                        pltpu.SemaphoreType.DMA((2, 7)), pltpu.SemaphoreType.DMA((2, 7)),
                        pltpu.SemaphoreType.DMA((2,))] + ride.scratch,
        compiler_params=_params(),
    )(gsend, c_t, tail, *ride.srcs)


def _rope(t, cos, s1, s2):
    return t * cos + pltpu.roll(t, 16, 1) * s1 + pltpu.roll(t, HEAD - 16, 1) * s2


def _rope_bwd(d, cos, s1, s2):
    return d * cos + pltpu.roll(d * s1, HEAD - 16, 1) + pltpu.roll(d * s2, 16, 1)


def attn_in_fwd(x, mods, k, win_t, rope, rider=None):
    s = x.shape[0]
    tm = _blk(s, TM_MM)

    def body(x_ref, mod_ref, w_ref, c_ref, s1_ref, s2_ref, q_ref, k_ref, v_ref):
        shift, scale, _ = _mod(mod_ref, k)
        h = (x_ref[...] * (1.0 + scale) + shift).astype(BF16)
        qkv = _nt(h, w_ref[...])
        cos, s1, s2 = c_ref[...], s1_ref[...], s2_ref[...]
        for hh in range(N_Q + N_KV):
            r = _rope(qkv[:, HEAD * hh:HEAD * (hh + 1)], cos, s1, s2).astype(BF16)
            if hh < N_Q:
                q_ref[:, HEAD * hh:HEAD * (hh + 1)] = r
            else:
                k_ref[:, HEAD * (hh - N_Q):HEAD * (hh - N_Q + 1)] = r
        v_ref[...] = qkv[:, HEAD * (N_Q + N_KV):].astype(BF16)

    return _call(
        body, name="attn_in_fwd", grid=(s // tm,),
        out_shape=[jax.ShapeDtypeStruct((s, N_Q * HEAD), BF16), jax.ShapeDtypeStruct((s, N_KV * HEAD), BF16),
                   jax.ShapeDtypeStruct((s, N_KV * HEAD), BF16)],
        in_specs=[_row(tm, D_MODEL), _res(mods.shape), _res(win_t.shape),
                  _row(tm, HEAD), _row(tm, HEAD), _row(tm, HEAD)],
        out_specs=[_row(tm, N_Q * HEAD), _row(tm, N_KV * HEAD), _row(tm, N_KV * HEAD)],
        args=(x, mods, win_t, *rope), rider=rider)


QPAIR = 8


def _kv_specs(nblk):
    w = N_KV * HEAD
    return [pl.BlockSpec((QBLK, w), lambda n: (jnp.maximum(QPAIR * n - 1, 0), 0)),
            pl.BlockSpec((QPAIR * QBLK, w), lambda n: (n, 0)),
            pl.BlockSpec((QBLK, w), lambda n: (jnp.minimum(QPAIR * (n + 1), nblk - 1), 0))]


GROUP = N_Q // N_KV


def _attn_mask(n, s):
    qi = lax.broadcasted_iota(jnp.int32, (GROUP * QBLK, 3 * QBLK), 0) & (QBLK - 1)
    kj = lax.broadcasted_iota(jnp.int32, (GROUP * QBLK, 3 * QBLK), 1)
    rel = kj - QBLK - qi
    kpos = kj + (n - 1) * QBLK
    return (jnp.abs(rel) <= QBLK) & (kpos >= 0) & (kpos < s)


def _stack_heads(ref, qb, kv):
    rows = slice(QBLK * qb, QBLK * (qb + 1))
    return jnp.concatenate([ref[rows, HEAD * (GROUP * kv + j):HEAD * (GROUP * kv + j + 1)] for j in range(GROUP)],
                           axis=0)


def _stack_sinks(sink_ref, kv):
    row = lax.broadcasted_iota(jnp.int32, (GROUP * QBLK, 1), 0)
    out = jnp.full((GROUP * QBLK, 1), sink_ref[0, GROUP * kv + GROUP - 1], F32)
    for j in range(GROUP - 2, -1, -1):
        out = jnp.where(row < QBLK * (j + 1), sink_ref[0, GROUP * kv + j], out)
    return out


def _attn_probs(qh, kh, valid, sink):
    sc = _nt(qh, kh) * (HEAD ** -0.5)
    sc = jnp.where(valid, sc, -1e30)
    m = jnp.maximum(jnp.max(sc, axis=-1, keepdims=True), sink)
    p = jnp.exp(sc - m)
    es = jnp.exp(sink - m)
    denom = jnp.sum(p, axis=-1, keepdims=True) + es
    return p / denom, es / denom


def attn_fwd(q, kk, v, sinks, rider=None):
    s = q.shape[0]
    nblk = s // QBLK

    def body(sink_ref, q_ref, kp, ko, kn, vp, vo, vn, o_ref):
        n = pl.program_id(0)
        kall = jnp.concatenate([kp[...], ko[...], kn[...]], axis=0)
        vall = jnp.concatenate([vp[...], vo[...], vn[...]], axis=0)
        for qb in range(QPAIR):
            valid = _attn_mask(QPAIR * n + qb, s)
            keys = slice(QBLK * qb, QBLK * (qb + 3))
            for kv in range(N_KV):
                cols = slice(HEAD * kv, HEAD * (kv + 1))
                probs, _ = _attn_probs(_stack_heads(q_ref, qb, kv), kall[keys, cols], valid,
                                       _stack_sinks(sink_ref, kv))
                og = _nn(probs.astype(BF16), vall[keys, cols]).astype(BF16)
                for j in range(GROUP):
                    hq = GROUP * kv + j
                    o_ref[QBLK * qb:QBLK * (qb + 1), HEAD * hq:HEAD * (hq + 1)] = og[QBLK * j:QBLK * (j + 1), :]

    qspec = pl.BlockSpec((QPAIR * QBLK, N_Q * HEAD), lambda n: (n, 0))
    return _call(
        body, name="attn_fwd", grid=(nblk // QPAIR,),
        out_shape=[jax.ShapeDtypeStruct((s, N_Q * HEAD), BF16)],
        in_specs=[pl.BlockSpec(memory_space=pltpu.SMEM), qspec] + _kv_specs(nblk) + _kv_specs(nblk),
        out_specs=[qspec],
        args=(sinks, q, kk, kk, kk, v, v, v), rider=rider)


def post_fwd(ypre, w, x, mods, k, lng, lnb, gate_act=None, rider=None):
    s = x.shape[0]
    tm = _blk(s, TM_MM)
    kdim = w.shape[0]
    rnn = gate_act is not None

    def body(*refs):
        if rnn:
            gt_ref, hs_ref, w_ref, x_ref, mod_ref, g_ref, b_ref, xo_ref, y_ref, yp_ref = refs
            act, _ = _gelu_parts(gt_ref[...].astype(F32))
            yp = (hs_ref[...].astype(F32) * act).astype(BF16)
            yp_ref[...] = yp
        else:
            yp_ref, w_ref, x_ref, mod_ref, g_ref, b_ref, xo_ref, y_ref = refs
            yp = yp_ref[...]
        _, _, gate = _mod(mod_ref, k)
        y = _nn(yp, w_ref[...])
        y_ref[...] = y
        xhat, _ = _ln_stats(ALPHA * x_ref[...] + (1.0 + gate) * y)
        xo_ref[...] = xhat * g_ref[...] + b_ref[...]

    act_in = list(gate_act) if rnn else [ypre]
    out_shape = [jax.ShapeDtypeStruct((s, D_MODEL), F32), jax.ShapeDtypeStruct((s, D_MODEL), F32)]
    out_specs = [_row(tm, D_MODEL), _row(tm, D_MODEL)]
    if rnn:
        out_shape.append(jax.ShapeDtypeStruct((s, kdim), BF16))
        out_specs.append(_row(tm, kdim))
    return _call(
        body, name="rnn_post_fwd" if rnn else "attn_post_fwd", grid=(s // tm,),
        out_shape=out_shape,
        in_specs=[_row(tm, kdim)] * len(act_in) + [_res(w.shape), _row(tm, D_MODEL), _res(mods.shape),
                                                    _res(lng.shape), _res(lnb.shape)],
        out_specs=out_specs,
        args=(*act_in, w, x, mods, lng, lnb), rider=rider)


def mlp_fwd(x, mods, k, w1_t, w2, lng, lnb, rider=None, last=False):
    s = x.shape[0]
    tm = _blk(s, TM_MLP_FWD)

    def body(x_ref, mod_ref, w1_ref, w2_ref, g_ref, b_ref, *outs):
        xo_ref = None if last else outs[0]
        y_ref, ra_ref, r_ref = outs[-3:]
        xv = x_ref[...]
        shift, scale, gate = _mod(mod_ref, k)
        h = (xv * (1.0 + scale) + shift).astype(BF16)
        y = jnp.zeros((tm, D_MODEL), F32)
        for c in range(D_FF // FF_CHUNK):
            rows = slice(FF_CHUNK * c, FF_CHUNK * (c + 1))
            a = jnp.maximum(_nt(h, w1_ref[rows, :]), 0.0)
            r = (a * a).astype(BF16)
            ra_ref[:, rows] = a.astype(BF16)
            r_ref[:, rows] = r
            y = y + _nn(r, w2_ref[rows, :])
        y_ref[...] = y
        if not last:
            xhat, _ = _ln_stats(ALPHA * xv + (1.0 + gate) * y)
            xo_ref[...] = xhat * g_ref[...] + b_ref[...]

    nf = 1 if last else 2
    return _call(
        body, name="mlp_fwd_last" if last else "mlp_fwd", grid=(s // tm,),
        out_shape=[jax.ShapeDtypeStruct((s, D_MODEL), F32)] * nf + [jax.ShapeDtypeStruct((s, D_FF), BF16)] * 2,
        in_specs=[_row(tm, D_MODEL), _res(mods.shape), _res(w1_t.shape), _res(w2.shape),
                  _res(lng.shape), _res(lnb.shape)],
        out_specs=[_row(tm, D_MODEL)] * nf + [_row(tm, D_FF)] * 2,
        args=(x, mods, w1_t, w2, lng, lnb), rider=rider)


def rnn_in_fwd(x, mods, k, win_t):
    s = x.shape[0]
    tm = _blk(s, TM_MM)

    def body(x_ref, mod_ref, w_ref, xr_ref, gt_ref):
        shift, scale, _ = _mod(mod_ref, k)
        h = (x_ref[...] * (1.0 + scale) + shift).astype(BF16)
        xr_ref[...] = _nt(h, w_ref[0:D_RNN, :])
        gt_ref[...] = _nt(h, w_ref[D_RNN:2 * D_RNN, :]).astype(BF16)

    return pl.pallas_call(
        body, name="rnn_in_fwd", grid=(s // tm,),
        out_shape=[jax.ShapeDtypeStruct((s, D_RNN), F32), jax.ShapeDtypeStruct((s, D_RNN), BF16)],
        in_specs=[_row(tm, D_MODEL), _res(mods.shape), _res(win_t.shape)],
        out_specs=[_row(tm, D_RNN)] * 2,
        compiler_params=_params(),
    )(x, mods, win_t)


def _shift_rows(v, k, row):
    n = v.shape[0]
    r = pltpu.roll(v, k % n, 0)
    keep = (row >= k) if k > 0 else (row < n + k)
    return jnp.where(keep, r, 0.0)


def conv_fwd(xr, cw, cb):
    s = xr.shape[0]

    def body(x_ref, w_ref, b_ref, o_ref):
        xv = x_ref[...]
        row = lax.broadcasted_iota(jnp.int32, xv.shape, 0)
        o_ref[...] = (b_ref[...] + w_ref[0:1, :] * _shift_rows(xv, 2, row) + w_ref[1:2, :] * _shift_rows(xv, 1, row)
                      + w_ref[2:3, :] * xv + w_ref[3:4, :] * _shift_rows(xv, -1, row))

    slab = pl.BlockSpec((s, 128), lambda j: (0, j))
    return pl.pallas_call(
        body, name="conv_fwd", grid=(D_RNN // 128,),
        out_shape=jax.ShapeDtypeStruct((D_RNN // 128, s, 128), F32),
        in_specs=[slab, pl.BlockSpec((4, 128), lambda j: (0, j)), pl.BlockSpec((1, 128), lambda j: (0, j))],
        out_specs=pl.BlockSpec((None, s, 128), lambda j: (j, 0, 0)),
        compiler_params=_params(),
    )(xr, cw, cb)


def _softplus_neg(lam):
    z = -lam
    e = jnp.exp(-jnp.abs(z))
    u = 1.0 + e
    log1p = jnp.where(u == 1.0, e, jnp.log(u) * e / jnp.where(u == 1.0, 1.0, u - 1.0))
    return jnp.maximum(z, 0.0) + log1p, 1.0 / (1.0 + jnp.exp(lam))


def _lru_gates(xv, wa_ref, wx_ref, ba_ref, bx_ref, lam_ref):
    xb = xv.astype(BF16)
    r = _sigmoid(_nn(xb, wa_ref[...]) + ba_ref[...])
    i = _sigmoid(_nn(xb, wx_ref[...]) + bx_ref[...])
    sp, sg = _softplus_neg(lam_ref[...])
    la = r * (-LRU_C * sp)
    a = jnp.exp(la)
    th = jnp.tanh(la)
    m2 = -2.0 * th / (1.0 - th)
    rmult = lax.rsqrt(jnp.maximum(m2, 1e-37))
    return xb, r, i, sp, sg, a, m2 * rmult, rmult


SLABS = CG // 128
GRP, SEG = 32, 4


def _lru_specs(nt, tt, reverse):
    tmap = (lambda t: nt - 1 - t) if reverse else (lambda t: t)
    blk = pl.BlockSpec((tt, CG), lambda g, t: (tmap(t), g))
    slabs = pl.BlockSpec((SLABS, tt, 128), lambda g, t: (g, tmap(t), 0))
    wsp = pl.BlockSpec((None, CG, CG), lambda g, t: (g, 0, 0))
    vec = pl.BlockSpec((1, CG), lambda g, t: (0, g))
    return tmap, blk, slabs, wsp, vec


def _slab_rows(ref3):
    return jnp.concatenate([ref3[l] for l in range(ref3.shape[0])], axis=1)


def _perm_load(ref3, tt):
    out = []
    for l in range(SLABS):
        r = ref3.at[l]
        out.append(jnp.concatenate([r[pl.ds(GRP * g + i, 8, stride=SEG), :]
                                    for g in range(tt // GRP) for i in range(SEG)], axis=0))
    return jnp.concatenate(out, axis=1)


def _perm_scan(a, u, carry, reverse, emit):
    n, c = a.shape
    sub = lax.broadcasted_iota(jnp.int32, (8, c), 0)
    steps = [(8 - sh, sub < 8 - sh) if reverse else (sh, sub >= sh) for sh in (1, 2, 4)]
    order = range(SEG - 1, -1, -1) if reverse else range(SEG)
    for g in (range(n // GRP - 1, -1, -1) if reverse else range(n // GRP)):
        hs, ps = [None] * SEG, [None] * SEG
        h = p = None
        for i in order:
            rows = slice(GRP * g + 8 * i, GRP * g + 8 * i + 8)
            h = u[rows] if h is None else a[rows] * h + u[rows]
            p = a[rows] if p is None else a[rows] * p
            hs[i], ps[i] = h, p
        d, f = p, h
        for rot, keep in steps:
            d_s = jnp.where(keep, pltpu.roll(d, rot, 0), 1.0)
            f_s = jnp.where(keep, pltpu.roll(f, rot, 0), 0.0)
            f = d * f_s + f
            d = d * d_s
        end = f + d * carry
        if reverse:
            init = jnp.where(sub == 7, carry, pltpu.roll(end, 7, 0))
            carry = jnp.broadcast_to(end[0:1], (8, c))
        else:
            init = jnp.where(sub == 0, carry, pltpu.roll(end, 1, 0))
            carry = jnp.broadcast_to(end[7:8], (8, c))
        for i in range(SEG):
            emit(g, i, hs[i] + ps[i] * init)
    return carry


def lru_fwd(xc, wa, wx, ba, bx, lam, reverse, other=None):
    s = xc.shape[1]
    tt = _blk(s, TT_RNN)
    nt = s // tt

    def body(x_ref, wa_ref, wx_ref, ba_ref, bx_ref, lam_ref, *rest):
        if other is None:
            hs_ref, carry = rest
        else:
            oth_ref, hs_ref, sum_ref, carry = rest

        @pl.when(pl.program_id(1) == 0)
        def _():
            carry[...] = jnp.zeros(carry.shape, F32)

        xv = _perm_load(x_ref, tt)
        _, _, i, _, _, a, mult, _ = _lru_gates(xv, wa_ref, wx_ref, ba_ref, bx_ref, lam_ref)
        u = mult * (i * xv)

        def emit(g, j, rows):
            for l in range(SLABS):
                hs_ref.at[l][pl.ds(GRP * g + j, 8, stride=SEG), :] = rows[:, 128 * l:128 * (l + 1)]

        carry[...] = _perm_scan(a, u, carry[...], reverse, emit)
        if other is not None:
            for l in range(SLABS):
                sum_ref[:, 128 * l:128 * (l + 1)] = (hs_ref[l] + oth_ref[l]).astype(BF16)

    _, blk, slabs, wsp, vec = _lru_specs(nt, tt, reverse)
    extra = [] if other is None else [other]
    return pl.pallas_call(
        body, name="lru_fwd_rev" if reverse else "lru_fwd", grid=(N_CG, nt),
        out_shape=[jax.ShapeDtypeStruct(xc.shape, F32)] + [jax.ShapeDtypeStruct((s, D_RNN), BF16)] * len(extra),
        in_specs=[slabs, wsp, wsp, vec, vec, vec] + [slabs] * len(extra), out_specs=[slabs] + [blk] * len(extra),
        scratch_shapes=[pltpu.VMEM((8, CG), F32)],
        compiler_params=_params(),
    )(xc, wa, wx, ba, bx, lam, *extra)


def lru_bwd(xc, dhs, hs, wa, wx, ba, bx, lam, reverse, rider=None):
    s = xc.shape[1]
    tt = _blk(s, TT_RNN)
    nt = s // tt
    ng = tt // GRP
    back = not reverse

    def rows_of(v, g, i):
        return v[GRP * g + 8 * i:GRP * g + 8 * i + 8]

    def neighbour(v, past, edge, sub):
        out = []
        for g in range(ng):
            for i in range(SEG):
                if past and i > 0:
                    r = rows_of(v, g, i - 1)
                elif past:
                    e = edge if g == 0 else rows_of(v, g - 1, SEG - 1)[7:8]
                    r = jnp.where(sub == 0, e, pltpu.roll(rows_of(v, g, SEG - 1), 1, 0))
                elif i < SEG - 1:
                    r = rows_of(v, g, i + 1)
                else:
                    e = edge if g == ng - 1 else rows_of(v, g + 1, 0)[0:1]
                    r = jnp.where(sub == 7, e, pltpu.roll(rows_of(v, g, 0), 7, 0))
                out.append(r)
        return jnp.concatenate(out, axis=0)

    def body(x_ref, dh_ref, hs_ref, nb_ref, wa_ref, wx_ref, ba_ref, bx_ref, lam_ref,
             dx_ref, dwa_ref, dwx_ref, dba_ref, dbx_ref, dlam_ref, carry):
        t = pl.program_id(1)

        @pl.when(t == 0)
        def _():
            carry[...] = jnp.zeros(carry.shape, F32)
            dwa_ref[...] = jnp.zeros(dwa_ref.shape, F32)
            dwx_ref[...] = jnp.zeros(dwx_ref.shape, F32)
            dba_ref[...] = jnp.zeros(dba_ref.shape, F32)
            dbx_ref[...] = jnp.zeros(dbx_ref.shape, F32)
            dlam_ref[...] = jnp.zeros(dlam_ref.shape, F32)

        xv = _perm_load(x_ref, tt)
        xb, r, i, sp, sg, a, mult, rmult = _lru_gates(xv, wa_ref, wx_ref, ba_ref, bx_ref, lam_ref)
        sub = lax.broadcasted_iota(jnp.int32, (8, CG), 0)
        hsv = _perm_load(hs_ref, tt)
        nbv = _slab_rows(nb_ref)
        inner = t < nt - 1
        h_edge = jnp.where(inner, nbv[0:1, :] if reverse else nbv[7:8, :], 0.0)
        hprev = neighbour(hsv, not reverse, h_edge, sub)
        a_next = neighbour(a, reverse, carry[8:9, :], sub)
        dhv = _perm_load(dh_ref, tt)
        gl = [None] * (ng * SEG)

        def emit(gi, j, rows):
            gl[gi * SEG + j] = rows

        carry[0:8, :] = _perm_scan(a_next, dhv, carry[0:8, :], back, emit)
        g = jnp.concatenate(gl, axis=0)
        carry[8:9, :] = a[0:1, :] if back else a[tt - 1:tt, :]

        da = g * hprev
        dmult = g * (i * xv)
        di = g * mult * xv
        dla = da * a - dmult * (a * a) * rmult
        dpa = (dla * (-LRU_C * sp)) * r * (1.0 - r)
        dpx = di * i * (1.0 - i)
        dlam_ref[...] += _colsum(dla * (LRU_C * r * sg))
        dba_ref[...] += _colsum(dpa)
        dbx_ref[...] += _colsum(dpx)
        dpab, dpxb = dpa.astype(BF16), dpx.astype(BF16)
        dxv = g * mult * i + _nt(dpab, wa_ref[...]) + _nt(dpxb, wx_ref[...])
        for gi in range(ng):
            for j in range(SEG):
                for l in range(SLABS):
                    dx_ref.at[l][pl.ds(GRP * gi + j, 8, stride=SEG), :] = rows_of(dxv, gi, j)[:, 128 * l:128 * (l + 1)]
        dwa_ref[...] += _tn(xb, dpab)
        dwx_ref[...] += _tn(xb, dpxb)

    tmap, blk, slabs, wsp, vec = _lru_specs(nt, tt, back)
    per8 = tt // 8
    if reverse:
        nb = pl.BlockSpec((SLABS, 8, 128), lambda g, t: (g, jnp.minimum((tmap(t) + 1) * per8, s // 8 - 1), 0))
    else:
        nb = pl.BlockSpec((SLABS, 8, 128), lambda g, t: (g, jnp.maximum(tmap(t) * per8 - 1, 0), 0))
    return _call(
        body, name="lru_bwd_rev" if reverse else "lru_bwd", grid=(N_CG, nt),
        out_shape=[jax.ShapeDtypeStruct(xc.shape, F32), jax.ShapeDtypeStruct((N_CG, CG, CG), F32),
                   jax.ShapeDtypeStruct((N_CG, CG, CG), F32)] + [jax.ShapeDtypeStruct((1, D_RNN), F32)] * 3,
        in_specs=[slabs, slabs, slabs, nb, wsp, wsp, vec, vec, vec],
        out_specs=[slabs, wsp, wsp, vec, vec, vec],
        scratch_shapes=[pltpu.VMEM((16, CG), F32)],
        args=(xc, dhs, hs, hs, wa, wx, ba, bx, lam), rider=rider)


def _ln_part_bwd(dxo, x, y, gate, g, sums_ref, loss_head=None):
    xhat, rstd = _ln_stats(ALPHA * x + (1.0 + gate) * y)
    if loss_head is not None:
        err = xhat * g + loss_head[0] - loss_head[1]
        dxo = err * (1.0 / D_MODEL)
        sums_ref[5:6, :] += _colsum(err * err)
    dz = _ln_bwd(dxo, xhat, rstd, g)
    sums_ref[2:3, :] += _colsum(dz * y)
    sums_ref[3:4, :] += _colsum(dxo * xhat)
    sums_ref[4:5, :] += _colsum(dxo)
    return dz


def mlp_bwd(dxo, x, y, ra, mods, k, w1_t, w2, lng, lnb=None, rider=None):
    s = x.shape[0]
    tm = _blk(s, TM_MLP)
    head = lnb is not None

    def body(d_ref, x_ref, y_ref, ra_ref, mod_ref, w1_ref, w2_ref, g_ref, *rest):
        b_ref = rest[0] if head else None
        dx_ref, da_ref, h_ref, dy_ref, sums_ref = rest[1:] if head else rest

        @pl.when(pl.program_id(0) == 0)
        def _():
            sums_ref[...] = jnp.zeros(sums_ref.shape, F32)

        xv = x_ref[...]
        shift, scale, gate = _mod(mod_ref, k)
        if head:
            dz = _ln_part_bwd(None, xv, y_ref[...], gate, g_ref[...], sums_ref, (b_ref[...], d_ref[...]))
        else:
            dz = _ln_part_bwd(d_ref[...], xv, y_ref[...], gate, g_ref[...], sums_ref)
        dyb = (dz * (1.0 + gate)).astype(BF16)
        dy_ref[...] = dyb
        h = (xv * (1.0 + scale) + shift).astype(BF16)
        h_ref[...] = h
        dh = jnp.zeros((tm, D_MODEL), F32)
        for c in range(D_FF // FF_CHUNK):
            rows = slice(FF_CHUNK * c, FF_CHUNK * (c + 1))
            da = (_nt(dyb, w2_ref[rows, :]) * (2.0 * ra_ref[:, rows].astype(F32))).astype(BF16)
            da_ref[:, rows] = da
            dh = dh + _nn(da, w1_ref[rows, :])
        dx_ref[...] = ALPHA * dz + dh * (1.0 + scale)
        sums_ref[0:1, :] += _colsum(dh)
        sums_ref[1:2, :] += _colsum(dh * xv)

    return _call(
        body, name="mlp_bwd", grid=(s // tm,),
        out_shape=[jax.ShapeDtypeStruct((s, D_MODEL), F32), jax.ShapeDtypeStruct((s, D_FF), BF16),
                   jax.ShapeDtypeStruct((s, D_MODEL), BF16),
                   jax.ShapeDtypeStruct((s, D_MODEL), BF16), jax.ShapeDtypeStruct((8, D_MODEL), F32)],
        in_specs=[_row(tm, D_MODEL)] * 3 + [_row(tm, D_FF), _res(mods.shape), _res(w1_t.shape), _res(w2.shape),
                                             _res(lng.shape)] + ([_res(lnb.shape)] if head else []),
        out_specs=[_row(tm, D_MODEL), _row(tm, D_FF), _row(tm, D_MODEL), _row(tm, D_MODEL), _res((8, D_MODEL))],
        args=(dxo, x, y, ra, mods, w1_t, w2, lng) + ((lnb,) if head else ()), rider=rider)


def post_bwd(dxo, x, y, mods, k, w, lng, gate_act=None, rider=None):
    s = x.shape[0]
    tm = _blk(s, TM_MM)
    kdim = w.shape[0]
    rnn = gate_act is not None

    def body(*refs):
        if rnn:
            (d_ref, x_ref, y_ref, mod_ref, w_ref, g_ref, gt_ref, hs_ref,
             dres_ref, dy_ref, sums_ref, dhs_ref, dgt_ref) = refs
        else:
            d_ref, x_ref, y_ref, mod_ref, w_ref, g_ref, dres_ref, dy_ref, sums_ref, dyp_ref = refs

        @pl.when(pl.program_id(0) == 0)
        def _():
            sums_ref[...] = jnp.zeros(sums_ref.shape, F32)

        _, _, gate = _mod(mod_ref, k)
        dz = _ln_part_bwd(d_ref[...], x_ref[...], y_ref[...], gate, g_ref[...], sums_ref)
        dres_ref[...] = ALPHA * dz
        dyb = (dz * (1.0 + gate)).astype(BF16)
        dy_ref[...] = dyb
        dyp = _nt(dyb, w_ref[...])
        if rnn:
            act, dact = _gelu_parts(gt_ref[...].astype(F32))
            dhs = dyp * act
            for l in range(kdim // 128):
                dhs_ref[l] = dhs[:, 128 * l:128 * (l + 1)]
            dgt_ref[...] = (dyp * hs_ref[...].astype(F32) * dact).astype(BF16)
        else:
            dyp_ref[...] = dyp.astype(BF16)

    ins = [dxo, x, y, mods, w, lng] + (list(gate_act) if rnn else [])
    in_specs = [_row(tm, D_MODEL)] * 3 + [_res(mods.shape), _res(w.shape), _res(lng.shape)]
    out_shape = [jax.ShapeDtypeStruct((s, D_MODEL), F32), jax.ShapeDtypeStruct((s, D_MODEL), BF16),
                 jax.ShapeDtypeStruct((8, D_MODEL), F32)]
    out_specs = [_row(tm, D_MODEL), _row(tm, D_MODEL), _res((8, D_MODEL))]
    if rnn:
        in_specs += [_row(tm, kdim)] * 2
        out_shape += [jax.ShapeDtypeStruct((kdim // 128, s, 128), F32), jax.ShapeDtypeStruct((s, kdim), BF16)]
        out_specs += [pl.BlockSpec((kdim // 128, tm, 128), lambda i: (0, i, 0)), _row(tm, kdim)]
    else:
        out_shape.append(jax.ShapeDtypeStruct((s, kdim), BF16))
        out_specs.append(_row(tm, kdim))
    return _call(
        body, name="rnn_post_bwd" if rnn else "attn_post_bwd", grid=(s // tm,),
        out_shape=out_shape, in_specs=in_specs, out_specs=out_specs, args=ins, rider=rider)


def attn_bwd(q, kk, v, do, sinks, rider=None):
    s = q.shape[0]
    nblk = s // QBLK
    scale = HEAD ** -0.5

    def body(sink_ref, q_ref, do_ref, kp, ko, kn, vp, vo, vn, dq_ref, dk_ref, dv_ref, ds_ref):
        n = pl.program_id(0)

        @pl.when(n == 0)
        def _():
            ds_ref[...] = jnp.zeros(ds_ref.shape, F32)
            dk_ref[...] = jnp.zeros(dk_ref.shape, F32)
            dv_ref[...] = jnp.zeros(dv_ref.shape, F32)

        kall = jnp.concatenate([kp[...], ko[...], kn[...]], axis=0)
        vall = jnp.concatenate([vp[...], vo[...], vn[...]], axis=0)
        lane = lax.broadcasted_iota(jnp.int32, (1, 128), 1)
        dsink = jnp.zeros((1, 128), F32)
        for qb in range(QPAIR):
            nb = QPAIR * n + qb
            valid = _attn_mask(nb, s)
            keys = slice(QBLK * qb, QBLK * (qb + 3))
            for kv in range(N_KV):
                cols = slice(HEAD * kv, HEAD * (kv + 1))
                qg, dog = _stack_heads(q_ref, qb, kv), _stack_heads(do_ref, qb, kv)
                kh, vh = kall[keys, cols], vall[keys, cols]
                probs, psink = _attn_probs(qg, kh, valid, _stack_sinks(sink_ref, kv))
                dprobs = _nt(dog, vh)
                dvp = _tn(probs.astype(BF16), dog)
                rowdot = jnp.sum(probs * dprobs, axis=-1, keepdims=True)
                dsb = (probs * (dprobs - rowdot) * scale).astype(BF16)
                dqg = _nn(dsb, kh)
                dkp = _tn(dsb, qg)
                for p in range(3):
                    blk = jnp.clip(nb - 1 + p, 0, nblk - 1)
                    rows = pl.ds(pl.multiple_of(blk * QBLK, QBLK), QBLK)
                    dk_ref[rows, cols] += dkp[QBLK * p:QBLK * (p + 1), :]
                    dv_ref[rows, cols] += dvp[QBLK * p:QBLK * (p + 1), :]
                dsk = -psink * rowdot
                for j in range(GROUP):
                    hq = GROUP * kv + j
                    dq_ref[QBLK * qb:QBLK * (qb + 1), HEAD * hq:HEAD * (hq + 1)] = dqg[QBLK * j:QBLK * (j + 1), :]
                    dsink = dsink + jnp.where(lane == hq, _colsum(dsk[QBLK * j:QBLK * (j + 1), :]), 0.0)
        ds_ref[...] += dsink

    qspec = pl.BlockSpec((QPAIR * QBLK, N_Q * HEAD), lambda n: (n, 0))
    return _call(
        body, name="attn_bwd", grid=(nblk // QPAIR,),
        out_shape=[jax.ShapeDtypeStruct((s, N_Q * HEAD), F32),
                   jax.ShapeDtypeStruct((s, N_KV * HEAD), F32), jax.ShapeDtypeStruct((s, N_KV * HEAD), F32),
                   jax.ShapeDtypeStruct((1, 128), F32)],
        in_specs=[pl.BlockSpec(memory_space=pltpu.SMEM), qspec, qspec] + _kv_specs(nblk) + _kv_specs(nblk),
        out_specs=[qspec, _res((s, N_KV * HEAD)), _res((s, N_KV * HEAD)), pl.BlockSpec((1, 128), lambda n: (0, 0))],
        args=(sinks, q, do, kk, kk, kk, v, v, v), rider=rider)


def _in_bwd_tail(dzb, w_ref, x_ref, mod_ref, k, dres_ref, dx_ref, h_ref, sums_ref):
    xv = x_ref[...]
    shift, scale, _ = _mod(mod_ref, k)
    h_ref[...] = (xv * (1.0 + scale) + shift).astype(BF16)
    dh = _nn(dzb, w_ref[...])
    dx_ref[...] = dres_ref[...] + dh * (1.0 + scale)
    sums_ref[0:1, :] += _colsum(dh)
    sums_ref[1:2, :] += _colsum(dh * xv)


def attn_in_bwd(dq, dk, dv, rope, x, mods, k, win_t, dres):
    s = x.shape[0]
    tm = _blk(s, TM_MM)

    def body(dq_ref, dk_ref, dv_ref, c_ref, s1_ref, s2_ref, x_ref, mod_ref, w_ref, dres_ref,
             dx_ref, dz_ref, h_ref, sums_ref):
        @pl.when(pl.program_id(0) == 0)
        def _():
            sums_ref[...] = jnp.zeros(sums_ref.shape, F32)

        cos, s1, s2 = c_ref[...], s1_ref[...], s2_ref[...]
        for hh in range(N_Q + N_KV):
            src = dq_ref[:, HEAD * hh:HEAD * (hh + 1)] if hh < N_Q else dk_ref[:, HEAD * (hh - N_Q):HEAD * (hh - N_Q + 1)]
            dz_ref[:, HEAD * hh:HEAD * (hh + 1)] = _rope_bwd(src, cos, s1, s2).astype(BF16)
        dz_ref[:, HEAD * (N_Q + N_KV):] = dv_ref[...].astype(BF16)
        _in_bwd_tail(dz_ref[...], w_ref, x_ref, mod_ref, k, dres_ref, dx_ref, h_ref, sums_ref)

    return pl.pallas_call(
        body, name="attn_in_bwd", grid=(s // tm,),
        out_shape=[jax.ShapeDtypeStruct((s, D_MODEL), F32), jax.ShapeDtypeStruct((s, D_QKV), BF16),
                   jax.ShapeDtypeStruct((s, D_MODEL), BF16), jax.ShapeDtypeStruct((8, D_MODEL), F32)],
        in_specs=[_row(tm, N_Q * HEAD), _row(tm, N_KV * HEAD), _row(tm, N_KV * HEAD),
                  _row(tm, HEAD), _row(tm, HEAD), _row(tm, HEAD), _row(tm, D_MODEL),
                  _res(mods.shape), _res(win_t.shape), _row(tm, D_MODEL)],
        out_specs=[_row(tm, D_MODEL), _row(tm, D_QKV), _row(tm, D_MODEL), _res((8, D_MODEL))],
        compiler_params=_params(),
    )(dq, dk, dv, *rope, x, mods, win_t, dres)


def _shift_blk(v, k, before, after, row):
    n = v.shape[0]
    r = pltpu.roll(v, k % n, 0)
    for j in range(abs(k)):
        if k > 0:
            r = jnp.where(row == j, before[8 - k + j:8 - k + j + 1, :], r)
        else:
            r = jnp.where(row == n + k + j, after[j:j + 1, :], r)
    return r


def rnn_in_bwd(dxc_f, dxc_b, xr, cw, dgt, x, mods, k, win_t, dres):
    s = x.shape[0]
    tm = _blk(s, TM_MM)
    n = s // tm

    def body(f_ref, fp_ref, fn_ref, b_ref, bp_ref, bn_ref, xr_ref, xp_ref, xn_ref, cw_ref, dgt_ref,
             x_ref, mod_ref, w_ref, dres_ref, dx_ref, dz_ref, h_ref, sums_ref, dcw_ref, dcb_ref):
        i = pl.program_id(0)

        @pl.when(i == 0)
        def _():
            sums_ref[...] = jnp.zeros(sums_ref.shape, F32)
            dcw_ref[...] = jnp.zeros(dcw_ref.shape, F32)
            dcb_ref[...] = jnp.zeros(dcb_ref.shape, F32)

        d = _slab_rows(f_ref) + _slab_rows(b_ref)
        xv = xr_ref[...]
        first, last = i == 0, i == n - 1
        d_before = jnp.where(first, 0.0, _slab_rows(fp_ref) + _slab_rows(bp_ref))
        d_after = jnp.where(last, 0.0, _slab_rows(fn_ref) + _slab_rows(bn_ref))
        x_before = jnp.where(first, 0.0, xp_ref[...])
        x_after = jnp.where(last, 0.0, xn_ref[...])
        row = lax.broadcasted_iota(jnp.int32, d.shape, 0)
        dxr = (cw_ref[0:1, :] * _shift_blk(d, -2, d_before, d_after, row)
               + cw_ref[1:2, :] * _shift_blk(d, -1, d_before, d_after, row)
               + cw_ref[2:3, :] * d + cw_ref[3:4, :] * _shift_blk(d, 1, d_before, d_after, row))
        dcw_ref[0:1, :] += _colsum(d * _shift_blk(xv, 2, x_before, x_after, row))
        dcw_ref[1:2, :] += _colsum(d * _shift_blk(xv, 1, x_before, x_after, row))
        dcw_ref[2:3, :] += _colsum(d * xv)
        dcw_ref[3:4, :] += _colsum(d * _shift_blk(xv, -1, x_before, x_after, row))
        dcb_ref[...] += _colsum(d)
        dz_ref[:, 0:D_RNN] = dxr.astype(BF16)
        dz_ref[:, D_RNN:2 * D_RNN] = dgt_ref[...]
        _in_bwd_tail(dz_ref[...], w_ref, x_ref, mod_ref, k, dres_ref, dx_ref, h_ref, sums_ref)

    per8 = tm // 8
    ns = D_RNN // 128
    blk = _row(tm, D_RNN)
    before = pl.BlockSpec((8, D_RNN), lambda i: (jnp.maximum(i * per8 - 1, 0), 0))
    after = pl.BlockSpec((8, D_RNN), lambda i: (jnp.minimum((i + 1) * per8, s // 8 - 1), 0))
    sblk = pl.BlockSpec((ns, tm, 128), lambda i: (0, i, 0))
    sbefore = pl.BlockSpec((ns, 8, 128), lambda i: (0, jnp.maximum(i * per8 - 1, 0), 0))
    safter = pl.BlockSpec((ns, 8, 128), lambda i: (0, jnp.minimum((i + 1) * per8, s // 8 - 1), 0))
    return pl.pallas_call(
        body, name="rnn_in_bwd", grid=(n,),
        out_shape=[jax.ShapeDtypeStruct((s, D_MODEL), F32), jax.ShapeDtypeStruct((s, 2 * D_RNN), BF16),
                   jax.ShapeDtypeStruct((s, D_MODEL), BF16), jax.ShapeDtypeStruct((8, D_MODEL), F32),
                   jax.ShapeDtypeStruct((4, D_RNN), F32), jax.ShapeDtypeStruct((1, D_RNN), F32)],
        in_specs=[sblk, sbefore, safter] * 2 + [blk, before, after] + [
            _res(cw.shape), blk, _row(tm, D_MODEL), _res(mods.shape), _res(win_t.shape), _row(tm, D_MODEL)],
        out_specs=[_row(tm, D_MODEL), _row(tm, 2 * D_RNN), _row(tm, D_MODEL), _res((8, D_MODEL)),
                   _res((4, D_RNN)), _res((1, D_RNN))],
        compiler_params=_params(),
    )(dxc_f, dxc_f, dxc_f, dxc_b, dxc_b, dxc_b, xr, xr, xr, cw, dgt, x, mods, win_t, dres)


def wgrad(a, b, name, rider=None):
    s, m = a.shape
    n = b.shape[1]
    tm = next(t for t in (1024, 768, 512, 384, 256, 128) if m % t == 0)
    tk = _blk(s, TK_WG)
    nk = s // tk

    def body(a_ref, b_ref, o_ref, acc):
        kk = pl.program_id(1)

        @pl.when(kk == 0)
        def _():
            acc[...] = jnp.zeros(acc.shape, F32)

        acc[...] += _tn(a_ref[...], b_ref[...])

        @pl.when(kk == nk - 1)
        def _():
            o_ref[...] = acc[...].astype(BF16)

    out, *rode = _call(
        body, name=name, grid=(m // tm, nk),
        out_shape=[jax.ShapeDtypeStruct((m, n), BF16)],
        in_specs=[pl.BlockSpec((tk, tm), lambda i, kk: (kk, i)), pl.BlockSpec((tk, n), lambda i, kk: (kk, 0))],
        out_specs=[pl.BlockSpec((tm, n), lambda i, kk: (i, 0))],
        scratch_shapes=[pltpu.VMEM((tm, n), F32)],
        args=(a, b), rider=rider)
    out = out.reshape(N_DEV, m // N_DEV, n)
    return (out, *rode) if rider is not None else out


def part_sum(parts, name):
    _, r, c = parts.shape
    tr = next(t for t in (256, 192, 128, 64, 32, 16, 8) if r % t == 0)

    def body(p_ref, o_ref):
        acc = p_ref[0].astype(F32)
        for j in range(1, N_DEV):
            acc = acc + p_ref[j].astype(F32)
        o_ref[...] = acc

    return pl.pallas_call(
        body, name=name, grid=(r // tr,),
        out_shape=jax.ShapeDtypeStruct((r, c), F32),
        in_specs=[pl.BlockSpec((N_DEV, tr, c), lambda i: (0, i, 0))],
        out_specs=pl.BlockSpec((tr, c), lambda i: (i, 0)),
        compiler_params=_params(),
    )(parts)


def adamw(w, g, m, v, name):
    shape = w.shape
    c = shape[-1]
    r = w.size // c
    w2, g2, m2, v2 = (t.reshape(r, c) for t in (w, g, m, v))
    tr = r if r * c <= 512 * 1024 else next(t for t in (512, 256, 128, 64, 32, 16, 8) if r % t == 0)

    def body(w_ref, g_ref, m_ref, v_ref, d_ref, nm_ref, nv_ref):
        gv = g_ref[...]
        nm = B1 * m_ref[...] + (1.0 - B1) * gv
        nv = B2 * v_ref[...] + (1.0 - B2) * (gv * gv)
        nm_ref[...] = nm
        nv_ref[...] = nv
        m_hat = nm / (1.0 - B1 ** STEP)
        v_hat = nv / (1.0 - B2 ** STEP)
        d_ref[...] = -LR * (m_hat / (jnp.sqrt(v_hat) + ADAM_EPS) + WD * w_ref[...])

    spec = pl.BlockSpec((tr, c), lambda i: (i, 0))
    outs = pl.pallas_call(
        body, name=name, grid=(r // tr,),
        out_shape=[jax.ShapeDtypeStruct((r, c), F32)] * 3,
        in_specs=[spec] * 4, out_specs=[spec] * 3,
        compiler_params=_params(),
    )(w2, g2, m2, v2)
    return tuple(o.reshape(shape) for o in outs)


def _rope_tables(s):
    half = ROT // 2
    inv_freq = THETA ** (-jnp.arange(0, ROT, 2, dtype=F32) / ROT)
    per_row = 128 // half
    pos = (per_row * jnp.arange(s // per_row)[:, None] + jnp.arange(128)[None, :] // half).astype(F32)
    ang = pos * jnp.tile(inv_freq, per_row)[None, :]
    cos, sin = lax.optimization_barrier((jnp.cos(ang), jnp.sin(ang)))
    cos, sin = cos.reshape(s, half), sin.reshape(s, half)
    zeros = jnp.zeros((s, HEAD - ROT), F32)
    c = jnp.concatenate([cos, cos, jnp.ones((s, HEAD - ROT), F32)], axis=1)
    s1 = jnp.concatenate([jnp.zeros((s, half), F32), sin, zeros], axis=1)
    s2 = jnp.concatenate([-sin, jnp.zeros((s, half), F32), zeros], axis=1)
    return c, s1, s2


def _blockdiag(w):
    w4 = w.reshape(N_CG, 4, RB_W, RB_W)
    eye = jnp.eye(4, dtype=w.dtype)
    return jnp.einsum("gipq,ij->gipjq", w4, eye).reshape(N_CG, CG, CG)


def _diag_blocks(w):
    w5 = w.reshape(N_CG, 4, RB_W, 4, RB_W)
    eye = jnp.eye(4, dtype=w.dtype)
    return jnp.einsum("gipjq,ij->gipq", w5, eye).reshape(N_RB, RB_W, RB_W)


def _cols(full, per):
    lead = full.shape[:-1]
    t = full.reshape(lead + (N_DEV, per))
    return jnp.moveaxis(t, -2, 0).reshape(N_DEV, -1)


def kernel(x, c, ada_w, ada_b, ln_g, ln_b, attn_w_in, attn_w_out, attn_sinks, rnn_w_in, rnn_conv_w, rnn_conv_b, rnn_w_a, rnn_b_a, rnn_w_x, rnn_b_x, rnn_lam, rnn_w_out, mlp_w1, mlp_w2, loss_target, m_ada_w, m_ada_b, m_ln_g, m_ln_b, m_attn_w_in, m_attn_w_out, m_attn_sinks, m_rnn_w_in, m_rnn_conv_w, m_rnn_conv_b, m_rnn_w_a, m_rnn_b_a, m_rnn_w_x, m_rnn_b_x, m_rnn_lam, m_rnn_w_out, m_mlp_w1, m_mlp_w2, v_ada_w, v_ada_b, v_ln_g, v_ln_b, v_attn_w_in, v_attn_w_out, v_attn_sinks, v_rnn_w_in, v_rnn_conv_w, v_rnn_conv_b, v_rnn_w_a, v_rnn_b_a, v_rnn_w_x, v_rnn_b_x, v_rnn_lam, v_rnn_w_out, v_mlp_w1, v_mlp_w2):
    s = x.shape[1]
    x0 = x.reshape(s, D_MODEL)
    target = loss_target.reshape(s, D_MODEL)
    weights = dict(ada_w=ada_w, ada_b=ada_b, ln_g=ln_g, ln_b=ln_b, attn_w_in=attn_w_in, attn_w_out=attn_w_out,
                   attn_sinks=attn_sinks, rnn_w_in=rnn_w_in, rnn_conv_w=rnn_conv_w, rnn_conv_b=rnn_conv_b,
                   rnn_w_a=rnn_w_a, rnn_b_a=rnn_b_a, rnn_w_x=rnn_w_x, rnn_b_x=rnn_b_x, rnn_lam=rnn_lam,
                   rnn_w_out=rnn_w_out, mlp_w1=mlp_w1, mlp_w2=mlp_w2)
    moments_m = dict(ada_w=m_ada_w, ada_b=m_ada_b, ln_g=m_ln_g, ln_b=m_ln_b, attn_w_in=m_attn_w_in,
                     attn_w_out=m_attn_w_out, attn_sinks=m_attn_sinks, rnn_w_in=m_rnn_w_in,
                     rnn_conv_w=m_rnn_conv_w, rnn_conv_b=m_rnn_conv_b, rnn_w_a=m_rnn_w_a, rnn_b_a=m_rnn_b_a,
                     rnn_w_x=m_rnn_w_x, rnn_b_x=m_rnn_b_x, rnn_lam=m_rnn_lam, rnn_w_out=m_rnn_w_out,
                     mlp_w1=m_mlp_w1, mlp_w2=m_mlp_w2)
    moments_v = dict(ada_w=v_ada_w, ada_b=v_ada_b, ln_g=v_ln_g, ln_b=v_ln_b, attn_w_in=v_attn_w_in,
                     attn_w_out=v_attn_w_out, attn_sinks=v_attn_sinks, rnn_w_in=v_rnn_w_in,
                     rnn_conv_w=v_rnn_conv_w, rnn_conv_b=v_rnn_conv_b, rnn_w_a=v_rnn_w_a, rnn_b_a=v_rnn_b_a,
                     rnn_w_x=v_rnn_w_x, rnn_b_x=v_rnn_b_x, rnn_lam=v_rnn_lam, rnn_w_out=v_rnn_w_out,
                     mlp_w1=v_mlp_w1, mlp_w2=v_mlp_w2)
    names = list(weights)

    def t16(w):
        return w.T.astype(BF16)

    big = [t16(attn_w_in[0]), attn_w_out[0].astype(BF16), t16(rnn_w_in[0]), rnn_w_out[0].astype(BF16),
           t16(mlp_w1[0]), mlp_w2[0].astype(BF16), t16(mlp_w1[1]), mlp_w2[1].astype(BF16)]
    small_local = jnp.concatenate([
        ln_g.reshape(-1), ln_b.reshape(-1), rnn_conv_w.reshape(-1), rnn_conv_b.reshape(-1),
        rnn_b_a.reshape(-1), rnn_b_x.reshape(-1), rnn_lam.reshape(-1)])
    small_local = jnp.pad(small_local, (0, 4096 - small_local.shape[0])).reshape(32, 128)
    flat = lambda g: g.reshape(N_DEV * g.shape[1], D_MODEL)
    c_all, modr, win_t, sm = ada_modulation(jnp.broadcast_to(c, (8, D_MODEL)), ada_w.reshape(4, D_MODEL, CG),
                                            ada_b.reshape(4, 1, CG), _Gather([big[0], small_local]))
    win_t = flat(win_t)
    sm = sm.reshape(N_DEV, 4096)

    def full_vec(off, rows, per):
        piece = sm[:, off:off + rows * per].reshape(N_DEV, rows, per)
        return jnp.moveaxis(piece, 0, 1).reshape(rows, N_DEV * per)

    lng_f, lnb_f = full_vec(0, 4, 128), full_vec(512, 4, 128)
    cw_f, cb_f = full_vec(1024, 4, 192), full_vec(1792, 1, 192)
    ba_f, bx_f, lam_f = full_vec(1984, 2, 192), full_vec(2368, 2, 192), full_vec(2752, 2, 192)
    wa_bd = [_blockdiag(rnn_w_a[0, d]).astype(BF16) for d in range(2)]
    wx_bd = [_blockdiag(rnn_w_x[0, d]).astype(BF16) for d in range(2)]

    mods = modr.reshape(N_DEV, 4, 8, CG)[:, :, 0, :]
    mods = jnp.moveaxis(mods, 0, 1).reshape(4, 3, D_MODEL).reshape(12, D_MODEL)
    rope = _rope_tables(s)
    ln = lambda k: (lng_f[k:k + 1], lnb_f[k:k + 1])

    q, kk, v, wout, w1t_0 = attn_in_fwd(x0, mods, 0, win_t, rope, rider=_Gather([big[1], big[4]]))
    wout, w1t_0 = flat(wout), flat(w1t_0)
    o, w2_0 = attn_fwd(q, kk, v, attn_sinks, rider=_Gather([big[5]]))
    w2_0 = flat(w2_0)
    x1, y0 = post_fwd(o, wout, x0, mods, 0, *ln(0))
    x2, y1, ra0, r0, *got = mlp_fwd(x1, mods, 1, w1t_0, w2_0, *ln(1),
                                    rider=_Gather([big[3], big[2], big[6], big[7]]))
    rout, rin_t, w1t_1, w2_1 = (flat(g) for g in got)
    xr, gt = rnn_in_fwd(x2, mods, 2, rin_t)
    xc = conv_fwd(xr, cw_f, cb_f)
    hf, = lru_fwd(xc, wa_bd[0], wx_bd[0], ba_f[0:1], bx_f[0:1], lam_f[0:1], False)
    hb, hsum = lru_fwd(xc, wa_bd[1], wx_bd[1], ba_f[1:2], bx_f[1:2], lam_f[1:2], True, other=hf)
    x3, y2, ypre = post_fwd(None, rout, x2, mods, 2, *ln(2), gate_act=(gt, hsum))
    y3, ra1, r1 = mlp_fwd(x3, mods, 3, w1t_1, w2_1, *ln(3), last=True)

    dx3, da1, h3, dy3, sums3 = mlp_bwd(target, x3, y3, ra1, mods, 3, w1t_1, w2_1, lng_f[3:4], lnb=lnb_f[3:4])
    g_w1t_1 = wgrad(da1, h3, "wgrad_w1_1")
    g_w2_1 = wgrad(r1, dy3, "wgrad_w2_1")
    dres2, dy2, sums2a, dhs, dgt, p_w1t_1 = post_bwd(dx3, x2, y2, mods, 2, rout, lng_f[2:3], gate_act=(gt, hsum),
                                                     rider=_AllToAll([g_w1t_1]))
    g_rout = wgrad(ypre, dy2, "wgrad_rnn_out")
    dxc_f, dwa_f, dwx_f, dba_f, dbx_f, dlam_f, p_w2_1, p_rout = lru_bwd(
        xc, dhs, hf, wa_bd[0], wx_bd[0], ba_f[0:1], bx_f[0:1], lam_f[0:1], False, rider=_AllToAll([g_w2_1, g_rout]))
    dxc_b, dwa_b, dwx_b, dba_b, dbx_b, dlam_b = lru_bwd(xc, dhs, hb, wa_bd[1], wx_bd[1], ba_f[1:2], bx_f[1:2],
                                                        lam_f[1:2], True)
    dx2, dzz, h2, sums2b, dcw, dcb = rnn_in_bwd(dxc_f, dxc_b, xr, cw_f, dgt, x2, mods, 2, rin_t, dres2)
    g_rin_t = wgrad(dzz, h2, "wgrad_rnn_in")
    d_wa = jnp.stack([_diag_blocks(dwa_f), _diag_blocks(dwa_b)])
    d_wx = jnp.stack([_diag_blocks(dwx_f), _diag_blocks(dwx_b)])
    nflat = d_wa.size // N_DEV
    gates = jnp.concatenate([d_wa.reshape(N_DEV, nflat), d_wx.reshape(N_DEV, nflat)], axis=1)
    gates = gates.reshape(N_DEV, 2 * nflat // 128, 128)
    dx1, da0, h1, dy1, sums1, p_rin_t, p_gates = mlp_bwd(dx2, x1, y1, ra0, mods, 1, w1t_0, w2_0, lng_f[1:2],
                                                         rider=_AllToAll([g_rin_t, gates]))
    gates_sum = part_sum(p_gates, "part_sum_gates")
    g_w1t_0 = wgrad(da0, h1, "wgrad_w1_0")
    g_w2_0, p_w1t_0 = wgrad(r0, dy1, "wgrad_w2_0", rider=_AllToAll([g_w1t_0]))
    dres0, dy0, sums0a, do = post_bwd(dx1, x0, y0, mods, 0, wout, lng_f[0:1])
    g_wout = wgrad(o, dy0, "wgrad_attn_out")
    dq, dk, dv, dsink, wag, p_w2_0 = attn_bwd(
        q, kk, v, do, attn_sinks, rider=_Multi(_Gather([gates_sum]), _AllToAll([g_w2_0])))
    dx0, dqkv, h0, sums0b = attn_in_bwd(dq, dk, dv, rope, x0, mods, 0, win_t, dres0)
    g_win_t, p_wout = wgrad(dqkv, h0, "wgrad_attn_in", rider=_AllToAll([g_wout]))

    sums = [sums0a + sums0b, sums1, sums2a + sums2b, sums3]
    gmod = jnp.stack([t[0:3] for t in sums])
    gsend = jnp.moveaxis(gmod.reshape(4, N_DEV, CG), 1, 0)
    gsend = jnp.pad(gsend, ((0, 0), (0, 4), (0, 0)))
    c_t = c_all[:, 0, :].T
    sq_err = jnp.sum(sums3[5]).reshape(1, 1)
    tail = jnp.concatenate([
        _cols(dcw, 192), _cols(dcb, 192),
        _cols(jnp.concatenate([dba_f, dba_b]), 192), _cols(jnp.concatenate([dbx_f, dbx_b]), 192),
        _cols(jnp.concatenate([dlam_f, dlam_b]), 192),
        _cols(jnp.stack([t[3] for t in sums]), 128), _cols(jnp.stack([t[4] for t in sums]), 128),
        jnp.broadcast_to(dsink[:, 0:8], (N_DEV, 8)), jnp.broadcast_to(sq_err, (N_DEV, 1))], axis=1)
    tail = jnp.pad(tail, ((0, 0), (0, 32 * 128 - tail.shape[1]))).reshape(N_DEV, 32, 128)
    g_ada_w, g_ada_b, red, p_win_t = epilogue(gsend, c_t, tail, _AllToAll([g_win_t]))
    grads = {"ada_w": g_ada_w.reshape(ada_w.shape), "ada_b": g_ada_b[0:4].reshape(ada_b.shape)}

    big_parts = [p_win_t, p_wout, p_rin_t, p_rout, p_w1t_0, p_w2_0, p_w1t_1, p_w2_1]
    gsum = [part_sum(p, "part_sum_%d" % i) for i, p in enumerate(big_parts)]
    grads.update({
        "attn_w_in": gsum[0].T[None], "attn_w_out": gsum[1][None],
        "rnn_w_in": gsum[2].T[None], "rnn_w_out": gsum[3][None],
        "mlp_w1": jnp.stack([gsum[4].T, gsum[6].T]), "mlp_w2": jnp.stack([gsum[5], gsum[7]]),
    })
    wag = wag.reshape(N_DEV, 2 * nflat)
    grads["rnn_w_a"] = wag[:, :nflat].reshape(rnn_w_a.shape)
    grads["rnn_w_x"] = wag[:, nflat:].reshape(rnn_w_x.shape)
    tl = red.reshape(-1)
    loss = 0.5 * tl[3144] / D_MODEL
    grads["rnn_conv_w"] = tl[0:768].reshape(rnn_conv_w.shape)
    grads["rnn_conv_b"] = tl[768:960].reshape(rnn_conv_b.shape)
    grads["rnn_b_a"] = tl[960:1344].reshape(rnn_b_a.shape)
    grads["rnn_b_x"] = tl[1344:1728].reshape(rnn_b_x.shape)
    grads["rnn_lam"] = tl[1728:2112].reshape(rnn_lam.shape)
    grads["ln_g"] = tl[2112:2624].reshape(ln_g.shape)
    grads["ln_b"] = tl[2624:3136].reshape(ln_b.shape)
    grads["attn_sinks"] = tl[3136:3144].reshape(attn_sinks.shape)

    delta, new_m, new_v = {}, {}, {}
    for n in names:
        delta[n], new_m[n], new_v[n] = adamw(weights[n], grads[n], moments_m[n], moments_v[n], "adamw_" + n)
    return (loss, dx0.reshape(x.shape), *[grads[n] for n in names], *[delta[n] for n in names],
            *[new_m[n] for n in names], *[new_v[n] for n in names])
```

```python
import functools
import math

import jax
import jax.numpy as jnp
from jax import lax
from jax.experimental import pallas as pl
from jax.experimental.pallas import tpu as pltpu

F32, BF16 = jnp.float32, jnp.bfloat16
MESH = pl.DeviceIdType.MESH

D_MODEL = 1024
N_Q, N_KV, HEAD = 8, 2, 128
ROT, THETA = 32, 500000.0
QBLK = 128
D_QKV = (N_Q + 2 * N_KV) * HEAD
D_RNN, N_RB, RB_W = 1536, 16, 96
CG = 384
N_CG = D_RNN // CG
D_FF = 4096
FF_CHUNK = 1024
DEPTH = 2
ALPHA = (2.0 * DEPTH) ** 0.25
LN_EPS = 1e-5
LRU_C = 8.0
N_DEV = 8
LR, B1, B2, ADAM_EPS, WD, STEP = 0.001, 0.9, 0.999, 1e-8, 0.01, 10

VMEM_LIMIT = 56 * 1024 * 1024
TM_MM = 512
TM_MLP = 256
TM_MLP_FWD = 512
TT_RNN = 2048
TK_WG = 4096


def _nn(a, b):
    return jnp.dot(a, b, preferred_element_type=F32)


def _nt(a, b):
    return lax.dot_general(a, b, (((1,), (1,)), ((), ())), preferred_element_type=F32)


def _tn(a, b):
    return lax.dot_general(a, b, (((0,), (0,)), ((), ())), preferred_element_type=F32)


def _blk(n, pref):
    t = min(n, pref)
    assert n % t == 0, (n, pref)
    return t


def _params(**kw):
    return pltpu.CompilerParams(vmem_limit_bytes=VMEM_LIMIT, **kw)


def _row(tm, w):
    return pl.BlockSpec((tm, w), lambda i: (i, 0))


def _res(shape):
    return pl.BlockSpec(shape, lambda i: (0,) * len(shape), pipeline_mode=pl.Buffered(1))


def _mod(mod_ref, k):
    return mod_ref[3 * k:3 * k + 1, :], mod_ref[3 * k + 1:3 * k + 2, :], mod_ref[3 * k + 2:3 * k + 3, :]


def _ln_stats(z):
    mu = jnp.mean(z, axis=-1, keepdims=True)
    zc = z - mu
    var = jnp.mean(zc * zc, axis=-1, keepdims=True)
    rstd = lax.rsqrt(var + LN_EPS)
    return zc * rstd, rstd


def _ln_bwd(dxo, xhat, rstd, g):
    dxh = dxo * g
    m1 = jnp.mean(dxh, axis=-1, keepdims=True)
    m2 = jnp.mean(dxh * xhat, axis=-1, keepdims=True)
    return rstd * (dxh - m1 - xhat * m2)


def _colsum(v):
    return jnp.sum(v, axis=0, keepdims=True)


def _sigmoid(v):
    return 0.5 * jnp.tanh(0.5 * v) + 0.5


def _gelu_parts(v):
    k = math.sqrt(2.0 / math.pi)
    u = k * (v + 0.044715 * v * v * v)
    t = jnp.tanh(u)
    g = 0.5 * v * (1.0 + t)
    dg = 0.5 * (1.0 + t) + 0.5 * v * (1.0 - t * t) * k * (1.0 + 3.0 * 0.044715 * v * v)
    return g, dg


def _me():
    return lax.axis_index("x"), lax.axis_index("y"), lax.axis_index("c")


def _idx(p):
    return 4 * p[0] + 2 * p[1] + p[2]


def _peers(me):
    x, y, c = me
    out = []
    for k in range(1, N_DEV):
        out.append((1 - x if k & 4 else x, 1 - y if k & 2 else y, 1 - c if k & 1 else c))
    return out


class _Gather:
    def __init__(self, srcs):
        self.srcs = list(srcs)
        n = len(self.srcs)
        self.out_shape = [jax.ShapeDtypeStruct((N_DEV,) + s.shape, s.dtype) for s in self.srcs]
        self.scratch = [pltpu.SemaphoreType.DMA((n, 7)), pltpu.SemaphoreType.DMA((n, 7)),
                        pltpu.SemaphoreType.DMA((n,))]

    @staticmethod
    def _places():
        x, y, c = me = _me()
        return me, (x, y, 1 - c), [(1 - x, y), (x, 1 - y), (1 - x, 1 - y)]

    @staticmethod
    def _copy(outs, sems, t, k, block, to, src=None):
        slot = outs[t].at[_idx(block)]
        return pltpu.make_async_remote_copy(
            src_ref=slot if src is None else src, dst_ref=slot, send_sem=sems[0].at[t, k],
            recv_sem=sems[1].at[t, k], device_id=to, device_id_type=MESH)

    def _firsts(self, ins, outs, sems):
        me, sibling, chips = self._places()
        out = []
        for t in range(len(ins)):
            out.append(self._copy(outs, sems, t, 0, me, sibling, src=ins[t]))
            out += [self._copy(outs, sems, t, 1 + j, me, (*chip, me[2]), src=ins[t]) for j, chip in enumerate(chips)]
        return out

    def _locals(self, ins, outs, sems):
        me = _me()
        return [pltpu.make_async_copy(ins[t], outs[t].at[_idx(me)], sems[2].at[t]) for t in range(len(ins))]

    def start(self, ins, outs, sems):
        for cp in self._locals(ins, outs, sems) + self._firsts(ins, outs, sems):
            cp.start()

    def mid(self, ins, outs, sems):
        me, sibling, chips = self._places()
        for j, chip in enumerate(chips):
            for t in range(len(ins)):
                self._copy(outs, sems, t, 1 + j, (*chip, me[2]), me).wait_recv()
                self._copy(outs, sems, t, 4 + j, (*chip, me[2]), sibling).start()

    def finish(self, ins, outs, sems):
        me, sibling, chips = self._places()
        for t in range(len(ins)):
            self._copy(outs, sems, t, 0, sibling, me).wait_recv()
            for j, chip in enumerate(chips):
                self._copy(outs, sems, t, 4 + j, (*chip, 1 - me[2]), me).wait_recv()
        for cp in self._firsts(ins, outs, sems):
            cp.wait_send()
        for j, chip in enumerate(chips):
            for t in range(len(ins)):
                self._copy(outs, sems, t, 4 + j, (*chip, me[2]), sibling).wait_send()
        for cp in self._locals(ins, outs, sems):
            cp.wait()


class _AllToAll:
    def __init__(self, srcs):
        self.srcs = list(srcs)
        n = len(self.srcs)
        self.out_shape = [jax.ShapeDtypeStruct(s.shape, s.dtype) for s in self.srcs]
        self.scratch = [pltpu.SemaphoreType.DMA((n, 7)), pltpu.SemaphoreType.DMA((n, 7)),
                        pltpu.SemaphoreType.DMA((n,))]

    def _copies(self, ins, outs, sems):
        me = _me()
        loc, rem = [], []
        for t in range(len(ins)):
            loc.append(pltpu.make_async_copy(ins[t].at[_idx(me)], outs[t].at[_idx(me)], sems[2].at[t]))
            for k, p in enumerate(_peers(me)):
                rem.append(pltpu.make_async_remote_copy(
                    src_ref=ins[t].at[_idx(p)], dst_ref=outs[t].at[_idx(me)], send_sem=sems[0].at[t, k],
                    recv_sem=sems[1].at[t, k], device_id=p, device_id_type=MESH))
        return loc, rem

    def start(self, ins, outs, sems):
        loc, rem = self._copies(ins, outs, sems)
        for cp in loc + rem:
            cp.start()

    def mid(self, ins, outs, sems):
        pass

    def finish(self, ins, outs, sems):
        me = _me()
        for t in range(len(ins)):
            for k, p in enumerate(_peers(me)):
                slot = outs[t].at[_idx(p)]
                pltpu.make_async_remote_copy(
                    src_ref=slot, dst_ref=slot, send_sem=sems[0].at[t, k], recv_sem=sems[1].at[t, k],
                    device_id=p, device_id_type=MESH).wait_recv()
        loc, rem = self._copies(ins, outs, sems)
        for cp in rem:
            cp.wait_send()
        for cp in loc:
            cp.wait()


class _Multi:
    def __init__(self, *exs):
        self.exs = exs
        self.srcs = [s for e in exs for s in e.srcs]
        self.out_shape = [s for e in exs for s in e.out_shape]
        self.scratch = [s for e in exs for s in e.scratch]

    def _each(self, ins, outs, sems):
        i = j = 0
        for e in self.exs:
            n, m = len(e.srcs), len(e.scratch)
            yield e, ins[i:i + n], outs[i:i + n], sems[j:j + m]
            i, j = i + n, j + m

    def start(self, ins, outs, sems):
        for e, a, b, c in self._each(ins, outs, sems):
            e.start(a, b, c)

    def mid(self, ins, outs, sems):
        for e, a, b, c in self._each(ins, outs, sems):
            e.mid(a, b, c)

    def finish(self, ins, outs, sems):
        for e, a, b, c in self._each(ins, outs, sems):
            e.finish(a, b, c)


def _call(body, *, name, grid, in_specs, out_specs, out_shape, args, scratch_shapes=(), rider=None):
    in_specs, out_specs, out_shape = list(in_specs), list(out_specs), list(out_shape)
    scratch_shapes = list(scratch_shapes)
    if rider is None:
        return pl.pallas_call(body, name=name, grid=grid, out_shape=out_shape, in_specs=in_specs,
                              out_specs=out_specs, scratch_shapes=scratch_shapes, compiler_params=_params())(*args)
    nci, nco, ncs, nr = len(in_specs), len(out_shape), len(scratch_shapes), len(rider.srcs)
    nsteps = math.prod(grid)
    assert nsteps >= 2, (name, grid)
    mid = max(1, (7 * nsteps) // 8)

    def full(*refs):
        ci, ri = refs[:nci], refs[nci:nci + nr]
        co, ro = refs[nci + nr:nci + nr + nco], refs[nci + nr + nco:nci + 2 * nr + nco]
        cs, rs = refs[nci + 2 * nr + nco:nci + 2 * nr + nco + ncs], refs[nci + 2 * nr + nco + ncs:]
        step = pl.program_id(0)
        for d in range(1, len(grid)):
            step = step * grid[d] + pl.program_id(d)

        @pl.when(step == 0)
        def _():
            rider.start(ri, ro, rs)

        @pl.when(step == mid)
        def _():
            rider.mid(ri, ro, rs)

        body(*ci, *co, *cs)

        @pl.when(step == nsteps - 1)
        def _():
            rider.finish(ri, ro, rs)

    any_spec = pl.BlockSpec(memory_space=pl.ANY)
    return pl.pallas_call(
        full, name=name, grid=grid, out_shape=out_shape + rider.out_shape,
        in_specs=in_specs + [any_spec] * nr, out_specs=out_specs + [any_spec] * nr,
        scratch_shapes=scratch_shapes + rider.scratch, compiler_params=_params(),
    )(*args, *rider.srcs)


def _a2a_start(srcs, dsts, send_sems, recv_sems, local_sems, me, sem_base=0):
    peers = _peers(me)
    started = []
    for t in range(len(srcs)):
        loc = pltpu.make_async_copy(srcs[t].at[_idx(me)], dsts[t].at[_idx(me)], local_sems.at[sem_base + t])
        loc.start()
        started.append(("local", loc))
        for k, p in enumerate(peers):
            cp = pltpu.make_async_remote_copy(
                src_ref=srcs[t].at[_idx(p)], dst_ref=dsts[t].at[_idx(me)],
                send_sem=send_sems.at[sem_base + t, k], recv_sem=recv_sems.at[sem_base + t, k],
                device_id=p, device_id_type=MESH)
            cp.start()
            started.append(("remote", cp))
    return started


def _a2a_finish(started, dsts, send_sems, recv_sems, me, sem_base=0):
    peers = _peers(me)
    for t in range(len(dsts)):
        for k, p in enumerate(peers):
            slot = dsts[t].at[_idx(p)]
            pltpu.make_async_remote_copy(
                src_ref=slot, dst_ref=slot, send_sem=send_sems.at[sem_base + t, k],
                recv_sem=recv_sems.at[sem_base + t, k], device_id=p, device_id_type=MESH).wait_recv()
    for kind, cp in started:
        if kind == "local":
            cp.wait()
        else:
            cp.wait_send()


def ada_modulation(c8, ada_w, ada_b, ride):
    nr = len(ride.srcs)

    def body(c_ref, w_ref, b_ref, *rest):
        ride_in, (call_ref, modr_ref), ride_out = rest[:nr], rest[nr:nr + 2], rest[nr + 2:2 * nr + 2]
        modp, send_sems, recv_sems, local_sems = rest[2 * nr + 2:2 * nr + 6]
        ride_sems = rest[2 * nr + 6:]
        ride.start(ride_in, ride_out, ride_sems)
        me = _me()
        peers = _peers(me)
        sends = []
        for k, p in enumerate(peers):
            cp = pltpu.make_async_remote_copy(
                src_ref=c_ref, dst_ref=call_ref.at[_idx(me)], send_sem=send_sems.at[0, k],
                recv_sem=recv_sems.at[0, k], device_id=p, device_id_type=MESH)
            cp.start()
            sends.append(cp)
        call_ref[_idx(me)] = c_ref[...]
        for k, p in enumerate(peers):
            slot = call_ref.at[_idx(p)]
            pltpu.make_async_remote_copy(
                src_ref=slot, dst_ref=slot, send_sem=send_sems.at[0, k], recv_sem=recv_sems.at[0, k],
                device_id=p, device_id_type=MESH).wait_recv()
        for cp in sends:
            cp.wait_send()
        cv = call_ref[...].reshape(N_DEV * 8, D_MODEL)
        s = (cv * _sigmoid(cv)).astype(BF16)
        for k in range(4):
            res = _nn(s, w_ref[k].astype(BF16)) + b_ref[k]
            for j in range(N_DEV):
                modp[j, 8 * k:8 * k + 8, :] = res[8 * j:8 * j + 8, :]
        started = _a2a_start([modp], [modr_ref], send_sems, recv_sems, local_sems, me, sem_base=1)
        _a2a_finish(started, [modr_ref], send_sems, recv_sems, me, sem_base=1)
        ride.mid(ride_in, ride_out, ride_sems)
        ride.finish(ride_in, ride_out, ride_sems)

    vm, hbm = pl.BlockSpec(memory_space=pltpu.VMEM), pl.BlockSpec(memory_space=pl.ANY)
    return pl.pallas_call(
        body, name="ada_modulation",
        out_shape=[jax.ShapeDtypeStruct((N_DEV, 8, D_MODEL), F32), jax.ShapeDtypeStruct((N_DEV, 32, CG), F32)]
        + ride.out_shape,
        in_specs=[vm, vm, vm] + [hbm] * nr, out_specs=[vm, vm] + [hbm] * nr,
        scratch_shapes=[pltpu.VMEM((N_DEV, 32, CG), F32), pltpu.SemaphoreType.DMA((2, 7)),
                        pltpu.SemaphoreType.DMA((2, 7)), pltpu.SemaphoreType.DMA((2,))] + ride.scratch,
        compiler_params=_params(),
    )(c8, ada_w, ada_b, *ride.srcs)


def epilogue(gsend, c_t, tail, ride):
    nr = len(ride.srcs)
    rt = tail.shape[1]

    def body(g_ref, ct_ref, t_ref, *rest):
        ride_in, (gw_ref, gb_ref, red_ref), ride_out = rest[:nr], rest[nr:nr + 3], rest[nr + 3:2 * nr + 3]
        grecv, trecv, send_sems, recv_sems, local_sems = rest[2 * nr + 3:2 * nr + 8]
        ride_sems = rest[2 * nr + 8:]
        ride.start(ride_in, ride_out, ride_sems)
        me = _me()
        started = _a2a_start([g_ref, t_ref], [grecv, trecv], send_sems, recv_sems, local_sems, me)
        _a2a_finish(started, [grecv, trecv], send_sems, recv_sems, me)
        acc = trecv[0]
        for j in range(1, N_DEV):
            acc = acc + trecv[j]
        red_ref[...] = acc
        ct = ct_ref[...]
        st = (ct * _sigmoid(ct)).astype(BF16).astype(F32)
        gb = jnp.zeros((8, CG), F32)
        for b in range(N_DEV):
            gb = gb + grecv[b]
        gb_ref[...] = gb
        for k in range(4):
            acc = jnp.zeros((D_MODEL, CG), F32)
            for b in range(N_DEV):
                row = grecv[b, k:k + 1, :].astype(BF16).astype(F32)
                acc = acc + st[:, b:b + 1] * row
            gw_ref[k] = acc
        ride.mid(ride_in, ride_out, ride_sems)
        ride.finish(ride_in, ride_out, ride_sems)

    vm, hbm = pl.BlockSpec(memory_space=pltpu.VMEM), pl.BlockSpec(memory_space=pl.ANY)
    return pl.pallas_call(
        body, name="epilogue",
        out_shape=[jax.ShapeDtypeStruct((4, D_MODEL, CG), F32), jax.ShapeDtypeStruct((8, CG), F32),
                   jax.ShapeDtypeStruct((rt, 128), F32)] + ride.out_shape,
        in_specs=[vm, vm, vm] + [hbm] * nr, out_specs=[vm, vm, vm] + [hbm] * nr,
        scratch_shapes=[pltpu.VMEM((N_DEV, 8, CG), F32), pltpu.VMEM((N_DEV, rt, 128), F32),
                        pltpu.SemaphoreType.DMA((2, 7)), pltpu.SemaphoreType.DMA((2, 7)),
                        pltpu.SemaphoreType.DMA((2,))] + ride.scratch,
        compiler_params=_params(),
    )(gsend, c_t, tail, *ride.srcs)


def _rope(t, cos, s1, s2):
    return t * cos + pltpu.roll(t, 16, 1) * s1 + pltpu.roll(t, HEAD - 16, 1) * s2


def _rope_bwd(d, cos, s1, s2):
    return d * cos + pltpu.roll(d * s1, HEAD - 16, 1) + pltpu.roll(d * s2, 16, 1)


def attn_in_fwd(x, mods, k, win_t, rope, rider=None):
    s = x.shape[0]
    tm = _blk(s, TM_MM)

    def body(x_ref, mod_ref, w_ref, c_ref, s1_ref, s2_ref, q_ref, k_ref, v_ref):
        shift, scale, _ = _mod(mod_ref, k)
        h = (x_ref[...] * (1.0 + scale) + shift).astype(BF16)
        qkv = _nt(h, w_ref[...])
        cos, s1, s2 = c_ref[...], s1_ref[...], s2_ref[...]
        for hh in range(N_Q + N_KV):
            r = _rope(qkv[:, HEAD * hh:HEAD * (hh + 1)], cos, s1, s2).astype(BF16)
            if hh < N_Q:
                q_ref[:, HEAD * hh:HEAD * (hh + 1)] = r
            else:
                k_ref[:, HEAD * (hh - N_Q):HEAD * (hh - N_Q + 1)] = r
        v_ref[...] = qkv[:, HEAD * (N_Q + N_KV):].astype(BF16)

    return _call(
        body, name="attn_in_fwd", grid=(s // tm,),
        out_shape=[jax.ShapeDtypeStruct((s, N_Q * HEAD), BF16), jax.ShapeDtypeStruct((s, N_KV * HEAD), BF16),
                   jax.ShapeDtypeStruct((s, N_KV * HEAD), BF16)],
        in_specs=[_row(tm, D_MODEL), _res(mods.shape), _res(win_t.shape),
                  _row(tm, HEAD), _row(tm, HEAD), _row(tm, HEAD)],
        out_specs=[_row(tm, N_Q * HEAD), _row(tm, N_KV * HEAD), _row(tm, N_KV * HEAD)],
        args=(x, mods, win_t, *rope), rider=rider)


QPAIR = 8


def _kv_specs(nblk):
    w = N_KV * HEAD
    return [pl.BlockSpec((QBLK, w), lambda n: (jnp.maximum(QPAIR * n - 1, 0), 0)),
            pl.BlockSpec((QPAIR * QBLK, w), lambda n: (n, 0)),
            pl.BlockSpec((QBLK, w), lambda n: (jnp.minimum(QPAIR * (n + 1), nblk - 1), 0))]


GROUP = N_Q // N_KV


def _attn_mask(n, s):
    qi = lax.broadcasted_iota(jnp.int32, (GROUP * QBLK, 3 * QBLK), 0) & (QBLK - 1)
    kj = lax.broadcasted_iota(jnp.int32, (GROUP * QBLK, 3 * QBLK), 1)
    rel = kj - QBLK - qi
    kpos = kj + (n - 1) * QBLK
    return (jnp.abs(rel) <= QBLK) & (kpos >= 0) & (kpos < s)


def _stack_heads(ref, qb, kv):
    rows = slice(QBLK * qb, QBLK * (qb + 1))
    return jnp.concatenate([ref[rows, HEAD * (GROUP * kv + j):HEAD * (GROUP * kv + j + 1)] for j in range(GROUP)],
                           axis=0)


def _stack_sinks(sink_ref, kv):
    row = lax.broadcasted_iota(jnp.int32, (GROUP * QBLK, 1), 0)
    out = jnp.full((GROUP * QBLK, 1), sink_ref[0, GROUP * kv + GROUP - 1], F32)
    for j in range(GROUP - 2, -1, -1):
        out = jnp.where(row < QBLK * (j + 1), sink_ref[0, GROUP * kv + j], out)
    return out


def _attn_probs(qh, kh, valid, sink):
    sc = _nt(qh, kh) * (HEAD ** -0.5)
    sc = jnp.where(valid, sc, -1e30)
    m = jnp.maximum(jnp.max(sc, axis=-1, keepdims=True), sink)
    p = jnp.exp(sc - m)
    es = jnp.exp(sink - m)
    denom = jnp.sum(p, axis=-1, keepdims=True) + es
    return p / denom, es / denom


def attn_fwd(q, kk, v, sinks, rider=None):
    s = q.shape[0]
    nblk = s // QBLK

    def body(sink_ref, q_ref, kp, ko, kn, vp, vo, vn, o_ref):
        n = pl.program_id(0)
        kall = jnp.concatenate([kp[...], ko[...], kn[...]], axis=0)
        vall = jnp.concatenate([vp[...], vo[...], vn[...]], axis=0)
        for qb in range(QPAIR):
            valid = _attn_mask(QPAIR * n + qb, s)
            keys = slice(QBLK * qb, QBLK * (qb + 3))
            for kv in range(N_KV):
                cols = slice(HEAD * kv, HEAD * (kv + 1))
                probs, _ = _attn_probs(_stack_heads(q_ref, qb, kv), kall[keys, cols], valid,
                                       _stack_sinks(sink_ref, kv))
                og = _nn(probs.astype(BF16), vall[keys, cols]).astype(BF16)
                for j in range(GROUP):
                    hq = GROUP * kv + j
                    o_ref[QBLK * qb:QBLK * (qb + 1), HEAD * hq:HEAD * (hq + 1)] = og[QBLK * j:QBLK * (j + 1), :]

    qspec = pl.BlockSpec((QPAIR * QBLK, N_Q * HEAD), lambda n: (n, 0))
    return _call(
        body, name="attn_fwd", grid=(nblk // QPAIR,),
        out_shape=[jax.ShapeDtypeStruct((s, N_Q * HEAD), BF16)],
        in_specs=[pl.BlockSpec(memory_space=pltpu.SMEM), qspec] + _kv_specs(nblk) + _kv_specs(nblk),
        out_specs=[qspec],
        args=(sinks, q, kk, kk, kk, v, v, v), rider=rider)


def post_fwd(ypre, w, x, mods, k, lng, lnb, gate_act=None, rider=None):
    s = x.shape[0]
    tm = _blk(s, TM_MM)
    kdim = w.shape[0]
    rnn = gate_act is not None

    def body(*refs):
        if rnn:
            gt_ref, hs_ref, w_ref, x_ref, mod_ref, g_ref, b_ref, xo_ref, y_ref, yp_ref = refs
            act, _ = _gelu_parts(gt_ref[...].astype(F32))
            yp = (hs_ref[...].astype(F32) * act).astype(BF16)
            yp_ref[...] = yp
        else:
            yp_ref, w_ref, x_ref, mod_ref, g_ref, b_ref, xo_ref, y_ref = refs
            yp = yp_ref[...]
        _, _, gate = _mod(mod_ref, k)
        y = _nn(yp, w_ref[...])
        y_ref[...] = y
        xhat, _ = _ln_stats(ALPHA * x_ref[...] + (1.0 + gate) * y)
        xo_ref[...] = xhat * g_ref[...] + b_ref[...]

    act_in = list(gate_act) if rnn else [ypre]
    out_shape = [jax.ShapeDtypeStruct((s, D_MODEL), F32), jax.ShapeDtypeStruct((s, D_MODEL), F32)]
    out_specs = [_row(tm, D_MODEL), _row(tm, D_MODEL)]
    if rnn:
        out_shape.append(jax.ShapeDtypeStruct((s, kdim), BF16))
        out_specs.append(_row(tm, kdim))
    return _call(
        body, name="rnn_post_fwd" if rnn else "attn_post_fwd", grid=(s // tm,),
        out_shape=out_shape,
        in_specs=[_row(tm, kdim)] * len(act_in) + [_res(w.shape), _row(tm, D_MODEL), _res(mods.shape),
                                                    _res(lng.shape), _res(lnb.shape)],
        out_specs=out_specs,
        args=(*act_in, w, x, mods, lng, lnb), rider=rider)


def mlp_fwd(x, mods, k, w1_t, w2, lng, lnb, rider=None, last=False):
    s = x.shape[0]
    tm = _blk(s, TM_MLP_FWD)

    def body(x_ref, mod_ref, w1_ref, w2_ref, g_ref, b_ref, *outs):
        xo_ref = None if last else outs[0]
        y_ref, ra_ref, r_ref = outs[-3:]
        xv = x_ref[...]
        shift, scale, gate = _mod(mod_ref, k)
        h = (xv * (1.0 + scale) + shift).astype(BF16)
        y = jnp.zeros((tm, D_MODEL), F32)
        for c in range(D_FF // FF_CHUNK):
            rows = slice(FF_CHUNK * c, FF_CHUNK * (c + 1))
            a = jnp.maximum(_nt(h, w1_ref[rows, :]), 0.0)
            r = (a * a).astype(BF16)
            ra_ref[:, rows] = a.astype(BF16)
            r_ref[:, rows] = r
            y = y + _nn(r, w2_ref[rows, :])
        y_ref[...] = y
        if not last:
            xhat, _ = _ln_stats(ALPHA * xv + (1.0 + gate) * y)
            xo_ref[...] = xhat * g_ref[...] + b_ref[...]

    nf = 1 if last else 2
    return _call(
        body, name="mlp_fwd_last" if last else "mlp_fwd", grid=(s // tm,),
        out_shape=[jax.ShapeDtypeStruct((s, D_MODEL), F32)] * nf + [jax.ShapeDtypeStruct((s, D_FF), BF16)] * 2,
        in_specs=[_row(tm, D_MODEL), _res(mods.shape), _res(w1_t.shape), _res(w2.shape),
                  _res(lng.shape), _res(lnb.shape)],
        out_specs=[_row(tm, D_MODEL)] * nf + [_row(tm, D_FF)] * 2,
        args=(x, mods, w1_t, w2, lng, lnb), rider=rider)


def rnn_in_fwd(x, mods, k, win_t):
    s = x.shape[0]
    tm = _blk(s, TM_MM)

    def body(x_ref, mod_ref, w_ref, xr_ref, gt_ref):
        shift, scale, _ = _mod(mod_ref, k)
        h = (x_ref[...] * (1.0 + scale) + shift).astype(BF16)
        xr_ref[...] = _nt(h, w_ref[0:D_RNN, :])
        gt_ref[...] = _nt(h, w_ref[D_RNN:2 * D_RNN, :]).astype(BF16)

    return pl.pallas_call(
        body, name="rnn_in_fwd", grid=(s // tm,),
        out_shape=[jax.ShapeDtypeStruct((s, D_RNN), F32), jax.ShapeDtypeStruct((s, D_RNN), BF16)],
        in_specs=[_row(tm, D_MODEL), _res(mods.shape), _res(win_t.shape)],
        out_specs=[_row(tm, D_RNN)] * 2,
        compiler_params=_params(),
    )(x, mods, win_t)


def _shift_rows(v, k, row):
    n = v.shape[0]
    r = pltpu.roll(v, k % n, 0)
    keep = (row >= k) if k > 0 else (row < n + k)
    return jnp.where(keep, r, 0.0)


def conv_fwd(xr, cw, cb):
    s = xr.shape[0]

    def body(x_ref, w_ref, b_ref, o_ref):
        xv = x_ref[...]
        row = lax.broadcasted_iota(jnp.int32, xv.shape, 0)
        o_ref[...] = (b_ref[...] + w_ref[0:1, :] * _shift_rows(xv, 2, row) + w_ref[1:2, :] * _shift_rows(xv, 1, row)
                      + w_ref[2:3, :] * xv + w_ref[3:4, :] * _shift_rows(xv, -1, row))

    slab = pl.BlockSpec((s, 128), lambda j: (0, j))
    return pl.pallas_call(
        body, name="conv_fwd", grid=(D_RNN // 128,),
        out_shape=jax.ShapeDtypeStruct((D_RNN // 128, s, 128), F32),
        in_specs=[slab, pl.BlockSpec((4, 128), lambda j: (0, j)), pl.BlockSpec((1, 128), lambda j: (0, j))],
        out_specs=pl.BlockSpec((None, s, 128), lambda j: (j, 0, 0)),
        compiler_params=_params(),
    )(xr, cw, cb)


def _softplus_neg(lam):
    z = -lam
    e = jnp.exp(-jnp.abs(z))
    u = 1.0 + e
    log1p = jnp.where(u == 1.0, e, jnp.log(u) * e / jnp.where(u == 1.0, 1.0, u - 1.0))
    return jnp.maximum(z, 0.0) + log1p, 1.0 / (1.0 + jnp.exp(lam))


def _lru_gates(xv, wa_ref, wx_ref, ba_ref, bx_ref, lam_ref):
    xb = xv.astype(BF16)
    r = _sigmoid(_nn(xb, wa_ref[...]) + ba_ref[...])
    i = _sigmoid(_nn(xb, wx_ref[...]) + bx_ref[...])
    sp, sg = _softplus_neg(lam_ref[...])
    la = r * (-LRU_C * sp)
    a = jnp.exp(la)
    th = jnp.tanh(la)
    m2 = -2.0 * th / (1.0 - th)
    rmult = lax.rsqrt(jnp.maximum(m2, 1e-37))
    return xb, r, i, sp, sg, a, m2 * rmult, rmult


SLABS = CG // 128
GRP, SEG = 32, 4


def _lru_specs(nt, tt, reverse):
    tmap = (lambda t: nt - 1 - t) if reverse else (lambda t: t)
    blk = pl.BlockSpec((tt, CG), lambda g, t: (tmap(t), g))
    slabs = pl.BlockSpec((SLABS, tt, 128), lambda g, t: (g, tmap(t), 0))
    wsp = pl.BlockSpec((None, CG, CG), lambda g, t: (g, 0, 0))
    vec = pl.BlockSpec((1, CG), lambda g, t: (0, g))
    return tmap, blk, slabs, wsp, vec


def _slab_rows(ref3):
    return jnp.concatenate([ref3[l] for l in range(ref3.shape[0])], axis=1)


def _perm_load(ref3, tt):
    out = []
    for l in range(SLABS):
        r = ref3.at[l]
        out.append(jnp.concatenate([r[pl.ds(GRP * g + i, 8, stride=SEG), :]
                                    for g in range(tt // GRP) for i in range(SEG)], axis=0))
    return jnp.concatenate(out, axis=1)


def _perm_scan(a, u, carry, reverse, emit):
    n, c = a.shape
    sub = lax.broadcasted_iota(jnp.int32, (8, c), 0)
    steps = [(8 - sh, sub < 8 - sh) if reverse else (sh, sub >= sh) for sh in (1, 2, 4)]
    order = range(SEG - 1, -1, -1) if reverse else range(SEG)
    for g in (range(n // GRP - 1, -1, -1) if reverse else range(n // GRP)):
        hs, ps = [None] * SEG, [None] * SEG
        h = p = None
        for i in order:
            rows = slice(GRP * g + 8 * i, GRP * g + 8 * i + 8)
            h = u[rows] if h is None else a[rows] * h + u[rows]
            p = a[rows] if p is None else a[rows] * p
            hs[i], ps[i] = h, p
        d, f = p, h
        for rot, keep in steps:
            d_s = jnp.where(keep, pltpu.roll(d, rot, 0), 1.0)
            f_s = jnp.where(keep, pltpu.roll(f, rot, 0), 0.0)
            f = d * f_s + f
            d = d * d_s
        end = f + d * carry
        if reverse:
            init = jnp.where(sub == 7, carry, pltpu.roll(end, 7, 0))
            carry = jnp.broadcast_to(end[0:1], (8, c))
        else:
            init = jnp.where(sub == 0, carry, pltpu.roll(end, 1, 0))
            carry = jnp.broadcast_to(end[7:8], (8, c))
        for i in range(SEG):
            emit(g, i, hs[i] + ps[i] * init)
    return carry


def lru_fwd(xc, wa, wx, ba, bx, lam, reverse, other=None):
    s = xc.shape[1]
    tt = _blk(s, TT_RNN)
    nt = s // tt

    def body(x_ref, wa_ref, wx_ref, ba_ref, bx_ref, lam_ref, *rest):
        if other is None:
            hs_ref, carry = rest
        else:
            oth_ref, hs_ref, sum_ref, carry = rest

        @pl.when(pl.program_id(1) == 0)
        def _():
            carry[...] = jnp.zeros(carry.shape, F32)

        xv = _perm_load(x_ref, tt)
        _, _, i, _, _, a, mult, _ = _lru_gates(xv, wa_ref, wx_ref, ba_ref, bx_ref, lam_ref)
        u = mult * (i * xv)

        def emit(g, j, rows):
            for l in range(SLABS):
                hs_ref.at[l][pl.ds(GRP * g + j, 8, stride=SEG), :] = rows[:, 128 * l:128 * (l + 1)]

        carry[...] = _perm_scan(a, u, carry[...], reverse, emit)
        if other is not None:
            for l in range(SLABS):
                sum_ref[:, 128 * l:128 * (l + 1)] = (hs_ref[l] + oth_ref[l]).astype(BF16)

    _, blk, slabs, wsp, vec = _lru_specs(nt, tt, reverse)
    extra = [] if other is None else [other]
    return pl.pallas_call(
        body, name="lru_fwd_rev" if reverse else "lru_fwd", grid=(N_CG, nt),
        out_shape=[jax.ShapeDtypeStruct(xc.shape, F32)] + [jax.ShapeDtypeStruct((s, D_RNN), BF16)] * len(extra),
        in_specs=[slabs, wsp, wsp, vec, vec, vec] + [slabs] * len(extra), out_specs=[slabs] + [blk] * len(extra),
        scratch_shapes=[pltpu.VMEM((8, CG), F32)],
        compiler_params=_params(),
    )(xc, wa, wx, ba, bx, lam, *extra)


def lru_bwd(xc, dhs, hs, wa, wx, ba, bx, lam, reverse, rider=None):
    s = xc.shape[1]
    tt = _blk(s, TT_RNN)
    nt = s // tt
    ng = tt // GRP
    back = not reverse

    def rows_of(v, g, i):
        return v[GRP * g + 8 * i:GRP * g + 8 * i + 8]

    def neighbour(v, past, edge, sub):
        out = []
        for g in range(ng):
            for i in range(SEG):
                if past and i > 0:
                    r = rows_of(v, g, i - 1)
                elif past:
                    e = edge if g == 0 else rows_of(v, g - 1, SEG - 1)[7:8]
                    r = jnp.where(sub == 0, e, pltpu.roll(rows_of(v, g, SEG - 1), 1, 0))
                elif i < SEG - 1:
                    r = rows_of(v, g, i + 1)
                else:
                    e = edge if g == ng - 1 else rows_of(v, g + 1, 0)[0:1]
                    r = jnp.where(sub == 7, e, pltpu.roll(rows_of(v, g, 0), 7, 0))
                out.append(r)
        return jnp.concatenate(out, axis=0)

    def body(x_ref, dh_ref, hs_ref, nb_ref, wa_ref, wx_ref, ba_ref, bx_ref, lam_ref,
             dx_ref, dwa_ref, dwx_ref, dba_ref, dbx_ref, dlam_ref, carry):
        t = pl.program_id(1)

        @pl.when(t == 0)
        def _():
            carry[...] = jnp.zeros(carry.shape, F32)
            dwa_ref[...] = jnp.zeros(dwa_ref.shape, F32)
            dwx_ref[...] = jnp.zeros(dwx_ref.shape, F32)
            dba_ref[...] = jnp.zeros(dba_ref.shape, F32)
            dbx_ref[...] = jnp.zeros(dbx_ref.shape, F32)
            dlam_ref[...] = jnp.zeros(dlam_ref.shape, F32)

        xv = _perm_load(x_ref, tt)
        xb, r, i, sp, sg, a, mult, rmult = _lru_gates(xv, wa_ref, wx_ref, ba_ref, bx_ref, lam_ref)
        sub = lax.broadcasted_iota(jnp.int32, (8, CG), 0)
        hsv = _perm_load(hs_ref, tt)
        nbv = _slab_rows(nb_ref)
        inner = t < nt - 1
        h_edge = jnp.where(inner, nbv[0:1, :] if reverse else nbv[7:8, :], 0.0)
        hprev = neighbour(hsv, not reverse, h_edge, sub)
        a_next = neighbour(a, reverse, carry[8:9, :], sub)
        dhv = _perm_load(dh_ref, tt)
        gl = [None] * (ng * SEG)

        def emit(gi, j, rows):
            gl[gi * SEG + j] = rows

        carry[0:8, :] = _perm_scan(a_next, dhv, carry[0:8, :], back, emit)
        g = jnp.concatenate(gl, axis=0)
        carry[8:9, :] = a[0:1, :] if back else a[tt - 1:tt, :]

        da = g * hprev
        dmult = g * (i * xv)
        di = g * mult * xv
        dla = da * a - dmult * (a * a) * rmult
        dpa = (dla * (-LRU_C * sp)) * r * (1.0 - r)
        dpx = di * i * (1.0 - i)
        dlam_ref[...] += _colsum(dla * (LRU_C * r * sg))
        dba_ref[...] += _colsum(dpa)
        dbx_ref[...] += _colsum(dpx)
        dpab, dpxb = dpa.astype(BF16), dpx.astype(BF16)
        dxv = g * mult * i + _nt(dpab, wa_ref[...]) + _nt(dpxb, wx_ref[...])
        for gi in range(ng):
            for j in range(SEG):
                for l in range(SLABS):
                    dx_ref.at[l][pl.ds(GRP * gi + j, 8, stride=SEG), :] = rows_of(dxv, gi, j)[:, 128 * l:128 * (l + 1)]
        dwa_ref[...] += _tn(xb, dpab)
        dwx_ref[...] += _tn(xb, dpxb)

    tmap, blk, slabs, wsp, vec = _lru_specs(nt, tt, back)
    per8 = tt // 8
    if reverse:
        nb = pl.BlockSpec((SLABS, 8, 128), lambda g, t: (g, jnp.minimum((tmap(t) + 1) * per8, s // 8 - 1), 0))
    else:
        nb = pl.BlockSpec((SLABS, 8, 128), lambda g, t: (g, jnp.maximum(tmap(t) * per8 - 1, 0), 0))
    return _call(
        body, name="lru_bwd_rev" if reverse else "lru_bwd", grid=(N_CG, nt),
        out_shape=[jax.ShapeDtypeStruct(xc.shape, F32), jax.ShapeDtypeStruct((N_CG, CG, CG), F32),
                   jax.ShapeDtypeStruct((N_CG, CG, CG), F32)] + [jax.ShapeDtypeStruct((1, D_RNN), F32)] * 3,
        in_specs=[slabs, slabs, slabs, nb, wsp, wsp, vec, vec, vec],
        out_specs=[slabs, wsp, wsp, vec, vec, vec],
        scratch_shapes=[pltpu.VMEM((16, CG), F32)],
        args=(xc, dhs, hs, hs, wa, wx, ba, bx, lam), rider=rider)


def _ln_part_bwd(dxo, x, y, gate, g, sums_ref, loss_head=None):
    xhat, rstd = _ln_stats(ALPHA * x + (1.0 + gate) * y)
    if loss_head is not None:
        err = xhat * g + loss_head[0] - loss_head[1]
        dxo = err * (1.0 / D_MODEL)
        sums_ref[5:6, :] += _colsum(err * err)
    dz = _ln_bwd(dxo, xhat, rstd, g)
    sums_ref[2:3, :] += _colsum(dz * y)
    sums_ref[3:4, :] += _colsum(dxo * xhat)
    sums_ref[4:5, :] += _colsum(dxo)
    return dz


def mlp_bwd(dxo, x, y, ra, mods, k, w1_t, w2, lng, lnb=None, rider=None):
    s = x.shape[0]
    tm = _blk(s, TM_MLP)
    head = lnb is not None

    def body(d_ref, x_ref, y_ref, ra_ref, mod_ref, w1_ref, w2_ref, g_ref, *rest):
        b_ref = rest[0] if head else None
        dx_ref, da_ref, h_ref, dy_ref, sums_ref = rest[1:] if head else rest

        @pl.when(pl.program_id(0) == 0)
        def _():
            sums_ref[...] = jnp.zeros(sums_ref.shape, F32)

        xv = x_ref[...]
        shift, scale, gate = _mod(mod_ref, k)
        if head:
            dz = _ln_part_bwd(None, xv, y_ref[...], gate, g_ref[...], sums_ref, (b_ref[...], d_ref[...]))
        else:
            dz = _ln_part_bwd(d_ref[...], xv, y_ref[...], gate, g_ref[...], sums_ref)
        dyb = (dz * (1.0 + gate)).astype(BF16)
        dy_ref[...] = dyb
        h = (xv * (1.0 + scale) + shift).astype(BF16)
        h_ref[...] = h
        dh = jnp.zeros((tm, D_MODEL), F32)
        for c in range(D_FF // FF_CHUNK):
            rows = slice(FF_CHUNK * c, FF_CHUNK * (c + 1))
            da = (_nt(dyb, w2_ref[rows, :]) * (2.0 * ra_ref[:, rows].astype(F32))).astype(BF16)
            da_ref[:, rows] = da
            dh = dh + _nn(da, w1_ref[rows, :])
        dx_ref[...] = ALPHA * dz + dh * (1.0 + scale)
        sums_ref[0:1, :] += _colsum(dh)
        sums_ref[1:2, :] += _colsum(dh * xv)

    return _call(
        body, name="mlp_bwd", grid=(s // tm,),
        out_shape=[jax.ShapeDtypeStruct((s, D_MODEL), F32), jax.ShapeDtypeStruct((s, D_FF), BF16),
                   jax.ShapeDtypeStruct((s, D_MODEL), BF16),
                   jax.ShapeDtypeStruct((s, D_MODEL), BF16), jax.ShapeDtypeStruct((8, D_MODEL), F32)],
        in_specs=[_row(tm, D_MODEL)] * 3 + [_row(tm, D_FF), _res(mods.shape), _res(w1_t.shape), _res(w2.shape),
                                             _res(lng.shape)] + ([_res(lnb.shape)] if head else []),
        out_specs=[_row(tm, D_MODEL), _row(tm, D_FF), _row(tm, D_MODEL), _row(tm, D_MODEL), _res((8, D_MODEL))],
        args=(dxo, x, y, ra, mods, w1_t, w2, lng) + ((lnb,) if head else ()), rider=rider)


def post_bwd(dxo, x, y, mods, k, w, lng, gate_act=None, rider=None):
    s = x.shape[0]
    tm = _blk(s, TM_MM)
    kdim = w.shape[0]
    rnn = gate_act is not None

    def body(*refs):
        if rnn:
            (d_ref, x_ref, y_ref, mod_ref, w_ref, g_ref, gt_ref, hs_ref,
             dres_ref, dy_ref, sums_ref, dhs_ref, dgt_ref) = refs
        else:
            d_ref, x_ref, y_ref, mod_ref, w_ref, g_ref, dres_ref, dy_ref, sums_ref, dyp_ref = refs

        @pl.when(pl.program_id(0) == 0)
        def _():
            sums_ref[...] = jnp.zeros(sums_ref.shape, F32)

        _, _, gate = _mod(mod_ref, k)
        dz = _ln_part_bwd(d_ref[...], x_ref[...], y_ref[...], gate, g_ref[...], sums_ref)
        dres_ref[...] = ALPHA * dz
        dyb = (dz * (1.0 + gate)).astype(BF16)
        dy_ref[...] = dyb
        dyp = _nt(dyb, w_ref[...])
        if rnn:
            act, dact = _gelu_parts(gt_ref[...].astype(F32))
            dhs = dyp * act
            for l in range(kdim // 128):
                dhs_ref[l] = dhs[:, 128 * l:128 * (l + 1)]
            dgt_ref[...] = (dyp * hs_ref[...].astype(F32) * dact).astype(BF16)
        else:
            dyp_ref[...] = dyp.astype(BF16)

    ins = [dxo, x, y, mods, w, lng] + (list(gate_act) if rnn else [])
    in_specs = [_row(tm, D_MODEL)] * 3 + [_res(mods.shape), _res(w.shape), _res(lng.shape)]
    out_shape = [jax.ShapeDtypeStruct((s, D_MODEL), F32), jax.ShapeDtypeStruct((s, D_MODEL), BF16),
                 jax.ShapeDtypeStruct((8, D_MODEL), F32)]
    out_specs = [_row(tm, D_MODEL), _row(tm, D_MODEL), _res((8, D_MODEL))]
    if rnn:
        in_specs += [_row(tm, kdim)] * 2
        out_shape += [jax.ShapeDtypeStruct((kdim // 128, s, 128), F32), jax.ShapeDtypeStruct((s, kdim), BF16)]
        out_specs += [pl.BlockSpec((kdim // 128, tm, 128), lambda i: (0, i, 0)), _row(tm, kdim)]
    else:
        out_shape.append(jax.ShapeDtypeStruct((s, kdim), BF16))
        out_specs.append(_row(tm, kdim))
    return _call(
        body, name="rnn_post_bwd" if rnn else "attn_post_bwd", grid=(s // tm,),
        out_shape=out_shape, in_specs=in_specs, out_specs=out_specs, args=ins, rider=rider)


def attn_bwd(q, kk, v, do, sinks, rider=None):
    s = q.shape[0]
    nblk = s // QBLK
    scale = HEAD ** -0.5

    def body(sink_ref, q_ref, do_ref, kp, ko, kn, vp, vo, vn, dq_ref, dk_ref, dv_ref, ds_ref):
        n = pl.program_id(0)

        @pl.when(n == 0)
        def _():
            ds_ref[...] = jnp.zeros(ds_ref.shape, F32)
            dk_ref[...] = jnp.zeros(dk_ref.shape, F32)
            dv_ref[...] = jnp.zeros(dv_ref.shape, F32)

        kall = jnp.concatenate([kp[...], ko[...], kn[...]], axis=0)
        vall = jnp.concatenate([vp[...], vo[...], vn[...]], axis=0)
        lane = lax.broadcasted_iota(jnp.int32, (1, 128), 1)
        dsink = jnp.zeros((1, 128), F32)
        for qb in range(QPAIR):
            nb = QPAIR * n + qb
            valid = _attn_mask(nb, s)
            keys = slice(QBLK * qb, QBLK * (qb + 3))
            for kv in range(N_KV):
                cols = slice(HEAD * kv, HEAD * (kv + 1))
                qg, dog = _stack_heads(q_ref, qb, kv), _stack_heads(do_ref, qb, kv)
                kh, vh = kall[keys, cols], vall[keys, cols]
                probs, psink = _attn_probs(qg, kh, valid, _stack_sinks(sink_ref, kv))
                dprobs = _nt(dog, vh)
                dvp = _tn(probs.astype(BF16), dog)
                rowdot = jnp.sum(probs * dprobs, axis=-1, keepdims=True)
                dsb = (probs * (dprobs - rowdot) * scale).astype(BF16)
                dqg = _nn(dsb, kh)
                dkp = _tn(dsb, qg)
                for p in range(3):
                    blk = jnp.clip(nb - 1 + p, 0, nblk - 1)
                    rows = pl.ds(pl.multiple_of(blk * QBLK, QBLK), QBLK)
                    dk_ref[rows, cols] += dkp[QBLK * p:QBLK * (p + 1), :]
                    dv_ref[rows, cols] += dvp[QBLK * p:QBLK * (p + 1), :]
                dsk = -psink * rowdot
                for j in range(GROUP):
                    hq = GROUP * kv + j
                    dq_ref[QBLK * qb:QBLK * (qb + 1), HEAD * hq:HEAD * (hq + 1)] = dqg[QBLK * j:QBLK * (j + 1), :]
                    dsink = dsink + jnp.where(lane == hq, _colsum(dsk[QBLK * j:QBLK * (j + 1), :]), 0.0)
        ds_ref[...] += dsink

    qspec = pl.BlockSpec((QPAIR * QBLK, N_Q * HEAD), lambda n: (n, 0))
    return _call(
        body, name="attn_bwd", grid=(nblk // QPAIR,),
        out_shape=[jax.ShapeDtypeStruct((s, N_Q * HEAD), F32),
                   jax.ShapeDtypeStruct((s, N_KV * HEAD), F32), jax.ShapeDtypeStruct((s, N_KV * HEAD), F32),
                   jax.ShapeDtypeStruct((1, 128), F32)],
        in_specs=[pl.BlockSpec(memory_space=pltpu.SMEM), qspec, qspec] + _kv_specs(nblk) + _kv_specs(nblk),
        out_specs=[qspec, _res((s, N_KV * HEAD)), _res((s, N_KV * HEAD)), pl.BlockSpec((1, 128), lambda n: (0, 0))],
        args=(sinks, q, do, kk, kk, kk, v, v, v), rider=rider)


def _in_bwd_tail(dzb, w_ref, x_ref, mod_ref, k, dres_ref, dx_ref, h_ref, sums_ref):
    xv = x_ref[...]
    shift, scale, _ = _mod(mod_ref, k)
    h_ref[...] = (xv * (1.0 + scale) + shift).astype(BF16)
    dh = _nn(dzb, w_ref[...])
    dx_ref[...] = dres_ref[...] + dh * (1.0 + scale)
    sums_ref[0:1, :] += _colsum(dh)
    sums_ref[1:2, :] += _colsum(dh * xv)


def attn_in_bwd(dq, dk, dv, rope, x, mods, k, win_t, dres):
    s = x.shape[0]
    tm = _blk(s, TM_MM)

    def body(dq_ref, dk_ref, dv_ref, c_ref, s1_ref, s2_ref, x_ref, mod_ref, w_ref, dres_ref,
             dx_ref, dz_ref, h_ref, sums_ref):
        @pl.when(pl.program_id(0) == 0)
        def _():
            sums_ref[...] = jnp.zeros(sums_ref.shape, F32)

        cos, s1, s2 = c_ref[...], s1_ref[...], s2_ref[...]
        for hh in range(N_Q + N_KV):
            src = dq_ref[:, HEAD * hh:HEAD * (hh + 1)] if hh < N_Q else dk_ref[:, HEAD * (hh - N_Q):HEAD * (hh - N_Q + 1)]
            dz_ref[:, HEAD * hh:HEAD * (hh + 1)] = _rope_bwd(src, cos, s1, s2).astype(BF16)
        dz_ref[:, HEAD * (N_Q + N_KV):] = dv_ref[...].astype(BF16)
        _in_bwd_tail(dz_ref[...], w_ref, x_ref, mod_ref, k, dres_ref, dx_ref, h_ref, sums_ref)

    return pl.pallas_call(
        body, name="attn_in_bwd", grid=(s // tm,),
        out_shape=[jax.ShapeDtypeStruct((s, D_MODEL), F32), jax.ShapeDtypeStruct((s, D_QKV), BF16),
                   jax.ShapeDtypeStruct((s, D_MODEL), BF16), jax.ShapeDtypeStruct((8, D_MODEL), F32)],
        in_specs=[_row(tm, N_Q * HEAD), _row(tm, N_KV * HEAD), _row(tm, N_KV * HEAD),
                  _row(tm, HEAD), _row(tm, HEAD), _row(tm, HEAD), _row(tm, D_MODEL),
                  _res(mods.shape), _res(win_t.shape), _row(tm, D_MODEL)],
        out_specs=[_row(tm, D_MODEL), _row(tm, D_QKV), _row(tm, D_MODEL), _res((8, D_MODEL))],
        compiler_params=_params(),
    )(dq, dk, dv, *rope, x, mods, win_t, dres)


def _shift_blk(v, k, before, after, row):
    n = v.shape[0]
    r = pltpu.roll(v, k % n, 0)
    for j in range(abs(k)):
        if k > 0:
            r = jnp.where(row == j, before[8 - k + j:8 - k + j + 1, :], r)
        else:
            r = jnp.where(row == n + k + j, after[j:j + 1, :], r)
    return r


def rnn_in_bwd(dxc_f, dxc_b, xr, cw, dgt, x, mods, k, win_t, dres):
    s = x.shape[0]
    tm = _blk(s, TM_MM)
    n = s // tm

    def body(f_ref, fp_ref, fn_ref, b_ref, bp_ref, bn_ref, xr_ref, xp_ref, xn_ref, cw_ref, dgt_ref,
             x_ref, mod_ref, w_ref, dres_ref, dx_ref, dz_ref, h_ref, sums_ref, dcw_ref, dcb_ref):
        i = pl.program_id(0)

        @pl.when(i == 0)
        def _():
            sums_ref[...] = jnp.zeros(sums_ref.shape, F32)
            dcw_ref[...] = jnp.zeros(dcw_ref.shape, F32)
            dcb_ref[...] = jnp.zeros(dcb_ref.shape, F32)

        d = _slab_rows(f_ref) + _slab_rows(b_ref)
        xv = xr_ref[...]
        first, last = i == 0, i == n - 1
        d_before = jnp.where(first, 0.0, _slab_rows(fp_ref) + _slab_rows(bp_ref))
        d_after = jnp.where(last, 0.0, _slab_rows(fn_ref) + _slab_rows(bn_ref))
        x_before = jnp.where(first, 0.0, xp_ref[...])
        x_after = jnp.where(last, 0.0, xn_ref[...])
        row = lax.broadcasted_iota(jnp.int32, d.shape, 0)
        dxr = (cw_ref[0:1, :] * _shift_blk(d, -2, d_before, d_after, row)
               + cw_ref[1:2, :] * _shift_blk(d, -1, d_before, d_after, row)
               + cw_ref[2:3, :] * d + cw_ref[3:4, :] * _shift_blk(d, 1, d_before, d_after, row))
        dcw_ref[0:1, :] += _colsum(d * _shift_blk(xv, 2, x_before, x_after, row))
        dcw_ref[1:2, :] += _colsum(d * _shift_blk(xv, 1, x_before, x_after, row))
        dcw_ref[2:3, :] += _colsum(d * xv)
        dcw_ref[3:4, :] += _colsum(d * _shift_blk(xv, -1, x_before, x_after, row))
        dcb_ref[...] += _colsum(d)
        dz_ref[:, 0:D_RNN] = dxr.astype(BF16)
        dz_ref[:, D_RNN:2 * D_RNN] = dgt_ref[...]
        _in_bwd_tail(dz_ref[...], w_ref, x_ref, mod_ref, k, dres_ref, dx_ref, h_ref, sums_ref)

    per8 = tm // 8
    ns = D_RNN // 128
    blk = _row(tm, D_RNN)
    before = pl.BlockSpec((8, D_RNN), lambda i: (jnp.maximum(i * per8 - 1, 0), 0))
    after = pl.BlockSpec((8, D_RNN), lambda i: (jnp.minimum((i + 1) * per8, s // 8 - 1), 0))
    sblk = pl.BlockSpec((ns, tm, 128), lambda i: (0, i, 0))
    sbefore = pl.BlockSpec((ns, 8, 128), lambda i: (0, jnp.maximum(i * per8 - 1, 0), 0))
    safter = pl.BlockSpec((ns, 8, 128), lambda i: (0, jnp.minimum((i + 1) * per8, s // 8 - 1), 0))
    return pl.pallas_call(
        body, name="rnn_in_bwd", grid=(n,),
        out_shape=[jax.ShapeDtypeStruct((s, D_MODEL), F32), jax.ShapeDtypeStruct((s, 2 * D_RNN), BF16),
                   jax.ShapeDtypeStruct((s, D_MODEL), BF16), jax.ShapeDtypeStruct((8, D_MODEL), F32),
                   jax.ShapeDtypeStruct((4, D_RNN), F32), jax.ShapeDtypeStruct((1, D_RNN), F32)],
        in_specs=[sblk, sbefore, safter] * 2 + [blk, before, after] + [
            _res(cw.shape), blk, _row(tm, D_MODEL), _res(mods.shape), _res(win_t.shape), _row(tm, D_MODEL)],
        out_specs=[_row(tm, D_MODEL), _row(tm, 2 * D_RNN), _row(tm, D_MODEL), _res((8, D_MODEL)),
                   _res((4, D_RNN)), _res((1, D_RNN))],
        compiler_params=_params(),
    )(dxc_f, dxc_f, dxc_f, dxc_b, dxc_b, dxc_b, xr, xr, xr, cw, dgt, x, mods, win_t, dres)


def wgrad(a, b, name, rider=None):
    s, m = a.shape
    n = b.shape[1]
    tm = next(t for t in (1024, 768, 512, 384, 256, 128) if m % t == 0)
    tk = _blk(s, TK_WG)
    nk = s // tk

    def body(a_ref, b_ref, o_ref, acc):
        kk = pl.program_id(1)

        @pl.when(kk == 0)
        def _():
            acc[...] = jnp.zeros(acc.shape, F32)

        acc[...] += _tn(a_ref[...], b_ref[...])

        @pl.when(kk == nk - 1)
        def _():
            o_ref[...] = acc[...].astype(BF16)

    out, *rode = _call(
        body, name=name, grid=(m // tm, nk),
        out_shape=[jax.ShapeDtypeStruct((m, n), BF16)],
        in_specs=[pl.BlockSpec((tk, tm), lambda i, kk: (kk, i)), pl.BlockSpec((tk, n), lambda i, kk: (kk, 0))],
        out_specs=[pl.BlockSpec((tm, n), lambda i, kk: (i, 0))],
        scratch_shapes=[pltpu.VMEM((tm, n), F32)],
        args=(a, b), rider=rider)
    out = out.reshape(N_DEV, m // N_DEV, n)
    return (out, *rode) if rider is not None else out


def part_sum(parts, name):
    _, r, c = parts.shape
    tr = next(t for t in (256, 192, 128, 64, 32, 16, 8) if r % t == 0)

    def body(p_ref, o_ref):
        acc = p_ref[0].astype(F32)
        for j in range(1, N_DEV):
            acc = acc + p_ref[j].astype(F32)
        o_ref[...] = acc

    return pl.pallas_call(
        body, name=name, grid=(r // tr,),
        out_shape=jax.ShapeDtypeStruct((r, c), F32),
        in_specs=[pl.BlockSpec((N_DEV, tr, c), lambda i: (0, i, 0))],
        out_specs=pl.BlockSpec((tr, c), lambda i: (i, 0)),
        compiler_params=_params(),
    )(parts)


def adamw(w, g, m, v, name):
    shape = w.shape
    c = shape[-1]
    r = w.size // c
    w2, g2, m2, v2 = (t.reshape(r, c) for t in (w, g, m, v))
    tr = r if r * c <= 512 * 1024 else next(t for t in (512, 256, 128, 64, 32, 16, 8) if r % t == 0)

    def body(w_ref, g_ref, m_ref, v_ref, d_ref, nm_ref, nv_ref):
        gv = g_ref[...]
        nm = B1 * m_ref[...] + (1.0 - B1) * gv
        nv = B2 * v_ref[...] + (1.0 - B2) * (gv * gv)
        nm_ref[...] = nm
        nv_ref[...] = nv
        m_hat = nm / (1.0 - B1 ** STEP)
        v_hat = nv / (1.0 - B2 ** STEP)
        d_ref[...] = -LR * (m_hat / (jnp.sqrt(v_hat) + ADAM_EPS) + WD * w_ref[...])

    spec = pl.BlockSpec((tr, c), lambda i: (i, 0))
    outs = pl.pallas_call(
        body, name=name, grid=(r // tr,),
        out_shape=[jax.ShapeDtypeStruct((r, c), F32)] * 3,
        in_specs=[spec] * 4, out_specs=[spec] * 3,
        compiler_params=_params(),
    )(w2, g2, m2, v2)
    return tuple(o.reshape(shape) for o in outs)


def _rope_tables(s):
    half = ROT // 2
    inv_freq = THETA ** (-jnp.arange(0, ROT, 2, dtype=F32) / ROT)
    per_row = 128 // half
    pos = (per_row * jnp.arange(s // per_row)[:, None] + jnp.arange(128)[None, :] // half).astype(F32)
    ang = pos * jnp.tile(inv_freq, per_row)[None, :]
    cos, sin = lax.optimization_barrier((jnp.cos(ang), jnp.sin(ang)))
    cos, sin = cos.reshape(s, half), sin.reshape(s, half)
    zeros = jnp.zeros((s, HEAD - ROT), F32)
    c = jnp.concatenate([cos, cos, jnp.ones((s, HEAD - ROT), F32)], axis=1)
    s1 = jnp.concatenate([jnp.zeros((s, half), F32), sin, zeros], axis=1)
    s2 = jnp.concatenate([-sin, jnp.zeros((s, half), F32), zeros], axis=1)
    return c, s1, s2


def _blockdiag(w):
    w4 = w.reshape(N_CG, 4, RB_W, RB_W)
    eye = jnp.eye(4, dtype=w.dtype)
    return jnp.einsum("gipq,ij->gipjq", w4, eye).reshape(N_CG, CG, CG)


def _diag_blocks(w):
    w5 = w.reshape(N_CG, 4, RB_W, 4, RB_W)
    eye = jnp.eye(4, dtype=w.dtype)
    return jnp.einsum("gipjq,ij->gipq", w5, eye).reshape(N_RB, RB_W, RB_W)


def _cols(full, per):
    lead = full.shape[:-1]
    t = full.reshape(lead + (N_DEV, per))
    return jnp.moveaxis(t, -2, 0).reshape(N_DEV, -1)


def kernel(x, c, ada_w, ada_b, ln_g, ln_b, attn_w_in, attn_w_out, attn_sinks, rnn_w_in, rnn_conv_w, rnn_conv_b, rnn_w_a, rnn_b_a, rnn_w_x, rnn_b_x, rnn_lam, rnn_w_out, mlp_w1, mlp_w2, loss_target, m_ada_w, m_ada_b, m_ln_g, m_ln_b, m_attn_w_in, m_attn_w_out, m_attn_sinks, m_rnn_w_in, m_rnn_conv_w, m_rnn_conv_b, m_rnn_w_a, m_rnn_b_a, m_rnn_w_x, m_rnn_b_x, m_rnn_lam, m_rnn_w_out, m_mlp_w1, m_mlp_w2, v_ada_w, v_ada_b, v_ln_g, v_ln_b, v_attn_w_in, v_attn_w_out, v_attn_sinks, v_rnn_w_in, v_rnn_conv_w, v_rnn_conv_b, v_rnn_w_a, v_rnn_b_a, v_rnn_w_x, v_rnn_b_x, v_rnn_lam, v_rnn_w_out, v_mlp_w1, v_mlp_w2):
    s = x.shape[1]
    x0 = x.reshape(s, D_MODEL)
    target = loss_target.reshape(s, D_MODEL)
    weights = dict(ada_w=ada_w, ada_b=ada_b, ln_g=ln_g, ln_b=ln_b, attn_w_in=attn_w_in, attn_w_out=attn_w_out,
                   attn_sinks=attn_sinks, rnn_w_in=rnn_w_in, rnn_conv_w=rnn_conv_w, rnn_conv_b=rnn_conv_b,
                   rnn_w_a=rnn_w_a, rnn_b_a=rnn_b_a, rnn_w_x=rnn_w_x, rnn_b_x=rnn_b_x, rnn_lam=rnn_lam,
                   rnn_w_out=rnn_w_out, mlp_w1=mlp_w1, mlp_w2=mlp_w2)
    moments_m = dict(ada_w=m_ada_w, ada_b=m_ada_b, ln_g=m_ln_g, ln_b=m_ln_b, attn_w_in=m_attn_w_in,
                     attn_w_out=m_attn_w_out, attn_sinks=m_attn_sinks, rnn_w_in=m_rnn_w_in,
                     rnn_conv_w=m_rnn_conv_w, rnn_conv_b=m_rnn_conv_b, rnn_w_a=m_rnn_w_a, rnn_b_a=m_rnn_b_a,
                     rnn_w_x=m_rnn_w_x, rnn_b_x=m_rnn_b_x, rnn_lam=m_rnn_lam, rnn_w_out=m_rnn_w_out,
                     mlp_w1=m_mlp_w1, mlp_w2=m_mlp_w2)
    moments_v = dict(ada_w=v_ada_w, ada_b=v_ada_b, ln_g=v_ln_g, ln_b=v_ln_b, attn_w_in=v_attn_w_in,
                     attn_w_out=v_attn_w_out, attn_sinks=v_attn_sinks, rnn_w_in=v_rnn_w_in,
                     rnn_conv_w=v_rnn_conv_w, rnn_conv_b=v_rnn_conv_b, rnn_w_a=v_rnn_w_a, rnn_b_a=v_rnn_b_a,
                     rnn_w_x=v_rnn_w_x, rnn_b_x=v_rnn_b_x, rnn_lam=v_rnn_lam, rnn_w_out=v_rnn_w_out,
                     mlp_w1=v_mlp_w1, mlp_w2=v_mlp_w2)
    names = list(weights)

    def t16(w):
        return w.T.astype(BF16)

    big = [t16(attn_w_in[0]), attn_w_out[0].astype(BF16), t16(rnn_w_in[0]), rnn_w_out[0].astype(BF16),
           t16(mlp_w1[0]), mlp_w2[0].astype(BF16), t16(mlp_w1[1]), mlp_w2[1].astype(BF16)]
    small_local = jnp.concatenate([
        ln_g.reshape(-1), ln_b.reshape(-1), rnn_conv_w.reshape(-1), rnn_conv_b.reshape(-1),
        rnn_b_a.reshape(-1), rnn_b_x.reshape(-1), rnn_lam.reshape(-1)])
    small_local = jnp.pad(small_local, (0, 4096 - small_local.shape[0])).reshape(32, 128)
    flat = lambda g: g.reshape(N_DEV * g.shape[1], D_MODEL)
    c_all, modr, win_t, sm = ada_modulation(jnp.broadcast_to(c, (8, D_MODEL)), ada_w.reshape(4, D_MODEL, CG),
                                            ada_b.reshape(4, 1, CG), _Gather([big[0], small_local]))
    win_t = flat(win_t)
    sm = sm.reshape(N_DEV, 4096)

    def full_vec(off, rows, per):
        piece = sm[:, off:off + rows * per].reshape(N_DEV, rows, per)
        return jnp.moveaxis(piece, 0, 1).reshape(rows, N_DEV * per)

    lng_f, lnb_f = full_vec(0, 4, 128), full_vec(512, 4, 128)
    cw_f, cb_f = full_vec(1024, 4, 192), full_vec(1792, 1, 192)
    ba_f, bx_f, lam_f = full_vec(1984, 2, 192), full_vec(2368, 2, 192), full_vec(2752, 2, 192)
    wa_bd = [_blockdiag(rnn_w_a[0, d]).astype(BF16) for d in range(2)]
    wx_bd = [_blockdiag(rnn_w_x[0, d]).astype(BF16) for d in range(2)]

    mods = modr.reshape(N_DEV, 4, 8, CG)[:, :, 0, :]
    mods = jnp.moveaxis(mods, 0, 1).reshape(4, 3, D_MODEL).reshape(12, D_MODEL)
    rope = _rope_tables(s)
    ln = lambda k: (lng_f[k:k + 1], lnb_f[k:k + 1])

    q, kk, v, wout, w1t_0 = attn_in_fwd(x0, mods, 0, win_t, rope, rider=_Gather([big[1], big[4]]))
    wout, w1t_0 = flat(wout), flat(w1t_0)
    o, w2_0 = attn_fwd(q, kk, v, attn_sinks, rider=_Gather([big[5]]))
    w2_0 = flat(w2_0)
    x1, y0, rout = post_fwd(o, wout, x0, mods, 0, *ln(0), rider=_Gather([big[3]]))
    rout = flat(rout)
    x2, y1, ra0, r0, *got = mlp_fwd(x1, mods, 1, w1t_0, w2_0, *ln(1), rider=_Gather([big[2], big[6], big[7]]))
    rin_t, w1t_1, w2_1 = (flat(g) for g in got)
    xr, gt = rnn_in_fwd(x2, mods, 2, rin_t)
    xc = conv_fwd(xr, cw_f, cb_f)
    hf, = lru_fwd(xc, wa_bd[0], wx_bd[0], ba_f[0:1], bx_f[0:1], lam_f[0:1], False)
    hb, hsum = lru_fwd(xc, wa_bd[1], wx_bd[1], ba_f[1:2], bx_f[1:2], lam_f[1:2], True, other=hf)
    x3, y2, ypre = post_fwd(None, rout, x2, mods, 2, *ln(2), gate_act=(gt, hsum))
    y3, ra1, r1 = mlp_fwd(x3, mods, 3, w1t_1, w2_1, *ln(3), last=True)

    dx3, da1, h3, dy3, sums3 = mlp_bwd(target, x3, y3, ra1, mods, 3, w1t_1, w2_1, lng_f[3:4], lnb=lnb_f[3:4])
    g_w1t_1 = wgrad(da1, h3, "wgrad_w1_1")
    g_w2_1 = wgrad(r1, dy3, "wgrad_w2_1")
    dres2, dy2, sums2a, dhs, dgt, p_w1t_1 = post_bwd(dx3, x2, y2, mods, 2, rout, lng_f[2:3], gate_act=(gt, hsum),
                                                     rider=_AllToAll([g_w1t_1]))
    g_rout = wgrad(ypre, dy2, "wgrad_rnn_out")
    dxc_f, dwa_f, dwx_f, dba_f, dbx_f, dlam_f, p_w2_1, p_rout = lru_bwd(
        xc, dhs, hf, wa_bd[0], wx_bd[0], ba_f[0:1], bx_f[0:1], lam_f[0:1], False, rider=_AllToAll([g_w2_1, g_rout]))
    dxc_b, dwa_b, dwx_b, dba_b, dbx_b, dlam_b = lru_bwd(xc, dhs, hb, wa_bd[1], wx_bd[1], ba_f[1:2], bx_f[1:2],
                                                        lam_f[1:2], True)
    dx2, dzz, h2, sums2b, dcw, dcb = rnn_in_bwd(dxc_f, dxc_b, xr, cw_f, dgt, x2, mods, 2, rin_t, dres2)
    g_rin_t = wgrad(dzz, h2, "wgrad_rnn_in")
    d_wa = jnp.stack([_diag_blocks(dwa_f), _diag_blocks(dwa_b)])
    d_wx = jnp.stack([_diag_blocks(dwx_f), _diag_blocks(dwx_b)])
    nflat = d_wa.size // N_DEV
    gates = jnp.concatenate([d_wa.reshape(N_DEV, nflat), d_wx.reshape(N_DEV, nflat)], axis=1)
    gates = gates.reshape(N_DEV, 2 * nflat // 128, 128)
    dx1, da0, h1, dy1, sums1, p_rin_t, p_gates = mlp_bwd(dx2, x1, y1, ra0, mods, 1, w1t_0, w2_0, lng_f[1:2],
                                                         rider=_AllToAll([g_rin_t, gates]))
    gates_sum = part_sum(p_gates, "part_sum_gates")
    g_w1t_0 = wgrad(da0, h1, "wgrad_w1_0")
    g_w2_0, p_w1t_0 = wgrad(r0, dy1, "wgrad_w2_0", rider=_AllToAll([g_w1t_0]))
    dres0, dy0, sums0a, do = post_bwd(dx1, x0, y0, mods, 0, wout, lng_f[0:1])
    g_wout = wgrad(o, dy0, "wgrad_attn_out")
    dq, dk, dv, dsink, wag, p_w2_0 = attn_bwd(
        q, kk, v, do, attn_sinks, rider=_Multi(_Gather([gates_sum]), _AllToAll([g_w2_0])))
    dx0, dqkv, h0, sums0b = attn_in_bwd(dq, dk, dv, rope, x0, mods, 0, win_t, dres0)
    g_win_t, p_wout = wgrad(dqkv, h0, "wgrad_attn_in", rider=_AllToAll([g_wout]))

    sums = [sums0a + sums0b, sums1, sums2a + sums2b, sums3]
    gmod = jnp.stack([t[0:3] for t in sums])
    gsend = jnp.moveaxis(gmod.reshape(4, N_DEV, CG), 1, 0)
    gsend = jnp.pad(gsend, ((0, 0), (0, 4), (0, 0)))
    c_t = c_all[:, 0, :].T
    sq_err = jnp.sum(sums3[5]).reshape(1, 1)
    tail = jnp.concatenate([
        _cols(dcw, 192), _cols(dcb, 192),
        _cols(jnp.concatenate([dba_f, dba_b]), 192), _cols(jnp.concatenate([dbx_f, dbx_b]), 192),
        _cols(jnp.concatenate([dlam_f, dlam_b]), 192),
        _cols(jnp.stack([t[3] for t in sums]), 128), _cols(jnp.stack([t[4] for t in sums]), 128),
        jnp.broadcast_to(dsink[:, 0:8], (N_DEV, 8)), jnp.broadcast_to(sq_err, (N_DEV, 1))], axis=1)
    tail = jnp.pad(tail, ((0, 0), (0, 32 * 128 - tail.shape[1]))).reshape(N_DEV, 32, 128)
    g_ada_w, g_ada_b, red, p_win_t = epilogue(gsend, c_t, tail, _AllToAll([g_win_t]))
    grads = {"ada_w": g_ada_w.reshape(ada_w.shape), "ada_b": g_ada_b[0:4].reshape(ada_b.shape)}

    big_parts = [p_win_t, p_wout, p_rin_t, p_rout, p_w1t_0, p_w2_0, p_w1t_1, p_w2_1]
    gsum = [part_sum(p, "part_sum_%d" % i) for i, p in enumerate(big_parts)]
    grads.update({
        "attn_w_in": gsum[0].T[None], "attn_w_out": gsum[1][None],
        "rnn_w_in": gsum[2].T[None], "rnn_w_out": gsum[3][None],
        "mlp_w1": jnp.stack([gsum[4].T, gsum[6].T]), "mlp_w2": jnp.stack([gsum[5], gsum[7]]),
    })
    wag = wag.reshape(N_DEV, 2 * nflat)
    grads["rnn_w_a"] = wag[:, :nflat].reshape(rnn_w_a.shape)
    grads["rnn_w_x"] = wag[:, nflat:].reshape(rnn_w_x.shape)
    tl = red.reshape(-1)
    loss = 0.5 * tl[3144] / D_MODEL
    grads["rnn_conv_w"] = tl[0:768].reshape(rnn_conv_w.shape)
    grads["rnn_conv_b"] = tl[768:960].reshape(rnn_conv_b.shape)
    grads["rnn_b_a"] = tl[960:1344].reshape(rnn_b_a.shape)
    grads["rnn_b_x"] = tl[1344:1728].reshape(rnn_b_x.shape)
    grads["rnn_lam"] = tl[1728:2112].reshape(rnn_lam.shape)
    grads["ln_g"] = tl[2112:2624].reshape(ln_g.shape)
    grads["ln_b"] = tl[2624:3136].reshape(ln_b.shape)
    grads["attn_sinks"] = tl[3136:3144].reshape(attn_sinks.shape)

    delta, new_m, new_v = {}, {}, {}
    for n in names:
        delta[n], new_m[n], new_v[n] = adamw(weights[n], grads[n], moments_m[n], moments_v[n], "adamw_" + n)
    return (loss, dx0.reshape(x.shape), *[grads[n] for n in names], *[delta[n] for n in names],
            *[new_m[n] for n in names], *[new_v[n] for n in names])
```

```python
import functools
import math

import jax
import jax.numpy as jnp
from jax import lax
from jax.experimental import pallas as pl
from jax.experimental.pallas import tpu as pltpu

F32, BF16 = jnp.float32, jnp.bfloat16
MESH = pl.DeviceIdType.MESH

D_MODEL = 1024
N_Q, N_KV, HEAD = 8, 2, 128
ROT, THETA = 32, 500000.0
QBLK = 128
D_QKV = (N_Q + 2 * N_KV) * HEAD
D_RNN, N_RB, RB_W = 1536, 16, 96
CG = 384
N_CG = D_RNN // CG
D_FF = 4096
FF_CHUNK = 1024
DEPTH = 2
ALPHA = (2.0 * DEPTH) ** 0.25
LN_EPS = 1e-5
LRU_C = 8.0
N_DEV = 8
LR, B1, B2, ADAM_EPS, WD, STEP = 0.001, 0.9, 0.999, 1e-8, 0.01, 10

VMEM_LIMIT = 60 * 1024 * 1024
TM_MM = 512
TM_MLP = 512
TM_MLP_FWD = 512
TT_RNN = 2048
TK_WG = 2048


def _nn(a, b):
    return jnp.dot(a, b, preferred_element_type=F32)


def _nt(a, b):
    return lax.dot_general(a, b, (((1,), (1,)), ((), ())), preferred_element_type=F32)


def _tn(a, b):
    return lax.dot_general(a, b, (((0,), (0,)), ((), ())), preferred_element_type=F32)


def _blk(n, pref):
    t = min(n, pref)
    assert n % t == 0, (n, pref)
    return t


def _params(**kw):
    return pltpu.CompilerParams(vmem_limit_bytes=VMEM_LIMIT, **kw)


def _row(tm, w):
    return pl.BlockSpec((tm, w), lambda i: (i, 0))


def _res(shape):
    return pl.BlockSpec(shape, lambda i: (0,) * len(shape), pipeline_mode=pl.Buffered(1))


def _mod(mod_ref, k):
    return mod_ref[3 * k:3 * k + 1, :], mod_ref[3 * k + 1:3 * k + 2, :], mod_ref[3 * k + 2:3 * k + 3, :]


def _ln_stats(z):
    mu = jnp.mean(z, axis=-1, keepdims=True)
    zc = z - mu
    var = jnp.mean(zc * zc, axis=-1, keepdims=True)
    rstd = lax.rsqrt(var + LN_EPS)
    return zc * rstd, rstd


def _ln_bwd(dxo, xhat, rstd, g):
    dxh = dxo * g
    m1 = jnp.mean(dxh, axis=-1, keepdims=True)
    m2 = jnp.mean(dxh * xhat, axis=-1, keepdims=True)
    return rstd * (dxh - m1 - xhat * m2)


def _colsum(v):
    return jnp.sum(v, axis=0, keepdims=True)


def _sigmoid(v):
    return 0.5 * jnp.tanh(0.5 * v) + 0.5


def _gelu_parts(v):
    k = math.sqrt(2.0 / math.pi)
    u = k * (v + 0.044715 * v * v * v)
    t = jnp.tanh(u)
    g = 0.5 * v * (1.0 + t)
    dg = 0.5 * (1.0 + t) + 0.5 * v * (1.0 - t * t) * k * (1.0 + 3.0 * 0.044715 * v * v)
    return g, dg


def _me():
    return lax.axis_index("x"), lax.axis_index("y"), lax.axis_index("c")


def _idx(p):
    return 4 * p[0] + 2 * p[1] + p[2]


def _peers(me):
    x, y, c = me
    out = []
    for k in range(1, N_DEV):
        out.append((1 - x if k & 4 else x, 1 - y if k & 2 else y, 1 - c if k & 1 else c))
    return out


class _Gather:
    def __init__(self, srcs):
        self.srcs = list(srcs)
        n = len(self.srcs)
        self.out_shape = [jax.ShapeDtypeStruct((N_DEV,) + s.shape, s.dtype) for s in self.srcs]
        self.scratch = [pltpu.SemaphoreType.DMA((n, 7)), pltpu.SemaphoreType.DMA((n, 7)),
                        pltpu.SemaphoreType.DMA((n,))]

    @staticmethod
    def _places():
        x, y, c = me = _me()
        return me, (x, y, 1 - c), [(1 - x, y), (x, 1 - y), (1 - x, 1 - y)]

    @staticmethod
    def _copy(outs, sems, t, k, block, to, src=None):
        slot = outs[t].at[_idx(block)]
        return pltpu.make_async_remote_copy(
            src_ref=slot if src is None else src, dst_ref=slot, send_sem=sems[0].at[t, k],
            recv_sem=sems[1].at[t, k], device_id=to, device_id_type=MESH)

    def _firsts(self, ins, outs, sems):
        me, sibling, chips = self._places()
        out = []
        for t in range(len(ins)):
            out.append(self._copy(outs, sems, t, 0, me, sibling, src=ins[t]))
            out += [self._copy(outs, sems, t, 1 + j, me, (*chip, me[2]), src=ins[t]) for j, chip in enumerate(chips)]
        return out

    def _locals(self, ins, outs, sems):
        me = _me()
        return [pltpu.make_async_copy(ins[t], outs[t].at[_idx(me)], sems[2].at[t]) for t in range(len(ins))]

    def start(self, ins, outs, sems):
        for cp in self._locals(ins, outs, sems) + self._firsts(ins, outs, sems):
            cp.start()

    def mid(self, ins, outs, sems):
        me, sibling, chips = self._places()
        for j, chip in enumerate(chips):
            for t in range(len(ins)):
                self._copy(outs, sems, t, 1 + j, (*chip, me[2]), me).wait_recv()
                self._copy(outs, sems, t, 4 + j, (*chip, me[2]), sibling).start()

    def finish(self, ins, outs, sems):
        me, sibling, chips = self._places()
        for t in range(len(ins)):
            self._copy(outs, sems, t, 0, sibling, me).wait_recv()
            for j, chip in enumerate(chips):
                self._copy(outs, sems, t, 4 + j, (*chip, 1 - me[2]), me).wait_recv()
        for cp in self._firsts(ins, outs, sems):
            cp.wait_send()
        for j, chip in enumerate(chips):
            for t in range(len(ins)):
                self._copy(outs, sems, t, 4 + j, (*chip, me[2]), sibling).wait_send()
        for cp in self._locals(ins, outs, sems):
            cp.wait()


class _AllToAll:
    def __init__(self, srcs):
        self.srcs = list(srcs)
        n = len(self.srcs)
        self.out_shape = [jax.ShapeDtypeStruct(s.shape, s.dtype) for s in self.srcs]
        self.scratch = [pltpu.SemaphoreType.DMA((n, 7)), pltpu.SemaphoreType.DMA((n, 7)),
                        pltpu.SemaphoreType.DMA((n,))]

    def _copies(self, ins, outs, sems):
        me = _me()
        loc, rem = [], []
        for t in range(len(ins)):
            loc.append(pltpu.make_async_copy(ins[t].at[_idx(me)], outs[t].at[_idx(me)], sems[2].at[t]))
            for k, p in enumerate(_peers(me)):
                rem.append(pltpu.make_async_remote_copy(
                    src_ref=ins[t].at[_idx(p)], dst_ref=outs[t].at[_idx(me)], send_sem=sems[0].at[t, k],
                    recv_sem=sems[1].at[t, k], device_id=p, device_id_type=MESH))
        return loc, rem

    def start(self, ins, outs, sems):
        loc, rem = self._copies(ins, outs, sems)
        for cp in loc + rem:
            cp.start()

    def mid(self, ins, outs, sems):
        pass

    def finish(self, ins, outs, sems):
        me = _me()
        for t in range(len(ins)):
            for k, p in enumerate(_peers(me)):
                slot = outs[t].at[_idx(p)]
                pltpu.make_async_remote_copy(
                    src_ref=slot, dst_ref=slot, send_sem=sems[0].at[t, k], recv_sem=sems[1].at[t, k],
                    device_id=p, device_id_type=MESH).wait_recv()
        loc, rem = self._copies(ins, outs, sems)
        for cp in rem:
            cp.wait_send()
        for cp in loc:
            cp.wait()


class _Multi:
    def __init__(self, *exs):
        self.exs = exs
        self.srcs = [s for e in exs for s in e.srcs]
        self.out_shape = [s for e in exs for s in e.out_shape]
        self.scratch = [s for e in exs for s in e.scratch]

    def _each(self, ins, outs, sems):
        i = j = 0
        for e in self.exs:
            n, m = len(e.srcs), len(e.scratch)
            yield e, ins[i:i + n], outs[i:i + n], sems[j:j + m]
            i, j = i + n, j + m

    def start(self, ins, outs, sems):
        for e, a, b, c in self._each(ins, outs, sems):
            e.start(a, b, c)

    def mid(self, ins, outs, sems):
        for e, a, b, c in self._each(ins, outs, sems):
            e.mid(a, b, c)

    def finish(self, ins, outs, sems):
        for e, a, b, c in self._each(ins, outs, sems):
            e.finish(a, b, c)


def _call(body, *, name, grid, in_specs, out_specs, out_shape, args, scratch_shapes=(), rider=None):
    in_specs, out_specs, out_shape = list(in_specs), list(out_specs), list(out_shape)
    scratch_shapes = list(scratch_shapes)
    if rider is None:
        return pl.pallas_call(body, name=name, grid=grid, out_shape=out_shape, in_specs=in_specs,
                              out_specs=out_specs, scratch_shapes=scratch_shapes, compiler_params=_params())(*args)
    nci, nco, ncs, nr = len(in_specs), len(out_shape), len(scratch_shapes), len(rider.srcs)
    nsteps = math.prod(grid)
    assert nsteps >= 2, (name, grid)
    mid = max(1, (7 * nsteps) // 8)

    def full(*refs):
        ci, ri = refs[:nci], refs[nci:nci + nr]
        co, ro = refs[nci + nr:nci + nr + nco], refs[nci + nr + nco:nci + 2 * nr + nco]
        cs, rs = refs[nci + 2 * nr + nco:nci + 2 * nr + nco + ncs], refs[nci + 2 * nr + nco + ncs:]
        step = pl.program_id(0)
        for d in range(1, len(grid)):
            step = step * grid[d] + pl.program_id(d)

        @pl.when(step == 0)
        def _():
            rider.start(ri, ro, rs)

        @pl.when(step == mid)
        def _():
            rider.mid(ri, ro, rs)

        body(*ci, *co, *cs)

        @pl.when(step == nsteps - 1)
        def _():
            rider.finish(ri, ro, rs)

    any_spec = pl.BlockSpec(memory_space=pl.ANY)
    return pl.pallas_call(
        full, name=name, grid=grid, out_shape=out_shape + rider.out_shape,
        in_specs=in_specs + [any_spec] * nr, out_specs=out_specs + [any_spec] * nr,
        scratch_shapes=scratch_shapes + rider.scratch, compiler_params=_params(),
    )(*args, *rider.srcs)


def _a2a_start(srcs, dsts, send_sems, recv_sems, local_sems, me, sem_base=0):
    peers = _peers(me)
    started = []
    for t in range(len(srcs)):
        loc = pltpu.make_async_copy(srcs[t].at[_idx(me)], dsts[t].at[_idx(me)], local_sems.at[sem_base + t])
        loc.start()
        started.append(("local", loc))
        for k, p in enumerate(peers):
            cp = pltpu.make_async_remote_copy(
                src_ref=srcs[t].at[_idx(p)], dst_ref=dsts[t].at[_idx(me)],
                send_sem=send_sems.at[sem_base + t, k], recv_sem=recv_sems.at[sem_base + t, k],
                device_id=p, device_id_type=MESH)
            cp.start()
            started.append(("remote", cp))
    return started


def _a2a_finish(started, dsts, send_sems, recv_sems, me, sem_base=0):
    peers = _peers(me)
    for t in range(len(dsts)):
        for k, p in enumerate(peers):
            slot = dsts[t].at[_idx(p)]
            pltpu.make_async_remote_copy(
                src_ref=slot, dst_ref=slot, send_sem=send_sems.at[sem_base + t, k],
                recv_sem=recv_sems.at[sem_base + t, k], device_id=p, device_id_type=MESH).wait_recv()
    for kind, cp in started:
        if kind == "local":
            cp.wait()
        else:
            cp.wait_send()


def ada_modulation(c8, ada_w, ada_b, ride):
    nr = len(ride.srcs)

    def body(c_ref, w_ref, b_ref, *rest):
        ride_in, (call_ref, modr_ref), ride_out = rest[:nr], rest[nr:nr + 2], rest[nr + 2:2 * nr + 2]
        modp, send_sems, recv_sems, local_sems = rest[2 * nr + 2:2 * nr + 6]
        ride_sems = rest[2 * nr + 6:]
        ride.start(ride_in, ride_out, ride_sems)
        me = _me()
        peers = _peers(me)
        sends = []
        for k, p in enumerate(peers):
            cp = pltpu.make_async_remote_copy(
                src_ref=c_ref, dst_ref=call_ref.at[_idx(me)], send_sem=send_sems.at[0, k],
                recv_sem=recv_sems.at[0, k], device_id=p, device_id_type=MESH)
            cp.start()
            sends.append(cp)
        call_ref[_idx(me)] = c_ref[...]
        for k, p in enumerate(peers):
            slot = call_ref.at[_idx(p)]
            pltpu.make_async_remote_copy(
                src_ref=slot, dst_ref=slot, send_sem=send_sems.at[0, k], recv_sem=recv_sems.at[0, k],
                device_id=p, device_id_type=MESH).wait_recv()
        for cp in sends:
            cp.wait_send()
        cv = call_ref[...].reshape(N_DEV * 8, D_MODEL)
        s = (cv * _sigmoid(cv)).astype(BF16)
        for k in range(4):
            res = _nn(s, w_ref[k].astype(BF16)) + b_ref[k]
            for j in range(N_DEV):
                modp[j, 8 * k:8 * k + 8, :] = res[8 * j:8 * j + 8, :]
        started = _a2a_start([modp], [modr_ref], send_sems, recv_sems, local_sems, me, sem_base=1)
        _a2a_finish(started, [modr_ref], send_sems, recv_sems, me, sem_base=1)
        ride.mid(ride_in, ride_out, ride_sems)
        ride.finish(ride_in, ride_out, ride_sems)

    vm, hbm = pl.BlockSpec(memory_space=pltpu.VMEM), pl.BlockSpec(memory_space=pl.ANY)
    return pl.pallas_call(
        body, name="ada_modulation",
        out_shape=[jax.ShapeDtypeStruct((N_DEV, 8, D_MODEL), F32), jax.ShapeDtypeStruct((N_DEV, 32, CG), F32)]
        + ride.out_shape,
        in_specs=[vm, vm, vm] + [hbm] * nr, out_specs=[vm, vm] + [hbm] * nr,
        scratch_shapes=[pltpu.VMEM((N_DEV, 32, CG), F32), pltpu.SemaphoreType.DMA((2, 7)),
                        pltpu.SemaphoreType.DMA((2, 7)), pltpu.SemaphoreType.DMA((2,))] + ride.scratch,
        compiler_params=_params(),
    )(c8, ada_w, ada_b, *ride.srcs)


def epilogue(gsend, c_t, tail, ride):
    nr = len(ride.srcs)
    rt = tail.shape[1]

    def body(g_ref, ct_ref, t_ref, *rest):
        ride_in, (gw_ref, gb_ref, red_ref), ride_out = rest[:nr], rest[nr:nr + 3], rest[nr + 3:2 * nr + 3]
        grecv, trecv, send_sems, recv_sems, local_sems = rest[2 * nr + 3:2 * nr + 8]
        ride_sems = rest[2 * nr + 8:]
        ride.start(ride_in, ride_out, ride_sems)
        me = _me()
        started = _a2a_start([g_ref, t_ref], [grecv, trecv], send_sems, recv_sems, local_sems, me)
        _a2a_finish(started, [grecv, trecv], send_sems, recv_sems, me)
        acc = trecv[0]
        for j in range(1, N_DEV):
            acc = acc + trecv[j]
        red_ref[...] = acc
        ct = ct_ref[...]
        st = (ct * _sigmoid(ct)).astype(BF16).astype(F32)
        gb = jnp.zeros((8, CG), F32)
        for b in range(N_DEV):
            gb = gb + grecv[b]
        gb_ref[...] = gb
        for k in range(4):
            acc = jnp.zeros((D_MODEL, CG), F32)
            for b in range(N_DEV):
                row = grecv[b, k:k + 1, :].astype(BF16).astype(F32)
                acc = acc + st[:, b:b + 1] * row
            gw_ref[k] = acc
        ride.mid(ride_in, ride_out, ride_sems)
        ride.finish(ride_in, ride_out, ride_sems)

    vm, hbm = pl.BlockSpec(memory_space=pltpu.VMEM), pl.BlockSpec(memory_space=pl.ANY)
    return pl.pallas_call(
        body, name="epilogue",
        out_shape=[jax.ShapeDtypeStruct((4, D_MODEL, CG), F32), jax.ShapeDtypeStruct((8, CG), F32),
                   jax.ShapeDtypeStruct((rt, 128), F32)] + ride.out_shape,
        in_specs=[vm, vm, vm] + [hbm] * nr, out_specs=[vm, vm, vm] + [hbm] * nr,
        scratch_shapes=[pltpu.VMEM((N_DEV, 8, CG), F32), pltpu.VMEM((N_DEV, rt, 128), F32),
                        pltpu.SemaphoreType.DMA((2, 7)), pltpu.SemaphoreType.DMA((2, 7)),
                        pltpu.SemaphoreType.DMA((2,))] + ride.scratch,
        compiler_params=_params(),
    )(gsend, c_t, tail, *ride.srcs)


def _rope(t, cos, s1, s2):
    return t * cos + pltpu.roll(t, 16, 1) * s1 + pltpu.roll(t, HEAD - 16, 1) * s2


def _rope_bwd(d, cos, s1, s2):
    return d * cos + pltpu.roll(d * s1, HEAD - 16, 1) + pltpu.roll(d * s2, 16, 1)


def attn_in_fwd(x, mods, k, win_t, rope, rider=None):
    s = x.shape[0]
    tm = _blk(s, TM_MM)

    def body(x_ref, mod_ref, w_ref, c_ref, s1_ref, s2_ref, q_ref, k_ref, v_ref):
        shift, scale, _ = _mod(mod_ref, k)
        h = (x_ref[...] * (1.0 + scale) + shift).astype(BF16)
        qkv = _nt(h, w_ref[...])
        cos, s1, s2 = c_ref[...], s1_ref[...], s2_ref[...]
        for hh in range(N_Q + N_KV):
            r = _rope(qkv[:, HEAD * hh:HEAD * (hh + 1)], cos, s1, s2).astype(BF16)
            if hh < N_Q:
                q_ref[:, HEAD * hh:HEAD * (hh + 1)] = r
            else:
                k_ref[:, HEAD * (hh - N_Q):HEAD * (hh - N_Q + 1)] = r
        v_ref[...] = qkv[:, HEAD * (N_Q + N_KV):].astype(BF16)

    return _call(
        body, name="attn_in_fwd", grid=(s // tm,),
        out_shape=[jax.ShapeDtypeStruct((s, N_Q * HEAD), BF16), jax.ShapeDtypeStruct((s, N_KV * HEAD), BF16),
                   jax.ShapeDtypeStruct((s, N_KV * HEAD), BF16)],
        in_specs=[_row(tm, D_MODEL), _res(mods.shape), _res(win_t.shape),
                  _row(tm, HEAD), _row(tm, HEAD), _row(tm, HEAD)],
        out_specs=[_row(tm, N_Q * HEAD), _row(tm, N_KV * HEAD), _row(tm, N_KV * HEAD)],
        args=(x, mods, win_t, *rope), rider=rider)


QPAIR = 8


def _kv_specs(nblk):
    w = N_KV * HEAD
    return [pl.BlockSpec((QBLK, w), lambda n: (jnp.maximum(QPAIR * n - 1, 0), 0)),
            pl.BlockSpec((QPAIR * QBLK, w), lambda n: (n, 0)),
            pl.BlockSpec((QBLK, w), lambda n: (jnp.minimum(QPAIR * (n + 1), nblk - 1), 0))]


GROUP = N_Q // N_KV


def _attn_mask(n, s):
    qi = lax.broadcasted_iota(jnp.int32, (GROUP * QBLK, 3 * QBLK), 0) & (QBLK - 1)
    kj = lax.broadcasted_iota(jnp.int32, (GROUP * QBLK, 3 * QBLK), 1)
    rel = kj - QBLK - qi
    kpos = kj + (n - 1) * QBLK
    return (jnp.abs(rel) <= QBLK) & (kpos >= 0) & (kpos < s)


def _stack_heads(ref, qb, kv):
    rows = slice(QBLK * qb, QBLK * (qb + 1))
    return jnp.concatenate([ref[rows, HEAD * (GROUP * kv + j):HEAD * (GROUP * kv + j + 1)] for j in range(GROUP)],
                           axis=0)


def _stack_sinks(sink_ref, kv):
    row = lax.broadcasted_iota(jnp.int32, (GROUP * QBLK, 1), 0)
    out = jnp.full((GROUP * QBLK, 1), sink_ref[0, GROUP * kv + GROUP - 1], F32)
    for j in range(GROUP - 2, -1, -1):
        out = jnp.where(row < QBLK * (j + 1), sink_ref[0, GROUP * kv + j], out)
    return out


def _attn_probs(qh, kh, valid, sink):
    sc = _nt(qh, kh) * (HEAD ** -0.5)
    sc = jnp.where(valid, sc, -1e30)
    m = jnp.maximum(jnp.max(sc, axis=-1, keepdims=True), sink)
    p = jnp.exp(sc - m)
    es = jnp.exp(sink - m)
    denom = jnp.sum(p, axis=-1, keepdims=True) + es
    return p / denom, es / denom


def attn_fwd(q, kk, v, sinks, rider=None):
    s = q.shape[0]
    nblk = s // QBLK

    def body(sink_ref, q_ref, kp, ko, kn, vp, vo, vn, o_ref):
        n = pl.program_id(0)
        kall = jnp.concatenate([kp[...], ko[...], kn[...]], axis=0)
        vall = jnp.concatenate([vp[...], vo[...], vn[...]], axis=0)
        for qb in range(QPAIR):
            valid = _attn_mask(QPAIR * n + qb, s)
            keys = slice(QBLK * qb, QBLK * (qb + 3))
            for kv in range(N_KV):
                cols = slice(HEAD * kv, HEAD * (kv + 1))
                probs, _ = _attn_probs(_stack_heads(q_ref, qb, kv), kall[keys, cols], valid,
                                       _stack_sinks(sink_ref, kv))
                og = _nn(probs.astype(BF16), vall[keys, cols]).astype(BF16)
                for j in range(GROUP):
                    hq = GROUP * kv + j
                    o_ref[QBLK * qb:QBLK * (qb + 1), HEAD * hq:HEAD * (hq + 1)] = og[QBLK * j:QBLK * (j + 1), :]

    qspec = pl.BlockSpec((QPAIR * QBLK, N_Q * HEAD), lambda n: (n, 0))
    return _call(
        body, name="attn_fwd", grid=(nblk // QPAIR,),
        out_shape=[jax.ShapeDtypeStruct((s, N_Q * HEAD), BF16)],
        in_specs=[pl.BlockSpec(memory_space=pltpu.SMEM), qspec] + _kv_specs(nblk) + _kv_specs(nblk),
        out_specs=[qspec],
        args=(sinks, q, kk, kk, kk, v, v, v), rider=rider)


def post_fwd(ypre, w, x, mods, k, lng, lnb, gate_act=None, rider=None):
    s = x.shape[0]
    tm = _blk(s, TM_MM)
    kdim = w.shape[0]
    rnn = gate_act is not None

    def body(*refs):
        if rnn:
            gt_ref, hs_ref, w_ref, x_ref, mod_ref, g_ref, b_ref, xo_ref, y_ref, yp_ref = refs
            act, _ = _gelu_parts(gt_ref[...].astype(F32))
            yp = (hs_ref[...].astype(F32) * act).astype(BF16)
            yp_ref[...] = yp
        else:
            yp_ref, w_ref, x_ref, mod_ref, g_ref, b_ref, xo_ref, y_ref = refs
            yp = yp_ref[...]
        _, _, gate = _mod(mod_ref, k)
        y = _nn(yp, w_ref[...])
        y_ref[...] = y
        xhat, _ = _ln_stats(ALPHA * x_ref[...] + (1.0 + gate) * y)
        xo_ref[...] = xhat * g_ref[...] + b_ref[...]

    act_in = list(gate_act) if rnn else [ypre]
    out_shape = [jax.ShapeDtypeStruct((s, D_MODEL), F32), jax.ShapeDtypeStruct((s, D_MODEL), F32)]
    out_specs = [_row(tm, D_MODEL), _row(tm, D_MODEL)]
    if rnn:
        out_shape.append(jax.ShapeDtypeStruct((s, kdim), BF16))
        out_specs.append(_row(tm, kdim))
    return _call(
        body, name="rnn_post_fwd" if rnn else "attn_post_fwd", grid=(s // tm,),
        out_shape=out_shape,
        in_specs=[_row(tm, kdim)] * len(act_in) + [_res(w.shape), _row(tm, D_MODEL), _res(mods.shape),
                                                    _res(lng.shape), _res(lnb.shape)],
        out_specs=out_specs,
        args=(*act_in, w, x, mods, lng, lnb), rider=rider)


def mlp_fwd(x, mods, k, w1_t, w2, lng, lnb, rider=None, last=False):
    s = x.shape[0]
    tm = _blk(s, TM_MLP_FWD)

    def body(x_ref, mod_ref, w1_ref, w2_ref, g_ref, b_ref, *outs):
        xo_ref = None if last else outs[0]
        y_ref, ra_ref, r_ref = outs[-3:]
        xv = x_ref[...]
        shift, scale, gate = _mod(mod_ref, k)
        h = (xv * (1.0 + scale) + shift).astype(BF16)
        y = jnp.zeros((tm, D_MODEL), F32)
        for c in range(D_FF // FF_CHUNK):
            rows = slice(FF_CHUNK * c, FF_CHUNK * (c + 1))
            a = jnp.maximum(_nt(h, w1_ref[rows, :]), 0.0)
            r = (a * a).astype(BF16)
            ra_ref[:, rows] = a.astype(BF16)
            r_ref[:, rows] = r
            y = y + _nn(r, w2_ref[rows, :])
        y_ref[...] = y
        if not last:
            xhat, _ = _ln_stats(ALPHA * xv + (1.0 + gate) * y)
            xo_ref[...] = xhat * g_ref[...] + b_ref[...]

    nf = 1 if last else 2
    return _call(
        body, name="mlp_fwd_last" if last else "mlp_fwd", grid=(s // tm,),
        out_shape=[jax.ShapeDtypeStruct((s, D_MODEL), F32)] * nf + [jax.ShapeDtypeStruct((s, D_FF), BF16)] * 2,
        in_specs=[_row(tm, D_MODEL), _res(mods.shape), _res(w1_t.shape), _res(w2.shape),
                  _res(lng.shape), _res(lnb.shape)],
        out_specs=[_row(tm, D_MODEL)] * nf + [_row(tm, D_FF)] * 2,
        args=(x, mods, w1_t, w2, lng, lnb), rider=rider)


def rnn_in_fwd(x, mods, k, win_t):
    s = x.shape[0]
    tm = _blk(s, TM_MM)

    def body(x_ref, mod_ref, w_ref, xr_ref, gt_ref):
        shift, scale, _ = _mod(mod_ref, k)
        h = (x_ref[...] * (1.0 + scale) + shift).astype(BF16)
        xr_ref[...] = _nt(h, w_ref[0:D_RNN, :])
        gt_ref[...] = _nt(h, w_ref[D_RNN:2 * D_RNN, :]).astype(BF16)

    return pl.pallas_call(
        body, name="rnn_in_fwd", grid=(s // tm,),
        out_shape=[jax.ShapeDtypeStruct((s, D_RNN), F32), jax.ShapeDtypeStruct((s, D_RNN), BF16)],
        in_specs=[_row(tm, D_MODEL), _res(mods.shape), _res(win_t.shape)],
        out_specs=[_row(tm, D_RNN)] * 2,
        compiler_params=_params(),
    )(x, mods, win_t)


def _shift_rows(v, k, row):
    n = v.shape[0]
    r = pltpu.roll(v, k % n, 0)
    keep = (row >= k) if k > 0 else (row < n + k)
    return jnp.where(keep, r, 0.0)


def conv_fwd(xr, cw, cb):
    s = xr.shape[0]

    def body(x_ref, w_ref, b_ref, o_ref):
        xv = x_ref[...]
        row = lax.broadcasted_iota(jnp.int32, xv.shape, 0)
        o_ref[...] = (b_ref[...] + w_ref[0:1, :] * _shift_rows(xv, 2, row) + w_ref[1:2, :] * _shift_rows(xv, 1, row)
                      + w_ref[2:3, :] * xv + w_ref[3:4, :] * _shift_rows(xv, -1, row))

    slab = pl.BlockSpec((s, 128), lambda j: (0, j))
    return pl.pallas_call(
        body, name="conv_fwd", grid=(D_RNN // 128,),
        out_shape=jax.ShapeDtypeStruct((D_RNN // 128, s, 128), F32),
        in_specs=[slab, pl.BlockSpec((4, 128), lambda j: (0, j)), pl.BlockSpec((1, 128), lambda j: (0, j))],
        out_specs=pl.BlockSpec((None, s, 128), lambda j: (j, 0, 0)),
        compiler_params=_params(),
    )(xr, cw, cb)


def _softplus_neg(lam):
    z = -lam
    e = jnp.exp(-jnp.abs(z))
    u = 1.0 + e
    log1p = jnp.where(u == 1.0, e, jnp.log(u) * e / jnp.where(u == 1.0, 1.0, u - 1.0))
    return jnp.maximum(z, 0.0) + log1p, 1.0 / (1.0 + jnp.exp(lam))


def _lru_gates(xv, wa_ref, wx_ref, ba_ref, bx_ref, lam_ref):
    xb = xv.astype(BF16)
    r = _sigmoid(_nn(xb, wa_ref[...]) + ba_ref[...])
    i = _sigmoid(_nn(xb, wx_ref[...]) + bx_ref[...])
    sp, sg = _softplus_neg(lam_ref[...])
    la = r * (-LRU_C * sp)
    a = jnp.exp(la)
    th = jnp.tanh(la)
    m2 = -2.0 * th / (1.0 - th)
    rmult = lax.rsqrt(jnp.maximum(m2, 1e-37))
    return xb, r, i, sp, sg, a, m2 * rmult, rmult


SLABS = CG // 128
GRP, SEG = 32, 4


def _lru_specs(nt, tt, reverse):
    tmap = (lambda t: nt - 1 - t) if reverse else (lambda t: t)
    blk = pl.BlockSpec((tt, CG), lambda g, t: (tmap(t), g))
    slabs = pl.BlockSpec((SLABS, tt, 128), lambda g, t: (g, tmap(t), 0))
    wsp = pl.BlockSpec((None, CG, CG), lambda g, t: (g, 0, 0))
    vec = pl.BlockSpec((1, CG), lambda g, t: (0, g))
    return tmap, blk, slabs, wsp, vec


def _slab_rows(ref3):
    return jnp.concatenate([ref3[l] for l in range(ref3.shape[0])], axis=1)


def _perm_load(ref3, tt):
    out = []
    for l in range(SLABS):
        r = ref3.at[l]
        out.append(jnp.concatenate([r[pl.ds(GRP * g + i, 8, stride=SEG), :]
                                    for g in range(tt // GRP) for i in range(SEG)], axis=0))
    return jnp.concatenate(out, axis=1)


def _perm_scan(a, u, carry, reverse, emit):
    n, c = a.shape
    sub = lax.broadcasted_iota(jnp.int32, (8, c), 0)
    steps = [(8 - sh, sub < 8 - sh) if reverse else (sh, sub >= sh) for sh in (1, 2, 4)]
    order = range(SEG - 1, -1, -1) if reverse else range(SEG)
    for g in (range(n // GRP - 1, -1, -1) if reverse else range(n // GRP)):
        hs, ps = [None] * SEG, [None] * SEG
        h = p = None
        for i in order:
            rows = slice(GRP * g + 8 * i, GRP * g + 8 * i + 8)
            h = u[rows] if h is None else a[rows] * h + u[rows]
            p = a[rows] if p is None else a[rows] * p
            hs[i], ps[i] = h, p
        d, f = p, h
        for rot, keep in steps:
            d_s = jnp.where(keep, pltpu.roll(d, rot, 0), 1.0)
            f_s = jnp.where(keep, pltpu.roll(f, rot, 0), 0.0)
            f = d * f_s + f
            d = d * d_s
        end = f + d * carry
        if reverse:
            init = jnp.where(sub == 7, carry, pltpu.roll(end, 7, 0))
            carry = jnp.broadcast_to(end[0:1], (8, c))
        else:
            init = jnp.where(sub == 0, carry, pltpu.roll(end, 1, 0))
            carry = jnp.broadcast_to(end[7:8], (8, c))
        for i in range(SEG):
            emit(g, i, hs[i] + ps[i] * init)
    return carry


def lru_fwd(xc, wa, wx, ba, bx, lam, reverse, other=None):
    s = xc.shape[1]
    tt = _blk(s, TT_RNN)
    nt = s // tt

    def body(x_ref, wa_ref, wx_ref, ba_ref, bx_ref, lam_ref, *rest):
        if other is None:
            hs_ref, carry = rest
        else:
            oth_ref, hs_ref, sum_ref, carry = rest

        @pl.when(pl.program_id(1) == 0)
        def _():
            carry[...] = jnp.zeros(carry.shape, F32)

        xv = _perm_load(x_ref, tt)
        _, _, i, _, _, a, mult, _ = _lru_gates(xv, wa_ref, wx_ref, ba_ref, bx_ref, lam_ref)
        u = mult * (i * xv)

        def emit(g, j, rows):
            for l in range(SLABS):
                hs_ref.at[l][pl.ds(GRP * g + j, 8, stride=SEG), :] = rows[:, 128 * l:128 * (l + 1)]

        carry[...] = _perm_scan(a, u, carry[...], reverse, emit)
        if other is not None:
            for l in range(SLABS):
                sum_ref[:, 128 * l:128 * (l + 1)] = (hs_ref[l] + oth_ref[l]).astype(BF16)

    _, blk, slabs, wsp, vec = _lru_specs(nt, tt, reverse)
    extra = [] if other is None else [other]
    return pl.pallas_call(
        body, name="lru_fwd_rev" if reverse else "lru_fwd", grid=(N_CG, nt),
        out_shape=[jax.ShapeDtypeStruct(xc.shape, F32)] + [jax.ShapeDtypeStruct((s, D_RNN), BF16)] * len(extra),
        in_specs=[slabs, wsp, wsp, vec, vec, vec] + [slabs] * len(extra), out_specs=[slabs] + [blk] * len(extra),
        scratch_shapes=[pltpu.VMEM((8, CG), F32)],
        compiler_params=_params(),
    )(xc, wa, wx, ba, bx, lam, *extra)


def lru_bwd(xc, dhs, hs, wa, wx, ba, bx, lam, reverse, rider=None):
    s = xc.shape[1]
    tt = _blk(s, TT_RNN)
    nt = s // tt
    ng = tt // GRP
    back = not reverse

    def rows_of(v, g, i):
        return v[GRP * g + 8 * i:GRP * g + 8 * i + 8]

    def neighbour(v, past, edge, sub):
        out = []
        for g in range(ng):
            for i in range(SEG):
                if past and i > 0:
                    r = rows_of(v, g, i - 1)
                elif past:
                    e = edge if g == 0 else rows_of(v, g - 1, SEG - 1)[7:8]
                    r = jnp.where(sub == 0, e, pltpu.roll(rows_of(v, g, SEG - 1), 1, 0))
                elif i < SEG - 1:
                    r = rows_of(v, g, i + 1)
                else:
                    e = edge if g == ng - 1 else rows_of(v, g + 1, 0)[0:1]
                    r = jnp.where(sub == 7, e, pltpu.roll(rows_of(v, g, 0), 7, 0))
                out.append(r)
        return jnp.concatenate(out, axis=0)

    def body(x_ref, dh_ref, hs_ref, nb_ref, wa_ref, wx_ref, ba_ref, bx_ref, lam_ref,
             dx_ref, dwa_ref, dwx_ref, dba_ref, dbx_ref, dlam_ref, carry):
        t = pl.program_id(1)

        @pl.when(t == 0)
        def _():
            carry[...] = jnp.zeros(carry.shape, F32)
            dwa_ref[...] = jnp.zeros(dwa_ref.shape, F32)
            dwx_ref[...] = jnp.zeros(dwx_ref.shape, F32)
            dba_ref[...] = jnp.zeros(dba_ref.shape, F32)
            dbx_ref[...] = jnp.zeros(dbx_ref.shape, F32)
            dlam_ref[...] = jnp.zeros(dlam_ref.shape, F32)

        xv = _perm_load(x_ref, tt)
        xb, r, i, sp, sg, a, mult, rmult = _lru_gates(xv, wa_ref, wx_ref, ba_ref, bx_ref, lam_ref)
        sub = lax.broadcasted_iota(jnp.int32, (8, CG), 0)
        hsv = _perm_load(hs_ref, tt)
        nbv = _slab_rows(nb_ref)
        inner = t < nt - 1
        h_edge = jnp.where(inner, nbv[0:1, :] if reverse else nbv[7:8, :], 0.0)
        hprev = neighbour(hsv, not reverse, h_edge, sub)
        a_next = neighbour(a, reverse, carry[8:9, :], sub)
        dhv = _perm_load(dh_ref, tt)
        gl = [None] * (ng * SEG)

        def emit(gi, j, rows):
            gl[gi * SEG + j] = rows

        carry[0:8, :] = _perm_scan(a_next, dhv, carry[0:8, :], back, emit)
        g = jnp.concatenate(gl, axis=0)
        carry[8:9, :] = a[0:1, :] if back else a[tt - 1:tt, :]

        da = g * hprev
        dmult = g * (i * xv)
        di = g * mult * xv
        dla = da * a - dmult * (a * a) * rmult
        dpa = (dla * (-LRU_C * sp)) * r * (1.0 - r)
        dpx = di * i * (1.0 - i)
        dlam_ref[...] += _colsum(dla * (LRU_C * r * sg))
        dba_ref[...] += _colsum(dpa)
        dbx_ref[...] += _colsum(dpx)
        dpab, dpxb = dpa.astype(BF16), dpx.astype(BF16)
        dxv = g * mult * i + _nt(dpab, wa_ref[...]) + _nt(dpxb, wx_ref[...])
        for gi in range(ng):
            for j in range(SEG):
                for l in range(SLABS):
                    dx_ref.at[l][pl.ds(GRP * gi + j, 8, stride=SEG), :] = rows_of(dxv, gi, j)[:, 128 * l:128 * (l + 1)]
        dwa_ref[...] += _tn(xb, dpab)
        dwx_ref[...] += _tn(xb, dpxb)

    tmap, blk, slabs, wsp, vec = _lru_specs(nt, tt, back)
    per8 = tt // 8
    if reverse:
        nb = pl.BlockSpec((SLABS, 8, 128), lambda g, t: (g, jnp.minimum((tmap(t) + 1) * per8, s // 8 - 1), 0))
    else:
        nb = pl.BlockSpec((SLABS, 8, 128), lambda g, t: (g, jnp.maximum(tmap(t) * per8 - 1, 0), 0))
    return _call(
        body, name="lru_bwd_rev" if reverse else "lru_bwd", grid=(N_CG, nt),
        out_shape=[jax.ShapeDtypeStruct(xc.shape, F32), jax.ShapeDtypeStruct((N_CG, CG, CG), F32),
                   jax.ShapeDtypeStruct((N_CG, CG, CG), F32)] + [jax.ShapeDtypeStruct((1, D_RNN), F32)] * 3,
        in_specs=[slabs, slabs, slabs, nb, wsp, wsp, vec, vec, vec],
        out_specs=[slabs, wsp, wsp, vec, vec, vec],
        scratch_shapes=[pltpu.VMEM((16, CG), F32)],
        args=(xc, dhs, hs, hs, wa, wx, ba, bx, lam), rider=rider)


def _ln_part_bwd(dxo, x, y, gate, g, sums_ref, loss_head=None):
    xhat, rstd = _ln_stats(ALPHA * x + (1.0 + gate) * y)
    if loss_head is not None:
        err = xhat * g + loss_head[0] - loss_head[1]
        dxo = err * (1.0 / D_MODEL)
        sums_ref[5:6, :] += _colsum(err * err)
    dz = _ln_bwd(dxo, xhat, rstd, g)
    sums_ref[2:3, :] += _colsum(dz * y)
    sums_ref[3:4, :] += _colsum(dxo * xhat)
    sums_ref[4:5, :] += _colsum(dxo)
    return dz


def mlp_bwd(dxo, x, y, ra, mods, k, w1_t, w2, lng, lnb=None, rider=None):
    s = x.shape[0]
    tm = _blk(s, TM_MLP)
    head = lnb is not None

    def body(d_ref, x_ref, y_ref, ra_ref, mod_ref, w1_ref, w2_ref, g_ref, *rest):
        b_ref = rest[0] if head else None
        dx_ref, da_ref, h_ref, dy_ref, sums_ref = rest[1:] if head else rest

        @pl.when(pl.program_id(0) == 0)
        def _():
            sums_ref[...] = jnp.zeros(sums_ref.shape, F32)

        xv = x_ref[...]
        shift, scale, gate = _mod(mod_ref, k)
        if head:
            dz = _ln_part_bwd(None, xv, y_ref[...], gate, g_ref[...], sums_ref, (b_ref[...], d_ref[...]))
        else:
            dz = _ln_part_bwd(d_ref[...], xv, y_ref[...], gate, g_ref[...], sums_ref)
        dyb = (dz * (1.0 + gate)).astype(BF16)
        dy_ref[...] = dyb
        h = (xv * (1.0 + scale) + shift).astype(BF16)
        h_ref[...] = h
        dh = jnp.zeros((tm, D_MODEL), F32)
        for c in range(D_FF // FF_CHUNK):
            rows = slice(FF_CHUNK * c, FF_CHUNK * (c + 1))
            da = (_nt(dyb, w2_ref[rows, :]) * (2.0 * ra_ref[:, rows].astype(F32))).astype(BF16)
            da_ref[:, rows] = da
            dh = dh + _nn(da, w1_ref[rows, :])
        dx_ref[...] = ALPHA * dz + dh * (1.0 + scale)
        sums_ref[0:1, :] += _colsum(dh)
        sums_ref[1:2, :] += _colsum(dh * xv)

    return _call(
        body, name="mlp_bwd", grid=(s // tm,),
        out_shape=[jax.ShapeDtypeStruct((s, D_MODEL), F32), jax.ShapeDtypeStruct((s, D_FF), BF16),
                   jax.ShapeDtypeStruct((s, D_MODEL), BF16),
                   jax.ShapeDtypeStruct((s, D_MODEL), BF16), jax.ShapeDtypeStruct((8, D_MODEL), F32)],
        in_specs=[_row(tm, D_MODEL)] * 3 + [_row(tm, D_FF), _res(mods.shape), _res(w1_t.shape), _res(w2.shape),
                                             _res(lng.shape)] + ([_res(lnb.shape)] if head else []),
        out_specs=[_row(tm, D_MODEL), pl.BlockSpec((tm, D_FF), lambda i: (i, 0), pipeline_mode=pl.Buffered(1)),
                   _row(tm, D_MODEL), _row(tm, D_MODEL), _res((8, D_MODEL))],
        args=(dxo, x, y, ra, mods, w1_t, w2, lng) + ((lnb,) if head else ()), rider=rider)


def post_bwd(dxo, x, y, mods, k, w, lng, gate_act=None, rider=None):
    s = x.shape[0]
    tm = _blk(s, TM_MM)
    kdim = w.shape[0]
    rnn = gate_act is not None

    def body(*refs):
        if rnn:
            (d_ref, x_ref, y_ref, mod_ref, w_ref, g_ref, gt_ref, hs_ref,
             dres_ref, dy_ref, sums_ref, dhs_ref, dgt_ref) = refs
        else:
            d_ref, x_ref, y_ref, mod_ref, w_ref, g_ref, dres_ref, dy_ref, sums_ref, dyp_ref = refs

        @pl.when(pl.program_id(0) == 0)
        def _():
            sums_ref[...] = jnp.zeros(sums_ref.shape, F32)

        _, _, gate = _mod(mod_ref, k)
        dz = _ln_part_bwd(d_ref[...], x_ref[...], y_ref[...], gate, g_ref[...], sums_ref)
        dres_ref[...] = ALPHA * dz
        dyb = (dz * (1.0 + gate)).astype(BF16)
        dy_ref[...] = dyb
        dyp = _nt(dyb, w_ref[...])
        if rnn:
            act, dact = _gelu_parts(gt_ref[...].astype(F32))
            dhs = dyp * act
            for l in range(kdim // 128):
                dhs_ref[l] = dhs[:, 128 * l:128 * (l + 1)]
            dgt_ref[...] = (dyp * hs_ref[...].astype(F32) * dact).astype(BF16)
        else:
            dyp_ref[...] = dyp.astype(BF16)

    ins = [dxo, x, y, mods, w, lng] + (list(gate_act) if rnn else [])
    in_specs = [_row(tm, D_MODEL)] * 3 + [_res(mods.shape), _res(w.shape), _res(lng.shape)]
    out_shape = [jax.ShapeDtypeStruct((s, D_MODEL), F32), jax.ShapeDtypeStruct((s, D_MODEL), BF16),
                 jax.ShapeDtypeStruct((8, D_MODEL), F32)]
    out_specs = [_row(tm, D_MODEL), _row(tm, D_MODEL), _res((8, D_MODEL))]
    if rnn:
        in_specs += [_row(tm, kdim)] * 2
        out_shape += [jax.ShapeDtypeStruct((kdim // 128, s, 128), F32), jax.ShapeDtypeStruct((s, kdim), BF16)]
        out_specs += [pl.BlockSpec((kdim // 128, tm, 128), lambda i: (0, i, 0)), _row(tm, kdim)]
    else:
        out_shape.append(jax.ShapeDtypeStruct((s, kdim), BF16))
        out_specs.append(_row(tm, kdim))
    return _call(
        body, name="rnn_post_bwd" if rnn else "attn_post_bwd", grid=(s // tm,),
        out_shape=out_shape, in_specs=in_specs, out_specs=out_specs, args=ins, rider=rider)


def attn_bwd(q, kk, v, do, sinks, rider=None):
    s = q.shape[0]
    nblk = s // QBLK
    scale = HEAD ** -0.5

    def body(sink_ref, q_ref, do_ref, kp, ko, kn, vp, vo, vn, dq_ref, dk_ref, dv_ref, ds_ref):
        n = pl.program_id(0)

        @pl.when(n == 0)
        def _():
            ds_ref[...] = jnp.zeros(ds_ref.shape, F32)
            dk_ref[...] = jnp.zeros(dk_ref.shape, F32)
            dv_ref[...] = jnp.zeros(dv_ref.shape, F32)

        kall = jnp.concatenate([kp[...], ko[...], kn[...]], axis=0)
        vall = jnp.concatenate([vp[...], vo[...], vn[...]], axis=0)
        lane = lax.broadcasted_iota(jnp.int32, (1, 128), 1)
        dsink = jnp.zeros((1, 128), F32)
        for qb in range(QPAIR):
            nb = QPAIR * n + qb
            valid = _attn_mask(nb, s)
            keys = slice(QBLK * qb, QBLK * (qb + 3))
            for kv in range(N_KV):
                cols = slice(HEAD * kv, HEAD * (kv + 1))
                qg, dog = _stack_heads(q_ref, qb, kv), _stack_heads(do_ref, qb, kv)
                kh, vh = kall[keys, cols], vall[keys, cols]
                probs, psink = _attn_probs(qg, kh, valid, _stack_sinks(sink_ref, kv))
                dprobs = _nt(dog, vh)
                dvp = _tn(probs.astype(BF16), dog)
                rowdot = jnp.sum(probs * dprobs, axis=-1, keepdims=True)
                dsb = (probs * (dprobs - rowdot) * scale).astype(BF16)
                dqg = _nn(dsb, kh)
                dkp = _tn(dsb, qg)
                for p in range(3):
                    blk = jnp.clip(nb - 1 + p, 0, nblk - 1)
                    rows = pl.ds(pl.multiple_of(blk * QBLK, QBLK), QBLK)
                    dk_ref[rows, cols] += dkp[QBLK * p:QBLK * (p + 1), :]
                    dv_ref[rows, cols] += dvp[QBLK * p:QBLK * (p + 1), :]
                dsk = -psink * rowdot
                for j in range(GROUP):
                    hq = GROUP * kv + j
                    dq_ref[QBLK * qb:QBLK * (qb + 1), HEAD * hq:HEAD * (hq + 1)] = dqg[QBLK * j:QBLK * (j + 1), :]
                    dsink = dsink + jnp.where(lane == hq, _colsum(dsk[QBLK * j:QBLK * (j + 1), :]), 0.0)
        ds_ref[...] += dsink

    qspec = pl.BlockSpec((QPAIR * QBLK, N_Q * HEAD), lambda n: (n, 0))
    return _call(
        body, name="attn_bwd", grid=(nblk // QPAIR,),
        out_shape=[jax.ShapeDtypeStruct((s, N_Q * HEAD), F32),
                   jax.ShapeDtypeStruct((s, N_KV * HEAD), F32), jax.ShapeDtypeStruct((s, N_KV * HEAD), F32),
                   jax.ShapeDtypeStruct((1, 128), F32)],
        in_specs=[pl.BlockSpec(memory_space=pltpu.SMEM), qspec, qspec] + _kv_specs(nblk) + _kv_specs(nblk),
        out_specs=[qspec, _res((s, N_KV * HEAD)), _res((s, N_KV * HEAD)), pl.BlockSpec((1, 128), lambda n: (0, 0))],
        args=(sinks, q, do, kk, kk, kk, v, v, v), rider=rider)


def _in_bwd_tail(dzb, w_ref, x_ref, mod_ref, k, dres_ref, dx_ref, h_ref, sums_ref):
    xv = x_ref[...]
    shift, scale, _ = _mod(mod_ref, k)
    h_ref[...] = (xv * (1.0 + scale) + shift).astype(BF16)
    dh = _nn(dzb, w_ref[...])
    dx_ref[...] = dres_ref[...] + dh * (1.0 + scale)
    sums_ref[0:1, :] += _colsum(dh)
    sums_ref[1:2, :] += _colsum(dh * xv)


def attn_in_bwd(dq, dk, dv, rope, x, mods, k, win_t, dres):
    s = x.shape[0]
    tm = _blk(s, TM_MM)

    def body(dq_ref, dk_ref, dv_ref, c_ref, s1_ref, s2_ref, x_ref, mod_ref, w_ref, dres_ref,
             dx_ref, dz_ref, h_ref, sums_ref):
        @pl.when(pl.program_id(0) == 0)
        def _():
            sums_ref[...] = jnp.zeros(sums_ref.shape, F32)

        cos, s1, s2 = c_ref[...], s1_ref[...], s2_ref[...]
        for hh in range(N_Q + N_KV):
            src = dq_ref[:, HEAD * hh:HEAD * (hh + 1)] if hh < N_Q else dk_ref[:, HEAD * (hh - N_Q):HEAD * (hh - N_Q + 1)]
            dz_ref[:, HEAD * hh:HEAD * (hh + 1)] = _rope_bwd(src, cos, s1, s2).astype(BF16)
        dz_ref[:, HEAD * (N_Q + N_KV):] = dv_ref[...].astype(BF16)
        _in_bwd_tail(dz_ref[...], w_ref, x_ref, mod_ref, k, dres_ref, dx_ref, h_ref, sums_ref)

    return pl.pallas_call(
        body, name="attn_in_bwd", grid=(s // tm,),
        out_shape=[jax.ShapeDtypeStruct((s, D_MODEL), F32), jax.ShapeDtypeStruct((s, D_QKV), BF16),
                   jax.ShapeDtypeStruct((s, D_MODEL), BF16), jax.ShapeDtypeStruct((8, D_MODEL), F32)],
        in_specs=[_row(tm, N_Q * HEAD), _row(tm, N_KV * HEAD), _row(tm, N_KV * HEAD),
                  _row(tm, HEAD), _row(tm, HEAD), _row(tm, HEAD), _row(tm, D_MODEL),
                  _res(mods.shape), _res(win_t.shape), _row(tm, D_MODEL)],
        out_specs=[_row(tm, D_MODEL), _row(tm, D_QKV), _row(tm, D_MODEL), _res((8, D_MODEL))],
        compiler_params=_params(),
    )(dq, dk, dv, *rope, x, mods, win_t, dres)


def _shift_blk(v, k, before, after, row):
    n = v.shape[0]
    r = pltpu.roll(v, k % n, 0)
    for j in range(abs(k)):
        if k > 0:
            r = jnp.where(row == j, before[8 - k + j:8 - k + j + 1, :], r)
        else:
            r = jnp.where(row == n + k + j, after[j:j + 1, :], r)
    return r


def rnn_in_bwd(dxc_f, dxc_b, xr, cw, dgt, x, mods, k, win_t, dres):
    s = x.shape[0]
    tm = _blk(s, TM_MM)
    n = s // tm

    def body(f_ref, fp_ref, fn_ref, b_ref, bp_ref, bn_ref, xr_ref, xp_ref, xn_ref, cw_ref, dgt_ref,
             x_ref, mod_ref, w_ref, dres_ref, dx_ref, dz_ref, h_ref, sums_ref, dcw_ref, dcb_ref):
        i = pl.program_id(0)

        @pl.when(i == 0)
        def _():
            sums_ref[...] = jnp.zeros(sums_ref.shape, F32)
            dcw_ref[...] = jnp.zeros(dcw_ref.shape, F32)
            dcb_ref[...] = jnp.zeros(dcb_ref.shape, F32)

        d = _slab_rows(f_ref) + _slab_rows(b_ref)
        xv = xr_ref[...]
        first, last = i == 0, i == n - 1
        d_before = jnp.where(first, 0.0, _slab_rows(fp_ref) + _slab_rows(bp_ref))
        d_after = jnp.where(last, 0.0, _slab_rows(fn_ref) + _slab_rows(bn_ref))
        x_before = jnp.where(first, 0.0, xp_ref[...])
        x_after = jnp.where(last, 0.0, xn_ref[...])
        row = lax.broadcasted_iota(jnp.int32, d.shape, 0)
        dxr = (cw_ref[0:1, :] * _shift_blk(d, -2, d_before, d_after, row)
               + cw_ref[1:2, :] * _shift_blk(d, -1, d_before, d_after, row)
               + cw_ref[2:3, :] * d + cw_ref[3:4, :] * _shift_blk(d, 1, d_before, d_after, row))
        dcw_ref[0:1, :] += _colsum(d * _shift_blk(xv, 2, x_before, x_after, row))
        dcw_ref[1:2, :] += _colsum(d * _shift_blk(xv, 1, x_before, x_after, row))
        dcw_ref[2:3, :] += _colsum(d * xv)
        dcw_ref[3:4, :] += _colsum(d * _shift_blk(xv, -1, x_before, x_after, row))
        dcb_ref[...] += _colsum(d)
        dz_ref[:, 0:D_RNN] = dxr.astype(BF16)
        dz_ref[:, D_RNN:2 * D_RNN] = dgt_ref[...]
        _in_bwd_tail(dz_ref[...], w_ref, x_ref, mod_ref, k, dres_ref, dx_ref, h_ref, sums_ref)

    per8 = tm // 8
    ns = D_RNN // 128
    blk = _row(tm, D_RNN)
    before = pl.BlockSpec((8, D_RNN), lambda i: (jnp.maximum(i * per8 - 1, 0), 0))
    after = pl.BlockSpec((8, D_RNN), lambda i: (jnp.minimum((i + 1) * per8, s // 8 - 1), 0))
    sblk = pl.BlockSpec((ns, tm, 128), lambda i: (0, i, 0))
    sbefore = pl.BlockSpec((ns, 8, 128), lambda i: (0, jnp.maximum(i * per8 - 1, 0), 0))
    safter = pl.BlockSpec((ns, 8, 128), lambda i: (0, jnp.minimum((i + 1) * per8, s // 8 - 1), 0))
    return pl.pallas_call(
        body, name="rnn_in_bwd", grid=(n,),
        out_shape=[jax.ShapeDtypeStruct((s, D_MODEL), F32), jax.ShapeDtypeStruct((s, 2 * D_RNN), BF16),
                   jax.ShapeDtypeStruct((s, D_MODEL), BF16), jax.ShapeDtypeStruct((8, D_MODEL), F32),
                   jax.ShapeDtypeStruct((4, D_RNN), F32), jax.ShapeDtypeStruct((1, D_RNN), F32)],
        in_specs=[sblk, sbefore, safter] * 2 + [blk, before, after] + [
            _res(cw.shape), blk, _row(tm, D_MODEL), _res(mods.shape), _res(win_t.shape), _row(tm, D_MODEL)],
        out_specs=[_row(tm, D_MODEL), _row(tm, 2 * D_RNN), _row(tm, D_MODEL), _res((8, D_MODEL)),
                   _res((4, D_RNN)), _res((1, D_RNN))],
        compiler_params=_params(),
    )(dxc_f, dxc_f, dxc_f, dxc_b, dxc_b, dxc_b, xr, xr, xr, cw, dgt, x, mods, win_t, dres)


def wgrad(a, b, name, rider=None):
    s, m = a.shape
    n = b.shape[1]
    tm = next(t for t in (1024, 768, 512, 384, 256, 128) if m % t == 0)
    tk = _blk(s, TK_WG)
    nk = s // tk

    def body(a_ref, b_ref, o_ref, acc):
        kk = pl.program_id(1)

        @pl.when(kk == 0)
        def _():
            acc[...] = jnp.zeros(acc.shape, F32)

        acc[...] += _tn(a_ref[...], b_ref[...])

        @pl.when(kk == nk - 1)
        def _():
            o_ref[...] = acc[...].astype(BF16)

    out, *rode = _call(
        body, name=name, grid=(m // tm, nk),
        out_shape=[jax.ShapeDtypeStruct((m, n), BF16)],
        in_specs=[pl.BlockSpec((tk, tm), lambda i, kk: (kk, i)), pl.BlockSpec((tk, n), lambda i, kk: (kk, 0))],
        out_specs=[pl.BlockSpec((tm, n), lambda i, kk: (i, 0))],
        scratch_shapes=[pltpu.VMEM((tm, n), F32)],
        args=(a, b), rider=rider)
    out = out.reshape(N_DEV, m // N_DEV, n)
    return (out, *rode) if rider is not None else out


def part_sum(parts, name):
    _, r, c = parts.shape
    tr = next(t for t in (256, 192, 128, 64, 32, 16, 8) if r % t == 0)

    def body(p_ref, o_ref):
        acc = p_ref[0].astype(F32)
        for j in range(1, N_DEV):
            acc = acc + p_ref[j].astype(F32)
        o_ref[...] = acc

    return pl.pallas_call(
        body, name=name, grid=(r // tr,),
        out_shape=jax.ShapeDtypeStruct((r, c), F32),
        in_specs=[pl.BlockSpec((N_DEV, tr, c), lambda i: (0, i, 0))],
        out_specs=pl.BlockSpec((tr, c), lambda i: (i, 0)),
        compiler_params=_params(),
    )(parts)


def adamw(w, g, m, v, name):
    shape = w.shape
    c = shape[-1]
    r = w.size // c
    w2, g2, m2, v2 = (t.reshape(r, c) for t in (w, g, m, v))
    tr = r if r * c <= 512 * 1024 else next(t for t in (512, 256, 128, 64, 32, 16, 8) if r % t == 0)

    def body(w_ref, g_ref, m_ref, v_ref, d_ref, nm_ref, nv_ref):
        gv = g_ref[...]
        nm = B1 * m_ref[...] + (1.0 - B1) * gv
        nv = B2 * v_ref[...] + (1.0 - B2) * (gv * gv)
        nm_ref[...] = nm
        nv_ref[...] = nv
        m_hat = nm / (1.0 - B1 ** STEP)
        v_hat = nv / (1.0 - B2 ** STEP)
        d_ref[...] = -LR * (m_hat / (jnp.sqrt(v_hat) + ADAM_EPS) + WD * w_ref[...])

    spec = pl.BlockSpec((tr, c), lambda i: (i, 0))
    outs = pl.pallas_call(
        body, name=name, grid=(r // tr,),
        out_shape=[jax.ShapeDtypeStruct((r, c), F32)] * 3,
        in_specs=[spec] * 4, out_specs=[spec] * 3,
        compiler_params=_params(),
    )(w2, g2, m2, v2)
    return tuple(o.reshape(shape) for o in outs)


def _rope_tables(s):
    half = ROT // 2
    inv_freq = THETA ** (-jnp.arange(0, ROT, 2, dtype=F32) / ROT)
    per_row = 128 // half
    pos = (per_row * jnp.arange(s // per_row)[:, None] + jnp.arange(128)[None, :] // half).astype(F32)
    ang = pos * jnp.tile(inv_freq, per_row)[None, :]
    cos, sin = lax.optimization_barrier((jnp.cos(ang), jnp.sin(ang)))
    cos, sin = cos.reshape(s, half), sin.reshape(s, half)
    zeros = jnp.zeros((s, HEAD - ROT), F32)
    c = jnp.concatenate([cos, cos, jnp.ones((s, HEAD - ROT), F32)], axis=1)
    s1 = jnp.concatenate([jnp.zeros((s, half), F32), sin, zeros], axis=1)
    s2 = jnp.concatenate([-sin, jnp.zeros((s, half), F32), zeros], axis=1)
    return c, s1, s2


def _blockdiag(w):
    w4 = w.reshape(N_CG, 4, RB_W, RB_W)
    eye = jnp.eye(4, dtype=w.dtype)
    return jnp.einsum("gipq,ij->gipjq", w4, eye).reshape(N_CG, CG, CG)


def _diag_blocks(w):
    w5 = w.reshape(N_CG, 4, RB_W, 4, RB_W)
    eye = jnp.eye(4, dtype=w.dtype)
    return jnp.einsum("gipjq,ij->gipq", w5, eye).reshape(N_RB, RB_W, RB_W)


def _cols(full, per):
    lead = full.shape[:-1]
    t = full.reshape(lead + (N_DEV, per))
    return jnp.moveaxis(t, -2, 0).reshape(N_DEV, -1)


def kernel(x, c, ada_w, ada_b, ln_g, ln_b, attn_w_in, attn_w_out, attn_sinks, rnn_w_in, rnn_conv_w, rnn_conv_b, rnn_w_a, rnn_b_a, rnn_w_x, rnn_b_x, rnn_lam, rnn_w_out, mlp_w1, mlp_w2, loss_target, m_ada_w, m_ada_b, m_ln_g, m_ln_b, m_attn_w_in, m_attn_w_out, m_attn_sinks, m_rnn_w_in, m_rnn_conv_w, m_rnn_conv_b, m_rnn_w_a, m_rnn_b_a, m_rnn_w_x, m_rnn_b_x, m_rnn_lam, m_rnn_w_out, m_mlp_w1, m_mlp_w2, v_ada_w, v_ada_b, v_ln_g, v_ln_b, v_attn_w_in, v_attn_w_out, v_attn_sinks, v_rnn_w_in, v_rnn_conv_w, v_rnn_conv_b, v_rnn_w_a, v_rnn_b_a, v_rnn_w_x, v_rnn_b_x, v_rnn_lam, v_rnn_w_out, v_mlp_w1, v_mlp_w2):
    s = x.shape[1]
    x0 = x.reshape(s, D_MODEL)
    target = loss_target.reshape(s, D_MODEL)
    weights = dict(ada_w=ada_w, ada_b=ada_b, ln_g=ln_g, ln_b=ln_b, attn_w_in=attn_w_in, attn_w_out=attn_w_out,
                   attn_sinks=attn_sinks, rnn_w_in=rnn_w_in, rnn_conv_w=rnn_conv_w, rnn_conv_b=rnn_conv_b,
                   rnn_w_a=rnn_w_a, rnn_b_a=rnn_b_a, rnn_w_x=rnn_w_x, rnn_b_x=rnn_b_x, rnn_lam=rnn_lam,
                   rnn_w_out=rnn_w_out, mlp_w1=mlp_w1, mlp_w2=mlp_w2)
    moments_m = dict(ada_w=m_ada_w, ada_b=m_ada_b, ln_g=m_ln_g, ln_b=m_ln_b, attn_w_in=m_attn_w_in,
                     attn_w_out=m_attn_w_out, attn_sinks=m_attn_sinks, rnn_w_in=m_rnn_w_in,
                     rnn_conv_w=m_rnn_conv_w, rnn_conv_b=m_rnn_conv_b, rnn_w_a=m_rnn_w_a, rnn_b_a=m_rnn_b_a,
                     rnn_w_x=m_rnn_w_x, rnn_b_x=m_rnn_b_x, rnn_lam=m_rnn_lam, rnn_w_out=m_rnn_w_out,
                     mlp_w1=m_mlp_w1, mlp_w2=m_mlp_w2)
    moments_v = dict(ada_w=v_ada_w, ada_b=v_ada_b, ln_g=v_ln_g, ln_b=v_ln_b, attn_w_in=v_attn_w_in,
                     attn_w_out=v_attn_w_out, attn_sinks=v_attn_sinks, rnn_w_in=v_rnn_w_in,
                     rnn_conv_w=v_rnn_conv_w, rnn_conv_b=v_rnn_conv_b, rnn_w_a=v_rnn_w_a, rnn_b_a=v_rnn_b_a,
                     rnn_w_x=v_rnn_w_x, rnn_b_x=v_rnn_b_x, rnn_lam=v_rnn_lam, rnn_w_out=v_rnn_w_out,
                     mlp_w1=v_mlp_w1, mlp_w2=v_mlp_w2)
    names = list(weights)

    def t16(w):
        return w.T.astype(BF16)

    big = [t16(attn_w_in[0]), attn_w_out[0].astype(BF16), t16(rnn_w_in[0]), rnn_w_out[0].astype(BF16),
           t16(mlp_w1[0]), mlp_w2[0].astype(BF16), t16(mlp_w1[1]), mlp_w2[1].astype(BF16)]
    small_local = jnp.concatenate([
        ln_g.reshape(-1), ln_b.reshape(-1), rnn_conv_w.reshape(-1), rnn_conv_b.reshape(-1),
        rnn_b_a.reshape(-1), rnn_b_x.reshape(-1), rnn_lam.reshape(-1)])
    small_local = jnp.pad(small_local, (0, 4096 - small_local.shape[0])).reshape(32, 128)
    flat = lambda g: g.reshape(N_DEV * g.shape[1], D_MODEL)
    c_all, modr, win_t, sm = ada_modulation(jnp.broadcast_to(c, (8, D_MODEL)), ada_w.reshape(4, D_MODEL, CG),
                                            ada_b.reshape(4, 1, CG), _Gather([big[0], small_local]))
    win_t = flat(win_t)
    sm = sm.reshape(N_DEV, 4096)

    def full_vec(off, rows, per):
        piece = sm[:, off:off + rows * per].reshape(N_DEV, rows, per)
        return jnp.moveaxis(piece, 0, 1).reshape(rows, N_DEV * per)

    lng_f, lnb_f = full_vec(0, 4, 128), full_vec(512, 4, 128)
    cw_f, cb_f = full_vec(1024, 4, 192), full_vec(1792, 1, 192)
    ba_f, bx_f, lam_f = full_vec(1984, 2, 192), full_vec(2368, 2, 192), full_vec(2752, 2, 192)
    wa_bd = [_blockdiag(rnn_w_a[0, d]).astype(BF16) for d in range(2)]
    wx_bd = [_blockdiag(rnn_w_x[0, d]).astype(BF16) for d in range(2)]

    mods = modr.reshape(N_DEV, 4, 8, CG)[:, :, 0, :]
    mods = jnp.moveaxis(mods, 0, 1).reshape(4, 3, D_MODEL).reshape(12, D_MODEL)
    rope = _rope_tables(s)
    ln = lambda k: (lng_f[k:k + 1], lnb_f[k:k + 1])

    q, kk, v, wout = attn_in_fwd(x0, mods, 0, win_t, rope, rider=_Gather([big[1]]))
    wout = flat(wout)
    o, *got = attn_fwd(q, kk, v, attn_sinks, rider=_Gather([big[4], big[5]]))
    w1t_0, w2_0 = (flat(g) for g in got)
    x1, y0, rout = post_fwd(o, wout, x0, mods, 0, *ln(0), rider=_Gather([big[3]]))
    rout = flat(rout)
    x2, y1, ra0, r0, *got = mlp_fwd(x1, mods, 1, w1t_0, w2_0, *ln(1), rider=_Gather([big[2], big[6], big[7]]))
    rin_t, w1t_1, w2_1 = (flat(g) for g in got)
    xr, gt = rnn_in_fwd(x2, mods, 2, rin_t)
    xc = conv_fwd(xr, cw_f, cb_f)
    hf, = lru_fwd(xc, wa_bd[0], wx_bd[0], ba_f[0:1], bx_f[0:1], lam_f[0:1], False)
    hb, hsum = lru_fwd(xc, wa_bd[1], wx_bd[1], ba_f[1:2], bx_f[1:2], lam_f[1:2], True, other=hf)
    x3, y2, ypre = post_fwd(None, rout, x2, mods, 2, *ln(2), gate_act=(gt, hsum))
    y3, ra1, r1 = mlp_fwd(x3, mods, 3, w1t_1, w2_1, *ln(3), last=True)

    dx3, da1, h3, dy3, sums3 = mlp_bwd(target, x3, y3, ra1, mods, 3, w1t_1, w2_1, lng_f[3:4], lnb=lnb_f[3:4])
    g_w1t_1 = wgrad(da1, h3, "wgrad_w1_1")
    g_w2_1 = wgrad(r1, dy3, "wgrad_w2_1")
    dres2, dy2, sums2a, dhs, dgt, p_w1t_1 = post_bwd(dx3, x2, y2, mods, 2, rout, lng_f[2:3], gate_act=(gt, hsum),
                                                     rider=_AllToAll([g_w1t_1]))
    g_rout = wgrad(ypre, dy2, "wgrad_rnn_out")
    dxc_f, dwa_f, dwx_f, dba_f, dbx_f, dlam_f, p_w2_1, p_rout = lru_bwd(
        xc, dhs, hf, wa_bd[0], wx_bd[0], ba_f[0:1], bx_f[0:1], lam_f[0:1], False, rider=_AllToAll([g_w2_1, g_rout]))
    dxc_b, dwa_b, dwx_b, dba_b, dbx_b, dlam_b = lru_bwd(xc, dhs, hb, wa_bd[1], wx_bd[1], ba_f[1:2], bx_f[1:2],
                                                        lam_f[1:2], True)
    dx2, dzz, h2, sums2b, dcw, dcb = rnn_in_bwd(dxc_f, dxc_b, xr, cw_f, dgt, x2, mods, 2, rin_t, dres2)
    g_rin_t = wgrad(dzz, h2, "wgrad_rnn_in")
    d_wa = jnp.stack([_diag_blocks(dwa_f), _diag_blocks(dwa_b)])
    d_wx = jnp.stack([_diag_blocks(dwx_f), _diag_blocks(dwx_b)])
    nflat = d_wa.size // N_DEV
    gates = jnp.concatenate([d_wa.reshape(N_DEV, nflat), d_wx.reshape(N_DEV, nflat)], axis=1)
    gates = gates.reshape(N_DEV, 2 * nflat // 128, 128)
    dx1, da0, h1, dy1, sums1, p_rin_t, p_gates = mlp_bwd(dx2, x1, y1, ra0, mods, 1, w1t_0, w2_0, lng_f[1:2],
                                                         rider=_AllToAll([g_rin_t, gates]))
    gates_sum = part_sum(p_gates, "part_sum_gates")
    g_w1t_0 = wgrad(da0, h1, "wgrad_w1_0")
    g_w2_0, p_w1t_0 = wgrad(r0, dy1, "wgrad_w2_0", rider=_AllToAll([g_w1t_0]))
    dres0, dy0, sums0a, do = post_bwd(dx1, x0, y0, mods, 0, wout, lng_f[0:1])
    g_wout = wgrad(o, dy0, "wgrad_attn_out")
    dq, dk, dv, dsink, wag, p_w2_0, p_wout = attn_bwd(
        q, kk, v, do, attn_sinks, rider=_Multi(_Gather([gates_sum]), _AllToAll([g_w2_0, g_wout])))
    dx0, dqkv, h0, sums0b = attn_in_bwd(dq, dk, dv, rope, x0, mods, 0, win_t, dres0)
    g_win_t = wgrad(dqkv, h0, "wgrad_attn_in")

    sums = [sums0a + sums0b, sums1, sums2a + sums2b, sums3]
    gmod = jnp.stack([t[0:3] for t in sums])
    gsend = jnp.moveaxis(gmod.reshape(4, N_DEV, CG), 1, 0)
    gsend = jnp.pad(gsend, ((0, 0), (0, 4), (0, 0)))
    c_t = c_all[:, 0, :].T
    sq_err = jnp.sum(sums3[5]).reshape(1, 1)
    tail = jnp.concatenate([
        _cols(dcw, 192), _cols(dcb, 192),
        _cols(jnp.concatenate([dba_f, dba_b]), 192), _cols(jnp.concatenate([dbx_f, dbx_b]), 192),
        _cols(jnp.concatenate([dlam_f, dlam_b]), 192),
        _cols(jnp.stack([t[3] for t in sums]), 128), _cols(jnp.stack([t[4] for t in sums]), 128),
        jnp.broadcast_to(dsink[:, 0:8], (N_DEV, 8)), jnp.broadcast_to(sq_err, (N_DEV, 1))], axis=1)
    tail = jnp.pad(tail, ((0, 0), (0, 32 * 128 - tail.shape[1]))).reshape(N_DEV, 32, 128)
    g_ada_w, g_ada_b, red, p_win_t = epilogue(gsend, c_t, tail, _AllToAll([g_win_t]))
    grads = {"ada_w": g_ada_w.reshape(ada_w.shape), "ada_b": g_ada_b[0:4].reshape(ada_b.shape)}

    big_parts = [p_win_t, p_wout, p_rin_t, p_rout, p_w1t_0, p_w2_0, p_w1t_1, p_w2_1]
    gsum = [part_sum(p, "part_sum_%d" % i) for i, p in enumerate(big_parts)]
    grads.update({
        "attn_w_in": gsum[0].T[None], "attn_w_out": gsum[1][None],
        "rnn_w_in": gsum[2].T[None], "rnn_w_out": gsum[3][None],
        "mlp_w1": jnp.stack([gsum[4].T, gsum[6].T]), "mlp_w2": jnp.stack([gsum[5], gsum[7]]),
    })
    wag = wag.reshape(N_DEV, 2 * nflat)
    grads["rnn_w_a"] = wag[:, :nflat].reshape(rnn_w_a.shape)
    grads["rnn_w_x"] = wag[:, nflat:].reshape(rnn_w_x.shape)
    tl = red.reshape(-1)
    loss = 0.5 * tl[3144] / D_MODEL
    grads["rnn_conv_w"] = tl[0:768].reshape(rnn_conv_w.shape)
    grads["rnn_conv_b"] = tl[768:960].reshape(rnn_conv_b.shape)
    grads["rnn_b_a"] = tl[960:1344].reshape(rnn_b_a.shape)
    grads["rnn_b_x"] = tl[1344:1728].reshape(rnn_b_x.shape)
    grads["rnn_lam"] = tl[1728:2112].reshape(rnn_lam.shape)
    grads["ln_g"] = tl[2112:2624].reshape(ln_g.shape)
    grads["ln_b"] = tl[2624:3136].reshape(ln_b.shape)
    grads["attn_sinks"] = tl[3136:3144].reshape(attn_sinks.shape)

    delta, new_m, new_v = {}, {}, {}
    for n in names:
        delta[n], new_m[n], new_v[n] = adamw(weights[n], grads[n], moments_m[n], moments_v[n], "adamw_" + n)
    return (loss, dx0.reshape(x.shape), *[grads[n] for n in names], *[delta[n] for n in names],
            *[new_m[n] for n in names], *[new_v[n] for n in names])
```

```python
import functools
import math

import jax
import jax.numpy as jnp
from jax import lax
from jax.experimental import pallas as pl
from jax.experimental.pallas import tpu as pltpu

F32, BF16 = jnp.float32, jnp.bfloat16
MESH = pl.DeviceIdType.MESH

D_MODEL = 1024
N_Q, N_KV, HEAD = 8, 2, 128
ROT, THETA = 32, 500000.0
QBLK = 128
D_QKV = (N_Q + 2 * N_KV) * HEAD
D_RNN, N_RB, RB_W = 1536, 16, 96
CG = 384
N_CG = D_RNN // CG
D_FF = 4096
FF_CHUNK = 1024
DEPTH = 2
ALPHA = (2.0 * DEPTH) ** 0.25
LN_EPS = 1e-5
LRU_C = 8.0
N_DEV = 8
LR, B1, B2, ADAM_EPS, WD, STEP = 0.001, 0.9, 0.999, 1e-8, 0.01, 10

VMEM_LIMIT = 56 * 1024 * 1024
TM_MM = 512
TM_MLP = 256
TM_MLP_FWD = 512
TT_RNN = 2048
TK_WG = 2048


def _nn(a, b):
    return jnp.dot(a, b, preferred_element_type=F32)


def _nt(a, b):
    return lax.dot_general(a, b, (((1,), (1,)), ((), ())), preferred_element_type=F32)


def _tn(a, b):
    return lax.dot_general(a, b, (((0,), (0,)), ((), ())), preferred_element_type=F32)


def _blk(n, pref):
    t = min(n, pref)
    assert n % t == 0, (n, pref)
    return t


def _params(**kw):
    return pltpu.CompilerParams(vmem_limit_bytes=VMEM_LIMIT, **kw)


def _row(tm, w):
    return pl.BlockSpec((tm, w), lambda i: (i, 0))


def _res(shape):
    return pl.BlockSpec(shape, lambda i: (0,) * len(shape), pipeline_mode=pl.Buffered(1))


def _mod(mod_ref, k):
    return mod_ref[3 * k:3 * k + 1, :], mod_ref[3 * k + 1:3 * k + 2, :], mod_ref[3 * k + 2:3 * k + 3, :]


def _ln_stats(z):
    mu = jnp.mean(z, axis=-1, keepdims=True)
    zc = z - mu
    var = jnp.mean(zc * zc, axis=-1, keepdims=True)
    rstd = lax.rsqrt(var + LN_EPS)
    return zc * rstd, rstd


def _ln_bwd(dxo, xhat, rstd, g):
    dxh = dxo * g
    m1 = jnp.mean(dxh, axis=-1, keepdims=True)
    m2 = jnp.mean(dxh * xhat, axis=-1, keepdims=True)
    return rstd * (dxh - m1 - xhat * m2)


def _colsum(v):
    return jnp.sum(v, axis=0, keepdims=True)


def _sigmoid(v):
    return 0.5 * jnp.tanh(0.5 * v) + 0.5


def _gelu_parts(v):
    k = math.sqrt(2.0 / math.pi)
    u = k * (v + 0.044715 * v * v * v)
    t = jnp.tanh(u)
    g = 0.5 * v * (1.0 + t)
    dg = 0.5 * (1.0 + t) + 0.5 * v * (1.0 - t * t) * k * (1.0 + 3.0 * 0.044715 * v * v)
    return g, dg


def _me():
    return lax.axis_index("x"), lax.axis_index("y"), lax.axis_index("c")


def _idx(p):
    return 4 * p[0] + 2 * p[1] + p[2]


def _peers(me):
    x, y, c = me
    out = []
    for k in range(1, N_DEV):
        out.append((1 - x if k & 4 else x, 1 - y if k & 2 else y, 1 - c if k & 1 else c))
    return out


class _Gather:
    def __init__(self, srcs):
        self.srcs = list(srcs)
        n = len(self.srcs)
        self.out_shape = [jax.ShapeDtypeStruct((N_DEV,) + s.shape, s.dtype) for s in self.srcs]
        self.scratch = [pltpu.SemaphoreType.DMA((n, 7)), pltpu.SemaphoreType.DMA((n, 7)),
                        pltpu.SemaphoreType.DMA((n,))]

    @staticmethod
    def _places():
        x, y, c = me = _me()
        return me, (x, y, 1 - c), [(1 - x, y), (x, 1 - y), (1 - x, 1 - y)]

    @staticmethod
    def _copy(outs, sems, t, k, block, to, src=None):
        slot = outs[t].at[_idx(block)]
        return pltpu.make_async_remote_copy(
            src_ref=slot if src is None else src, dst_ref=slot, send_sem=sems[0].at[t, k],
            recv_sem=sems[1].at[t, k], device_id=to, device_id_type=MESH)

    def _firsts(self, ins, outs, sems):
        me, sibling, chips = self._places()
        out = []
        for t in range(len(ins)):
            out.append(self._copy(outs, sems, t, 0, me, sibling, src=ins[t]))
            out += [self._copy(outs, sems, t, 1 + j, me, (*chip, me[2]), src=ins[t]) for j, chip in enumerate(chips)]
        return out

    def _locals(self, ins, outs, sems):
        me = _me()
        return [pltpu.make_async_copy(ins[t], outs[t].at[_idx(me)], sems[2].at[t]) for t in range(len(ins))]

    def start(self, ins, outs, sems):
        for cp in self._locals(ins, outs, sems) + self._firsts(ins, outs, sems):
            cp.start()

    def mid(self, ins, outs, sems):
        me, sibling, chips = self._places()
        for j, chip in enumerate(chips):
            for t in range(len(ins)):
                self._copy(outs, sems, t, 1 + j, (*chip, me[2]), me).wait_recv()
                self._copy(outs, sems, t, 4 + j, (*chip, me[2]), sibling).start()

    def finish(self, ins, outs, sems):
        me, sibling, chips = self._places()
        for t in range(len(ins)):
            self._copy(outs, sems, t, 0, sibling, me).wait_recv()
            for j, chip in enumerate(chips):
                self._copy(outs, sems, t, 4 + j, (*chip, 1 - me[2]), me).wait_recv()
        for cp in self._firsts(ins, outs, sems):
            cp.wait_send()
        for j, chip in enumerate(chips):
            for t in range(len(ins)):
                self._copy(outs, sems, t, 4 + j, (*chip, me[2]), sibling).wait_send()
        for cp in self._locals(ins, outs, sems):
            cp.wait()


class _AllToAll:
    def __init__(self, srcs):
        self.srcs = list(srcs)
        n = len(self.srcs)
        self.out_shape = [jax.ShapeDtypeStruct(s.shape, s.dtype) for s in self.srcs]
        self.scratch = [pltpu.SemaphoreType.DMA((n, 7)), pltpu.SemaphoreType.DMA((n, 7)),
                        pltpu.SemaphoreType.DMA((n,))]

    def _copies(self, ins, outs, sems):
        me = _me()
        loc, rem = [], []
        for t in range(len(ins)):
            loc.append(pltpu.make_async_copy(ins[t].at[_idx(me)], outs[t].at[_idx(me)], sems[2].at[t]))
            for k, p in enumerate(_peers(me)):
                rem.append(pltpu.make_async_remote_copy(
                    src_ref=ins[t].at[_idx(p)], dst_ref=outs[t].at[_idx(me)], send_sem=sems[0].at[t, k],
                    recv_sem=sems[1].at[t, k], device_id=p, device_id_type=MESH))
        return loc, rem

    def start(self, ins, outs, sems):
        loc, rem = self._copies(ins, outs, sems)
        for cp in loc + rem:
            cp.start()

    def mid(self, ins, outs, sems):
        pass

    def finish(self, ins, outs, sems):
        me = _me()
        for t in range(len(ins)):
            for k, p in enumerate(_peers(me)):
                slot = outs[t].at[_idx(p)]
                pltpu.make_async_remote_copy(
                    src_ref=slot, dst_ref=slot, send_sem=sems[0].at[t, k], recv_sem=sems[1].at[t, k],
                    device_id=p, device_id_type=MESH).wait_recv()
        loc, rem = self._copies(ins, outs, sems)
        for cp in rem:
            cp.wait_send()
        for cp in loc:
            cp.wait()


class _Multi:
    def __init__(self, *exs):
        self.exs = exs
        self.srcs = [s for e in exs for s in e.srcs]
        self.out_shape = [s for e in exs for s in e.out_shape]
        self.scratch = [s for e in exs for s in e.scratch]

    def _each(self, ins, outs, sems):
        i = j = 0
        for e in self.exs:
            n, m = len(e.srcs), len(e.scratch)
            yield e, ins[i:i + n], outs[i:i + n], sems[j:j + m]
            i, j = i + n, j + m

    def start(self, ins, outs, sems):
        for e, a, b, c in self._each(ins, outs, sems):
            e.start(a, b, c)

    def mid(self, ins, outs, sems):
        for e, a, b, c in self._each(ins, outs, sems):
            e.mid(a, b, c)

    def finish(self, ins, outs, sems):
        for e, a, b, c in self._each(ins, outs, sems):
            e.finish(a, b, c)


def _call(body, *, name, grid, in_specs, out_specs, out_shape, args, scratch_shapes=(), rider=None):
    in_specs, out_specs, out_shape = list(in_specs), list(out_specs), list(out_shape)
    scratch_shapes = list(scratch_shapes)
    if rider is None:
        return pl.pallas_call(body, name=name, grid=grid, out_shape=out_shape, in_specs=in_specs,
                              out_specs=out_specs, scratch_shapes=scratch_shapes, compiler_params=_params())(*args)
    nci, nco, ncs, nr = len(in_specs), len(out_shape), len(scratch_shapes), len(rider.srcs)
    nsteps = math.prod(grid)
    assert nsteps >= 2, (name, grid)
    mid = max(1, (7 * nsteps) // 8)

    def full(*refs):
        ci, ri = refs[:nci], refs[nci:nci + nr]
        co, ro = refs[nci + nr:nci + nr + nco], refs[nci + nr + nco:nci + 2 * nr + nco]
        cs, rs = refs[nci + 2 * nr + nco:nci + 2 * nr + nco + ncs], refs[nci + 2 * nr + nco + ncs:]
        step = pl.program_id(0)
        for d in range(1, len(grid)):
            step = step * grid[d] + pl.program_id(d)

        @pl.when(step == 0)
        def _():
            rider.start(ri, ro, rs)

        @pl.when(step == mid)
        def _():
            rider.mid(ri, ro, rs)

        body(*ci, *co, *cs)

        @pl.when(step == nsteps - 1)
        def _():
            rider.finish(ri, ro, rs)

    any_spec = pl.BlockSpec(memory_space=pl.ANY)
    return pl.pallas_call(
        full, name=name, grid=grid, out_shape=out_shape + rider.out_shape,
        in_specs=in_specs + [any_spec] * nr, out_specs=out_specs + [any_spec] * nr,
        scratch_shapes=scratch_shapes + rider.scratch, compiler_params=_params(),
    )(*args, *rider.srcs)


def _a2a_start(srcs, dsts, send_sems, recv_sems, local_sems, me, sem_base=0):
    peers = _peers(me)
    started = []
    for t in range(len(srcs)):
        loc = pltpu.make_async_copy(srcs[t].at[_idx(me)], dsts[t].at[_idx(me)], local_sems.at[sem_base + t])
        loc.start()
        started.append(("local", loc))
        for k, p in enumerate(peers):
            cp = pltpu.make_async_remote_copy(
                src_ref=srcs[t].at[_idx(p)], dst_ref=dsts[t].at[_idx(me)],
                send_sem=send_sems.at[sem_base + t, k], recv_sem=recv_sems.at[sem_base + t, k],
                device_id=p, device_id_type=MESH)
            cp.start()
            started.append(("remote", cp))
    return started


def _a2a_finish(started, dsts, send_sems, recv_sems, me, sem_base=0):
    peers = _peers(me)
    for t in range(len(dsts)):
        for k, p in enumerate(peers):
            slot = dsts[t].at[_idx(p)]
            pltpu.make_async_remote_copy(
                src_ref=slot, dst_ref=slot, send_sem=send_sems.at[sem_base + t, k],
                recv_sem=recv_sems.at[sem_base + t, k], device_id=p, device_id_type=MESH).wait_recv()
    for kind, cp in started:
        if kind == "local":
            cp.wait()
        else:
            cp.wait_send()


def ada_modulation(c8, ada_w, ada_b, ride):
    nr = len(ride.srcs)

    def body(c_ref, w_ref, b_ref, *rest):
        ride_in, (call_ref, modr_ref), ride_out = rest[:nr], rest[nr:nr + 2], rest[nr + 2:2 * nr + 2]
        modp, send_sems, recv_sems, local_sems = rest[2 * nr + 2:2 * nr + 6]
        ride_sems = rest[2 * nr + 6:]
        ride.start(ride_in, ride_out, ride_sems)
        me = _me()
        peers = _peers(me)
        sends = []
        for k, p in enumerate(peers):
            cp = pltpu.make_async_remote_copy(
                src_ref=c_ref, dst_ref=call_ref.at[_idx(me)], send_sem=send_sems.at[0, k],
                recv_sem=recv_sems.at[0, k], device_id=p, device_id_type=MESH)
            cp.start()
            sends.append(cp)
        call_ref[_idx(me)] = c_ref[...]
        for k, p in enumerate(peers):
            slot = call_ref.at[_idx(p)]
            pltpu.make_async_remote_copy(
                src_ref=slot, dst_ref=slot, send_sem=send_sems.at[0, k], recv_sem=recv_sems.at[0, k],
                device_id=p, device_id_type=MESH).wait_recv()
        for cp in sends:
            cp.wait_send()
        cv = call_ref[...].reshape(N_DEV * 8, D_MODEL)
        s = (cv * _sigmoid(cv)).astype(BF16)
        for k in range(4):
            res = _nn(s, w_ref[k].astype(BF16)) + b_ref[k]
            for j in range(N_DEV):
                modp[j, 8 * k:8 * k + 8, :] = res[8 * j:8 * j + 8, :]
        started = _a2a_start([modp], [modr_ref], send_sems, recv_sems, local_sems, me, sem_base=1)
        _a2a_finish(started, [modr_ref], send_sems, recv_sems, me, sem_base=1)
        ride.mid(ride_in, ride_out, ride_sems)
        ride.finish(ride_in, ride_out, ride_sems)

    vm, hbm = pl.BlockSpec(memory_space=pltpu.VMEM), pl.BlockSpec(memory_space=pl.ANY)
    return pl.pallas_call(
        body, name="ada_modulation",
        out_shape=[jax.ShapeDtypeStruct((N_DEV, 8, D_MODEL), F32), jax.ShapeDtypeStruct((N_DEV, 32, CG), F32)]
        + ride.out_shape,
        in_specs=[vm, vm, vm] + [hbm] * nr, out_specs=[vm, vm] + [hbm] * nr,
        scratch_shapes=[pltpu.VMEM((N_DEV, 32, CG), F32), pltpu.SemaphoreType.DMA((2, 7)),
                        pltpu.SemaphoreType.DMA((2, 7)), pltpu.SemaphoreType.DMA((2,))] + ride.scratch,
        compiler_params=_params(),
    )(c8, ada_w, ada_b, *ride.srcs)


def epilogue(gsend, c_t, tail, ride):
    nr = len(ride.srcs)
    rt = tail.shape[1]

    def body(g_ref, ct_ref, t_ref, *rest):
        ride_in, (gw_ref, gb_ref, red_ref), ride_out = rest[:nr], rest[nr:nr + 3], rest[nr + 3:2 * nr + 3]
        grecv, trecv, send_sems, recv_sems, local_sems = rest[2 * nr + 3:2 * nr + 8]
        ride_sems = rest[2 * nr + 8:]
        ride.start(ride_in, ride_out, ride_sems)
        me = _me()
        started = _a2a_start([g_ref, t_ref], [grecv, trecv], send_sems, recv_sems, local_sems, me)
        _a2a_finish(started, [grecv, trecv], send_sems, recv_sems, me)
        acc = trecv[0]
        for j in range(1, N_DEV):
            acc = acc + trecv[j]
        red_ref[...] = acc
        ct = ct_ref[...]
        st = (ct * _sigmoid(ct)).astype(BF16).astype(F32)
        gb = jnp.zeros((8, CG), F32)
        for b in range(N_DEV):
            gb = gb + grecv[b]
        gb_ref[...] = gb
        for k in range(4):
            acc = jnp.zeros((D_MODEL, CG), F32)
            for b in range(N_DEV):
                row = grecv[b, k:k + 1, :].astype(BF16).astype(F32)
                acc = acc + st[:, b:b + 1] * row
            gw_ref[k] = acc
        ride.mid(ride_in, ride_out, ride_sems)
        ride.finish(ride_in, ride_out, ride_sems)

    vm, hbm = pl.BlockSpec(memory_space=pltpu.VMEM), pl.BlockSpec(memory_space=pl.ANY)
    return pl.pallas_call(
        body, name="epilogue",
        out_shape=[jax.ShapeDtypeStruct((4, D_MODEL, CG), F32), jax.ShapeDtypeStruct((8, CG), F32),
                   jax.ShapeDtypeStruct((rt, 128), F32)] + ride.out_shape,
        in_specs=[vm, vm, vm] + [hbm] * nr, out_specs=[vm, vm, vm] + [hbm] * nr,
        scratch_shapes=[pltpu.VMEM((N_DEV, 8, CG), F32), pltpu.VMEM((N_DEV, rt, 128), F32),
                        pltpu.SemaphoreType.DMA((2, 7)), pltpu.SemaphoreType.DMA((2, 7)),
                        pltpu.SemaphoreType.DMA((2,))] + ride.scratch,
        compiler_params=_params(),
    )(gsend, c_t, tail, *ride.srcs)


def _rope(t, cos, s1, s2):
    return t * cos + pltpu.roll(t, 16, 1) * s1 + pltpu.roll(t, HEAD - 16, 1) * s2


def _rope_bwd(d, cos, s1, s2):
    return d * cos + pltpu.roll(d * s1, HEAD - 16, 1) + pltpu.roll(d * s2, 16, 1)


def attn_in_fwd(x, mods, k, win_t, rope, rider=None):
    s = x.shape[0]
    tm = _blk(s, TM_MM)

    def body(x_ref, mod_ref, w_ref, c_ref, s1_ref, s2_ref, q_ref, k_ref, v_ref):
        shift, scale, _ = _mod(mod_ref, k)
        h = (x_ref[...] * (1.0 + scale) + shift).astype(BF16)
        qkv = _nt(h, w_ref[...])
        cos, s1, s2 = c_ref[...], s1_ref[...], s2_ref[...]
        for hh in range(N_Q + N_KV):
            r = _rope(qkv[:, HEAD * hh:HEAD * (hh + 1)], cos, s1, s2).astype(BF16)
            if hh < N_Q:
                q_ref[:, HEAD * hh:HEAD * (hh + 1)] = r
            else:
                k_ref[:, HEAD * (hh - N_Q):HEAD * (hh - N_Q + 1)] = r
        v_ref[...] = qkv[:, HEAD * (N_Q + N_KV):].astype(BF16)

    return _call(
        body, name="attn_in_fwd", grid=(s // tm,),
        out_shape=[jax.ShapeDtypeStruct((s, N_Q * HEAD), BF16), jax.ShapeDtypeStruct((s, N_KV * HEAD), BF16),
                   jax.ShapeDtypeStruct((s, N_KV * HEAD), BF16)],
        in_specs=[_row(tm, D_MODEL), _res(mods.shape), _res(win_t.shape),
                  _row(tm, HEAD), _row(tm, HEAD), _row(tm, HEAD)],
        out_specs=[_row(tm, N_Q * HEAD), _row(tm, N_KV * HEAD), _row(tm, N_KV * HEAD)],
        args=(x, mods, win_t, *rope), rider=rider)


QPAIR = 8


def _kv_specs(nblk):
    w = N_KV * HEAD
    return [pl.BlockSpec((QBLK, w), lambda n: (jnp.maximum(QPAIR * n - 1, 0), 0)),
            pl.BlockSpec((QPAIR * QBLK, w), lambda n: (n, 0)),
            pl.BlockSpec((QBLK, w), lambda n: (jnp.minimum(QPAIR * (n + 1), nblk - 1), 0))]


GROUP = N_Q // N_KV


def _attn_mask(n, s):
    qi = lax.broadcasted_iota(jnp.int32, (GROUP * QBLK, 3 * QBLK), 0) & (QBLK - 1)
    kj = lax.broadcasted_iota(jnp.int32, (GROUP * QBLK, 3 * QBLK), 1)
    rel = kj - QBLK - qi
    kpos = kj + (n - 1) * QBLK
    return (jnp.abs(rel) <= QBLK) & (kpos >= 0) & (kpos < s)


def _stack_heads(ref, qb, kv):
    rows = slice(QBLK * qb, QBLK * (qb + 1))
    return jnp.concatenate([ref[rows, HEAD * (GROUP * kv + j):HEAD * (GROUP * kv + j + 1)] for j in range(GROUP)],
                           axis=0)


def _stack_sinks(sink_ref, kv):
    row = lax.broadcasted_iota(jnp.int32, (GROUP * QBLK, 1), 0)
    out = jnp.full((GROUP * QBLK, 1), sink_ref[0, GROUP * kv + GROUP - 1], F32)
    for j in range(GROUP - 2, -1, -1):
        out = jnp.where(row < QBLK * (j + 1), sink_ref[0, GROUP * kv + j], out)
    return out


def _attn_probs(qh, kh, valid, sink):
    sc = _nt(qh, kh) * (HEAD ** -0.5)
    sc = jnp.where(valid, sc, -1e30)
    m = jnp.maximum(jnp.max(sc, axis=-1, keepdims=True), sink)
    p = jnp.exp(sc - m)
    es = jnp.exp(sink - m)
    denom = jnp.sum(p, axis=-1, keepdims=True) + es
    return p / denom, es / denom


def attn_fwd(q, kk, v, sinks, rider=None):
    s = q.shape[0]
    nblk = s // QBLK

    def body(sink_ref, q_ref, kp, ko, kn, vp, vo, vn, o_ref):
        n = pl.program_id(0)
        kall = jnp.concatenate([kp[...], ko[...], kn[...]], axis=0)
        vall = jnp.concatenate([vp[...], vo[...], vn[...]], axis=0)
        for qb in range(QPAIR):
            valid = _attn_mask(QPAIR * n + qb, s)
            keys = slice(QBLK * qb, QBLK * (qb + 3))
            for kv in range(N_KV):
                cols = slice(HEAD * kv, HEAD * (kv + 1))
                probs, _ = _attn_probs(_stack_heads(q_ref, qb, kv), kall[keys, cols], valid,
                                       _stack_sinks(sink_ref, kv))
                og = _nn(probs.astype(BF16), vall[keys, cols]).astype(BF16)
                for j in range(GROUP):
                    hq = GROUP * kv + j
                    o_ref[QBLK * qb:QBLK * (qb + 1), HEAD * hq:HEAD * (hq + 1)] = og[QBLK * j:QBLK * (j + 1), :]

    qspec = pl.BlockSpec((QPAIR * QBLK, N_Q * HEAD), lambda n: (n, 0))
    return _call(
        body, name="attn_fwd", grid=(nblk // QPAIR,),
        out_shape=[jax.ShapeDtypeStruct((s, N_Q * HEAD), BF16)],
        in_specs=[pl.BlockSpec(memory_space=pltpu.SMEM), qspec] + _kv_specs(nblk) + _kv_specs(nblk),
        out_specs=[qspec],
        args=(sinks, q, kk, kk, kk, v, v, v), rider=rider)


def post_fwd(ypre, w, x, mods, k, lng, lnb, gate_act=None, rider=None):
    s = x.shape[0]
    tm = _blk(s, TM_MM)
    kdim = w.shape[0]
    rnn = gate_act is not None

    def body(*refs):
        if rnn:
            gt_ref, hs_ref, w_ref, x_ref, mod_ref, g_ref, b_ref, xo_ref, y_ref, yp_ref = refs
            act, _ = _gelu_parts(gt_ref[...].astype(F32))
            yp = (hs_ref[...].astype(F32) * act).astype(BF16)
            yp_ref[...] = yp
        else:
            yp_ref, w_ref, x_ref, mod_ref, g_ref, b_ref, xo_ref, y_ref = refs
            yp = yp_ref[...]
        _, _, gate = _mod(mod_ref, k)
        y = _nn(yp, w_ref[...])
        y_ref[...] = y
        xhat, _ = _ln_stats(ALPHA * x_ref[...] + (1.0 + gate) * y)
        xo_ref[...] = xhat * g_ref[...] + b_ref[...]

    act_in = list(gate_act) if rnn else [ypre]
    out_shape = [jax.ShapeDtypeStruct((s, D_MODEL), F32), jax.ShapeDtypeStruct((s, D_MODEL), F32)]
    out_specs = [_row(tm, D_MODEL), _row(tm, D_MODEL)]
    if rnn:
        out_shape.append(jax.ShapeDtypeStruct((s, kdim), BF16))
        out_specs.append(_row(tm, kdim))
    return _call(
        body, name="rnn_post_fwd" if rnn else "attn_post_fwd", grid=(s // tm,),
        out_shape=out_shape,
        in_specs=[_row(tm, kdim)] * len(act_in) + [_res(w.shape), _row(tm, D_MODEL), _res(mods.shape),
                                                    _res(lng.shape), _res(lnb.shape)],
        out_specs=out_specs,
        args=(*act_in, w, x, mods, lng, lnb), rider=rider)


def mlp_fwd(x, mods, k, w1_t, w2, lng, lnb, rider=None, last=False):
    s = x.shape[0]
    tm = _blk(s, TM_MLP_FWD)

    def body(x_ref, mod_ref, w1_ref, w2_ref, g_ref, b_ref, *outs):
        xo_ref = None if last else outs[0]
        y_ref, ra_ref, r_ref = outs[-3:]
        xv = x_ref[...]
        shift, scale, gate = _mod(mod_ref, k)
        h = (xv * (1.0 + scale) + shift).astype(BF16)
        y = jnp.zeros((tm, D_MODEL), F32)
        for c in range(D_FF // FF_CHUNK):
            rows = slice(FF_CHUNK * c, FF_CHUNK * (c + 1))
            a = jnp.maximum(_nt(h, w1_ref[rows, :]), 0.0)
            r = (a * a).astype(BF16)
            ra_ref[:, rows] = a.astype(BF16)
            r_ref[:, rows] = r
            y = y + _nn(r, w2_ref[rows, :])
        y_ref[...] = y
        if not last:
            xhat, _ = _ln_stats(ALPHA * xv + (1.0 + gate) * y)
            xo_ref[...] = xhat * g_ref[...] + b_ref[...]

    nf = 1 if last else 2
    return _call(
        body, name="mlp_fwd_last" if last else "mlp_fwd", grid=(s // tm,),
        out_shape=[jax.ShapeDtypeStruct((s, D_MODEL), F32)] * nf + [jax.ShapeDtypeStruct((s, D_FF), BF16)] * 2,
        in_specs=[_row(tm, D_MODEL), _res(mods.shape), _res(w1_t.shape), _res(w2.shape),
                  _res(lng.shape), _res(lnb.shape)],
        out_specs=[_row(tm, D_MODEL)] * nf + [_row(tm, D_FF)] * 2,
        args=(x, mods, w1_t, w2, lng, lnb), rider=rider)


def rnn_in_fwd(x, mods, k, win_t):
    s = x.shape[0]
    tm = _blk(s, TM_MM)

    def body(x_ref, mod_ref, w_ref, xr_ref, gt_ref):
        shift, scale, _ = _mod(mod_ref, k)
        h = (x_ref[...] * (1.0 + scale) + shift).astype(BF16)
        xr_ref[...] = _nt(h, w_ref[0:D_RNN, :])
        gt_ref[...] = _nt(h, w_ref[D_RNN:2 * D_RNN, :]).astype(BF16)

    return pl.pallas_call(
        body, name="rnn_in_fwd", grid=(s // tm,),
        out_shape=[jax.ShapeDtypeStruct((s, D_RNN), F32), jax.ShapeDtypeStruct((s, D_RNN), BF16)],
        in_specs=[_row(tm, D_MODEL), _res(mods.shape), _res(win_t.shape)],
        out_specs=[_row(tm, D_RNN)] * 2,
        compiler_params=_params(),
    )(x, mods, win_t)


def _shift_rows(v, k, row):
    n = v.shape[0]
    r = pltpu.roll(v, k % n, 0)
    keep = (row >= k) if k > 0 else (row < n + k)
    return jnp.where(keep, r, 0.0)


def conv_fwd(xr, cw, cb):
    s = xr.shape[0]

    def body(x_ref, w_ref, b_ref, o_ref):
        xv = x_ref[...]
        row = lax.broadcasted_iota(jnp.int32, xv.shape, 0)
        o_ref[...] = (b_ref[...] + w_ref[0:1, :] * _shift_rows(xv, 2, row) + w_ref[1:2, :] * _shift_rows(xv, 1, row)
                      + w_ref[2:3, :] * xv + w_ref[3:4, :] * _shift_rows(xv, -1, row))

    slab = pl.BlockSpec((s, 128), lambda j: (0, j))
    return pl.pallas_call(
        body, name="conv_fwd", grid=(D_RNN // 128,),
        out_shape=jax.ShapeDtypeStruct((D_RNN // 128, s, 128), F32),
        in_specs=[slab, pl.BlockSpec((4, 128), lambda j: (0, j)), pl.BlockSpec((1, 128), lambda j: (0, j))],
        out_specs=pl.BlockSpec((None, s, 128), lambda j: (j, 0, 0)),
        compiler_params=_params(),
    )(xr, cw, cb)


def _softplus_neg(lam):
    z = -lam
    e = jnp.exp(-jnp.abs(z))
    u = 1.0 + e
    log1p = jnp.where(u == 1.0, e, jnp.log(u) * e / jnp.where(u == 1.0, 1.0, u - 1.0))
    return jnp.maximum(z, 0.0) + log1p, 1.0 / (1.0 + jnp.exp(lam))


def _lru_gates(xv, wa_ref, wx_ref, ba_ref, bx_ref, lam_ref):
    xb = xv.astype(BF16)
    r = _sigmoid(_nn(xb, wa_ref[...]) + ba_ref[...])
    i = _sigmoid(_nn(xb, wx_ref[...]) + bx_ref[...])
    sp, sg = _softplus_neg(lam_ref[...])
    la = r * (-LRU_C * sp)
    a = jnp.exp(la)
    th = jnp.tanh(la)
    m2 = -2.0 * th / (1.0 - th)
    rmult = lax.rsqrt(jnp.maximum(m2, 1e-37))
    return xb, r, i, sp, sg, a, m2 * rmult, rmult


SLABS = CG // 128
GRP, SEG = 32, 4


def _lru_specs(nt, tt, reverse):
    tmap = (lambda t: nt - 1 - t) if reverse else (lambda t: t)
    blk = pl.BlockSpec((tt, CG), lambda g, t: (tmap(t), g))
    slabs = pl.BlockSpec((SLABS, tt, 128), lambda g, t: (g, tmap(t), 0))
    wsp = pl.BlockSpec((None, CG, CG), lambda g, t: (g, 0, 0))
    vec = pl.BlockSpec((1, CG), lambda g, t: (0, g))
    return tmap, blk, slabs, wsp, vec


def _slab_rows(ref3):
    return jnp.concatenate([ref3[l] for l in range(ref3.shape[0])], axis=1)


def _perm_load(ref3, tt):
    out = []
    for l in range(SLABS):
        r = ref3.at[l]
        out.append(jnp.concatenate([r[pl.ds(GRP * g + i, 8, stride=SEG), :]
                                    for g in range(tt // GRP) for i in range(SEG)], axis=0))
    return jnp.concatenate(out, axis=1)


def _perm_scan(a, u, carry, reverse, emit):
    n, c = a.shape
    sub = lax.broadcasted_iota(jnp.int32, (8, c), 0)
    steps = [(8 - sh, sub < 8 - sh) if reverse else (sh, sub >= sh) for sh in (1, 2, 4)]
    order = range(SEG - 1, -1, -1) if reverse else range(SEG)
    for g in (range(n // GRP - 1, -1, -1) if reverse else range(n // GRP)):
        hs, ps = [None] * SEG, [None] * SEG
        h = p = None
        for i in order:
            rows = slice(GRP * g + 8 * i, GRP * g + 8 * i + 8)
            h = u[rows] if h is None else a[rows] * h + u[rows]
            p = a[rows] if p is None else a[rows] * p
            hs[i], ps[i] = h, p
        d, f = p, h
        for rot, keep in steps:
            d_s = jnp.where(keep, pltpu.roll(d, rot, 0), 1.0)
            f_s = jnp.where(keep, pltpu.roll(f, rot, 0), 0.0)
            f = d * f_s + f
            d = d * d_s
        end = f + d * carry
        if reverse:
            init = jnp.where(sub == 7, carry, pltpu.roll(end, 7, 0))
            carry = jnp.broadcast_to(end[0:1], (8, c))
        else:
            init = jnp.where(sub == 0, carry, pltpu.roll(end, 1, 0))
            carry = jnp.broadcast_to(end[7:8], (8, c))
        for i in range(SEG):
            emit(g, i, hs[i] + ps[i] * init)
    return carry


def lru_fwd(xc, wa, wx, ba, bx, lam, reverse, other=None):
    s = xc.shape[1]
    tt = _blk(s, TT_RNN)
    nt = s // tt

    def body(x_ref, wa_ref, wx_ref, ba_ref, bx_ref, lam_ref, *rest):
        if other is None:
            hs_ref, carry = rest
        else:
            oth_ref, hs_ref, sum_ref, carry = rest

        @pl.when(pl.program_id(1) == 0)
        def _():
            carry[...] = jnp.zeros(carry.shape, F32)

        xv = _perm_load(x_ref, tt)
        _, _, i, _, _, a, mult, _ = _lru_gates(xv, wa_ref, wx_ref, ba_ref, bx_ref, lam_ref)
        u = mult * (i * xv)

        def emit(g, j, rows):
            for l in range(SLABS):
                hs_ref.at[l][pl.ds(GRP * g + j, 8, stride=SEG), :] = rows[:, 128 * l:128 * (l + 1)]

        carry[...] = _perm_scan(a, u, carry[...], reverse, emit)
        if other is not None:
            for l in range(SLABS):
                sum_ref[:, 128 * l:128 * (l + 1)] = (hs_ref[l] + oth_ref[l]).astype(BF16)

    _, blk, slabs, wsp, vec = _lru_specs(nt, tt, reverse)
    extra = [] if other is None else [other]
    return pl.pallas_call(
        body, name="lru_fwd_rev" if reverse else "lru_fwd", grid=(N_CG, nt),
        out_shape=[jax.ShapeDtypeStruct(xc.shape, F32)] + [jax.ShapeDtypeStruct((s, D_RNN), BF16)] * len(extra),
        in_specs=[slabs, wsp, wsp, vec, vec, vec] + [slabs] * len(extra), out_specs=[slabs] + [blk] * len(extra),
        scratch_shapes=[pltpu.VMEM((8, CG), F32)],
        compiler_params=_params(),
    )(xc, wa, wx, ba, bx, lam, *extra)


def lru_bwd(xc, dhs, hs, wa, wx, ba, bx, lam, reverse, rider=None):
    s = xc.shape[1]
    tt = _blk(s, TT_RNN)
    nt = s // tt
    ng = tt // GRP
    back = not reverse

    def rows_of(v, g, i):
        return v[GRP * g + 8 * i:GRP * g + 8 * i + 8]

    def neighbour(v, past, edge, sub):
        out = []
        for g in range(ng):
            for i in range(SEG):
                if past and i > 0:
                    r = rows_of(v, g, i - 1)
                elif past:
                    e = edge if g == 0 else rows_of(v, g - 1, SEG - 1)[7:8]
                    r = jnp.where(sub == 0, e, pltpu.roll(rows_of(v, g, SEG - 1), 1, 0))
                elif i < SEG - 1:
                    r = rows_of(v, g, i + 1)
                else:
                    e = edge if g == ng - 1 else rows_of(v, g + 1, 0)[0:1]
                    r = jnp.where(sub == 7, e, pltpu.roll(rows_of(v, g, 0), 7, 0))
                out.append(r)
        return jnp.concatenate(out, axis=0)

    def body(x_ref, dh_ref, hs_ref, nb_ref, wa_ref, wx_ref, ba_ref, bx_ref, lam_ref,
             dx_ref, dwa_ref, dwx_ref, dba_ref, dbx_ref, dlam_ref, carry):
        t = pl.program_id(1)

        @pl.when(t == 0)
        def _():
            carry[...] = jnp.zeros(carry.shape, F32)
            dwa_ref[...] = jnp.zeros(dwa_ref.shape, F32)
            dwx_ref[...] = jnp.zeros(dwx_ref.shape, F32)
            dba_ref[...] = jnp.zeros(dba_ref.shape, F32)
            dbx_ref[...] = jnp.zeros(dbx_ref.shape, F32)
            dlam_ref[...] = jnp.zeros(dlam_ref.shape, F32)

        xv = _perm_load(x_ref, tt)
        xb, r, i, sp, sg, a, mult, rmult = _lru_gates(xv, wa_ref, wx_ref, ba_ref, bx_ref, lam_ref)
        sub = lax.broadcasted_iota(jnp.int32, (8, CG), 0)
        hsv = _perm_load(hs_ref, tt)
        nbv = _slab_rows(nb_ref)
        inner = t < nt - 1
        h_edge = jnp.where(inner, nbv[0:1, :] if reverse else nbv[7:8, :], 0.0)
        hprev = neighbour(hsv, not reverse, h_edge, sub)
        a_next = neighbour(a, reverse, carry[8:9, :], sub)
        dhv = _perm_load(dh_ref, tt)
        gl = [None] * (ng * SEG)

        def emit(gi, j, rows):
            gl[gi * SEG + j] = rows

        carry[0:8, :] = _perm_scan(a_next, dhv, carry[0:8, :], back, emit)
        g = jnp.concatenate(gl, axis=0)
        carry[8:9, :] = a[0:1, :] if back else a[tt - 1:tt, :]

        da = g * hprev
        dmult = g * (i * xv)
        di = g * mult * xv
        dla = da * a - dmult * (a * a) * rmult
        dpa = (dla * (-LRU_C * sp)) * r * (1.0 - r)
        dpx = di * i * (1.0 - i)
        dlam_ref[...] += _colsum(dla * (LRU_C * r * sg))
        dba_ref[...] += _colsum(dpa)
        dbx_ref[...] += _colsum(dpx)
        dpab, dpxb = dpa.astype(BF16), dpx.astype(BF16)
        dxv = g * mult * i + _nt(dpab, wa_ref[...]) + _nt(dpxb, wx_ref[...])
        for gi in range(ng):
            for j in range(SEG):
                for l in range(SLABS):
                    dx_ref.at[l][pl.ds(GRP * gi + j, 8, stride=SEG), :] = rows_of(dxv, gi, j)[:, 128 * l:128 * (l + 1)]
        dwa_ref[...] += _tn(xb, dpab)
        dwx_ref[...] += _tn(xb, dpxb)

    tmap, blk, slabs, wsp, vec = _lru_specs(nt, tt, back)
    per8 = tt // 8
    if reverse:
        nb = pl.BlockSpec((SLABS, 8, 128), lambda g, t: (g, jnp.minimum((tmap(t) + 1) * per8, s // 8 - 1), 0))
    else:
        nb = pl.BlockSpec((SLABS, 8, 128), lambda g, t: (g, jnp.maximum(tmap(t) * per8 - 1, 0), 0))
    return _call(
        body, name="lru_bwd_rev" if reverse else "lru_bwd", grid=(N_CG, nt),
        out_shape=[jax.ShapeDtypeStruct(xc.shape, F32), jax.ShapeDtypeStruct((N_CG, CG, CG), F32),
                   jax.ShapeDtypeStruct((N_CG, CG, CG), F32)] + [jax.ShapeDtypeStruct((1, D_RNN), F32)] * 3,
        in_specs=[slabs, slabs, slabs, nb, wsp, wsp, vec, vec, vec],
        out_specs=[slabs, wsp, wsp, vec, vec, vec],
        scratch_shapes=[pltpu.VMEM((16, CG), F32)],
        args=(xc, dhs, hs, hs, wa, wx, ba, bx, lam), rider=rider)


def _ln_part_bwd(dxo, x, y, gate, g, sums_ref, loss_head=None):
    xhat, rstd = _ln_stats(ALPHA * x + (1.0 + gate) * y)
    if loss_head is not None:
        err = xhat * g + loss_head[0] - loss_head[1]
        dxo = err * (1.0 / D_MODEL)
        sums_ref[5:6, :] += _colsum(err * err)
    dz = _ln_bwd(dxo, xhat, rstd, g)
    sums_ref[2:3, :] += _colsum(dz * y)
    sums_ref[3:4, :] += _colsum(dxo * xhat)
    sums_ref[4:5, :] += _colsum(dxo)
    return dz


def mlp_bwd(dxo, x, y, ra, mods, k, w1_t, w2, lng, lnb=None, rider=None):
    s = x.shape[0]
    tm = _blk(s, TM_MLP)
    head = lnb is not None

    def body(d_ref, x_ref, y_ref, ra_ref, mod_ref, w1_ref, w2_ref, g_ref, *rest):
        b_ref = rest[0] if head else None
        dx_ref, da_ref, h_ref, dy_ref, sums_ref = rest[1:] if head else rest

        @pl.when(pl.program_id(0) == 0)
        def _():
            sums_ref[...] = jnp.zeros(sums_ref.shape, F32)

        xv = x_ref[...]
        shift, scale, gate = _mod(mod_ref, k)
        if head:
            dz = _ln_part_bwd(None, xv, y_ref[...], gate, g_ref[...], sums_ref, (b_ref[...], d_ref[...]))
        else:
            dz = _ln_part_bwd(d_ref[...], xv, y_ref[...], gate, g_ref[...], sums_ref)
        dyb = (dz * (1.0 + gate)).astype(BF16)
        dy_ref[...] = dyb
        h = (xv * (1.0 + scale) + shift).astype(BF16)
        h_ref[...] = h
        dh = jnp.zeros((tm, D_MODEL), F32)
        for c in range(D_FF // FF_CHUNK):
            rows = slice(FF_CHUNK * c, FF_CHUNK * (c + 1))
            da = (_nt(dyb, w2_ref[rows, :]) * (2.0 * ra_ref[:, rows].astype(F32))).astype(BF16)
            da_ref[:, rows] = da
            dh = dh + _nn(da, w1_ref[rows, :])
        dx_ref[...] = ALPHA * dz + dh * (1.0 + scale)
        sums_ref[0:1, :] += _colsum(dh)
        sums_ref[1:2, :] += _colsum(dh * xv)

    return _call(
        body, name="mlp_bwd", grid=(s // tm,),
        out_shape=[jax.ShapeDtypeStruct((s, D_MODEL), F32), jax.ShapeDtypeStruct((s, D_FF), BF16),
                   jax.ShapeDtypeStruct((s, D_MODEL), BF16),
                   jax.ShapeDtypeStruct((s, D_MODEL), BF16), jax.ShapeDtypeStruct((8, D_MODEL), F32)],
        in_specs=[_row(tm, D_MODEL)] * 3 + [_row(tm, D_FF), _res(mods.shape), _res(w1_t.shape), _res(w2.shape),
                                             _res(lng.shape)] + ([_res(lnb.shape)] if head else []),
        out_specs=[_row(tm, D_MODEL), _row(tm, D_FF), _row(tm, D_MODEL), _row(tm, D_MODEL), _res((8, D_MODEL))],
        args=(dxo, x, y, ra, mods, w1_t, w2, lng) + ((lnb,) if head else ()), rider=rider)


def post_bwd(dxo, x, y, mods, k, w, lng, gate_act=None, rider=None):
    s = x.shape[0]
    tm = _blk(s, TM_MM)
    kdim = w.shape[0]
    rnn = gate_act is not None

    def body(*refs):
        if rnn:
            (d_ref, x_ref, y_ref, mod_ref, w_ref, g_ref, gt_ref, hs_ref,
             dres_ref, dy_ref, sums_ref, dhs_ref, dgt_ref) = refs
        else:
            d_ref, x_ref, y_ref, mod_ref, w_ref, g_ref, dres_ref, dy_ref, sums_ref, dyp_ref = refs

        @pl.when(pl.program_id(0) == 0)
        def _():
            sums_ref[...] = jnp.zeros(sums_ref.shape, F32)

        _, _, gate = _mod(mod_ref, k)
        dz = _ln_part_bwd(d_ref[...], x_ref[...], y_ref[...], gate, g_ref[...], sums_ref)
        dres_ref[...] = ALPHA * dz
        dyb = (dz * (1.0 + gate)).astype(BF16)
        dy_ref[...] = dyb
        dyp = _nt(dyb, w_ref[...])
        if rnn:
            act, dact = _gelu_parts(gt_ref[...].astype(F32))
            dhs = dyp * act
            for l in range(kdim // 128):
                dhs_ref[l] = dhs[:, 128 * l:128 * (l + 1)]
            dgt_ref[...] = (dyp * hs_ref[...].astype(F32) * dact).astype(BF16)
        else:
            dyp_ref[...] = dyp.astype(BF16)

    ins = [dxo, x, y, mods, w, lng] + (list(gate_act) if rnn else [])
    in_specs = [_row(tm, D_MODEL)] * 3 + [_res(mods.shape), _res(w.shape), _res(lng.shape)]
    out_shape = [jax.ShapeDtypeStruct((s, D_MODEL), F32), jax.ShapeDtypeStruct((s, D_MODEL), BF16),
                 jax.ShapeDtypeStruct((8, D_MODEL), F32)]
    out_specs = [_row(tm, D_MODEL), _row(tm, D_MODEL), _res((8, D_MODEL))]
    if rnn:
        in_specs += [_row(tm, kdim)] * 2
        out_shape += [jax.ShapeDtypeStruct((kdim // 128, s, 128), F32), jax.ShapeDtypeStruct((s, kdim), BF16)]
        out_specs += [pl.BlockSpec((kdim // 128, tm, 128), lambda i: (0, i, 0)), _row(tm, kdim)]
    else:
        out_shape.append(jax.ShapeDtypeStruct((s, kdim), BF16))
        out_specs.append(_row(tm, kdim))
    return _call(
        body, name="rnn_post_bwd" if rnn else "attn_post_bwd", grid=(s // tm,),
        out_shape=out_shape, in_specs=in_specs, out_specs=out_specs, args=ins, rider=rider)


def attn_bwd(q, kk, v, do, sinks, rider=None):
    s = q.shape[0]
    nblk = s // QBLK
    scale = HEAD ** -0.5

    def body(sink_ref, q_ref, do_ref, kp, ko, kn, vp, vo, vn, dq_ref, dk_ref, dv_ref, ds_ref):
        n = pl.program_id(0)

        @pl.when(n == 0)
        def _():
            ds_ref[...] = jnp.zeros(ds_ref.shape, F32)
            dk_ref[...] = jnp.zeros(dk_ref.shape, F32)
            dv_ref[...] = jnp.zeros(dv_ref.shape, F32)

        kall = jnp.concatenate([kp[...], ko[...], kn[...]], axis=0)
        vall = jnp.concatenate([vp[...], vo[...], vn[...]], axis=0)
        lane = lax.broadcasted_iota(jnp.int32, (1, 128), 1)
        dsink = jnp.zeros((1, 128), F32)
        for qb in range(QPAIR):
            nb = QPAIR * n + qb
            valid = _attn_mask(nb, s)
            keys = slice(QBLK * qb, QBLK * (qb + 3))
            for kv in range(N_KV):
                cols = slice(HEAD * kv, HEAD * (kv + 1))
                qg, dog = _stack_heads(q_ref, qb, kv), _stack_heads(do_ref, qb, kv)
                kh, vh = kall[keys, cols], vall[keys, cols]
                probs, psink = _attn_probs(qg, kh, valid, _stack_sinks(sink_ref, kv))
                dprobs = _nt(dog, vh)
                dvp = _tn(probs.astype(BF16), dog)
                rowdot = jnp.sum(probs * dprobs, axis=-1, keepdims=True)
                dsb = (probs * (dprobs - rowdot) * scale).astype(BF16)
                dqg = _nn(dsb, kh)
                dkp = _tn(dsb, qg)
                for p in range(3):
                    blk = jnp.clip(nb - 1 + p, 0, nblk - 1)
                    rows = pl.ds(pl.multiple_of(blk * QBLK, QBLK), QBLK)
                    dk_ref[rows, cols] += dkp[QBLK * p:QBLK * (p + 1), :]
                    dv_ref[rows, cols] += dvp[QBLK * p:QBLK * (p + 1), :]
                dsk = -psink * rowdot
                for j in range(GROUP):
                    hq = GROUP * kv + j
                    dq_ref[QBLK * qb:QBLK * (qb + 1), HEAD * hq:HEAD * (hq + 1)] = dqg[QBLK * j:QBLK * (j + 1), :]
                    dsink = dsink + jnp.where(lane == hq, _colsum(dsk[QBLK * j:QBLK * (j + 1), :]), 0.0)
        ds_ref[...] += dsink

    qspec = pl.BlockSpec((QPAIR * QBLK, N_Q * HEAD), lambda n: (n, 0))
    return _call(
        body, name="attn_bwd", grid=(nblk // QPAIR,),
        out_shape=[jax.ShapeDtypeStruct((s, N_Q * HEAD), F32),
                   jax.ShapeDtypeStruct((s, N_KV * HEAD), F32), jax.ShapeDtypeStruct((s, N_KV * HEAD), F32),
                   jax.ShapeDtypeStruct((1, 128), F32)],
        in_specs=[pl.BlockSpec(memory_space=pltpu.SMEM), qspec, qspec] + _kv_specs(nblk) + _kv_specs(nblk),
        out_specs=[qspec, _res((s, N_KV * HEAD)), _res((s, N_KV * HEAD)), pl.BlockSpec((1, 128), lambda n: (0, 0))],
        args=(sinks, q, do, kk, kk, kk, v, v, v), rider=rider)


def _in_bwd_tail(dzb, w_ref, x_ref, mod_ref, k, dres_ref, dx_ref, h_ref, sums_ref):
    xv = x_ref[...]
    shift, scale, _ = _mod(mod_ref, k)
    h_ref[...] = (xv * (1.0 + scale) + shift).astype(BF16)
    dh = _nn(dzb, w_ref[...])
    dx_ref[...] = dres_ref[...] + dh * (1.0 + scale)
    sums_ref[0:1, :] += _colsum(dh)
    sums_ref[1:2, :] += _colsum(dh * xv)


def attn_in_bwd(dq, dk, dv, rope, x, mods, k, win_t, dres):
    s = x.shape[0]
    tm = _blk(s, TM_MM)

    def body(dq_ref, dk_ref, dv_ref, c_ref, s1_ref, s2_ref, x_ref, mod_ref, w_ref, dres_ref,
             dx_ref, dz_ref, h_ref, sums_ref):
        @pl.when(pl.program_id(0) == 0)
        def _():
            sums_ref[...] = jnp.zeros(sums_ref.shape, F32)

        cos, s1, s2 = c_ref[...], s1_ref[...], s2_ref[...]
        for hh in range(N_Q + N_KV):
            src = dq_ref[:, HEAD * hh:HEAD * (hh + 1)] if hh < N_Q else dk_ref[:, HEAD * (hh - N_Q):HEAD * (hh - N_Q + 1)]
            dz_ref[:, HEAD * hh:HEAD * (hh + 1)] = _rope_bwd(src, cos, s1, s2).astype(BF16)
        dz_ref[:, HEAD * (N_Q + N_KV):] = dv_ref[...].astype(BF16)
        _in_bwd_tail(dz_ref[...], w_ref, x_ref, mod_ref, k, dres_ref, dx_ref, h_ref, sums_ref)

    return pl.pallas_call(
        body, name="attn_in_bwd", grid=(s // tm,),
        out_shape=[jax.ShapeDtypeStruct((s, D_MODEL), F32), jax.ShapeDtypeStruct((s, D_QKV), BF16),
                   jax.ShapeDtypeStruct((s, D_MODEL), BF16), jax.ShapeDtypeStruct((8, D_MODEL), F32)],
        in_specs=[_row(tm, N_Q * HEAD), _row(tm, N_KV * HEAD), _row(tm, N_KV * HEAD),
                  _row(tm, HEAD), _row(tm, HEAD), _row(tm, HEAD), _row(tm, D_MODEL),
                  _res(mods.shape), _res(win_t.shape), _row(tm, D_MODEL)],
        out_specs=[_row(tm, D_MODEL), _row(tm, D_QKV), _row(tm, D_MODEL), _res((8, D_MODEL))],
        compiler_params=_params(),
    )(dq, dk, dv, *rope, x, mods, win_t, dres)


def _shift_blk(v, k, before, after, row):
    n = v.shape[0]
    r = pltpu.roll(v, k % n, 0)
    for j in range(abs(k)):
        if k > 0:
            r = jnp.where(row == j, before[8 - k + j:8 - k + j + 1, :], r)
        else:
            r = jnp.where(row == n + k + j, after[j:j + 1, :], r)
    return r


def rnn_in_bwd(dxc_f, dxc_b, xr, cw, dgt, x, mods, k, win_t, dres):
    s = x.shape[0]
    tm = _blk(s, TM_MM)
    n = s // tm

    def body(f_ref, fp_ref, fn_ref, b_ref, bp_ref, bn_ref, xr_ref, xp_ref, xn_ref, cw_ref, dgt_ref,
             x_ref, mod_ref, w_ref, dres_ref, dx_ref, dz_ref, h_ref, sums_ref, dcw_ref, dcb_ref):
        i = pl.program_id(0)

        @pl.when(i == 0)
        def _():
            sums_ref[...] = jnp.zeros(sums_ref.shape, F32)
            dcw_ref[...] = jnp.zeros(dcw_ref.shape, F32)
            dcb_ref[...] = jnp.zeros(dcb_ref.shape, F32)

        d = _slab_rows(f_ref) + _slab_rows(b_ref)
        xv = xr_ref[...]
        first, last = i == 0, i == n - 1
        d_before = jnp.where(first, 0.0, _slab_rows(fp_ref) + _slab_rows(bp_ref))
        d_after = jnp.where(last, 0.0, _slab_rows(fn_ref) + _slab_rows(bn_ref))
        x_before = jnp.where(first, 0.0, xp_ref[...])
        x_after = jnp.where(last, 0.0, xn_ref[...])
        row = lax.broadcasted_iota(jnp.int32, d.shape, 0)
        dxr = (cw_ref[0:1, :] * _shift_blk(d, -2, d_before, d_after, row)
               + cw_ref[1:2, :] * _shift_blk(d, -1, d_before, d_after, row)
               + cw_ref[2:3, :] * d + cw_ref[3:4, :] * _shift_blk(d, 1, d_before, d_after, row))
        dcw_ref[0:1, :] += _colsum(d * _shift_blk(xv, 2, x_before, x_after, row))
        dcw_ref[1:2, :] += _colsum(d * _shift_blk(xv, 1, x_before, x_after, row))
        dcw_ref[2:3, :] += _colsum(d * xv)
        dcw_ref[3:4, :] += _colsum(d * _shift_blk(xv, -1, x_before, x_after, row))
        dcb_ref[...] += _colsum(d)
        dz_ref[:, 0:D_RNN] = dxr.astype(BF16)
        dz_ref[:, D_RNN:2 * D_RNN] = dgt_ref[...]
        _in_bwd_tail(dz_ref[...], w_ref, x_ref, mod_ref, k, dres_ref, dx_ref, h_ref, sums_ref)

    per8 = tm // 8
    ns = D_RNN // 128
    blk = _row(tm, D_RNN)
    before = pl.BlockSpec((8, D_RNN), lambda i: (jnp.maximum(i * per8 - 1, 0), 0))
    after = pl.BlockSpec((8, D_RNN), lambda i: (jnp.minimum((i + 1) * per8, s // 8 - 1), 0))
    sblk = pl.BlockSpec((ns, tm, 128), lambda i: (0, i, 0))
    sbefore = pl.BlockSpec((ns, 8, 128), lambda i: (0, jnp.maximum(i * per8 - 1, 0), 0))
    safter = pl.BlockSpec((ns, 8, 128), lambda i: (0, jnp.minimum((i + 1) * per8, s // 8 - 1), 0))
    return pl.pallas_call(
        body, name="rnn_in_bwd", grid=(n,),
        out_shape=[jax.ShapeDtypeStruct((s, D_MODEL), F32), jax.ShapeDtypeStruct((s, 2 * D_RNN), BF16),
                   jax.ShapeDtypeStruct((s, D_MODEL), BF16), jax.ShapeDtypeStruct((8, D_MODEL), F32),
                   jax.ShapeDtypeStruct((4, D_RNN), F32), jax.ShapeDtypeStruct((1, D_RNN), F32)],
        in_specs=[sblk, sbefore, safter] * 2 + [blk, before, after] + [
            _res(cw.shape), blk, _row(tm, D_MODEL), _res(mods.shape), _res(win_t.shape), _row(tm, D_MODEL)],
        out_specs=[_row(tm, D_MODEL), _row(tm, 2 * D_RNN), _row(tm, D_MODEL), _res((8, D_MODEL)),
                   _res((4, D_RNN)), _res((1, D_RNN))],
        compiler_params=_params(),
    )(dxc_f, dxc_f, dxc_f, dxc_b, dxc_b, dxc_b, xr, xr, xr, cw, dgt, x, mods, win_t, dres)


def wgrad(a, b, name, rider=None):
    s, m = a.shape
    n = b.shape[1]
    tm = next(t for t in (1024, 768, 512, 384, 256, 128) if m % t == 0)
    tk = _blk(s, TK_WG)
    nk = s // tk

    def body(a_ref, b_ref, o_ref, acc):
        kk = pl.program_id(1)

        @pl.when(kk == 0)
        def _():
            acc[...] = jnp.zeros(acc.shape, F32)

        acc[...] += _tn(a_ref[...], b_ref[...])

        @pl.when(kk == nk - 1)
        def _():
            o_ref[...] = acc[...].astype(BF16)

    out, *rode = _call(
        body, name=name, grid=(m // tm, nk),
        out_shape=[jax.ShapeDtypeStruct((m, n), BF16)],
        in_specs=[pl.BlockSpec((tk, tm), lambda i, kk: (kk, i)), pl.BlockSpec((tk, n), lambda i, kk: (kk, 0))],
        out_specs=[pl.BlockSpec((tm, n), lambda i, kk: (i, 0))],
        scratch_shapes=[pltpu.VMEM((tm, n), F32)],
        args=(a, b), rider=rider)
    out = out.reshape(N_DEV, m // N_DEV, n)
    return (out, *rode) if rider is not None else out


def part_sum(parts, name):
    _, r, c = parts.shape
    tr = next(t for t in (256, 192, 128, 64, 32, 16, 8) if r % t == 0)

    def body(p_ref, o_ref):
        acc = p_ref[0].astype(F32)
        for j in range(1, N_DEV):
            acc = acc + p_ref[j].astype(F32)
        o_ref[...] = acc

    return pl.pallas_call(
        body, name=name, grid=(r // tr,),
        out_shape=jax.ShapeDtypeStruct((r, c), F32),
        in_specs=[pl.BlockSpec((N_DEV, tr, c), lambda i: (0, i, 0))],
        out_specs=pl.BlockSpec((tr, c), lambda i: (i, 0)),
        compiler_params=_params(),
    )(parts)


def adamw(w, g, m, v, name):
    shape = w.shape
    c = shape[-1]
    r = w.size // c
    w2, g2, m2, v2 = (t.reshape(r, c) for t in (w, g, m, v))
    tr = r if r * c <= 512 * 1024 else next(t for t in (512, 256, 128, 64, 32, 16, 8) if r % t == 0)

    def body(w_ref, g_ref, m_ref, v_ref, d_ref, nm_ref, nv_ref):
        gv = g_ref[...]
        nm = B1 * m_ref[...] + (1.0 - B1) * gv
        nv = B2 * v_ref[...] + (1.0 - B2) * (gv * gv)
        nm_ref[...] = nm
        nv_ref[...] = nv
        m_hat = nm / (1.0 - B1 ** STEP)
        v_hat = nv / (1.0 - B2 ** STEP)
        d_ref[...] = -LR * (m_hat / (jnp.sqrt(v_hat) + ADAM_EPS) + WD * w_ref[...])

    spec = pl.BlockSpec((tr, c), lambda i: (i, 0))
    outs = pl.pallas_call(
        body, name=name, grid=(r // tr,),
        out_shape=[jax.ShapeDtypeStruct((r, c), F32)] * 3,
        in_specs=[spec] * 4, out_specs=[spec] * 3,
        compiler_params=_params(),
    )(w2, g2, m2, v2)
    return tuple(o.reshape(shape) for o in outs)


def _rope_tables(s):
    half = ROT // 2
    inv_freq = THETA ** (-jnp.arange(0, ROT, 2, dtype=F32) / ROT)
    per_row = 128 // half
    pos = (per_row * jnp.arange(s // per_row)[:, None] + jnp.arange(128)[None, :] // half).astype(F32)
    ang = pos * jnp.tile(inv_freq, per_row)[None, :]
    cos, sin = lax.optimization_barrier((jnp.cos(ang), jnp.sin(ang)))
    cos, sin = cos.reshape(s, half), sin.reshape(s, half)
    zeros = jnp.zeros((s, HEAD - ROT), F32)
    c = jnp.concatenate([cos, cos, jnp.ones((s, HEAD - ROT), F32)], axis=1)
    s1 = jnp.concatenate([jnp.zeros((s, half), F32), sin, zeros], axis=1)
    s2 = jnp.concatenate([-sin, jnp.zeros((s, half), F32), zeros], axis=1)
    return c, s1, s2


def _blockdiag(w):
    w4 = w.reshape(N_CG, 4, RB_W, RB_W)
    eye = jnp.eye(4, dtype=w.dtype)
    return jnp.einsum("gipq,ij->gipjq", w4, eye).reshape(N_CG, CG, CG)


def _diag_blocks(w):
    w5 = w.reshape(N_CG, 4, RB_W, 4, RB_W)
    eye = jnp.eye(4, dtype=w.dtype)
    return jnp.einsum("gipjq,ij->gipq", w5, eye).reshape(N_RB, RB_W, RB_W)


def _cols(full, per):
    lead = full.shape[:-1]
    t = full.reshape(lead + (N_DEV, per))
    return jnp.moveaxis(t, -2, 0).reshape(N_DEV, -1)


def kernel(x, c, ada_w, ada_b, ln_g, ln_b, attn_w_in, attn_w_out, attn_sinks, rnn_w_in, rnn_conv_w, rnn_conv_b, rnn_w_a, rnn_b_a, rnn_w_x, rnn_b_x, rnn_lam, rnn_w_out, mlp_w1, mlp_w2, loss_target, m_ada_w, m_ada_b, m_ln_g, m_ln_b, m_attn_w_in, m_attn_w_out, m_attn_sinks, m_rnn_w_in, m_rnn_conv_w, m_rnn_conv_b, m_rnn_w_a, m_rnn_b_a, m_rnn_w_x, m_rnn_b_x, m_rnn_lam, m_rnn_w_out, m_mlp_w1, m_mlp_w2, v_ada_w, v_ada_b, v_ln_g, v_ln_b, v_attn_w_in, v_attn_w_out, v_attn_sinks, v_rnn_w_in, v_rnn_conv_w, v_rnn_conv_b, v_rnn_w_a, v_rnn_b_a, v_rnn_w_x, v_rnn_b_x, v_rnn_lam, v_rnn_w_out, v_mlp_w1, v_mlp_w2):
    s = x.shape[1]
    x0 = x.reshape(s, D_MODEL)
    target = loss_target.reshape(s, D_MODEL)
    weights = dict(ada_w=ada_w, ada_b=ada_b, ln_g=ln_g, ln_b=ln_b, attn_w_in=attn_w_in, attn_w_out=attn_w_out,
                   attn_sinks=attn_sinks, rnn_w_in=rnn_w_in, rnn_conv_w=rnn_conv_w, rnn_conv_b=rnn_conv_b,
                   rnn_w_a=rnn_w_a, rnn_b_a=rnn_b_a, rnn_w_x=rnn_w_x, rnn_b_x=rnn_b_x, rnn_lam=rnn_lam,
                   rnn_w_out=rnn_w_out, mlp_w1=mlp_w1, mlp_w2=mlp_w2)
    moments_m = dict(ada_w=m_ada_w, ada_b=m_ada_b, ln_g=m_ln_g, ln_b=m_ln_b, attn_w_in=m_attn_w_in,
                     attn_w_out=m_attn_w_out, attn_sinks=m_attn_sinks, rnn_w_in=m_rnn_w_in,
                     rnn_conv_w=m_rnn_conv_w, rnn_conv_b=m_rnn_conv_b, rnn_w_a=m_rnn_w_a, rnn_b_a=m_rnn_b_a,
                     rnn_w_x=m_rnn_w_x, rnn_b_x=m_rnn_b_x, rnn_lam=m_rnn_lam, rnn_w_out=m_rnn_w_out,
                     mlp_w1=m_mlp_w1, mlp_w2=m_mlp_w2)
    moments_v = dict(ada_w=v_ada_w, ada_b=v_ada_b, ln_g=v_ln_g, ln_b=v_ln_b, attn_w_in=v_attn_w_in,
                     attn_w_out=v_attn_w_out, attn_sinks=v_attn_sinks, rnn_w_in=v_rnn_w_in,
                     rnn_conv_w=v_rnn_conv_w, rnn_conv_b=v_rnn_conv_b, rnn_w_a=v_rnn_w_a, rnn_b_a=v_rnn_b_a,
                     rnn_w_x=v_rnn_w_x, rnn_b_x=v_rnn_b_x, rnn_lam=v_rnn_lam, rnn_w_out=v_rnn_w_out,
                     mlp_w1=v_mlp_w1, mlp_w2=v_mlp_w2)
    names = list(weights)

    def t16(w):
        return w.T.astype(BF16)

    big = [t16(attn_w_in[0]), attn_w_out[0].astype(BF16), t16(rnn_w_in[0]), rnn_w_out[0].astype(BF16),
           t16(mlp_w1[0]), mlp_w2[0].astype(BF16), t16(mlp_w1[1]), mlp_w2[1].astype(BF16)]
    small_local = jnp.concatenate([
        ln_g.reshape(-1), ln_b.reshape(-1), rnn_conv_w.reshape(-1), rnn_conv_b.reshape(-1),
        rnn_b_a.reshape(-1), rnn_b_x.reshape(-1), rnn_lam.reshape(-1)])
    small_local = jnp.pad(small_local, (0, 4096 - small_local.shape[0])).reshape(32, 128)
    flat = lambda g: g.reshape(N_DEV * g.shape[1], D_MODEL)
    c_all, modr, win_t, wout, sm = ada_modulation(
        jnp.broadcast_to(c, (8, D_MODEL)), ada_w.reshape(4, D_MODEL, CG), ada_b.reshape(4, 1, CG),
        _Gather([big[0], big[1], small_local]))
    win_t, wout = flat(win_t), flat(wout)
    sm = sm.reshape(N_DEV, 4096)

    def full_vec(off, rows, per):
        piece = sm[:, off:off + rows * per].reshape(N_DEV, rows, per)
        return jnp.moveaxis(piece, 0, 1).reshape(rows, N_DEV * per)

    lng_f, lnb_f = full_vec(0, 4, 128), full_vec(512, 4, 128)
    cw_f, cb_f = full_vec(1024, 4, 192), full_vec(1792, 1, 192)
    ba_f, bx_f, lam_f = full_vec(1984, 2, 192), full_vec(2368, 2, 192), full_vec(2752, 2, 192)
    wa_bd = [_blockdiag(rnn_w_a[0, d]).astype(BF16) for d in range(2)]
    wx_bd = [_blockdiag(rnn_w_x[0, d]).astype(BF16) for d in range(2)]

    mods = modr.reshape(N_DEV, 4, 8, CG)[:, :, 0, :]
    mods = jnp.moveaxis(mods, 0, 1).reshape(4, 3, D_MODEL).reshape(12, D_MODEL)
    rope = _rope_tables(s)
    ln = lambda k: (lng_f[k:k + 1], lnb_f[k:k + 1])

    q, kk, v = attn_in_fwd(x0, mods, 0, win_t, rope)
    o, *got = attn_fwd(q, kk, v, attn_sinks, rider=_Gather([big[4], big[5]]))
    w1t_0, w2_0 = (flat(g) for g in got)
    x1, y0, rout = post_fwd(o, wout, x0, mods, 0, *ln(0), rider=_Gather([big[3]]))
    rout = flat(rout)
    x2, y1, ra0, r0, *got = mlp_fwd(x1, mods, 1, w1t_0, w2_0, *ln(1), rider=_Gather([big[2], big[6], big[7]]))
    rin_t, w1t_1, w2_1 = (flat(g) for g in got)
    xr, gt = rnn_in_fwd(x2, mods, 2, rin_t)
    xc = conv_fwd(xr, cw_f, cb_f)
    hf, = lru_fwd(xc, wa_bd[0], wx_bd[0], ba_f[0:1], bx_f[0:1], lam_f[0:1], False)
    hb, hsum = lru_fwd(xc, wa_bd[1], wx_bd[1], ba_f[1:2], bx_f[1:2], lam_f[1:2], True, other=hf)
    x3, y2, ypre = post_fwd(None, rout, x2, mods, 2, *ln(2), gate_act=(gt, hsum))
    y3, ra1, r1 = mlp_fwd(x3, mods, 3, w1t_1, w2_1, *ln(3), last=True)

    dx3, da1, h3, dy3, sums3 = mlp_bwd(target, x3, y3, ra1, mods, 3, w1t_1, w2_1, lng_f[3:4], lnb=lnb_f[3:4])
    g_w1t_1 = wgrad(da1, h3, "wgrad_w1_1")
    g_w2_1 = wgrad(r1, dy3, "wgrad_w2_1")
    dres2, dy2, sums2a, dhs, dgt, p_w1t_1 = post_bwd(dx3, x2, y2, mods, 2, rout, lng_f[2:3], gate_act=(gt, hsum),
                                                     rider=_AllToAll([g_w1t_1]))
    g_rout = wgrad(ypre, dy2, "wgrad_rnn_out")
    dxc_f, dwa_f, dwx_f, dba_f, dbx_f, dlam_f, p_w2_1, p_rout = lru_bwd(
        xc, dhs, hf, wa_bd[0], wx_bd[0], ba_f[0:1], bx_f[0:1], lam_f[0:1], False, rider=_AllToAll([g_w2_1, g_rout]))
    dxc_b, dwa_b, dwx_b, dba_b, dbx_b, dlam_b = lru_bwd(xc, dhs, hb, wa_bd[1], wx_bd[1], ba_f[1:2], bx_f[1:2],
                                                        lam_f[1:2], True)
    dx2, dzz, h2, sums2b, dcw, dcb = rnn_in_bwd(dxc_f, dxc_b, xr, cw_f, dgt, x2, mods, 2, rin_t, dres2)
    g_rin_t = wgrad(dzz, h2, "wgrad_rnn_in")
    d_wa = jnp.stack([_diag_blocks(dwa_f), _diag_blocks(dwa_b)])
    d_wx = jnp.stack([_diag_blocks(dwx_f), _diag_blocks(dwx_b)])
    nflat = d_wa.size // N_DEV
    gates = jnp.concatenate([d_wa.reshape(N_DEV, nflat), d_wx.reshape(N_DEV, nflat)], axis=1)
    gates = gates.reshape(N_DEV, 2 * nflat // 128, 128)
    dx1, da0, h1, dy1, sums1, p_rin_t, p_gates = mlp_bwd(dx2, x1, y1, ra0, mods, 1, w1t_0, w2_0, lng_f[1:2],
                                                         rider=_AllToAll([g_rin_t, gates]))
    gates_sum = part_sum(p_gates, "part_sum_gates")
    g_w1t_0 = wgrad(da0, h1, "wgrad_w1_0")
    g_w2_0, p_w1t_0 = wgrad(r0, dy1, "wgrad_w2_0", rider=_AllToAll([g_w1t_0]))
    dres0, dy0, sums0a, do = post_bwd(dx1, x0, y0, mods, 0, wout, lng_f[0:1])
    g_wout = wgrad(o, dy0, "wgrad_attn_out")
    dq, dk, dv, dsink, wag, p_w2_0, p_wout = attn_bwd(
        q, kk, v, do, attn_sinks, rider=_Multi(_Gather([gates_sum]), _AllToAll([g_w2_0, g_wout])))
    dx0, dqkv, h0, sums0b = attn_in_bwd(dq, dk, dv, rope, x0, mods, 0, win_t, dres0)
    g_win_t = wgrad(dqkv, h0, "wgrad_attn_in")

    sums = [sums0a + sums0b, sums1, sums2a + sums2b, sums3]
    gmod = jnp.stack([t[0:3] for t in sums])
    gsend = jnp.moveaxis(gmod.reshape(4, N_DEV, CG), 1, 0)
    gsend = jnp.pad(gsend, ((0, 0), (0, 4), (0, 0)))
    c_t = c_all[:, 0, :].T
    sq_err = jnp.sum(sums3[5]).reshape(1, 1)
    tail = jnp.concatenate([
        _cols(dcw, 192), _cols(dcb, 192),
        _cols(jnp.concatenate([dba_f, dba_b]), 192), _cols(jnp.concatenate([dbx_f, dbx_b]), 192),
        _cols(jnp.concatenate([dlam_f, dlam_b]), 192),
        _cols(jnp.stack([t[3] for t in sums]), 128), _cols(jnp.stack([t[4] for t in sums]), 128),
        jnp.broadcast_to(dsink[:, 0:8], (N_DEV, 8)), jnp.broadcast_to(sq_err, (N_DEV, 1))], axis=1)
    tail = jnp.pad(tail, ((0, 0), (0, 32 * 128 - tail.shape[1]))).reshape(N_DEV, 32, 128)
    g_ada_w, g_ada_b, red, p_win_t = epilogue(gsend, c_t, tail, _AllToAll([g_win_t]))
    grads = {"ada_w": g_ada_w.reshape(ada_w.shape), "ada_b": g_ada_b[0:4].reshape(ada_b.shape)}

    big_parts = [p_win_t, p_wout, p_rin_t, p_rout, p_w1t_0, p_w2_0, p_w1t_1, p_w2_1]
    gsum = [part_sum(p, "part_sum_%d" % i) for i, p in enumerate(big_parts)]
    grads.update({
        "attn_w_in": gsum[0].T[None], "attn_w_out": gsum[1][None],
        "rnn_w_in": gsum[2].T[None], "rnn_w_out": gsum[3][None],
        "mlp_w1": jnp.stack([gsum[4].T, gsum[6].T]), "mlp_w2": jnp.stack([gsum[5], gsum[7]]),
    })
    wag = wag.reshape(N_DEV, 2 * nflat)
    grads["rnn_w_a"] = wag[:, :nflat].reshape(rnn_w_a.shape)
    grads["rnn_w_x"] = wag[:, nflat:].reshape(rnn_w_x.shape)
    tl = red.reshape(-1)
    loss = 0.5 * tl[3144] / D_MODEL
    grads["rnn_conv_w"] = tl[0:768].reshape(rnn_conv_w.shape)
    grads["rnn_conv_b"] = tl[768:960].reshape(rnn_conv_b.shape)
    grads["rnn_b_a"] = tl[960:1344].reshape(rnn_b_a.shape)
    grads["rnn_b_x"] = tl[1344:1728].reshape(rnn_b_x.shape)
    grads["rnn_lam"] = tl[1728:2112].reshape(rnn_lam.shape)
    grads["ln_g"] = tl[2112:2624].reshape(ln_g.shape)
    grads["ln_b"] = tl[2624:3136].reshape(ln_b.shape)
    grads["attn_sinks"] = tl[3136:3144].reshape(attn_sinks.shape)

    delta, new_m, new_v = {}, {}, {}
    for n in names:
        delta[n], new_m[n], new_v[n] = adamw(weights[n], grads[n], moments_m[n], moments_v[n], "adamw_" + n)
    return (loss, dx0.reshape(x.shape), *[grads[n] for n in names], *[delta[n] for n in names],
            *[new_m[n] for n in names], *[new_v[n] for n in names])
```

```python
import functools
import math

import jax
import jax.numpy as jnp
from jax import lax
from jax.experimental import pallas as pl
from jax.experimental.pallas import tpu as pltpu

F32, BF16 = jnp.float32, jnp.bfloat16
MESH = pl.DeviceIdType.MESH

D_MODEL = 1024
N_Q, N_KV, HEAD = 8, 2, 128
ROT, THETA = 32, 500000.0
QBLK = 128
D_QKV = (N_Q + 2 * N_KV) * HEAD
D_RNN, N_RB, RB_W = 1536, 16, 96
CG = 384
N_CG = D_RNN // CG
D_FF = 4096
FF_CHUNK = 1024
DEPTH = 2
ALPHA = (2.0 * DEPTH) ** 0.25
LN_EPS = 1e-5
LRU_C = 8.0
N_DEV = 8
LR, B1, B2, ADAM_EPS, WD, STEP = 0.001, 0.9, 0.999, 1e-8, 0.01, 10

VMEM_LIMIT = 56 * 1024 * 1024
TM_MM = 512
TM_MLP = 256
TM_MLP_FWD = 512
TT_RNN = 2048
TK_WG = 2048


def _nn(a, b):
    return jnp.dot(a, b, preferred_element_type=F32)


def _nt(a, b):
    return lax.dot_general(a, b, (((1,), (1,)), ((), ())), preferred_element_type=F32)


def _tn(a, b):
    return lax.dot_general(a, b, (((0,), (0,)), ((), ())), preferred_element_type=F32)


def _blk(n, pref):
    t = min(n, pref)
    assert n % t == 0, (n, pref)
    return t


def _params(**kw):
    return pltpu.CompilerParams(vmem_limit_bytes=VMEM_LIMIT, **kw)


def _row(tm, w):
    return pl.BlockSpec((tm, w), lambda i: (i, 0))


def _res(shape):
    return pl.BlockSpec(shape, lambda i: (0,) * len(shape), pipeline_mode=pl.Buffered(1))


def _mod(mod_ref, k):
    return mod_ref[3 * k:3 * k + 1, :], mod_ref[3 * k + 1:3 * k + 2, :], mod_ref[3 * k + 2:3 * k + 3, :]


def _ln_stats(z):
    mu = jnp.mean(z, axis=-1, keepdims=True)
    zc = z - mu
    var = jnp.mean(zc * zc, axis=-1, keepdims=True)
    rstd = lax.rsqrt(var + LN_EPS)
    return zc * rstd, rstd


def _ln_bwd(dxo, xhat, rstd, g):
    dxh = dxo * g
    m1 = jnp.mean(dxh, axis=-1, keepdims=True)
    m2 = jnp.mean(dxh * xhat, axis=-1, keepdims=True)
    return rstd * (dxh - m1 - xhat * m2)


def _colsum(v):
    return jnp.sum(v, axis=0, keepdims=True)


def _sigmoid(v):
    return 0.5 * jnp.tanh(0.5 * v) + 0.5


def _gelu_parts(v):
    k = math.sqrt(2.0 / math.pi)
    u = k * (v + 0.044715 * v * v * v)
    t = jnp.tanh(u)
    g = 0.5 * v * (1.0 + t)
    dg = 0.5 * (1.0 + t) + 0.5 * v * (1.0 - t * t) * k * (1.0 + 3.0 * 0.044715 * v * v)
    return g, dg


def _me():
    return lax.axis_index("x"), lax.axis_index("y"), lax.axis_index("c")


def _idx(p):
    return 4 * p[0] + 2 * p[1] + p[2]


def _peers(me):
    x, y, c = me
    out = []
    for k in range(1, N_DEV):
        out.append((1 - x if k & 4 else x, 1 - y if k & 2 else y, 1 - c if k & 1 else c))
    return out


class _Gather:
    def __init__(self, srcs):
        self.srcs = list(srcs)
        n = len(self.srcs)
        self.out_shape = [jax.ShapeDtypeStruct((N_DEV,) + s.shape, s.dtype) for s in self.srcs]
        self.scratch = [pltpu.SemaphoreType.DMA((n, 7)), pltpu.SemaphoreType.DMA((n, 7)),
                        pltpu.SemaphoreType.DMA((n,))]

    @staticmethod
    def _places():
        x, y, c = me = _me()
        return me, (x, y, 1 - c), [(1 - x, y), (x, 1 - y), (1 - x, 1 - y)]

    @staticmethod
    def _copy(outs, sems, t, k, block, to, src=None):
        slot = outs[t].at[_idx(block)]
        return pltpu.make_async_remote_copy(
            src_ref=slot if src is None else src, dst_ref=slot, send_sem=sems[0].at[t, k],
            recv_sem=sems[1].at[t, k], device_id=to, device_id_type=MESH)

    def _firsts(self, ins, outs, sems):
        me, sibling, chips = self._places()
        out = []
        for t in range(len(ins)):
            out.append(self._copy(outs, sems, t, 0, me, sibling, src=ins[t]))
            out += [self._copy(outs, sems, t, 1 + j, me, (*chip, me[2]), src=ins[t]) for j, chip in enumerate(chips)]
        return out

    def _locals(self, ins, outs, sems):
        me = _me()
        return [pltpu.make_async_copy(ins[t], outs[t].at[_idx(me)], sems[2].at[t]) for t in range(len(ins))]

    def start(self, ins, outs, sems):
        for cp in self._locals(ins, outs, sems) + self._firsts(ins, outs, sems):
            cp.start()

    def mid(self, ins, outs, sems):
        me, sibling, chips = self._places()
        for j, chip in enumerate(chips):
            for t in range(len(ins)):
                self._copy(outs, sems, t, 1 + j, (*chip, me[2]), me).wait_recv()
                self._copy(outs, sems, t, 4 + j, (*chip, me[2]), sibling).start()

    def finish(self, ins, outs, sems):
        me, sibling, chips = self._places()
        for t in range(len(ins)):
            self._copy(outs, sems, t, 0, sibling, me).wait_recv()
            for j, chip in enumerate(chips):
                self._copy(outs, sems, t, 4 + j, (*chip, 1 - me[2]), me).wait_recv()
        for cp in self._firsts(ins, outs, sems):
            cp.wait_send()
        for j, chip in enumerate(chips):
            for t in range(len(ins)):
                self._copy(outs, sems, t, 4 + j, (*chip, me[2]), sibling).wait_send()
        for cp in self._locals(ins, outs, sems):
            cp.wait()


class _AllToAll:
    def __init__(self, srcs):
        self.srcs = list(srcs)
        n = len(self.srcs)
        self.out_shape = [jax.ShapeDtypeStruct(s.shape, s.dtype) for s in self.srcs]
        self.scratch = [pltpu.SemaphoreType.DMA((n, 7)), pltpu.SemaphoreType.DMA((n, 7)),
                        pltpu.SemaphoreType.DMA((n,))]

    def _copies(self, ins, outs, sems):
        me = _me()
        loc, rem = [], []
        for t in range(len(ins)):
            loc.append(pltpu.make_async_copy(ins[t].at[_idx(me)], outs[t].at[_idx(me)], sems[2].at[t]))
            for k, p in enumerate(_peers(me)):
                rem.append(pltpu.make_async_remote_copy(
                    src_ref=ins[t].at[_idx(p)], dst_ref=outs[t].at[_idx(me)], send_sem=sems[0].at[t, k],
                    recv_sem=sems[1].at[t, k], device_id=p, device_id_type=MESH))
        return loc, rem

    def start(self, ins, outs, sems):
        loc, rem = self._copies(ins, outs, sems)
        for cp in loc + rem:
            cp.start()

    def mid(self, ins, outs, sems):
        pass

    def finish(self, ins, outs, sems):
        me = _me()
        for t in range(len(ins)):
            for k, p in enumerate(_peers(me)):
                slot = outs[t].at[_idx(p)]
                pltpu.make_async_remote_copy(
                    src_ref=slot, dst_ref=slot, send_sem=sems[0].at[t, k], recv_sem=sems[1].at[t, k],
                    device_id=p, device_id_type=MESH).wait_recv()
        loc, rem = self._copies(ins, outs, sems)
        for cp in rem:
            cp.wait_send()
        for cp in loc:
            cp.wait()


class _Multi:
    def __init__(self, *exs):
        self.exs = exs
        self.srcs = [s for e in exs for s in e.srcs]
        self.out_shape = [s for e in exs for s in e.out_shape]
        self.scratch = [s for e in exs for s in e.scratch]

    def _each(self, ins, outs, sems):
        i = j = 0
        for e in self.exs:
            n, m = len(e.srcs), len(e.scratch)
            yield e, ins[i:i + n], outs[i:i + n], sems[j:j + m]
            i, j = i + n, j + m

    def start(self, ins, outs, sems):
        for e, a, b, c in self._each(ins, outs, sems):
            e.start(a, b, c)

    def mid(self, ins, outs, sems):
        for e, a, b, c in self._each(ins, outs, sems):
            e.mid(a, b, c)

    def finish(self, ins, outs, sems):
        for e, a, b, c in self._each(ins, outs, sems):
            e.finish(a, b, c)


def _call(body, *, name, grid, in_specs, out_specs, out_shape, args, scratch_shapes=(), rider=None):
    in_specs, out_specs, out_shape = list(in_specs), list(out_specs), list(out_shape)
    scratch_shapes = list(scratch_shapes)
    if rider is None:
        return pl.pallas_call(body, name=name, grid=grid, out_shape=out_shape, in_specs=in_specs,
                              out_specs=out_specs, scratch_shapes=scratch_shapes, compiler_params=_params())(*args)
    nci, nco, ncs, nr = len(in_specs), len(out_shape), len(scratch_shapes), len(rider.srcs)
    nsteps = math.prod(grid)
    assert nsteps >= 2, (name, grid)
    mid = max(1, (7 * nsteps) // 8)

    def full(*refs):
        ci, ri = refs[:nci], refs[nci:nci + nr]
        co, ro = refs[nci + nr:nci + nr + nco], refs[nci + nr + nco:nci + 2 * nr + nco]
        cs, rs = refs[nci + 2 * nr + nco:nci + 2 * nr + nco + ncs], refs[nci + 2 * nr + nco + ncs:]
        step = pl.program_id(0)
        for d in range(1, len(grid)):
            step = step * grid[d] + pl.program_id(d)

        @pl.when(step == 0)
        def _():
            rider.start(ri, ro, rs)

        @pl.when(step == mid)
        def _():
            rider.mid(ri, ro, rs)

        body(*ci, *co, *cs)

        @pl.when(step == nsteps - 1)
        def _():
            rider.finish(ri, ro, rs)

    any_spec = pl.BlockSpec(memory_space=pl.ANY)
    return pl.pallas_call(
        full, name=name, grid=grid, out_shape=out_shape + rider.out_shape,
        in_specs=in_specs + [any_spec] * nr, out_specs=out_specs + [any_spec] * nr,
        scratch_shapes=scratch_shapes + rider.scratch, compiler_params=_params(),
    )(*args, *rider.srcs)


def _a2a_start(srcs, dsts, send_sems, recv_sems, local_sems, me, sem_base=0):
    peers = _peers(me)
    started = []
    for t in range(len(srcs)):
        loc = pltpu.make_async_copy(srcs[t].at[_idx(me)], dsts[t].at[_idx(me)], local_sems.at[sem_base + t])
        loc.start()
        started.append(("local", loc))
        for k, p in enumerate(peers):
            cp = pltpu.make_async_remote_copy(
                src_ref=srcs[t].at[_idx(p)], dst_ref=dsts[t].at[_idx(me)],
                send_sem=send_sems.at[sem_base + t, k], recv_sem=recv_sems.at[sem_base + t, k],
                device_id=p, device_id_type=MESH)
            cp.start()
            started.append(("remote", cp))
    return started


def _a2a_finish(started, dsts, send_sems, recv_sems, me, sem_base=0):
    peers = _peers(me)
    for t in range(len(dsts)):
        for k, p in enumerate(peers):
            slot = dsts[t].at[_idx(p)]
            pltpu.make_async_remote_copy(
                src_ref=slot, dst_ref=slot, send_sem=send_sems.at[sem_base + t, k],
                recv_sem=recv_sems.at[sem_base + t, k], device_id=p, device_id_type=MESH).wait_recv()
    for kind, cp in started:
        if kind == "local":
            cp.wait()
        else:
            cp.wait_send()


def ada_modulation(c8, ada_w, ada_b, ride):
    nr = len(ride.srcs)

    def body(c_ref, w_ref, b_ref, *rest):
        ride_in, (call_ref, modr_ref), ride_out = rest[:nr], rest[nr:nr + 2], rest[nr + 2:2 * nr + 2]
        modp, send_sems, recv_sems, local_sems = rest[2 * nr + 2:2 * nr + 6]
        ride_sems = rest[2 * nr + 6:]
        ride.start(ride_in, ride_out, ride_sems)
        me = _me()
        peers = _peers(me)
        sends = []
        for k, p in enumerate(peers):
            cp = pltpu.make_async_remote_copy(
                src_ref=c_ref, dst_ref=call_ref.at[_idx(me)], send_sem=send_sems.at[0, k],
                recv_sem=recv_sems.at[0, k], device_id=p, device_id_type=MESH)
            cp.start()
            sends.append(cp)
        call_ref[_idx(me)] = c_ref[...]
        for k, p in enumerate(peers):
            slot = call_ref.at[_idx(p)]
            pltpu.make_async_remote_copy(
                src_ref=slot, dst_ref=slot, send_sem=send_sems.at[0, k], recv_sem=recv_sems.at[0, k],
                device_id=p, device_id_type=MESH).wait_recv()
        for cp in sends:
            cp.wait_send()
        cv = call_ref[...].reshape(N_DEV * 8, D_MODEL)
        s = (cv * _sigmoid(cv)).astype(BF16)
        for k in range(4):
            res = _nn(s, w_ref[k].astype(BF16)) + b_ref[k]
            for j in range(N_DEV):
                modp[j, 8 * k:8 * k + 8, :] = res[8 * j:8 * j + 8, :]
        started = _a2a_start([modp], [modr_ref], send_sems, recv_sems, local_sems, me, sem_base=1)
        _a2a_finish(started, [modr_ref], send_sems, recv_sems, me, sem_base=1)
        ride.mid(ride_in, ride_out, ride_sems)
        ride.finish(ride_in, ride_out, ride_sems)

    vm, hbm = pl.BlockSpec(memory_space=pltpu.VMEM), pl.BlockSpec(memory_space=pl.ANY)
    return pl.pallas_call(
        body, name="ada_modulation",
        out_shape=[jax.ShapeDtypeStruct((N_DEV, 8, D_MODEL), F32), jax.ShapeDtypeStruct((N_DEV, 32, CG), F32)]
        + ride.out_shape,
        in_specs=[vm, vm, vm] + [hbm] * nr, out_specs=[vm, vm] + [hbm] * nr,
        scratch_shapes=[pltpu.VMEM((N_DEV, 32, CG), F32), pltpu.SemaphoreType.DMA((2, 7)),
                        pltpu.SemaphoreType.DMA((2, 7)), pltpu.SemaphoreType.DMA((2,))] + ride.scratch,
        compiler_params=_params(),
    )(c8, ada_w, ada_b, *ride.srcs)


def epilogue(gsend, c_t, tail, ride):
    nr = len(ride.srcs)
    rt = tail.shape[1]

    def body(g_ref, ct_ref, t_ref, *rest):
        ride_in, (gw_ref, gb_ref, red_ref), ride_out = rest[:nr], rest[nr:nr + 3], rest[nr + 3:2 * nr + 3]
        grecv, trecv, send_sems, recv_sems, local_sems = rest[2 * nr + 3:2 * nr + 8]
        ride_sems = rest[2 * nr + 8:]
        ride.start(ride_in, ride_out, ride_sems)
        me = _me()
        started = _a2a_start([g_ref, t_ref], [grecv, trecv], send_sems, recv_sems, local_sems, me)
        _a2a_finish(started, [grecv, trecv], send_sems, recv_sems, me)
        acc = trecv[0]
        for j in range(1, N_DEV):
            acc = acc + trecv[j]
        red_ref[...] = acc
        ct = ct_ref[...]
        st = (ct * _sigmoid(ct)).astype(BF16).astype(F32)
        gb = jnp.zeros((8, CG), F32)
        for b in range(N_DEV):
            gb = gb + grecv[b]
        gb_ref[...] = gb
        for k in range(4):
            acc = jnp.zeros((D_MODEL, CG), F32)
            for b in range(N_DEV):
                row = grecv[b, k:k + 1, :].astype(BF16).astype(F32)
                acc = acc + st[:, b:b + 1] * row
            gw_ref[k] = acc
        ride.mid(ride_in, ride_out, ride_sems)
        ride.finish(ride_in, ride_out, ride_sems)

    vm, hbm = pl.BlockSpec(memory_space=pltpu.VMEM), pl.BlockSpec(memory_space=pl.ANY)
    return pl.pallas_call(
        body, name="epilogue",
        out_shape=[jax.ShapeDtypeStruct((4, D_MODEL, CG), F32), jax.ShapeDtypeStruct((8, CG), F32),
                   jax.ShapeDtypeStruct((rt, 128), F32)] + ride.out_shape,
        in_specs=[vm, vm, vm] + [hbm] * nr, out_specs=[vm, vm, vm] + [hbm] * nr,
        scratch_shapes=[pltpu.VMEM((N_DEV, 8, CG), F32), pltpu.VMEM((N_DEV, rt, 128), F32),
                        pltpu.SemaphoreType.DMA((2, 7)), pltpu.SemaphoreType.DMA((2, 7)),
                        pltpu.SemaphoreType.DMA((2,))] + ride.scratch,
        compiler_params=_params(),
    )(gsend, c_t, tail, *ride.srcs)


def _rope(t, cos, s1, s2):
    return t * cos + pltpu.roll(t, 16, 1) * s1 + pltpu.roll(t, HEAD - 16, 1) * s2


def _rope_bwd(d, cos, s1, s2):
    return d * cos + pltpu.roll(d * s1, HEAD - 16, 1) + pltpu.roll(d * s2, 16, 1)


def attn_in_fwd(x, mods, k, win_t, rope, rider=None):
    s = x.shape[0]
    tm = _blk(s, TM_MM)

    def body(x_ref, mod_ref, w_ref, c_ref, s1_ref, s2_ref, q_ref, k_ref, v_ref):
        shift, scale, _ = _mod(mod_ref, k)
        h = (x_ref[...] * (1.0 + scale) + shift).astype(BF16)
        qkv = _nt(h, w_ref[...])
        cos, s1, s2 = c_ref[...], s1_ref[...], s2_ref[...]
        for hh in range(N_Q + N_KV):
            r = _rope(qkv[:, HEAD * hh:HEAD * (hh + 1)], cos, s1, s2).astype(BF16)
            if hh < N_Q:
                q_ref[:, HEAD * hh:HEAD * (hh + 1)] = r
            else:
                k_ref[:, HEAD * (hh - N_Q):HEAD * (hh - N_Q + 1)] = r
        v_ref[...] = qkv[:, HEAD * (N_Q + N_KV):].astype(BF16)

    return _call(
        body, name="attn_in_fwd", grid=(s // tm,),
        out_shape=[jax.ShapeDtypeStruct((s, N_Q * HEAD), BF16), jax.ShapeDtypeStruct((s, N_KV * HEAD), BF16),
                   jax.ShapeDtypeStruct((s, N_KV * HEAD), BF16)],
        in_specs=[_row(tm, D_MODEL), _res(mods.shape), _res(win_t.shape),
                  _row(tm, HEAD), _row(tm, HEAD), _row(tm, HEAD)],
        out_specs=[_row(tm, N_Q * HEAD), _row(tm, N_KV * HEAD), _row(tm, N_KV * HEAD)],
        args=(x, mods, win_t, *rope), rider=rider)


QPAIR = 4


def _kv_specs(nblk):
    w = N_KV * HEAD
    return [pl.BlockSpec((QBLK, w), lambda n: (jnp.maximum(QPAIR * n - 1, 0), 0)),
            pl.BlockSpec((QPAIR * QBLK, w), lambda n: (n, 0)),
            pl.BlockSpec((QBLK, w), lambda n: (jnp.minimum(QPAIR * (n + 1), nblk - 1), 0))]


GROUP = N_Q // N_KV


def _attn_mask(n, s):
    qi = lax.broadcasted_iota(jnp.int32, (GROUP * QBLK, 3 * QBLK), 0) & (QBLK - 1)
    kj = lax.broadcasted_iota(jnp.int32, (GROUP * QBLK, 3 * QBLK), 1)
    rel = kj - QBLK - qi
    kpos = kj + (n - 1) * QBLK
    return (jnp.abs(rel) <= QBLK) & (kpos >= 0) & (kpos < s)


def _stack_heads(ref, qb, kv):
    rows = slice(QBLK * qb, QBLK * (qb + 1))
    return jnp.concatenate([ref[rows, HEAD * (GROUP * kv + j):HEAD * (GROUP * kv + j + 1)] for j in range(GROUP)],
                           axis=0)


def _stack_sinks(sink_ref, kv):
    row = lax.broadcasted_iota(jnp.int32, (GROUP * QBLK, 1), 0)
    out = jnp.full((GROUP * QBLK, 1), sink_ref[0, GROUP * kv + GROUP - 1], F32)
    for j in range(GROUP - 2, -1, -1):
        out = jnp.where(row < QBLK * (j + 1), sink_ref[0, GROUP * kv + j], out)
    return out


def _attn_probs(qh, kh, valid, sink):
    sc = _nt(qh, kh) * (HEAD ** -0.5)
    sc = jnp.where(valid, sc, -1e30)
    m = jnp.maximum(jnp.max(sc, axis=-1, keepdims=True), sink)
    p = jnp.exp(sc - m)
    es = jnp.exp(sink - m)
    denom = jnp.sum(p, axis=-1, keepdims=True) + es
    return p / denom, es / denom


def attn_fwd(q, kk, v, sinks, rider=None):
    s = q.shape[0]
    nblk = s // QBLK

    def body(sink_ref, q_ref, kp, ko, kn, vp, vo, vn, o_ref):
        n = pl.program_id(0)
        kall = jnp.concatenate([kp[...], ko[...], kn[...]], axis=0)
        vall = jnp.concatenate([vp[...], vo[...], vn[...]], axis=0)
        for qb in range(QPAIR):
            valid = _attn_mask(QPAIR * n + qb, s)
            keys = slice(QBLK * qb, QBLK * (qb + 3))
            for kv in range(N_KV):
                cols = slice(HEAD * kv, HEAD * (kv + 1))
                probs, _ = _attn_probs(_stack_heads(q_ref, qb, kv), kall[keys, cols], valid,
                                       _stack_sinks(sink_ref, kv))
                og = _nn(probs.astype(BF16), vall[keys, cols]).astype(BF16)
                for j in range(GROUP):
                    hq = GROUP * kv + j
                    o_ref[QBLK * qb:QBLK * (qb + 1), HEAD * hq:HEAD * (hq + 1)] = og[QBLK * j:QBLK * (j + 1), :]

    qspec = pl.BlockSpec((QPAIR * QBLK, N_Q * HEAD), lambda n: (n, 0))
    return _call(
        body, name="attn_fwd", grid=(nblk // QPAIR,),
        out_shape=[jax.ShapeDtypeStruct((s, N_Q * HEAD), BF16)],
        in_specs=[pl.BlockSpec(memory_space=pltpu.SMEM), qspec] + _kv_specs(nblk) + _kv_specs(nblk),
        out_specs=[qspec],
        args=(sinks, q, kk, kk, kk, v, v, v), rider=rider)


def post_fwd(ypre, w, x, mods, k, lng, lnb, gate_act=None, rider=None):
    s = x.shape[0]
    tm = _blk(s, TM_MM)
    kdim = w.shape[0]
    rnn = gate_act is not None

    def body(*refs):
        if rnn:
            gt_ref, hs_ref, w_ref, x_ref, mod_ref, g_ref, b_ref, xo_ref, y_ref, yp_ref = refs
            act, _ = _gelu_parts(gt_ref[...].astype(F32))
            yp = (hs_ref[...].astype(F32) * act).astype(BF16)
            yp_ref[...] = yp
        else:
            yp_ref, w_ref, x_ref, mod_ref, g_ref, b_ref, xo_ref, y_ref = refs
            yp = yp_ref[...]
        _, _, gate = _mod(mod_ref, k)
        y = _nn(yp, w_ref[...])
        y_ref[...] = y
        xhat, _ = _ln_stats(ALPHA * x_ref[...] + (1.0 + gate) * y)
        xo_ref[...] = xhat * g_ref[...] + b_ref[...]

    act_in = list(gate_act) if rnn else [ypre]
    out_shape = [jax.ShapeDtypeStruct((s, D_MODEL), F32), jax.ShapeDtypeStruct((s, D_MODEL), F32)]
    out_specs = [_row(tm, D_MODEL), _row(tm, D_MODEL)]
    if rnn:
        out_shape.append(jax.ShapeDtypeStruct((s, kdim), BF16))
        out_specs.append(_row(tm, kdim))
    return _call(
        body, name="rnn_post_fwd" if rnn else "attn_post_fwd", grid=(s // tm,),
        out_shape=out_shape,
        in_specs=[_row(tm, kdim)] * len(act_in) + [_res(w.shape), _row(tm, D_MODEL), _res(mods.shape),
                                                    _res(lng.shape), _res(lnb.shape)],
        out_specs=out_specs,
        args=(*act_in, w, x, mods, lng, lnb), rider=rider)


def mlp_fwd(x, mods, k, w1_t, w2, lng, lnb, rider=None, last=False):
    s = x.shape[0]
    tm = _blk(s, TM_MLP_FWD)

    def body(x_ref, mod_ref, w1_ref, w2_ref, g_ref, b_ref, *outs):
        xo_ref = None if last else outs[0]
        y_ref, ra_ref, r_ref = outs[-3:]
        xv = x_ref[...]
        shift, scale, gate = _mod(mod_ref, k)
        h = (xv * (1.0 + scale) + shift).astype(BF16)
        y = jnp.zeros((tm, D_MODEL), F32)
        for c in range(D_FF // FF_CHUNK):
            rows = slice(FF_CHUNK * c, FF_CHUNK * (c + 1))
            a = jnp.maximum(_nt(h, w1_ref[rows, :]), 0.0)
            r = (a * a).astype(BF16)
            ra_ref[:, rows] = a.astype(BF16)
            r_ref[:, rows] = r
            y = y + _nn(r, w2_ref[rows, :])
        y_ref[...] = y
        if not last:
            xhat, _ = _ln_stats(ALPHA * xv + (1.0 + gate) * y)
            xo_ref[...] = xhat * g_ref[...] + b_ref[...]

    nf = 1 if last else 2
    return _call(
        body, name="mlp_fwd_last" if last else "mlp_fwd", grid=(s // tm,),
        out_shape=[jax.ShapeDtypeStruct((s, D_MODEL), F32)] * nf + [jax.ShapeDtypeStruct((s, D_FF), BF16)] * 2,
        in_specs=[_row(tm, D_MODEL), _res(mods.shape), _res(w1_t.shape), _res(w2.shape),
                  _res(lng.shape), _res(lnb.shape)],
        out_specs=[_row(tm, D_MODEL)] * nf + [_row(tm, D_FF)] * 2,
        args=(x, mods, w1_t, w2, lng, lnb), rider=rider)


def rnn_in_fwd(x, mods, k, win_t):
    s = x.shape[0]
    tm = _blk(s, TM_MM)

    def body(x_ref, mod_ref, w_ref, xr_ref, gt_ref):
        shift, scale, _ = _mod(mod_ref, k)
        h = (x_ref[...] * (1.0 + scale) + shift).astype(BF16)
        xr_ref[...] = _nt(h, w_ref[0:D_RNN, :])
        gt_ref[...] = _nt(h, w_ref[D_RNN:2 * D_RNN, :]).astype(BF16)

    return pl.pallas_call(
        body, name="rnn_in_fwd", grid=(s // tm,),
        out_shape=[jax.ShapeDtypeStruct((s, D_RNN), F32), jax.ShapeDtypeStruct((s, D_RNN), BF16)],
        in_specs=[_row(tm, D_MODEL), _res(mods.shape), _res(win_t.shape)],
        out_specs=[_row(tm, D_RNN)] * 2,
        compiler_params=_params(),
    )(x, mods, win_t)


def _shift_rows(v, k, row):
    n = v.shape[0]
    r = pltpu.roll(v, k % n, 0)
    keep = (row >= k) if k > 0 else (row < n + k)
    return jnp.where(keep, r, 0.0)


def conv_fwd(xr, cw, cb):
    s = xr.shape[0]

    def body(x_ref, w_ref, b_ref, o_ref):
        xv = x_ref[...]
        row = lax.broadcasted_iota(jnp.int32, xv.shape, 0)
        o_ref[...] = (b_ref[...] + w_ref[0:1, :] * _shift_rows(xv, 2, row) + w_ref[1:2, :] * _shift_rows(xv, 1, row)
                      + w_ref[2:3, :] * xv + w_ref[3:4, :] * _shift_rows(xv, -1, row))

    slab = pl.BlockSpec((s, 128), lambda j: (0, j))
    return pl.pallas_call(
        body, name="conv_fwd", grid=(D_RNN // 128,),
        out_shape=jax.ShapeDtypeStruct((D_RNN // 128, s, 128), F32),
        in_specs=[slab, pl.BlockSpec((4, 128), lambda j: (0, j)), pl.BlockSpec((1, 128), lambda j: (0, j))],
        out_specs=pl.BlockSpec((None, s, 128), lambda j: (j, 0, 0)),
        compiler_params=_params(),
    )(xr, cw, cb)


def _softplus_neg(lam):
    z = -lam
    e = jnp.exp(-jnp.abs(z))
    u = 1.0 + e
    log1p = jnp.where(u == 1.0, e, jnp.log(u) * e / jnp.where(u == 1.0, 1.0, u - 1.0))
    return jnp.maximum(z, 0.0) + log1p, 1.0 / (1.0 + jnp.exp(lam))


def _lru_gates(xv, wa_ref, wx_ref, ba_ref, bx_ref, lam_ref):
    xb = xv.astype(BF16)
    r = _sigmoid(_nn(xb, wa_ref[...]) + ba_ref[...])
    i = _sigmoid(_nn(xb, wx_ref[...]) + bx_ref[...])
    sp, sg = _softplus_neg(lam_ref[...])
    la = r * (-LRU_C * sp)
    a = jnp.exp(la)
    th = jnp.tanh(la)
    m2 = -2.0 * th / (1.0 - th)
    rmult = lax.rsqrt(jnp.maximum(m2, 1e-37))
    return xb, r, i, sp, sg, a, m2 * rmult, rmult


SLABS = CG // 128
GRP, SEG = 32, 4


def _lru_specs(nt, tt, reverse):
    tmap = (lambda t: nt - 1 - t) if reverse else (lambda t: t)
    blk = pl.BlockSpec((tt, CG), lambda g, t: (tmap(t), g))
    slabs = pl.BlockSpec((SLABS, tt, 128), lambda g, t: (g, tmap(t), 0))
    wsp = pl.BlockSpec((None, CG, CG), lambda g, t: (g, 0, 0))
    vec = pl.BlockSpec((1, CG), lambda g, t: (0, g))
    return tmap, blk, slabs, wsp, vec


def _slab_rows(ref3):
    return jnp.concatenate([ref3[l] for l in range(ref3.shape[0])], axis=1)


def _perm_load(ref3, tt):
    out = []
    for l in range(SLABS):
        r = ref3.at[l]
        out.append(jnp.concatenate([r[pl.ds(GRP * g + i, 8, stride=SEG), :]
                                    for g in range(tt // GRP) for i in range(SEG)], axis=0))
    return jnp.concatenate(out, axis=1)


def _perm_scan(a, u, carry, reverse, emit):
    n, c = a.shape
    sub = lax.broadcasted_iota(jnp.int32, (8, c), 0)
    steps = [(8 - sh, sub < 8 - sh) if reverse else (sh, sub >= sh) for sh in (1, 2, 4)]
    order = range(SEG - 1, -1, -1) if reverse else range(SEG)
    for g in (range(n // GRP - 1, -1, -1) if reverse else range(n // GRP)):
        hs, ps = [None] * SEG, [None] * SEG
        h = p = None
        for i in order:
            rows = slice(GRP * g + 8 * i, GRP * g + 8 * i + 8)
            h = u[rows] if h is None else a[rows] * h + u[rows]
            p = a[rows] if p is None else a[rows] * p
            hs[i], ps[i] = h, p
        d, f = p, h
        for rot, keep in steps:
            d_s = jnp.where(keep, pltpu.roll(d, rot, 0), 1.0)
            f_s = jnp.where(keep, pltpu.roll(f, rot, 0), 0.0)
            f = d * f_s + f
            d = d * d_s
        end = f + d * carry
        if reverse:
            init = jnp.where(sub == 7, carry, pltpu.roll(end, 7, 0))
            carry = jnp.broadcast_to(end[0:1], (8, c))
        else:
            init = jnp.where(sub == 0, carry, pltpu.roll(end, 1, 0))
            carry = jnp.broadcast_to(end[7:8], (8, c))
        for i in range(SEG):
            emit(g, i, hs[i] + ps[i] * init)
    return carry


def lru_fwd(xc, wa, wx, ba, bx, lam, reverse, other=None):
    s = xc.shape[1]
    tt = _blk(s, TT_RNN)
    nt = s // tt

    def body(x_ref, wa_ref, wx_ref, ba_ref, bx_ref, lam_ref, *rest):
        if other is None:
            hs_ref, carry = rest
        else:
            oth_ref, hs_ref, sum_ref, carry = rest

        @pl.when(pl.program_id(1) == 0)
        def _():
            carry[...] = jnp.zeros(carry.shape, F32)

        xv = _perm_load(x_ref, tt)
        _, _, i, _, _, a, mult, _ = _lru_gates(xv, wa_ref, wx_ref, ba_ref, bx_ref, lam_ref)
        u = mult * (i * xv)

        def emit(g, j, rows):
            for l in range(SLABS):
                hs_ref.at[l][pl.ds(GRP * g + j, 8, stride=SEG), :] = rows[:, 128 * l:128 * (l + 1)]

        carry[...] = _perm_scan(a, u, carry[...], reverse, emit)
        if other is not None:
            for l in range(SLABS):
                sum_ref[:, 128 * l:128 * (l + 1)] = (hs_ref[l] + oth_ref[l]).astype(BF16)

    _, blk, slabs, wsp, vec = _lru_specs(nt, tt, reverse)
    extra = [] if other is None else [other]
    return pl.pallas_call(
        body, name="lru_fwd_rev" if reverse else "lru_fwd", grid=(N_CG, nt),
        out_shape=[jax.ShapeDtypeStruct(xc.shape, F32)] + [jax.ShapeDtypeStruct((s, D_RNN), BF16)] * len(extra),
        in_specs=[slabs, wsp, wsp, vec, vec, vec] + [slabs] * len(extra), out_specs=[slabs] + [blk] * len(extra),
        scratch_shapes=[pltpu.VMEM((8, CG), F32)],
        compiler_params=_params(),
    )(xc, wa, wx, ba, bx, lam, *extra)


def lru_bwd(xc, dhs, hs, wa, wx, ba, bx, lam, reverse, rider=None):
    s = xc.shape[1]
    tt = _blk(s, TT_RNN)
    nt = s // tt
    ng = tt // GRP
    back = not reverse

    def rows_of(v, g, i):
        return v[GRP * g + 8 * i:GRP * g + 8 * i + 8]

    def neighbour(v, past, edge, sub):
        out = []
        for g in range(ng):
            for i in range(SEG):
                if past and i > 0:
                    r = rows_of(v, g, i - 1)
                elif past:
                    e = edge if g == 0 else rows_of(v, g - 1, SEG - 1)[7:8]
                    r = jnp.where(sub == 0, e, pltpu.roll(rows_of(v, g, SEG - 1), 1, 0))
                elif i < SEG - 1:
                    r = rows_of(v, g, i + 1)
                else:
                    e = edge if g == ng - 1 else rows_of(v, g + 1, 0)[0:1]
                    r = jnp.where(sub == 7, e, pltpu.roll(rows_of(v, g, 0), 7, 0))
                out.append(r)
        return jnp.concatenate(out, axis=0)

    def body(x_ref, dh_ref, hs_ref, nb_ref, wa_ref, wx_ref, ba_ref, bx_ref, lam_ref,
             dx_ref, dwa_ref, dwx_ref, dba_ref, dbx_ref, dlam_ref, carry):
        t = pl.program_id(1)

        @pl.when(t == 0)
        def _():
            carry[...] = jnp.zeros(carry.shape, F32)
            dwa_ref[...] = jnp.zeros(dwa_ref.shape, F32)
            dwx_ref[...] = jnp.zeros(dwx_ref.shape, F32)
            dba_ref[...] = jnp.zeros(dba_ref.shape, F32)
            dbx_ref[...] = jnp.zeros(dbx_ref.shape, F32)
            dlam_ref[...] = jnp.zeros(dlam_ref.shape, F32)

        xv = _perm_load(x_ref, tt)
        xb, r, i, sp, sg, a, mult, rmult = _lru_gates(xv, wa_ref, wx_ref, ba_ref, bx_ref, lam_ref)
        sub = lax.broadcasted_iota(jnp.int32, (8, CG), 0)
        hsv = _perm_load(hs_ref, tt)
        nbv = _slab_rows(nb_ref)
        inner = t < nt - 1
        h_edge = jnp.where(inner, nbv[0:1, :] if reverse else nbv[7:8, :], 0.0)
        hprev = neighbour(hsv, not reverse, h_edge, sub)
        a_next = neighbour(a, reverse, carry[8:9, :], sub)
        dhv = _perm_load(dh_ref, tt)
        gl = [None] * (ng * SEG)

        def emit(gi, j, rows):
            gl[gi * SEG + j] = rows

        carry[0:8, :] = _perm_scan(a_next, dhv, carry[0:8, :], back, emit)
        g = jnp.concatenate(gl, axis=0)
        carry[8:9, :] = a[0:1, :] if back else a[tt - 1:tt, :]

        da = g * hprev
        dmult = g * (i * xv)
        di = g * mult * xv
        dla = da * a - dmult * (a * a) * rmult
        dpa = (dla * (-LRU_C * sp)) * r * (1.0 - r)
        dpx = di * i * (1.0 - i)
        dlam_ref[...] += _colsum(dla * (LRU_C * r * sg))
        dba_ref[...] += _colsum(dpa)
        dbx_ref[...] += _colsum(dpx)
        dpab, dpxb = dpa.astype(BF16), dpx.astype(BF16)
        dxv = g * mult * i + _nt(dpab, wa_ref[...]) + _nt(dpxb, wx_ref[...])
        for gi in range(ng):
            for j in range(SEG):
                for l in range(SLABS):
                    dx_ref.at[l][pl.ds(GRP * gi + j, 8, stride=SEG), :] = rows_of(dxv, gi, j)[:, 128 * l:128 * (l + 1)]
        dwa_ref[...] += _tn(xb, dpab)
        dwx_ref[...] += _tn(xb, dpxb)

    tmap, blk, slabs, wsp, vec = _lru_specs(nt, tt, back)
    per8 = tt // 8
    if reverse:
        nb = pl.BlockSpec((SLABS, 8, 128), lambda g, t: (g, jnp.minimum((tmap(t) + 1) * per8, s // 8 - 1), 0))
    else:
        nb = pl.BlockSpec((SLABS, 8, 128), lambda g, t: (g, jnp.maximum(tmap(t) * per8 - 1, 0), 0))
    return _call(
        body, name="lru_bwd_rev" if reverse else "lru_bwd", grid=(N_CG, nt),
        out_shape=[jax.ShapeDtypeStruct(xc.shape, F32), jax.ShapeDtypeStruct((N_CG, CG, CG), F32),
                   jax.ShapeDtypeStruct((N_CG, CG, CG), F32)] + [jax.ShapeDtypeStruct((1, D_RNN), F32)] * 3,
        in_specs=[slabs, slabs, slabs, nb, wsp, wsp, vec, vec, vec],
        out_specs=[slabs, wsp, wsp, vec, vec, vec],
        scratch_shapes=[pltpu.VMEM((16, CG), F32)],
        args=(xc, dhs, hs, hs, wa, wx, ba, bx, lam), rider=rider)


def _ln_part_bwd(dxo, x, y, gate, g, sums_ref, loss_head=None):
    xhat, rstd = _ln_stats(ALPHA * x + (1.0 + gate) * y)
    if loss_head is not None:
        err = xhat * g + loss_head[0] - loss_head[1]
        dxo = err * (1.0 / D_MODEL)
        sums_ref[5:6, :] += _colsum(err * err)
    dz = _ln_bwd(dxo, xhat, rstd, g)
    sums_ref[2:3, :] += _colsum(dz * y)
    sums_ref[3:4, :] += _colsum(dxo * xhat)
    sums_ref[4:5, :] += _colsum(dxo)
    return dz


def mlp_bwd(dxo, x, y, ra, mods, k, w1_t, w2, lng, lnb=None, rider=None):
    s = x.shape[0]
    tm = _blk(s, TM_MLP)
    head = lnb is not None

    def body(d_ref, x_ref, y_ref, ra_ref, mod_ref, w1_ref, w2_ref, g_ref, *rest):
        b_ref = rest[0] if head else None
        dx_ref, da_ref, h_ref, dy_ref, sums_ref = rest[1:] if head else rest

        @pl.when(pl.program_id(0) == 0)
        def _():
            sums_ref[...] = jnp.zeros(sums_ref.shape, F32)

        xv = x_ref[...]
        shift, scale, gate = _mod(mod_ref, k)
        if head:
            dz = _ln_part_bwd(None, xv, y_ref[...], gate, g_ref[...], sums_ref, (b_ref[...], d_ref[...]))
        else:
            dz = _ln_part_bwd(d_ref[...], xv, y_ref[...], gate, g_ref[...], sums_ref)
        dyb = (dz * (1.0 + gate)).astype(BF16)
        dy_ref[...] = dyb
        h = (xv * (1.0 + scale) + shift).astype(BF16)
        h_ref[...] = h
        dh = jnp.zeros((tm, D_MODEL), F32)
        for c in range(D_FF // FF_CHUNK):
            rows = slice(FF_CHUNK * c, FF_CHUNK * (c + 1))
            da = (_nt(dyb, w2_ref[rows, :]) * (2.0 * ra_ref[:, rows].astype(F32))).astype(BF16)
            da_ref[:, rows] = da
            dh = dh + _nn(da, w1_ref[rows, :])
        dx_ref[...] = ALPHA * dz + dh * (1.0 + scale)
        sums_ref[0:1, :] += _colsum(dh)
        sums_ref[1:2, :] += _colsum(dh * xv)

    return _call(
        body, name="mlp_bwd", grid=(s // tm,),
        out_shape=[jax.ShapeDtypeStruct((s, D_MODEL), F32), jax.ShapeDtypeStruct((s, D_FF), BF16),
                   jax.ShapeDtypeStruct((s, D_MODEL), BF16),
                   jax.ShapeDtypeStruct((s, D_MODEL), BF16), jax.ShapeDtypeStruct((8, D_MODEL), F32)],
        in_specs=[_row(tm, D_MODEL)] * 3 + [_row(tm, D_FF), _res(mods.shape), _res(w1_t.shape), _res(w2.shape),
                                             _res(lng.shape)] + ([_res(lnb.shape)] if head else []),
        out_specs=[_row(tm, D_MODEL), _row(tm, D_FF), _row(tm, D_MODEL), _row(tm, D_MODEL), _res((8, D_MODEL))],
        args=(dxo, x, y, ra, mods, w1_t, w2, lng) + ((lnb,) if head else ()), rider=rider)


def post_bwd(dxo, x, y, mods, k, w, lng, gate_act=None, rider=None):
    s = x.shape[0]
    tm = _blk(s, TM_MM)
    kdim = w.shape[0]
    rnn = gate_act is not None

    def body(*refs):
        if rnn:
            (d_ref, x_ref, y_ref, mod_ref, w_ref, g_ref, gt_ref, hs_ref,
             dres_ref, dy_ref, sums_ref, dhs_ref, dgt_ref) = refs
        else:
            d_ref, x_ref, y_ref, mod_ref, w_ref, g_ref, dres_ref, dy_ref, sums_ref, dyp_ref = refs

        @pl.when(pl.program_id(0) == 0)
        def _():
            sums_ref[...] = jnp.zeros(sums_ref.shape, F32)

        _, _, gate = _mod(mod_ref, k)
        dz = _ln_part_bwd(d_ref[...], x_ref[...], y_ref[...], gate, g_ref[...], sums_ref)
        dres_ref[...] = ALPHA * dz
        dyb = (dz * (1.0 + gate)).astype(BF16)
        dy_ref[...] = dyb
        dyp = _nt(dyb, w_ref[...])
        if rnn:
            act, dact = _gelu_parts(gt_ref[...].astype(F32))
            dhs = dyp * act
            for l in range(kdim // 128):
                dhs_ref[l] = dhs[:, 128 * l:128 * (l + 1)]
            dgt_ref[...] = (dyp * hs_ref[...].astype(F32) * dact).astype(BF16)
        else:
            dyp_ref[...] = dyp.astype(BF16)

    ins = [dxo, x, y, mods, w, lng] + (list(gate_act) if rnn else [])
    in_specs = [_row(tm, D_MODEL)] * 3 + [_res(mods.shape), _res(w.shape), _res(lng.shape)]
    out_shape = [jax.ShapeDtypeStruct((s, D_MODEL), F32), jax.ShapeDtypeStruct((s, D_MODEL), BF16),
                 jax.ShapeDtypeStruct((8, D_MODEL), F32)]
    out_specs = [_row(tm, D_MODEL), _row(tm, D_MODEL), _res((8, D_MODEL))]
    if rnn:
        in_specs += [_row(tm, kdim)] * 2
        out_shape += [jax.ShapeDtypeStruct((kdim // 128, s, 128), F32), jax.ShapeDtypeStruct((s, kdim), BF16)]
        out_specs += [pl.BlockSpec((kdim // 128, tm, 128), lambda i: (0, i, 0)), _row(tm, kdim)]
    else:
        out_shape.append(jax.ShapeDtypeStruct((s, kdim), BF16))
        out_specs.append(_row(tm, kdim))
    return _call(
        body, name="rnn_post_bwd" if rnn else "attn_post_bwd", grid=(s // tm,),
        out_shape=out_shape, in_specs=in_specs, out_specs=out_specs, args=ins, rider=rider)


def attn_bwd(q, kk, v, do, sinks, rider=None):
    s = q.shape[0]
    nblk = s // QBLK
    scale = HEAD ** -0.5

    def body(sink_ref, q_ref, do_ref, kp, ko, kn, vp, vo, vn, dq_ref, dk_ref, dv_ref, ds_ref):
        n = pl.program_id(0)

        @pl.when(n == 0)
        def _():
            ds_ref[...] = jnp.zeros(ds_ref.shape, F32)
            dk_ref[...] = jnp.zeros(dk_ref.shape, F32)
            dv_ref[...] = jnp.zeros(dv_ref.shape, F32)

        kall = jnp.concatenate([kp[...], ko[...], kn[...]], axis=0)
        vall = jnp.concatenate([vp[...], vo[...], vn[...]], axis=0)
        lane = lax.broadcasted_iota(jnp.int32, (1, 128), 1)
        dsink = jnp.zeros((1, 128), F32)
        for qb in range(QPAIR):
            nb = QPAIR * n + qb
            valid = _attn_mask(nb, s)
            keys = slice(QBLK * qb, QBLK * (qb + 3))
            for kv in range(N_KV):
                cols = slice(HEAD * kv, HEAD * (kv + 1))
                qg, dog = _stack_heads(q_ref, qb, kv), _stack_heads(do_ref, qb, kv)
                kh, vh = kall[keys, cols], vall[keys, cols]
                probs, psink = _attn_probs(qg, kh, valid, _stack_sinks(sink_ref, kv))
                dprobs = _nt(dog, vh)
                dvp = _tn(probs.astype(BF16), dog)
                rowdot = jnp.sum(probs * dprobs, axis=-1, keepdims=True)
                dsb = (probs * (dprobs - rowdot) * scale).astype(BF16)
                dqg = _nn(dsb, kh)
                dkp = _tn(dsb, qg)
                for p in range(3):
                    blk = jnp.clip(nb - 1 + p, 0, nblk - 1)
                    rows = pl.ds(pl.multiple_of(blk * QBLK, QBLK), QBLK)
                    dk_ref[rows, cols] += dkp[QBLK * p:QBLK * (p + 1), :]
                    dv_ref[rows, cols] += dvp[QBLK * p:QBLK * (p + 1), :]
                dsk = -psink * rowdot
                for j in range(GROUP):
                    hq = GROUP * kv + j
                    dq_ref[QBLK * qb:QBLK * (qb + 1), HEAD * hq:HEAD * (hq + 1)] = dqg[QBLK * j:QBLK * (j + 1), :]
                    dsink = dsink + jnp.where(lane == hq, _colsum(dsk[QBLK * j:QBLK * (j + 1), :]), 0.0)
        ds_ref[...] += dsink

    qspec = pl.BlockSpec((QPAIR * QBLK, N_Q * HEAD), lambda n: (n, 0))
    return _call(
        body, name="attn_bwd", grid=(nblk // QPAIR,),
        out_shape=[jax.ShapeDtypeStruct((s, N_Q * HEAD), F32),
                   jax.ShapeDtypeStruct((s, N_KV * HEAD), F32), jax.ShapeDtypeStruct((s, N_KV * HEAD), F32),
                   jax.ShapeDtypeStruct((1, 128), F32)],
        in_specs=[pl.BlockSpec(memory_space=pltpu.SMEM), qspec, qspec] + _kv_specs(nblk) + _kv_specs(nblk),
        out_specs=[qspec, _res((s, N_KV * HEAD)), _res((s, N_KV * HEAD)), pl.BlockSpec((1, 128), lambda n: (0, 0))],
        args=(sinks, q, do, kk, kk, kk, v, v, v), rider=rider)


def _in_bwd_tail(dzb, w_ref, x_ref, mod_ref, k, dres_ref, dx_ref, h_ref, sums_ref):
    xv = x_ref[...]
    shift, scale, _ = _mod(mod_ref, k)
    h_ref[...] = (xv * (1.0 + scale) + shift).astype(BF16)
    dh = _nn(dzb, w_ref[...])
    dx_ref[...] = dres_ref[...] + dh * (1.0 + scale)
    sums_ref[0:1, :] += _colsum(dh)
    sums_ref[1:2, :] += _colsum(dh * xv)


def attn_in_bwd(dq, dk, dv, rope, x, mods, k, win_t, dres):
    s = x.shape[0]
    tm = _blk(s, TM_MM)

    def body(dq_ref, dk_ref, dv_ref, c_ref, s1_ref, s2_ref, x_ref, mod_ref, w_ref, dres_ref,
             dx_ref, dz_ref, h_ref, sums_ref):
        @pl.when(pl.program_id(0) == 0)
        def _():
            sums_ref[...] = jnp.zeros(sums_ref.shape, F32)

        cos, s1, s2 = c_ref[...], s1_ref[...], s2_ref[...]
        for hh in range(N_Q + N_KV):
            src = dq_ref[:, HEAD * hh:HEAD * (hh + 1)] if hh < N_Q else dk_ref[:, HEAD * (hh - N_Q):HEAD * (hh - N_Q + 1)]
            dz_ref[:, HEAD * hh:HEAD * (hh + 1)] = _rope_bwd(src, cos, s1, s2).astype(BF16)
        dz_ref[:, HEAD * (N_Q + N_KV):] = dv_ref[...].astype(BF16)
        _in_bwd_tail(dz_ref[...], w_ref, x_ref, mod_ref, k, dres_ref, dx_ref, h_ref, sums_ref)

    return pl.pallas_call(
        body, name="attn_in_bwd", grid=(s // tm,),
        out_shape=[jax.ShapeDtypeStruct((s, D_MODEL), F32), jax.ShapeDtypeStruct((s, D_QKV), BF16),
                   jax.ShapeDtypeStruct((s, D_MODEL), BF16), jax.ShapeDtypeStruct((8, D_MODEL), F32)],
        in_specs=[_row(tm, N_Q * HEAD), _row(tm, N_KV * HEAD), _row(tm, N_KV * HEAD),
                  _row(tm, HEAD), _row(tm, HEAD), _row(tm, HEAD), _row(tm, D_MODEL),
                  _res(mods.shape), _res(win_t.shape), _row(tm, D_MODEL)],
        out_specs=[_row(tm, D_MODEL), _row(tm, D_QKV), _row(tm, D_MODEL), _res((8, D_MODEL))],
        compiler_params=_params(),
    )(dq, dk, dv, *rope, x, mods, win_t, dres)


def _shift_blk(v, k, before, after, row):
    n = v.shape[0]
    r = pltpu.roll(v, k % n, 0)
    for j in range(abs(k)):
        if k > 0:
            r = jnp.where(row == j, before[8 - k + j:8 - k + j + 1, :], r)
        else:
            r = jnp.where(row == n + k + j, after[j:j + 1, :], r)
    return r


def rnn_in_bwd(dxc_f, dxc_b, xr, cw, dgt, x, mods, k, win_t, dres):
    s = x.shape[0]
    tm = _blk(s, TM_MM)
    n = s // tm

    def body(f_ref, fp_ref, fn_ref, b_ref, bp_ref, bn_ref, xr_ref, xp_ref, xn_ref, cw_ref, dgt_ref,
             x_ref, mod_ref, w_ref, dres_ref, dx_ref, dz_ref, h_ref, sums_ref, dcw_ref, dcb_ref):
        i = pl.program_id(0)

        @pl.when(i == 0)
        def _():
            sums_ref[...] = jnp.zeros(sums_ref.shape, F32)
            dcw_ref[...] = jnp.zeros(dcw_ref.shape, F32)
            dcb_ref[...] = jnp.zeros(dcb_ref.shape, F32)

        d = _slab_rows(f_ref) + _slab_rows(b_ref)
        xv = xr_ref[...]
        first, last = i == 0, i == n - 1
        d_before = jnp.where(first, 0.0, _slab_rows(fp_ref) + _slab_rows(bp_ref))
        d_after = jnp.where(last, 0.0, _slab_rows(fn_ref) + _slab_rows(bn_ref))
        x_before = jnp.where(first, 0.0, xp_ref[...])
        x_after = jnp.where(last, 0.0, xn_ref[...])
        row = lax.broadcasted_iota(jnp.int32, d.shape, 0)
        dxr = (cw_ref[0:1, :] * _shift_blk(d, -2, d_before, d_after, row)
               + cw_ref[1:2, :] * _shift_blk(d, -1, d_before, d_after, row)
               + cw_ref[2:3, :] * d + cw_ref[3:4, :] * _shift_blk(d, 1, d_before, d_after, row))
        dcw_ref[0:1, :] += _colsum(d * _shift_blk(xv, 2, x_before, x_after, row))
        dcw_ref[1:2, :] += _colsum(d * _shift_blk(xv, 1, x_before, x_after, row))
        dcw_ref[2:3, :] += _colsum(d * xv)
        dcw_ref[3:4, :] += _colsum(d * _shift_blk(xv, -1, x_before, x_after, row))
        dcb_ref[...] += _colsum(d)
        dz_ref[:, 0:D_RNN] = dxr.astype(BF16)
        dz_ref[:, D_RNN:2 * D_RNN] = dgt_ref[...]
        _in_bwd_tail(dz_ref[...], w_ref, x_ref, mod_ref, k, dres_ref, dx_ref, h_ref, sums_ref)

    per8 = tm // 8
    ns = D_RNN // 128
    blk = _row(tm, D_RNN)
    before = pl.BlockSpec((8, D_RNN), lambda i: (jnp.maximum(i * per8 - 1, 0), 0))
    after = pl.BlockSpec((8, D_RNN), lambda i: (jnp.minimum((i + 1) * per8, s // 8 - 1), 0))
    sblk = pl.BlockSpec((ns, tm, 128), lambda i: (0, i, 0))
    sbefore = pl.BlockSpec((ns, 8, 128), lambda i: (0, jnp.maximum(i * per8 - 1, 0), 0))
    safter = pl.BlockSpec((ns, 8, 128), lambda i: (0, jnp.minimum((i + 1) * per8, s // 8 - 1), 0))
    return pl.pallas_call(
        body, name="rnn_in_bwd", grid=(n,),
        out_shape=[jax.ShapeDtypeStruct((s, D_MODEL), F32), jax.ShapeDtypeStruct((s, 2 * D_RNN), BF16),
                   jax.ShapeDtypeStruct((s, D_MODEL), BF16), jax.ShapeDtypeStruct((8, D_MODEL), F32),
                   jax.ShapeDtypeStruct((4, D_RNN), F32), jax.ShapeDtypeStruct((1, D_RNN), F32)],
        in_specs=[sblk, sbefore, safter] * 2 + [blk, before, after] + [
            _res(cw.shape), blk, _row(tm, D_MODEL), _res(mods.shape), _res(win_t.shape), _row(tm, D_MODEL)],
        out_specs=[_row(tm, D_MODEL), _row(tm, 2 * D_RNN), _row(tm, D_MODEL), _res((8, D_MODEL)),
                   _res((4, D_RNN)), _res((1, D_RNN))],
        compiler_params=_params(),
    )(dxc_f, dxc_f, dxc_f, dxc_b, dxc_b, dxc_b, xr, xr, xr, cw, dgt, x, mods, win_t, dres)


def wgrad(a, b, name, rider=None):
    s, m = a.shape
    n = b.shape[1]
    tm = next(t for t in (1024, 768, 512, 384, 256, 128) if m % t == 0)
    tk = _blk(s, TK_WG)
    nk = s // tk

    def body(a_ref, b_ref, o_ref, acc):
        kk = pl.program_id(1)

        @pl.when(kk == 0)
        def _():
            acc[...] = jnp.zeros(acc.shape, F32)

        acc[...] += _tn(a_ref[...], b_ref[...])

        @pl.when(kk == nk - 1)
        def _():
            o_ref[...] = acc[...].astype(BF16)

    out, *rode = _call(
        body, name=name, grid=(m // tm, nk),
        out_shape=[jax.ShapeDtypeStruct((m, n), BF16)],
        in_specs=[pl.BlockSpec((tk, tm), lambda i, kk: (kk, i)), pl.BlockSpec((tk, n), lambda i, kk: (kk, 0))],
        out_specs=[pl.BlockSpec((tm, n), lambda i, kk: (i, 0))],
        scratch_shapes=[pltpu.VMEM((tm, n), F32)],
        args=(a, b), rider=rider)
    out = out.reshape(N_DEV, m // N_DEV, n)
    return (out, *rode) if rider is not None else out


def part_sum(parts, name):
    _, r, c = parts.shape
    tr = next(t for t in (256, 192, 128, 64, 32, 16, 8) if r % t == 0)

    def body(p_ref, o_ref):
        acc = p_ref[0].astype(F32)
        for j in range(1, N_DEV):
            acc = acc + p_ref[j].astype(F32)
        o_ref[...] = acc

    return pl.pallas_call(
        body, name=name, grid=(r // tr,),
        out_shape=jax.ShapeDtypeStruct((r, c), F32),
        in_specs=[pl.BlockSpec((N_DEV, tr, c), lambda i: (0, i, 0))],
        out_specs=pl.BlockSpec((tr, c), lambda i: (i, 0)),
        compiler_params=_params(),
    )(parts)


def adamw(w, g, m, v, name):
    shape = w.shape
    c = shape[-1]
    r = w.size // c
    w2, g2, m2, v2 = (t.reshape(r, c) for t in (w, g, m, v))
    tr = r if r * c <= 512 * 1024 else next(t for t in (512, 256, 128, 64, 32, 16, 8) if r % t == 0)

    def body(w_ref, g_ref, m_ref, v_ref, d_ref, nm_ref, nv_ref):
        gv = g_ref[...]
        nm = B1 * m_ref[...] + (1.0 - B1) * gv
        nv = B2 * v_ref[...] + (1.0 - B2) * (gv * gv)
        nm_ref[...] = nm
        nv_ref[...] = nv
        m_hat = nm / (1.0 - B1 ** STEP)
        v_hat = nv / (1.0 - B2 ** STEP)
        d_ref[...] = -LR * (m_hat / (jnp.sqrt(v_hat) + ADAM_EPS) + WD * w_ref[...])

    spec = pl.BlockSpec((tr, c), lambda i: (i, 0))
    outs = pl.pallas_call(
        body, name=name, grid=(r // tr,),
        out_shape=[jax.ShapeDtypeStruct((r, c), F32)] * 3,
        in_specs=[spec] * 4, out_specs=[spec] * 3,
        compiler_params=_params(),
    )(w2, g2, m2, v2)
    return tuple(o.reshape(shape) for o in outs)


def _rope_tables(s):
    half = ROT // 2
    inv_freq = THETA ** (-jnp.arange(0, ROT, 2, dtype=F32) / ROT)
    per_row = 128 // half
    pos = (per_row * jnp.arange(s // per_row)[:, None] + jnp.arange(128)[None, :] // half).astype(F32)
    ang = pos * jnp.tile(inv_freq, per_row)[None, :]
    cos, sin = lax.optimization_barrier((jnp.cos(ang), jnp.sin(ang)))
    cos, sin = cos.reshape(s, half), sin.reshape(s, half)
    zeros = jnp.zeros((s, HEAD - ROT), F32)
    c = jnp.concatenate([cos, cos, jnp.ones((s, HEAD - ROT), F32)], axis=1)
    s1 = jnp.concatenate([jnp.zeros((s, half), F32), sin, zeros], axis=1)
    s2 = jnp.concatenate([-sin, jnp.zeros((s, half), F32), zeros], axis=1)
    return c, s1, s2


def _blockdiag(w):
    w4 = w.reshape(N_CG, 4, RB_W, RB_W)
    eye = jnp.eye(4, dtype=w.dtype)
    return jnp.einsum("gipq,ij->gipjq", w4, eye).reshape(N_CG, CG, CG)


def _diag_blocks(w):
    w5 = w.reshape(N_CG, 4, RB_W, 4, RB_W)
    eye = jnp.eye(4, dtype=w.dtype)
    return jnp.einsum("gipjq,ij->gipq", w5, eye).reshape(N_RB, RB_W, RB_W)


def _cols(full, per):
    lead = full.shape[:-1]
    t = full.reshape(lead + (N_DEV, per))
    return jnp.moveaxis(t, -2, 0).reshape(N_DEV, -1)


def kernel(x, c, ada_w, ada_b, ln_g, ln_b, attn_w_in, attn_w_out, attn_sinks, rnn_w_in, rnn_conv_w, rnn_conv_b, rnn_w_a, rnn_b_a, rnn_w_x, rnn_b_x, rnn_lam, rnn_w_out, mlp_w1, mlp_w2, loss_target, m_ada_w, m_ada_b, m_ln_g, m_ln_b, m_attn_w_in, m_attn_w_out, m_attn_sinks, m_rnn_w_in, m_rnn_conv_w, m_rnn_conv_b, m_rnn_w_a, m_rnn_b_a, m_rnn_w_x, m_rnn_b_x, m_rnn_lam, m_rnn_w_out, m_mlp_w1, m_mlp_w2, v_ada_w, v_ada_b, v_ln_g, v_ln_b, v_attn_w_in, v_attn_w_out, v_attn_sinks, v_rnn_w_in, v_rnn_conv_w, v_rnn_conv_b, v_rnn_w_a, v_rnn_b_a, v_rnn_w_x, v_rnn_b_x, v_rnn_lam, v_rnn_w_out, v_mlp_w1, v_mlp_w2):
    s = x.shape[1]
    x0 = x.reshape(s, D_MODEL)
    target = loss_target.reshape(s, D_MODEL)
    weights = dict(ada_w=ada_w, ada_b=ada_b, ln_g=ln_g, ln_b=ln_b, attn_w_in=attn_w_in, attn_w_out=attn_w_out,
                   attn_sinks=attn_sinks, rnn_w_in=rnn_w_in, rnn_conv_w=rnn_conv_w, rnn_conv_b=rnn_conv_b,
                   rnn_w_a=rnn_w_a, rnn_b_a=rnn_b_a, rnn_w_x=rnn_w_x, rnn_b_x=rnn_b_x, rnn_lam=rnn_lam,
                   rnn_w_out=rnn_w_out, mlp_w1=mlp_w1, mlp_w2=mlp_w2)
    moments_m = dict(ada_w=m_ada_w, ada_b=m_ada_b, ln_g=m_ln_g, ln_b=m_ln_b, attn_w_in=m_attn_w_in,
                     attn_w_out=m_attn_w_out, attn_sinks=m_attn_sinks, rnn_w_in=m_rnn_w_in,
                     rnn_conv_w=m_rnn_conv_w, rnn_conv_b=m_rnn_conv_b, rnn_w_a=m_rnn_w_a, rnn_b_a=m_rnn_b_a,
                     rnn_w_x=m_rnn_w_x, rnn_b_x=m_rnn_b_x, rnn_lam=m_rnn_lam, rnn_w_out=m_rnn_w_out,
                     mlp_w1=m_mlp_w1, mlp_w2=m_mlp_w2)
    moments_v = dict(ada_w=v_ada_w, ada_b=v_ada_b, ln_g=v_ln_g, ln_b=v_ln_b, attn_w_in=v_attn_w_in,
                     attn_w_out=v_attn_w_out, attn_sinks=v_attn_sinks, rnn_w_in=v_rnn_w_in,
                     rnn_conv_w=v_rnn_conv_w, rnn_conv_b=v_rnn_conv_b, rnn_w_a=v_rnn_w_a, rnn_b_a=v_rnn_b_a,
                     rnn_w_x=v_rnn_w_x, rnn_b_x=v_rnn_b_x, rnn_lam=v_rnn_lam, rnn_w_out=v_rnn_w_out,
                     mlp_w1=v_mlp_w1, mlp_w2=v_mlp_w2)
    names = list(weights)

    def t16(w):
        return w.T.astype(BF16)

    big = [t16(attn_w_in[0]), attn_w_out[0].astype(BF16), t16(rnn_w_in[0]), rnn_w_out[0].astype(BF16),
           t16(mlp_w1[0]), mlp_w2[0].astype(BF16), t16(mlp_w1[1]), mlp_w2[1].astype(BF16)]
    small_local = jnp.concatenate([
        ln_g.reshape(-1), ln_b.reshape(-1), rnn_conv_w.reshape(-1), rnn_conv_b.reshape(-1),
        rnn_b_a.reshape(-1), rnn_b_x.reshape(-1), rnn_lam.reshape(-1)])
    small_local = jnp.pad(small_local, (0, 4096 - small_local.shape[0])).reshape(32, 128)
    flat = lambda g: g.reshape(N_DEV * g.shape[1], D_MODEL)
    c_all, modr, win_t, sm = ada_modulation(jnp.broadcast_to(c, (8, D_MODEL)), ada_w.reshape(4, D_MODEL, CG),
                                            ada_b.reshape(4, 1, CG), _Gather([big[0], small_local]))
    win_t = flat(win_t)
    sm = sm.reshape(N_DEV, 4096)

    def full_vec(off, rows, per):
        piece = sm[:, off:off + rows * per].reshape(N_DEV, rows, per)
        return jnp.moveaxis(piece, 0, 1).reshape(rows, N_DEV * per)

    lng_f, lnb_f = full_vec(0, 4, 128), full_vec(512, 4, 128)
    cw_f, cb_f = full_vec(1024, 4, 192), full_vec(1792, 1, 192)
    ba_f, bx_f, lam_f = full_vec(1984, 2, 192), full_vec(2368, 2, 192), full_vec(2752, 2, 192)
    wa_bd = [_blockdiag(rnn_w_a[0, d]).astype(BF16) for d in range(2)]
    wx_bd = [_blockdiag(rnn_w_x[0, d]).astype(BF16) for d in range(2)]

    mods = modr.reshape(N_DEV, 4, 8, CG)[:, :, 0, :]
    mods = jnp.moveaxis(mods, 0, 1).reshape(4, 3, D_MODEL).reshape(12, D_MODEL)
    rope = _rope_tables(s)
    ln = lambda k: (lng_f[k:k + 1], lnb_f[k:k + 1])

    q, kk, v, wout = attn_in_fwd(x0, mods, 0, win_t, rope, rider=_Gather([big[1]]))
    wout = flat(wout)
    o, *got = attn_fwd(q, kk, v, attn_sinks, rider=_Gather([big[4], big[5]]))
    w1t_0, w2_0 = (flat(g) for g in got)
    x1, y0, rout = post_fwd(o, wout, x0, mods, 0, *ln(0), rider=_Gather([big[3]]))
    rout = flat(rout)
    x2, y1, ra0, r0, *got = mlp_fwd(x1, mods, 1, w1t_0, w2_0, *ln(1), rider=_Gather([big[2], big[6], big[7]]))
    rin_t, w1t_1, w2_1 = (flat(g) for g in got)
    xr, gt = rnn_in_fwd(x2, mods, 2, rin_t)
    xc = conv_fwd(xr, cw_f, cb_f)
    hf, = lru_fwd(xc, wa_bd[0], wx_bd[0], ba_f[0:1], bx_f[0:1], lam_f[0:1], False)
    hb, hsum = lru_fwd(xc, wa_bd[1], wx_bd[1], ba_f[1:2], bx_f[1:2], lam_f[1:2], True, other=hf)
    x3, y2, ypre = post_fwd(None, rout, x2, mods, 2, *ln(2), gate_act=(gt, hsum))
    y3, ra1, r1 = mlp_fwd(x3, mods, 3, w1t_1, w2_1, *ln(3), last=True)

    dx3, da1, h3, dy3, sums3 = mlp_bwd(target, x3, y3, ra1, mods, 3, w1t_1, w2_1, lng_f[3:4], lnb=lnb_f[3:4])
    g_w1t_1 = wgrad(da1, h3, "wgrad_w1_1")
    g_w2_1 = wgrad(r1, dy3, "wgrad_w2_1")
    dres2, dy2, sums2a, dhs, dgt, p_w1t_1 = post_bwd(dx3, x2, y2, mods, 2, rout, lng_f[2:3], gate_act=(gt, hsum),
                                                     rider=_AllToAll([g_w1t_1]))
    g_rout = wgrad(ypre, dy2, "wgrad_rnn_out")
    dxc_f, dwa_f, dwx_f, dba_f, dbx_f, dlam_f, p_w2_1, p_rout = lru_bwd(
        xc, dhs, hf, wa_bd[0], wx_bd[0], ba_f[0:1], bx_f[0:1], lam_f[0:1], False, rider=_AllToAll([g_w2_1, g_rout]))
    dxc_b, dwa_b, dwx_b, dba_b, dbx_b, dlam_b = lru_bwd(xc, dhs, hb, wa_bd[1], wx_bd[1], ba_f[1:2], bx_f[1:2],
                                                        lam_f[1:2], True)
    dx2, dzz, h2, sums2b, dcw, dcb = rnn_in_bwd(dxc_f, dxc_b, xr, cw_f, dgt, x2, mods, 2, rin_t, dres2)
    g_rin_t = wgrad(dzz, h2, "wgrad_rnn_in")
    d_wa = jnp.stack([_diag_blocks(dwa_f), _diag_blocks(dwa_b)])
    d_wx = jnp.stack([_diag_blocks(dwx_f), _diag_blocks(dwx_b)])
    nflat = d_wa.size // N_DEV
    gates = jnp.concatenate([d_wa.reshape(N_DEV, nflat), d_wx.reshape(N_DEV, nflat)], axis=1)
    gates = gates.reshape(N_DEV, 2 * nflat // 128, 128)
    dx1, da0, h1, dy1, sums1, p_rin_t, p_gates = mlp_bwd(dx2, x1, y1, ra0, mods, 1, w1t_0, w2_0, lng_f[1:2],
                                                         rider=_AllToAll([g_rin_t, gates]))
    gates_sum = part_sum(p_gates, "part_sum_gates")
    g_w1t_0 = wgrad(da0, h1, "wgrad_w1_0")
    g_w2_0, p_w1t_0 = wgrad(r0, dy1, "wgrad_w2_0", rider=_AllToAll([g_w1t_0]))
    dres0, dy0, sums0a, do = post_bwd(dx1, x0, y0, mods, 0, wout, lng_f[0:1])
    g_wout = wgrad(o, dy0, "wgrad_attn_out")
    dq, dk, dv, dsink, wag, p_w2_0, p_wout = attn_bwd(
        q, kk, v, do, attn_sinks, rider=_Multi(_Gather([gates_sum]), _AllToAll([g_w2_0, g_wout])))
    dx0, dqkv, h0, sums0b = attn_in_bwd(dq, dk, dv, rope, x0, mods, 0, win_t, dres0)
    g_win_t = wgrad(dqkv, h0, "wgrad_attn_in")

    sums = [sums0a + sums0b, sums1, sums2a + sums2b, sums3]
    gmod = jnp.stack([t[0:3] for t in sums])
    gsend = jnp.moveaxis(gmod.reshape(4, N_DEV, CG), 1, 0)
    gsend = jnp.pad(gsend, ((0, 0), (0, 4), (0, 0)))
    c_t = c_all[:, 0, :].T
    sq_err = jnp.sum(sums3[5]).reshape(1, 1)
    tail = jnp.concatenate([
        _cols(dcw, 192), _cols(dcb, 192),
        _cols(jnp.concatenate([dba_f, dba_b]), 192), _cols(jnp.concatenate([dbx_f, dbx_b]), 192),
        _cols(jnp.concatenate([dlam_f, dlam_b]), 192),
        _cols(jnp.stack([t[3] for t in sums]), 128), _cols(jnp.stack([t[4] for t in sums]), 128),
        jnp.broadcast_to(dsink[:, 0:8], (N_DEV, 8)), jnp.broadcast_to(sq_err, (N_DEV, 1))], axis=1)
    tail = jnp.pad(tail, ((0, 0), (0, 32 * 128 - tail.shape[1]))).reshape(N_DEV, 32, 128)
    g_ada_w, g_ada_b, red, p_win_t = epilogue(gsend, c_t, tail, _AllToAll([g_win_t]))
    grads = {"ada_w": g_ada_w.reshape(ada_w.shape), "ada_b": g_ada_b[0:4].reshape(ada_b.shape)}

    big_parts = [p_win_t, p_wout, p_rin_t, p_rout, p_w1t_0, p_w2_0, p_w1t_1, p_w2_1]
    gsum = [part_sum(p, "part_sum_%d" % i) for i, p in enumerate(big_parts)]
    grads.update({
        "attn_w_in": gsum[0].T[None], "attn_w_out": gsum[1][None],
        "rnn_w_in": gsum[2].T[None], "rnn_w_out": gsum[3][None],
        "mlp_w1": jnp.stack([gsum[4].T, gsum[6].T]), "mlp_w2": jnp.stack([gsum[5], gsum[7]]),
    })
    wag = wag.reshape(N_DEV, 2 * nflat)
    grads["rnn_w_a"] = wag[:, :nflat].reshape(rnn_w_a.shape)
    grads["rnn_w_x"] = wag[:, nflat:].reshape(rnn_w_x.shape)
    tl = red.reshape(-1)
    loss = 0.5 * tl[3144] / D_MODEL
    grads["rnn_conv_w"] = tl[0:768].reshape(rnn_conv_w.shape)
    grads["rnn_conv_b"] = tl[768:960].reshape(rnn_conv_b.shape)
    grads["rnn_b_a"] = tl[960:1344].reshape(rnn_b_a.shape)
    grads["rnn_b_x"] = tl[1344:1728].reshape(rnn_b_x.shape)
    grads["rnn_lam"] = tl[1728:2112].reshape(rnn_lam.shape)
    grads["ln_g"] = tl[2112:2624].reshape(ln_g.shape)
    grads["ln_b"] = tl[2624:3136].reshape(ln_b.shape)
    grads["attn_sinks"] = tl[3136:3144].reshape(attn_sinks.shape)

    delta, new_m, new_v = {}, {}, {}
    for n in names:
        delta[n], new_m[n], new_v[n] = adamw(weights[n], grads[n], moments_m[n], moments_v[n], "adamw_" + n)
    return (loss, dx0.reshape(x.shape), *[grads[n] for n in names], *[delta[n] for n in names],
            *[new_m[n] for n in names], *[new_v[n] for n in names])
```
